```python
import math
import jax
import jax.numpy as jnp
from jax import lax
import numpy as np

D_MODEL = 1024
BATCH = 8
SEQ = 16384
DEPTH = 4

N_MIXERS = 2
N_A_LAYERS = (DEPTH + 1) // 2
N_B_LAYERS = DEPTH // 2
NORM_EPS = 1e-5
D_FF = 4 * D_MODEL
N_MOD = 6

S5_WIDTH = D_MODEL
S5_GROUP = 16
S5_GROUPS = S5_WIDTH // S5_GROUP
S5_STATE = 64
S5_CHUNK = 128
S5_DT_MIN = 0.001
S5_DT_MAX = 0.1

M2_D_INNER = 2 * D_MODEL
M2_HEADDIM = 64
M2_HEADS = M2_D_INNER // M2_HEADDIM
M2_GROUPS = 4
M2_HPG = M2_HEADS // M2_GROUPS
M2_STATE = 128
M2_CONV = 4
M2_CHUNK = 128
M2_CONV_DIM = M2_D_INNER + 2 * M2_GROUPS * M2_STATE
M2_IN_DIM = M2_D_INNER + M2_CONV_DIM + M2_HEADS
M2_DT_MIN = 0.001
M2_DT_MAX = 0.1

kernel_name = 'hybrid_s5_ssd_adaln_trunk'


def _rmsnorm(x, g):
    xf = x.astype(jnp.float32)
    y = xf * lax.rsqrt(jnp.mean(xf * xf, axis=-1, keepdims=True) + NORM_EPS)
    return (y * g.astype(jnp.float32)).astype(x.dtype)


def _modulate(h, shift, scale):
    return h * (1 + scale) + shift


def _sq_relu_mlp(h, w1, w2):
    a = jax.nn.relu(h @ w1)
    return (a * a) @ w2


def _s5_combine(e1, e2):
    a1, b1 = e1
    a2, b2 = e2
    return a2 * a1, a2 * b1 + b2


def _s5_mixer(h, w_in, lam_re, lam_im, log_dt, b_re, b_im, c_re, c_im, d, w_glu, b_glu):
    f32 = jnp.float32
    bsz, L, _ = h.shape
    nc = L // S5_CHUNK
    u = (h @ w_in).astype(f32)
    lam = lax.complex(lam_re.astype(f32), lam_im.astype(f32))
    dt = jnp.exp(log_dt.astype(f32))[:, None]
    lam_dt = lam * dt
    lam_bar = jnp.exp(lam_dt)
    b_bar = ((lam_bar - 1) / lam)[..., None] * lax.complex(b_re.astype(f32), b_im.astype(f32))
    c_mat = lax.complex(c_re.astype(f32), c_im.astype(f32))
    steps = jnp.arange(1, S5_CHUNK + 1, dtype=f32)[:, None, None]
    a_pow = jnp.exp(lam_dt[None] * steps)
    a_elems = jnp.broadcast_to(lam_bar, (bsz, S5_CHUNK, S5_GROUPS, S5_STATE))
    u_blocks = u.reshape(bsz, nc, S5_CHUNK, S5_GROUPS, S5_GROUP).transpose(1, 0, 2, 3, 4)

    def block_step(carry, u_blk):
        bu = jnp.einsum('btgh,gph->btgp', u_blk.astype(jnp.complex64), b_bar)
        _, hs = lax.associative_scan(_s5_combine, (a_elems, bu), axis=1)
        hs = hs + a_pow[None] * carry[:, None]
        y = jnp.einsum('btgp,ghp->btgh', hs, c_mat).real
        return hs[:, -1], y

    carry0 = jnp.zeros((bsz, S5_GROUPS, S5_STATE), jnp.complex64)
    _, ys = lax.scan(block_step, carry0, u_blocks)
    y = ys.transpose(1, 0, 2, 3, 4).reshape(bsz, L, S5_WIDTH) + d.astype(f32) * u
    g = jax.nn.gelu(y)
    ab = g @ w_glu.astype(f32) + b_glu.astype(f32)
    val, gate = jnp.split(ab, 2, axis=-1)
    return (val * jax.nn.sigmoid(gate)).astype(h.dtype)


def _causal_dwconv(u, w):
    return lax.conv_general_dilated(u, w[:, None, :].astype(u.dtype), (1,), [(M2_CONV - 1, 0)],
                                    dimension_numbers=('NWC', 'WIO', 'NWC'),
                                    feature_group_count=u.shape[-1])


def _segsum(a):
    t = a.shape[-1]
    cs = jnp.cumsum(a, axis=-1)
    diff = cs[..., :, None] - cs[..., None, :]
    mask = jnp.tril(jnp.ones((t, t), dtype=bool))
    return jnp.where(mask, diff, -jnp.inf)


def _ssd_mixer(h, w_in, conv_w, conv_b, dt_bias, a_log, d, norm_g, w_out):
    f32 = jnp.float32
    bsz, L, _ = h.shape
    nc = L // M2_CHUNK
    G, R, P, N, Q = M2_GROUPS, M2_HPG, M2_HEADDIM, M2_STATE, M2_CHUNK
    zxbcdt = h @ w_in
    z, xbc, dt_raw = jnp.split(zxbcdt, [M2_D_INNER, M2_D_INNER + M2_CONV_DIM], axis=-1)
    xbc = jax.nn.silu(_causal_dwconv(xbc, conv_w) + conv_b)
    xs, b_in, c_in = jnp.split(xbc, [M2_D_INNER, M2_D_INNER + G * N], axis=-1)
    dt = jax.nn.softplus(dt_raw.astype(f32) + dt_bias.astype(f32))
    a = -jnp.exp(a_log.astype(f32)).reshape(G, R)
    xs = xs.astype(f32).reshape(bsz, nc, Q, G, R, P)
    b_in = b_in.astype(f32).reshape(bsz, nc, Q, G, N)
    c_in = c_in.astype(f32).reshape(bsz, nc, Q, G, N)
    dt = dt.reshape(bsz, nc, Q, G, R)
    xdt = xs * dt[..., None]
    a_dt = jnp.transpose(dt * a, (0, 3, 4, 1, 2))
    a_cs = jnp.cumsum(a_dt, axis=-1)
    l_mat = jnp.exp(_segsum(a_dt))
    cb = jnp.einsum('bclgn,bcsgn->bcgls', c_in, b_in)
    y_diag = jnp.einsum('bcgls,bgrcls,bcsgrp->bclgrp', cb, l_mat, xdt)
    decay_states = jnp.exp(a_cs[..., -1:] - a_cs)
    states = jnp.einsum('bclgn,bgrcl,bclgrp->bcgrpn', b_in, decay_states, xdt)
    chunk_decay = jnp.exp(a_cs[..., -1])

    def carry_step(carry, inp):
        st, dec = inp
        return dec[..., None, None] * carry + st, carry

    _, prev = lax.scan(carry_step, jnp.zeros((bsz, G, R, P, N), f32),
                       (jnp.moveaxis(states, 1, 0), jnp.moveaxis(chunk_decay, -1, 0)))
    prev = jnp.moveaxis(prev, 0, 1)
    y_off = jnp.einsum('bclgn,bcgrpn,bgrcl->bclgrp', c_in, prev, jnp.exp(a_cs))
    y = y_diag + y_off + xs * d.astype(f32).reshape(G, R, 1)
    y = y.reshape(bsz, L, M2_D_INNER) * jax.nn.silu(z.astype(f32))
    yg = y.reshape(bsz, L, G, M2_D_INNER // G)
    yg = yg * lax.rsqrt(jnp.mean(yg * yg, axis=-1, keepdims=True) + NORM_EPS)
    y = yg.reshape(bsz, L, M2_D_INNER) * norm_g.astype(f32)
    return y.astype(h.dtype) @ w_out


def _fwd_setup_inputs(seed: int = 0) -> dict:
    f32 = jnp.float32
    key = jax.random.key(seed)
    it = iter(jax.random.split(key, 28))

    def nrm(shape, scale):
        return scale * jax.random.normal(next(it), shape, f32)

    D, F = D_MODEL, D_FF
    NA, NB = N_A_LAYERS, N_B_LAYERS
    G, P, H = S5_GROUPS, S5_STATE, S5_GROUP
    x = nrm((BATCH, SEQ, D), 1.0)
    c = nrm((BATCH, D), 1.0)
    ada_w = nrm((DEPTH, D, N_MOD * D), 0.5 * D ** -0.5)
    ada_b = nrm((DEPTH, N_MOD * D), 0.02)
    norm_mix_g = 1.0 + nrm((DEPTH, D), 0.05)
    norm_mlp_g = 1.0 + nrm((DEPTH, D), 0.05)
    mlp_w1 = nrm((DEPTH, D, F), D ** -0.5)
    mlp_w2 = nrm((DEPTH, F, D), F ** -0.5)
    s5_w_in = nrm((NA, D, S5_WIDTH), D ** -0.5)
    s5_lambda_re = -0.5 + nrm((NA, G, P), 0.01)
    s5_lambda_im = math.pi * jnp.arange(P, dtype=f32) + nrm((NA, G, P), 0.01)
    s5_log_dt = jax.random.uniform(next(it), (NA, G), f32, minval=math.log(S5_DT_MIN), maxval=math.log(S5_DT_MAX))
    s5_b_re = nrm((NA, G, P, H), (2 * H) ** -0.5)
    s5_b_im = nrm((NA, G, P, H), (2 * H) ** -0.5)
    s5_c_re = nrm((NA, G, H, P), (2 * P) ** -0.5)
    s5_c_im = nrm((NA, G, H, P), (2 * P) ** -0.5)
    s5_d = nrm((NA, S5_WIDTH), 1.0)
    s5_w_glu = nrm((NA, S5_WIDTH, 2 * D), S5_WIDTH ** -0.5)
    s5_b_glu = nrm((NA, 2 * D), 0.02)
    m2_w_in = nrm((NB, D, M2_IN_DIM), D ** -0.5)
    m2_conv_w = nrm((NB, M2_CONV, M2_CONV_DIM), M2_CONV ** -0.5)
    m2_conv_b = nrm((NB, M2_CONV_DIM), 0.02)
    dt0 = jnp.exp(jax.random.uniform(next(it), (NB, M2_HEADS), f32, minval=math.log(M2_DT_MIN), maxval=math.log(M2_DT_MAX)))
    m2_dt_bias = dt0 + jnp.log(-jnp.expm1(-dt0))
    m2_a_log = jnp.log(jax.random.uniform(next(it), (NB, M2_HEADS), f32, minval=1.0, maxval=16.0))
    m2_d = 1.0 + nrm((NB, M2_HEADS), 0.1)
    m2_norm_g = 1.0 + nrm((NB, M2_D_INNER), 0.05)
    m2_w_out = nrm((NB, M2_D_INNER, D), M2_D_INNER ** -0.5)
    final_norm_g = 1.0 + nrm((D,), 0.05)
    return {'x': x, 'c': c, 'ada_w': ada_w, 'ada_b': ada_b,
            'norm_mix_g': norm_mix_g, 'norm_mlp_g': norm_mlp_g,
            'mlp_w1': mlp_w1, 'mlp_w2': mlp_w2,
            's5_w_in': s5_w_in, 's5_lambda_re': s5_lambda_re, 's5_lambda_im': s5_lambda_im,
            's5_log_dt': s5_log_dt, 's5_b_re': s5_b_re, 's5_b_im': s5_b_im,
            's5_c_re': s5_c_re, 's5_c_im': s5_c_im, 's5_d': s5_d,
            's5_w_glu': s5_w_glu, 's5_b_glu': s5_b_glu,
            'm2_w_in': m2_w_in, 'm2_conv_w': m2_conv_w, 'm2_conv_b': m2_conv_b,
            'm2_dt_bias': m2_dt_bias, 'm2_a_log': m2_a_log, 'm2_d': m2_d,
            'm2_norm_g': m2_norm_g, 'm2_w_out': m2_w_out,
            'final_norm_g': final_norm_g}


def _fwd_reference(x, c, ada_w, ada_b, norm_mix_g, norm_mlp_g, mlp_w1, mlp_w2,
              s5_w_in, s5_lambda_re, s5_lambda_im, s5_log_dt, s5_b_re, s5_b_im,
              s5_c_re, s5_c_im, s5_d, s5_w_glu, s5_b_glu,
              m2_w_in, m2_conv_w, m2_conv_b, m2_dt_bias, m2_a_log, m2_d,
              m2_norm_g, m2_w_out, final_norm_g):
    cond = jax.nn.silu(c)
    for i in range(DEPTH):
        mod = cond @ ada_w[i] + ada_b[i]
        sh1, sc1, g1, sh2, sc2, g2 = jnp.split(mod[:, None, :], N_MOD, axis=-1)
        h = _modulate(_rmsnorm(x, norm_mix_g[i]), sh1, sc1)
        j = i // N_MIXERS
        if i % N_MIXERS == 0:
            y = _s5_mixer(h, s5_w_in[j], s5_lambda_re[j], s5_lambda_im[j], s5_log_dt[j],
                          s5_b_re[j], s5_b_im[j], s5_c_re[j], s5_c_im[j], s5_d[j],
                          s5_w_glu[j], s5_b_glu[j])
        else:
            y = _ssd_mixer(h, m2_w_in[j], m2_conv_w[j], m2_conv_b[j], m2_dt_bias[j],
                           m2_a_log[j], m2_d[j], m2_norm_g[j], m2_w_out[j])
        x = x + g1 * y
        h = _modulate(_rmsnorm(x, norm_mlp_g[i]), sh2, sc2)
        x = x + g2 * _sq_relu_mlp(h, mlp_w1[i], mlp_w2[i])
    return _rmsnorm(x, final_norm_g)


import jax as _jax
import jax.numpy as _jnp

TWIN_FORMAT = 'train_step'
FWD_PARAMS = ['x', 'c', 'ada_w', 'ada_b', 'norm_mix_g', 'norm_mlp_g', 'mlp_w1', 'mlp_w2', 's5_w_in', 's5_lambda_re', 's5_lambda_im', 's5_log_dt', 's5_b_re', 's5_b_im', 's5_c_re', 's5_c_im', 's5_d', 's5_w_glu', 's5_b_glu', 'm2_w_in', 'm2_conv_w', 'm2_conv_b', 'm2_dt_bias', 'm2_a_log', 'm2_d', 'm2_norm_g', 'm2_w_out', 'final_norm_g']
TWIN_WEIGHTS = ['ada_w', 'ada_b', 'norm_mix_g', 'norm_mlp_g', 'mlp_w1', 'mlp_w2', 's5_w_in', 's5_lambda_re', 's5_lambda_im', 's5_log_dt', 's5_b_re', 's5_b_im', 's5_c_re', 's5_c_im', 's5_d', 's5_w_glu', 's5_b_glu', 'm2_w_in', 'm2_conv_w', 'm2_conv_b', 'm2_dt_bias', 'm2_a_log', 'm2_d', 'm2_norm_g', 'm2_w_out', 'final_norm_g']
TWIN_DIFF_INPUT = 'x'
TWIN_INPUTS = ['x', 'c', 'ada_w', 'ada_b', 'norm_mix_g', 'norm_mlp_g', 'mlp_w1', 'mlp_w2', 's5_w_in', 's5_lambda_re', 's5_lambda_im', 's5_log_dt', 's5_b_re', 's5_b_im', 's5_c_re', 's5_c_im', 's5_d', 's5_w_glu', 's5_b_glu', 'm2_w_in', 'm2_conv_w', 'm2_conv_b', 'm2_dt_bias', 'm2_a_log', 'm2_d', 'm2_norm_g', 'm2_w_out', 'final_norm_g', 'loss_target', 'm_ada_w', 'm_ada_b', 'm_norm_mix_g', 'm_norm_mlp_g', 'm_mlp_w1', 'm_mlp_w2', 'm_s5_w_in', 'm_s5_lambda_re', 'm_s5_lambda_im', 'm_s5_log_dt', 'm_s5_b_re', 'm_s5_b_im', 'm_s5_c_re', 'm_s5_c_im', 'm_s5_d', 'm_s5_w_glu', 'm_s5_b_glu', 'm_m2_w_in', 'm_m2_conv_w', 'm_m2_conv_b', 'm_m2_dt_bias', 'm_m2_a_log', 'm_m2_d', 'm_m2_norm_g', 'm_m2_w_out', 'm_final_norm_g', 'v_ada_w', 'v_ada_b', 'v_norm_mix_g', 'v_norm_mlp_g', 'v_mlp_w1', 'v_mlp_w2', 'v_s5_w_in', 'v_s5_lambda_re', 'v_s5_lambda_im', 'v_s5_log_dt', 'v_s5_b_re', 'v_s5_b_im', 'v_s5_c_re', 'v_s5_c_im', 'v_s5_d', 'v_s5_w_glu', 'v_s5_b_glu', 'v_m2_w_in', 'v_m2_conv_w', 'v_m2_conv_b', 'v_m2_dt_bias', 'v_m2_a_log', 'v_m2_d', 'v_m2_norm_g', 'v_m2_w_out', 'v_final_norm_g']
TWIN_OUTPUTS = ['loss', 'grad_x', 'grad_ada_w', 'grad_ada_b', 'grad_norm_mix_g', 'grad_norm_mlp_g', 'grad_mlp_w1', 'grad_mlp_w2', 'grad_s5_w_in', 'grad_s5_lambda_re', 'grad_s5_lambda_im', 'grad_s5_log_dt', 'grad_s5_b_re', 'grad_s5_b_im', 'grad_s5_c_re', 'grad_s5_c_im', 'grad_s5_d', 'grad_s5_w_glu', 'grad_s5_b_glu', 'grad_m2_w_in', 'grad_m2_conv_w', 'grad_m2_conv_b', 'grad_m2_dt_bias', 'grad_m2_a_log', 'grad_m2_d', 'grad_m2_norm_g', 'grad_m2_w_out', 'grad_final_norm_g', 'delta_ada_w', 'delta_ada_b', 'delta_norm_mix_g', 'delta_norm_mlp_g', 'delta_mlp_w1', 'delta_mlp_w2', 'delta_s5_w_in', 'delta_s5_lambda_re', 'delta_s5_lambda_im', 'delta_s5_log_dt', 'delta_s5_b_re', 'delta_s5_b_im', 'delta_s5_c_re', 'delta_s5_c_im', 'delta_s5_d', 'delta_s5_w_glu', 'delta_s5_b_glu', 'delta_m2_w_in', 'delta_m2_conv_w', 'delta_m2_conv_b', 'delta_m2_dt_bias', 'delta_m2_a_log', 'delta_m2_d', 'delta_m2_norm_g', 'delta_m2_w_out', 'delta_final_norm_g', 'new_m_ada_w', 'new_m_ada_b', 'new_m_norm_mix_g', 'new_m_norm_mlp_g', 'new_m_mlp_w1', 'new_m_mlp_w2', 'new_m_s5_w_in', 'new_m_s5_lambda_re', 'new_m_s5_lambda_im', 'new_m_s5_log_dt', 'new_m_s5_b_re', 'new_m_s5_b_im', 'new_m_s5_c_re', 'new_m_s5_c_im', 'new_m_s5_d', 'new_m_s5_w_glu', 'new_m_s5_b_glu', 'new_m_m2_w_in', 'new_m_m2_conv_w', 'new_m_m2_conv_b', 'new_m_m2_dt_bias', 'new_m_m2_a_log', 'new_m_m2_d', 'new_m_m2_norm_g', 'new_m_m2_w_out', 'new_m_final_norm_g', 'new_v_ada_w', 'new_v_ada_b', 'new_v_norm_mix_g', 'new_v_norm_mlp_g', 'new_v_mlp_w1', 'new_v_mlp_w2', 'new_v_s5_w_in', 'new_v_s5_lambda_re', 'new_v_s5_lambda_im', 'new_v_s5_log_dt', 'new_v_s5_b_re', 'new_v_s5_b_im', 'new_v_s5_c_re', 'new_v_s5_c_im', 'new_v_s5_d', 'new_v_s5_w_glu', 'new_v_s5_b_glu', 'new_v_m2_w_in', 'new_v_m2_conv_w', 'new_v_m2_conv_b', 'new_v_m2_dt_bias', 'new_v_m2_a_log', 'new_v_m2_d', 'new_v_m2_norm_g', 'new_v_m2_w_out', 'new_v_final_norm_g']
TWIN_LEAF_KINDS = {'loss': 'loss', 'grad_x': 'grad_x', 'grad_ada_w': 'grad_w', 'grad_ada_b': 'grad_w', 'grad_norm_mix_g': 'grad_w', 'grad_norm_mlp_g': 'grad_w', 'grad_mlp_w1': 'grad_w', 'grad_mlp_w2': 'grad_w', 'grad_s5_w_in': 'grad_w', 'grad_s5_lambda_re': 'grad_w', 'grad_s5_lambda_im': 'grad_w', 'grad_s5_log_dt': 'grad_w', 'grad_s5_b_re': 'grad_w', 'grad_s5_b_im': 'grad_w', 'grad_s5_c_re': 'grad_w', 'grad_s5_c_im': 'grad_w', 'grad_s5_d': 'grad_w', 'grad_s5_w_glu': 'grad_w', 'grad_s5_b_glu': 'grad_w', 'grad_m2_w_in': 'grad_w', 'grad_m2_conv_w': 'grad_w', 'grad_m2_conv_b': 'grad_w', 'grad_m2_dt_bias': 'grad_w', 'grad_m2_a_log': 'grad_w', 'grad_m2_d': 'grad_w', 'grad_m2_norm_g': 'grad_w', 'grad_m2_w_out': 'grad_w', 'grad_final_norm_g': 'grad_w', 'delta_ada_w': 'delta_w', 'delta_ada_b': 'delta_w', 'delta_norm_mix_g': 'delta_w', 'delta_norm_mlp_g': 'delta_w', 'delta_mlp_w1': 'delta_w', 'delta_mlp_w2': 'delta_w', 'delta_s5_w_in': 'delta_w', 'delta_s5_lambda_re': 'delta_w', 'delta_s5_lambda_im': 'delta_w', 'delta_s5_log_dt': 'delta_w', 'delta_s5_b_re': 'delta_w', 'delta_s5_b_im': 'delta_w', 'delta_s5_c_re': 'delta_w', 'delta_s5_c_im': 'delta_w', 'delta_s5_d': 'delta_w', 'delta_s5_w_glu': 'delta_w', 'delta_s5_b_glu': 'delta_w', 'delta_m2_w_in': 'delta_w', 'delta_m2_conv_w': 'delta_w', 'delta_m2_conv_b': 'delta_w', 'delta_m2_dt_bias': 'delta_w', 'delta_m2_a_log': 'delta_w', 'delta_m2_d': 'delta_w', 'delta_m2_norm_g': 'delta_w', 'delta_m2_w_out': 'delta_w', 'delta_final_norm_g': 'delta_w', 'new_m_ada_w': 'new_m', 'new_m_ada_b': 'new_m', 'new_m_norm_mix_g': 'new_m', 'new_m_norm_mlp_g': 'new_m', 'new_m_mlp_w1': 'new_m', 'new_m_mlp_w2': 'new_m', 'new_m_s5_w_in': 'new_m', 'new_m_s5_lambda_re': 'new_m', 'new_m_s5_lambda_im': 'new_m', 'new_m_s5_log_dt': 'new_m', 'new_m_s5_b_re': 'new_m', 'new_m_s5_b_im': 'new_m', 'new_m_s5_c_re': 'new_m', 'new_m_s5_c_im': 'new_m', 'new_m_s5_d': 'new_m', 'new_m_s5_w_glu': 'new_m', 'new_m_s5_b_glu': 'new_m', 'new_m_m2_w_in': 'new_m', 'new_m_m2_conv_w': 'new_m', 'new_m_m2_conv_b': 'new_m', 'new_m_m2_dt_bias': 'new_m', 'new_m_m2_a_log': 'new_m', 'new_m_m2_d': 'new_m', 'new_m_m2_norm_g': 'new_m', 'new_m_m2_w_out': 'new_m', 'new_m_final_norm_g': 'new_m', 'new_v_ada_w': 'new_v', 'new_v_ada_b': 'new_v', 'new_v_norm_mix_g': 'new_v', 'new_v_norm_mlp_g': 'new_v', 'new_v_mlp_w1': 'new_v', 'new_v_mlp_w2': 'new_v', 'new_v_s5_w_in': 'new_v', 'new_v_s5_lambda_re': 'new_v', 'new_v_s5_lambda_im': 'new_v', 'new_v_s5_log_dt': 'new_v', 'new_v_s5_b_re': 'new_v', 'new_v_s5_b_im': 'new_v', 'new_v_s5_c_re': 'new_v', 'new_v_s5_c_im': 'new_v', 'new_v_s5_d': 'new_v', 'new_v_s5_w_glu': 'new_v', 'new_v_s5_b_glu': 'new_v', 'new_v_m2_w_in': 'new_v', 'new_v_m2_conv_w': 'new_v', 'new_v_m2_conv_b': 'new_v', 'new_v_m2_dt_bias': 'new_v', 'new_v_m2_a_log': 'new_v', 'new_v_m2_d': 'new_v', 'new_v_m2_norm_g': 'new_v', 'new_v_m2_w_out': 'new_v', 'new_v_final_norm_g': 'new_v'}


def _forward(args):
    return _fwd_reference(*[args[k] for k in FWD_PARAMS])


def _output_shape():
    def fwd():
        inp = _fwd_setup_inputs(0)
        return _fwd_reference(*[inp[k] for k in FWD_PARAMS])
    out = _jax.eval_shape(fwd)
    return out.shape, out.dtype

N_MICROBATCH = 1
ADAM_LR = 0.001
ADAM_B1 = 0.9
ADAM_B2 = 0.999
ADAM_EPS = 1e-08
ADAM_WD = 0.01
ADAM_STEP = 10
PER_EXAMPLE_BATCH_AXIS = {'x': 0, 'c': 0, 'loss_target': 0}
SHARED_INPUTS = []
_WEIGHT_DTYPES = {'ada_w': _jnp.float32, 'ada_b': _jnp.float32, 'norm_mix_g': _jnp.float32, 'norm_mlp_g': _jnp.float32, 'mlp_w1': _jnp.float32, 'mlp_w2': _jnp.float32, 's5_w_in': _jnp.float32, 's5_lambda_re': _jnp.float32, 's5_lambda_im': _jnp.float32, 's5_log_dt': _jnp.float32, 's5_b_re': _jnp.float32, 's5_b_im': _jnp.float32, 's5_c_re': _jnp.float32, 's5_c_im': _jnp.float32, 's5_d': _jnp.float32, 's5_w_glu': _jnp.float32, 's5_b_glu': _jnp.float32, 'm2_w_in': _jnp.float32, 'm2_conv_w': _jnp.float32, 'm2_conv_b': _jnp.float32, 'm2_dt_bias': _jnp.float32, 'm2_a_log': _jnp.float32, 'm2_d': _jnp.float32, 'm2_norm_g': _jnp.float32, 'm2_w_out': _jnp.float32, 'final_norm_g': _jnp.float32}
MOMENT_SCALE = {'ada_w': 2.610597e-01, 'ada_b': 5.581374e-01, 'norm_mix_g': 1.006259e-01, 'norm_mlp_g': 1.622735e-01, 'mlp_w1': 8.342436e-02, 'mlp_w2': 2.093703e-01, 's5_w_in': 4.636220e-02, 's5_lambda_re': 4.887382e-03, 's5_lambda_im': 4.880458e-03, 's5_log_dt': 1.635065e+00, 's5_b_re': 2.557947e-03, 's5_b_im': 2.353991e-03, 's5_c_re': 5.135033e-03, 's5_c_im': 4.974558e-03, 's5_d': 6.030238e-02, 's5_w_glu': 3.998728e-02, 's5_b_glu': 8.451513e-02, 'm2_w_in': 6.270970e-02, 'm2_conv_w': 6.017147e-02, 'm2_conv_b': 8.427369e-02, 'm2_dt_bias': 1.348947e-01, 'm2_a_log': 2.354730e-01, 'm2_d': 3.061941e-01, 'm2_norm_g': 7.098012e-02, 'm2_w_out': 1.056247e-01, 'final_norm_g': 1.291863e+02}


def _to_microbatches(a, axis):
    t = _jnp.moveaxis(a, axis, 0)
    t = t.reshape((N_MICROBATCH, t.shape[0] // N_MICROBATCH) + t.shape[1:])
    return _jnp.moveaxis(t, 1, axis + 1)


def setup_inputs(seed: int = 0) -> dict:
    inp = _fwd_setup_inputs(seed)
    key = _jax.random.fold_in(_jax.random.key(seed), 7919)
    shape, _ = _output_shape()
    out = dict(inp)
    out["loss_target"] = _jax.random.normal(_jax.random.fold_in(key, 0), shape, _jnp.float32)
    for i, name in enumerate(TWIN_WEIGHTS):
        w = inp[name].astype(_jnp.float32)
        if MOMENT_SCALE is None:
            s = _jnp.sqrt(_jnp.mean(_jnp.square(w)) + 1e-30)
        else:
            s = MOMENT_SCALE[name]
        km, kv = _jax.random.split(_jax.random.fold_in(key, i + 1))
        out[name] = w
        out["m_" + name] = s * _jax.random.normal(km, w.shape, _jnp.float32)
        out["v_" + name] = (s * s) * _jax.random.uniform(kv, w.shape, _jnp.float32, 0.5, 1.5)
    if N_MICROBATCH > 1:
        for name, axis in PER_EXAMPLE_BATCH_AXIS.items():
            out[name] = _to_microbatches(out[name], axis)
    return {'x': out['x'], 'c': out['c'], 'ada_w': out['ada_w'], 'ada_b': out['ada_b'], 'norm_mix_g': out['norm_mix_g'], 'norm_mlp_g': out['norm_mlp_g'], 'mlp_w1': out['mlp_w1'], 'mlp_w2': out['mlp_w2'], 's5_w_in': out['s5_w_in'], 's5_lambda_re': out['s5_lambda_re'], 's5_lambda_im': out['s5_lambda_im'], 's5_log_dt': out['s5_log_dt'], 's5_b_re': out['s5_b_re'], 's5_b_im': out['s5_b_im'], 's5_c_re': out['s5_c_re'], 's5_c_im': out['s5_c_im'], 's5_d': out['s5_d'], 's5_w_glu': out['s5_w_glu'], 's5_b_glu': out['s5_b_glu'], 'm2_w_in': out['m2_w_in'], 'm2_conv_w': out['m2_conv_w'], 'm2_conv_b': out['m2_conv_b'], 'm2_dt_bias': out['m2_dt_bias'], 'm2_a_log': out['m2_a_log'], 'm2_d': out['m2_d'], 'm2_norm_g': out['m2_norm_g'], 'm2_w_out': out['m2_w_out'], 'final_norm_g': out['final_norm_g'], 'loss_target': out['loss_target'], 'm_ada_w': out['m_ada_w'], 'm_ada_b': out['m_ada_b'], 'm_norm_mix_g': out['m_norm_mix_g'], 'm_norm_mlp_g': out['m_norm_mlp_g'], 'm_mlp_w1': out['m_mlp_w1'], 'm_mlp_w2': out['m_mlp_w2'], 'm_s5_w_in': out['m_s5_w_in'], 'm_s5_lambda_re': out['m_s5_lambda_re'], 'm_s5_lambda_im': out['m_s5_lambda_im'], 'm_s5_log_dt': out['m_s5_log_dt'], 'm_s5_b_re': out['m_s5_b_re'], 'm_s5_b_im': out['m_s5_b_im'], 'm_s5_c_re': out['m_s5_c_re'], 'm_s5_c_im': out['m_s5_c_im'], 'm_s5_d': out['m_s5_d'], 'm_s5_w_glu': out['m_s5_w_glu'], 'm_s5_b_glu': out['m_s5_b_glu'], 'm_m2_w_in': out['m_m2_w_in'], 'm_m2_conv_w': out['m_m2_conv_w'], 'm_m2_conv_b': out['m_m2_conv_b'], 'm_m2_dt_bias': out['m_m2_dt_bias'], 'm_m2_a_log': out['m_m2_a_log'], 'm_m2_d': out['m_m2_d'], 'm_m2_norm_g': out['m_m2_norm_g'], 'm_m2_w_out': out['m_m2_w_out'], 'm_final_norm_g': out['m_final_norm_g'], 'v_ada_w': out['v_ada_w'], 'v_ada_b': out['v_ada_b'], 'v_norm_mix_g': out['v_norm_mix_g'], 'v_norm_mlp_g': out['v_norm_mlp_g'], 'v_mlp_w1': out['v_mlp_w1'], 'v_mlp_w2': out['v_mlp_w2'], 'v_s5_w_in': out['v_s5_w_in'], 'v_s5_lambda_re': out['v_s5_lambda_re'], 'v_s5_lambda_im': out['v_s5_lambda_im'], 'v_s5_log_dt': out['v_s5_log_dt'], 'v_s5_b_re': out['v_s5_b_re'], 'v_s5_b_im': out['v_s5_b_im'], 'v_s5_c_re': out['v_s5_c_re'], 'v_s5_c_im': out['v_s5_c_im'], 'v_s5_d': out['v_s5_d'], 'v_s5_w_glu': out['v_s5_w_glu'], 'v_s5_b_glu': out['v_s5_b_glu'], 'v_m2_w_in': out['v_m2_w_in'], 'v_m2_conv_w': out['v_m2_conv_w'], 'v_m2_conv_b': out['v_m2_conv_b'], 'v_m2_dt_bias': out['v_m2_dt_bias'], 'v_m2_a_log': out['v_m2_a_log'], 'v_m2_d': out['v_m2_d'], 'v_m2_norm_g': out['v_m2_norm_g'], 'v_m2_w_out': out['v_m2_w_out'], 'v_final_norm_g': out['v_final_norm_g']}


def _loss(weights, diff, rest, loss_target):
    with _jax.named_scope("forward"):
        args = {**rest, TWIN_DIFF_INPUT: diff, **{k: w.astype(_WEIGHT_DTYPES[k]) for k, w in weights.items()}}
        y = _forward(args)
    with _jax.named_scope("loss_head"):
        err = _jnp.square(y.astype(_jnp.float32) - loss_target)
        return 0.5 * _jnp.sum(_jnp.mean(err, axis=-1)) if err.ndim else 0.5 * err


def _adamw(w, g, m, v):
    m = ADAM_B1 * m + (1.0 - ADAM_B1) * g
    v = ADAM_B2 * v + (1.0 - ADAM_B2) * _jnp.square(g)
    m_hat = m / (1.0 - ADAM_B1 ** ADAM_STEP)
    v_hat = v / (1.0 - ADAM_B2 ** ADAM_STEP)
    delta = -ADAM_LR * (m_hat / (_jnp.sqrt(v_hat) + ADAM_EPS) + ADAM_WD * w)
    return delta, m, v


def reference(x, c, ada_w, ada_b, norm_mix_g, norm_mlp_g, mlp_w1, mlp_w2, s5_w_in, s5_lambda_re, s5_lambda_im, s5_log_dt, s5_b_re, s5_b_im, s5_c_re, s5_c_im, s5_d, s5_w_glu, s5_b_glu, m2_w_in, m2_conv_w, m2_conv_b, m2_dt_bias, m2_a_log, m2_d, m2_norm_g, m2_w_out, final_norm_g, loss_target, m_ada_w, m_ada_b, m_norm_mix_g, m_norm_mlp_g, m_mlp_w1, m_mlp_w2, m_s5_w_in, m_s5_lambda_re, m_s5_lambda_im, m_s5_log_dt, m_s5_b_re, m_s5_b_im, m_s5_c_re, m_s5_c_im, m_s5_d, m_s5_w_glu, m_s5_b_glu, m_m2_w_in, m_m2_conv_w, m_m2_conv_b, m_m2_dt_bias, m_m2_a_log, m_m2_d, m_m2_norm_g, m_m2_w_out, m_final_norm_g, v_ada_w, v_ada_b, v_norm_mix_g, v_norm_mlp_g, v_mlp_w1, v_mlp_w2, v_s5_w_in, v_s5_lambda_re, v_s5_lambda_im, v_s5_log_dt, v_s5_b_re, v_s5_b_im, v_s5_c_re, v_s5_c_im, v_s5_d, v_s5_w_glu, v_s5_b_glu, v_m2_w_in, v_m2_conv_w, v_m2_conv_b, v_m2_dt_bias, v_m2_a_log, v_m2_d, v_m2_norm_g, v_m2_w_out, v_final_norm_g):
    given = dict(x=x, c=c, ada_w=ada_w, ada_b=ada_b, norm_mix_g=norm_mix_g, norm_mlp_g=norm_mlp_g, mlp_w1=mlp_w1, mlp_w2=mlp_w2, s5_w_in=s5_w_in, s5_lambda_re=s5_lambda_re, s5_lambda_im=s5_lambda_im, s5_log_dt=s5_log_dt, s5_b_re=s5_b_re, s5_b_im=s5_b_im, s5_c_re=s5_c_re, s5_c_im=s5_c_im, s5_d=s5_d, s5_w_glu=s5_w_glu, s5_b_glu=s5_b_glu, m2_w_in=m2_w_in, m2_conv_w=m2_conv_w, m2_conv_b=m2_conv_b, m2_dt_bias=m2_dt_bias, m2_a_log=m2_a_log, m2_d=m2_d, m2_norm_g=m2_norm_g, m2_w_out=m2_w_out, final_norm_g=final_norm_g, loss_target=loss_target, m_ada_w=m_ada_w, m_ada_b=m_ada_b, m_norm_mix_g=m_norm_mix_g, m_norm_mlp_g=m_norm_mlp_g, m_mlp_w1=m_mlp_w1, m_mlp_w2=m_mlp_w2, m_s5_w_in=m_s5_w_in, m_s5_lambda_re=m_s5_lambda_re, m_s5_lambda_im=m_s5_lambda_im, m_s5_log_dt=m_s5_log_dt, m_s5_b_re=m_s5_b_re, m_s5_b_im=m_s5_b_im, m_s5_c_re=m_s5_c_re, m_s5_c_im=m_s5_c_im, m_s5_d=m_s5_d, m_s5_w_glu=m_s5_w_glu, m_s5_b_glu=m_s5_b_glu, m_m2_w_in=m_m2_w_in, m_m2_conv_w=m_m2_conv_w, m_m2_conv_b=m_m2_conv_b, m_m2_dt_bias=m_m2_dt_bias, m_m2_a_log=m_m2_a_log, m_m2_d=m_m2_d, m_m2_norm_g=m_m2_norm_g, m_m2_w_out=m_m2_w_out, m_final_norm_g=m_final_norm_g, v_ada_w=v_ada_w, v_ada_b=v_ada_b, v_norm_mix_g=v_norm_mix_g, v_norm_mlp_g=v_norm_mlp_g, v_mlp_w1=v_mlp_w1, v_mlp_w2=v_mlp_w2, v_s5_w_in=v_s5_w_in, v_s5_lambda_re=v_s5_lambda_re, v_s5_lambda_im=v_s5_lambda_im, v_s5_log_dt=v_s5_log_dt, v_s5_b_re=v_s5_b_re, v_s5_b_im=v_s5_b_im, v_s5_c_re=v_s5_c_re, v_s5_c_im=v_s5_c_im, v_s5_d=v_s5_d, v_s5_w_glu=v_s5_w_glu, v_s5_b_glu=v_s5_b_glu, v_m2_w_in=v_m2_w_in, v_m2_conv_w=v_m2_conv_w, v_m2_conv_b=v_m2_conv_b, v_m2_dt_bias=v_m2_dt_bias, v_m2_a_log=v_m2_a_log, v_m2_d=v_m2_d, v_m2_norm_g=v_m2_norm_g, v_m2_w_out=v_m2_w_out, v_final_norm_g=v_final_norm_g)
    weights = {n: given[n] for n in TWIN_WEIGHTS}
    shared = {n: given[n] for n in SHARED_INPUTS}
    per_example = {n: given[n] for n in ['x', 'c']}
    grad_fn = _jax.value_and_grad(_loss, argnums=(0, 1))

    def one_microbatch(ex, loss_target):
        ex = dict(ex)
        diff = ex.pop(TWIN_DIFF_INPUT)
        return grad_fn(weights, diff, {**shared, **ex}, loss_target)

    if N_MICROBATCH == 1:
        loss, (grad_w, grad_x) = one_microbatch(per_example, given["loss_target"])
    else:
        def body(carry, xs):
            loss_sum, grad_sum = carry
            l_k, (gw_k, gx_k) = one_microbatch(xs[0], xs[1])
            with _jax.named_scope("update"):
                return (loss_sum + l_k, _jax.tree.map(_jnp.add, grad_sum, gw_k)), gx_k

        init = (_jnp.zeros((), _jnp.float32), _jax.tree.map(_jnp.zeros_like, weights))
        (loss, grad_w), grad_x = _jax.lax.scan(body, init, (per_example, given["loss_target"]))
    with _jax.named_scope("update"):
        delta_w, new_m, new_v = {}, {}, {}
        for n in TWIN_WEIGHTS:
            delta_w[n], new_m[n], new_v[n] = _adamw(weights[n], grad_w[n], given["m_" + n], given["v_" + n])
    return (loss, grad_x, *[grad_w[n] for n in TWIN_WEIGHTS], *[delta_w[n] for n in TWIN_WEIGHTS],
            *[new_m[n] for n in TWIN_WEIGHTS], *[new_v[n] for n in TWIN_WEIGHTS])
```

```python
import functools
import math

import jax
import jax.numpy as jnp
from jax import lax
from jax.experimental import pallas as pl
from jax.experimental.pallas import tpu as pltpu

F32 = jnp.float32
BF16 = jnp.bfloat16
HIGHEST = lax.Precision.HIGHEST

NORM_EPS = 1e-5
N_MOD = 6
S5_H, S5_P, S5_T = 16, 64, 64
M2_P, M2_N, M2_G, M2_Q, M2_K = 64, 128, 4, 128, 4
LANES = 128
ADAM_LR, ADAM_B1, ADAM_B2, ADAM_EPS, ADAM_WD, ADAM_STEP = 0.001, 0.9, 0.999, 1e-08, 0.01, 10
VMEM_LIMIT_BYTES = 48 * 1024 * 1024
ROW_TILE = 256
FLAT_W = 1024
FLAT_ROWS = 256
MESH = pl.DeviceIdType.MESH


def _params(*sem):
    return pltpu.CompilerParams(dimension_semantics=sem, vmem_limit_bytes=VMEM_LIMIT_BYTES)


def _dot(a, b, dn="nn", precision=None):
    dims = {"nn": ((1,), (0,)), "nt": ((1,), (1,)), "tn": ((0,), (0,))}[dn]
    return lax.dot_general(a, b, (dims, ((), ())), preferred_element_type=F32, precision=precision)


def _bdot(a, b, dn="nn"):
    return _dot(a.astype(BF16), b.astype(BF16), dn)


def _sigmoid(x):
    return jax.nn.sigmoid(x)


def _colsum(x):
    return jnp.sum(x, axis=0, keepdims=True)


def _pick_tile(dim, want):
    if dim <= want:
        return dim
    for t in range(want - want % LANES, 0, -LANES):
        if dim % t == 0:
            return t
    raise ValueError((dim, want))


def _matmul(name, a, b, mode="nn", out_dtype=F32, relu=False, square_a=False, mul2=None, addin=None,
            tm=1024, tn=1024, tk=512):
    if mode == "nn":
        (m, k), (k2, n) = a.shape, b.shape
    elif mode == "nt":
        (m, k), (n, k2) = a.shape, b.shape
    else:
        (k, m), (k2, n) = a.shape, b.shape
    assert k == k2, (name, a.shape, b.shape)
    tm, tn, tk = _pick_tile(m, tm), _pick_tile(n, tn), _pick_tile(k, tk)
    assert m % tm == 0 and n % tn == 0 and k % tk == 0, (name, a.shape, b.shape)
    nk = k // tk
    extras = [e for e in (mul2, addin) if e is not None]

    def kern(*refs):
        a_ref, b_ref = refs[:2]
        e_refs = refs[2:2 + len(extras)]
        o_ref, acc = refs[2 + len(extras)], refs[3 + len(extras)]
        kk = pl.program_id(2)

        @pl.when(kk == 0)
        def _():
            acc[...] = jnp.zeros_like(acc)

        av = a_ref[...]
        if square_a:
            av = av * av
        acc[...] += _bdot(av, b_ref[...], mode)

        @pl.when(kk == nk - 1)
        def _():
            r = acc[...]
            if relu:
                r = jnp.maximum(r, 0.0)
            idx = 0
            if mul2 is not None:
                r = r * (2.0 * e_refs[idx][...].astype(F32))
                idx += 1
            if addin is not None:
                r = r + e_refs[idx][...].astype(F32)
            o_ref[...] = r.astype(out_dtype)

    if mode == "tn":
        a_spec = pl.BlockSpec((tk, tm), lambda i, j, kk: (kk, i))
    else:
        a_spec = pl.BlockSpec((tm, tk), lambda i, j, kk: (i, kk))
    if mode == "nt":
        b_spec = pl.BlockSpec((tn, tk), lambda i, j, kk: (j, kk))
    else:
        b_spec = pl.BlockSpec((tk, tn), lambda i, j, kk: (kk, j))
    o_spec = pl.BlockSpec((tm, tn), lambda i, j, kk: (i, j))
    return pl.pallas_call(
        kern, name=name, grid=(m // tm, n // tn, nk),
        in_specs=[a_spec, b_spec] + [o_spec] * len(extras), out_specs=o_spec,
        out_shape=jax.ShapeDtypeStruct((m, n), out_dtype),
        scratch_shapes=[pltpu.VMEM((tm, tn), F32)],
        compiler_params=_params("parallel", "parallel", "arbitrary"),
    )(a, b, *extras)


def _rowcall(name, body, rows, tile, row_ins, small_ins, row_outs, acc_outs):
    tile = min(tile, rows)
    assert rows % tile == 0, (name, rows, tile)
    n_in = len(row_ins) + len(small_ins)

    def kern(*refs):
        i = pl.program_id(0)
        accs = refs[n_in + len(row_outs):]

        @pl.when(i == 0)
        def _():
            for acc in accs:
                acc[...] = jnp.zeros_like(acc)

        body(i, *refs)

    def whole(shape):
        return pl.BlockSpec(shape, lambda i, nd=len(shape): (0,) * nd)

    in_specs = [pl.BlockSpec((tile, a.shape[1]), lambda i: (i, 0)) for a in row_ins]
    in_specs += [whole(a.shape) for a in small_ins]
    out_specs = [pl.BlockSpec((tile, w), lambda i: (i, 0)) for (w, _) in row_outs]
    out_specs += [whole(s) for s in acc_outs]
    out_shape = [jax.ShapeDtypeStruct((rows, w), dt) for (w, dt) in row_outs]
    out_shape += [jax.ShapeDtypeStruct(s, F32) for s in acc_outs]
    return pl.pallas_call(
        kern, name=name, grid=(rows // tile,), in_specs=in_specs, out_specs=out_specs, out_shape=out_shape,
        compiler_params=_params("arbitrary"),
    )(*row_ins, *small_ins)


def _rms(x):
    r = lax.rsqrt(jnp.mean(x * x, axis=-1, keepdims=True) + NORM_EPS)
    return x * r, r


def _rms_bwd(dxhat, xhat, r):
    return r * (dxhat - xhat * jnp.mean(dxhat * xhat, axis=-1, keepdims=True))


def _normmod_fwd(name, x, g, sh, sc):
    def body(i, x_ref, g_ref, sh_ref, sc_ref, o_ref):
        xhat, _ = _rms(x_ref[...])
        o_ref[...] = ((xhat * g_ref[...]) * (1.0 + sc_ref[...]) + sh_ref[...]).astype(BF16)

    return _rowcall(name, body, x.shape[0], ROW_TILE, [x], [g, sh, sc], [(x.shape[1], BF16)], [])[0]


def _normmod_bwd(name, x, g, sh, sc, dh, dx_pass):
    d = x.shape[1]

    def body(i, x_ref, dh_ref, dxp_ref, g_ref, sh_ref, sc_ref, dx_ref, dg_ref, dsh_ref, dsc_ref):
        xhat, r = _rms(x_ref[...])
        dh = dh_ref[...].astype(F32)
        gv = g_ref[...]
        dn = dh * (1.0 + sc_ref[...])
        dsc_ref[...] += _colsum(dh * (xhat * gv))
        dsh_ref[...] += _colsum(dh)
        dg_ref[...] += _colsum(dn * xhat)
        dx_ref[...] = dxp_ref[...] + _rms_bwd(dn * gv, xhat, r)

    return _rowcall(name, body, x.shape[0], ROW_TILE, [x, dh, dx_pass], [g, sh, sc], [(d, F32)],
                    [(1, d), (1, d), (1, d)])


def _resid_fwd(name, x, y, g):
    def body(i, x_ref, y_ref, g_ref, o_ref):
        o_ref[...] = x_ref[...] + g_ref[...] * y_ref[...]

    return _rowcall(name, body, x.shape[0], ROW_TILE, [x, y], [g], [(x.shape[1], F32)], [])[0]


def _resid_bwd(name, dxo, y, g):
    d = y.shape[1]

    def body(i, dx_ref, y_ref, g_ref, dy_ref, dg_ref):
        dxo_v = dx_ref[...]
        dy_ref[...] = (g_ref[...] * dxo_v).astype(F32)
        dg_ref[...] += _colsum(dxo_v * y_ref[...])

    return _rowcall(name, body, y.shape[0], ROW_TILE, [dxo, y], [g], [(d, F32)], [(1, d)])


GELU_K = math.sqrt(2.0 / math.pi)
GELU_C = 0.044715


def _gelu_fwd(name, y):
    def body(i, y_ref, o_ref):
        v = y_ref[...]
        t = jnp.tanh(GELU_K * (v + GELU_C * (v * v * v)))
        o_ref[...] = (0.5 * v * (1.0 + t)).astype(BF16)

    return _rowcall(name, body, y.shape[0], ROW_TILE, [y], [], [(y.shape[1], BF16)], [])[0]


def _gelu_bwd(name, y, dgl):
    def body(i, y_ref, d_ref, o_ref):
        v = y_ref[...]
        t = jnp.tanh(GELU_K * (v + GELU_C * (v * v * v)))
        dv = 0.5 * (1.0 + t) + 0.5 * v * (1.0 - t * t) * (GELU_K * (1.0 + 3.0 * GELU_C * v * v))
        o_ref[...] = d_ref[...] * dv

    return _rowcall(name, body, y.shape[0], ROW_TILE, [y, dgl], [], [(y.shape[1], F32)], [])[0]


def _glu_fwd(name, ab, bias):
    d = ab.shape[1] // 2

    def body(i, ab_ref, b_ref, o_ref):
        v = ab_ref[:, :d] + b_ref[:, :d]
        gt = ab_ref[:, d:] + b_ref[:, d:]
        o_ref[...] = v * _sigmoid(gt)

    return _rowcall(name, body, ab.shape[0], ROW_TILE, [ab], [bias], [(d, F32)], [])[0]


def _glu_bwd(name, ab, bias, dout):
    d = ab.shape[1] // 2

    def body(i, ab_ref, do_ref, b_ref, dab_ref, db_ref):
        v = ab_ref[:, :d] + b_ref[:, :d]
        s = _sigmoid(ab_ref[:, d:] + b_ref[:, d:])
        do = do_ref[...]
        dv = do * s
        dg = do * v * (s * (1.0 - s))
        dab_ref[:, :d] = dv.astype(BF16)
        dab_ref[:, d:] = dg.astype(BF16)
        db_ref[:, :d] += _colsum(dv)
        db_ref[:, d:] += _colsum(dg)

    return _rowcall(name, body, ab.shape[0], ROW_TILE, [ab, dout], [bias], [(2 * d, BF16)], [(1, 2 * d)])


def _gatenorm_fwd(name, y, z, ng):
    di = y.shape[1]
    gw = di // M2_G

    def body(i, y_ref, z_ref, g_ref, o_ref):
        for gi in range(M2_G):
            sl = slice(gi * gw, (gi + 1) * gw)
            zz = z_ref[:, sl]
            y2 = y_ref[:, sl] * (zz * _sigmoid(zz))
            yh, _ = _rms(y2)
            o_ref[:, sl] = (yh * g_ref[:, sl]).astype(BF16)

    return _rowcall(name, body, y.shape[0], ROW_TILE, [y, z], [ng], [(di, BF16)], [])[0]


def _gatenorm_bwd(name, y, z, ng, dyn):
    di = y.shape[1]
    gw = di // M2_G

    def body(i, y_ref, z_ref, d_ref, g_ref, dy_ref, dz_ref, dg_ref):
        for gi in range(M2_G):
            sl = slice(gi * gw, (gi + 1) * gw)
            zz = z_ref[:, sl]
            yy = y_ref[:, sl]
            s = _sigmoid(zz)
            sz = zz * s
            yh, r = _rms(yy * sz)
            dn = d_ref[:, sl]
            dg_ref[:, sl] += _colsum(dn * yh)
            dy2 = _rms_bwd(dn * g_ref[:, sl], yh, r)
            dy_ref[:, sl] = dy2 * sz
            dz_ref[:, sl] = dy2 * yy * (s * (1.0 + zz * (1.0 - s)))

    return _rowcall(name, body, y.shape[0], ROW_TILE, [y, z, dyn], [ng], [(di, F32), (di, F32)], [(1, di)])


def _loss_head(name, x, target, g):
    d = x.shape[1]

    def body(i, x_ref, t_ref, g_ref, dx_ref, dg_ref, loss_ref):
        xhat, r = _rms(x_ref[...])
        gv = g_ref[...]
        err = xhat * gv - t_ref[...]
        per_row = jnp.sum(err * err, axis=-1, keepdims=True) * (0.5 / d)
        loss_ref[...] += jnp.broadcast_to(_colsum(per_row), loss_ref.shape)
        dy = err * (1.0 / d)
        dg_ref[...] += _colsum(dy * xhat)
        dx_ref[...] = _rms_bwd(dy * gv, xhat, r)

    return _rowcall(name, body, x.shape[0], ROW_TILE, [x, target], [g], [(d, F32)], [(1, d), (1, LANES)])


HALO = 8


def _conv_taps(ext_ref, w_ref, b_ref, tile):
    acc = b_ref[...] + w_ref[0:1, :] * ext_ref[pl.ds(HALO - 3, tile), :]
    for k in range(1, M2_K):
        acc = acc + w_ref[k:k + 1, :] * ext_ref[pl.ds(HALO - 3 + k, tile), :]
    return acc


def _halo_call(name, body, rows, tile, width, mains, halo_of, halo_next, smalls, row_outs, acc_outs, scratch):
    tile = min(tile, rows)
    nb = tile // HALO
    last = rows // HALO - 1
    n_in = len(mains) + 1 + len(smalls)

    def kern(*refs):
        i = pl.program_id(0)
        accs = refs[n_in + len(row_outs):n_in + len(row_outs) + len(acc_outs)]

        @pl.when(i == 0)
        def _():
            for acc in accs:
                acc[...] = jnp.zeros_like(acc)

        body(i, *refs)

    def whole(shape):
        return pl.BlockSpec(shape, lambda i, nd=len(shape): (0,) * nd)

    if halo_next:
        halo_spec = pl.BlockSpec((HALO, width), lambda i: (jnp.minimum((i + 1) * nb, last), 0))
    else:
        halo_spec = pl.BlockSpec((HALO, width), lambda i: (jnp.maximum(i * nb - 1, 0), 0))
    in_specs = [pl.BlockSpec((tile, a.shape[1]), lambda i: (i, 0)) for a in mains] + [halo_spec]
    in_specs += [whole(a.shape) for a in smalls]
    out_specs = [pl.BlockSpec((tile, w), lambda i: (i, 0)) for (w, _) in row_outs] + [whole(s) for s in acc_outs]
    out_shape = [jax.ShapeDtypeStruct((rows, w), dt) for (w, dt) in row_outs]
    out_shape += [jax.ShapeDtypeStruct(s, F32) for s in acc_outs]
    return pl.pallas_call(
        kern, name=name, grid=(rows // tile,), in_specs=in_specs, out_specs=out_specs, out_shape=out_shape,
        scratch_shapes=scratch, compiler_params=_params("arbitrary"),
    )(*mains, mains[halo_of], *smalls)


CONV_TILE = 128


def _conv_fwd(name, xin, w, b):
    rows, c = xin.shape
    tile = min(CONV_TILE, rows)

    def body(i, x_ref, h_ref, w_ref, b_ref, o_ref, ext):
        ext[0:HALO, :] = jnp.where(i == 0, 0.0, h_ref[...])
        ext[HALO:, :] = x_ref[...]
        pre = _conv_taps(ext, w_ref, b_ref, tile)
        o_ref[...] = pre * _sigmoid(pre)

    return _halo_call(name, body, rows, tile, c, [xin], 0, False, [w, b], [(c, F32)], [],
                      [pltpu.VMEM((tile + HALO, c), F32)])[0]


def _conv_bwd_pre(name, xin, dout, w, b):
    rows, c = xin.shape
    tile = min(CONV_TILE, rows)

    def body(i, x_ref, d_ref, h_ref, w_ref, b_ref, dp_ref, dw_ref, db_ref, ext):
        ext[0:HALO, :] = jnp.where(i == 0, 0.0, h_ref[...])
        ext[HALO:, :] = x_ref[...]
        pre = _conv_taps(ext, w_ref, b_ref, tile)
        s = _sigmoid(pre)
        dp = d_ref[...] * (s * (1.0 + pre * (1.0 - s)))
        dp_ref[...] = dp
        db_ref[...] += _colsum(dp)
        for k in range(M2_K):
            dw_ref[k:k + 1, :] += _colsum(dp * ext[pl.ds(HALO - 3 + k, tile), :])

    return _halo_call(name, body, rows, tile, c, [xin, dout], 0, False, [w, b], [(c, F32)], [(M2_K, c), (1, c)],
                      [pltpu.VMEM((tile + HALO, c), F32)])


def _conv_bwd_in(name, dpre, w):
    rows, c = dpre.shape
    tile = min(CONV_TILE, rows)
    n_tiles = rows // tile

    def body(i, d_ref, h_ref, w_ref, o_ref, ext):
        ext[0:tile, :] = d_ref[...]
        ext[tile:, :] = jnp.where(i == n_tiles - 1, 0.0, h_ref[...])
        acc = w_ref[0:1, :] * ext[pl.ds(3, tile), :]
        for k in range(1, M2_K):
            acc = acc + w_ref[k:k + 1, :] * ext[pl.ds(3 - k, tile), :]
        o_ref[...] = acc

    return _halo_call(name, body, rows, tile, c, [dpre], 0, True, [w], [(c, F32)], [],
                      [pltpu.VMEM((tile + HALO, c), F32)])[0]


def _s5_build(lam_re, lam_im, log_dt, b_re, b_im, c_re, c_im, d):
    g, p = lam_re.shape
    h = b_re.shape[-1]
    t = S5_T
    dt = jnp.exp(log_dt)[:, None]
    lam = lax.complex(lam_re, lam_im)
    lam_dt = lam * dt
    lam_bar = jnp.exp(lam_dt)
    b_bar = ((lam_bar - 1) / lam)[..., None] * lax.complex(b_re, b_im)
    c_mat = lax.complex(c_re, c_im)
    tau = jnp.arange(t + 1, dtype=F32)
    pw = jnp.exp(lam_dt[:, None, :] * tau[None, :, None])
    cp = c_mat[:, None, :, :] * pw[:, :, None, :]
    cp0_re, cp0_im = jnp.real(cp[:, :t]), jnp.imag(cp[:, :t])
    bb_re, bb_im = jnp.real(b_bar), jnp.imag(b_bar)
    kern = (jnp.einsum("gthp,gpk->gthk", cp0_re, bb_re, precision=HIGHEST)
            - jnp.einsum("gthp,gpk->gthk", cp0_im, bb_im, precision=HIGHEST))
    kt = jnp.transpose(kern, (0, 3, 2, 1))
    v = jnp.concatenate([kt, jnp.zeros_like(kt)], axis=-1)
    tl = jnp.broadcast_to(v[..., None, :], (g, h, h, t, 2 * t)).reshape(g, h, h, 2 * t * t)
    sk = tl[..., :t * (2 * t - 1)].reshape(g, h, h, t, 2 * t - 1)[..., :t]
    toep = jnp.transpose(sk, (0, 3, 1, 4, 2)).reshape(g, t * h, t * h)
    bpow = pw[:, t - 1::-1, :][:, :t]
    be = bpow[:, :, :, None] * b_bar[:, None, :, :]
    be = jnp.transpose(be, (0, 1, 3, 2))
    bend = jnp.concatenate([jnp.real(be), jnp.imag(be)], axis=-1).reshape(g, t * h, 2 * p)
    cp1 = jnp.transpose(cp[:, 1:], (0, 3, 1, 2))
    cpow = jnp.concatenate([jnp.real(cp1), -jnp.imag(cp1)], axis=1).reshape(g, 2 * p, t * h)
    at = pw[:, t, :]
    a1 = jnp.concatenate([jnp.real(at), jnp.real(at)], axis=-1)[:, None, :]
    a2 = jnp.concatenate([-jnp.imag(at), jnp.imag(at)], axis=-1)[:, None, :]
    dtile = jnp.tile(d.reshape(g, 1, h), (1, t, 1)).reshape(g, 1, t * h)
    return toep, bend, cpow, a1, a2, dtile


def _swap_halves(x, axis):
    n = x.shape[axis] // 2
    lo = lax.slice_in_dim(x, 0, n, axis=axis)
    hi = lax.slice_in_dim(x, n, 2 * n, axis=axis)
    return jnp.concatenate([hi, lo], axis=axis)


def _group_spec(shape):
    return pl.BlockSpec((1,) + tuple(shape[1:]), lambda g: (g, 0, 0))


def _s5_core_fwd(name, u, ops):
    toep, bend, cpow, a1, a2, dtile = ops
    g, nc, th = u.shape
    p2 = bend.shape[-1]
    toep_b, bend_b, cpow_b = toep.astype(BF16), bend.astype(BF16), cpow.astype(BF16)
    bend_s = _swap_halves(bend_b, 2)
    a2s = _swap_halves(a2, 2)

    def kern(u_ref, t_ref, b_ref, bs_ref, c_ref, a1_ref, a2_ref, a2s_ref, d_ref, y_ref, sp_ref, x_scr, xs_scr):
        uv = u_ref[0]
        ub = uv.astype(BF16)
        x_scr[...] = _dot(ub, b_ref[0])
        xs_scr[...] = _dot(ub, bs_ref[0])
        a1v, a2v, a2sv = a1_ref[0], a2_ref[0], a2s_ref[0]

        def step(c, carry):
            s, ss = carry
            sp_ref[0, pl.ds(c, 1), :] = s
            s_new = a1v * s + a2v * ss + x_scr[pl.ds(c, 1), :]
            ss_new = a1v * ss + a2sv * s + xs_scr[pl.ds(c, 1), :]
            return s_new, ss_new

        zero = jnp.zeros((1, p2), F32)
        lax.fori_loop(0, nc, step, (zero, zero))
        y_ref[0] = _dot(ub, t_ref[0]) + _dot(sp_ref[0].astype(BF16), c_ref[0]) + uv * d_ref[0]

    ins = [u, toep_b, bend_b, bend_s, cpow_b, a1, a2, a2s, dtile]
    return pl.pallas_call(
        kern, name=name, grid=(g,), in_specs=[_group_spec(a.shape) for a in ins],
        out_specs=[_group_spec((g, nc, th)), _group_spec((g, nc, p2))],
        out_shape=[jax.ShapeDtypeStruct((g, nc, th), F32), jax.ShapeDtypeStruct((g, nc, p2), F32)],
        scratch_shapes=[pltpu.VMEM((nc, p2), F32), pltpu.VMEM((nc, p2), F32)],
        compiler_params=_params("arbitrary"),
    )(*ins)


def _s5_core_bwd(name, u, dy, sprev, ops):
    toep, bend, cpow, a1, a2, dtile = ops
    g, nc, th = u.shape
    p2 = bend.shape[-1]
    toep_b, bend_b, cpow_b = toep.astype(BF16), bend.astype(BF16), cpow.astype(BF16)
    cpow_s = _swap_halves(cpow_b, 1)
    a2s = _swap_halves(a2, 2)

    def kern(u_ref, dy_ref, sp_ref, t_ref, b_ref, c_ref, cs_ref, a1_ref, a2_ref, a2s_ref, d_ref,
             du_ref, dt_ref, db_ref, dc_ref, da1_ref, da2_ref, dd_ref, g_scr, gs_scr, dx_scr):
        uv, dyv = u_ref[0], dy_ref[0]
        ub, dyb = uv.astype(BF16), dyv.astype(BF16)
        dt_ref[0] = _dot(ub, dyb, "tn")
        dd_ref[0] = _colsum(dyv * uv)
        spv = sp_ref[0]
        dc_ref[0] = _dot(spv.astype(BF16), dyb, "tn")
        g_scr[...] = _dot(dyb, c_ref[0], "nt")
        gs_scr[...] = _dot(dyb, cs_ref[0], "nt")
        a1v, a2v, a2sv = a1_ref[0], a2_ref[0], a2s_ref[0]

        def step(k, carry):
            gr, grs, da1, da2 = carry
            c = nc - 1 - k
            dx_scr[pl.ds(c, 1), :] = gr
            s_in = sp_ref[0, pl.ds(c, 1), :]
            da1 = da1 + gr * s_in
            da2 = da2 + grs * s_in
            gr_new = g_scr[pl.ds(c, 1), :] + a1v * gr + a2sv * grs
            grs_new = gs_scr[pl.ds(c, 1), :] + a1v * grs + a2v * gr
            return gr_new, grs_new, da1, da2

        zero = jnp.zeros((1, p2), F32)
        _, _, da1, da2 = lax.fori_loop(0, nc, step, (zero, zero, zero, zero))
        da1_ref[0] = da1
        da2_ref[0] = da2
        dxb = dx_scr[...].astype(BF16)
        db_ref[0] = _dot(ub, dxb, "tn")
        du_ref[0] = _dot(dyb, t_ref[0], "nt") + _dot(dxb, b_ref[0], "nt") + dyv * d_ref[0]

    ins = [u, dy, sprev, toep_b, bend_b, cpow_b, cpow_s, a1, a2, a2s, dtile]
    outs = [(g, nc, th), (g, th, th), (g, th, p2), (g, p2, th), (g, 1, p2), (g, 1, p2), (g, 1, th)]
    return pl.pallas_call(
        kern, name=name, grid=(g,), in_specs=[_group_spec(a.shape) for a in ins],
        out_specs=[_group_spec(s) for s in outs], out_shape=[jax.ShapeDtypeStruct(s, F32) for s in outs],
        scratch_shapes=[pltpu.VMEM((nc, p2), F32), pltpu.VMEM((nc, p2), F32), pltpu.VMEM((nc, p2), F32)],
        compiler_params=_params("arbitrary"),
    )(*ins)


def _s5_to_groups(u):
    rows, w = u.shape
    g = w // S5_H
    nc = rows // S5_T
    return u.reshape(nc, S5_T, g, S5_H).transpose(2, 0, 1, 3).reshape(g, nc, S5_T * S5_H)


def _s5_from_groups(y):
    g, nc, _ = y.shape
    return y.reshape(g, nc, S5_T, S5_H).transpose(1, 2, 0, 3).reshape(nc * S5_T, g * S5_H)


def _softplus(x):
    return jnp.maximum(x, 0.0) + jnp.log(1.0 + jnp.exp(-jnp.abs(x)))


def _ssd_chunk_prep(dtraw_ref, dtb_ref, a_ref, cst, dtt, lastt, n_heads):
    q = M2_Q
    lane = lax.broadcasted_iota(jnp.int32, (q, LANES), 1)
    dt = jnp.where(lane < n_heads, _softplus(dtraw_ref[...] + dtb_ref[...]), 0.0)
    adt = dt * a_ref[...]
    row = lax.broadcasted_iota(jnp.int32, (q, q), 0)
    col = lax.broadcasted_iota(jnp.int32, (q, q), 1)
    cs = _dot(jnp.where(row >= col, 1.0, 0.0), adt, precision=HIGHEST)
    cst[...] = cs.T
    dtt[...] = dt.T
    lastt[...] = jnp.broadcast_to(_colsum(adt), (q, LANES)).T
    return dt


def _pair_tables(cst, dtt, lastt, p):
    q = M2_Q
    out = []
    for hh in (2 * p, 2 * p + 1):
        rc = jnp.broadcast_to(cst[hh:hh + 1, :], (q, q))
        cc = rc.T
        dtc = jnp.broadcast_to(dtt[hh:hh + 1, :], (q, q)).T
        lb = jnp.broadcast_to(lastt[hh:hh + 1, :], (q, q))
        out.append((rc, cc, dtc, lb))
    return out


def _ssd_pair_fwd(x, bm, cm, cb, hs, tabs):
    q = M2_Q
    row = lax.broadcasted_iota(jnp.int32, (q, q), 0)
    col = lax.broadcasted_iota(jnp.int32, (q, q), 1)
    causal = row >= col
    lo = col < M2_P
    slo = row < M2_P
    (rc0, cc0, dtc0, lb0), (rc1, cc1, dtc1, lb1) = tabs
    l0 = jnp.where(causal, jnp.exp(jnp.where(causal, cc0 - rc0, 0.0)), 0.0)
    l1 = jnp.where(causal, jnp.exp(jnp.where(causal, cc1 - rc1, 0.0)), 0.0)
    m0, m1 = cb * l0, cb * l1
    dtp = jnp.where(lo, dtc0, dtc1)
    xdt = x * dtp
    xdt0 = jnp.where(lo, xdt, 0.0)
    xdt1 = jnp.where(lo, 0.0, xdt)
    e = jnp.where(lo, jnp.exp(cc0), jnp.exp(cc1))
    z = _bdot(cm, hs, "nt")
    yoff = z * e
    dec = jnp.where(lo, jnp.exp(lb0 - cc0), jnp.exp(lb1 - cc1))
    xdd = xdt * dec
    cd = jnp.where(slo, jnp.exp(lb0), jnp.exp(lb1))
    return dict(l0=l0, l1=l1, m0=m0, m1=m1, dtp=dtp, xdt=xdt, xdt0=xdt0, xdt1=xdt1, e=e, yoff=yoff,
                dec=dec, xdd=xdd, cd=cd, lo=lo, slo=slo)


def _ssd_fwd(name, xbc, dtraw, dtb, arow, dvec, n_heads):
    rows, c = xbc.shape
    q, n = M2_Q, M2_N
    di = n_heads * M2_P
    n_pairs = n_heads // 2
    ppg = n_pairs // M2_G
    nc = rows // q

    def kern(xbc_ref, dtraw_ref, dtb_ref, a_ref, d_ref, y_ref, prev_ref, state, cst, dtt, lastt):
        @pl.when(pl.program_id(0) == 0)
        def _():
            state[...] = jnp.zeros_like(state)

        _ssd_chunk_prep(dtraw_ref, dtb_ref, a_ref, cst, dtt, lastt, n_heads)
        for p in range(n_pairs):
            gi = p // ppg
            sl = slice(p * LANES, (p + 1) * LANES)
            x = xbc_ref[:, sl]
            bm = xbc_ref[:, di + gi * n:di + (gi + 1) * n]
            cm = xbc_ref[:, di + (M2_G + gi) * n:di + (M2_G + gi + 1) * n]
            if p % ppg == 0:
                cb = _bdot(cm, bm, "nt")
            hs = state[p]
            f = _ssd_pair_fwd(x, bm, cm, cb, hs, _pair_tables(cst, dtt, lastt, p))
            ydiag = _bdot(f["m0"], f["xdt0"]) + _bdot(f["m1"], f["xdt1"])
            y_ref[:, sl] = ydiag + f["yoff"] + d_ref[:, sl] * x
            prev_ref[0, p] = hs
            state[p] = f["cd"] * hs + _bdot(f["xdd"], bm, "tn")

    def whole(a):
        return pl.BlockSpec(a.shape, lambda i: (0, 0))

    return pl.pallas_call(
        kern, name=name, grid=(nc,),
        in_specs=[pl.BlockSpec((q, c), lambda i: (i, 0)), pl.BlockSpec((q, LANES), lambda i: (i, 0)),
                  whole(dtb), whole(arow), whole(dvec)],
        out_specs=[pl.BlockSpec((q, di), lambda i: (i, 0)),
                   pl.BlockSpec((1, n_pairs, 2 * M2_P, n), lambda i: (i, 0, 0, 0))],
        out_shape=[jax.ShapeDtypeStruct((rows, di), F32),
                   jax.ShapeDtypeStruct((nc, n_pairs, 2 * M2_P, n), F32)],
        scratch_shapes=[pltpu.VMEM((n_pairs, 2 * M2_P, n), F32), pltpu.VMEM((LANES, q), F32),
                        pltpu.VMEM((LANES, q), F32), pltpu.VMEM((LANES, q), F32)],
        compiler_params=_params("arbitrary"),
    )(xbc, dtraw, dtb, arow, dvec)


def _ssd_bwd(name, xbc, dtraw, dy, prev, dtb, arow, dvec, seg, n_heads):
    rows, c = xbc.shape
    q, n = M2_Q, M2_N
    di = n_heads * M2_P
    n_pairs = n_heads // 2
    ppg = n_pairs // M2_G
    nc = rows // q

    def kern(xbc_ref, dtraw_ref, dy_ref, prev_ref, dtb_ref, a_ref, d_ref, seg_ref,
             dxbc_ref, ddt_ref, da_ref, ddtb_ref, dd_ref,
             dstate, cst, dtt, lastt, dcst, wx, colterm, ddfull):
        step = pl.program_id(0)

        @pl.when(step == 0)
        def _():
            dstate[...] = jnp.zeros_like(dstate)
            ddfull[...] = jnp.zeros_like(ddfull)
            da_ref[...] = jnp.zeros_like(da_ref)
            ddtb_ref[...] = jnp.zeros_like(ddtb_ref)
            dd_ref[...] = jnp.zeros_like(dd_ref)

        dt = _ssd_chunk_prep(dtraw_ref, dtb_ref, a_ref, cst, dtt, lastt, n_heads)
        dcst[...] = jnp.zeros_like(dcst)
        lane_q = lax.broadcasted_iota(jnp.int32, (1, q), 1)
        last_hot = jnp.where(lane_q == q - 1, 1.0, 0.0)

        def total(v):
            return jnp.sum(jnp.sum(v, axis=1, keepdims=True), axis=0, keepdims=True)

        for gi in range(M2_G):
            bm = xbc_ref[:, di + gi * n:di + (gi + 1) * n]
            cm = xbc_ref[:, di + (M2_G + gi) * n:di + (M2_G + gi + 1) * n]
            cb = _bdot(cm, bm, "nt")
            dcb = jnp.zeros((q, q), F32)
            dbm = jnp.zeros((q, n), F32)
            dcm = jnp.zeros((q, n), F32)
            for p in range(gi * ppg, (gi + 1) * ppg):
                sl = slice(p * LANES, (p + 1) * LANES)
                x = xbc_ref[:, sl]
                dyp = dy_ref[:, sl]
                hs = prev_ref[0, p]
                ds = dstate[p]
                f = _ssd_pair_fwd(x, bm, cm, cb, hs, _pair_tables(cst, dtt, lastt, p))
                lo, slo = f["lo"], f["slo"]
                ddfull[:, sl] += _colsum(dyp * x)
                dy0 = jnp.where(lo, dyp, 0.0)
                dy1 = jnp.where(lo, 0.0, dyp)
                dm0 = _bdot(dyp, f["xdt0"], "nt")
                dm1 = _bdot(dyp, f["xdt1"], "nt")
                dxdt = _bdot(f["m0"], dy0, "tn") + _bdot(f["m1"], dy1, "tn")
                dcb = dcb + dm0 * f["l0"] + dm1 * f["l1"]
                w0, w1 = dm0 * f["m0"], dm1 * f["m1"]
                dz = dyp * f["e"]
                dcm = dcm + _bdot(dz, hs)
                dhs = _bdot(dz, cm, "tn") + f["cd"] * ds
                tot = ds * hs * f["cd"]
                dxdd = _bdot(bm, ds, "nt")
                dbm = dbm + _bdot(f["xdd"], ds)
                ee = dxdd * f["xdd"]
                colterm[:, sl] = dyp * f["yoff"] - ee
                dxdt = dxdt + dxdd * f["dec"]
                t_all = total(tot)
                t_lo = total(jnp.where(slo, tot, 0.0))
                e_all = total(ee)
                e_lo = total(jnp.where(lo, ee, 0.0))
                dlast0 = t_lo + e_lo
                dlast1 = (t_all - t_lo) + (e_all - e_lo)
                dcst[2 * p:2 * p + 1, :] = _colsum(w0.T - w0) + dlast0 * last_hot
                dcst[2 * p + 1:2 * p + 2, :] = _colsum(w1.T - w1) + dlast1 * last_hot
                dxbc_ref[:, sl] = d_ref[:, sl] * dyp + dxdt * f["dtp"]
                wx[:, sl] = dxdt * x
                dstate[p] = dhs
            dcm = dcm + _bdot(dcb, bm)
            dbm = dbm + _bdot(dcb, cm, "tn")
            dxbc_ref[:, di + gi * n:di + (gi + 1) * n] = dbm
            dxbc_ref[:, di + (M2_G + gi) * n:di + (M2_G + gi + 1) * n] = dcm

        segv = seg_ref[...]
        dcs = _dot(colterm[...], segv, precision=HIGHEST) + dcst[...].T
        row = lax.broadcasted_iota(jnp.int32, (q, q), 0)
        col = lax.broadcasted_iota(jnp.int32, (q, q), 1)
        ddelta = _dot(jnp.where(col >= row, 1.0, 0.0), dcs, precision=HIGHEST)
        ddt = _dot(wx[...], segv, precision=HIGHEST) + ddelta * a_ref[...]
        da_ref[...] += _colsum(ddelta * dt)
        lane = lax.broadcasted_iota(jnp.int32, (q, LANES), 1)
        ddtraw = jnp.where(lane < n_heads, ddt * _sigmoid(dtraw_ref[...] + dtb_ref[...]), 0.0)
        ddt_ref[...] = ddtraw
        ddtb_ref[...] += _colsum(ddtraw)

        @pl.when(step == nc - 1)
        def _():
            dd_ref[...] = _dot(jnp.broadcast_to(ddfull[...], (8, di)), segv, precision=HIGHEST)

    def whole(a):
        return pl.BlockSpec(a.shape, lambda i: (0, 0))

    def rev(i):
        return nc - 1 - i

    acc = jax.ShapeDtypeStruct((1, LANES), F32)
    acc_spec = pl.BlockSpec((1, LANES), lambda i: (0, 0))
    acc8 = jax.ShapeDtypeStruct((8, LANES), F32)
    acc8_spec = pl.BlockSpec((8, LANES), lambda i: (0, 0))
    return pl.pallas_call(
        kern, name=name, grid=(nc,),
        in_specs=[pl.BlockSpec((q, c), lambda i: (rev(i), 0)), pl.BlockSpec((q, LANES), lambda i: (rev(i), 0)),
                  pl.BlockSpec((q, di), lambda i: (rev(i), 0)),
                  pl.BlockSpec((1, n_pairs, 2 * M2_P, n), lambda i: (rev(i), 0, 0, 0)),
                  whole(dtb), whole(arow), whole(dvec), whole(seg)],
        out_specs=[pl.BlockSpec((q, c), lambda i: (rev(i), 0)), pl.BlockSpec((q, LANES), lambda i: (rev(i), 0)),
                   acc_spec, acc_spec, acc8_spec],
        out_shape=[jax.ShapeDtypeStruct((rows, c), F32), jax.ShapeDtypeStruct((rows, LANES), F32), acc, acc, acc8],
        scratch_shapes=[pltpu.VMEM((n_pairs, 2 * M2_P, n), F32), pltpu.VMEM((LANES, q), F32),
                        pltpu.VMEM((LANES, q), F32), pltpu.VMEM((LANES, q), F32), pltpu.VMEM((LANES, q), F32),
                        pltpu.VMEM((q, di), F32), pltpu.VMEM((q, di), F32), pltpu.VMEM((1, di), F32)],
        compiler_params=_params("arbitrary"),
    )(xbc, dtraw, dy, prev, dtb, arow, dvec, seg)


S5_PARAM_NAMES = ("s5_lambda_re", "s5_lambda_im", "s5_log_dt", "s5_b_re", "s5_b_im", "s5_c_re", "s5_c_im", "s5_d")


def _row(v):
    return v.reshape(1, -1)


def _s5_layer_fwd(tag, h, w, j):
    u = _matmul(tag + "_win", h, w["s5_w_in"][j])
    params = [w[k][j] for k in S5_PARAM_NAMES]
    ops, build_vjp = jax.vjp(_s5_build, *params)
    yg, sprev = _s5_core_fwd(tag + "_core", _s5_to_groups(u), ops)
    yy = _s5_from_groups(yg)
    gl = _gelu_fwd(tag + "_gelu", yy)
    ab = _matmul(tag + "_wglu", gl, w["s5_w_glu"][j])
    out = _glu_fwd(tag + "_glu", ab, _row(w["s5_b_glu"][j]))
    return out, dict(u=u, ops=ops, build_vjp=build_vjp, sprev=sprev, yy=yy, gl=gl, ab=ab)


def _s5_layer_bwd(tag, dy, h, sv, w, j):
    dab, db_glu = _glu_bwd(tag + "_glu_b", sv["ab"], _row(w["s5_b_glu"][j]), dy)
    dw_glu = _matmul(tag + "_dwglu", sv["gl"], dab, "tn")
    dgl = _matmul(tag + "_dgl", dab, w["s5_w_glu"][j], "nt")
    dyy = _gelu_bwd(tag + "_gelu_b", sv["yy"], dgl)
    dug, dtoep, dbend, dcpow, da1, da2s, ddtile = _s5_core_bwd(
        tag + "_core_b", _s5_to_groups(sv["u"]), _s5_to_groups(dyy), sv["sprev"], sv["ops"])
    dparams = sv["build_vjp"]((dtoep, dbend, dcpow, da1, _swap_halves(da2s, 2), ddtile))
    du = _s5_from_groups(dug)
    grads = dict(zip(S5_PARAM_NAMES, dparams))
    grads["s5_w_in"] = _matmul(tag + "_dwin", h, du, "tn")
    grads["s5_w_glu"] = dw_glu
    grads["s5_b_glu"] = db_glu.reshape(-1)
    dh = _matmul(tag + "_dh", du, w["s5_w_in"][j], "nt")
    return dh, grads


def _ssd_consts(w, j, d_model):
    di = 2 * d_model
    heads = di // M2_P

    def pad_row(v):
        return jnp.zeros((1, LANES), F32).at[0, :heads].set(v)

    a = -jnp.exp(w["m2_a_log"][j])
    seg = (jnp.arange(di)[:, None] // M2_P == jnp.arange(LANES)[None, :]).astype(F32)
    w_in = w["m2_w_in"][j]
    conv_dim = di + 2 * M2_G * M2_N
    w_dt = jnp.zeros((d_model, LANES), w_in.dtype).at[:, :heads].set(w_in[:, di + conv_dim:])
    return dict(di=di, heads=heads, conv_dim=conv_dim, a=a, arow=pad_row(a), dtb=pad_row(w["m2_dt_bias"][j]),
                dvec=_row(jnp.repeat(w["m2_d"][j], M2_P)), seg=seg,
                w_z=w_in[:, :di], w_xbc=w_in[:, di:di + conv_dim], w_dt=w_dt,
                conv_w=w["m2_conv_w"][j], conv_b=_row(w["m2_conv_b"][j]), norm_g=_row(w["m2_norm_g"][j]))


def _ssd_layer_fwd(tag, h, w, j):
    k = _ssd_consts(w, j, h.shape[1])
    z = _matmul(tag + "_wz", h, k["w_z"])
    xbc_pre = _matmul(tag + "_wxbc", h, k["w_xbc"])
    dtraw = _matmul(tag + "_wdt", h, k["w_dt"])
    xbc = _conv_fwd(tag + "_conv", xbc_pre, k["conv_w"], k["conv_b"])
    y, prev = _ssd_fwd(tag + "_core", xbc, dtraw, k["dtb"], k["arow"], k["dvec"], k["heads"])
    yn = _gatenorm_fwd(tag + "_gn", y, z, k["norm_g"])
    out = _matmul(tag + "_wout", yn, w["m2_w_out"][j])
    return out, dict(k=k, z=z, xbc_pre=xbc_pre, dtraw=dtraw, xbc=xbc, y=y, prev=prev, yn=yn)


def _ssd_layer_bwd(tag, dy, h, sv, w, j):
    k = sv["k"]
    heads = k["heads"]
    grads = {"m2_w_out": _matmul(tag + "_dwout", sv["yn"], dy, "tn")}
    dyn = _matmul(tag + "_dyn", dy, w["m2_w_out"][j], "nt")
    dyssd, dz, dng = _gatenorm_bwd(tag + "_gn_b", sv["y"], sv["z"], k["norm_g"], dyn)
    dxbc, ddtraw, da, ddtb, dd = _ssd_bwd(tag + "_core_b", sv["xbc"], sv["dtraw"], dyssd, sv["prev"],
                                          k["dtb"], k["arow"], k["dvec"], k["seg"], heads)
    dpre, dcw, dcb = _conv_bwd_pre(tag + "_conv_b1", sv["xbc_pre"], dxbc, k["conv_w"], k["conv_b"])
    dxbc_pre = _conv_bwd_in(tag + "_conv_b2", dpre, k["conv_w"])
    dw_z = _matmul(tag + "_dwz", h, dz, "tn")
    dw_xbc = _matmul(tag + "_dwxbc", h, dxbc_pre, "tn")
    dw_dt = _matmul(tag + "_dwdt", h, ddtraw, "tn")
    dh = _matmul(tag + "_dh1", dz, k["w_z"], "nt")
    dh = _matmul(tag + "_dh2", dxbc_pre, k["w_xbc"], "nt", addin=dh)
    dh = _matmul(tag + "_dh3", ddtraw, k["w_dt"], "nt", addin=dh)
    grads["m2_w_in"] = jnp.concatenate([dw_z, dw_xbc, dw_dt[:, :heads]], axis=1)
    grads["m2_conv_w"] = dcw
    grads["m2_conv_b"] = dcb.reshape(-1)
    grads["m2_dt_bias"] = ddtb[0, :heads]
    grads["m2_a_log"] = da[0, :heads] * k["a"]
    grads["m2_d"] = dd[0, :heads]
    grads["m2_norm_g"] = dng.reshape(-1)
    return dh, grads


def _layer_fwd(li, x, mod, w):
    tag = "L%d" % li
    sh1, sc1, g1, sh2, sc2, g2 = mod
    j = li // 2
    h = _normmod_fwd(tag + "_nm1", x, _row(w["norm_mix_g"][li]), sh1, sc1)
    if li % 2 == 0:
        y, mix = _s5_layer_fwd(tag + "_s5", h, w, j)
    else:
        y, mix = _ssd_layer_fwd(tag + "_m2", h, w, j)
    x1 = _resid_fwd(tag + "_res1", x, y, g1)
    h2 = _normmod_fwd(tag + "_nm2", x1, _row(w["norm_mlp_g"][li]), sh2, sc2)
    r = _matmul(tag + "_w1", h2, w["mlp_w1"][li], relu=True, out_dtype=BF16)
    m = _matmul(tag + "_w2", r, w["mlp_w2"][li], square_a=True)
    x2 = _resid_fwd(tag + "_res2", x1, m, g2)
    return x2, dict(x=x, h=h, y=y, mix=mix, x1=x1, h2=h2, r=r, m=m)


def _layer_bwd(li, dx2, sv, mod, w):
    tag = "L%d" % li
    sh1, sc1, g1, sh2, sc2, g2 = mod
    j = li // 2
    dm, dg2 = _resid_bwd(tag + "_res2_b", dx2, sv["m"], g2)
    grads = {"mlp_w2": _matmul(tag + "_dw2", sv["r"], dm, "tn", square_a=True)}
    dr = _matmul(tag + "_dr", dm, w["mlp_w2"][li], "nt", out_dtype=BF16, mul2=sv["r"])
    grads["mlp_w1"] = _matmul(tag + "_dw1", sv["h2"], dr, "tn")
    dh2 = _matmul(tag + "_dh2", dr, w["mlp_w1"][li], "nt")
    dx1, dgm, dsh2, dsc2 = _normmod_bwd(tag + "_nm2_b", sv["x1"], _row(w["norm_mlp_g"][li]), sh2, sc2, dh2, dx2)
    dy, dg1 = _resid_bwd(tag + "_res1_b", dx1, sv["y"], g1)
    if li % 2 == 0:
        dh, mix_grads = _s5_layer_bwd(tag + "_s5", dy, sv["h"], sv["mix"], w, j)
    else:
        dh, mix_grads = _ssd_layer_bwd(tag + "_m2", dy, sv["h"], sv["mix"], w, j)
    dx, dgx, dsh1, dsc1 = _normmod_bwd(tag + "_nm1_b", sv["x"], _row(w["norm_mix_g"][li]), sh1, sc1, dh, dx1)
    grads["norm_mix_g"] = dgx.reshape(-1)
    grads["norm_mlp_g"] = dgm.reshape(-1)
    dmod = jnp.concatenate([dsh1, dsc1, dg1, dsh2, dsc2, dg2], axis=1)
    return dx, grads, mix_grads, dmod


STACKED = ("norm_mix_g", "norm_mlp_g", "mlp_w1", "mlp_w2")
S5_NAMES = S5_PARAM_NAMES + ("s5_w_in", "s5_w_glu", "s5_b_glu")
M2_NAMES = ("m2_w_in", "m2_conv_w", "m2_conv_b", "m2_dt_bias", "m2_a_log", "m2_d", "m2_norm_g", "m2_w_out")


def _local_step(x, target, mods, w):
    depth = w["norm_mix_g"].shape[0]
    d = x.shape[1]
    saved = []
    mod_rows = []
    for li in range(depth):
        mod = [mods[li:li + 1, i * d:(i + 1) * d] for i in range(N_MOD)]
        mod_rows.append(mod)
        x, sv = _layer_fwd(li, x, mod, w)
        saved.append(sv)
    dx, dgf, loss = _loss_head("loss_head", x, target, _row(w["final_norm_g"]))
    per_layer = {k: [None] * depth for k in STACKED}
    s5 = {k: [None] * (depth - depth // 2) for k in S5_NAMES}
    m2 = {k: [None] * (depth // 2) for k in M2_NAMES}
    dmods = [None] * depth
    for li in reversed(range(depth)):
        dx, grads, mix_grads, dmods[li] = _layer_bwd(li, dx, saved[li], mod_rows[li], w)
        for k, v in grads.items():
            per_layer[k][li] = v
        for k, v in mix_grads.items():
            (s5 if li % 2 == 0 else m2)[k][li // 2] = v
    out = {k: jnp.stack(v) for k, v in {**per_layer, **s5, **m2}.items()}
    out["final_norm_g"] = dgf.reshape(-1)
    return loss, dx, out, jnp.concatenate(dmods, axis=0)


ANY = pl.BlockSpec(memory_space=pl.ANY)
N_DEV = 8
N_CHIP = 4


def _coords():
    return lax.axis_index("x"), lax.axis_index("y"), lax.axis_index("c")


def _allgather8(name, block):
    r, wd = block.shape

    def body(x_ref, out_ref, send_sems, recv_sems, local_sem):
        x, y, c = _coords()
        me, sibling = (x, y, c), (x, y, 1 - c)
        chips = [(1 - x, y), (x, 1 - y), (1 - x, 1 - y)]

        def slot(px, py, pc):
            return out_ref.at[4 * px + 2 * py + pc]

        def copy(k, blk, to, src=None):
            return pltpu.make_async_remote_copy(
                src_ref=slot(*blk) if src is None else src, dst_ref=slot(*blk),
                send_sem=send_sems.at[k], recv_sem=recv_sems.at[k], device_id=to, device_id_type=MESH)

        mine = pltpu.make_async_copy(x_ref, slot(*me), local_sem)
        mine.start()
        first = [copy(0, me, sibling, src=x_ref)]
        first += [copy(1 + j, me, (*chip, c), src=x_ref) for j, chip in enumerate(chips)]
        for cp in first:
            cp.start()
        passed = [copy(4 + j, (*chip, c), sibling) for j, chip in enumerate(chips)]
        for j, chip in enumerate(chips):
            copy(1 + j, (*chip, c), me).wait_recv()
            passed[j].start()
        copy(0, sibling, me).wait_recv()
        for j, chip in enumerate(chips):
            copy(4 + j, (*chip, 1 - c), me).wait_recv()
        for cp in first + passed:
            cp.wait_send()
        mine.wait()

    return pl.pallas_call(
        body, name=name, in_specs=[ANY], out_specs=ANY,
        out_shape=jax.ShapeDtypeStruct((N_DEV, r, wd), block.dtype),
        scratch_shapes=[pltpu.SemaphoreType.DMA((7,)), pltpu.SemaphoreType.DMA((7,)), pltpu.SemaphoreType.DMA],
    )(block)


def _exchange4(name, pieces, same):
    r, wd = pieces.shape[-2:]

    def body(x_ref, out_ref, send_sems, recv_sems, local_sem):
        x, y, c = _coords()
        my_chip = 2 * x + y
        chips = [(1 - x, y), (x, 1 - y), (1 - x, 1 - y)]

        def src(k):
            return x_ref if same else x_ref.at[k]

        mine = pltpu.make_async_copy(src(my_chip), out_ref.at[my_chip], local_sem)
        mine.start()
        sends = []
        for j, (px, py) in enumerate(chips):
            cp = pltpu.make_async_remote_copy(
                src_ref=src(2 * px + py), dst_ref=out_ref.at[my_chip],
                send_sem=send_sems.at[j], recv_sem=recv_sems.at[j], device_id=(px, py, c), device_id_type=MESH)
            cp.start()
            sends.append(cp)
        for j, (px, py) in enumerate(chips):
            pltpu.make_async_remote_copy(
                src_ref=src(my_chip), dst_ref=out_ref.at[2 * px + py],
                send_sem=send_sems.at[j], recv_sem=recv_sems.at[j], device_id=(px, py, c),
                device_id_type=MESH).wait_recv()
        for cp in sends:
            cp.wait_send()
        mine.wait()

    return pl.pallas_call(
        body, name=name, in_specs=[ANY], out_specs=ANY,
        out_shape=jax.ShapeDtypeStruct((N_CHIP, r, wd), pieces.dtype),
        scratch_shapes=[pltpu.SemaphoreType.DMA((3,)), pltpu.SemaphoreType.DMA((3,)), pltpu.SemaphoreType.DMA],
    )(pieces)


def _swap_sibling(name, block):
    def body(x_ref, out_ref, send_sem, recv_sem):
        x, y, c = _coords()
        cp = pltpu.make_async_remote_copy(src_ref=x_ref, dst_ref=out_ref, send_sem=send_sem, recv_sem=recv_sem,
                                          device_id=(x, y, 1 - c), device_id_type=MESH)
        cp.start()
        cp.wait()

    return pl.pallas_call(
        body, name=name, in_specs=[ANY], out_specs=ANY, out_shape=jax.ShapeDtypeStruct(block.shape, block.dtype),
        scratch_shapes=[pltpu.SemaphoreType.DMA, pltpu.SemaphoreType.DMA],
    )(block)


def _sum_slots(name, stacked):
    n, r, wd = stacked.shape
    tile = min(FLAT_ROWS, r)

    def kern(x_ref, o_ref):
        acc = x_ref[0]
        for s in range(1, n):
            acc = acc + x_ref[s]
        o_ref[...] = acc

    return pl.pallas_call(
        kern, name=name, grid=(r // tile,), in_specs=[pl.BlockSpec((n, tile, wd), lambda i: (0, i, 0))],
        out_specs=pl.BlockSpec((tile, wd), lambda i: (i, 0)), out_shape=jax.ShapeDtypeStruct((r, wd), F32),
        compiler_params=_params("parallel"),
    )(stacked)


def _adamw(name, w, m, v, g, g2=None):
    r, wd = w.shape
    grads = [g] if g2 is None else [g, g2]
    c1 = 1.0 - ADAM_B1 ** ADAM_STEP
    c2 = 1.0 - ADAM_B2 ** ADAM_STEP

    def body(i, *refs):
        w_ref, m_ref, v_ref = refs[:3]
        g_refs = refs[3:3 + len(grads)]
        go_ref, d_ref, mo_ref, vo_ref = refs[3 + len(grads):]
        gv = g_refs[0][...]
        if g2 is not None:
            gv = gv + g_refs[1][...]
        mn = ADAM_B1 * m_ref[...] + (1.0 - ADAM_B1) * gv
        vn = ADAM_B2 * v_ref[...] + (1.0 - ADAM_B2) * (gv * gv)
        go_ref[...] = gv
        mo_ref[...] = mn
        vo_ref[...] = vn
        d_ref[...] = -ADAM_LR * ((mn / c1) / (jnp.sqrt(vn / c2) + ADAM_EPS) + ADAM_WD * w_ref[...])

    return _rowcall(name, body, r, FLAT_ROWS, [w, m, v] + grads, [], [(wd, F32)] * 4, [])


FLAT_BLOCK = FLAT_ROWS * FLAT_W


def _pack(arrays, dtype):
    flat = jnp.concatenate([a.reshape(-1).astype(dtype) for a in arrays])
    pad = (-flat.shape[0]) % FLAT_BLOCK
    return jnp.pad(flat, (0, pad)).reshape(-1, FLAT_W)


def _unpack(buf, shapes):
    flat = buf.reshape(-1)
    out, off = [], 0
    for s in shapes:
        n = math.prod(s)
        out.append(flat[off:off + n].reshape(s))
        off += n
    return out


SHARDED_BIG = {"mlp_w1": 2, "mlp_w2": 1, "s5_w_in": 1, "s5_w_glu": 2, "m2_w_in": 2, "m2_w_out": 1}
SHARDED_SMALL = {"m2_conv_w": 2, "m2_conv_b": 1, "m2_norm_g": 1}
REPLICATED = ("ada_b", "norm_mix_g", "norm_mlp_g", "s5_lambda_re", "s5_lambda_im", "s5_log_dt", "s5_b_re",
              "s5_b_im", "s5_c_re", "s5_c_im", "s5_d", "s5_b_glu", "m2_dt_bias", "m2_a_log", "m2_d", "final_norm_g")
WEIGHT_NAMES = ("ada_w", "ada_b", "norm_mix_g", "norm_mlp_g", "mlp_w1", "mlp_w2", "s5_w_in", "s5_lambda_re",
                "s5_lambda_im", "s5_log_dt", "s5_b_re", "s5_b_im", "s5_c_re", "s5_c_im", "s5_d", "s5_w_glu",
                "s5_b_glu", "m2_w_in", "m2_conv_w", "m2_conv_b", "m2_dt_bias", "m2_a_log", "m2_d", "m2_norm_g",
                "m2_w_out", "final_norm_g")


def _gather_weights(name, local, names_axes, dtype):
    names = list(names_axes)
    got = _exchange4(name, _pack([local[k] for k in names], dtype), same=True)
    per_chip = [_unpack(got[j], [local[k].shape for k in names]) for j in range(N_CHIP)]
    return {k: jnp.concatenate([per_chip[j][i] for j in range(N_CHIP)], axis=names_axes[k])
            for i, k in enumerate(names)}


def _chip_slice(a, chip, axis):
    size = a.shape[axis] // N_CHIP
    return lax.slice_in_dim(a, chip * size, (chip + 1) * size, axis=axis)


def kernel(x, c, ada_w, ada_b, norm_mix_g, norm_mlp_g, mlp_w1, mlp_w2, s5_w_in, s5_lambda_re, s5_lambda_im, s5_log_dt, s5_b_re, s5_b_im, s5_c_re, s5_c_im, s5_d, s5_w_glu, s5_b_glu, m2_w_in, m2_conv_w, m2_conv_b, m2_dt_bias, m2_a_log, m2_d, m2_norm_g, m2_w_out, final_norm_g, loss_target, m_ada_w, m_ada_b, m_norm_mix_g, m_norm_mlp_g, m_mlp_w1, m_mlp_w2, m_s5_w_in, m_s5_lambda_re, m_s5_lambda_im, m_s5_log_dt, m_s5_b_re, m_s5_b_im, m_s5_c_re, m_s5_c_im, m_s5_d, m_s5_w_glu, m_s5_b_glu, m_m2_w_in, m_m2_conv_w, m_m2_conv_b, m_m2_dt_bias, m_m2_a_log, m_m2_d, m_m2_norm_g, m_m2_w_out, m_final_norm_g, v_ada_w, v_ada_b, v_norm_mix_g, v_norm_mlp_g, v_mlp_w1, v_mlp_w2, v_s5_w_in, v_s5_lambda_re, v_s5_lambda_im, v_s5_log_dt, v_s5_b_re, v_s5_b_im, v_s5_c_re, v_s5_c_im, v_s5_d, v_s5_w_glu, v_s5_b_glu, v_m2_w_in, v_m2_conv_w, v_m2_conv_b, v_m2_dt_bias, v_m2_a_log, v_m2_d, v_m2_norm_g, v_m2_w_out, v_final_norm_g):
    args = locals()
    local = {k: args[k] for k in WEIGHT_NAMES}
    mom_m = {k: args["m_" + k] for k in WEIGHT_NAMES}
    mom_v = {k: args["v_" + k] for k in WEIGHT_NAMES}
    depth, d = norm_mix_g.shape
    xi, yi, ci = _coords()
    my_chip = 2 * xi + yi
    my_dev = 2 * my_chip + ci

    cond = jax.nn.silu(c).reshape(-1, LANES)
    cond_all = _allgather8("ag_cond", cond).reshape(N_DEV, d)
    cond_pad = jnp.zeros((LANES, d), F32).at[:N_DEV].set(cond_all)
    mod_cols = ada_w.shape[2]
    mod_part = jnp.stack([_matmul("ada_%d" % i, cond_pad, ada_w[i])[:N_DEV] for i in range(depth)])
    mod_all = _allgather8("ag_mod", mod_part.reshape(-1, LANES)).reshape(N_CHIP, 2, depth, N_DEV, mod_cols)[:, 0]
    mod_mine = lax.dynamic_index_in_dim(mod_all, my_dev, axis=2, keepdims=False)
    mods = jnp.transpose(mod_mine, (1, 0, 2)).reshape(depth, N_CHIP * mod_cols) + ada_b

    w = {k: local[k] for k in REPLICATED}
    w.update(_gather_weights("ag_w_big", local, SHARDED_BIG, BF16))
    w.update(_gather_weights("ag_w_small", local, SHARDED_SMALL, F32))

    loss_row, dx, grads, dmods = _local_step(x[0], loss_target[0], mods, w)
    grads["ada_b"] = dmods

    rep_shapes = [grads[k].shape for k in REPLICATED]
    rep_all = _allgather8("ag_grep", _pack([grads[k] for k in REPLICATED], F32))
    rep_sum = _sum_slots("sum_grep", rep_all)
    dmods_all = rep_all.reshape(N_DEV, -1)[:, :dmods.size].reshape(N_DEV, depth, N_CHIP * mod_cols)

    dm_mine = lax.dynamic_slice_in_dim(dmods_all, my_chip * mod_cols, mod_cols, axis=2)
    dm_pad = jnp.zeros((LANES, depth, mod_cols), F32).at[:N_DEV].set(dm_mine)
    g_ada_w = jnp.stack([_matmul("dada_%d" % i, cond_pad, dm_pad[:, i], "tn") for i in range(depth)])

    sh_names = list(SHARDED_BIG) + list(SHARDED_SMALL)
    sh_axes = {**SHARDED_BIG, **SHARDED_SMALL}
    pieces = jnp.stack([_pack([_chip_slice(grads[k], j, sh_axes[k]) for k in sh_names], F32)
                        for j in range(N_CHIP)])
    landed = _exchange4("rs_grads", pieces, same=False)
    part = _sum_slots("sum_gsh", landed)
    other = _swap_sibling("swap_gsh", part)

    out_g, out_d, out_m, out_v = {}, {}, {}, {}
    sh_shapes = [local[k].shape for k in sh_names]
    res = _adamw("adam_sh", _pack([local[k] for k in sh_names], F32), _pack([mom_m[k] for k in sh_names], F32),
                 _pack([mom_v[k] for k in sh_names], F32), part, other)
    for dst, buf in zip((out_g, out_d, out_m, out_v), res):
        dst.update(zip(sh_names, _unpack(buf, sh_shapes)))
    res = _adamw("adam_rep", _pack([local[k] for k in REPLICATED], F32), _pack([mom_m[k] for k in REPLICATED], F32),
                 _pack([mom_v[k] for k in REPLICATED], F32), rep_sum)
    for dst, buf in zip((out_g, out_d, out_m, out_v), res):
        dst.update(zip(REPLICATED, _unpack(buf, rep_shapes)))
    flat2 = (-1, mod_cols)
    res = _adamw("adam_ada", ada_w.reshape(flat2), m_ada_w.reshape(flat2), v_ada_w.reshape(flat2),
                 g_ada_w.reshape(flat2))
    for dst, buf in zip((out_g, out_d, out_m, out_v), res):
        dst["ada_w"] = buf.reshape(ada_w.shape)

    loss = lax.psum(loss_row[0, 0], ("x", "y", "c"))
    outs = [loss, dx[None]]
    for dst in (out_g, out_d, out_m, out_v):
        outs += [dst[k] for k in WEIGHT_NAMES]
    return tuple(outs)
```

```python
import functools
import math

import jax
import jax.numpy as jnp
from jax import lax
from jax.experimental import pallas as pl
from jax.experimental.pallas import tpu as pltpu

F32 = jnp.float32
BF16 = jnp.bfloat16
HIGHEST = lax.Precision.HIGHEST

NORM_EPS = 1e-5
N_MOD = 6
S5_H, S5_P, S5_T = 16, 64, 64
M2_P, M2_N, M2_G, M2_Q, M2_K = 64, 128, 4, 128, 4
LANES = 128
ADAM_LR, ADAM_B1, ADAM_B2, ADAM_EPS, ADAM_WD, ADAM_STEP = 0.001, 0.9, 0.999, 1e-08, 0.01, 10
VMEM_LIMIT_BYTES = 48 * 1024 * 1024
ROW_TILE = 256
FLAT_W = 1024
FLAT_ROWS = 256
MESH = pl.DeviceIdType.MESH


def _params(*sem):
    return pltpu.CompilerParams(dimension_semantics=sem, vmem_limit_bytes=VMEM_LIMIT_BYTES)


def _dot(a, b, dn="nn", precision=None):
    dims = {"nn": ((1,), (0,)), "nt": ((1,), (1,)), "tn": ((0,), (0,))}[dn]
    return lax.dot_general(a, b, (dims, ((), ())), preferred_element_type=F32, precision=precision)


def _bdot(a, b, dn="nn"):
    return _dot(a.astype(BF16), b.astype(BF16), dn)


def _sigmoid(x):
    return jax.nn.sigmoid(x)


def _colsum(x):
    return jnp.sum(x, axis=0, keepdims=True)


def _pick_tile(dim, want):
    if dim <= want:
        return dim
    for t in range(want - want % LANES, 0, -LANES):
        if dim % t == 0:
            return t
    raise ValueError((dim, want))


def _matmul(name, a, b, mode="nn", out_dtype=F32, relu=False, square_a=False, mul2=None, addin=None,
            tm=1024, tn=1024, tk=512):
    if mode == "nn":
        (m, k), (k2, n) = a.shape, b.shape
    elif mode == "nt":
        (m, k), (n, k2) = a.shape, b.shape
    else:
        (k, m), (k2, n) = a.shape, b.shape
    assert k == k2, (name, a.shape, b.shape)
    tm, tn, tk = _pick_tile(m, tm), _pick_tile(n, tn), _pick_tile(k, tk)
    assert m % tm == 0 and n % tn == 0 and k % tk == 0, (name, a.shape, b.shape)
    nk = k // tk
    extras = [e for e in (mul2, addin) if e is not None]

    def kern(*refs):
        a_ref, b_ref = refs[:2]
        e_refs = refs[2:2 + len(extras)]
        o_ref, acc = refs[2 + len(extras)], refs[3 + len(extras)]
        kk = pl.program_id(2)

        @pl.when(kk == 0)
        def _():
            acc[...] = jnp.zeros_like(acc)

        av = a_ref[...]
        if square_a:
            av = av * av
        acc[...] += _bdot(av, b_ref[...], mode)

        @pl.when(kk == nk - 1)
        def _():
            r = acc[...]
            if relu:
                r = jnp.maximum(r, 0.0)
            idx = 0
            if mul2 is not None:
                r = r * (2.0 * e_refs[idx][...].astype(F32))
                idx += 1
            if addin is not None:
                r = r + e_refs[idx][...].astype(F32)
            o_ref[...] = r.astype(out_dtype)

    if mode == "tn":
        a_spec = pl.BlockSpec((tk, tm), lambda i, j, kk: (kk, i))
    else:
        a_spec = pl.BlockSpec((tm, tk), lambda i, j, kk: (i, kk))
    if mode == "nt":
        b_spec = pl.BlockSpec((tn, tk), lambda i, j, kk: (j, kk))
    else:
        b_spec = pl.BlockSpec((tk, tn), lambda i, j, kk: (kk, j))
    o_spec = pl.BlockSpec((tm, tn), lambda i, j, kk: (i, j))
    return pl.pallas_call(
        kern, name=name, grid=(m // tm, n // tn, nk),
        in_specs=[a_spec, b_spec] + [o_spec] * len(extras), out_specs=o_spec,
        out_shape=jax.ShapeDtypeStruct((m, n), out_dtype),
        scratch_shapes=[pltpu.VMEM((tm, tn), F32)],
        compiler_params=_params("parallel", "parallel", "arbitrary"),
    )(a, b, *extras)


def _rowcall(name, body, rows, tile, row_ins, small_ins, row_outs, acc_outs):
    tile = min(tile, rows)
    assert rows % tile == 0, (name, rows, tile)
    n_in = len(row_ins) + len(small_ins)

    def kern(*refs):
        i = pl.program_id(0)
        accs = refs[n_in + len(row_outs):]

        @pl.when(i == 0)
        def _():
            for acc in accs:
                acc[...] = jnp.zeros_like(acc)

        body(i, *refs)

    def whole(shape):
        return pl.BlockSpec(shape, lambda i, nd=len(shape): (0,) * nd)

    in_specs = [pl.BlockSpec((tile, a.shape[1]), lambda i: (i, 0)) for a in row_ins]
    in_specs += [whole(a.shape) for a in small_ins]
    out_specs = [pl.BlockSpec((tile, w), lambda i: (i, 0)) for (w, _) in row_outs]
    out_specs += [whole(s) for s in acc_outs]
    out_shape = [jax.ShapeDtypeStruct((rows, w), dt) for (w, dt) in row_outs]
    out_shape += [jax.ShapeDtypeStruct(s, F32) for s in acc_outs]
    return pl.pallas_call(
        kern, name=name, grid=(rows // tile,), in_specs=in_specs, out_specs=out_specs, out_shape=out_shape,
        compiler_params=_params("arbitrary"),
    )(*row_ins, *small_ins)


def _rms(x):
    r = lax.rsqrt(jnp.mean(x * x, axis=-1, keepdims=True) + NORM_EPS)
    return x * r, r


def _rms_bwd(dxhat, xhat, r):
    return r * (dxhat - xhat * jnp.mean(dxhat * xhat, axis=-1, keepdims=True))


def _normmod_fwd(name, x, g, sh, sc):
    def body(i, x_ref, g_ref, sh_ref, sc_ref, o_ref):
        xhat, _ = _rms(x_ref[...])
        o_ref[...] = ((xhat * g_ref[...]) * (1.0 + sc_ref[...]) + sh_ref[...]).astype(BF16)

    return _rowcall(name, body, x.shape[0], ROW_TILE, [x], [g, sh, sc], [(x.shape[1], BF16)], [])[0]


def _normmod_bwd(name, x, g, sh, sc, dh, dx_pass):
    d = x.shape[1]

    def body(i, x_ref, dh_ref, dxp_ref, g_ref, sh_ref, sc_ref, dx_ref, dg_ref, dsh_ref, dsc_ref):
        xhat, r = _rms(x_ref[...])
        dh = dh_ref[...].astype(F32)
        gv = g_ref[...]
        dn = dh * (1.0 + sc_ref[...])
        dsc_ref[...] += _colsum(dh * (xhat * gv))
        dsh_ref[...] += _colsum(dh)
        dg_ref[...] += _colsum(dn * xhat)
        dx_ref[...] = dxp_ref[...] + _rms_bwd(dn * gv, xhat, r)

    return _rowcall(name, body, x.shape[0], ROW_TILE, [x, dh, dx_pass], [g, sh, sc], [(d, F32)],
                    [(1, d), (1, d), (1, d)])


def _resid_fwd(name, x, y, g):
    def body(i, x_ref, y_ref, g_ref, o_ref):
        o_ref[...] = x_ref[...] + g_ref[...] * y_ref[...]

    return _rowcall(name, body, x.shape[0], ROW_TILE, [x, y], [g], [(x.shape[1], F32)], [])[0]


def _resid_bwd(name, dxo, y, g):
    d = y.shape[1]

    def body(i, dx_ref, y_ref, g_ref, dy_ref, dg_ref):
        dxo_v = dx_ref[...]
        dy_ref[...] = (g_ref[...] * dxo_v).astype(F32)
        dg_ref[...] += _colsum(dxo_v * y_ref[...])

    return _rowcall(name, body, y.shape[0], ROW_TILE, [dxo, y], [g], [(d, F32)], [(1, d)])


GELU_K = math.sqrt(2.0 / math.pi)
GELU_C = 0.044715


def _gelu_fwd(name, y, u, skip):
    def body(i, y_ref, u_ref, s_ref, o_ref):
        v = y_ref[...] + s_ref[...] * u_ref[...]
        t = jnp.tanh(GELU_K * (v + GELU_C * (v * v * v)))
        o_ref[...] = (0.5 * v * (1.0 + t)).astype(BF16)

    return _rowcall(name, body, y.shape[0], ROW_TILE, [y, u], [skip], [(y.shape[1], BF16)], [])[0]


def _gelu_bwd(name, y, u, skip, dgl):
    d = y.shape[1]

    def body(i, y_ref, u_ref, d_ref, s_ref, o_ref, ds_ref):
        uv = u_ref[...]
        v = y_ref[...] + s_ref[...] * uv
        t = jnp.tanh(GELU_K * (v + GELU_C * (v * v * v)))
        dv = d_ref[...] * (0.5 * (1.0 + t) + 0.5 * v * (1.0 - t * t) * (GELU_K * (1.0 + 3.0 * GELU_C * v * v)))
        o_ref[...] = dv
        ds_ref[...] += _colsum(dv * uv)

    return _rowcall(name, body, y.shape[0], ROW_TILE, [y, u, dgl], [skip], [(d, F32)], [(1, d)])


def _axpy(name, a, b, scale):
    def body(i, a_ref, b_ref, s_ref, o_ref):
        o_ref[...] = (a_ref[...] + s_ref[...] * b_ref[...]).astype(BF16)

    return _rowcall(name, body, a.shape[0], ROW_TILE, [a, b], [scale], [(a.shape[1], BF16)], [])[0]


def _glu_fwd(name, ab, bias):
    d = ab.shape[1] // 2

    def body(i, ab_ref, b_ref, o_ref):
        v = ab_ref[:, :d] + b_ref[:, :d]
        gt = ab_ref[:, d:] + b_ref[:, d:]
        o_ref[...] = v * _sigmoid(gt)

    return _rowcall(name, body, ab.shape[0], ROW_TILE, [ab], [bias], [(d, F32)], [])[0]


def _glu_bwd(name, ab, bias, dout):
    d = ab.shape[1] // 2

    def body(i, ab_ref, do_ref, b_ref, dab_ref, db_ref):
        v = ab_ref[:, :d] + b_ref[:, :d]
        s = _sigmoid(ab_ref[:, d:] + b_ref[:, d:])
        do = do_ref[...]
        dv = do * s
        dg = do * v * (s * (1.0 - s))
        dab_ref[:, :d] = dv.astype(BF16)
        dab_ref[:, d:] = dg.astype(BF16)
        db_ref[:, :d] += _colsum(dv)
        db_ref[:, d:] += _colsum(dg)

    return _rowcall(name, body, ab.shape[0], ROW_TILE, [ab, dout], [bias], [(2 * d, BF16)], [(1, 2 * d)])


def _gatenorm_fwd(name, y, z, ng):
    di = y.shape[1]
    gw = di // M2_G

    def body(i, y_ref, z_ref, g_ref, o_ref):
        for gi in range(M2_G):
            sl = slice(gi * gw, (gi + 1) * gw)
            zz = z_ref[:, sl]
            y2 = y_ref[:, sl] * (zz * _sigmoid(zz))
            yh, _ = _rms(y2)
            o_ref[:, sl] = (yh * g_ref[:, sl]).astype(BF16)

    return _rowcall(name, body, y.shape[0], ROW_TILE, [y, z], [ng], [(di, BF16)], [])[0]


def _gatenorm_bwd(name, y, z, ng, dyn):
    di = y.shape[1]
    gw = di // M2_G

    def body(i, y_ref, z_ref, d_ref, g_ref, dy_ref, dz_ref, dg_ref):
        for gi in range(M2_G):
            sl = slice(gi * gw, (gi + 1) * gw)
            zz = z_ref[:, sl]
            yy = y_ref[:, sl]
            s = _sigmoid(zz)
            sz = zz * s
            yh, r = _rms(yy * sz)
            dn = d_ref[:, sl]
            dg_ref[:, sl] += _colsum(dn * yh)
            dy2 = _rms_bwd(dn * g_ref[:, sl], yh, r)
            dy_ref[:, sl] = dy2 * sz
            dz_ref[:, sl] = dy2 * yy * (s * (1.0 + zz * (1.0 - s)))

    return _rowcall(name, body, y.shape[0], ROW_TILE, [y, z, dyn], [ng], [(di, F32), (di, F32)], [(1, di)])


def _loss_head(name, x, target, g):
    d = x.shape[1]

    def body(i, x_ref, t_ref, g_ref, dx_ref, dg_ref, loss_ref):
        xhat, r = _rms(x_ref[...])
        gv = g_ref[...]
        err = xhat * gv - t_ref[...]
        per_row = jnp.sum(err * err, axis=-1, keepdims=True) * (0.5 / d)
        loss_ref[...] += jnp.broadcast_to(_colsum(per_row), loss_ref.shape)
        dy = err * (1.0 / d)
        dg_ref[...] += _colsum(dy * xhat)
        dx_ref[...] = _rms_bwd(dy * gv, xhat, r)

    return _rowcall(name, body, x.shape[0], ROW_TILE, [x, target], [g], [(d, F32)], [(1, d), (1, LANES)])


HALO = 8


def _conv_taps(ext_ref, w_ref, b_ref, tile):
    acc = b_ref[...] + w_ref[0:1, :] * ext_ref[pl.ds(HALO - 3, tile), :]
    for k in range(1, M2_K):
        acc = acc + w_ref[k:k + 1, :] * ext_ref[pl.ds(HALO - 3 + k, tile), :]
    return acc


def _halo_call(name, body, rows, tile, width, mains, halo_of, halo_next, smalls, row_outs, acc_outs, scratch):
    tile = min(tile, rows)
    nb = tile // HALO
    last = rows // HALO - 1
    n_in = len(mains) + 1 + len(smalls)

    def kern(*refs):
        i = pl.program_id(0)
        accs = refs[n_in + len(row_outs):n_in + len(row_outs) + len(acc_outs)]

        @pl.when(i == 0)
        def _():
            for acc in accs:
                acc[...] = jnp.zeros_like(acc)

        body(i, *refs)

    def whole(shape):
        return pl.BlockSpec(shape, lambda i, nd=len(shape): (0,) * nd)

    if halo_next:
        halo_spec = pl.BlockSpec((HALO, width), lambda i: (jnp.minimum((i + 1) * nb, last), 0))
    else:
        halo_spec = pl.BlockSpec((HALO, width), lambda i: (jnp.maximum(i * nb - 1, 0), 0))
    in_specs = [pl.BlockSpec((tile, a.shape[1]), lambda i: (i, 0)) for a in mains] + [halo_spec]
    in_specs += [whole(a.shape) for a in smalls]
    out_specs = [pl.BlockSpec((tile, w), lambda i: (i, 0)) for (w, _) in row_outs] + [whole(s) for s in acc_outs]
    out_shape = [jax.ShapeDtypeStruct((rows, w), dt) for (w, dt) in row_outs]
    out_shape += [jax.ShapeDtypeStruct(s, F32) for s in acc_outs]
    return pl.pallas_call(
        kern, name=name, grid=(rows // tile,), in_specs=in_specs, out_specs=out_specs, out_shape=out_shape,
        scratch_shapes=scratch, compiler_params=_params("arbitrary"),
    )(*mains, mains[halo_of], *smalls)


CONV_TILE = 128


def _conv_fwd(name, xin, w, b):
    rows, c = xin.shape
    tile = min(CONV_TILE, rows)

    def body(i, x_ref, h_ref, w_ref, b_ref, o_ref, ext):
        ext[0:HALO, :] = jnp.where(i == 0, 0.0, h_ref[...])
        ext[HALO:, :] = x_ref[...]
        pre = _conv_taps(ext, w_ref, b_ref, tile)
        o_ref[...] = pre * _sigmoid(pre)

    return _halo_call(name, body, rows, tile, c, [xin], 0, False, [w, b], [(c, F32)], [],
                      [pltpu.VMEM((tile + HALO, c), F32)])[0]


def _conv_bwd_pre(name, xin, dout, w, b):
    rows, c = xin.shape
    tile = min(CONV_TILE, rows)

    def body(i, x_ref, d_ref, h_ref, w_ref, b_ref, dp_ref, dw_ref, db_ref, ext):
        ext[0:HALO, :] = jnp.where(i == 0, 0.0, h_ref[...])
        ext[HALO:, :] = x_ref[...]
        pre = _conv_taps(ext, w_ref, b_ref, tile)
        s = _sigmoid(pre)
        dp = d_ref[...] * (s * (1.0 + pre * (1.0 - s)))
        dp_ref[...] = dp
        db_ref[...] += _colsum(dp)
        for k in range(M2_K):
            dw_ref[k:k + 1, :] += _colsum(dp * ext[pl.ds(HALO - 3 + k, tile), :])

    return _halo_call(name, body, rows, tile, c, [xin, dout], 0, False, [w, b], [(c, F32)], [(M2_K, c), (1, c)],
                      [pltpu.VMEM((tile + HALO, c), F32)])


def _conv_bwd_in(name, dpre, w):
    rows, c = dpre.shape
    tile = min(CONV_TILE, rows)
    n_tiles = rows // tile

    def body(i, d_ref, h_ref, w_ref, o_ref, ext):
        ext[0:tile, :] = d_ref[...]
        ext[tile:, :] = jnp.where(i == n_tiles - 1, 0.0, h_ref[...])
        acc = w_ref[0:1, :] * ext[pl.ds(3, tile), :]
        for k in range(1, M2_K):
            acc = acc + w_ref[k:k + 1, :] * ext[pl.ds(3 - k, tile), :]
        o_ref[...] = acc

    return _halo_call(name, body, rows, tile, c, [dpre], 0, True, [w], [(c, F32)], [],
                      [pltpu.VMEM((tile + HALO, c), F32)])[0]


def _s5_build(lam_re, lam_im, log_dt, b_re, b_im, c_re, c_im):
    g, p = lam_re.shape
    h = b_re.shape[-1]
    t = S5_T
    dt = jnp.exp(log_dt)[:, None]
    lam = lax.complex(lam_re, lam_im)
    lam_dt = lam * dt
    lam_bar = jnp.exp(lam_dt)
    b_bar = ((lam_bar - 1) / lam)[..., None] * lax.complex(b_re, b_im)
    c_mat = lax.complex(c_re, c_im)
    tau = jnp.arange(t + 1, dtype=F32)
    pw = jnp.exp(lam_dt[:, None, :] * tau[None, :, None])
    cp = c_mat[:, None, :, :] * pw[:, :, None, :]
    cp0_re, cp0_im = jnp.real(cp[:, :t]), jnp.imag(cp[:, :t])
    bb_re, bb_im = jnp.real(b_bar), jnp.imag(b_bar)
    kern = (jnp.einsum("gthp,gpk->gthk", cp0_re, bb_re, precision=HIGHEST)
            - jnp.einsum("gthp,gpk->gthk", cp0_im, bb_im, precision=HIGHEST))
    kc = jnp.transpose(kern, (0, 3, 1, 2)).reshape(g, h, t * h)
    bpow = pw[:, t - 1::-1, :][:, :t]
    be = bpow[:, :, :, None] * b_bar[:, None, :, :]
    be = jnp.transpose(be, (0, 3, 1, 2))
    bend = jnp.concatenate([jnp.real(be), jnp.imag(be)], axis=-1).reshape(g, h * t, 2 * p)
    cp1 = jnp.transpose(cp[:, 1:], (0, 3, 1, 2))
    cpow = jnp.concatenate([jnp.real(cp1), -jnp.imag(cp1)], axis=1).reshape(g, 2 * p, t * h)
    at = pw[:, t, :]
    a1 = jnp.concatenate([jnp.real(at), jnp.real(at)], axis=-1)[:, None, :]
    a2 = jnp.concatenate([-jnp.imag(at), jnp.imag(at)], axis=-1)[:, None, :]
    return kc, bend, cpow, a1, a2


def _swap_halves(x, axis):
    n = x.shape[axis] // 2
    lo = lax.slice_in_dim(x, 0, n, axis=axis)
    hi = lax.slice_in_dim(x, n, 2 * n, axis=axis)
    return jnp.concatenate([hi, lo], axis=axis)


def _group_spec(shape):
    return pl.BlockSpec((1,) + tuple(shape[1:]), lambda g: (g, 0, 0))


def _s5_expand_toeplitz(kc_ref, ext, toep):
    t, th = S5_T, S5_T * S5_H
    ext[:, th:] = jnp.zeros((t, th), F32)
    for hin in range(S5_H):
        ext[:, :th] = jnp.broadcast_to(kc_ref[0, hin:hin + 1, :], (t, th))
        rolled = pltpu.roll(ext[...], 0, 1, stride=S5_H, stride_axis=0)
        toep[hin * t:(hin + 1) * t, :] = rolled[:, :th].astype(BF16)


def _s5_core_fwd(name, u, ops):
    kc, bend, cpow, a1, a2 = ops
    g, nc, th = u.shape
    p2 = bend.shape[-1]
    bend_b, cpow_b = bend.astype(BF16), cpow.astype(BF16)
    bend_s = _swap_halves(bend_b, 2)
    a2s = _swap_halves(a2, 2)

    def kern(u_ref, k_ref, b_ref, bs_ref, c_ref, a1_ref, a2_ref, a2s_ref, y_ref, sp_ref, x_scr, xs_scr, ext, toep):
        _s5_expand_toeplitz(k_ref, ext, toep)
        ub = u_ref[0].astype(BF16)
        x_scr[...] = _dot(ub, b_ref[0])
        xs_scr[...] = _dot(ub, bs_ref[0])
        a1v, a2v, a2sv = a1_ref[0], a2_ref[0], a2s_ref[0]

        def step(c, carry):
            s, ss = carry
            sp_ref[0, pl.ds(c, 1), :] = s
            s_new = a1v * s + a2v * ss + x_scr[pl.ds(c, 1), :]
            ss_new = a1v * ss + a2sv * s + xs_scr[pl.ds(c, 1), :]
            return s_new, ss_new

        zero = jnp.zeros((1, p2), F32)
        lax.fori_loop(0, nc, step, (zero, zero))
        y_ref[0] = _dot(ub, toep[...]) + _dot(sp_ref[0].astype(BF16), c_ref[0])

    ins = [u, kc, bend_b, bend_s, cpow_b, a1, a2, a2s]
    return pl.pallas_call(
        kern, name=name, grid=(g,), in_specs=[_group_spec(a.shape) for a in ins],
        out_specs=[_group_spec((g, nc, th)), _group_spec((g, nc, p2))],
        out_shape=[jax.ShapeDtypeStruct((g, nc, th), F32), jax.ShapeDtypeStruct((g, nc, p2), F32)],
        scratch_shapes=[pltpu.VMEM((nc, p2), F32), pltpu.VMEM((nc, p2), F32),
                        pltpu.VMEM((S5_T, 2 * th), F32), pltpu.VMEM((th, th), BF16)],
        compiler_params=_params("arbitrary"),
    )(*ins)


def _s5_core_bwd(name, u, dy, sprev, ops):
    kc, bend, cpow, a1, a2 = ops
    g, nc, th = u.shape
    t = S5_T
    p2 = bend.shape[-1]
    bend_b, cpow_b = bend.astype(BF16), cpow.astype(BF16)
    cpow_s = _swap_halves(cpow_b, 1)
    a2s = _swap_halves(a2, 2)
    idx = jnp.arange(th)
    flip = (idx[:, None] // t == idx[None, :] // t) & (idx[:, None] % t == t - 1 - idx[None, :] % t)
    flip = flip.astype(BF16)

    def kern(u_ref, dy_ref, sp_ref, k_ref, b_ref, c_ref, cs_ref, a1_ref, a2_ref, a2s_ref, f_ref,
             du_ref, dk_ref, db_ref, dc_ref, da1_ref, da2_ref, g_scr, gs_scr, dx_scr, ext, toep, dtoep):
        _s5_expand_toeplitz(k_ref, ext, toep)
        ub, dyb = u_ref[0].astype(BF16), dy_ref[0].astype(BF16)
        dtoep[...] = _dot(_dot(ub, f_ref[...]).astype(BF16), dyb, "tn")
        for hin in range(S5_H):
            ext[:, :th] = dtoep[hin * t:(hin + 1) * t, :]
            rolled = pltpu.roll(ext[...], 0, 1, stride=S5_H, stride_axis=0)
            rolled = pltpu.roll(rolled, 2 * th - S5_H * (t - 1), 1)
            dk_ref[0, hin:hin + 1, :] = _colsum(rolled)[:, :th]
        spv = sp_ref[0]
        dc_ref[0] = _dot(spv.astype(BF16), dyb, "tn")
        g_scr[...] = _dot(dyb, c_ref[0], "nt")
        gs_scr[...] = _dot(dyb, cs_ref[0], "nt")
        a1v, a2v, a2sv = a1_ref[0], a2_ref[0], a2s_ref[0]

        def step(k, carry):
            gr, grs, da1, da2 = carry
            c = nc - 1 - k
            dx_scr[pl.ds(c, 1), :] = gr
            s_in = sp_ref[0, pl.ds(c, 1), :]
            da1 = da1 + gr * s_in
            da2 = da2 + grs * s_in
            gr_new = g_scr[pl.ds(c, 1), :] + a1v * gr + a2sv * grs
            grs_new = gs_scr[pl.ds(c, 1), :] + a1v * grs + a2v * gr
            return gr_new, grs_new, da1, da2

        zero = jnp.zeros((1, p2), F32)
        _, _, da1, da2 = lax.fori_loop(0, nc, step, (zero, zero, zero, zero))
        da1_ref[0] = da1
        da2_ref[0] = da2
        dxb = dx_scr[...].astype(BF16)
        db_ref[0] = _dot(ub, dxb, "tn")
        du_ref[0] = _dot(dyb, toep[...], "nt") + _dot(dxb, b_ref[0], "nt")

    ins = [u, dy, sprev, kc, bend_b, cpow_b, cpow_s, a1, a2, a2s]
    outs = [(g, nc, th), (g, S5_H, th), (g, th, p2), (g, p2, th), (g, 1, p2), (g, 1, p2)]
    return pl.pallas_call(
        kern, name=name, grid=(g,),
        in_specs=[_group_spec(a.shape) for a in ins] + [pl.BlockSpec((th, th), lambda gi: (0, 0))],
        out_specs=[_group_spec(s) for s in outs], out_shape=[jax.ShapeDtypeStruct(s, F32) for s in outs],
        scratch_shapes=[pltpu.VMEM((nc, p2), F32), pltpu.VMEM((nc, p2), F32), pltpu.VMEM((nc, p2), F32),
                        pltpu.VMEM((t, 2 * th), F32), pltpu.VMEM((th, th), BF16), pltpu.VMEM((th, th), F32)],
        compiler_params=_params("arbitrary"),
    )(*ins, flip)


def _s5_to_groups(u, channel_major):
    rows, w = u.shape
    g = w // S5_H
    nc = rows // S5_T
    perm = (2, 0, 3, 1) if channel_major else (2, 0, 1, 3)
    return u.reshape(nc, S5_T, g, S5_H).transpose(perm).reshape(g, nc, S5_T * S5_H)


def _s5_from_groups(y, channel_major):
    g, nc, _ = y.shape
    if channel_major:
        return y.reshape(g, nc, S5_H, S5_T).transpose(1, 3, 0, 2).reshape(nc * S5_T, g * S5_H)
    return y.reshape(g, nc, S5_T, S5_H).transpose(1, 2, 0, 3).reshape(nc * S5_T, g * S5_H)


def _softplus(x):
    return jnp.maximum(x, 0.0) + jnp.log(1.0 + jnp.exp(-jnp.abs(x)))


def _ssd_chunk_prep(dtraw_ref, dtb_ref, a_ref, cst, dtt, lastt, n_heads):
    q = M2_Q
    lane = lax.broadcasted_iota(jnp.int32, (q, LANES), 1)
    dt = jnp.where(lane < n_heads, _softplus(dtraw_ref[...] + dtb_ref[...]), 0.0)
    adt = dt * a_ref[...]
    row = lax.broadcasted_iota(jnp.int32, (q, q), 0)
    col = lax.broadcasted_iota(jnp.int32, (q, q), 1)
    cs = _dot(jnp.where(row >= col, 1.0, 0.0), adt, precision=HIGHEST)
    cst[...] = cs.T
    dtt[...] = dt.T
    lastt[...] = jnp.broadcast_to(_colsum(adt), (q, LANES)).T
    return dt


def _pair_tables(cst, dtt, lastt, p):
    q = M2_Q
    out = []
    for hh in (2 * p, 2 * p + 1):
        rc = jnp.broadcast_to(cst[hh:hh + 1, :], (q, q))
        cc = rc.T
        dtc = jnp.broadcast_to(dtt[hh:hh + 1, :], (q, q)).T
        lb = jnp.broadcast_to(lastt[hh:hh + 1, :], (q, q))
        out.append((rc, cc, dtc, lb))
    return out


def _ssd_pair_fwd(x, bm, cm, cb, hs, tabs):
    q = M2_Q
    row = lax.broadcasted_iota(jnp.int32, (q, q), 0)
    col = lax.broadcasted_iota(jnp.int32, (q, q), 1)
    causal = row >= col
    lo = col < M2_P
    slo = row < M2_P
    (rc0, cc0, dtc0, lb0), (rc1, cc1, dtc1, lb1) = tabs
    l0 = jnp.where(causal, jnp.exp(jnp.where(causal, cc0 - rc0, 0.0)), 0.0)
    l1 = jnp.where(causal, jnp.exp(jnp.where(causal, cc1 - rc1, 0.0)), 0.0)
    m0, m1 = cb * l0, cb * l1
    dtp = jnp.where(lo, dtc0, dtc1)
    xdt = x * dtp
    xdt0 = jnp.where(lo, xdt, 0.0)
    xdt1 = jnp.where(lo, 0.0, xdt)
    e = jnp.where(lo, jnp.exp(cc0), jnp.exp(cc1))
    z = _bdot(cm, hs, "nt")
    yoff = z * e
    dec = jnp.where(lo, jnp.exp(lb0 - cc0), jnp.exp(lb1 - cc1))
    xdd = xdt * dec
    cd = jnp.where(slo, jnp.exp(lb0), jnp.exp(lb1))
    return dict(l0=l0, l1=l1, m0=m0, m1=m1, dtp=dtp, xdt=xdt, xdt0=xdt0, xdt1=xdt1, e=e, yoff=yoff,
                dec=dec, xdd=xdd, cd=cd, lo=lo, slo=slo)


def _ssd_fwd(name, xbc, dtraw, dtb, arow, dvec, n_heads):
    rows, c = xbc.shape
    q, n = M2_Q, M2_N
    di = n_heads * M2_P
    n_pairs = n_heads // 2
    ppg = n_pairs // M2_G
    nc = rows // q

    def kern(xbc_ref, dtraw_ref, dtb_ref, a_ref, d_ref, y_ref, prev_ref, state, cst, dtt, lastt):
        @pl.when(pl.program_id(0) == 0)
        def _():
            state[...] = jnp.zeros_like(state)

        _ssd_chunk_prep(dtraw_ref, dtb_ref, a_ref, cst, dtt, lastt, n_heads)
        for p in range(n_pairs):
            gi = p // ppg
            sl = slice(p * LANES, (p + 1) * LANES)
            x = xbc_ref[:, sl]
            bm = xbc_ref[:, di + gi * n:di + (gi + 1) * n]
            cm = xbc_ref[:, di + (M2_G + gi) * n:di + (M2_G + gi + 1) * n]
            if p % ppg == 0:
                cb = _bdot(cm, bm, "nt")
            hs = state[p]
            f = _ssd_pair_fwd(x, bm, cm, cb, hs, _pair_tables(cst, dtt, lastt, p))
            ydiag = _bdot(f["m0"], f["xdt0"]) + _bdot(f["m1"], f["xdt1"])
            y_ref[:, sl] = ydiag + f["yoff"] + d_ref[:, sl] * x
            prev_ref[0, p] = hs
            state[p] = f["cd"] * hs + _bdot(f["xdd"], bm, "tn")

    def whole(a):
        return pl.BlockSpec(a.shape, lambda i: (0, 0))

    return pl.pallas_call(
        kern, name=name, grid=(nc,),
        in_specs=[pl.BlockSpec((q, c), lambda i: (i, 0)), pl.BlockSpec((q, LANES), lambda i: (i, 0)),
                  whole(dtb), whole(arow), whole(dvec)],
        out_specs=[pl.BlockSpec((q, di), lambda i: (i, 0)),
                   pl.BlockSpec((1, n_pairs, 2 * M2_P, n), lambda i: (i, 0, 0, 0))],
        out_shape=[jax.ShapeDtypeStruct((rows, di), F32),
                   jax.ShapeDtypeStruct((nc, n_pairs, 2 * M2_P, n), F32)],
        scratch_shapes=[pltpu.VMEM((n_pairs, 2 * M2_P, n), F32), pltpu.VMEM((LANES, q), F32),
                        pltpu.VMEM((LANES, q), F32), pltpu.VMEM((LANES, q), F32)],
        compiler_params=_params("arbitrary"),
    )(xbc, dtraw, dtb, arow, dvec)


def _ssd_bwd(name, xbc, dtraw, dy, prev, dtb, arow, dvec, seg, n_heads):
    rows, c = xbc.shape
    q, n = M2_Q, M2_N
    di = n_heads * M2_P
    n_pairs = n_heads // 2
    ppg = n_pairs // M2_G
    nc = rows // q

    def kern(xbc_ref, dtraw_ref, dy_ref, prev_ref, dtb_ref, a_ref, d_ref, seg_ref,
             dxbc_ref, ddt_ref, da_ref, ddtb_ref, dd_ref,
             dstate, cst, dtt, lastt, dcst, wx, colterm, ddfull):
        step = pl.program_id(0)

        @pl.when(step == 0)
        def _():
            dstate[...] = jnp.zeros_like(dstate)
            ddfull[...] = jnp.zeros_like(ddfull)
            da_ref[...] = jnp.zeros_like(da_ref)
            ddtb_ref[...] = jnp.zeros_like(ddtb_ref)
            dd_ref[...] = jnp.zeros_like(dd_ref)

        dt = _ssd_chunk_prep(dtraw_ref, dtb_ref, a_ref, cst, dtt, lastt, n_heads)
        dcst[...] = jnp.zeros_like(dcst)
        lane_q = lax.broadcasted_iota(jnp.int32, (1, q), 1)
        last_hot = jnp.where(lane_q == q - 1, 1.0, 0.0)

        def total(v):
            return jnp.sum(jnp.sum(v, axis=1, keepdims=True), axis=0, keepdims=True)

        for gi in range(M2_G):
            bm = xbc_ref[:, di + gi * n:di + (gi + 1) * n]
            cm = xbc_ref[:, di + (M2_G + gi) * n:di + (M2_G + gi + 1) * n]
            cb = _bdot(cm, bm, "nt")
            dcb = jnp.zeros((q, q), F32)
            dbm = jnp.zeros((q, n), F32)
            dcm = jnp.zeros((q, n), F32)
            for p in range(gi * ppg, (gi + 1) * ppg):
                sl = slice(p * LANES, (p + 1) * LANES)
                x = xbc_ref[:, sl]
                dyp = dy_ref[:, sl]
                hs = prev_ref[0, p]
                ds = dstate[p]
                f = _ssd_pair_fwd(x, bm, cm, cb, hs, _pair_tables(cst, dtt, lastt, p))
                lo, slo = f["lo"], f["slo"]
                ddfull[:, sl] += _colsum(dyp * x)
                dy0 = jnp.where(lo, dyp, 0.0)
                dy1 = jnp.where(lo, 0.0, dyp)
                dm0 = _bdot(dyp, f["xdt0"], "nt")
                dm1 = _bdot(dyp, f["xdt1"], "nt")
                dxdt = _bdot(f["m0"], dy0, "tn") + _bdot(f["m1"], dy1, "tn")
                dcb = dcb + dm0 * f["l0"] + dm1 * f["l1"]
                w0, w1 = dm0 * f["m0"], dm1 * f["m1"]
                dz = dyp * f["e"]
                dcm = dcm + _bdot(dz, hs)
                dhs = _bdot(dz, cm, "tn") + f["cd"] * ds
                tot = ds * hs * f["cd"]
                dxdd = _bdot(bm, ds, "nt")
                dbm = dbm + _bdot(f["xdd"], ds)
                ee = dxdd * f["xdd"]
                colterm[:, sl] = dyp * f["yoff"] - ee
                dxdt = dxdt + dxdd * f["dec"]
                t_all = total(tot)
                t_lo = total(jnp.where(slo, tot, 0.0))
                e_all = total(ee)
                e_lo = total(jnp.where(lo, ee, 0.0))
                dlast0 = t_lo + e_lo
                dlast1 = (t_all - t_lo) + (e_all - e_lo)
                dcst[2 * p:2 * p + 1, :] = _colsum(w0.T - w0) + dlast0 * last_hot
                dcst[2 * p + 1:2 * p + 2, :] = _colsum(w1.T - w1) + dlast1 * last_hot
                dxbc_ref[:, sl] = d_ref[:, sl] * dyp + dxdt * f["dtp"]
                wx[:, sl] = dxdt * x
                dstate[p] = dhs
            dcm = dcm + _bdot(dcb, bm)
            dbm = dbm + _bdot(dcb, cm, "tn")
            dxbc_ref[:, di + gi * n:di + (gi + 1) * n] = dbm
            dxbc_ref[:, di + (M2_G + gi) * n:di + (M2_G + gi + 1) * n] = dcm

        segv = seg_ref[...]
        dcs = _dot(colterm[...], segv, precision=HIGHEST) + dcst[...].T
        row = lax.broadcasted_iota(jnp.int32, (q, q), 0)
        col = lax.broadcasted_iota(jnp.int32, (q, q), 1)
        ddelta = _dot(jnp.where(col >= row, 1.0, 0.0), dcs, precision=HIGHEST)
        ddt = _dot(wx[...], segv, precision=HIGHEST) + ddelta * a_ref[...]
        da_ref[...] += _colsum(ddelta * dt)
        lane = lax.broadcasted_iota(jnp.int32, (q, LANES), 1)
        ddtraw = jnp.where(lane < n_heads, ddt * _sigmoid(dtraw_ref[...] + dtb_ref[...]), 0.0)
        ddt_ref[...] = ddtraw
        ddtb_ref[...] += _colsum(ddtraw)

        @pl.when(step == nc - 1)
        def _():
            dd_ref[...] = _dot(jnp.broadcast_to(ddfull[...], (8, di)), segv, precision=HIGHEST)

    def whole(a):
        return pl.BlockSpec(a.shape, lambda i: (0, 0))

    def rev(i):
        return nc - 1 - i

    acc = jax.ShapeDtypeStruct((1, LANES), F32)
    acc_spec = pl.BlockSpec((1, LANES), lambda i: (0, 0))
    acc8 = jax.ShapeDtypeStruct((8, LANES), F32)
    acc8_spec = pl.BlockSpec((8, LANES), lambda i: (0, 0))
    return pl.pallas_call(
        kern, name=name, grid=(nc,),
        in_specs=[pl.BlockSpec((q, c), lambda i: (rev(i), 0)), pl.BlockSpec((q, LANES), lambda i: (rev(i), 0)),
                  pl.BlockSpec((q, di), lambda i: (rev(i), 0)),
                  pl.BlockSpec((1, n_pairs, 2 * M2_P, n), lambda i: (rev(i), 0, 0, 0)),
                  whole(dtb), whole(arow), whole(dvec), whole(seg)],
        out_specs=[pl.BlockSpec((q, c), lambda i: (rev(i), 0)), pl.BlockSpec((q, LANES), lambda i: (rev(i), 0)),
                   acc_spec, acc_spec, acc8_spec],
        out_shape=[jax.ShapeDtypeStruct((rows, c), F32), jax.ShapeDtypeStruct((rows, LANES), F32), acc, acc, acc8],
        scratch_shapes=[pltpu.VMEM((n_pairs, 2 * M2_P, n), F32), pltpu.VMEM((LANES, q), F32),
                        pltpu.VMEM((LANES, q), F32), pltpu.VMEM((LANES, q), F32), pltpu.VMEM((LANES, q), F32),
                        pltpu.VMEM((q, di), F32), pltpu.VMEM((q, di), F32), pltpu.VMEM((1, di), F32)],
        compiler_params=_params("arbitrary"),
    )(xbc, dtraw, dy, prev, dtb, arow, dvec, seg)


S5_PARAM_NAMES = ("s5_lambda_re", "s5_lambda_im", "s5_log_dt", "s5_b_re", "s5_b_im", "s5_c_re", "s5_c_im")


def _row(v):
    return v.reshape(1, -1)


def _s5_layer_fwd(tag, h, w, j):
    u = _matmul(tag + "_win", h, w["s5_w_in"][j])
    params = [w[k][j] for k in S5_PARAM_NAMES]
    ops, build_vjp = jax.vjp(_s5_build, *params)
    ug = _s5_to_groups(u, True)
    yg, sprev = _s5_core_fwd(tag + "_core", ug, ops)
    yy = _s5_from_groups(yg, False)
    skip = _row(w["s5_d"][j])
    gl = _gelu_fwd(tag + "_gelu", yy, u, skip)
    ab = _matmul(tag + "_wglu", gl, w["s5_w_glu"][j])
    out = _glu_fwd(tag + "_glu", ab, _row(w["s5_b_glu"][j]))
    return out, dict(u=u, ug=ug, ops=ops, build_vjp=build_vjp, sprev=sprev, yy=yy, gl=gl, ab=ab, skip=skip)


def _s5_layer_bwd(tag, dy, h, sv, w, j):
    dab, db_glu = _glu_bwd(tag + "_glu_b", sv["ab"], _row(w["s5_b_glu"][j]), dy)
    dw_glu = _matmul(tag + "_dwglu", sv["gl"], dab, "tn")
    dgl = _matmul(tag + "_dgl", dab, w["s5_w_glu"][j], "nt")
    dyy, dskip = _gelu_bwd(tag + "_gelu_b", sv["yy"], sv["u"], sv["skip"], dgl)
    dug, dkc, dbend, dcpow, da1, da2s = _s5_core_bwd(
        tag + "_core_b", sv["ug"], _s5_to_groups(dyy, False), sv["sprev"], sv["ops"])
    dparams = sv["build_vjp"]((dkc, dbend, dcpow, da1, _swap_halves(da2s, 2)))
    du = _axpy(tag + "_du", _s5_from_groups(dug, True), dyy, sv["skip"])
    grads = dict(zip(S5_PARAM_NAMES, dparams))
    grads["s5_d"] = dskip.reshape(-1)
    grads["s5_w_in"] = _matmul(tag + "_dwin", h, du, "tn")
    grads["s5_w_glu"] = dw_glu
    grads["s5_b_glu"] = db_glu.reshape(-1)
    dh = _matmul(tag + "_dh", du, w["s5_w_in"][j], "nt")
    return dh, grads


def _ssd_consts(w, j, d_model):
    di = 2 * d_model
    heads = di // M2_P

    def pad_row(v):
        return jnp.zeros((1, LANES), F32).at[0, :heads].set(v)

    a = -jnp.exp(w["m2_a_log"][j])
    seg = (jnp.arange(di)[:, None] // M2_P == jnp.arange(LANES)[None, :]).astype(F32)
    w_in = w["m2_w_in"][j]
    conv_dim = di + 2 * M2_G * M2_N
    w_dt = jnp.zeros((d_model, LANES), w_in.dtype).at[:, :heads].set(w_in[:, di + conv_dim:])
    return dict(di=di, heads=heads, conv_dim=conv_dim, a=a, arow=pad_row(a), dtb=pad_row(w["m2_dt_bias"][j]),
                dvec=_row(jnp.repeat(w["m2_d"][j], M2_P)), seg=seg,
                w_z=w_in[:, :di], w_xbc=w_in[:, di:di + conv_dim], w_dt=w_dt,
                conv_w=w["m2_conv_w"][j], conv_b=_row(w["m2_conv_b"][j]), norm_g=_row(w["m2_norm_g"][j]))


def _ssd_layer_fwd(tag, h, w, j):
    k = _ssd_consts(w, j, h.shape[1])
    z = _matmul(tag + "_wz", h, k["w_z"])
    xbc_pre = _matmul(tag + "_wxbc", h, k["w_xbc"])
    dtraw = _matmul(tag + "_wdt", h, k["w_dt"])
    xbc = _conv_fwd(tag + "_conv", xbc_pre, k["conv_w"], k["conv_b"])
    y, prev = _ssd_fwd(tag + "_core", xbc, dtraw, k["dtb"], k["arow"], k["dvec"], k["heads"])
    yn = _gatenorm_fwd(tag + "_gn", y, z, k["norm_g"])
    out = _matmul(tag + "_wout", yn, w["m2_w_out"][j])
    return out, dict(k=k, z=z, xbc_pre=xbc_pre, dtraw=dtraw, xbc=xbc, y=y, prev=prev, yn=yn)


def _ssd_layer_bwd(tag, dy, h, sv, w, j):
    k = sv["k"]
    heads = k["heads"]
    grads = {"m2_w_out": _matmul(tag + "_dwout", sv["yn"], dy, "tn")}
    dyn = _matmul(tag + "_dyn", dy, w["m2_w_out"][j], "nt")
    dyssd, dz, dng = _gatenorm_bwd(tag + "_gn_b", sv["y"], sv["z"], k["norm_g"], dyn)
    dxbc, ddtraw, da, ddtb, dd = _ssd_bwd(tag + "_core_b", sv["xbc"], sv["dtraw"], dyssd, sv["prev"],
                                          k["dtb"], k["arow"], k["dvec"], k["seg"], heads)
    dpre, dcw, dcb = _conv_bwd_pre(tag + "_conv_b1", sv["xbc_pre"], dxbc, k["conv_w"], k["conv_b"])
    dxbc_pre = _conv_bwd_in(tag + "_conv_b2", dpre, k["conv_w"])
    dw_z = _matmul(tag + "_dwz", h, dz, "tn")
    dw_xbc = _matmul(tag + "_dwxbc", h, dxbc_pre, "tn")
    dw_dt = _matmul(tag + "_dwdt", h, ddtraw, "tn")
    dh = _matmul(tag + "_dh1", dz, k["w_z"], "nt")
    dh = _matmul(tag + "_dh2", dxbc_pre, k["w_xbc"], "nt", addin=dh)
    dh = _matmul(tag + "_dh3", ddtraw, k["w_dt"], "nt", addin=dh)
    grads["m2_w_in"] = jnp.concatenate([dw_z, dw_xbc, dw_dt[:, :heads]], axis=1)
    grads["m2_conv_w"] = dcw
    grads["m2_conv_b"] = dcb.reshape(-1)
    grads["m2_dt_bias"] = ddtb[0, :heads]
    grads["m2_a_log"] = da[0, :heads] * k["a"]
    grads["m2_d"] = dd[0, :heads]
    grads["m2_norm_g"] = dng.reshape(-1)
    return dh, grads


def _layer_fwd(li, x, mod, w):
    tag = "L%d" % li
    sh1, sc1, g1, sh2, sc2, g2 = mod
    j = li // 2
    h = _normmod_fwd(tag + "_nm1", x, _row(w["norm_mix_g"][li]), sh1, sc1)
    if li % 2 == 0:
        y, mix = _s5_layer_fwd(tag + "_s5", h, w, j)
    else:
        y, mix = _ssd_layer_fwd(tag + "_m2", h, w, j)
    x1 = _resid_fwd(tag + "_res1", x, y, g1)
    h2 = _normmod_fwd(tag + "_nm2", x1, _row(w["norm_mlp_g"][li]), sh2, sc2)
    r = _matmul(tag + "_w1", h2, w["mlp_w1"][li], relu=True, out_dtype=BF16)
    m = _matmul(tag + "_w2", r, w["mlp_w2"][li], square_a=True)
    x2 = _resid_fwd(tag + "_res2", x1, m, g2)
    return x2, dict(x=x, h=h, y=y, mix=mix, x1=x1, h2=h2, r=r, m=m)


def _layer_bwd(li, dx2, sv, mod, w):
    tag = "L%d" % li
    sh1, sc1, g1, sh2, sc2, g2 = mod
    j = li // 2
    dm, dg2 = _resid_bwd(tag + "_res2_b", dx2, sv["m"], g2)
    grads = {"mlp_w2": _matmul(tag + "_dw2", sv["r"], dm, "tn", square_a=True)}
    dr = _matmul(tag + "_dr", dm, w["mlp_w2"][li], "nt", out_dtype=BF16, mul2=sv["r"])
    grads["mlp_w1"] = _matmul(tag + "_dw1", sv["h2"], dr, "tn")
    dh2 = _matmul(tag + "_dh2", dr, w["mlp_w1"][li], "nt")
    dx1, dgm, dsh2, dsc2 = _normmod_bwd(tag + "_nm2_b", sv["x1"], _row(w["norm_mlp_g"][li]), sh2, sc2, dh2, dx2)
    dy, dg1 = _resid_bwd(tag + "_res1_b", dx1, sv["y"], g1)
    if li % 2 == 0:
        dh, mix_grads = _s5_layer_bwd(tag + "_s5", dy, sv["h"], sv["mix"], w, j)
    else:
        dh, mix_grads = _ssd_layer_bwd(tag + "_m2", dy, sv["h"], sv["mix"], w, j)
    dx, dgx, dsh1, dsc1 = _normmod_bwd(tag + "_nm1_b", sv["x"], _row(w["norm_mix_g"][li]), sh1, sc1, dh, dx1)
    grads["norm_mix_g"] = dgx.reshape(-1)
    grads["norm_mlp_g"] = dgm.reshape(-1)
    dmod = jnp.concatenate([dsh1, dsc1, dg1, dsh2, dsc2, dg2], axis=1)
    return dx, grads, mix_grads, dmod


STACKED = ("norm_mix_g", "norm_mlp_g", "mlp_w1", "mlp_w2")
S5_NAMES = S5_PARAM_NAMES + ("s5_d", "s5_w_in", "s5_w_glu", "s5_b_glu")
M2_NAMES = ("m2_w_in", "m2_conv_w", "m2_conv_b", "m2_dt_bias", "m2_a_log", "m2_d", "m2_norm_g", "m2_w_out")


def _local_step(x, target, mods, w):
    depth = w["norm_mix_g"].shape[0]
    d = x.shape[1]
    saved = []
    mod_rows = []
    for li in range(depth):
        mod = [mods[li:li + 1, i * d:(i + 1) * d] for i in range(N_MOD)]
        mod_rows.append(mod)
        x, sv = _layer_fwd(li, x, mod, w)
        saved.append(sv)
    dx, dgf, loss = _loss_head("loss_head", x, target, _row(w["final_norm_g"]))
    per_layer = {k: [None] * depth for k in STACKED}
    s5 = {k: [None] * (depth - depth // 2) for k in S5_NAMES}
    m2 = {k: [None] * (depth // 2) for k in M2_NAMES}
    dmods = [None] * depth
    for li in reversed(range(depth)):
        dx, grads, mix_grads, dmods[li] = _layer_bwd(li, dx, saved[li], mod_rows[li], w)
        for k, v in grads.items():
            per_layer[k][li] = v
        for k, v in mix_grads.items():
            (s5 if li % 2 == 0 else m2)[k][li // 2] = v
    out = {k: jnp.stack(v) for k, v in {**per_layer, **s5, **m2}.items()}
    out["final_norm_g"] = dgf.reshape(-1)
    return loss, dx, out, jnp.concatenate(dmods, axis=0)


ANY = pl.BlockSpec(memory_space=pl.ANY)
N_DEV = 8
N_CHIP = 4


def _coords():
    return lax.axis_index("x"), lax.axis_index("y"), lax.axis_index("c")


def _allgather8(name, block):
    r, wd = block.shape

    def body(x_ref, out_ref, send_sems, recv_sems, local_sem):
        x, y, c = _coords()
        me, sibling = (x, y, c), (x, y, 1 - c)
        chips = [(1 - x, y), (x, 1 - y), (1 - x, 1 - y)]

        def slot(px, py, pc):
            return out_ref.at[4 * px + 2 * py + pc]

        def copy(k, blk, to, src=None):
            return pltpu.make_async_remote_copy(
                src_ref=slot(*blk) if src is None else src, dst_ref=slot(*blk),
                send_sem=send_sems.at[k], recv_sem=recv_sems.at[k], device_id=to, device_id_type=MESH)

        mine = pltpu.make_async_copy(x_ref, slot(*me), local_sem)
        mine.start()
        first = [copy(0, me, sibling, src=x_ref)]
        first += [copy(1 + j, me, (*chip, c), src=x_ref) for j, chip in enumerate(chips)]
        for cp in first:
            cp.start()
        passed = [copy(4 + j, (*chip, c), sibling) for j, chip in enumerate(chips)]
        for j, chip in enumerate(chips):
            copy(1 + j, (*chip, c), me).wait_recv()
            passed[j].start()
        copy(0, sibling, me).wait_recv()
        for j, chip in enumerate(chips):
            copy(4 + j, (*chip, 1 - c), me).wait_recv()
        for cp in first + passed:
            cp.wait_send()
        mine.wait()

    return pl.pallas_call(
        body, name=name, in_specs=[ANY], out_specs=ANY,
        out_shape=jax.ShapeDtypeStruct((N_DEV, r, wd), block.dtype),
        scratch_shapes=[pltpu.SemaphoreType.DMA((7,)), pltpu.SemaphoreType.DMA((7,)), pltpu.SemaphoreType.DMA],
    )(block)


def _exchange4(name, pieces, same):
    r, wd = pieces.shape[-2:]

    def body(x_ref, out_ref, send_sems, recv_sems, local_sem):
        x, y, c = _coords()
        my_chip = 2 * x + y
        chips = [(1 - x, y), (x, 1 - y), (1 - x, 1 - y)]

        def src(k):
            return x_ref if same else x_ref.at[k]

        mine = pltpu.make_async_copy(src(my_chip), out_ref.at[my_chip], local_sem)
        mine.start()
        sends = []
        for j, (px, py) in enumerate(chips):
            cp = pltpu.make_async_remote_copy(
                src_ref=src(2 * px + py), dst_ref=out_ref.at[my_chip],
                send_sem=send_sems.at[j], recv_sem=recv_sems.at[j], device_id=(px, py, c), device_id_type=MESH)
            cp.start()
            sends.append(cp)
        for j, (px, py) in enumerate(chips):
            pltpu.make_async_remote_copy(
                src_ref=src(my_chip), dst_ref=out_ref.at[2 * px + py],
                send_sem=send_sems.at[j], recv_sem=recv_sems.at[j], device_id=(px, py, c),
                device_id_type=MESH).wait_recv()
        for cp in sends:
            cp.wait_send()
        mine.wait()

    return pl.pallas_call(
        body, name=name, in_specs=[ANY], out_specs=ANY,
        out_shape=jax.ShapeDtypeStruct((N_CHIP, r, wd), pieces.dtype),
        scratch_shapes=[pltpu.SemaphoreType.DMA((3,)), pltpu.SemaphoreType.DMA((3,)), pltpu.SemaphoreType.DMA],
    )(pieces)


def _swap_sibling(name, block):
    def body(x_ref, out_ref, send_sem, recv_sem):
        x, y, c = _coords()
        cp = pltpu.make_async_remote_copy(src_ref=x_ref, dst_ref=out_ref, send_sem=send_sem, recv_sem=recv_sem,
                                          device_id=(x, y, 1 - c), device_id_type=MESH)
        cp.start()
        cp.wait()

    return pl.pallas_call(
        body, name=name, in_specs=[ANY], out_specs=ANY, out_shape=jax.ShapeDtypeStruct(block.shape, block.dtype),
        scratch_shapes=[pltpu.SemaphoreType.DMA, pltpu.SemaphoreType.DMA],
    )(block)


def _sum_slots(name, stacked):
    n, r, wd = stacked.shape
    tile = min(FLAT_ROWS, r)

    def kern(x_ref, o_ref):
        acc = x_ref[0].astype(F32)
        for s in range(1, n):
            acc = acc + x_ref[s].astype(F32)
        o_ref[...] = acc

    return pl.pallas_call(
        kern, name=name, grid=(r // tile,), in_specs=[pl.BlockSpec((n, tile, wd), lambda i: (0, i, 0))],
        out_specs=pl.BlockSpec((tile, wd), lambda i: (i, 0)), out_shape=jax.ShapeDtypeStruct((r, wd), F32),
        compiler_params=_params("parallel"),
    )(stacked)


def _adamw(name, w, m, v, g, g2=None):
    r, wd = w.shape
    grads = [g] if g2 is None else [g, g2]
    c1 = 1.0 - ADAM_B1 ** ADAM_STEP
    c2 = 1.0 - ADAM_B2 ** ADAM_STEP

    def body(i, *refs):
        w_ref, m_ref, v_ref = refs[:3]
        g_refs = refs[3:3 + len(grads)]
        go_ref, d_ref, mo_ref, vo_ref = refs[3 + len(grads):]
        gv = g_refs[0][...]
        if g2 is not None:
            gv = gv + g_refs[1][...]
        mn = ADAM_B1 * m_ref[...] + (1.0 - ADAM_B1) * gv
        vn = ADAM_B2 * v_ref[...] + (1.0 - ADAM_B2) * (gv * gv)
        go_ref[...] = gv
        mo_ref[...] = mn
        vo_ref[...] = vn
        d_ref[...] = -ADAM_LR * ((mn / c1) / (jnp.sqrt(vn / c2) + ADAM_EPS) + ADAM_WD * w_ref[...])

    return _rowcall(name, body, r, FLAT_ROWS, [w, m, v] + grads, [], [(wd, F32)] * 4, [])


FLAT_BLOCK = FLAT_ROWS * FLAT_W


def _pack(arrays, dtype):
    flat = jnp.concatenate([a.reshape(-1).astype(dtype) for a in arrays])
    pad = (-flat.shape[0]) % FLAT_BLOCK
    return jnp.pad(flat, (0, pad)).reshape(-1, FLAT_W)


def _unpack(buf, shapes):
    flat = buf.reshape(-1)
    out, off = [], 0
    for s in shapes:
        n = math.prod(s)
        out.append(flat[off:off + n].reshape(s))
        off += n
    return out


SHARDED_BIG = {"mlp_w1": 2, "mlp_w2": 1, "s5_w_in": 1, "s5_w_glu": 2, "m2_w_in": 2, "m2_w_out": 1}
SHARDED_SMALL = {"m2_conv_w": 2, "m2_conv_b": 1, "m2_norm_g": 1}
REPLICATED = ("ada_b", "norm_mix_g", "norm_mlp_g", "s5_lambda_re", "s5_lambda_im", "s5_log_dt", "s5_b_re",
              "s5_b_im", "s5_c_re", "s5_c_im", "s5_d", "s5_b_glu", "m2_dt_bias", "m2_a_log", "m2_d", "final_norm_g")
WEIGHT_NAMES = ("ada_w", "ada_b", "norm_mix_g", "norm_mlp_g", "mlp_w1", "mlp_w2", "s5_w_in", "s5_lambda_re",
                "s5_lambda_im", "s5_log_dt", "s5_b_re", "s5_b_im", "s5_c_re", "s5_c_im", "s5_d", "s5_w_glu",
                "s5_b_glu", "m2_w_in", "m2_conv_w", "m2_conv_b", "m2_dt_bias", "m2_a_log", "m2_d", "m2_norm_g",
                "m2_w_out", "final_norm_g")


def _gather_weights(name, local, names_axes, dtype):
    names = list(names_axes)
    got = _exchange4(name, _pack([local[k] for k in names], dtype), same=True)
    per_chip = [_unpack(got[j], [local[k].shape for k in names]) for j in range(N_CHIP)]
    return {k: jnp.concatenate([per_chip[j][i] for j in range(N_CHIP)], axis=names_axes[k])
            for i, k in enumerate(names)}


def _chip_slice(a, chip, axis):
    size = a.shape[axis] // N_CHIP
    return lax.slice_in_dim(a, chip * size, (chip + 1) * size, axis=axis)


def kernel(x, c, ada_w, ada_b, norm_mix_g, norm_mlp_g, mlp_w1, mlp_w2, s5_w_in, s5_lambda_re, s5_lambda_im, s5_log_dt, s5_b_re, s5_b_im, s5_c_re, s5_c_im, s5_d, s5_w_glu, s5_b_glu, m2_w_in, m2_conv_w, m2_conv_b, m2_dt_bias, m2_a_log, m2_d, m2_norm_g, m2_w_out, final_norm_g, loss_target, m_ada_w, m_ada_b, m_norm_mix_g, m_norm_mlp_g, m_mlp_w1, m_mlp_w2, m_s5_w_in, m_s5_lambda_re, m_s5_lambda_im, m_s5_log_dt, m_s5_b_re, m_s5_b_im, m_s5_c_re, m_s5_c_im, m_s5_d, m_s5_w_glu, m_s5_b_glu, m_m2_w_in, m_m2_conv_w, m_m2_conv_b, m_m2_dt_bias, m_m2_a_log, m_m2_d, m_m2_norm_g, m_m2_w_out, m_final_norm_g, v_ada_w, v_ada_b, v_norm_mix_g, v_norm_mlp_g, v_mlp_w1, v_mlp_w2, v_s5_w_in, v_s5_lambda_re, v_s5_lambda_im, v_s5_log_dt, v_s5_b_re, v_s5_b_im, v_s5_c_re, v_s5_c_im, v_s5_d, v_s5_w_glu, v_s5_b_glu, v_m2_w_in, v_m2_conv_w, v_m2_conv_b, v_m2_dt_bias, v_m2_a_log, v_m2_d, v_m2_norm_g, v_m2_w_out, v_final_norm_g):
    args = locals()
    local = {k: args[k] for k in WEIGHT_NAMES}
    mom_m = {k: args["m_" + k] for k in WEIGHT_NAMES}
    mom_v = {k: args["v_" + k] for k in WEIGHT_NAMES}
    depth, d = norm_mix_g.shape
    xi, yi, ci = _coords()
    my_chip = 2 * xi + yi
    my_dev = 2 * my_chip + ci

    cond = jax.nn.silu(c).reshape(-1, LANES)
    cond_all = _allgather8("ag_cond", cond).reshape(N_DEV, d)
    cond_pad = jnp.zeros((LANES, d), F32).at[:N_DEV].set(cond_all)
    mod_cols = ada_w.shape[2]
    mod_part = jnp.stack([_matmul("ada_%d" % i, cond_pad, ada_w[i])[:N_DEV] for i in range(depth)])
    mod_all = _allgather8("ag_mod", mod_part.reshape(-1, LANES)).reshape(N_CHIP, 2, depth, N_DEV, mod_cols)[:, 0]
    mod_mine = lax.dynamic_index_in_dim(mod_all, my_dev, axis=2, keepdims=False)
    mods = jnp.transpose(mod_mine, (1, 0, 2)).reshape(depth, N_CHIP * mod_cols) + ada_b

    w = {k: local[k] for k in REPLICATED}
    w.update(_gather_weights("ag_w_big", local, SHARDED_BIG, BF16))
    w.update(_gather_weights("ag_w_small", local, SHARDED_SMALL, F32))

    loss_row, dx, grads, dmods = _local_step(x[0], loss_target[0], mods, w)
    grads["ada_b"] = dmods

    rep_shapes = [grads[k].shape for k in REPLICATED]
    rep_all = _allgather8("ag_grep", _pack([grads[k] for k in REPLICATED], F32))
    rep_sum = _sum_slots("sum_grep", rep_all)
    dmods_all = rep_all.reshape(N_DEV, -1)[:, :dmods.size].reshape(N_DEV, depth, N_CHIP * mod_cols)

    dm_mine = lax.dynamic_slice_in_dim(dmods_all, my_chip * mod_cols, mod_cols, axis=2)
    dm_pad = jnp.zeros((LANES, depth, mod_cols), F32).at[:N_DEV].set(dm_mine)
    g_ada_w = jnp.stack([_matmul("dada_%d" % i, cond_pad, dm_pad[:, i], "tn") for i in range(depth)])

    sh_names = list(SHARDED_BIG) + list(SHARDED_SMALL)
    sh_axes = {**SHARDED_BIG, **SHARDED_SMALL}
    pieces = jnp.stack([_pack([_chip_slice(grads[k], j, sh_axes[k]) for k in sh_names], BF16)
                        for j in range(N_CHIP)])
    landed = _exchange4("rs_grads", pieces, same=False)
    part = _sum_slots("sum_gsh", landed)
    other = _swap_sibling("swap_gsh", part)

    out_g, out_d, out_m, out_v = {}, {}, {}, {}
    sh_shapes = [local[k].shape for k in sh_names]
    res = _adamw("adam_sh", _pack([local[k] for k in sh_names], F32), _pack([mom_m[k] for k in sh_names], F32),
                 _pack([mom_v[k] for k in sh_names], F32), part, other)
    for dst, buf in zip((out_g, out_d, out_m, out_v), res):
        dst.update(zip(sh_names, _unpack(buf, sh_shapes)))
    res = _adamw("adam_rep", _pack([local[k] for k in REPLICATED], F32), _pack([mom_m[k] for k in REPLICATED], F32),
                 _pack([mom_v[k] for k in REPLICATED], F32), rep_sum)
    for dst, buf in zip((out_g, out_d, out_m, out_v), res):
        dst.update(zip(REPLICATED, _unpack(buf, rep_shapes)))
    flat2 = (-1, mod_cols)
    res = _adamw("adam_ada", ada_w.reshape(flat2), m_ada_w.reshape(flat2), v_ada_w.reshape(flat2),
                 g_ada_w.reshape(flat2))
    for dst, buf in zip((out_g, out_d, out_m, out_v), res):
        dst["ada_w"] = buf.reshape(ada_w.shape)

    loss = lax.psum(loss_row[0, 0], ("x", "y", "c"))
    outs = [loss, dx[None]]
    for dst in (out_g, out_d, out_m, out_v):
        outs += [dst[k] for k in WEIGHT_NAMES]
    return tuple(outs)
```

```python
import functools
import math

import jax
import jax.numpy as jnp
from jax import lax
from jax.experimental import pallas as pl
from jax.experimental.pallas import tpu as pltpu

F32 = jnp.float32
BF16 = jnp.bfloat16
HIGHEST = lax.Precision.HIGHEST

NORM_EPS = 1e-5
N_MOD = 6
S5_H, S5_P, S5_T = 16, 64, 64
M2_P, M2_N, M2_G, M2_Q, M2_K = 64, 128, 4, 128, 4
LANES = 128
ADAM_LR, ADAM_B1, ADAM_B2, ADAM_EPS, ADAM_WD, ADAM_STEP = 0.001, 0.9, 0.999, 1e-08, 0.01, 10
VMEM_LIMIT_BYTES = 48 * 1024 * 1024
ROW_TILE = 256
FLAT_W = 1024
FLAT_ROWS = 256
MESH = pl.DeviceIdType.MESH


def _params(*sem):
    return pltpu.CompilerParams(dimension_semantics=sem, vmem_limit_bytes=VMEM_LIMIT_BYTES)


def _dot(a, b, dn="nn", precision=None):
    dims = {"nn": ((1,), (0,)), "nt": ((1,), (1,)), "tn": ((0,), (0,))}[dn]
    return lax.dot_general(a, b, (dims, ((), ())), preferred_element_type=F32, precision=precision)


def _bdot(a, b, dn="nn"):
    return _dot(a.astype(BF16), b.astype(BF16), dn)


def _sigmoid(x):
    return jax.nn.sigmoid(x)


def _colsum(x):
    return jnp.sum(x, axis=0, keepdims=True)


def _pick_tile(dim, want):
    if dim <= want:
        return dim
    for t in range(want - want % LANES, 0, -LANES):
        if dim % t == 0:
            return t
    raise ValueError((dim, want))


def _matmul(name, a, b, mode="nn", out_dtype=F32, relu=False, square_a=False, mul2=None, colscale=None,
            addin=None, colsum_with=None, tm=1024, tn=1024, tk=1024):
    if mode == "nn":
        (m, k), (k2, n) = a.shape, b.shape
    elif mode == "nt":
        (m, k), (n, k2) = a.shape, b.shape
    else:
        (k, m), (k2, n) = a.shape, b.shape
    assert k == k2, (name, a.shape, b.shape)
    tm, tn, tk = _pick_tile(m, tm), _pick_tile(n, tn), _pick_tile(k, tk)
    nk = k // tk
    tiles = [e for e in (mul2, addin, colsum_with) if e is not None]
    n_ext = len(tiles) + (colscale is not None)
    n_out = 1 + (colsum_with is not None)

    def kern(*refs):
        a_ref, b_ref = refs[:2]
        e_refs = list(refs[2:2 + n_ext])
        o_refs = refs[2 + n_ext:2 + n_ext + n_out]
        kk = pl.program_id(2)
        av = a_ref[...]
        if square_a:
            av = av * av
        part = _bdot(av, b_ref[...], mode)

        def finish(r):
            ext = list(e_refs)
            m2v = ext.pop(0)[...].astype(F32) if mul2 is not None else None
            addv = ext.pop(0)[...].astype(F32) if addin is not None else None
            if colsum_with is not None:
                o_refs[1][0] = _colsum(r * ext.pop(0)[...].astype(F32))
            if relu:
                r = jnp.maximum(r, 0.0)
            if m2v is not None:
                r = r * (2.0 * m2v)
            if colscale is not None:
                r = r * ext.pop(0)[...]
            if addv is not None:
                r = r + addv
            o_refs[0][...] = r.astype(out_dtype)

        if nk == 1:
            finish(part)
        else:
            acc = refs[-1]

            @pl.when(kk == 0)
            def _():
                acc[...] = part

            @pl.when(kk > 0)
            def _():
                acc[...] += part

            @pl.when(kk == nk - 1)
            def _():
                finish(acc[...])

    if mode == "tn":
        a_spec = pl.BlockSpec((tk, tm), lambda i, j, kk: (kk, i))
    else:
        a_spec = pl.BlockSpec((tm, tk), lambda i, j, kk: (i, kk))
    if mode == "nt":
        b_spec = pl.BlockSpec((tn, tk), lambda i, j, kk: (j, kk))
    else:
        b_spec = pl.BlockSpec((tk, tn), lambda i, j, kk: (kk, j))
    o_spec = pl.BlockSpec((tm, tn), lambda i, j, kk: (i, j))
    in_specs = [a_spec, b_spec] + [o_spec] * len(tiles)
    operands = [a, b] + tiles
    if colscale is not None:
        in_specs.append(pl.BlockSpec((1, tn), lambda i, j, kk: (0, j)))
        operands.append(colscale)
    out_specs = [o_spec]
    out_shape = [jax.ShapeDtypeStruct((m, n), out_dtype)]
    if colsum_with is not None:
        out_specs.append(pl.BlockSpec((1, 1, tn), lambda i, j, kk: (i, 0, j)))
        out_shape.append(jax.ShapeDtypeStruct((m // tm, 1, n), F32))
    res = pl.pallas_call(
        kern, name=name, grid=(m // tm, n // tn, nk), in_specs=in_specs, out_specs=out_specs, out_shape=out_shape,
        scratch_shapes=[pltpu.VMEM((tm, tn), F32)] if nk > 1 else [],
        compiler_params=_params("parallel", "parallel", "arbitrary"),
    )(*operands)
    return res if colsum_with is not None else res[0]


def _rowcall(name, body, rows, tile, row_ins, small_ins, row_outs, acc_outs):
    tile = min(tile, rows)
    assert rows % tile == 0, (name, rows, tile)
    n_in = len(row_ins) + len(small_ins)

    def kern(*refs):
        i = pl.program_id(0)
        accs = refs[n_in + len(row_outs):]

        @pl.when(i == 0)
        def _():
            for acc in accs:
                acc[...] = jnp.zeros_like(acc)

        body(i, *refs)

    def whole(shape):
        return pl.BlockSpec(shape, lambda i, nd=len(shape): (0,) * nd)

    in_specs = [pl.BlockSpec((tile, a.shape[1]), lambda i: (i, 0)) for a in row_ins]
    in_specs += [whole(a.shape) for a in small_ins]
    out_specs = [pl.BlockSpec((tile, w), lambda i: (i, 0)) for (w, _) in row_outs]
    out_specs += [whole(s) for s in acc_outs]
    out_shape = [jax.ShapeDtypeStruct((rows, w), dt) for (w, dt) in row_outs]
    out_shape += [jax.ShapeDtypeStruct(s, F32) for s in acc_outs]
    return pl.pallas_call(
        kern, name=name, grid=(rows // tile,), in_specs=in_specs, out_specs=out_specs, out_shape=out_shape,
        compiler_params=_params("arbitrary"),
    )(*row_ins, *small_ins)


def _rms(x):
    r = lax.rsqrt(jnp.mean(x * x, axis=-1, keepdims=True) + NORM_EPS)
    return x * r, r


def _rms_bwd(dxhat, xhat, r):
    return r * (dxhat - xhat * jnp.mean(dxhat * xhat, axis=-1, keepdims=True))


def _normmod_fwd(name, x, g, sh, sc):
    def body(i, x_ref, g_ref, sh_ref, sc_ref, o_ref):
        xhat, _ = _rms(x_ref[...])
        o_ref[...] = ((xhat * g_ref[...]) * (1.0 + sc_ref[...]) + sh_ref[...]).astype(BF16)

    return _rowcall(name, body, x.shape[0], ROW_TILE, [x], [g, sh, sc], [(x.shape[1], BF16)], [])[0]


def _normmod_bwd(name, x, g, sh, sc, dh, dx_pass):
    d = x.shape[1]

    def body(i, x_ref, dh_ref, dxp_ref, g_ref, sh_ref, sc_ref, dx_ref, dg_ref, dsh_ref, dsc_ref):
        xhat, r = _rms(x_ref[...])
        dh = dh_ref[...].astype(F32)
        gv = g_ref[...]
        dn = dh * (1.0 + sc_ref[...])
        dsc_ref[...] += _colsum(dh * (xhat * gv))
        dsh_ref[...] += _colsum(dh)
        dg_ref[...] += _colsum(dn * xhat)
        dx_ref[...] = dxp_ref[...] + _rms_bwd(dn * gv, xhat, r)

    return _rowcall(name, body, x.shape[0], ROW_TILE, [x, dh, dx_pass], [g, sh, sc], [(d, F32)],
                    [(1, d), (1, d), (1, d)])


def _scale_cols(name, w, g):
    def body(i, w_ref, g_ref, o_ref):
        o_ref[...] = (w_ref[...].astype(F32) * g_ref[...]).astype(BF16)

    return _rowcall(name, body, w.shape[0], ROW_TILE, [w], [g], [(w.shape[1], BF16)], [])[0]


GELU_K = math.sqrt(2.0 / math.pi)
GELU_C = 0.044715


def _gelu_fwd(name, y, u, skip):
    def body(i, y_ref, u_ref, s_ref, o_ref):
        v = y_ref[...].astype(F32) + s_ref[...] * u_ref[...]
        t = jnp.tanh(GELU_K * (v + GELU_C * (v * v * v)))
        o_ref[...] = (0.5 * v * (1.0 + t)).astype(BF16)

    return _rowcall(name, body, y.shape[0], ROW_TILE, [y, u], [skip], [(y.shape[1], BF16)], [])[0]


def _gelu_bwd(name, y, u, skip, dgl):
    d = y.shape[1]

    def body(i, y_ref, u_ref, d_ref, s_ref, o_ref, ds_ref):
        uv = u_ref[...]
        v = y_ref[...].astype(F32) + s_ref[...] * uv
        t = jnp.tanh(GELU_K * (v + GELU_C * (v * v * v)))
        dv = d_ref[...] * (0.5 * (1.0 + t) + 0.5 * v * (1.0 - t * t) * (GELU_K * (1.0 + 3.0 * GELU_C * v * v)))
        o_ref[...] = dv
        ds_ref[...] += _colsum(dv * uv)

    return _rowcall(name, body, y.shape[0], ROW_TILE, [y, u, dgl], [skip], [(d, F32)], [(1, d)])


def _axpy(name, a, b, scale):
    def body(i, a_ref, b_ref, s_ref, o_ref):
        o_ref[...] = (a_ref[...].astype(F32) + s_ref[...] * b_ref[...]).astype(BF16)

    return _rowcall(name, body, a.shape[0], ROW_TILE, [a, b], [scale], [(a.shape[1], BF16)], [])[0]


def _glu_fwd(name, ab, bias, x, gate):
    d = ab.shape[1] // 2

    def body(i, ab_ref, x_ref, b_ref, g_ref, o_ref):
        v = ab_ref[:, :d] + b_ref[:, :d]
        gt = ab_ref[:, d:] + b_ref[:, d:]
        o_ref[...] = x_ref[...] + g_ref[...] * (v * _sigmoid(gt))

    return _rowcall(name, body, ab.shape[0], ROW_TILE, [ab, x], [bias, gate], [(d, F32)], [])[0]


def _glu_bwd(name, ab, bias, dxo, gate):
    d = ab.shape[1] // 2

    def body(i, ab_ref, dx_ref, b_ref, g_ref, dab_ref, db_ref, dg_ref):
        v = ab_ref[:, :d] + b_ref[:, :d]
        s = _sigmoid(ab_ref[:, d:] + b_ref[:, d:])
        dxo_v = dx_ref[...]
        dg_ref[...] += _colsum(dxo_v * (v * s))
        do = g_ref[...] * dxo_v
        dv = do * s
        dgt = do * v * (s * (1.0 - s))
        dab_ref[:, :d] = dv.astype(BF16)
        dab_ref[:, d:] = dgt.astype(BF16)
        db_ref[:, :d] += _colsum(dv)
        db_ref[:, d:] += _colsum(dgt)

    return _rowcall(name, body, ab.shape[0], ROW_TILE, [ab, dxo], [bias, gate], [(2 * d, BF16)],
                    [(1, 2 * d), (1, d)])


def _gatenorm_fwd(name, y, z, ng):
    di = y.shape[1]
    gw = di // M2_G

    def body(i, y_ref, z_ref, g_ref, o_ref):
        for gi in range(M2_G):
            sl = slice(gi * gw, (gi + 1) * gw)
            zz = z_ref[:, sl]
            y2 = y_ref[:, sl] * (zz * _sigmoid(zz))
            yh, _ = _rms(y2)
            o_ref[:, sl] = (yh * g_ref[:, sl]).astype(BF16)

    return _rowcall(name, body, y.shape[0], ROW_TILE, [y, z], [ng], [(di, BF16)], [])[0]


def _gatenorm_bwd(name, y, z, ng, dyn):
    di = y.shape[1]
    gw = di // M2_G

    def body(i, y_ref, z_ref, d_ref, g_ref, dy_ref, dz_ref, dg_ref):
        for gi in range(M2_G):
            sl = slice(gi * gw, (gi + 1) * gw)
            zz = z_ref[:, sl]
            yy = y_ref[:, sl]
            s = _sigmoid(zz)
            sz = zz * s
            yh, r = _rms(yy * sz)
            dn = d_ref[:, sl]
            dg_ref[:, sl] += _colsum(dn * yh)
            dy2 = _rms_bwd(dn * g_ref[:, sl], yh, r)
            dy_ref[:, sl] = dy2 * sz
            dz_ref[:, sl] = (dy2 * yy * (s * (1.0 + zz * (1.0 - s)))).astype(BF16)

    return _rowcall(name, body, y.shape[0], ROW_TILE, [y, z, dyn], [ng], [(di, F32), (di, BF16)], [(1, di)])


def _loss_head(name, x, target, g):
    d = x.shape[1]

    def body(i, x_ref, t_ref, g_ref, dx_ref, dg_ref, loss_ref):
        xhat, r = _rms(x_ref[...])
        gv = g_ref[...]
        err = xhat * gv - t_ref[...]
        per_row = jnp.sum(err * err, axis=-1, keepdims=True) * (0.5 / d)
        loss_ref[...] += jnp.broadcast_to(_colsum(per_row), loss_ref.shape)
        dy = err * (1.0 / d)
        dg_ref[...] += _colsum(dy * xhat)
        dx_ref[...] = _rms_bwd(dy * gv, xhat, r)

    return _rowcall(name, body, x.shape[0], ROW_TILE, [x, target], [g], [(d, F32)], [(1, d), (1, LANES)])


HALO = 8


def _halo_call(name, body, rows, tile, width, mains, halo_of, halo_next, smalls, row_outs, acc_outs, scratch):
    tile = min(tile, rows)
    nb = tile // HALO
    last = rows // HALO - 1
    n_in = len(mains) + 1 + len(smalls)

    def kern(*refs):
        i = pl.program_id(0)
        accs = refs[n_in + len(row_outs):n_in + len(row_outs) + len(acc_outs)]

        @pl.when(i == 0)
        def _():
            for acc in accs:
                acc[...] = jnp.zeros_like(acc)

        body(i, *refs)

    def whole(shape):
        return pl.BlockSpec(shape, lambda i, nd=len(shape): (0,) * nd)

    if halo_next:
        halo_spec = pl.BlockSpec((HALO, width), lambda i: (jnp.minimum((i + 1) * nb, last), 0))
    else:
        halo_spec = pl.BlockSpec((HALO, width), lambda i: (jnp.maximum(i * nb - 1, 0), 0))
    in_specs = [pl.BlockSpec((tile, a.shape[1]), lambda i: (i, 0)) for a in mains] + [halo_spec]
    in_specs += [whole(a.shape) for a in smalls]
    out_specs = [pl.BlockSpec((tile, w), lambda i: (i, 0)) for (w, _) in row_outs] + [whole(s) for s in acc_outs]
    out_shape = [jax.ShapeDtypeStruct((rows, w), dt) for (w, dt) in row_outs]
    out_shape += [jax.ShapeDtypeStruct(s, F32) for s in acc_outs]
    return pl.pallas_call(
        kern, name=name, grid=(rows // tile,), in_specs=in_specs, out_specs=out_specs, out_shape=out_shape,
        scratch_shapes=scratch, compiler_params=_params("arbitrary"),
    )(*mains, mains[halo_of], *smalls)


CONV_TILE = 128
CONV_ROWS = 16
CONV_STRIP = 512


def _conv_blocks(tile, c):
    strip = CONV_STRIP if c % CONV_STRIP == 0 else LANES
    rb = min(CONV_ROWS, tile)
    return [(r0, rb, slice(c0, c0 + strip)) for c0 in range(0, c, strip) for r0 in range(0, tile, rb)]


def _conv_window(x_ref, ext, r0, rb, sl, k):
    if r0 == 0:
        return ext[pl.ds(HALO - 3 + k, rb), sl]
    return x_ref[pl.ds(r0 - 3 + k, rb), sl]


def _conv_fwd(name, xin, w, b):
    rows, c = xin.shape
    tile = min(CONV_TILE, rows)
    rb0 = min(CONV_ROWS, tile)

    def body(i, x_ref, h_ref, w_ref, b_ref, o_ref, ext):
        ext[0:HALO, :] = jnp.where(i == 0, 0.0, h_ref[...])
        ext[HALO:, :] = x_ref[0:rb0, :]
        for r0, rb, sl in _conv_blocks(tile, c):
            pre = b_ref[:, sl] + w_ref[0:1, sl] * _conv_window(x_ref, ext, r0, rb, sl, 0)
            for k in range(1, M2_K):
                pre = pre + w_ref[k:k + 1, sl] * _conv_window(x_ref, ext, r0, rb, sl, k)
            o_ref[r0:r0 + rb, sl] = pre * _sigmoid(pre)

    return _halo_call(name, body, rows, tile, c, [xin], 0, False, [w, b], [(c, F32)], [],
                      [pltpu.VMEM((rb0 + HALO, c), F32)])[0]


def _conv_bwd_pre(name, xin, dout, w, b):
    rows, c = xin.shape
    tile = min(CONV_TILE, rows)
    rb0 = min(CONV_ROWS, tile)

    def body(i, x_ref, d_ref, h_ref, w_ref, b_ref, dp_ref, dw_ref, db_ref, ext):
        ext[0:HALO, :] = jnp.where(i == 0, 0.0, h_ref[...])
        ext[HALO:, :] = x_ref[0:rb0, :]
        sums = {}
        for r0, rb, sl in _conv_blocks(tile, c):
            taps = [_conv_window(x_ref, ext, r0, rb, sl, k) for k in range(M2_K)]
            pre = b_ref[:, sl] + w_ref[0:1, sl] * taps[0]
            for k in range(1, M2_K):
                pre = pre + w_ref[k:k + 1, sl] * taps[k]
            s = _sigmoid(pre)
            dp = d_ref[r0:r0 + rb, sl] * (s * (1.0 + pre * (1.0 - s)))
            dp_ref[r0:r0 + rb, sl] = dp
            part = [_colsum(dp)] + [_colsum(dp * taps[k]) for k in range(M2_K)]
            key = sl.start
            sums[key] = part if key not in sums else [p + q for p, q in zip(sums[key], part)]
            if r0 + rb == tile:
                db_ref[:, sl] += sums[key][0]
                for k in range(M2_K):
                    dw_ref[k:k + 1, sl] += sums[key][1 + k]

    return _halo_call(name, body, rows, tile, c, [xin, dout], 0, False, [w, b], [(c, F32)], [(M2_K, c), (1, c)],
                      [pltpu.VMEM((rb0 + HALO, c), F32)])


def _conv_bwd_in(name, dpre, w):
    rows, c = dpre.shape
    tile = min(CONV_TILE, rows)
    n_tiles = rows // tile
    rb0 = min(CONV_ROWS, tile)

    def body(i, d_ref, h_ref, w_ref, o_ref, ext):
        ext[0:rb0, :] = d_ref[tile - rb0:tile, :]
        ext[rb0:, :] = jnp.where(i == n_tiles - 1, 0.0, h_ref[...])
        for r0, rb, sl in _conv_blocks(tile, c):
            def window(k):
                if r0 + rb == tile:
                    return ext[pl.ds(3 - k, rb), sl]
                return d_ref[pl.ds(r0 + 3 - k, rb), sl]

            acc = w_ref[0:1, sl] * window(0)
            for k in range(1, M2_K):
                acc = acc + w_ref[k:k + 1, sl] * window(k)
            o_ref[r0:r0 + rb, sl] = acc.astype(BF16)

    return _halo_call(name, body, rows, tile, c, [dpre], 0, True, [w], [(c, BF16)], [],
                      [pltpu.VMEM((rb0 + HALO, c), F32)])[0]


def _s5_build(lam_re, lam_im, log_dt, b_re, b_im, c_re, c_im):
    g, p = lam_re.shape
    h = b_re.shape[-1]
    t = S5_T
    dt = jnp.exp(log_dt)[:, None]
    lam = lax.complex(lam_re, lam_im)
    lam_dt = lam * dt
    lam_bar = jnp.exp(lam_dt)
    b_bar = ((lam_bar - 1) / lam)[..., None] * lax.complex(b_re, b_im)
    c_mat = lax.complex(c_re, c_im)
    tau = jnp.arange(t + 1, dtype=F32)
    pw = jnp.exp(lam_dt[:, None, :] * tau[None, :, None])
    cp = c_mat[:, None, :, :] * pw[:, :, None, :]
    cp0_re, cp0_im = jnp.real(cp[:, :t]), jnp.imag(cp[:, :t])
    bb_re, bb_im = jnp.real(b_bar), jnp.imag(b_bar)
    kern = (jnp.einsum("gthp,gpk->gthk", cp0_re, bb_re, precision=HIGHEST)
            - jnp.einsum("gthp,gpk->gthk", cp0_im, bb_im, precision=HIGHEST))
    kc = jnp.transpose(kern, (0, 3, 1, 2)).reshape(g, h, t * h)
    bpow = pw[:, t - 1::-1, :][:, :t]
    be = bpow[:, :, :, None] * b_bar[:, None, :, :]
    be = jnp.transpose(be, (0, 3, 1, 2))
    bend = jnp.concatenate([jnp.real(be), jnp.imag(be)], axis=-1).reshape(g, h * t, 2 * p)
    cp1 = jnp.transpose(cp[:, 1:], (0, 3, 1, 2))
    cpow = jnp.concatenate([jnp.real(cp1), -jnp.imag(cp1)], axis=1).reshape(g, 2 * p, t * h)
    at = pw[:, t, :]
    a1 = jnp.concatenate([jnp.real(at), jnp.real(at)], axis=-1)[:, None, :]
    a2 = jnp.concatenate([-jnp.imag(at), jnp.imag(at)], axis=-1)[:, None, :]
    return kc, bend, cpow, a1, a2


def _swap_halves(x, axis):
    n = x.shape[axis] // 2
    lo = lax.slice_in_dim(x, 0, n, axis=axis)
    hi = lax.slice_in_dim(x, n, 2 * n, axis=axis)
    return jnp.concatenate([hi, lo], axis=axis)


def _group_spec(shape):
    return pl.BlockSpec((1,) + tuple(shape[1:]), lambda g: (g, 0, 0))


def _s5_expand_toeplitz(kc_ref, ext, toep):
    t, th = S5_T, S5_T * S5_H
    ext[:, th:] = jnp.zeros((t, th), F32)
    for hin in range(S5_H):
        ext[:, :th] = jnp.broadcast_to(kc_ref[0, hin:hin + 1, :], (t, th))
        rolled = pltpu.roll(ext[...], 0, 1, stride=S5_H, stride_axis=0)
        toep[hin * t:(hin + 1) * t, :] = rolled[:, :th].astype(BF16)


def _s5_core_fwd(name, u, ops):
    kc, bend, cpow, a1, a2 = ops
    g, nc, th = u.shape
    p2 = bend.shape[-1]
    bend_b, cpow_b = bend.astype(BF16), cpow.astype(BF16)
    bend_s = _swap_halves(bend_b, 2)
    a2s = _swap_halves(a2, 2)

    def kern(u_ref, k_ref, b_ref, bs_ref, c_ref, a1_ref, a2_ref, a2s_ref, y_ref, sp_ref, x_scr, xs_scr, ext, toep):
        _s5_expand_toeplitz(k_ref, ext, toep)
        ub = u_ref[0].astype(BF16)
        x_scr[...] = _dot(ub, b_ref[0])
        xs_scr[...] = _dot(ub, bs_ref[0])
        a1v, a2v, a2sv = a1_ref[0], a2_ref[0], a2s_ref[0]

        def step(c, carry):
            s, ss = carry
            sp_ref[0, pl.ds(c, 1), :] = s
            s_new = a1v * s + a2v * ss + x_scr[pl.ds(c, 1), :]
            ss_new = a1v * ss + a2sv * s + xs_scr[pl.ds(c, 1), :]
            return s_new, ss_new

        zero = jnp.zeros((1, p2), F32)
        lax.fori_loop(0, nc, step, (zero, zero))
        y_ref[0] = (_dot(ub, toep[...]) + _dot(sp_ref[0].astype(BF16), c_ref[0])).astype(BF16)

    ins = [u, kc, bend_b, bend_s, cpow_b, a1, a2, a2s]
    return pl.pallas_call(
        kern, name=name, grid=(g,), in_specs=[_group_spec(a.shape) for a in ins],
        out_specs=[_group_spec((g, nc, th)), _group_spec((g, nc, p2))],
        out_shape=[jax.ShapeDtypeStruct((g, nc, th), BF16), jax.ShapeDtypeStruct((g, nc, p2), F32)],
        scratch_shapes=[pltpu.VMEM((nc, p2), F32), pltpu.VMEM((nc, p2), F32),
                        pltpu.VMEM((S5_T, 2 * th), F32), pltpu.VMEM((th, th), BF16)],
        compiler_params=_params("arbitrary"),
    )(*ins)


def _s5_core_bwd(name, u, dy, sprev, ops):
    kc, bend, cpow, a1, a2 = ops
    g, nc, th = u.shape
    t = S5_T
    p2 = bend.shape[-1]
    bend_b, cpow_b = bend.astype(BF16), cpow.astype(BF16)
    cpow_s = _swap_halves(cpow_b, 1)
    a2s = _swap_halves(a2, 2)
    idx = jnp.arange(th)
    flip = (idx[:, None] // t == idx[None, :] // t) & (idx[:, None] % t == t - 1 - idx[None, :] % t)
    flip = flip.astype(BF16)

    def kern(u_ref, dy_ref, sp_ref, k_ref, b_ref, c_ref, cs_ref, a1_ref, a2_ref, a2s_ref, f_ref,
             du_ref, dk_ref, db_ref, dc_ref, da1_ref, da2_ref, g_scr, gs_scr, dx_scr, ext, toep, dtoep):
        _s5_expand_toeplitz(k_ref, ext, toep)
        ub, dyb = u_ref[0].astype(BF16), dy_ref[0].astype(BF16)
        dtoep[...] = _dot(_dot(ub, f_ref[...]).astype(BF16), dyb, "tn")
        for hin in range(S5_H):
            ext[:, :th] = dtoep[hin * t:(hin + 1) * t, :]
            rolled = pltpu.roll(ext[...], 0, 1, stride=S5_H, stride_axis=0)
            rolled = pltpu.roll(rolled, 2 * th - S5_H * (t - 1), 1)
            dk_ref[0, hin:hin + 1, :] = _colsum(rolled)[:, :th]
        spv = sp_ref[0]
        dc_ref[0] = _dot(spv.astype(BF16), dyb, "tn")
        g_scr[...] = _dot(dyb, c_ref[0], "nt")
        gs_scr[...] = _dot(dyb, cs_ref[0], "nt")
        a1v, a2v, a2sv = a1_ref[0], a2_ref[0], a2s_ref[0]

        def step(k, carry):
            gr, grs, da1, da2 = carry
            c = nc - 1 - k
            dx_scr[pl.ds(c, 1), :] = gr
            s_in = sp_ref[0, pl.ds(c, 1), :]
            da1 = da1 + gr * s_in
            da2 = da2 + grs * s_in
            gr_new = g_scr[pl.ds(c, 1), :] + a1v * gr + a2sv * grs
            grs_new = gs_scr[pl.ds(c, 1), :] + a1v * grs + a2v * gr
            return gr_new, grs_new, da1, da2

        zero = jnp.zeros((1, p2), F32)
        _, _, da1, da2 = lax.fori_loop(0, nc, step, (zero, zero, zero, zero))
        da1_ref[0] = da1
        da2_ref[0] = da2
        dxb = dx_scr[...].astype(BF16)
        db_ref[0] = _dot(ub, dxb, "tn")
        du_ref[0] = (_dot(dyb, toep[...], "nt") + _dot(dxb, b_ref[0], "nt")).astype(BF16)

    ins = [u, dy, sprev, kc, bend_b, cpow_b, cpow_s, a1, a2, a2s]
    outs = [(g, nc, th), (g, S5_H, th), (g, th, p2), (g, p2, th), (g, 1, p2), (g, 1, p2)]
    out_types = [BF16] + [F32] * (len(outs) - 1)
    return pl.pallas_call(
        kern, name=name, grid=(g,),
        in_specs=[_group_spec(a.shape) for a in ins] + [pl.BlockSpec((th, th), lambda gi: (0, 0))],
        out_specs=[_group_spec(s) for s in outs],
        out_shape=[jax.ShapeDtypeStruct(s, dt) for s, dt in zip(outs, out_types)],
        scratch_shapes=[pltpu.VMEM((nc, p2), F32), pltpu.VMEM((nc, p2), F32), pltpu.VMEM((nc, p2), F32),
                        pltpu.VMEM((t, 2 * th), F32), pltpu.VMEM((th, th), BF16), pltpu.VMEM((th, th), F32)],
        compiler_params=_params("arbitrary"),
    )(*ins, flip)


def _s5_to_groups(u, channel_major):
    rows, w = u.shape
    g = w // S5_H
    nc = rows // S5_T
    perm = (2, 0, 3, 1) if channel_major else (2, 0, 1, 3)
    return u.reshape(nc, S5_T, g, S5_H).transpose(perm).reshape(g, nc, S5_T * S5_H)


def _s5_from_groups(y, channel_major):
    g, nc, _ = y.shape
    if channel_major:
        return y.reshape(g, nc, S5_H, S5_T).transpose(1, 3, 0, 2).reshape(nc * S5_T, g * S5_H)
    return y.reshape(g, nc, S5_T, S5_H).transpose(1, 2, 0, 3).reshape(nc * S5_T, g * S5_H)


def _softplus(x):
    return jnp.maximum(x, 0.0) + jnp.log(1.0 + jnp.exp(-jnp.abs(x)))


def _ssd_chunk_prep(dtraw_ref, dtb_ref, a_ref, cst, dtt, lastt, n_heads):
    q = M2_Q
    lane = lax.broadcasted_iota(jnp.int32, (q, LANES), 1)
    dt = jnp.where(lane < n_heads, _softplus(dtraw_ref[...] + dtb_ref[...]), 0.0)
    adt = dt * a_ref[...]
    row = lax.broadcasted_iota(jnp.int32, (q, q), 0)
    col = lax.broadcasted_iota(jnp.int32, (q, q), 1)
    cs = _dot(jnp.where(row >= col, 1.0, 0.0), adt, precision=HIGHEST)
    cst[...] = cs.T
    dtt[...] = dt.T
    lastt[...] = jnp.broadcast_to(_colsum(adt), (q, LANES)).T
    return dt


def _pair_tables(cst, dtt, lastt, p):
    q = M2_Q
    out = []
    for hh in (2 * p, 2 * p + 1):
        rc = jnp.broadcast_to(cst[hh:hh + 1, :], (q, q))
        cc = rc.T
        dtc = jnp.broadcast_to(dtt[hh:hh + 1, :], (q, q)).T
        lb = jnp.broadcast_to(lastt[hh:hh + 1, :], (q, q))
        out.append((rc, cc, dtc, lb))
    return out


def _ssd_pair_fwd(x, bm, cm, cb, hs, tabs):
    q = M2_Q
    row = lax.broadcasted_iota(jnp.int32, (q, q), 0)
    col = lax.broadcasted_iota(jnp.int32, (q, q), 1)
    causal = row >= col
    lo = col < M2_P
    slo = row < M2_P
    (rc0, cc0, dtc0, lb0), (rc1, cc1, dtc1, lb1) = tabs
    l0 = jnp.where(causal, jnp.exp(jnp.where(causal, cc0 - rc0, 0.0)), 0.0)
    l1 = jnp.where(causal, jnp.exp(jnp.where(causal, cc1 - rc1, 0.0)), 0.0)
    m0, m1 = cb * l0, cb * l1
    dtp = jnp.where(lo, dtc0, dtc1)
    xdt = x * dtp
    xdt0 = jnp.where(lo, xdt, 0.0)
    xdt1 = jnp.where(lo, 0.0, xdt)
    e = jnp.where(lo, jnp.exp(cc0), jnp.exp(cc1))
    z = _bdot(cm, hs, "nt")
    yoff = z * e
    dec = jnp.where(lo, jnp.exp(lb0 - cc0), jnp.exp(lb1 - cc1))
    xdd = xdt * dec
    cd = jnp.where(slo, jnp.exp(lb0), jnp.exp(lb1))
    return dict(l0=l0, l1=l1, m0=m0, m1=m1, dtp=dtp, xdt=xdt, xdt0=xdt0, xdt1=xdt1, e=e, yoff=yoff,
                dec=dec, xdd=xdd, cd=cd, lo=lo, slo=slo)


def _ssd_fwd(name, xbc, dtraw, dtb, arow, dvec, n_heads):
    rows, c = xbc.shape
    q, n = M2_Q, M2_N
    di = n_heads * M2_P
    n_pairs = n_heads // 2
    ppg = n_pairs // M2_G
    nc = rows // q

    def kern(xbc_ref, dtraw_ref, dtb_ref, a_ref, d_ref, y_ref, prev_ref, state, cst, dtt, lastt):
        @pl.when(pl.program_id(0) == 0)
        def _():
            state[...] = jnp.zeros_like(state)

        _ssd_chunk_prep(dtraw_ref, dtb_ref, a_ref, cst, dtt, lastt, n_heads)
        for p in range(n_pairs):
            gi = p // ppg
            sl = slice(p * LANES, (p + 1) * LANES)
            x = xbc_ref[:, sl]
            bm = xbc_ref[:, di + gi * n:di + (gi + 1) * n]
            cm = xbc_ref[:, di + (M2_G + gi) * n:di + (M2_G + gi + 1) * n]
            if p % ppg == 0:
                cb = _bdot(cm, bm, "nt")
            hs = state[p]
            f = _ssd_pair_fwd(x, bm, cm, cb, hs, _pair_tables(cst, dtt, lastt, p))
            ydiag = _bdot(f["m0"], f["xdt0"]) + _bdot(f["m1"], f["xdt1"])
            y_ref[:, sl] = ydiag + f["yoff"] + d_ref[:, sl] * x
            prev_ref[0, p] = hs
            state[p] = f["cd"] * hs + _bdot(f["xdd"], bm, "tn")

    def whole(a):
        return pl.BlockSpec(a.shape, lambda i: (0, 0))

    return pl.pallas_call(
        kern, name=name, grid=(nc,),
        in_specs=[pl.BlockSpec((q, c), lambda i: (i, 0)), pl.BlockSpec((q, LANES), lambda i: (i, 0)),
                  whole(dtb), whole(arow), whole(dvec)],
        out_specs=[pl.BlockSpec((q, di), lambda i: (i, 0)),
                   pl.BlockSpec((1, n_pairs, 2 * M2_P, n), lambda i: (i, 0, 0, 0))],
        out_shape=[jax.ShapeDtypeStruct((rows, di), F32),
                   jax.ShapeDtypeStruct((nc, n_pairs, 2 * M2_P, n), F32)],
        scratch_shapes=[pltpu.VMEM((n_pairs, 2 * M2_P, n), F32), pltpu.VMEM((LANES, q), F32),
                        pltpu.VMEM((LANES, q), F32), pltpu.VMEM((LANES, q), F32)],
        compiler_params=_params("arbitrary"),
    )(xbc, dtraw, dtb, arow, dvec)


def _ssd_bwd(name, xbc, dtraw, dy, prev, dtb, arow, dvec, seg, n_heads):
    rows, c = xbc.shape
    q, n = M2_Q, M2_N
    di = n_heads * M2_P
    n_pairs = n_heads // 2
    ppg = n_pairs // M2_G
    nc = rows // q

    def kern(xbc_ref, dtraw_ref, dy_ref, prev_ref, dtb_ref, a_ref, d_ref, seg_ref,
             dxbc_ref, ddt_ref, da_ref, ddtb_ref, dd_ref,
             dstate, cst, dtt, lastt, dcst, wx, colterm, ddfull):
        step = pl.program_id(0)

        @pl.when(step == 0)
        def _():
            dstate[...] = jnp.zeros_like(dstate)
            ddfull[...] = jnp.zeros_like(ddfull)
            da_ref[...] = jnp.zeros_like(da_ref)
            ddtb_ref[...] = jnp.zeros_like(ddtb_ref)
            dd_ref[...] = jnp.zeros_like(dd_ref)

        dt = _ssd_chunk_prep(dtraw_ref, dtb_ref, a_ref, cst, dtt, lastt, n_heads)
        dcst[...] = jnp.zeros_like(dcst)
        lane_q = lax.broadcasted_iota(jnp.int32, (1, q), 1)
        last_hot = jnp.where(lane_q == q - 1, 1.0, 0.0)

        def total(v):
            return jnp.sum(jnp.sum(v, axis=1, keepdims=True), axis=0, keepdims=True)

        for gi in range(M2_G):
            bm = xbc_ref[:, di + gi * n:di + (gi + 1) * n]
            cm = xbc_ref[:, di + (M2_G + gi) * n:di + (M2_G + gi + 1) * n]
            cb = _bdot(cm, bm, "nt")
            dcb = jnp.zeros((q, q), F32)
            dbm = jnp.zeros((q, n), F32)
            dcm = jnp.zeros((q, n), F32)
            for p in range(gi * ppg, (gi + 1) * ppg):
                sl = slice(p * LANES, (p + 1) * LANES)
                x = xbc_ref[:, sl]
                dyp = dy_ref[:, sl]
                hs = prev_ref[0, p]
                ds = dstate[p]
                f = _ssd_pair_fwd(x, bm, cm, cb, hs, _pair_tables(cst, dtt, lastt, p))
                lo, slo = f["lo"], f["slo"]
                ddfull[:, sl] += _colsum(dyp * x)
                dy0 = jnp.where(lo, dyp, 0.0)
                dy1 = jnp.where(lo, 0.0, dyp)
                dm0 = _bdot(dyp, f["xdt0"], "nt")
                dm1 = _bdot(dyp, f["xdt1"], "nt")
                dxdt = _bdot(f["m0"], dy0, "tn") + _bdot(f["m1"], dy1, "tn")
                dcb = dcb + dm0 * f["l0"] + dm1 * f["l1"]
                w0, w1 = dm0 * f["m0"], dm1 * f["m1"]
                dz = dyp * f["e"]
                dcm = dcm + _bdot(dz, hs)
                dhs = _bdot(dz, cm, "tn") + f["cd"] * ds
                tot = ds * hs * f["cd"]
                dxdd = _bdot(bm, ds, "nt")
                dbm = dbm + _bdot(f["xdd"], ds)
                ee = dxdd * f["xdd"]
                colterm[:, sl] = dyp * f["yoff"] - ee
                dxdt = dxdt + dxdd * f["dec"]
                t_all = total(tot)
                t_lo = total(jnp.where(slo, tot, 0.0))
                e_all = total(ee)
                e_lo = total(jnp.where(lo, ee, 0.0))
                dlast0 = t_lo + e_lo
                dlast1 = (t_all - t_lo) + (e_all - e_lo)
                dcst[2 * p:2 * p + 1, :] = _colsum(w0.T - w0) + dlast0 * last_hot
                dcst[2 * p + 1:2 * p + 2, :] = _colsum(w1.T - w1) + dlast1 * last_hot
                dxbc_ref[:, sl] = d_ref[:, sl] * dyp + dxdt * f["dtp"]
                wx[:, sl] = dxdt * x
                dstate[p] = dhs
            dcm = dcm + _bdot(dcb, bm)
            dbm = dbm + _bdot(dcb, cm, "tn")
            dxbc_ref[:, di + gi * n:di + (gi + 1) * n] = dbm
            dxbc_ref[:, di + (M2_G + gi) * n:di + (M2_G + gi + 1) * n] = dcm

        segv = seg_ref[...]
        dcs = _dot(colterm[...], segv, precision=HIGHEST) + dcst[...].T
        row = lax.broadcasted_iota(jnp.int32, (q, q), 0)
        col = lax.broadcasted_iota(jnp.int32, (q, q), 1)
        ddelta = _dot(jnp.where(col >= row, 1.0, 0.0), dcs, precision=HIGHEST)
        ddt = _dot(wx[...], segv, precision=HIGHEST) + ddelta * a_ref[...]
        da_ref[...] += _colsum(ddelta * dt)
        lane = lax.broadcasted_iota(jnp.int32, (q, LANES), 1)
        ddtraw = jnp.where(lane < n_heads, ddt * _sigmoid(dtraw_ref[...] + dtb_ref[...]), 0.0)
        ddt_ref[...] = ddtraw
        ddtb_ref[...] += _colsum(ddtraw)

        @pl.when(step == nc - 1)
        def _():
            dd_ref[...] = _dot(jnp.broadcast_to(ddfull[...], (8, di)), segv, precision=HIGHEST)

    def whole(a):
        return pl.BlockSpec(a.shape, lambda i: (0, 0))

    def rev(i):
        return nc - 1 - i

    acc = jax.ShapeDtypeStruct((1, LANES), F32)
    acc_spec = pl.BlockSpec((1, LANES), lambda i: (0, 0))
    acc8 = jax.ShapeDtypeStruct((8, LANES), F32)
    acc8_spec = pl.BlockSpec((8, LANES), lambda i: (0, 0))
    return pl.pallas_call(
        kern, name=name, grid=(nc,),
        in_specs=[pl.BlockSpec((q, c), lambda i: (rev(i), 0)), pl.BlockSpec((q, LANES), lambda i: (rev(i), 0)),
                  pl.BlockSpec((q, di), lambda i: (rev(i), 0)),
                  pl.BlockSpec((1, n_pairs, 2 * M2_P, n), lambda i: (rev(i), 0, 0, 0)),
                  whole(dtb), whole(arow), whole(dvec), whole(seg)],
        out_specs=[pl.BlockSpec((q, c), lambda i: (rev(i), 0)), pl.BlockSpec((q, LANES), lambda i: (rev(i), 0)),
                   acc_spec, acc_spec, acc8_spec],
        out_shape=[jax.ShapeDtypeStruct((rows, c), F32), jax.ShapeDtypeStruct((rows, LANES), F32), acc, acc, acc8],
        scratch_shapes=[pltpu.VMEM((n_pairs, 2 * M2_P, n), F32), pltpu.VMEM((LANES, q), F32),
                        pltpu.VMEM((LANES, q), F32), pltpu.VMEM((LANES, q), F32), pltpu.VMEM((LANES, q), F32),
                        pltpu.VMEM((q, di), F32), pltpu.VMEM((q, di), F32), pltpu.VMEM((1, di), F32)],
        compiler_params=_params("arbitrary"),
    )(xbc, dtraw, dy, prev, dtb, arow, dvec, seg)


S5_PARAM_NAMES = ("s5_lambda_re", "s5_lambda_im", "s5_log_dt", "s5_b_re", "s5_b_im", "s5_c_re", "s5_c_im")


def _row(v):
    return v.reshape(1, -1)


def _s5_layer_fwd(tag, x, gate, h, w, j):
    u = _matmul(tag + "_win", h, w["s5_w_in"][j])
    params = [w[k][j] for k in S5_PARAM_NAMES]
    ops, build_vjp = jax.vjp(_s5_build, *params)
    ug = _s5_to_groups(u.astype(BF16), True)
    yg, sprev = _s5_core_fwd(tag + "_core", ug, ops)
    yy = _s5_from_groups(yg, False)
    skip = _row(w["s5_d"][j])
    gl = _gelu_fwd(tag + "_gelu", yy, u, skip)
    ab = _matmul(tag + "_wglu", gl, w["s5_w_glu"][j])
    x1 = _glu_fwd(tag + "_glu", ab, _row(w["s5_b_glu"][j]), x, gate)
    return x1, dict(u=u, ug=ug, ops=ops, build_vjp=build_vjp, sprev=sprev, yy=yy, gl=gl, ab=ab, skip=skip)


def _s5_layer_bwd(tag, dx1, gate, h, sv, w, j):
    dab, db_glu, dgate = _glu_bwd(tag + "_glu_b", sv["ab"], _row(w["s5_b_glu"][j]), dx1, gate)
    dw_glu = _matmul(tag + "_dwglu", sv["gl"], dab, "tn")
    dgl = _matmul(tag + "_dgl", dab, w["s5_w_glu"][j], "nt")
    dyy, dskip = _gelu_bwd(tag + "_gelu_b", sv["yy"], sv["u"], sv["skip"], dgl)
    dug, dkc, dbend, dcpow, da1, da2s = _s5_core_bwd(
        tag + "_core_b", sv["ug"], _s5_to_groups(dyy.astype(BF16), False), sv["sprev"], sv["ops"])
    dparams = sv["build_vjp"]((dkc, dbend, dcpow, da1, _swap_halves(da2s, 2)))
    du = _axpy(tag + "_du", _s5_from_groups(dug, True), dyy, sv["skip"])
    grads = dict(zip(S5_PARAM_NAMES, dparams))
    grads["s5_d"] = dskip.reshape(-1)
    grads["s5_w_in"] = _matmul(tag + "_dwin", h, du, "tn")
    grads["s5_w_glu"] = dw_glu
    grads["s5_b_glu"] = db_glu.reshape(-1)
    dh = _matmul(tag + "_dh", du, w["s5_w_in"][j], "nt")
    return dh, grads, dgate


def _ssd_consts(w, j, d_model):
    di = 2 * d_model
    heads = di // M2_P

    def pad_row(v):
        return jnp.zeros((1, LANES), F32).at[0, :heads].set(v)

    a = -jnp.exp(w["m2_a_log"][j])
    seg = (jnp.arange(di)[:, None] // M2_P == jnp.arange(LANES)[None, :]).astype(F32)
    w_in = w["m2_w_in"][j]
    conv_dim = di + 2 * M2_G * M2_N
    w_dt = jnp.zeros((d_model, LANES), w_in.dtype).at[:, :heads].set(w_in[:, di + conv_dim:])
    return dict(di=di, heads=heads, conv_dim=conv_dim, a=a, arow=pad_row(a), dtb=pad_row(w["m2_dt_bias"][j]),
                dvec=_row(jnp.repeat(w["m2_d"][j], M2_P)), seg=seg,
                w_z=w_in[:, :di], w_xbc=w_in[:, di:di + conv_dim], w_dt=w_dt,
                conv_w=w["m2_conv_w"][j], conv_b=_row(w["m2_conv_b"][j]), norm_g=_row(w["m2_norm_g"][j]))


def _gated_out_bwd(tag, act, dxo, w_out, gate, **kw):
    dw, dgate_parts = _matmul(tag + "_dwo", act, dxo, "tn", colscale=gate, colsum_with=w_out, **kw)
    dgate = jnp.sum(dgate_parts, axis=0)
    return dw, dgate, _scale_cols(tag + "_wog", w_out, gate)


def _ssd_layer_fwd(tag, x, gate, h, w, j):
    k = _ssd_consts(w, j, h.shape[1])
    z = _matmul(tag + "_wz", h, k["w_z"])
    xbc_pre = _matmul(tag + "_wxbc", h, k["w_xbc"])
    dtraw = _matmul(tag + "_wdt", h, k["w_dt"])
    xbc = _conv_fwd(tag + "_conv", xbc_pre, k["conv_w"], k["conv_b"])
    y, prev = _ssd_fwd(tag + "_core", xbc, dtraw, k["dtb"], k["arow"], k["dvec"], k["heads"])
    yn = _gatenorm_fwd(tag + "_gn", y, z, k["norm_g"])
    x1 = _matmul(tag + "_wout", yn, w["m2_w_out"][j], colscale=gate, addin=x)
    return x1, dict(k=k, z=z, xbc_pre=xbc_pre, dtraw=dtraw, xbc=xbc, y=y, prev=prev, yn=yn)


def _ssd_layer_bwd(tag, dx1, gate, h, sv, w, j):
    k = sv["k"]
    heads = k["heads"]
    dw_out, dgate, wog = _gated_out_bwd(tag, sv["yn"], dx1, w["m2_w_out"][j], gate)
    grads = {"m2_w_out": dw_out}
    dyn = _matmul(tag + "_dyn", dx1, wog, "nt")
    dyssd, dz, dng = _gatenorm_bwd(tag + "_gn_b", sv["y"], sv["z"], k["norm_g"], dyn)
    dxbc, ddtraw, da, ddtb, dd = _ssd_bwd(tag + "_core_b", sv["xbc"], sv["dtraw"], dyssd, sv["prev"],
                                          k["dtb"], k["arow"], k["dvec"], k["seg"], heads)
    dpre, dcw, dcb = _conv_bwd_pre(tag + "_conv_b1", sv["xbc_pre"], dxbc, k["conv_w"], k["conv_b"])
    dxbc_pre = _conv_bwd_in(tag + "_conv_b2", dpre, k["conv_w"])
    dw_z = _matmul(tag + "_dwz", h, dz, "tn")
    dw_xbc = _matmul(tag + "_dwxbc", h, dxbc_pre, "tn")
    dw_dt = _matmul(tag + "_dwdt", h, ddtraw, "tn")
    dh = _matmul(tag + "_dh1", dz, k["w_z"], "nt")
    dh = _matmul(tag + "_dh2", dxbc_pre, k["w_xbc"], "nt", addin=dh)
    dh = _matmul(tag + "_dh3", ddtraw, k["w_dt"], "nt", addin=dh)
    grads["m2_w_in"] = jnp.concatenate([dw_z, dw_xbc, dw_dt[:, :heads]], axis=1)
    grads["m2_conv_w"] = dcw
    grads["m2_conv_b"] = dcb.reshape(-1)
    grads["m2_dt_bias"] = ddtb[0, :heads]
    grads["m2_a_log"] = da[0, :heads] * k["a"]
    grads["m2_d"] = dd[0, :heads]
    grads["m2_norm_g"] = dng.reshape(-1)
    return dh, grads, dgate


def _layer_fwd(li, x, mod, w):
    tag = "L%d" % li
    sh1, sc1, g1, sh2, sc2, g2 = mod
    j = li // 2
    h = _normmod_fwd(tag + "_nm1", x, _row(w["norm_mix_g"][li]), sh1, sc1)
    if li % 2 == 0:
        x1, mix = _s5_layer_fwd(tag + "_s5", x, g1, h, w, j)
    else:
        x1, mix = _ssd_layer_fwd(tag + "_m2", x, g1, h, w, j)
    h2 = _normmod_fwd(tag + "_nm2", x1, _row(w["norm_mlp_g"][li]), sh2, sc2)
    r = _matmul(tag + "_w1", h2, w["mlp_w1"][li], relu=True, out_dtype=BF16)
    x2 = _matmul(tag + "_w2", r, w["mlp_w2"][li], square_a=True, colscale=g2, addin=x1)
    return x2, dict(x=x, h=h, mix=mix, x1=x1, h2=h2, r=r)


def _layer_bwd(li, dx2, sv, mod, w):
    tag = "L%d" % li
    sh1, sc1, g1, sh2, sc2, g2 = mod
    j = li // 2
    dw2, dg2, w2g = _gated_out_bwd(tag + "_mlp", sv["r"], dx2, w["mlp_w2"][li], g2, square_a=True)
    grads = {"mlp_w2": dw2}
    dr = _matmul(tag + "_dr", dx2, w2g, "nt", out_dtype=BF16, mul2=sv["r"])
    grads["mlp_w1"] = _matmul(tag + "_dw1", sv["h2"], dr, "tn")
    dh2 = _matmul(tag + "_dh2", dr, w["mlp_w1"][li], "nt")
    dx1, dgm, dsh2, dsc2 = _normmod_bwd(tag + "_nm2_b", sv["x1"], _row(w["norm_mlp_g"][li]), sh2, sc2, dh2, dx2)
    if li % 2 == 0:
        dh, mix_grads, dg1 = _s5_layer_bwd(tag + "_s5", dx1, g1, sv["h"], sv["mix"], w, j)
    else:
        dh, mix_grads, dg1 = _ssd_layer_bwd(tag + "_m2", dx1, g1, sv["h"], sv["mix"], w, j)
    dx, dgx, dsh1, dsc1 = _normmod_bwd(tag + "_nm1_b", sv["x"], _row(w["norm_mix_g"][li]), sh1, sc1, dh, dx1)
    grads["norm_mix_g"] = dgx.reshape(-1)
    grads["norm_mlp_g"] = dgm.reshape(-1)
    dmod = jnp.concatenate([dsh1, dsc1, dg1, dsh2, dsc2, dg2], axis=1)
    return dx, grads, mix_grads, dmod


STACKED = ("norm_mix_g", "norm_mlp_g", "mlp_w1", "mlp_w2")
S5_NAMES = S5_PARAM_NAMES + ("s5_d", "s5_w_in", "s5_w_glu", "s5_b_glu")
M2_NAMES = ("m2_w_in", "m2_conv_w", "m2_conv_b", "m2_dt_bias", "m2_a_log", "m2_d", "m2_norm_g", "m2_w_out")


def _local_step(x, target, mods, w):
    depth = w["norm_mix_g"].shape[0]
    d = x.shape[1]
    saved = []
    mod_rows = []
    for li in range(depth):
        mod = [mods[li:li + 1, i * d:(i + 1) * d] for i in range(N_MOD)]
        mod_rows.append(mod)
        x, sv = _layer_fwd(li, x, mod, w)
        saved.append(sv)
    dx, dgf, loss = _loss_head("loss_head", x, target, _row(w["final_norm_g"]))
    per_layer = {k: [None] * depth for k in STACKED}
    s5 = {k: [None] * (depth - depth // 2) for k in S5_NAMES}
    m2 = {k: [None] * (depth // 2) for k in M2_NAMES}
    dmods = [None] * depth
    for li in reversed(range(depth)):
        dx, grads, mix_grads, dmods[li] = _layer_bwd(li, dx, saved[li], mod_rows[li], w)
        for k, v in grads.items():
            per_layer[k][li] = v
        for k, v in mix_grads.items():
            (s5 if li % 2 == 0 else m2)[k][li // 2] = v
    out = {k: jnp.stack(v) for k, v in {**per_layer, **s5, **m2}.items()}
    out["final_norm_g"] = dgf.reshape(-1)
    return loss, dx, out, jnp.concatenate(dmods, axis=0)


ANY = pl.BlockSpec(memory_space=pl.ANY)
N_DEV = 8
N_CHIP = 4


def _coords():
    return lax.axis_index("x"), lax.axis_index("y"), lax.axis_index("c")


def _allgather8(name, block):
    r, wd = block.shape

    def body(x_ref, out_ref, send_sems, recv_sems, local_sem):
        x, y, c = _coords()
        me, sibling = (x, y, c), (x, y, 1 - c)
        chips = [(1 - x, y), (x, 1 - y), (1 - x, 1 - y)]

        def slot(px, py, pc):
            return out_ref.at[4 * px + 2 * py + pc]

        def copy(k, blk, to, src=None):
            return pltpu.make_async_remote_copy(
                src_ref=slot(*blk) if src is None else src, dst_ref=slot(*blk),
                send_sem=send_sems.at[k], recv_sem=recv_sems.at[k], device_id=to, device_id_type=MESH)

        mine = pltpu.make_async_copy(x_ref, slot(*me), local_sem)
        mine.start()
        first = [copy(0, me, sibling, src=x_ref)]
        first += [copy(1 + j, me, (*chip, c), src=x_ref) for j, chip in enumerate(chips)]
        for cp in first:
            cp.start()
        passed = [copy(4 + j, (*chip, c), sibling) for j, chip in enumerate(chips)]
        for j, chip in enumerate(chips):
            copy(1 + j, (*chip, c), me).wait_recv()
            passed[j].start()
        copy(0, sibling, me).wait_recv()
        for j, chip in enumerate(chips):
            copy(4 + j, (*chip, 1 - c), me).wait_recv()
        for cp in first + passed:
            cp.wait_send()
        mine.wait()

    return pl.pallas_call(
        body, name=name, in_specs=[ANY], out_specs=ANY,
        out_shape=jax.ShapeDtypeStruct((N_DEV, r, wd), block.dtype),
        scratch_shapes=[pltpu.SemaphoreType.DMA((7,)), pltpu.SemaphoreType.DMA((7,)), pltpu.SemaphoreType.DMA],
    )(block)


def _exchange4(name, pieces, same):
    r, wd = pieces.shape[-2:]

    def body(x_ref, out_ref, send_sems, recv_sems, local_sem):
        x, y, c = _coords()
        my_chip = 2 * x + y
        chips = [(1 - x, y), (x, 1 - y), (1 - x, 1 - y)]

        def src(k):
            return x_ref if same else x_ref.at[k]

        mine = pltpu.make_async_copy(src(my_chip), out_ref.at[my_chip], local_sem)
        mine.start()
        sends = []
        for j, (px, py) in enumerate(chips):
            cp = pltpu.make_async_remote_copy(
                src_ref=src(2 * px + py), dst_ref=out_ref.at[my_chip],
                send_sem=send_sems.at[j], recv_sem=recv_sems.at[j], device_id=(px, py, c), device_id_type=MESH)
            cp.start()
            sends.append(cp)
        for j, (px, py) in enumerate(chips):
            pltpu.make_async_remote_copy(
                src_ref=src(my_chip), dst_ref=out_ref.at[2 * px + py],
                send_sem=send_sems.at[j], recv_sem=recv_sems.at[j], device_id=(px, py, c),
                device_id_type=MESH).wait_recv()
        for cp in sends:
            cp.wait_send()
        mine.wait()

    return pl.pallas_call(
        body, name=name, in_specs=[ANY], out_specs=ANY,
        out_shape=jax.ShapeDtypeStruct((N_CHIP, r, wd), pieces.dtype),
        scratch_shapes=[pltpu.SemaphoreType.DMA((3,)), pltpu.SemaphoreType.DMA((3,)), pltpu.SemaphoreType.DMA],
    )(pieces)


def _swap_sibling(name, block):
    def body(x_ref, out_ref, send_sem, recv_sem):
        x, y, c = _coords()
        cp = pltpu.make_async_remote_copy(src_ref=x_ref, dst_ref=out_ref, send_sem=send_sem, recv_sem=recv_sem,
                                          device_id=(x, y, 1 - c), device_id_type=MESH)
        cp.start()
        cp.wait()

    return pl.pallas_call(
        body, name=name, in_specs=[ANY], out_specs=ANY, out_shape=jax.ShapeDtypeStruct(block.shape, block.dtype),
        scratch_shapes=[pltpu.SemaphoreType.DMA, pltpu.SemaphoreType.DMA],
    )(block)


def _sum_slots(name, stacked):
    n, r, wd = stacked.shape
    tile = min(FLAT_ROWS, r)

    def kern(x_ref, o_ref):
        acc = x_ref[0].astype(F32)
        for s in range(1, n):
            acc = acc + x_ref[s].astype(F32)
        o_ref[...] = acc

    return pl.pallas_call(
        kern, name=name, grid=(r // tile,), in_specs=[pl.BlockSpec((n, tile, wd), lambda i: (0, i, 0))],
        out_specs=pl.BlockSpec((tile, wd), lambda i: (i, 0)), out_shape=jax.ShapeDtypeStruct((r, wd), F32),
        compiler_params=_params("parallel"),
    )(stacked)


def _adamw(name, w, m, v, g, g2=None):
    r, wd = w.shape
    grads = [g] if g2 is None else [g, g2]
    c1 = 1.0 - ADAM_B1 ** ADAM_STEP
    c2 = 1.0 - ADAM_B2 ** ADAM_STEP

    def body(i, *refs):
        w_ref, m_ref, v_ref = refs[:3]
        g_refs = refs[3:3 + len(grads)]
        go_ref, d_ref, mo_ref, vo_ref = refs[3 + len(grads):]
        gv = g_refs[0][...]
        if g2 is not None:
            gv = gv + g_refs[1][...]
        mn = ADAM_B1 * m_ref[...] + (1.0 - ADAM_B1) * gv
        vn = ADAM_B2 * v_ref[...] + (1.0 - ADAM_B2) * (gv * gv)
        go_ref[...] = gv
        mo_ref[...] = mn
        vo_ref[...] = vn
        d_ref[...] = -ADAM_LR * ((mn / c1) / (jnp.sqrt(vn / c2) + ADAM_EPS) + ADAM_WD * w_ref[...])

    return _rowcall(name, body, r, FLAT_ROWS, [w, m, v] + grads, [], [(wd, F32)] * 4, [])


FLAT_BLOCK = FLAT_ROWS * FLAT_W


def _pack(arrays, dtype):
    flat = jnp.concatenate([a.reshape(-1).astype(dtype) for a in arrays])
    pad = (-flat.shape[0]) % FLAT_BLOCK
    return jnp.pad(flat, (0, pad)).reshape(-1, FLAT_W)


def _unpack(buf, shapes):
    flat = buf.reshape(-1)
    out, off = [], 0
    for s in shapes:
        n = math.prod(s)
        out.append(flat[off:off + n].reshape(s))
        off += n
    return out


SHARDED_BIG = {"mlp_w1": 2, "mlp_w2": 1, "s5_w_in": 1, "s5_w_glu": 2, "m2_w_in": 2, "m2_w_out": 1}
SHARDED_SMALL = {"m2_conv_w": 2, "m2_conv_b": 1, "m2_norm_g": 1}
REPLICATED = ("ada_b", "norm_mix_g", "norm_mlp_g", "s5_lambda_re", "s5_lambda_im", "s5_log_dt", "s5_b_re",
              "s5_b_im", "s5_c_re", "s5_c_im", "s5_d", "s5_b_glu", "m2_dt_bias", "m2_a_log", "m2_d", "final_norm_g")
WEIGHT_NAMES = ("ada_w", "ada_b", "norm_mix_g", "norm_mlp_g", "mlp_w1", "mlp_w2", "s5_w_in", "s5_lambda_re",
                "s5_lambda_im", "s5_log_dt", "s5_b_re", "s5_b_im", "s5_c_re", "s5_c_im", "s5_d", "s5_w_glu",
                "s5_b_glu", "m2_w_in", "m2_conv_w", "m2_conv_b", "m2_dt_bias", "m2_a_log", "m2_d", "m2_norm_g",
                "m2_w_out", "final_norm_g")


def _gather_weights(name, local, names_axes, dtype):
    names = list(names_axes)
    got = _exchange4(name, _pack([local[k] for k in names], dtype), same=True)
    per_chip = [_unpack(got[j], [local[k].shape for k in names]) for j in range(N_CHIP)]
    return {k: jnp.concatenate([per_chip[j][i] for j in range(N_CHIP)], axis=names_axes[k])
            for i, k in enumerate(names)}


def _chip_slice(a, chip, axis):
    size = a.shape[axis] // N_CHIP
    return lax.slice_in_dim(a, chip * size, (chip + 1) * size, axis=axis)


def kernel(x, c, ada_w, ada_b, norm_mix_g, norm_mlp_g, mlp_w1, mlp_w2, s5_w_in, s5_lambda_re, s5_lambda_im, s5_log_dt, s5_b_re, s5_b_im, s5_c_re, s5_c_im, s5_d, s5_w_glu, s5_b_glu, m2_w_in, m2_conv_w, m2_conv_b, m2_dt_bias, m2_a_log, m2_d, m2_norm_g, m2_w_out, final_norm_g, loss_target, m_ada_w, m_ada_b, m_norm_mix_g, m_norm_mlp_g, m_mlp_w1, m_mlp_w2, m_s5_w_in, m_s5_lambda_re, m_s5_lambda_im, m_s5_log_dt, m_s5_b_re, m_s5_b_im, m_s5_c_re, m_s5_c_im, m_s5_d, m_s5_w_glu, m_s5_b_glu, m_m2_w_in, m_m2_conv_w, m_m2_conv_b, m_m2_dt_bias, m_m2_a_log, m_m2_d, m_m2_norm_g, m_m2_w_out, m_final_norm_g, v_ada_w, v_ada_b, v_norm_mix_g, v_norm_mlp_g, v_mlp_w1, v_mlp_w2, v_s5_w_in, v_s5_lambda_re, v_s5_lambda_im, v_s5_log_dt, v_s5_b_re, v_s5_b_im, v_s5_c_re, v_s5_c_im, v_s5_d, v_s5_w_glu, v_s5_b_glu, v_m2_w_in, v_m2_conv_w, v_m2_conv_b, v_m2_dt_bias, v_m2_a_log, v_m2_d, v_m2_norm_g, v_m2_w_out, v_final_norm_g):
    args = locals()
    local = {k: args[k] for k in WEIGHT_NAMES}
    mom_m = {k: args["m_" + k] for k in WEIGHT_NAMES}
    mom_v = {k: args["v_" + k] for k in WEIGHT_NAMES}
    depth, d = norm_mix_g.shape
    xi, yi, ci = _coords()
    my_chip = 2 * xi + yi
    my_dev = 2 * my_chip + ci

    cond = jax.nn.silu(c).reshape(-1, LANES)
    cond_all = _allgather8("ag_cond", cond).reshape(N_DEV, d)
    cond_pad = jnp.zeros((LANES, d), F32).at[:N_DEV].set(cond_all)
    mod_cols = ada_w.shape[2]
    mod_part = jnp.stack([_matmul("ada_%d" % i, cond_pad, ada_w[i])[:N_DEV] for i in range(depth)])
    mod_all = _allgather8("ag_mod", mod_part.reshape(-1, LANES)).reshape(N_CHIP, 2, depth, N_DEV, mod_cols)[:, 0]
    mod_mine = lax.dynamic_index_in_dim(mod_all, my_dev, axis=2, keepdims=False)
    mods = jnp.transpose(mod_mine, (1, 0, 2)).reshape(depth, N_CHIP * mod_cols) + ada_b

    w = {k: local[k] for k in REPLICATED}
    w.update(_gather_weights("ag_w_big", local, SHARDED_BIG, BF16))
    w.update(_gather_weights("ag_w_small", local, SHARDED_SMALL, F32))

    loss_row, dx, grads, dmods = _local_step(x[0], loss_target[0], mods, w)
    grads["ada_b"] = dmods

    rep_shapes = [grads[k].shape for k in REPLICATED]
    rep_all = _allgather8("ag_grep", _pack([grads[k] for k in REPLICATED], F32))
    rep_sum = _sum_slots("sum_grep", rep_all)
    dmods_all = rep_all.reshape(N_DEV, -1)[:, :dmods.size].reshape(N_DEV, depth, N_CHIP * mod_cols)

    dm_mine = lax.dynamic_slice_in_dim(dmods_all, my_chip * mod_cols, mod_cols, axis=2)
    dm_pad = jnp.zeros((LANES, depth, mod_cols), F32).at[:N_DEV].set(dm_mine)
    g_ada_w = jnp.stack([_matmul("dada_%d" % i, cond_pad, dm_pad[:, i], "tn") for i in range(depth)])

    sh_names = list(SHARDED_BIG) + list(SHARDED_SMALL)
    sh_axes = {**SHARDED_BIG, **SHARDED_SMALL}
    pieces = jnp.stack([_pack([_chip_slice(grads[k], j, sh_axes[k]) for k in sh_names], BF16)
                        for j in range(N_CHIP)])
    landed = _exchange4("rs_grads", pieces, same=False)
    part = _sum_slots("sum_gsh", landed)
    other = _swap_sibling("swap_gsh", part)

    out_g, out_d, out_m, out_v = {}, {}, {}, {}
    sh_shapes = [local[k].shape for k in sh_names]
    res = _adamw("adam_sh", _pack([local[k] for k in sh_names], F32), _pack([mom_m[k] for k in sh_names], F32),
                 _pack([mom_v[k] for k in sh_names], F32), part, other)
    for dst, buf in zip((out_g, out_d, out_m, out_v), res):
        dst.update(zip(sh_names, _unpack(buf, sh_shapes)))
    res = _adamw("adam_rep", _pack([local[k] for k in REPLICATED], F32), _pack([mom_m[k] for k in REPLICATED], F32),
                 _pack([mom_v[k] for k in REPLICATED], F32), rep_sum)
    for dst, buf in zip((out_g, out_d, out_m, out_v), res):
        dst.update(zip(REPLICATED, _unpack(buf, rep_shapes)))
    flat2 = (-1, mod_cols)
    res = _adamw("adam_ada", ada_w.reshape(flat2), m_ada_w.reshape(flat2), v_ada_w.reshape(flat2),
                 g_ada_w.reshape(flat2))
    for dst, buf in zip((out_g, out_d, out_m, out_v), res):
        dst["ada_w"] = buf.reshape(ada_w.shape)

    loss = lax.psum(loss_row[0, 0], ("x", "y", "c"))
    outs = [loss, dx[None]]
    for dst in (out_g, out_d, out_m, out_v):
        outs += [dst[k] for k in WEIGHT_NAMES]
    return tuple(outs)
```

```python
import functools
import math

import jax
import jax.numpy as jnp
from jax import lax
from jax.experimental import pallas as pl
from jax.experimental.pallas import tpu as pltpu

F32 = jnp.float32
BF16 = jnp.bfloat16
HIGHEST = lax.Precision.HIGHEST

NORM_EPS = 1e-5
N_MOD = 6
S5_H, S5_P, S5_T = 16, 64, 64
M2_P, M2_N, M2_G, M2_Q, M2_K = 64, 128, 4, 128, 4
LANES = 128
ADAM_LR, ADAM_B1, ADAM_B2, ADAM_EPS, ADAM_WD, ADAM_STEP = 0.001, 0.9, 0.999, 1e-08, 0.01, 10
VMEM_LIMIT_BYTES = 48 * 1024 * 1024
ROW_TILE = 256
FLAT_W = 1024
FLAT_ROWS = 256
MESH = pl.DeviceIdType.MESH


def _params(*sem):
    return pltpu.CompilerParams(dimension_semantics=sem, vmem_limit_bytes=VMEM_LIMIT_BYTES)


def _dot(a, b, dn="nn", precision=None):
    dims = {"nn": ((1,), (0,)), "nt": ((1,), (1,)), "tn": ((0,), (0,))}[dn]
    return lax.dot_general(a, b, (dims, ((), ())), preferred_element_type=F32, precision=precision)


def _bdot(a, b, dn="nn"):
    return _dot(a.astype(BF16), b.astype(BF16), dn)


def _sigmoid(x):
    return jax.nn.sigmoid(x)


def _colsum(x):
    return jnp.sum(x, axis=0, keepdims=True)


def _pick_tile(dim, want):
    if dim <= want:
        return dim
    for t in range(want - want % LANES, 0, -LANES):
        if dim % t == 0:
            return t
    raise ValueError((dim, want))


def _matmul(name, a, b, mode="nn", out_dtype=F32, relu=False, square_a=False, mul2=None, colscale=None,
            addin=None, colsum_with=None, ride=None, tm=1024, tn=1024, tk=1024):
    if mode == "nn":
        (m, k), (k2, n) = a.shape, b.shape
    elif mode == "nt":
        (m, k), (n, k2) = a.shape, b.shape
    else:
        (k, m), (k2, n) = a.shape, b.shape
    assert k == k2, (name, a.shape, b.shape)
    tm, tn, tk = _pick_tile(m, tm), _pick_tile(n, tn), _pick_tile(k, tk)
    nk = k // tk
    tiles = [e for e in (mul2, addin, colsum_with) if e is not None]
    n_ext = len(tiles) + (colscale is not None)
    n_out = 1 + (colsum_with is not None)
    n_ride = 0 if ride is None else 1
    grid = (m // tm, n // tn, nk)

    def kern(*refs):
        a_ref, b_ref = refs[:2]
        e_refs = list(refs[2:2 + n_ext])
        o_refs = refs[2 + n_ext + n_ride:2 + n_ext + n_ride + n_out]
        kk = pl.program_id(2)
        if ride is not None:
            here = [pl.program_id(ax) for ax in range(3)]
            ride_refs = (refs[2 + n_ext], refs[2 + n_ext + n_ride + n_out]) + tuple(refs[-3:])
            first = (here[0] == 0) & (here[1] == 0) & (here[2] == 0)
            last = (here[0] == grid[0] - 1) & (here[1] == grid[1] - 1) & (here[2] == grid[2] - 1)

            @pl.when(first)
            def _():
                _exchange4_ops(*ride_refs, ride[1])[0]()

        av = a_ref[...]
        if square_a:
            av = av * av
        part = _bdot(av, b_ref[...], mode)

        def finish(r):
            ext = list(e_refs)
            m2v = ext.pop(0)[...].astype(F32) if mul2 is not None else None
            addv = ext.pop(0)[...].astype(F32) if addin is not None else None
            if colsum_with is not None:
                o_refs[1][0] = _colsum(r * ext.pop(0)[...].astype(F32))
            if relu:
                r = jnp.maximum(r, 0.0)
            if m2v is not None:
                r = r * (2.0 * m2v)
            if colscale is not None:
                r = r * ext.pop(0)[...]
            if addv is not None:
                r = r + addv
            o_refs[0][...] = r.astype(out_dtype)

        if nk == 1:
            finish(part)
        else:
            acc = refs[-4] if ride is not None else refs[-1]

            @pl.when(kk == 0)
            def _():
                acc[...] = part

            @pl.when(kk > 0)
            def _():
                acc[...] += part

            @pl.when(kk == nk - 1)
            def _():
                finish(acc[...])

        if ride is not None:
            @pl.when(last)
            def _():
                _exchange4_ops(*ride_refs, ride[1])[1]()

    if mode == "tn":
        a_spec = pl.BlockSpec((tk, tm), lambda i, j, kk: (kk, i))
    else:
        a_spec = pl.BlockSpec((tm, tk), lambda i, j, kk: (i, kk))
    if mode == "nt":
        b_spec = pl.BlockSpec((tn, tk), lambda i, j, kk: (j, kk))
    else:
        b_spec = pl.BlockSpec((tk, tn), lambda i, j, kk: (kk, j))
    o_spec = pl.BlockSpec((tm, tn), lambda i, j, kk: (i, j))
    in_specs = [a_spec, b_spec] + [o_spec] * len(tiles)
    operands = [a, b] + tiles
    if colscale is not None:
        in_specs.append(pl.BlockSpec((1, tn), lambda i, j, kk: (0, j)))
        operands.append(colscale)
    out_specs = [o_spec]
    out_shape = [jax.ShapeDtypeStruct((m, n), out_dtype)]
    if colsum_with is not None:
        out_specs.append(pl.BlockSpec((1, 1, tn), lambda i, j, kk: (i, 0, j)))
        out_shape.append(jax.ShapeDtypeStruct((m // tm, 1, n), F32))
    scratch = [pltpu.VMEM((tm, tn), F32)] if nk > 1 else []
    semantics = ("parallel", "parallel", "arbitrary")
    if ride is not None:
        pieces = ride[0]
        in_specs.append(pl.BlockSpec(memory_space=pl.ANY))
        operands.append(pieces)
        out_specs.append(pl.BlockSpec(memory_space=pl.ANY))
        out_shape.append(jax.ShapeDtypeStruct((N_CHIP,) + pieces.shape[-2:], pieces.dtype))
        scratch += [pltpu.SemaphoreType.DMA((3,)), pltpu.SemaphoreType.DMA((3,)), pltpu.SemaphoreType.DMA]
        semantics = ("arbitrary", "arbitrary", "arbitrary")
    res = pl.pallas_call(
        kern, name=name, grid=grid, in_specs=in_specs, out_specs=out_specs, out_shape=out_shape,
        scratch_shapes=scratch, compiler_params=_params(*semantics),
    )(*operands)
    return res if len(res) > 1 else res[0]


def _rowcall(name, body, rows, tile, row_ins, small_ins, row_outs, acc_outs):
    tile = min(tile, rows)
    assert rows % tile == 0, (name, rows, tile)
    n_in = len(row_ins) + len(small_ins)

    def kern(*refs):
        i = pl.program_id(0)
        accs = refs[n_in + len(row_outs):]

        @pl.when(i == 0)
        def _():
            for acc in accs:
                acc[...] = jnp.zeros_like(acc)

        body(i, *refs)

    def whole(shape):
        return pl.BlockSpec(shape, lambda i, nd=len(shape): (0,) * nd)

    in_specs = [pl.BlockSpec((tile, a.shape[1]), lambda i: (i, 0)) for a in row_ins]
    in_specs += [whole(a.shape) for a in small_ins]
    out_specs = [pl.BlockSpec((tile, w), lambda i: (i, 0)) for (w, _) in row_outs]
    out_specs += [whole(s) for s in acc_outs]
    out_shape = [jax.ShapeDtypeStruct((rows, w), dt) for (w, dt) in row_outs]
    out_shape += [jax.ShapeDtypeStruct(s, F32) for s in acc_outs]
    return pl.pallas_call(
        kern, name=name, grid=(rows // tile,), in_specs=in_specs, out_specs=out_specs, out_shape=out_shape,
        compiler_params=_params("arbitrary"),
    )(*row_ins, *small_ins)


def _rms(x):
    r = lax.rsqrt(jnp.mean(x * x, axis=-1, keepdims=True) + NORM_EPS)
    return x * r, r


def _rms_bwd(dxhat, xhat, r):
    return r * (dxhat - xhat * jnp.mean(dxhat * xhat, axis=-1, keepdims=True))


def _normmod_fwd(name, x, g, sh, sc):
    def body(i, x_ref, g_ref, sh_ref, sc_ref, o_ref):
        xhat, _ = _rms(x_ref[...])
        o_ref[...] = ((xhat * g_ref[...]) * (1.0 + sc_ref[...]) + sh_ref[...]).astype(BF16)

    return _rowcall(name, body, x.shape[0], ROW_TILE, [x], [g, sh, sc], [(x.shape[1], BF16)], [])[0]


def _normmod_bwd(name, x, g, sh, sc, dh, dx_pass):
    d = x.shape[1]

    def body(i, x_ref, dh_ref, dxp_ref, g_ref, sh_ref, sc_ref, dx_ref, dg_ref, dsh_ref, dsc_ref):
        xhat, r = _rms(x_ref[...])
        dh = dh_ref[...].astype(F32)
        gv = g_ref[...]
        dn = dh * (1.0 + sc_ref[...])
        dsc_ref[...] += _colsum(dh * (xhat * gv))
        dsh_ref[...] += _colsum(dh)
        dg_ref[...] += _colsum(dn * xhat)
        dx_ref[...] = dxp_ref[...] + _rms_bwd(dn * gv, xhat, r)

    return _rowcall(name, body, x.shape[0], ROW_TILE, [x, dh, dx_pass], [g, sh, sc], [(d, F32)],
                    [(1, d), (1, d), (1, d)])


def _scale_cols(name, w, g):
    def body(i, w_ref, g_ref, o_ref):
        o_ref[...] = (w_ref[...].astype(F32) * g_ref[...]).astype(BF16)

    return _rowcall(name, body, w.shape[0], ROW_TILE, [w], [g], [(w.shape[1], BF16)], [])[0]


GELU_K = math.sqrt(2.0 / math.pi)
GELU_C = 0.044715


def _gelu_fwd(name, y, u, skip):
    def body(i, y_ref, u_ref, s_ref, o_ref):
        v = y_ref[...].astype(F32) + s_ref[...] * u_ref[...]
        t = jnp.tanh(GELU_K * (v + GELU_C * (v * v * v)))
        o_ref[...] = (0.5 * v * (1.0 + t)).astype(BF16)

    return _rowcall(name, body, y.shape[0], ROW_TILE, [y, u], [skip], [(y.shape[1], BF16)], [])[0]


def _gelu_bwd(name, y, u, skip, dgl):
    d = y.shape[1]

    def body(i, y_ref, u_ref, d_ref, s_ref, o_ref, ds_ref):
        uv = u_ref[...]
        v = y_ref[...].astype(F32) + s_ref[...] * uv
        t = jnp.tanh(GELU_K * (v + GELU_C * (v * v * v)))
        dv = d_ref[...] * (0.5 * (1.0 + t) + 0.5 * v * (1.0 - t * t) * (GELU_K * (1.0 + 3.0 * GELU_C * v * v)))
        o_ref[...] = dv
        ds_ref[...] += _colsum(dv * uv)

    return _rowcall(name, body, y.shape[0], ROW_TILE, [y, u, dgl], [skip], [(d, F32)], [(1, d)])


def _axpy(name, a, b, scale):
    def body(i, a_ref, b_ref, s_ref, o_ref):
        o_ref[...] = (a_ref[...].astype(F32) + s_ref[...] * b_ref[...]).astype(BF16)

    return _rowcall(name, body, a.shape[0], ROW_TILE, [a, b], [scale], [(a.shape[1], BF16)], [])[0]


def _glu_fwd(name, ab, bias, x, gate):
    d = ab.shape[1] // 2

    def body(i, ab_ref, x_ref, b_ref, g_ref, o_ref):
        v = ab_ref[:, :d] + b_ref[:, :d]
        gt = ab_ref[:, d:] + b_ref[:, d:]
        o_ref[...] = x_ref[...] + g_ref[...] * (v * _sigmoid(gt))

    return _rowcall(name, body, ab.shape[0], ROW_TILE, [ab, x], [bias, gate], [(d, F32)], [])[0]


def _glu_bwd(name, ab, bias, dxo, gate):
    d = ab.shape[1] // 2

    def body(i, ab_ref, dx_ref, b_ref, g_ref, dab_ref, db_ref, dg_ref):
        v = ab_ref[:, :d] + b_ref[:, :d]
        s = _sigmoid(ab_ref[:, d:] + b_ref[:, d:])
        dxo_v = dx_ref[...]
        dg_ref[...] += _colsum(dxo_v * (v * s))
        do = g_ref[...] * dxo_v
        dv = do * s
        dgt = do * v * (s * (1.0 - s))
        dab_ref[:, :d] = dv.astype(BF16)
        dab_ref[:, d:] = dgt.astype(BF16)
        db_ref[:, :d] += _colsum(dv)
        db_ref[:, d:] += _colsum(dgt)

    return _rowcall(name, body, ab.shape[0], ROW_TILE, [ab, dxo], [bias, gate], [(2 * d, BF16)],
                    [(1, 2 * d), (1, d)])


def _gatenorm_fwd(name, y, z, ng):
    di = y.shape[1]
    gw = di // M2_G

    def body(i, y_ref, z_ref, g_ref, o_ref):
        for gi in range(M2_G):
            sl = slice(gi * gw, (gi + 1) * gw)
            zz = z_ref[:, sl]
            y2 = y_ref[:, sl] * (zz * _sigmoid(zz))
            yh, _ = _rms(y2)
            o_ref[:, sl] = (yh * g_ref[:, sl]).astype(BF16)

    return _rowcall(name, body, y.shape[0], ROW_TILE, [y, z], [ng], [(di, BF16)], [])[0]


def _gatenorm_bwd(name, y, z, ng, dyn):
    di = y.shape[1]
    gw = di // M2_G

    def body(i, y_ref, z_ref, d_ref, g_ref, dy_ref, dz_ref, dg_ref):
        for gi in range(M2_G):
            sl = slice(gi * gw, (gi + 1) * gw)
            zz = z_ref[:, sl]
            yy = y_ref[:, sl]
            s = _sigmoid(zz)
            sz = zz * s
            yh, r = _rms(yy * sz)
            dn = d_ref[:, sl]
            dg_ref[:, sl] += _colsum(dn * yh)
            dy2 = _rms_bwd(dn * g_ref[:, sl], yh, r)
            dy_ref[:, sl] = dy2 * sz
            dz_ref[:, sl] = (dy2 * yy * (s * (1.0 + zz * (1.0 - s)))).astype(BF16)

    return _rowcall(name, body, y.shape[0], ROW_TILE, [y, z, dyn], [ng], [(di, F32), (di, BF16)], [(1, di)])


def _loss_head(name, x, target, g):
    d = x.shape[1]

    def body(i, x_ref, t_ref, g_ref, dx_ref, dg_ref, loss_ref):
        xhat, r = _rms(x_ref[...])
        gv = g_ref[...]
        err = xhat * gv - t_ref[...]
        per_row = jnp.sum(err * err, axis=-1, keepdims=True) * (0.5 / d)
        loss_ref[...] += jnp.broadcast_to(_colsum(per_row), loss_ref.shape)
        dy = err * (1.0 / d)
        dg_ref[...] += _colsum(dy * xhat)
        dx_ref[...] = _rms_bwd(dy * gv, xhat, r)

    return _rowcall(name, body, x.shape[0], ROW_TILE, [x, target], [g], [(d, F32)], [(1, d), (1, LANES)])


HALO = 8


def _halo_call(name, body, rows, tile, width, mains, halo_of, halo_next, smalls, row_outs, acc_outs, scratch):
    tile = min(tile, rows)
    nb = tile // HALO
    last = rows // HALO - 1
    n_in = len(mains) + 1 + len(smalls)

    def kern(*refs):
        i = pl.program_id(0)
        accs = refs[n_in + len(row_outs):n_in + len(row_outs) + len(acc_outs)]

        @pl.when(i == 0)
        def _():
            for acc in accs:
                acc[...] = jnp.zeros_like(acc)

        body(i, *refs)

    def whole(shape):
        return pl.BlockSpec(shape, lambda i, nd=len(shape): (0,) * nd)

    if halo_next:
        halo_spec = pl.BlockSpec((HALO, width), lambda i: (jnp.minimum((i + 1) * nb, last), 0))
    else:
        halo_spec = pl.BlockSpec((HALO, width), lambda i: (jnp.maximum(i * nb - 1, 0), 0))
    in_specs = [pl.BlockSpec((tile, a.shape[1]), lambda i: (i, 0)) for a in mains] + [halo_spec]
    in_specs += [whole(a.shape) for a in smalls]
    out_specs = [pl.BlockSpec((tile, w), lambda i: (i, 0)) for (w, _) in row_outs] + [whole(s) for s in acc_outs]
    out_shape = [jax.ShapeDtypeStruct((rows, w), dt) for (w, dt) in row_outs]
    out_shape += [jax.ShapeDtypeStruct(s, F32) for s in acc_outs]
    return pl.pallas_call(
        kern, name=name, grid=(rows // tile,), in_specs=in_specs, out_specs=out_specs, out_shape=out_shape,
        scratch_shapes=scratch, compiler_params=_params("arbitrary"),
    )(*mains, mains[halo_of], *smalls)


CONV_TILE = 128
CONV_ROWS = 16
CONV_STRIP = 512


def _conv_blocks(tile, c):
    strip = CONV_STRIP if c % CONV_STRIP == 0 else LANES
    rb = min(CONV_ROWS, tile)
    return [(r0, rb, slice(c0, c0 + strip)) for c0 in range(0, c, strip) for r0 in range(0, tile, rb)]


def _conv_window(x_ref, ext, r0, rb, sl, k):
    if r0 == 0:
        return ext[pl.ds(HALO - 3 + k, rb), sl]
    return x_ref[pl.ds(r0 - 3 + k, rb), sl]


def _conv_fwd(name, xin, w, b):
    rows, c = xin.shape
    tile = min(CONV_TILE, rows)
    rb0 = min(CONV_ROWS, tile)

    def body(i, x_ref, h_ref, w_ref, b_ref, o_ref, ext):
        ext[0:HALO, :] = jnp.where(i == 0, 0.0, h_ref[...])
        ext[HALO:, :] = x_ref[0:rb0, :]
        for r0, rb, sl in _conv_blocks(tile, c):
            pre = b_ref[:, sl] + w_ref[0:1, sl] * _conv_window(x_ref, ext, r0, rb, sl, 0)
            for k in range(1, M2_K):
                pre = pre + w_ref[k:k + 1, sl] * _conv_window(x_ref, ext, r0, rb, sl, k)
            o_ref[r0:r0 + rb, sl] = pre * _sigmoid(pre)

    return _halo_call(name, body, rows, tile, c, [xin], 0, False, [w, b], [(c, F32)], [],
                      [pltpu.VMEM((rb0 + HALO, c), F32)])[0]


def _conv_bwd_pre(name, xin, dout, w, b):
    rows, c = xin.shape
    tile = min(CONV_TILE, rows)
    rb0 = min(CONV_ROWS, tile)

    def body(i, x_ref, d_ref, h_ref, w_ref, b_ref, dp_ref, dw_ref, db_ref, ext):
        ext[0:HALO, :] = jnp.where(i == 0, 0.0, h_ref[...])
        ext[HALO:, :] = x_ref[0:rb0, :]
        sums = {}
        for r0, rb, sl in _conv_blocks(tile, c):
            taps = [_conv_window(x_ref, ext, r0, rb, sl, k) for k in range(M2_K)]
            pre = b_ref[:, sl] + w_ref[0:1, sl] * taps[0]
            for k in range(1, M2_K):
                pre = pre + w_ref[k:k + 1, sl] * taps[k]
            s = _sigmoid(pre)
            dp = d_ref[r0:r0 + rb, sl] * (s * (1.0 + pre * (1.0 - s)))
            dp_ref[r0:r0 + rb, sl] = dp
            part = [_colsum(dp)] + [_colsum(dp * taps[k]) for k in range(M2_K)]
            key = sl.start
            sums[key] = part if key not in sums else [p + q for p, q in zip(sums[key], part)]
            if r0 + rb == tile:
                db_ref[:, sl] += sums[key][0]
                for k in range(M2_K):
                    dw_ref[k:k + 1, sl] += sums[key][1 + k]

    return _halo_call(name, body, rows, tile, c, [xin, dout], 0, False, [w, b], [(c, F32)], [(M2_K, c), (1, c)],
                      [pltpu.VMEM((rb0 + HALO, c), F32)])


def _conv_bwd_in(name, dpre, w):
    rows, c = dpre.shape
    tile = min(CONV_TILE, rows)
    n_tiles = rows // tile
    rb0 = min(CONV_ROWS, tile)

    def body(i, d_ref, h_ref, w_ref, o_ref, ext):
        ext[0:rb0, :] = d_ref[tile - rb0:tile, :]
        ext[rb0:, :] = jnp.where(i == n_tiles - 1, 0.0, h_ref[...])
        for r0, rb, sl in _conv_blocks(tile, c):
            def window(k):
                if r0 + rb == tile:
                    return ext[pl.ds(3 - k, rb), sl]
                return d_ref[pl.ds(r0 + 3 - k, rb), sl]

            acc = w_ref[0:1, sl] * window(0)
            for k in range(1, M2_K):
                acc = acc + w_ref[k:k + 1, sl] * window(k)
            o_ref[r0:r0 + rb, sl] = acc.astype(BF16)

    return _halo_call(name, body, rows, tile, c, [dpre], 0, True, [w], [(c, BF16)], [],
                      [pltpu.VMEM((rb0 + HALO, c), F32)])[0]


def _s5_build(lam_re, lam_im, log_dt, b_re, b_im, c_re, c_im):
    g, p = lam_re.shape
    h = b_re.shape[-1]
    t = S5_T
    dt = jnp.exp(log_dt)[:, None]
    lam = lax.complex(lam_re, lam_im)
    lam_dt = lam * dt
    lam_bar = jnp.exp(lam_dt)
    b_bar = ((lam_bar - 1) / lam)[..., None] * lax.complex(b_re, b_im)
    c_mat = lax.complex(c_re, c_im)
    tau = jnp.arange(t + 1, dtype=F32)
    pw = jnp.exp(lam_dt[:, :, None] * tau[None, None, :])
    c_t = jnp.transpose(c_mat, (0, 2, 1))
    cp = pw[:, :, :, None] * c_t[:, :, None, :]
    cp0 = cp[:, :, :t].reshape(g, p, t * h)
    cp1 = cp[:, :, 1:].reshape(g, p, t * h)
    bb_t = jnp.transpose(b_bar, (0, 2, 1))
    kc = (jnp.einsum("ghp,gpn->ghn", jnp.real(bb_t), jnp.real(cp0), precision=HIGHEST)
          - jnp.einsum("ghp,gpn->ghn", jnp.imag(bb_t), jnp.imag(cp0), precision=HIGHEST))
    bpow = jnp.transpose(pw[:, :, t - 1::-1][:, :, :t], (0, 2, 1))
    be = bb_t[:, :, None, :] * bpow[:, None, :, :]
    bend = jnp.concatenate([jnp.real(be), jnp.imag(be)], axis=-1).reshape(g, h * t, 2 * p)
    cpow = jnp.concatenate([jnp.real(cp1), -jnp.imag(cp1)], axis=1)
    at = pw[:, :, t]
    a1 = jnp.concatenate([jnp.real(at), jnp.real(at)], axis=-1)[:, None, :]
    a2 = jnp.concatenate([-jnp.imag(at), jnp.imag(at)], axis=-1)[:, None, :]
    return kc, bend, cpow, a1, a2


def _swap_halves(x, axis):
    n = x.shape[axis] // 2
    lo = lax.slice_in_dim(x, 0, n, axis=axis)
    hi = lax.slice_in_dim(x, n, 2 * n, axis=axis)
    return jnp.concatenate([hi, lo], axis=axis)


def _group_spec(shape):
    return pl.BlockSpec((1,) + tuple(shape[1:]), lambda g: (g, 0, 0))


def _s5_expand_toeplitz(kc_ref, ext, toep):
    t, th = S5_T, S5_T * S5_H
    ext[:, th:] = jnp.zeros((t, th), F32)
    for hin in range(S5_H):
        ext[:, :th] = jnp.broadcast_to(kc_ref[0, hin:hin + 1, :], (t, th))
        rolled = pltpu.roll(ext[...], 0, 1, stride=S5_H, stride_axis=0)
        toep[hin * t:(hin + 1) * t, :] = rolled[:, :th].astype(BF16)


def _s5_core_fwd(name, u, ops):
    kc, bend, cpow, a1, a2 = ops
    g, nc, th = u.shape
    p2 = bend.shape[-1]
    bend_b, cpow_b = bend.astype(BF16), cpow.astype(BF16)
    bend_s = _swap_halves(bend_b, 2)
    a2s = _swap_halves(a2, 2)

    def kern(u_ref, k_ref, b_ref, bs_ref, c_ref, a1_ref, a2_ref, a2s_ref, y_ref, sp_ref, x_scr, xs_scr, ext, toep):
        _s5_expand_toeplitz(k_ref, ext, toep)
        ub = u_ref[0].astype(BF16)
        x_scr[...] = _dot(ub, b_ref[0])
        xs_scr[...] = _dot(ub, bs_ref[0])
        a1v, a2v, a2sv = a1_ref[0], a2_ref[0], a2s_ref[0]

        def step(c, carry):
            s, ss = carry
            sp_ref[0, pl.ds(c, 1), :] = s
            s_new = a1v * s + a2v * ss + x_scr[pl.ds(c, 1), :]
            ss_new = a1v * ss + a2sv * s + xs_scr[pl.ds(c, 1), :]
            return s_new, ss_new

        zero = jnp.zeros((1, p2), F32)
        lax.fori_loop(0, nc, step, (zero, zero))
        y_ref[0] = (_dot(ub, toep[...]) + _dot(sp_ref[0].astype(BF16), c_ref[0])).astype(BF16)

    ins = [u, kc, bend_b, bend_s, cpow_b, a1, a2, a2s]
    return pl.pallas_call(
        kern, name=name, grid=(g,), in_specs=[_group_spec(a.shape) for a in ins],
        out_specs=[_group_spec((g, nc, th)), _group_spec((g, nc, p2))],
        out_shape=[jax.ShapeDtypeStruct((g, nc, th), BF16), jax.ShapeDtypeStruct((g, nc, p2), F32)],
        scratch_shapes=[pltpu.VMEM((nc, p2), F32), pltpu.VMEM((nc, p2), F32),
                        pltpu.VMEM((S5_T, 2 * th), F32), pltpu.VMEM((th, th), BF16)],
        compiler_params=_params("arbitrary"),
    )(*ins)


def _s5_core_bwd(name, u, dy, sprev, ops):
    kc, bend, cpow, a1, a2 = ops
    g, nc, th = u.shape
    t = S5_T
    p2 = bend.shape[-1]
    bend_b, cpow_b = bend.astype(BF16), cpow.astype(BF16)
    cpow_s = _swap_halves(cpow_b, 1)
    a2s = _swap_halves(a2, 2)
    idx = jnp.arange(th)
    flip = (idx[:, None] // t == idx[None, :] // t) & (idx[:, None] % t == t - 1 - idx[None, :] % t)
    flip = flip.astype(BF16)

    def kern(u_ref, dy_ref, sp_ref, k_ref, b_ref, c_ref, cs_ref, a1_ref, a2_ref, a2s_ref, f_ref,
             du_ref, dk_ref, db_ref, dc_ref, da1_ref, da2_ref, g_scr, gs_scr, dx_scr, ext, toep, dtoep):
        _s5_expand_toeplitz(k_ref, ext, toep)
        ub, dyb = u_ref[0].astype(BF16), dy_ref[0].astype(BF16)
        dtoep[...] = _dot(_dot(ub, f_ref[...]).astype(BF16), dyb, "tn")
        for hin in range(S5_H):
            ext[:, :th] = dtoep[hin * t:(hin + 1) * t, :]
            rolled = pltpu.roll(ext[...], 0, 1, stride=S5_H, stride_axis=0)
            rolled = pltpu.roll(rolled, 2 * th - S5_H * (t - 1), 1)
            dk_ref[0, hin:hin + 1, :] = _colsum(rolled)[:, :th]
        spv = sp_ref[0]
        dc_ref[0] = _dot(spv.astype(BF16), dyb, "tn")
        g_scr[...] = _dot(dyb, c_ref[0], "nt")
        gs_scr[...] = _dot(dyb, cs_ref[0], "nt")
        a1v, a2v, a2sv = a1_ref[0], a2_ref[0], a2s_ref[0]

        def step(k, carry):
            gr, grs, da1, da2 = carry
            c = nc - 1 - k
            dx_scr[pl.ds(c, 1), :] = gr
            s_in = sp_ref[0, pl.ds(c, 1), :]
            da1 = da1 + gr * s_in
            da2 = da2 + grs * s_in
            gr_new = g_scr[pl.ds(c, 1), :] + a1v * gr + a2sv * grs
            grs_new = gs_scr[pl.ds(c, 1), :] + a1v * grs + a2v * gr
            return gr_new, grs_new, da1, da2

        zero = jnp.zeros((1, p2), F32)
        _, _, da1, da2 = lax.fori_loop(0, nc, step, (zero, zero, zero, zero))
        da1_ref[0] = da1
        da2_ref[0] = da2
        dxb = dx_scr[...].astype(BF16)
        db_ref[0] = _dot(ub, dxb, "tn")
        du_ref[0] = (_dot(dyb, toep[...], "nt") + _dot(dxb, b_ref[0], "nt")).astype(BF16)

    ins = [u, dy, sprev, kc, bend_b, cpow_b, cpow_s, a1, a2, a2s]
    outs = [(g, nc, th), (g, S5_H, th), (g, th, p2), (g, p2, th), (g, 1, p2), (g, 1, p2)]
    out_types = [BF16] + [F32] * (len(outs) - 1)
    return pl.pallas_call(
        kern, name=name, grid=(g,),
        in_specs=[_group_spec(a.shape) for a in ins] + [pl.BlockSpec((th, th), lambda gi: (0, 0))],
        out_specs=[_group_spec(s) for s in outs],
        out_shape=[jax.ShapeDtypeStruct(s, dt) for s, dt in zip(outs, out_types)],
        scratch_shapes=[pltpu.VMEM((nc, p2), F32), pltpu.VMEM((nc, p2), F32), pltpu.VMEM((nc, p2), F32),
                        pltpu.VMEM((t, 2 * th), F32), pltpu.VMEM((th, th), BF16), pltpu.VMEM((th, th), F32)],
        compiler_params=_params("arbitrary"),
    )(*ins, flip)


def _s5_to_groups(u, channel_major):
    rows, w = u.shape
    g = w // S5_H
    nc = rows // S5_T
    perm = (2, 0, 3, 1) if channel_major else (2, 0, 1, 3)
    return u.reshape(nc, S5_T, g, S5_H).transpose(perm).reshape(g, nc, S5_T * S5_H)


def _s5_from_groups(y, channel_major):
    g, nc, _ = y.shape
    if channel_major:
        return y.reshape(g, nc, S5_H, S5_T).transpose(1, 3, 0, 2).reshape(nc * S5_T, g * S5_H)
    return y.reshape(g, nc, S5_T, S5_H).transpose(1, 2, 0, 3).reshape(nc * S5_T, g * S5_H)


def _softplus(x):
    return jnp.maximum(x, 0.0) + jnp.log(1.0 + jnp.exp(-jnp.abs(x)))


def _ssd_chunk_prep(dtraw_ref, dtb_ref, a_ref, cst, dtt, lastt, n_heads):
    q = M2_Q
    lane = lax.broadcasted_iota(jnp.int32, (q, LANES), 1)
    dt = jnp.where(lane < n_heads, _softplus(dtraw_ref[...] + dtb_ref[...]), 0.0)
    adt = dt * a_ref[...]
    row = lax.broadcasted_iota(jnp.int32, (q, q), 0)
    col = lax.broadcasted_iota(jnp.int32, (q, q), 1)
    cs = _dot(jnp.where(row >= col, 1.0, 0.0), adt, precision=HIGHEST)
    cst[...] = cs.T
    dtt[...] = dt.T
    lastt[...] = jnp.broadcast_to(_colsum(adt), (q, LANES)).T
    return dt


def _pair_tables(cst, dtt, lastt, p):
    q = M2_Q
    out = []
    for hh in (2 * p, 2 * p + 1):
        rc = jnp.broadcast_to(cst[hh:hh + 1, :], (q, q))
        cc = rc.T
        dtc = jnp.broadcast_to(dtt[hh:hh + 1, :], (q, q)).T
        lb = jnp.broadcast_to(lastt[hh:hh + 1, :], (q, q))
        out.append((rc, cc, dtc, lb))
    return out


def _ssd_pair_fwd(x, bm, cm, cb, hs, tabs):
    q = M2_Q
    row = lax.broadcasted_iota(jnp.int32, (q, q), 0)
    col = lax.broadcasted_iota(jnp.int32, (q, q), 1)
    causal = row >= col
    lo = col < M2_P
    slo = row < M2_P
    (rc0, cc0, dtc0, lb0), (rc1, cc1, dtc1, lb1) = tabs
    l0 = jnp.where(causal, jnp.exp(jnp.where(causal, cc0 - rc0, 0.0)), 0.0)
    l1 = jnp.where(causal, jnp.exp(jnp.where(causal, cc1 - rc1, 0.0)), 0.0)
    m0, m1 = cb * l0, cb * l1
    dtp = jnp.where(lo, dtc0, dtc1)
    xdt = x * dtp
    xdt0 = jnp.where(lo, xdt, 0.0)
    xdt1 = jnp.where(lo, 0.0, xdt)
    e = jnp.where(lo, jnp.exp(cc0), jnp.exp(cc1))
    z = _bdot(cm, hs, "nt")
    yoff = z * e
    dec = jnp.where(lo, jnp.exp(lb0 - cc0), jnp.exp(lb1 - cc1))
    xdd = xdt * dec
    cd = jnp.where(slo, jnp.exp(lb0), jnp.exp(lb1))
    return dict(l0=l0, l1=l1, m0=m0, m1=m1, dtp=dtp, xdt=xdt, xdt0=xdt0, xdt1=xdt1, e=e, yoff=yoff,
                dec=dec, xdd=xdd, cd=cd, lo=lo, slo=slo)


def _ssd_fwd(name, xbc, dtraw, dtb, arow, dvec, n_heads):
    rows, c = xbc.shape
    q, n = M2_Q, M2_N
    di = n_heads * M2_P
    n_pairs = n_heads // 2
    ppg = n_pairs // M2_G
    nc = rows // q

    def kern(xbc_ref, dtraw_ref, dtb_ref, a_ref, d_ref, y_ref, prev_ref, state, cst, dtt, lastt):
        @pl.when(pl.program_id(0) == 0)
        def _():
            state[...] = jnp.zeros_like(state)

        _ssd_chunk_prep(dtraw_ref, dtb_ref, a_ref, cst, dtt, lastt, n_heads)
        for p in range(n_pairs):
            gi = p // ppg
            sl = slice(p * LANES, (p + 1) * LANES)
            x = xbc_ref[:, sl]
            bm = xbc_ref[:, di + gi * n:di + (gi + 1) * n]
            cm = xbc_ref[:, di + (M2_G + gi) * n:di + (M2_G + gi + 1) * n]
            if p % ppg == 0:
                cb = _bdot(cm, bm, "nt")
            hs = state[p]
            f = _ssd_pair_fwd(x, bm, cm, cb, hs, _pair_tables(cst, dtt, lastt, p))
            ydiag = _bdot(f["m0"], f["xdt0"]) + _bdot(f["m1"], f["xdt1"])
            y_ref[:, sl] = ydiag + f["yoff"] + d_ref[:, sl] * x
            prev_ref[0, p] = hs
            state[p] = f["cd"] * hs + _bdot(f["xdd"], bm, "tn")

    def whole(a):
        return pl.BlockSpec(a.shape, lambda i: (0, 0))

    return pl.pallas_call(
        kern, name=name, grid=(nc,),
        in_specs=[pl.BlockSpec((q, c), lambda i: (i, 0)), pl.BlockSpec((q, LANES), lambda i: (i, 0)),
                  whole(dtb), whole(arow), whole(dvec)],
        out_specs=[pl.BlockSpec((q, di), lambda i: (i, 0)),
                   pl.BlockSpec((1, n_pairs, 2 * M2_P, n), lambda i: (i, 0, 0, 0))],
        out_shape=[jax.ShapeDtypeStruct((rows, di), F32),
                   jax.ShapeDtypeStruct((nc, n_pairs, 2 * M2_P, n), F32)],
        scratch_shapes=[pltpu.VMEM((n_pairs, 2 * M2_P, n), F32), pltpu.VMEM((LANES, q), F32),
                        pltpu.VMEM((LANES, q), F32), pltpu.VMEM((LANES, q), F32)],
        compiler_params=_params("arbitrary"),
    )(xbc, dtraw, dtb, arow, dvec)


def _ssd_bwd(name, xbc, dtraw, dy, prev, dtb, arow, dvec, seg, n_heads):
    rows, c = xbc.shape
    q, n = M2_Q, M2_N
    di = n_heads * M2_P
    n_pairs = n_heads // 2
    ppg = n_pairs // M2_G
    nc = rows // q

    def kern(xbc_ref, dtraw_ref, dy_ref, prev_ref, dtb_ref, a_ref, d_ref, seg_ref,
             dxbc_ref, ddt_ref, da_ref, ddtb_ref, dd_ref,
             dstate, cst, dtt, lastt, dcst, wx, colterm, ddfull):
        step = pl.program_id(0)

        @pl.when(step == 0)
        def _():
            dstate[...] = jnp.zeros_like(dstate)
            ddfull[...] = jnp.zeros_like(ddfull)
            da_ref[...] = jnp.zeros_like(da_ref)
            ddtb_ref[...] = jnp.zeros_like(ddtb_ref)
            dd_ref[...] = jnp.zeros_like(dd_ref)

        dt = _ssd_chunk_prep(dtraw_ref, dtb_ref, a_ref, cst, dtt, lastt, n_heads)
        dcst[...] = jnp.zeros_like(dcst)
        lane_q = lax.broadcasted_iota(jnp.int32, (1, q), 1)
        last_hot = jnp.where(lane_q == q - 1, 1.0, 0.0)

        def total(v):
            return jnp.sum(jnp.sum(v, axis=1, keepdims=True), axis=0, keepdims=True)

        for gi in range(M2_G):
            bm = xbc_ref[:, di + gi * n:di + (gi + 1) * n]
            cm = xbc_ref[:, di + (M2_G + gi) * n:di + (M2_G + gi + 1) * n]
            cb = _bdot(cm, bm, "nt")
            dcb = jnp.zeros((q, q), F32)
            dbm = jnp.zeros((q, n), F32)
            dcm = jnp.zeros((q, n), F32)
            for p in range(gi * ppg, (gi + 1) * ppg):
                sl = slice(p * LANES, (p + 1) * LANES)
                x = xbc_ref[:, sl]
                dyp = dy_ref[:, sl]
                hs = prev_ref[0, p]
                ds = dstate[p]
                f = _ssd_pair_fwd(x, bm, cm, cb, hs, _pair_tables(cst, dtt, lastt, p))
                lo, slo = f["lo"], f["slo"]
                ddfull[:, sl] += _colsum(dyp * x)
                dy0 = jnp.where(lo, dyp, 0.0)
                dy1 = jnp.where(lo, 0.0, dyp)
                dm0 = _bdot(dyp, f["xdt0"], "nt")
                dm1 = _bdot(dyp, f["xdt1"], "nt")
                dxdt = _bdot(f["m0"], dy0, "tn") + _bdot(f["m1"], dy1, "tn")
                dcb = dcb + dm0 * f["l0"] + dm1 * f["l1"]
                w0, w1 = dm0 * f["m0"], dm1 * f["m1"]
                dz = dyp * f["e"]
                dcm = dcm + _bdot(dz, hs)
                dhs = _bdot(dz, cm, "tn") + f["cd"] * ds
                tot = ds * hs * f["cd"]
                dxdd = _bdot(bm, ds, "nt")
                dbm = dbm + _bdot(f["xdd"], ds)
                ee = dxdd * f["xdd"]
                colterm[:, sl] = dyp * f["yoff"] - ee
                dxdt = dxdt + dxdd * f["dec"]
                t_all = total(tot)
                t_lo = total(jnp.where(slo, tot, 0.0))
                e_all = total(ee)
                e_lo = total(jnp.where(lo, ee, 0.0))
                dlast0 = t_lo + e_lo
                dlast1 = (t_all - t_lo) + (e_all - e_lo)
                dcst[2 * p:2 * p + 1, :] = _colsum(w0.T - w0) + dlast0 * last_hot
                dcst[2 * p + 1:2 * p + 2, :] = _colsum(w1.T - w1) + dlast1 * last_hot
                dxbc_ref[:, sl] = d_ref[:, sl] * dyp + dxdt * f["dtp"]
                wx[:, sl] = dxdt * x
                dstate[p] = dhs
            dcm = dcm + _bdot(dcb, bm)
            dbm = dbm + _bdot(dcb, cm, "tn")
            dxbc_ref[:, di + gi * n:di + (gi + 1) * n] = dbm
            dxbc_ref[:, di + (M2_G + gi) * n:di + (M2_G + gi + 1) * n] = dcm

        segv = seg_ref[...]
        dcs = _dot(colterm[...], segv, precision=HIGHEST) + dcst[...].T
        row = lax.broadcasted_iota(jnp.int32, (q, q), 0)
        col = lax.broadcasted_iota(jnp.int32, (q, q), 1)
        ddelta = _dot(jnp.where(col >= row, 1.0, 0.0), dcs, precision=HIGHEST)
        ddt = _dot(wx[...], segv, precision=HIGHEST) + ddelta * a_ref[...]
        da_ref[...] += _colsum(ddelta * dt)
        lane = lax.broadcasted_iota(jnp.int32, (q, LANES), 1)
        ddtraw = jnp.where(lane < n_heads, ddt * _sigmoid(dtraw_ref[...] + dtb_ref[...]), 0.0)
        ddt_ref[...] = ddtraw
        ddtb_ref[...] += _colsum(ddtraw)

        @pl.when(step == nc - 1)
        def _():
            dd_ref[...] = _dot(jnp.broadcast_to(ddfull[...], (8, di)), segv, precision=HIGHEST)

    def whole(a):
        return pl.BlockSpec(a.shape, lambda i: (0, 0))

    def rev(i):
        return nc - 1 - i

    acc = jax.ShapeDtypeStruct((1, LANES), F32)
    acc_spec = pl.BlockSpec((1, LANES), lambda i: (0, 0))
    acc8 = jax.ShapeDtypeStruct((8, LANES), F32)
    acc8_spec = pl.BlockSpec((8, LANES), lambda i: (0, 0))
    return pl.pallas_call(
        kern, name=name, grid=(nc,),
        in_specs=[pl.BlockSpec((q, c), lambda i: (rev(i), 0)), pl.BlockSpec((q, LANES), lambda i: (rev(i), 0)),
                  pl.BlockSpec((q, di), lambda i: (rev(i), 0)),
                  pl.BlockSpec((1, n_pairs, 2 * M2_P, n), lambda i: (rev(i), 0, 0, 0)),
                  whole(dtb), whole(arow), whole(dvec), whole(seg)],
        out_specs=[pl.BlockSpec((q, c), lambda i: (rev(i), 0)), pl.BlockSpec((q, LANES), lambda i: (rev(i), 0)),
                   acc_spec, acc_spec, acc8_spec],
        out_shape=[jax.ShapeDtypeStruct((rows, c), F32), jax.ShapeDtypeStruct((rows, LANES), F32), acc, acc, acc8],
        scratch_shapes=[pltpu.VMEM((n_pairs, 2 * M2_P, n), F32), pltpu.VMEM((LANES, q), F32),
                        pltpu.VMEM((LANES, q), F32), pltpu.VMEM((LANES, q), F32), pltpu.VMEM((LANES, q), F32),
                        pltpu.VMEM((q, di), F32), pltpu.VMEM((q, di), F32), pltpu.VMEM((1, di), F32)],
        compiler_params=_params("arbitrary"),
    )(xbc, dtraw, dy, prev, dtb, arow, dvec, seg)


S5_PARAM_NAMES = ("s5_lambda_re", "s5_lambda_im", "s5_log_dt", "s5_b_re", "s5_b_im", "s5_c_re", "s5_c_im")


def _row(v):
    return v.reshape(1, -1)


def _s5_layer_fwd(tag, x, gate, h, w, j):
    u = _matmul(tag + "_win", h, w["s5_w_in"][j])
    params = [w[k][j] for k in S5_PARAM_NAMES]
    ops, build_vjp = jax.vjp(_s5_build, *params)
    ug = _s5_to_groups(u.astype(BF16), True)
    yg, sprev = _s5_core_fwd(tag + "_core", ug, ops)
    yy = _s5_from_groups(yg, False)
    skip = _row(w["s5_d"][j])
    gl = _gelu_fwd(tag + "_gelu", yy, u, skip)
    ab = _matmul(tag + "_wglu", gl, w["s5_w_glu"][j])
    x1 = _glu_fwd(tag + "_glu", ab, _row(w["s5_b_glu"][j]), x, gate)
    return x1, dict(u=u, ug=ug, ops=ops, build_vjp=build_vjp, sprev=sprev, yy=yy, gl=gl, ab=ab, skip=skip)


def _s5_layer_bwd(tag, dx1, gate, h, sv, w, j):
    dab, db_glu, dgate = _glu_bwd(tag + "_glu_b", sv["ab"], _row(w["s5_b_glu"][j]), dx1, gate)
    dw_glu = _matmul(tag + "_dwglu", sv["gl"], dab, "tn")
    dgl = _matmul(tag + "_dgl", dab, w["s5_w_glu"][j], "nt")
    dyy, dskip = _gelu_bwd(tag + "_gelu_b", sv["yy"], sv["u"], sv["skip"], dgl)
    dug, dkc, dbend, dcpow, da1, da2s = _s5_core_bwd(
        tag + "_core_b", sv["ug"], _s5_to_groups(dyy.astype(BF16), False), sv["sprev"], sv["ops"])
    dparams = sv["build_vjp"]((dkc, dbend, dcpow, da1, _swap_halves(da2s, 2)))
    du = _axpy(tag + "_du", _s5_from_groups(dug, True), dyy, sv["skip"])
    grads = dict(zip(S5_PARAM_NAMES, dparams))
    grads["s5_d"] = dskip.reshape(-1)
    grads["s5_w_in"] = _matmul(tag + "_dwin", h, du, "tn")
    grads["s5_w_glu"] = dw_glu
    grads["s5_b_glu"] = db_glu.reshape(-1)
    dh = _matmul(tag + "_dh", du, w["s5_w_in"][j], "nt")
    return dh, grads, dgate


def _ssd_consts(w, j, d_model):
    di = 2 * d_model
    heads = di // M2_P

    def pad_row(v):
        return jnp.zeros((1, LANES), F32).at[0, :heads].set(v)

    a = -jnp.exp(w["m2_a_log"][j])
    seg = (jnp.arange(di)[:, None] // M2_P == jnp.arange(LANES)[None, :]).astype(F32)
    w_in = w["m2_w_in"][j]
    conv_dim = di + 2 * M2_G * M2_N
    w_dt = jnp.zeros((d_model, LANES), w_in.dtype).at[:, :heads].set(w_in[:, di + conv_dim:])
    return dict(di=di, heads=heads, conv_dim=conv_dim, a=a, arow=pad_row(a), dtb=pad_row(w["m2_dt_bias"][j]),
                dvec=_row(jnp.repeat(w["m2_d"][j], M2_P)), seg=seg,
                w_z=w_in[:, :di], w_xbc=w_in[:, di:di + conv_dim], w_dt=w_dt,
                conv_w=w["m2_conv_w"][j], conv_b=_row(w["m2_conv_b"][j]), norm_g=_row(w["m2_norm_g"][j]))


def _gated_out_bwd(tag, act, dxo, w_out, gate, **kw):
    dw, dgate_parts = _matmul(tag + "_dwo", act, dxo, "tn", colscale=gate, colsum_with=w_out, **kw)
    dgate = jnp.sum(dgate_parts, axis=0)
    return dw, dgate, _scale_cols(tag + "_wog", w_out, gate)


def _ssd_layer_fwd(tag, x, gate, h, w, j):
    k = _ssd_consts(w, j, h.shape[1])
    z = _matmul(tag + "_wz", h, k["w_z"])
    xbc_pre = _matmul(tag + "_wxbc", h, k["w_xbc"])
    dtraw = _matmul(tag + "_wdt", h, k["w_dt"])
    xbc = _conv_fwd(tag + "_conv", xbc_pre, k["conv_w"], k["conv_b"])
    y, prev = _ssd_fwd(tag + "_core", xbc, dtraw, k["dtb"], k["arow"], k["dvec"], k["heads"])
    yn = _gatenorm_fwd(tag + "_gn", y, z, k["norm_g"])
    x1 = _matmul(tag + "_wout", yn, w["m2_w_out"][j], colscale=gate, addin=x)
    return x1, dict(k=k, z=z, xbc_pre=xbc_pre, dtraw=dtraw, xbc=xbc, y=y, prev=prev, yn=yn)


def _ssd_layer_bwd(tag, dx1, gate, h, sv, w, j):
    k = sv["k"]
    heads = k["heads"]
    dw_out, dgate, wog = _gated_out_bwd(tag, sv["yn"], dx1, w["m2_w_out"][j], gate)
    grads = {"m2_w_out": dw_out}
    dyn = _matmul(tag + "_dyn", dx1, wog, "nt")
    dyssd, dz, dng = _gatenorm_bwd(tag + "_gn_b", sv["y"], sv["z"], k["norm_g"], dyn)
    dxbc, ddtraw, da, ddtb, dd = _ssd_bwd(tag + "_core_b", sv["xbc"], sv["dtraw"], dyssd, sv["prev"],
                                          k["dtb"], k["arow"], k["dvec"], k["seg"], heads)
    dpre, dcw, dcb = _conv_bwd_pre(tag + "_conv_b1", sv["xbc_pre"], dxbc, k["conv_w"], k["conv_b"])
    dxbc_pre = _conv_bwd_in(tag + "_conv_b2", dpre, k["conv_w"])
    dw_z = _matmul(tag + "_dwz", h, dz, "tn")
    dw_xbc = _matmul(tag + "_dwxbc", h, dxbc_pre, "tn")
    dw_dt = _matmul(tag + "_dwdt", h, ddtraw, "tn")
    dh = _matmul(tag + "_dh1", dz, k["w_z"], "nt")
    dh = _matmul(tag + "_dh2", dxbc_pre, k["w_xbc"], "nt", addin=dh)
    dh = _matmul(tag + "_dh3", ddtraw, k["w_dt"], "nt", addin=dh)
    grads["m2_w_in"] = jnp.concatenate([dw_z, dw_xbc, dw_dt[:, :heads]], axis=1)
    grads["m2_conv_w"] = dcw
    grads["m2_conv_b"] = dcb.reshape(-1)
    grads["m2_dt_bias"] = ddtb[0, :heads]
    grads["m2_a_log"] = da[0, :heads] * k["a"]
    grads["m2_d"] = dd[0, :heads]
    grads["m2_norm_g"] = dng.reshape(-1)
    return dh, grads, dgate


def _ride(rides, k, same):
    return None if rides is None else (rides[k], same)


def _layer_fwd(li, x, mod, w, rides):
    tag = "L%d" % li
    sh1, sc1, g1, sh2, sc2, g2 = mod
    j = li // 2
    h = _normmod_fwd(tag + "_nm1", x, _row(w["norm_mix_g"][li]), sh1, sc1)
    if li % 2 == 0:
        x1, mix = _s5_layer_fwd(tag + "_s5", x, g1, h, w, j)
    else:
        x1, mix = _ssd_layer_fwd(tag + "_m2", x, g1, h, w, j)
    h2 = _normmod_fwd(tag + "_nm2", x1, _row(w["norm_mlp_g"][li]), sh2, sc2)
    r = _matmul(tag + "_w1", h2, w["mlp_w1"][li], relu=True, out_dtype=BF16, ride=_ride(rides, 0, True))
    landed = None
    if rides is not None:
        r, land0 = r
    x2 = _matmul(tag + "_w2", r, w["mlp_w2"][li], square_a=True, colscale=g2, addin=x1, ride=_ride(rides, 1, True))
    if rides is not None:
        x2, land1 = x2
        landed = [land0, land1]
    return x2, dict(x=x, h=h, mix=mix, x1=x1, h2=h2, r=r), landed


def _layer_bwd(li, dx2, sv, mod, w, rides):
    tag = "L%d" % li
    sh1, sc1, g1, sh2, sc2, g2 = mod
    j = li // 2
    dw2, dg2, w2g = _gated_out_bwd(tag + "_mlp", sv["r"], dx2, w["mlp_w2"][li], g2, square_a=True)
    grads = {"mlp_w2": dw2}
    dr = _matmul(tag + "_dr", dx2, w2g, "nt", out_dtype=BF16, mul2=sv["r"], ride=_ride(rides, 0, False))
    landed = None
    if rides is not None:
        dr, land0 = dr
    grads["mlp_w1"] = _matmul(tag + "_dw1", sv["h2"], dr, "tn")
    dh2 = _matmul(tag + "_dh2", dr, w["mlp_w1"][li], "nt", ride=_ride(rides, 1, False))
    if rides is not None:
        dh2, land1 = dh2
        landed = [land0, land1]
    dx1, dgm, dsh2, dsc2 = _normmod_bwd(tag + "_nm2_b", sv["x1"], _row(w["norm_mlp_g"][li]), sh2, sc2, dh2, dx2)
    if li % 2 == 0:
        dh, mix_grads, dg1 = _s5_layer_bwd(tag + "_s5", dx1, g1, sv["h"], sv["mix"], w, j)
    else:
        dh, mix_grads, dg1 = _ssd_layer_bwd(tag + "_m2", dx1, g1, sv["h"], sv["mix"], w, j)
    dx, dgx, dsh1, dsc1 = _normmod_bwd(tag + "_nm1_b", sv["x"], _row(w["norm_mix_g"][li]), sh1, sc1, dh, dx1)
    grads["norm_mix_g"] = dgx.reshape(-1)
    grads["norm_mlp_g"] = dgm.reshape(-1)
    dmod = jnp.concatenate([dsh1, dsc1, dg1, dsh2, dsc2, dg2], axis=1)
    return dx, {**grads, **mix_grads}, dmod, landed


def _layer_parts(li):
    j = li // 2
    if li % 2 == 0:
        return [[("s5_w_in", j, 0), ("s5_w_glu", j, 1)], [("mlp_w1", li, 1), ("mlp_w2", li, 0)]], []
    return ([[("m2_w_in", j, 1), ("m2_w_out", j, 0)], [("mlp_w1", li, 1), ("mlp_w2", li, 0)]],
            [("m2_conv_w", j, 1), ("m2_conv_b", j, 0), ("m2_norm_g", j, 0)])


def _grad_parts(li):
    big, small = _layer_parts(li)
    return [big[0] + small, big[1]]


def _assemble(landed, local, part):
    shapes = [local[n][i].shape for n, i, _ in part]
    per_chip = [_unpack(landed[k], shapes) for k in range(N_CHIP)]
    return {n: {i: jnp.concatenate([per_chip[k][q] for k in range(N_CHIP)], axis=ax)}
            for q, (n, i, ax) in enumerate(part)}


def _grad_pieces(grads, part):
    return jnp.stack([_pack([_chip_slice(grads[n], k, ax) for n, _, ax in part], BF16) for k in range(N_CHIP)])


def _local_step(x, target, mods, w, local):
    depth = w["norm_mix_g"].shape[0]
    d = x.shape[1]
    saved, mod_rows, layer_w = [], [], []
    big0 = _layer_parts(0)[0]
    nxt = [_exchange4("ag_w_L0_%d" % k, _pack([local[n][i] for n, i, _ in part], BF16), same=True)
           for k, part in enumerate(big0)]
    for li in range(depth):
        wl = dict(w)
        for landed, part in zip(nxt, _layer_parts(li)[0]):
            wl.update(_assemble(landed, local, part))
        layer_w.append(wl)
        rides = None
        if li + 1 < depth:
            rides = [_pack([local[n][i] for n, i, _ in part], BF16) for part in _layer_parts(li + 1)[0]]
        mod = [mods[li:li + 1, i * d:(i + 1) * d] for i in range(N_MOD)]
        mod_rows.append(mod)
        x, sv, nxt = _layer_fwd(li, x, mod, wl, rides)
        saved.append(sv)
    dx, dgf, loss = _loss_head("loss_head", x, target, _row(w["final_norm_g"]))
    layer_grads = [None] * depth
    dmods = [None] * depth
    landed = [None] * depth
    rides = None
    for li in reversed(range(depth)):
        dx, layer_grads[li], dmods[li], got = _layer_bwd(li, dx, saved[li], mod_rows[li], layer_w[li], rides)
        if got is not None:
            landed[li + 1] = got
        rides = [_grad_pieces(layer_grads[li], part) for part in _grad_parts(li)]
    landed[0] = [_exchange4("rs_g_L0_%d" % k, pieces, same=False) for k, pieces in enumerate(rides)]
    return loss, dx, layer_grads, dgf.reshape(-1), jnp.concatenate(dmods, axis=0), landed


ANY = pl.BlockSpec(memory_space=pl.ANY)
N_DEV = 8
N_CHIP = 4


def _coords():
    return lax.axis_index("x"), lax.axis_index("y"), lax.axis_index("c")


def _allgather8(name, block):
    r, wd = block.shape

    def body(x_ref, out_ref, send_sems, recv_sems, local_sem):
        x, y, c = _coords()
        me, sibling = (x, y, c), (x, y, 1 - c)
        chips = [(1 - x, y), (x, 1 - y), (1 - x, 1 - y)]

        def slot(px, py, pc):
            return out_ref.at[4 * px + 2 * py + pc]

        def copy(k, blk, to, src=None):
            return pltpu.make_async_remote_copy(
                src_ref=slot(*blk) if src is None else src, dst_ref=slot(*blk),
                send_sem=send_sems.at[k], recv_sem=recv_sems.at[k], device_id=to, device_id_type=MESH)

        mine = pltpu.make_async_copy(x_ref, slot(*me), local_sem)
        mine.start()
        first = [copy(0, me, sibling, src=x_ref)]
        first += [copy(1 + j, me, (*chip, c), src=x_ref) for j, chip in enumerate(chips)]
        for cp in first:
            cp.start()
        passed = [copy(4 + j, (*chip, c), sibling) for j, chip in enumerate(chips)]
        for j, chip in enumerate(chips):
            copy(1 + j, (*chip, c), me).wait_recv()
            passed[j].start()
        copy(0, sibling, me).wait_recv()
        for j, chip in enumerate(chips):
            copy(4 + j, (*chip, 1 - c), me).wait_recv()
        for cp in first + passed:
            cp.wait_send()
        mine.wait()

    return pl.pallas_call(
        body, name=name, in_specs=[ANY], out_specs=ANY,
        out_shape=jax.ShapeDtypeStruct((N_DEV, r, wd), block.dtype),
        scratch_shapes=[pltpu.SemaphoreType.DMA((7,)), pltpu.SemaphoreType.DMA((7,)), pltpu.SemaphoreType.DMA],
    )(block)


def _exchange4_ops(x_ref, out_ref, send_sems, recv_sems, local_sem, same):
    x, y, c = _coords()
    my_chip = 2 * x + y
    chips = [(1 - x, y), (x, 1 - y), (1 - x, 1 - y)]

    def src(k):
        return x_ref if same else x_ref.at[k]

    def copy(j, piece, slot):
        px, py = chips[j]
        return pltpu.make_async_remote_copy(
            src_ref=src(piece), dst_ref=out_ref.at[slot], send_sem=send_sems.at[j], recv_sem=recv_sems.at[j],
            device_id=(px, py, c), device_id_type=MESH)

    def mine():
        return pltpu.make_async_copy(src(my_chip), out_ref.at[my_chip], local_sem)

    def start():
        mine().start()
        for j, (px, py) in enumerate(chips):
            copy(j, 2 * px + py, my_chip).start()

    def wait():
        for j, (px, py) in enumerate(chips):
            copy(j, my_chip, 2 * px + py).wait_recv()
        for j, (px, py) in enumerate(chips):
            copy(j, 2 * px + py, my_chip).wait_send()
        mine().wait()

    return start, wait


def _exchange4(name, pieces, same):
    r, wd = pieces.shape[-2:]

    def body(x_ref, out_ref, send_sems, recv_sems, local_sem):
        start, wait = _exchange4_ops(x_ref, out_ref, send_sems, recv_sems, local_sem, same)
        start()
        wait()

    return pl.pallas_call(
        body, name=name, in_specs=[ANY], out_specs=ANY,
        out_shape=jax.ShapeDtypeStruct((N_CHIP, r, wd), pieces.dtype),
        scratch_shapes=[pltpu.SemaphoreType.DMA((3,)), pltpu.SemaphoreType.DMA((3,)), pltpu.SemaphoreType.DMA],
    )(pieces)


def _swap_sibling(name, block):
    def body(x_ref, out_ref, send_sem, recv_sem):
        x, y, c = _coords()
        cp = pltpu.make_async_remote_copy(src_ref=x_ref, dst_ref=out_ref, send_sem=send_sem, recv_sem=recv_sem,
                                          device_id=(x, y, 1 - c), device_id_type=MESH)
        cp.start()
        cp.wait()

    return pl.pallas_call(
        body, name=name, in_specs=[ANY], out_specs=ANY, out_shape=jax.ShapeDtypeStruct(block.shape, block.dtype),
        scratch_shapes=[pltpu.SemaphoreType.DMA, pltpu.SemaphoreType.DMA],
    )(block)


def _sum_slots(name, stacked):
    n, r, wd = stacked.shape
    tile = min(FLAT_ROWS, r)

    def kern(x_ref, o_ref):
        acc = x_ref[0].astype(F32)
        for s in range(1, n):
            acc = acc + x_ref[s].astype(F32)
        o_ref[...] = acc

    return pl.pallas_call(
        kern, name=name, grid=(r // tile,), in_specs=[pl.BlockSpec((n, tile, wd), lambda i: (0, i, 0))],
        out_specs=pl.BlockSpec((tile, wd), lambda i: (i, 0)), out_shape=jax.ShapeDtypeStruct((r, wd), F32),
        compiler_params=_params("parallel"),
    )(stacked)


def _adamw(name, w, m, v, g, g2=None):
    r, wd = w.shape
    grads = [g] if g2 is None else [g, g2]
    c1 = 1.0 - ADAM_B1 ** ADAM_STEP
    c2 = 1.0 - ADAM_B2 ** ADAM_STEP

    def body(i, *refs):
        w_ref, m_ref, v_ref = refs[:3]
        g_refs = refs[3:3 + len(grads)]
        go_ref, d_ref, mo_ref, vo_ref = refs[3 + len(grads):]
        gv = g_refs[0][...]
        if g2 is not None:
            gv = gv + g_refs[1][...]
        mn = ADAM_B1 * m_ref[...] + (1.0 - ADAM_B1) * gv
        vn = ADAM_B2 * v_ref[...] + (1.0 - ADAM_B2) * (gv * gv)
        go_ref[...] = gv
        mo_ref[...] = mn
        vo_ref[...] = vn
        d_ref[...] = -ADAM_LR * ((mn / c1) / (jnp.sqrt(vn / c2) + ADAM_EPS) + ADAM_WD * w_ref[...])

    return _rowcall(name, body, r, FLAT_ROWS, [w, m, v] + grads, [], [(wd, F32)] * 4, [])


FLAT_BLOCK = FLAT_ROWS * FLAT_W


def _pack(arrays, dtype):
    flat = jnp.concatenate([a.reshape(-1).astype(dtype) for a in arrays])
    pad = (-flat.shape[0]) % FLAT_BLOCK
    return jnp.pad(flat, (0, pad)).reshape(-1, FLAT_W)


def _unpack(buf, shapes):
    flat = buf.reshape(-1)
    out, off = [], 0
    for s in shapes:
        n = math.prod(s)
        out.append(flat[off:off + n].reshape(s))
        off += n
    return out


SHARDED_BIG = {"mlp_w1": 2, "mlp_w2": 1, "s5_w_in": 1, "s5_w_glu": 2, "m2_w_in": 2, "m2_w_out": 1}
SHARDED_SMALL = {"m2_conv_w": 2, "m2_conv_b": 1, "m2_norm_g": 1}
REPLICATED = ("ada_b", "norm_mix_g", "norm_mlp_g", "s5_lambda_re", "s5_lambda_im", "s5_log_dt", "s5_b_re",
              "s5_b_im", "s5_c_re", "s5_c_im", "s5_d", "s5_b_glu", "m2_dt_bias", "m2_a_log", "m2_d", "final_norm_g")
WEIGHT_NAMES = ("ada_w", "ada_b", "norm_mix_g", "norm_mlp_g", "mlp_w1", "mlp_w2", "s5_w_in", "s5_lambda_re",
                "s5_lambda_im", "s5_log_dt", "s5_b_re", "s5_b_im", "s5_c_re", "s5_c_im", "s5_d", "s5_w_glu",
                "s5_b_glu", "m2_w_in", "m2_conv_w", "m2_conv_b", "m2_dt_bias", "m2_a_log", "m2_d", "m2_norm_g",
                "m2_w_out", "final_norm_g")


def _gather_weights(name, local, names_axes, dtype):
    names = list(names_axes)
    got = _exchange4(name, _pack([local[k] for k in names], dtype), same=True)
    per_chip = [_unpack(got[j], [local[k].shape for k in names]) for j in range(N_CHIP)]
    return {k: jnp.concatenate([per_chip[j][i] for j in range(N_CHIP)], axis=names_axes[k])
            for i, k in enumerate(names)}


def _chip_slice(a, chip, axis):
    size = a.shape[axis] // N_CHIP
    return lax.slice_in_dim(a, chip * size, (chip + 1) * size, axis=axis)


def kernel(x, c, ada_w, ada_b, norm_mix_g, norm_mlp_g, mlp_w1, mlp_w2, s5_w_in, s5_lambda_re, s5_lambda_im, s5_log_dt, s5_b_re, s5_b_im, s5_c_re, s5_c_im, s5_d, s5_w_glu, s5_b_glu, m2_w_in, m2_conv_w, m2_conv_b, m2_dt_bias, m2_a_log, m2_d, m2_norm_g, m2_w_out, final_norm_g, loss_target, m_ada_w, m_ada_b, m_norm_mix_g, m_norm_mlp_g, m_mlp_w1, m_mlp_w2, m_s5_w_in, m_s5_lambda_re, m_s5_lambda_im, m_s5_log_dt, m_s5_b_re, m_s5_b_im, m_s5_c_re, m_s5_c_im, m_s5_d, m_s5_w_glu, m_s5_b_glu, m_m2_w_in, m_m2_conv_w, m_m2_conv_b, m_m2_dt_bias, m_m2_a_log, m_m2_d, m_m2_norm_g, m_m2_w_out, m_final_norm_g, v_ada_w, v_ada_b, v_norm_mix_g, v_norm_mlp_g, v_mlp_w1, v_mlp_w2, v_s5_w_in, v_s5_lambda_re, v_s5_lambda_im, v_s5_log_dt, v_s5_b_re, v_s5_b_im, v_s5_c_re, v_s5_c_im, v_s5_d, v_s5_w_glu, v_s5_b_glu, v_m2_w_in, v_m2_conv_w, v_m2_conv_b, v_m2_dt_bias, v_m2_a_log, v_m2_d, v_m2_norm_g, v_m2_w_out, v_final_norm_g):
    args = locals()
    local = {k: args[k] for k in WEIGHT_NAMES}
    mom_m = {k: args["m_" + k] for k in WEIGHT_NAMES}
    mom_v = {k: args["v_" + k] for k in WEIGHT_NAMES}
    depth, d = norm_mix_g.shape
    xi, yi, ci = _coords()
    my_chip = 2 * xi + yi
    my_dev = 2 * my_chip + ci

    cond = jax.nn.silu(c).reshape(-1, LANES)
    cond_all = _allgather8("ag_cond", cond).reshape(N_DEV, d)
    cond_pad = jnp.zeros((LANES, d), F32).at[:N_DEV].set(cond_all)
    mod_cols = ada_w.shape[2]
    mod_part = jnp.stack([_matmul("ada_%d" % i, cond_pad, ada_w[i])[:N_DEV] for i in range(depth)])
    mod_all = _allgather8("ag_mod", mod_part.reshape(-1, LANES)).reshape(N_CHIP, 2, depth, N_DEV, mod_cols)[:, 0]
    mod_mine = lax.dynamic_index_in_dim(mod_all, my_dev, axis=2, keepdims=False)
    mods = jnp.transpose(mod_mine, (1, 0, 2)).reshape(depth, N_CHIP * mod_cols) + ada_b

    w = {k: local[k] for k in REPLICATED}
    w.update(_gather_weights("ag_w_small", local, SHARDED_SMALL, F32))

    loss_row, dx, layer_grads, g_final, dmods, landed = _local_step(x[0], loss_target[0], mods, w, local)
    grads = {"ada_b": dmods, "final_norm_g": g_final}
    for k in REPLICATED[1:-1]:
        grads[k] = jnp.stack([g[k] for g in layer_grads if k in g])

    rep_shapes = [grads[k].shape for k in REPLICATED]
    rep_all = _allgather8("ag_grep", _pack([grads[k] for k in REPLICATED], F32))
    rep_sum = _sum_slots("sum_grep", rep_all)
    dmods_all = rep_all.reshape(N_DEV, -1)[:, :dmods.size].reshape(N_DEV, depth, N_CHIP * mod_cols)

    dm_mine = lax.dynamic_slice_in_dim(dmods_all, my_chip * mod_cols, mod_cols, axis=2)
    dm_pad = jnp.zeros((LANES, depth, mod_cols), F32).at[:N_DEV].set(dm_mine)
    g_ada_w = jnp.stack([_matmul("dada_%d" % i, cond_pad, dm_pad[:, i], "tn") for i in range(depth)])

    sums = [_sum_slots("sum_g_L%d_%d" % (li, k), landed[li][k]) for li in range(depth) for k in range(2)]
    mine = jnp.concatenate(sums, axis=0)
    other = _swap_sibling("swap_gsh", mine)
    by_entry = ({}, {})
    row = 0
    for li in range(depth):
        for k, part in enumerate(_grad_parts(li)):
            rows = sums[2 * li + k].shape[0]
            shapes = [local[n][i].shape for n, i, _ in part]
            for buf, dst in zip((mine, other), by_entry):
                dst.update({(n, i): v for (n, i, _), v in zip(part, _unpack(buf[row:row + rows], shapes))})
            row += rows

    out_g, out_d, out_m, out_v = {}, {}, {}, {}
    for name in list(SHARDED_BIG) + list(SHARDED_SMALL):
        shape = local[name].shape
        flat2 = (-1, shape[-1])
        g1, g2 = (jnp.stack([e[(name, i)] for i in range(shape[0])]).reshape(flat2) for e in by_entry)
        res = _adamw("adam_" + name, local[name].reshape(flat2), mom_m[name].reshape(flat2),
                     mom_v[name].reshape(flat2), g1, g2)
        for dst, buf in zip((out_g, out_d, out_m, out_v), res):
            dst[name] = buf.reshape(shape)
    res = _adamw("adam_rep", _pack([local[k] for k in REPLICATED], F32), _pack([mom_m[k] for k in REPLICATED], F32),
                 _pack([mom_v[k] for k in REPLICATED], F32), rep_sum)
    for dst, buf in zip((out_g, out_d, out_m, out_v), res):
        dst.update(zip(REPLICATED, _unpack(buf, rep_shapes)))
    flat2 = (-1, mod_cols)
    res = _adamw("adam_ada", ada_w.reshape(flat2), m_ada_w.reshape(flat2), v_ada_w.reshape(flat2),
                 g_ada_w.reshape(flat2))
    for dst, buf in zip((out_g, out_d, out_m, out_v), res):
        dst["ada_w"] = buf.reshape(ada_w.shape)

    loss = lax.psum(loss_row[0, 0], ("x", "y", "c"))
    outs = [loss, dx[None]]
    for dst in (out_g, out_d, out_m, out_v):
        outs += [dst[k] for k in WEIGHT_NAMES]
    return tuple(outs)
```

```python
import functools
import math

import jax
import jax.numpy as jnp
from jax import lax
from jax.experimental import pallas as pl
from jax.experimental.pallas import tpu as pltpu

F32 = jnp.float32
BF16 = jnp.bfloat16
HIGHEST = lax.Precision.HIGHEST

NORM_EPS = 1e-5
N_MOD = 6
S5_H, S5_P, S5_T = 16, 64, 64
M2_P, M2_N, M2_G, M2_Q, M2_K = 64, 128, 4, 128, 4
LANES = 128
ADAM_LR, ADAM_B1, ADAM_B2, ADAM_EPS, ADAM_WD, ADAM_STEP = 0.001, 0.9, 0.999, 1e-08, 0.01, 10
VMEM_LIMIT_BYTES = 48 * 1024 * 1024
ROW_TILE = 256
FLAT_W = 1024
FLAT_ROWS = 256
MESH = pl.DeviceIdType.MESH


def _params(*sem):
    return pltpu.CompilerParams(dimension_semantics=sem, vmem_limit_bytes=VMEM_LIMIT_BYTES)


def _dot(a, b, dn="nn", precision=None):
    dims = {"nn": ((1,), (0,)), "nt": ((1,), (1,)), "tn": ((0,), (0,))}[dn]
    return lax.dot_general(a, b, (dims, ((), ())), preferred_element_type=F32, precision=precision)


def _bdot(a, b, dn="nn"):
    return _dot(a.astype(BF16), b.astype(BF16), dn)


def _sigmoid(x):
    return jax.nn.sigmoid(x)


def _colsum(x):
    return jnp.sum(x, axis=0, keepdims=True)


def _pick_tile(dim, want):
    if dim <= want:
        return dim
    for t in range(want - want % LANES, 0, -LANES):
        if dim % t == 0:
            return t
    raise ValueError((dim, want))


def _matmul(name, a, b, mode="nn", out_dtype=F32, relu=False, square_a=False, mul2=None, colscale=None,
            addin=None, colsum_with=None, ride=None, tm=1024, tn=1024, tk=1024):
    if mode == "nn":
        (m, k), (k2, n) = a.shape, b.shape
    elif mode == "nt":
        (m, k), (n, k2) = a.shape, b.shape
    else:
        (k, m), (k2, n) = a.shape, b.shape
    assert k == k2, (name, a.shape, b.shape)
    tm, tn, tk = _pick_tile(m, tm), _pick_tile(n, tn), _pick_tile(k, tk)
    nk = k // tk
    tiles = [e for e in (mul2, addin, colsum_with) if e is not None]
    n_ext = len(tiles) + (colscale is not None)
    n_out = 1 + (colsum_with is not None)
    n_ride = 0 if ride is None else len(ride)
    grid = (m // tm, n // tn, nk)

    def kern(*refs):
        a_ref, b_ref = refs[:2]
        e_refs = list(refs[2:2 + n_ext])
        o_refs = refs[2 + n_ext + n_ride:2 + n_ext + n_ride + n_out]
        kk = pl.program_id(2)
        if ride is not None:
            here = [pl.program_id(ax) for ax in range(3)]
            land0 = 2 + n_ext + n_ride + n_out
            ride_refs = (refs[2 + n_ext:2 + n_ext + n_ride], refs[land0:land0 + n_ride]) + tuple(refs[-3:])
            ride_kinds = [kind for _, kind in ride]
            first = (here[0] == 0) & (here[1] == 0) & (here[2] == 0)
            last = (here[0] == grid[0] - 1) & (here[1] == grid[1] - 1) & (here[2] == grid[2] - 1)

            @pl.when(first)
            def _():
                _exchange4_ops(*ride_refs, ride_kinds)[0]()

        av = a_ref[...]
        if square_a:
            av = av * av
        part = _bdot(av, b_ref[...], mode)

        def finish(r):
            ext = list(e_refs)
            m2v = ext.pop(0)[...].astype(F32) if mul2 is not None else None
            addv = ext.pop(0)[...].astype(F32) if addin is not None else None
            if colsum_with is not None:
                o_refs[1][0] = _colsum(r * ext.pop(0)[...].astype(F32))
            if relu:
                r = jnp.maximum(r, 0.0)
            if m2v is not None:
                r = r * (2.0 * m2v)
            if colscale is not None:
                r = r * ext.pop(0)[...]
            if addv is not None:
                r = r + addv
            o_refs[0][...] = r.astype(out_dtype)

        if nk == 1:
            finish(part)
        else:
            acc = refs[-4] if ride is not None else refs[-1]

            @pl.when(kk == 0)
            def _():
                acc[...] = part

            @pl.when(kk > 0)
            def _():
                acc[...] += part

            @pl.when(kk == nk - 1)
            def _():
                finish(acc[...])

        if ride is not None:
            @pl.when(last)
            def _():
                _exchange4_ops(*ride_refs, ride_kinds)[1]()

    if mode == "tn":
        a_spec = pl.BlockSpec((tk, tm), lambda i, j, kk: (kk, i))
    else:
        a_spec = pl.BlockSpec((tm, tk), lambda i, j, kk: (i, kk))
    if mode == "nt":
        b_spec = pl.BlockSpec((tn, tk), lambda i, j, kk: (j, kk))
    else:
        b_spec = pl.BlockSpec((tk, tn), lambda i, j, kk: (kk, j))
    o_spec = pl.BlockSpec((tm, tn), lambda i, j, kk: (i, j))
    in_specs = [a_spec, b_spec] + [o_spec] * len(tiles)
    operands = [a, b] + tiles
    if colscale is not None:
        in_specs.append(pl.BlockSpec((1, tn), lambda i, j, kk: (0, j)))
        operands.append(colscale)
    out_specs = [o_spec]
    out_shape = [jax.ShapeDtypeStruct((m, n), out_dtype)]
    if colsum_with is not None:
        out_specs.append(pl.BlockSpec((1, 1, tn), lambda i, j, kk: (i, 0, j)))
        out_shape.append(jax.ShapeDtypeStruct((m // tm, 1, n), F32))
    scratch = [pltpu.VMEM((tm, tn), F32)] if nk > 1 else []
    semantics = ("parallel", "parallel", "arbitrary")
    if ride is not None:
        for src, kind in ride:
            in_specs.append(pl.BlockSpec(memory_space=pl.ANY))
            operands.append(src)
            out_specs.append(pl.BlockSpec(memory_space=pl.ANY))
            out_shape.append(jax.ShapeDtypeStruct(_landing_shape(src, kind), src.dtype))
        scratch += _exchange_scratch(n_ride)
        semantics = ("arbitrary", "arbitrary", "arbitrary")
    res = pl.pallas_call(
        kern, name=name, grid=grid, in_specs=in_specs, out_specs=out_specs, out_shape=out_shape,
        scratch_shapes=scratch, compiler_params=_params(*semantics),
    )(*operands)
    return res if len(res) > 1 else res[0]


def _rowcall(name, body, rows, tile, row_ins, small_ins, row_outs, acc_outs):
    tile = min(tile, rows)
    assert rows % tile == 0, (name, rows, tile)
    n_in = len(row_ins) + len(small_ins)

    def kern(*refs):
        i = pl.program_id(0)
        accs = refs[n_in + len(row_outs):]

        @pl.when(i == 0)
        def _():
            for acc in accs:
                acc[...] = jnp.zeros_like(acc)

        body(i, *refs)

    def whole(shape):
        return pl.BlockSpec(shape, lambda i, nd=len(shape): (0,) * nd)

    in_specs = [pl.BlockSpec((tile, a.shape[1]), lambda i: (i, 0)) for a in row_ins]
    in_specs += [whole(a.shape) for a in small_ins]
    out_specs = [pl.BlockSpec((tile, w), lambda i: (i, 0)) for (w, _) in row_outs]
    out_specs += [whole(s) for s in acc_outs]
    out_shape = [jax.ShapeDtypeStruct((rows, w), dt) for (w, dt) in row_outs]
    out_shape += [jax.ShapeDtypeStruct(s, F32) for s in acc_outs]
    return pl.pallas_call(
        kern, name=name, grid=(rows // tile,), in_specs=in_specs, out_specs=out_specs, out_shape=out_shape,
        compiler_params=_params("arbitrary"),
    )(*row_ins, *small_ins)


def _rms(x):
    r = lax.rsqrt(jnp.mean(x * x, axis=-1, keepdims=True) + NORM_EPS)
    return x * r, r


def _rms_bwd(dxhat, xhat, r):
    return r * (dxhat - xhat * jnp.mean(dxhat * xhat, axis=-1, keepdims=True))


def _normmod_fwd(name, x, g, sh, sc):
    def body(i, x_ref, g_ref, sh_ref, sc_ref, o_ref):
        xhat, _ = _rms(x_ref[...])
        o_ref[...] = ((xhat * g_ref[...]) * (1.0 + sc_ref[...]) + sh_ref[...]).astype(BF16)

    return _rowcall(name, body, x.shape[0], ROW_TILE, [x], [g, sh, sc], [(x.shape[1], BF16)], [])[0]


def _normmod_bwd(name, x, g, sh, sc, dh, dx_pass):
    d = x.shape[1]

    def body(i, x_ref, dh_ref, dxp_ref, g_ref, sh_ref, sc_ref, dx_ref, dg_ref, dsh_ref, dsc_ref):
        xhat, r = _rms(x_ref[...])
        dh = dh_ref[...].astype(F32)
        gv = g_ref[...]
        dn = dh * (1.0 + sc_ref[...])
        dsc_ref[...] += _colsum(dh * (xhat * gv))
        dsh_ref[...] += _colsum(dh)
        dg_ref[...] += _colsum(dn * xhat)
        dx_ref[...] = dxp_ref[...] + _rms_bwd(dn * gv, xhat, r)

    return _rowcall(name, body, x.shape[0], ROW_TILE, [x, dh, dx_pass], [g, sh, sc], [(d, F32)],
                    [(1, d), (1, d), (1, d)])


def _scale_cols(name, w, g):
    def body(i, w_ref, g_ref, o_ref):
        o_ref[...] = (w_ref[...].astype(F32) * g_ref[...]).astype(BF16)

    return _rowcall(name, body, w.shape[0], ROW_TILE, [w], [g], [(w.shape[1], BF16)], [])[0]


GELU_K = math.sqrt(2.0 / math.pi)
GELU_C = 0.044715


def _gelu_fwd(name, y, u, skip):
    def body(i, y_ref, u_ref, s_ref, o_ref):
        v = y_ref[...].astype(F32) + s_ref[...] * u_ref[...]
        t = jnp.tanh(GELU_K * (v + GELU_C * (v * v * v)))
        o_ref[...] = (0.5 * v * (1.0 + t)).astype(BF16)

    return _rowcall(name, body, y.shape[0], ROW_TILE, [y, u], [skip], [(y.shape[1], BF16)], [])[0]


def _gelu_bwd(name, y, u, skip, dgl):
    d = y.shape[1]

    def body(i, y_ref, u_ref, d_ref, s_ref, o_ref, ds_ref):
        uv = u_ref[...]
        v = y_ref[...].astype(F32) + s_ref[...] * uv
        t = jnp.tanh(GELU_K * (v + GELU_C * (v * v * v)))
        dv = d_ref[...] * (0.5 * (1.0 + t) + 0.5 * v * (1.0 - t * t) * (GELU_K * (1.0 + 3.0 * GELU_C * v * v)))
        o_ref[...] = dv
        ds_ref[...] += _colsum(dv * uv)

    return _rowcall(name, body, y.shape[0], ROW_TILE, [y, u, dgl], [skip], [(d, F32)], [(1, d)])


def _axpy(name, a, b, scale):
    def body(i, a_ref, b_ref, s_ref, o_ref):
        o_ref[...] = (a_ref[...].astype(F32) + s_ref[...] * b_ref[...]).astype(BF16)

    return _rowcall(name, body, a.shape[0], ROW_TILE, [a, b], [scale], [(a.shape[1], BF16)], [])[0]


def _glu_fwd(name, ab, bias, x, gate):
    d = ab.shape[1] // 2

    def body(i, ab_ref, x_ref, b_ref, g_ref, o_ref):
        v = ab_ref[:, :d] + b_ref[:, :d]
        gt = ab_ref[:, d:] + b_ref[:, d:]
        o_ref[...] = x_ref[...] + g_ref[...] * (v * _sigmoid(gt))

    return _rowcall(name, body, ab.shape[0], ROW_TILE, [ab, x], [bias, gate], [(d, F32)], [])[0]


def _glu_bwd(name, ab, bias, dxo, gate):
    d = ab.shape[1] // 2

    def body(i, ab_ref, dx_ref, b_ref, g_ref, dab_ref, db_ref, dg_ref):
        v = ab_ref[:, :d] + b_ref[:, :d]
        s = _sigmoid(ab_ref[:, d:] + b_ref[:, d:])
        dxo_v = dx_ref[...]
        dg_ref[...] += _colsum(dxo_v * (v * s))
        do = g_ref[...] * dxo_v
        dv = do * s
        dgt = do * v * (s * (1.0 - s))
        dab_ref[:, :d] = dv.astype(BF16)
        dab_ref[:, d:] = dgt.astype(BF16)
        db_ref[:, :d] += _colsum(dv)
        db_ref[:, d:] += _colsum(dgt)

    return _rowcall(name, body, ab.shape[0], ROW_TILE, [ab, dxo], [bias, gate], [(2 * d, BF16)],
                    [(1, 2 * d), (1, d)])


def _gatenorm_fwd(name, y, z, ng):
    di = y.shape[1]
    gw = di // M2_G

    def body(i, y_ref, z_ref, g_ref, o_ref):
        for gi in range(M2_G):
            sl = slice(gi * gw, (gi + 1) * gw)
            zz = z_ref[:, sl]
            y2 = y_ref[:, sl] * (zz * _sigmoid(zz))
            yh, _ = _rms(y2)
            o_ref[:, sl] = (yh * g_ref[:, sl]).astype(BF16)

    return _rowcall(name, body, y.shape[0], ROW_TILE, [y, z], [ng], [(di, BF16)], [])[0]


def _gatenorm_bwd(name, y, z, ng, dyn):
    di = y.shape[1]
    gw = di // M2_G

    def body(i, y_ref, z_ref, d_ref, g_ref, dy_ref, dz_ref, dg_ref):
        for gi in range(M2_G):
            sl = slice(gi * gw, (gi + 1) * gw)
            zz = z_ref[:, sl]
            yy = y_ref[:, sl]
            s = _sigmoid(zz)
            sz = zz * s
            yh, r = _rms(yy * sz)
            dn = d_ref[:, sl]
            dg_ref[:, sl] += _colsum(dn * yh)
            dy2 = _rms_bwd(dn * g_ref[:, sl], yh, r)
            dy_ref[:, sl] = dy2 * sz
            dz_ref[:, sl] = (dy2 * yy * (s * (1.0 + zz * (1.0 - s)))).astype(BF16)

    return _rowcall(name, body, y.shape[0], ROW_TILE, [y, z, dyn], [ng], [(di, F32), (di, BF16)], [(1, di)])


def _loss_head(name, x, target, g):
    d = x.shape[1]

    def body(i, x_ref, t_ref, g_ref, dx_ref, dg_ref, loss_ref):
        xhat, r = _rms(x_ref[...])
        gv = g_ref[...]
        err = xhat * gv - t_ref[...]
        per_row = jnp.sum(err * err, axis=-1, keepdims=True) * (0.5 / d)
        loss_ref[...] += jnp.broadcast_to(_colsum(per_row), loss_ref.shape)
        dy = err * (1.0 / d)
        dg_ref[...] += _colsum(dy * xhat)
        dx_ref[...] = _rms_bwd(dy * gv, xhat, r)

    return _rowcall(name, body, x.shape[0], ROW_TILE, [x, target], [g], [(d, F32)], [(1, d), (1, LANES)])


HALO = 8


def _halo_call(name, body, rows, tile, width, mains, halo_of, halo_next, smalls, row_outs, acc_outs, scratch):
    tile = min(tile, rows)
    nb = tile // HALO
    last = rows // HALO - 1
    n_in = len(mains) + 1 + len(smalls)

    def kern(*refs):
        i = pl.program_id(0)
        accs = refs[n_in + len(row_outs):n_in + len(row_outs) + len(acc_outs)]

        @pl.when(i == 0)
        def _():
            for acc in accs:
                acc[...] = jnp.zeros_like(acc)

        body(i, *refs)

    def whole(shape):
        return pl.BlockSpec(shape, lambda i, nd=len(shape): (0,) * nd)

    if halo_next:
        halo_spec = pl.BlockSpec((HALO, width), lambda i: (jnp.minimum((i + 1) * nb, last), 0))
    else:
        halo_spec = pl.BlockSpec((HALO, width), lambda i: (jnp.maximum(i * nb - 1, 0), 0))
    in_specs = [pl.BlockSpec((tile, a.shape[1]), lambda i: (i, 0)) for a in mains] + [halo_spec]
    in_specs += [whole(a.shape) for a in smalls]
    out_specs = [pl.BlockSpec((tile, w), lambda i: (i, 0)) for (w, _) in row_outs] + [whole(s) for s in acc_outs]
    out_shape = [jax.ShapeDtypeStruct((rows, w), dt) for (w, dt) in row_outs]
    out_shape += [jax.ShapeDtypeStruct(s, F32) for s in acc_outs]
    return pl.pallas_call(
        kern, name=name, grid=(rows // tile,), in_specs=in_specs, out_specs=out_specs, out_shape=out_shape,
        scratch_shapes=scratch, compiler_params=_params("arbitrary"),
    )(*mains, mains[halo_of], *smalls)


CONV_TILE = 128
CONV_ROWS = 16
CONV_STRIP = 512


def _conv_blocks(tile, c):
    strip = CONV_STRIP if c % CONV_STRIP == 0 else LANES
    rb = min(CONV_ROWS, tile)
    return [(r0, rb, slice(c0, c0 + strip)) for c0 in range(0, c, strip) for r0 in range(0, tile, rb)]


def _shifted_windows(base, rb, offsets):
    n = base.shape[0]
    out = []
    for o in offsets:
        if o % HALO == 0:
            out.append(base[o:o + rb, :])
        else:
            out.append(pltpu.roll(base, n - o, 0)[0:rb, :])
    return out


def _conv_windows(x_ref, ext, r0, rb, sl):
    base = ext[:, sl] if r0 == 0 else x_ref[r0 - HALO:r0 + rb, sl]
    return _shifted_windows(base, rb, [HALO - 3 + k for k in range(M2_K)])


def _conv_fwd(name, xin, w, b):
    rows, c = xin.shape
    tile = min(CONV_TILE, rows)
    rb0 = min(CONV_ROWS, tile)

    def body(i, x_ref, h_ref, w_ref, b_ref, o_ref, ext):
        ext[0:HALO, :] = jnp.where(i == 0, 0.0, h_ref[...])
        ext[HALO:, :] = x_ref[0:rb0, :]
        for r0, rb, sl in _conv_blocks(tile, c):
            taps = _conv_windows(x_ref, ext, r0, rb, sl)
            pre = b_ref[:, sl] + w_ref[0:1, sl] * taps[0]
            for k in range(1, M2_K):
                pre = pre + w_ref[k:k + 1, sl] * taps[k]
            o_ref[r0:r0 + rb, sl] = pre * _sigmoid(pre)

    return _halo_call(name, body, rows, tile, c, [xin], 0, False, [w, b], [(c, F32)], [],
                      [pltpu.VMEM((rb0 + HALO, c), F32)])[0]


def _conv_bwd_pre(name, xin, dout, w, b):
    rows, c = xin.shape
    tile = min(CONV_TILE, rows)
    rb0 = min(CONV_ROWS, tile)

    def body(i, x_ref, d_ref, h_ref, w_ref, b_ref, dp_ref, dw_ref, db_ref, ext):
        ext[0:HALO, :] = jnp.where(i == 0, 0.0, h_ref[...])
        ext[HALO:, :] = x_ref[0:rb0, :]
        sums = {}
        for r0, rb, sl in _conv_blocks(tile, c):
            taps = _conv_windows(x_ref, ext, r0, rb, sl)
            pre = b_ref[:, sl] + w_ref[0:1, sl] * taps[0]
            for k in range(1, M2_K):
                pre = pre + w_ref[k:k + 1, sl] * taps[k]
            s = _sigmoid(pre)
            dp = d_ref[r0:r0 + rb, sl] * (s * (1.0 + pre * (1.0 - s)))
            dp_ref[r0:r0 + rb, sl] = dp
            part = [_colsum(dp)] + [_colsum(dp * taps[k]) for k in range(M2_K)]
            key = sl.start
            sums[key] = part if key not in sums else [p + q for p, q in zip(sums[key], part)]
            if r0 + rb == tile:
                db_ref[:, sl] += sums[key][0]
                for k in range(M2_K):
                    dw_ref[k:k + 1, sl] += sums[key][1 + k]

    return _halo_call(name, body, rows, tile, c, [xin, dout], 0, False, [w, b], [(c, F32)], [(M2_K, c), (1, c)],
                      [pltpu.VMEM((rb0 + HALO, c), F32)])


def _conv_bwd_in(name, dpre, w):
    rows, c = dpre.shape
    tile = min(CONV_TILE, rows)
    n_tiles = rows // tile
    rb0 = min(CONV_ROWS, tile)

    def body(i, d_ref, h_ref, w_ref, o_ref, ext):
        ext[0:rb0, :] = d_ref[tile - rb0:tile, :]
        ext[rb0:, :] = jnp.where(i == n_tiles - 1, 0.0, h_ref[...])
        for r0, rb, sl in _conv_blocks(tile, c):
            base = ext[:, sl] if r0 + rb == tile else d_ref[r0:r0 + rb + HALO, sl]
            wins = _shifted_windows(base, rb, [3 - k for k in range(M2_K)])
            acc = w_ref[0:1, sl] * wins[0]
            for k in range(1, M2_K):
                acc = acc + w_ref[k:k + 1, sl] * wins[k]
            o_ref[r0:r0 + rb, sl] = acc.astype(BF16)

    return _halo_call(name, body, rows, tile, c, [dpre], 0, True, [w], [(c, BF16)], [],
                      [pltpu.VMEM((rb0 + HALO, c), F32)])[0]


def _s5_build(lam_re, lam_im, log_dt, b_re, b_im, c_re, c_im):
    g, p = lam_re.shape
    h = b_re.shape[-1]
    t = S5_T
    dt = jnp.exp(log_dt)[:, None]
    lam = lax.complex(lam_re, lam_im)
    lam_dt = lam * dt
    lam_bar = jnp.exp(lam_dt)
    b_bar = ((lam_bar - 1) / lam)[..., None] * lax.complex(b_re, b_im)
    c_mat = lax.complex(c_re, c_im)
    tau = jnp.arange(t + 1, dtype=F32)
    pw = jnp.exp(lam_dt[:, :, None] * tau[None, None, :])
    c_t = jnp.transpose(c_mat, (0, 2, 1))
    cp = pw[:, :, :, None] * c_t[:, :, None, :]
    cp0 = cp[:, :, :t].reshape(g, p, t * h)
    cp1 = cp[:, :, 1:].reshape(g, p, t * h)
    bb_t = jnp.transpose(b_bar, (0, 2, 1))
    kc = (jnp.einsum("ghp,gpn->ghn", jnp.real(bb_t), jnp.real(cp0), precision=HIGHEST)
          - jnp.einsum("ghp,gpn->ghn", jnp.imag(bb_t), jnp.imag(cp0), precision=HIGHEST))
    bpow = jnp.transpose(pw[:, :, t - 1::-1][:, :, :t], (0, 2, 1))
    be = bb_t[:, :, None, :] * bpow[:, None, :, :]
    bend = jnp.concatenate([jnp.real(be), jnp.imag(be)], axis=-1).reshape(g, h * t, 2 * p)
    cpow = jnp.concatenate([jnp.real(cp1), -jnp.imag(cp1)], axis=1)
    at = pw[:, :, t]
    a1 = jnp.concatenate([jnp.real(at), jnp.real(at)], axis=-1)[:, None, :]
    a2 = jnp.concatenate([-jnp.imag(at), jnp.imag(at)], axis=-1)[:, None, :]
    return kc, bend, cpow, a1, a2


def _swap_halves(x, axis):
    n = x.shape[axis] // 2
    lo = lax.slice_in_dim(x, 0, n, axis=axis)
    hi = lax.slice_in_dim(x, n, 2 * n, axis=axis)
    return jnp.concatenate([hi, lo], axis=axis)


def _group_spec(shape):
    return pl.BlockSpec((1,) + tuple(shape[1:]), lambda g: (g, 0, 0))


def _s5_expand_toeplitz(kc_ref, ext, toep):
    t, th = S5_T, S5_T * S5_H
    ext[:, th:] = jnp.zeros((t, th), F32)
    for hin in range(S5_H):
        ext[:, :th] = jnp.broadcast_to(kc_ref[0, hin:hin + 1, :], (t, th))
        rolled = pltpu.roll(ext[...], 0, 1, stride=S5_H, stride_axis=0)
        toep[hin * t:(hin + 1) * t, :] = rolled[:, :th].astype(BF16)


def _s5_core_fwd(name, u, ops):
    kc, bend, cpow, a1, a2 = ops
    g, nc, th = u.shape
    p2 = bend.shape[-1]
    bend_b, cpow_b = bend.astype(BF16), cpow.astype(BF16)
    bend_s = _swap_halves(bend_b, 2)
    a2s = _swap_halves(a2, 2)

    def kern(u_ref, k_ref, b_ref, bs_ref, c_ref, a1_ref, a2_ref, a2s_ref, y_ref, sp_ref, x_scr, xs_scr, ext, toep):
        _s5_expand_toeplitz(k_ref, ext, toep)
        ub = u_ref[0].astype(BF16)
        x_scr[...] = _dot(ub, b_ref[0])
        xs_scr[...] = _dot(ub, bs_ref[0])
        a1v, a2v, a2sv = a1_ref[0], a2_ref[0], a2s_ref[0]

        def step(c, carry):
            s, ss = carry
            sp_ref[0, pl.ds(c, 1), :] = s
            s_new = a1v * s + a2v * ss + x_scr[pl.ds(c, 1), :]
            ss_new = a1v * ss + a2sv * s + xs_scr[pl.ds(c, 1), :]
            return s_new, ss_new

        zero = jnp.zeros((1, p2), F32)
        lax.fori_loop(0, nc, step, (zero, zero))
        y_ref[0] = (_dot(ub, toep[...]) + _dot(sp_ref[0].astype(BF16), c_ref[0])).astype(BF16)

    ins = [u, kc, bend_b, bend_s, cpow_b, a1, a2, a2s]
    return pl.pallas_call(
        kern, name=name, grid=(g,), in_specs=[_group_spec(a.shape) for a in ins],
        out_specs=[_group_spec((g, nc, th)), _group_spec((g, nc, p2))],
        out_shape=[jax.ShapeDtypeStruct((g, nc, th), BF16), jax.ShapeDtypeStruct((g, nc, p2), F32)],
        scratch_shapes=[pltpu.VMEM((nc, p2), F32), pltpu.VMEM((nc, p2), F32),
                        pltpu.VMEM((S5_T, 2 * th), F32), pltpu.VMEM((th, th), BF16)],
        compiler_params=_params("arbitrary"),
    )(*ins)


def _s5_core_bwd(name, u, dy, sprev, ops):
    kc, bend, cpow, a1, a2 = ops
    g, nc, th = u.shape
    t = S5_T
    p2 = bend.shape[-1]
    bend_b, cpow_b = bend.astype(BF16), cpow.astype(BF16)
    cpow_s = _swap_halves(cpow_b, 1)
    a2s = _swap_halves(a2, 2)
    idx = jnp.arange(th)
    flip = (idx[:, None] // t == idx[None, :] // t) & (idx[:, None] % t == t - 1 - idx[None, :] % t)
    flip = flip.astype(BF16)

    def kern(u_ref, dy_ref, sp_ref, k_ref, b_ref, c_ref, cs_ref, a1_ref, a2_ref, a2s_ref, f_ref,
             du_ref, dk_ref, db_ref, dc_ref, da1_ref, da2_ref, g_scr, gs_scr, dx_scr, ext, toep, dtoep):
        _s5_expand_toeplitz(k_ref, ext, toep)
        ub, dyb = u_ref[0].astype(BF16), dy_ref[0].astype(BF16)
        dtoep[...] = _dot(_dot(ub, f_ref[...]).astype(BF16), dyb, "tn")
        for hin in range(S5_H):
            ext[:, :th] = dtoep[hin * t:(hin + 1) * t, :]
            rolled = pltpu.roll(ext[...], 0, 1, stride=S5_H, stride_axis=0)
            rolled = pltpu.roll(rolled, 2 * th - S5_H * (t - 1), 1)
            dk_ref[0, hin:hin + 1, :] = _colsum(rolled)[:, :th]
        spv = sp_ref[0]
        dc_ref[0] = _dot(spv.astype(BF16), dyb, "tn")
        g_scr[...] = _dot(dyb, c_ref[0], "nt")
        gs_scr[...] = _dot(dyb, cs_ref[0], "nt")
        a1v, a2v, a2sv = a1_ref[0], a2_ref[0], a2s_ref[0]

        def step(k, carry):
            gr, grs, da1, da2 = carry
            c = nc - 1 - k
            dx_scr[pl.ds(c, 1), :] = gr
            s_in = sp_ref[0, pl.ds(c, 1), :]
            da1 = da1 + gr * s_in
            da2 = da2 + grs * s_in
            gr_new = g_scr[pl.ds(c, 1), :] + a1v * gr + a2sv * grs
            grs_new = gs_scr[pl.ds(c, 1), :] + a1v * grs + a2v * gr
            return gr_new, grs_new, da1, da2

        zero = jnp.zeros((1, p2), F32)
        _, _, da1, da2 = lax.fori_loop(0, nc, step, (zero, zero, zero, zero))
        da1_ref[0] = da1
        da2_ref[0] = da2
        dxb = dx_scr[...].astype(BF16)
        db_ref[0] = _dot(ub, dxb, "tn")
        du_ref[0] = (_dot(dyb, toep[...], "nt") + _dot(dxb, b_ref[0], "nt")).astype(BF16)

    ins = [u, dy, sprev, kc, bend_b, cpow_b, cpow_s, a1, a2, a2s]
    outs = [(g, nc, th), (g, S5_H, th), (g, th, p2), (g, p2, th), (g, 1, p2), (g, 1, p2)]
    out_types = [BF16] + [F32] * (len(outs) - 1)
    return pl.pallas_call(
        kern, name=name, grid=(g,),
        in_specs=[_group_spec(a.shape) for a in ins] + [pl.BlockSpec((th, th), lambda gi: (0, 0))],
        out_specs=[_group_spec(s) for s in outs],
        out_shape=[jax.ShapeDtypeStruct(s, dt) for s, dt in zip(outs, out_types)],
        scratch_shapes=[pltpu.VMEM((nc, p2), F32), pltpu.VMEM((nc, p2), F32), pltpu.VMEM((nc, p2), F32),
                        pltpu.VMEM((t, 2 * th), F32), pltpu.VMEM((th, th), BF16), pltpu.VMEM((th, th), F32)],
        compiler_params=_params("arbitrary"),
    )(*ins, flip)


def _s5_to_groups(u, channel_major):
    rows, w = u.shape
    g = w // S5_H
    nc = rows // S5_T
    perm = (2, 0, 3, 1) if channel_major else (2, 0, 1, 3)
    return u.reshape(nc, S5_T, g, S5_H).transpose(perm).reshape(g, nc, S5_T * S5_H)


def _s5_from_groups(y, channel_major):
    g, nc, _ = y.shape
    if channel_major:
        return y.reshape(g, nc, S5_H, S5_T).transpose(1, 3, 0, 2).reshape(nc * S5_T, g * S5_H)
    return y.reshape(g, nc, S5_T, S5_H).transpose(1, 2, 0, 3).reshape(nc * S5_T, g * S5_H)


def _softplus(x):
    return jnp.maximum(x, 0.0) + jnp.log(1.0 + jnp.exp(-jnp.abs(x)))


def _ssd_chunk_prep(dtraw_ref, dtb_ref, a_ref, cst, dtt, lastt, n_heads):
    q = M2_Q
    lane = lax.broadcasted_iota(jnp.int32, (q, LANES), 1)
    dt = jnp.where(lane < n_heads, _softplus(dtraw_ref[...] + dtb_ref[...]), 0.0)
    adt = dt * a_ref[...]
    row = lax.broadcasted_iota(jnp.int32, (q, q), 0)
    col = lax.broadcasted_iota(jnp.int32, (q, q), 1)
    cs = _dot(jnp.where(row >= col, 1.0, 0.0), adt, precision=HIGHEST)
    cst[...] = cs.T
    dtt[...] = dt.T
    lastt[...] = jnp.broadcast_to(_colsum(adt), (q, LANES)).T
    return dt


def _pair_tables(cst, dtt, lastt, p):
    q = M2_Q
    out = []
    for hh in (2 * p, 2 * p + 1):
        rc = jnp.broadcast_to(cst[hh:hh + 1, :], (q, q))
        cc = rc.T
        dtc = jnp.broadcast_to(dtt[hh:hh + 1, :], (q, q)).T
        lb = jnp.broadcast_to(lastt[hh:hh + 1, :], (q, q))
        out.append((rc, cc, dtc, lb))
    return out


def _ssd_pair_fwd(x, bm, cm, cb, hs, tabs):
    q = M2_Q
    row = lax.broadcasted_iota(jnp.int32, (q, q), 0)
    col = lax.broadcasted_iota(jnp.int32, (q, q), 1)
    causal = row >= col
    lo = col < M2_P
    slo = row < M2_P
    (rc0, cc0, dtc0, lb0), (rc1, cc1, dtc1, lb1) = tabs
    l0 = jnp.where(causal, jnp.exp(jnp.where(causal, cc0 - rc0, 0.0)), 0.0)
    l1 = jnp.where(causal, jnp.exp(jnp.where(causal, cc1 - rc1, 0.0)), 0.0)
    m0, m1 = cb * l0, cb * l1
    dtp = jnp.where(lo, dtc0, dtc1)
    xdt = x * dtp
    xdt0 = jnp.where(lo, xdt, 0.0)
    xdt1 = jnp.where(lo, 0.0, xdt)
    e = jnp.where(lo, jnp.exp(cc0), jnp.exp(cc1))
    z = _bdot(cm, hs, "nt")
    yoff = z * e
    dec = jnp.where(lo, jnp.exp(lb0 - cc0), jnp.exp(lb1 - cc1))
    xdd = xdt * dec
    cd = jnp.where(slo, jnp.exp(lb0), jnp.exp(lb1))
    return dict(l0=l0, l1=l1, m0=m0, m1=m1, dtp=dtp, xdt=xdt, xdt0=xdt0, xdt1=xdt1, e=e, yoff=yoff,
                dec=dec, xdd=xdd, cd=cd, lo=lo, slo=slo)


def _ssd_fwd(name, xbc, dtraw, dtb, arow, dvec, n_heads):
    rows, c = xbc.shape
    q, n = M2_Q, M2_N
    di = n_heads * M2_P
    n_pairs = n_heads // 2
    ppg = n_pairs // M2_G
    nc = rows // q

    def kern(xbc_ref, dtraw_ref, dtb_ref, a_ref, d_ref, y_ref, prev_ref, state, cst, dtt, lastt):
        @pl.when(pl.program_id(0) == 0)
        def _():
            state[...] = jnp.zeros_like(state)

        _ssd_chunk_prep(dtraw_ref, dtb_ref, a_ref, cst, dtt, lastt, n_heads)
        for p in range(n_pairs):
            gi = p // ppg
            sl = slice(p * LANES, (p + 1) * LANES)
            x = xbc_ref[:, sl]
            bm = xbc_ref[:, di + gi * n:di + (gi + 1) * n]
            cm = xbc_ref[:, di + (M2_G + gi) * n:di + (M2_G + gi + 1) * n]
            if p % ppg == 0:
                cb = _bdot(cm, bm, "nt")
            hs = state[p]
            f = _ssd_pair_fwd(x, bm, cm, cb, hs, _pair_tables(cst, dtt, lastt, p))
            ydiag = _bdot(f["m0"], f["xdt0"]) + _bdot(f["m1"], f["xdt1"])
            y_ref[:, sl] = ydiag + f["yoff"] + d_ref[:, sl] * x
            prev_ref[0, p] = hs
            state[p] = f["cd"] * hs + _bdot(f["xdd"], bm, "tn")

    def whole(a):
        return pl.BlockSpec(a.shape, lambda i: (0, 0))

    return pl.pallas_call(
        kern, name=name, grid=(nc,),
        in_specs=[pl.BlockSpec((q, c), lambda i: (i, 0)), pl.BlockSpec((q, LANES), lambda i: (i, 0)),
                  whole(dtb), whole(arow), whole(dvec)],
        out_specs=[pl.BlockSpec((q, di), lambda i: (i, 0)),
                   pl.BlockSpec((1, n_pairs, 2 * M2_P, n), lambda i: (i, 0, 0, 0))],
        out_shape=[jax.ShapeDtypeStruct((rows, di), F32),
                   jax.ShapeDtypeStruct((nc, n_pairs, 2 * M2_P, n), F32)],
        scratch_shapes=[pltpu.VMEM((n_pairs, 2 * M2_P, n), F32), pltpu.VMEM((LANES, q), F32),
                        pltpu.VMEM((LANES, q), F32), pltpu.VMEM((LANES, q), F32)],
        compiler_params=_params("arbitrary"),
    )(xbc, dtraw, dtb, arow, dvec)


def _ssd_bwd(name, xbc, dtraw, dy, prev, dtb, arow, dvec, seg, n_heads):
    rows, c = xbc.shape
    q, n = M2_Q, M2_N
    di = n_heads * M2_P
    n_pairs = n_heads // 2
    ppg = n_pairs // M2_G
    nc = rows // q

    def kern(xbc_ref, dtraw_ref, dy_ref, prev_ref, dtb_ref, a_ref, d_ref, seg_ref,
             dxbc_ref, ddt_ref, da_ref, ddtb_ref, dd_ref,
             dstate, cst, dtt, lastt, dcst, wx, colterm, ddfull):
        step = pl.program_id(0)

        @pl.when(step == 0)
        def _():
            dstate[...] = jnp.zeros_like(dstate)
            ddfull[...] = jnp.zeros_like(ddfull)
            da_ref[...] = jnp.zeros_like(da_ref)
            ddtb_ref[...] = jnp.zeros_like(ddtb_ref)
            dd_ref[...] = jnp.zeros_like(dd_ref)

        dt = _ssd_chunk_prep(dtraw_ref, dtb_ref, a_ref, cst, dtt, lastt, n_heads)
        dcst[...] = jnp.zeros_like(dcst)
        lane_q = lax.broadcasted_iota(jnp.int32, (1, q), 1)
        last_hot = jnp.where(lane_q == q - 1, 1.0, 0.0)

        def total(v):
            return jnp.sum(jnp.sum(v, axis=1, keepdims=True), axis=0, keepdims=True)

        for gi in range(M2_G):
            bm = xbc_ref[:, di + gi * n:di + (gi + 1) * n]
            cm = xbc_ref[:, di + (M2_G + gi) * n:di + (M2_G + gi + 1) * n]
            cb = _bdot(cm, bm, "nt")
            dcb = jnp.zeros((q, q), F32)
            dbm = jnp.zeros((q, n), F32)
            dcm = jnp.zeros((q, n), F32)
            for p in range(gi * ppg, (gi + 1) * ppg):
                sl = slice(p * LANES, (p + 1) * LANES)
                x = xbc_ref[:, sl]
                dyp = dy_ref[:, sl]
                hs = prev_ref[0, p]
                ds = dstate[p]
                f = _ssd_pair_fwd(x, bm, cm, cb, hs, _pair_tables(cst, dtt, lastt, p))
                lo, slo = f["lo"], f["slo"]
                ddfull[:, sl] += _colsum(dyp * x)
                dy0 = jnp.where(lo, dyp, 0.0)
                dy1 = jnp.where(lo, 0.0, dyp)
                dm0 = _bdot(dyp, f["xdt0"], "nt")
                dm1 = _bdot(dyp, f["xdt1"], "nt")
                dxdt = _bdot(f["m0"], dy0, "tn") + _bdot(f["m1"], dy1, "tn")
                dcb = dcb + dm0 * f["l0"] + dm1 * f["l1"]
                w0, w1 = dm0 * f["m0"], dm1 * f["m1"]
                dz = dyp * f["e"]
                dcm = dcm + _bdot(dz, hs)
                dhs = _bdot(dz, cm, "tn") + f["cd"] * ds
                tot = ds * hs * f["cd"]
                dxdd = _bdot(bm, ds, "nt")
                dbm = dbm + _bdot(f["xdd"], ds)
                ee = dxdd * f["xdd"]
                colterm[:, sl] = dyp * f["yoff"] - ee
                dxdt = dxdt + dxdd * f["dec"]
                t_all = total(tot)
                t_lo = total(jnp.where(slo, tot, 0.0))
                e_all = total(ee)
                e_lo = total(jnp.where(lo, ee, 0.0))
                dlast0 = t_lo + e_lo
                dlast1 = (t_all - t_lo) + (e_all - e_lo)
                dcst[2 * p:2 * p + 1, :] = _colsum(w0.T - w0) + dlast0 * last_hot
                dcst[2 * p + 1:2 * p + 2, :] = _colsum(w1.T - w1) + dlast1 * last_hot
                dxbc_ref[:, sl] = d_ref[:, sl] * dyp + dxdt * f["dtp"]
                wx[:, sl] = dxdt * x
                dstate[p] = dhs
            dcm = dcm + _bdot(dcb, bm)
            dbm = dbm + _bdot(dcb, cm, "tn")
            dxbc_ref[:, di + gi * n:di + (gi + 1) * n] = dbm
            dxbc_ref[:, di + (M2_G + gi) * n:di + (M2_G + gi + 1) * n] = dcm

        segv = seg_ref[...]
        dcs = _dot(colterm[...], segv, precision=HIGHEST) + dcst[...].T
        row = lax.broadcasted_iota(jnp.int32, (q, q), 0)
        col = lax.broadcasted_iota(jnp.int32, (q, q), 1)
        ddelta = _dot(jnp.where(col >= row, 1.0, 0.0), dcs, precision=HIGHEST)
        ddt = _dot(wx[...], segv, precision=HIGHEST) + ddelta * a_ref[...]
        da_ref[...] += _colsum(ddelta * dt)
        lane = lax.broadcasted_iota(jnp.int32, (q, LANES), 1)
        ddtraw = jnp.where(lane < n_heads, ddt * _sigmoid(dtraw_ref[...] + dtb_ref[...]), 0.0)
        ddt_ref[...] = ddtraw
        ddtb_ref[...] += _colsum(ddtraw)

        @pl.when(step == nc - 1)
        def _():
            dd_ref[...] = _dot(jnp.broadcast_to(ddfull[...], (8, di)), segv, precision=HIGHEST)

    def whole(a):
        return pl.BlockSpec(a.shape, lambda i: (0, 0))

    def rev(i):
        return nc - 1 - i

    acc = jax.ShapeDtypeStruct((1, LANES), F32)
    acc_spec = pl.BlockSpec((1, LANES), lambda i: (0, 0))
    acc8 = jax.ShapeDtypeStruct((8, LANES), F32)
    acc8_spec = pl.BlockSpec((8, LANES), lambda i: (0, 0))
    return pl.pallas_call(
        kern, name=name, grid=(nc,),
        in_specs=[pl.BlockSpec((q, c), lambda i: (rev(i), 0)), pl.BlockSpec((q, LANES), lambda i: (rev(i), 0)),
                  pl.BlockSpec((q, di), lambda i: (rev(i), 0)),
                  pl.BlockSpec((1, n_pairs, 2 * M2_P, n), lambda i: (rev(i), 0, 0, 0)),
                  whole(dtb), whole(arow), whole(dvec), whole(seg)],
        out_specs=[pl.BlockSpec((q, c), lambda i: (rev(i), 0)), pl.BlockSpec((q, LANES), lambda i: (rev(i), 0)),
                   acc_spec, acc_spec, acc8_spec],
        out_shape=[jax.ShapeDtypeStruct((rows, c), F32), jax.ShapeDtypeStruct((rows, LANES), F32), acc, acc, acc8],
        scratch_shapes=[pltpu.VMEM((n_pairs, 2 * M2_P, n), F32), pltpu.VMEM((LANES, q), F32),
                        pltpu.VMEM((LANES, q), F32), pltpu.VMEM((LANES, q), F32), pltpu.VMEM((LANES, q), F32),
                        pltpu.VMEM((q, di), F32), pltpu.VMEM((q, di), F32), pltpu.VMEM((1, di), F32)],
        compiler_params=_params("arbitrary"),
    )(xbc, dtraw, dy, prev, dtb, arow, dvec, seg)


S5_PARAM_NAMES = ("s5_lambda_re", "s5_lambda_im", "s5_log_dt", "s5_b_re", "s5_b_im", "s5_c_re", "s5_c_im")


def _row(v):
    return v.reshape(1, -1)


class _Rides:
    def __init__(self):
        self.pending = {}
        self.landed = {}

    def matmul(self, site, name, *args, **kw):
        ride = self.pending.pop(site, None)
        res = _matmul(name, *args, ride=ride, **kw)
        if ride is None:
            return res
        self.landed[site] = list(res[-len(ride):])
        res = list(res[:-len(ride)])
        return res[0] if len(res) == 1 else res


def _s5_layer_fwd(tag, x, gate, h, w, j, rides):
    u = rides.matmul("s5_win", tag + "_win", h, w["s5_w_in"][j])
    params = [w[k][j] for k in S5_PARAM_NAMES]
    ops, build_vjp = jax.vjp(_s5_build, *params)
    ug = _s5_to_groups(u.astype(BF16), True)
    yg, sprev = _s5_core_fwd(tag + "_core", ug, ops)
    yy = _s5_from_groups(yg, False)
    skip = _row(w["s5_d"][j])
    gl = _gelu_fwd(tag + "_gelu", yy, u, skip)
    ab = rides.matmul("s5_wglu", tag + "_wglu", gl, w["s5_w_glu"][j])
    x1 = _glu_fwd(tag + "_glu", ab, _row(w["s5_b_glu"][j]), x, gate)
    return x1, dict(u=u, ug=ug, ops=ops, build_vjp=build_vjp, sprev=sprev, yy=yy, gl=gl, ab=ab, skip=skip)


def _s5_layer_bwd(tag, dx1, gate, h, sv, w, j, rides):
    dab, db_glu, dgate = _glu_bwd(tag + "_glu_b", sv["ab"], _row(w["s5_b_glu"][j]), dx1, gate)
    dw_glu = rides.matmul("s5_dwglu", tag + "_dwglu", sv["gl"], dab, "tn", out_dtype=BF16)
    dgl = rides.matmul("s5_dgl", tag + "_dgl", dab, w["s5_w_glu"][j], "nt")
    dyy, dskip = _gelu_bwd(tag + "_gelu_b", sv["yy"], sv["u"], sv["skip"], dgl)
    dug, dkc, dbend, dcpow, da1, da2s = _s5_core_bwd(
        tag + "_core_b", sv["ug"], _s5_to_groups(dyy.astype(BF16), False), sv["sprev"], sv["ops"])
    dparams = sv["build_vjp"]((dkc, dbend, dcpow, da1, _swap_halves(da2s, 2)))
    du = _axpy(tag + "_du", _s5_from_groups(dug, True), dyy, sv["skip"])
    grads = dict(zip(S5_PARAM_NAMES, dparams))
    grads["s5_d"] = dskip.reshape(-1)
    grads["s5_w_in"] = _matmul(tag + "_dwin", h, du, "tn", out_dtype=BF16)
    grads["s5_w_glu"] = dw_glu
    grads["s5_b_glu"] = db_glu.reshape(-1)
    dh = _matmul(tag + "_dh", du, w["s5_w_in"][j], "nt")
    return dh, grads, dgate


def _ssd_consts(w, j, d_model):
    di = 2 * d_model
    heads = di // M2_P

    def pad_row(v):
        return jnp.zeros((1, LANES), F32).at[0, :heads].set(v)

    a = -jnp.exp(w["m2_a_log"][j])
    seg = (jnp.arange(di)[:, None] // M2_P == jnp.arange(LANES)[None, :]).astype(F32)
    w_in = w["m2_w_in"][j]
    conv_dim = di + 2 * M2_G * M2_N
    w_dt = jnp.zeros((d_model, LANES), w_in.dtype).at[:, :heads].set(w_in[:, di + conv_dim:])
    return dict(di=di, heads=heads, conv_dim=conv_dim, a=a, arow=pad_row(a), dtb=pad_row(w["m2_dt_bias"][j]),
                dvec=_row(jnp.repeat(w["m2_d"][j], M2_P)), seg=seg,
                w_z=w_in[:, :di], w_xbc=w_in[:, di:di + conv_dim], w_dt=w_dt,
                conv_w=w["m2_conv_w"][j], conv_b=_row(w["m2_conv_b"][j]), norm_g=_row(w["m2_norm_g"][j]))


def _gated_out_bwd(tag, act, dxo, w_out, gate, **kw):
    dw, dgate_parts = _matmul(tag + "_dwo", act, dxo, "tn", out_dtype=BF16, colscale=gate, colsum_with=w_out, **kw)
    dgate = jnp.sum(dgate_parts, axis=0)
    return dw, dgate, _scale_cols(tag + "_wog", w_out, gate)


def _ssd_layer_fwd(tag, x, gate, h, w, j):
    k = _ssd_consts(w, j, h.shape[1])
    z = _matmul(tag + "_wz", h, k["w_z"])
    xbc_pre = _matmul(tag + "_wxbc", h, k["w_xbc"])
    dtraw = _matmul(tag + "_wdt", h, k["w_dt"])
    xbc = _conv_fwd(tag + "_conv", xbc_pre, k["conv_w"], k["conv_b"])
    y, prev = _ssd_fwd(tag + "_core", xbc, dtraw, k["dtb"], k["arow"], k["dvec"], k["heads"])
    yn = _gatenorm_fwd(tag + "_gn", y, z, k["norm_g"])
    x1 = _matmul(tag + "_wout", yn, w["m2_w_out"][j], colscale=gate, addin=x)
    return x1, dict(k=k, z=z, xbc_pre=xbc_pre, dtraw=dtraw, xbc=xbc, y=y, prev=prev, yn=yn)


def _ssd_layer_bwd(tag, dx1, gate, h, sv, w, j):
    k = sv["k"]
    heads = k["heads"]
    dw_out, dgate, wog = _gated_out_bwd(tag, sv["yn"], dx1, w["m2_w_out"][j], gate)
    grads = {"m2_w_out": dw_out}
    dyn = _matmul(tag + "_dyn", dx1, wog, "nt")
    dyssd, dz, dng = _gatenorm_bwd(tag + "_gn_b", sv["y"], sv["z"], k["norm_g"], dyn)
    dxbc, ddtraw, da, ddtb, dd = _ssd_bwd(tag + "_core_b", sv["xbc"], sv["dtraw"], dyssd, sv["prev"],
                                          k["dtb"], k["arow"], k["dvec"], k["seg"], heads)
    dpre, dcw, dcb = _conv_bwd_pre(tag + "_conv_b1", sv["xbc_pre"], dxbc, k["conv_w"], k["conv_b"])
    dxbc_pre = _conv_bwd_in(tag + "_conv_b2", dpre, k["conv_w"])
    dw_z = _matmul(tag + "_dwz", h, dz, "tn", out_dtype=BF16)
    dw_xbc = _matmul(tag + "_dwxbc", h, dxbc_pre, "tn", out_dtype=BF16)
    dw_dt = _matmul(tag + "_dwdt", h, ddtraw, "tn", out_dtype=BF16)
    dh = _matmul(tag + "_dh1", dz, k["w_z"], "nt")
    dh = _matmul(tag + "_dh2", dxbc_pre, k["w_xbc"], "nt", addin=dh)
    dh = _matmul(tag + "_dh3", ddtraw, k["w_dt"], "nt", addin=dh)
    grads["m2_w_in"] = jnp.concatenate([dw_z, dw_xbc, dw_dt[:, :heads]], axis=1)
    grads["m2_conv_w"] = dcw
    grads["m2_conv_b"] = dcb.reshape(-1)
    grads["m2_dt_bias"] = ddtb[0, :heads]
    grads["m2_a_log"] = da[0, :heads] * k["a"]
    grads["m2_d"] = dd[0, :heads]
    grads["m2_norm_g"] = dng.reshape(-1)
    return dh, grads, dgate


def _layer_fwd(li, x, mod, w, rides, late_weights=None):
    tag = "L%d" % li
    sh1, sc1, g1, sh2, sc2, g2 = mod
    j = li // 2
    h = _normmod_fwd(tag + "_nm1", x, _row(w["norm_mix_g"][li]), sh1, sc1)
    if li % 2 == 0:
        x1, mix = _s5_layer_fwd(tag + "_s5", x, g1, h, w, j, rides)
    else:
        x1, mix = _ssd_layer_fwd(tag + "_m2", x, g1, h, w, j)
    if late_weights is not None:
        w = {**w, **late_weights(rides.landed)}
    h2 = _normmod_fwd(tag + "_nm2", x1, _row(w["norm_mlp_g"][li]), sh2, sc2)
    r = rides.matmul("w1", tag + "_w1", h2, w["mlp_w1"][li], relu=True, out_dtype=BF16)
    x2 = rides.matmul("w2", tag + "_w2", r, w["mlp_w2"][li], square_a=True, colscale=g2, addin=x1)
    return x2, dict(x=x, h=h, mix=mix, x1=x1, h2=h2, r=r), w


def _layer_bwd(li, dx2, sv, mod, w, rides, ride_own_mlp):
    tag = "L%d" % li
    sh1, sc1, g1, sh2, sc2, g2 = mod
    j = li // 2
    dw2, dg2, w2g = _gated_out_bwd(tag + "_mlp", sv["r"], dx2, w["mlp_w2"][li], g2, square_a=True)
    dr = rides.matmul("dr", tag + "_dr", dx2, w2g, "nt", out_dtype=BF16, mul2=sv["r"])
    dw1 = _matmul(tag + "_dw1", sv["h2"], dr, "tn", out_dtype=BF16)
    grads = {"mlp_w2": dw2, "mlp_w1": dw1}
    dh2 = rides.matmul("dh2", tag + "_dh2", dr, w["mlp_w1"][li], "nt")
    if ride_own_mlp:
        mlp_bufs = _grad_buffers(li, grads, parts=[1])[0]
        rides.pending["s5_dgl"], rides.pending["s5_dwglu"] = mlp_bufs[:1], mlp_bufs[1:]
    dx1, dgm, dsh2, dsc2 = _normmod_bwd(tag + "_nm2_b", sv["x1"], _row(w["norm_mlp_g"][li]), sh2, sc2, dh2, dx2)
    if li % 2 == 0:
        dh, mix_grads, dg1 = _s5_layer_bwd(tag + "_s5", dx1, g1, sv["h"], sv["mix"], w, j, rides)
    else:
        dh, mix_grads, dg1 = _ssd_layer_bwd(tag + "_m2", dx1, g1, sv["h"], sv["mix"], w, j)
    dx, dgx, dsh1, dsc1 = _normmod_bwd(tag + "_nm1_b", sv["x"], _row(w["norm_mix_g"][li]), sh1, sc1, dh, dx1)
    grads["norm_mix_g"] = dgx.reshape(-1)
    grads["norm_mlp_g"] = dgm.reshape(-1)
    dmod = jnp.concatenate([dsh1, dsc1, dg1, dsh2, dsc2, dg2], axis=1)
    return dx, {**grads, **mix_grads}, dmod


def _layer_parts(li):
    j = li // 2
    mix = [("s5_w_in", j, 0), ("s5_w_glu", j, 1)] if li % 2 == 0 else [("m2_w_in", j, 1), ("m2_w_out", j, 0)]
    return [mix, [("mlp_w1", li, 1), ("mlp_w2", li, 0)]]


def _grad_specs(li):
    j = li // 2
    mlp = [("one", ("mlp_w2", li, 0)), ("one", ("mlp_w1", li, 1))]
    if li % 2 == 0:
        return [[("one", ("s5_w_in", j, 0)), ("one", ("s5_w_glu", j, 1))], mlp]
    packed = [("m2_w_in", j, 1), ("m2_conv_w", j, 1), ("m2_conv_b", j, 0), ("m2_norm_g", j, 0)]
    return [[("one", ("m2_w_out", j, 0)), ("packed", packed)], mlp]


def _grad_buffers(li, grads, parts=(0, 1)):
    out = []
    for k in parts:
        bufs = []
        for kind, entry in _grad_specs(li)[k]:
            if kind == "one":
                bufs.append((grads[entry[0]], "rows" if entry[2] == 0 else "cols"))
            else:
                pieces = [_pack([_chip_slice(grads[n], c, ax) for n, _, ax in entry], BF16) for c in range(N_CHIP)]
                bufs.append((jnp.stack(pieces), "packed"))
        out.append(bufs)
    return out


def _gather_buffers(local, part):
    return [(local[n][i].astype(BF16), "same") for n, i, _ in part]


def _assemble(landed, part):
    out = {}
    for buf, (n, i, ax) in zip(landed, part):
        if ax == 0:
            out[n] = {i: buf.reshape(-1, buf.shape[2])}
        else:
            out[n] = {i: jnp.concatenate([buf[k] for k in range(N_CHIP)], axis=1)}
    return out


def _local_step(x, target, mods, w, local):
    depth = w["norm_mix_g"].shape[0]
    d = x.shape[1]
    rides = _Rides()
    saved, mod_rows, layer_w = [], [], []
    mix0, mlp0 = _layer_parts(0)
    wl = {**w, **_assemble(_exchange4("ag_w_L0", _gather_buffers(local, mix0)), mix0)}
    rides.pending["s5_win"] = _gather_buffers(local, mlp0[:1])
    rides.pending["s5_wglu"] = _gather_buffers(local, mlp0[1:])

    def late_mlp0(landed):
        return _assemble(landed.pop("s5_win") + landed.pop("s5_wglu"), mlp0)

    for li in range(depth):
        if li + 1 < depth:
            nxt = _layer_parts(li + 1)
            rides.pending["w1"] = _gather_buffers(local, nxt[0])
            rides.pending["w2"] = _gather_buffers(local, nxt[1])
        mod = [mods[li:li + 1, i * d:(i + 1) * d] for i in range(N_MOD)]
        mod_rows.append(mod)
        x, sv, wl = _layer_fwd(li, x, mod, wl, rides, late_mlp0 if li == 0 else None)
        saved.append(sv)
        layer_w.append(wl)
        if li + 1 < depth:
            wl = {**w, **_assemble(rides.landed.pop("w1"), nxt[0]), **_assemble(rides.landed.pop("w2"), nxt[1])}
    dx, dgf, loss = _loss_head("loss_head", x, target, _row(w["final_norm_g"]))
    layer_grads = [None] * depth
    dmods = [None] * depth
    landed = {}
    for li in reversed(range(depth)):
        dx, layer_grads[li], dmods[li] = _layer_bwd(li, dx, saved[li], mod_rows[li], layer_w[li], rides, li == 0)
        if li + 1 < depth:
            landed[(li + 1, 0)] = rides.landed.pop("dr")
            landed[(li + 1, 1)] = rides.landed.pop("dh2")
        if li > 0:
            rides.pending["dr"], rides.pending["dh2"] = _grad_buffers(li, layer_grads[li])
    landed[(0, 1)] = rides.landed.pop("s5_dgl") + rides.landed.pop("s5_dwglu")
    landed[(0, 0)] = _exchange4("rs_g_L0", _grad_buffers(0, layer_grads[0], parts=[0])[0])
    return loss, dx, layer_grads, dgf.reshape(-1), jnp.concatenate(dmods, axis=0), landed


ANY = pl.BlockSpec(memory_space=pl.ANY)
N_DEV = 8
N_CHIP = 4


def _coords():
    return lax.axis_index("x"), lax.axis_index("y"), lax.axis_index("c")


def _allgather8(name, block):
    r, wd = block.shape

    def body(x_ref, out_ref, send_sems, recv_sems, local_sem):
        x, y, c = _coords()
        me, sibling = (x, y, c), (x, y, 1 - c)
        chips = [(1 - x, y), (x, 1 - y), (1 - x, 1 - y)]

        def slot(px, py, pc):
            return out_ref.at[4 * px + 2 * py + pc]

        def copy(k, blk, to, src=None):
            return pltpu.make_async_remote_copy(
                src_ref=slot(*blk) if src is None else src, dst_ref=slot(*blk),
                send_sem=send_sems.at[k], recv_sem=recv_sems.at[k], device_id=to, device_id_type=MESH)

        mine = pltpu.make_async_copy(x_ref, slot(*me), local_sem)
        mine.start()
        first = [copy(0, me, sibling, src=x_ref)]
        first += [copy(1 + j, me, (*chip, c), src=x_ref) for j, chip in enumerate(chips)]
        for cp in first:
            cp.start()
        passed = [copy(4 + j, (*chip, c), sibling) for j, chip in enumerate(chips)]
        for j, chip in enumerate(chips):
            copy(1 + j, (*chip, c), me).wait_recv()
            passed[j].start()
        copy(0, sibling, me).wait_recv()
        for j, chip in enumerate(chips):
            copy(4 + j, (*chip, 1 - c), me).wait_recv()
        for cp in first + passed:
            cp.wait_send()
        mine.wait()

    return pl.pallas_call(
        body, name=name, in_specs=[ANY], out_specs=ANY,
        out_shape=jax.ShapeDtypeStruct((N_DEV, r, wd), block.dtype),
        scratch_shapes=[pltpu.SemaphoreType.DMA((7,)), pltpu.SemaphoreType.DMA((7,)), pltpu.SemaphoreType.DMA],
    )(block)


def _landing_shape(src, kind):
    if kind == "same":
        return (N_CHIP,) + src.shape
    if kind == "packed":
        return src.shape
    rows, cols = src.shape
    return (N_CHIP, rows // N_CHIP, cols) if kind == "rows" else (N_CHIP, rows, cols // N_CHIP)


def _exchange4_ops(srcs, dsts, send_sems, recv_sems, local_sems, kinds):
    x, y, c = _coords()
    my_chip = 2 * x + y
    chips = [(1 - x, y), (x, 1 - y), (1 - x, 1 - y)]

    def piece(q, k):
        ref, kind = srcs[q], kinds[q]
        if kind == "same":
            return ref
        if kind == "packed":
            return ref.at[k]
        _, rows, cols = dsts[q].shape
        return ref.at[pl.ds(k * rows, rows), :] if kind == "rows" else ref.at[:, pl.ds(k * cols, cols)]

    def copy(q, j, k, slot):
        px, py = chips[j]
        return pltpu.make_async_remote_copy(
            src_ref=piece(q, k), dst_ref=dsts[q].at[slot], send_sem=send_sems.at[3 * q + j],
            recv_sem=recv_sems.at[3 * q + j], device_id=(px, py, c), device_id_type=MESH)

    def mine(q):
        return pltpu.make_async_copy(piece(q, my_chip), dsts[q].at[my_chip], local_sems.at[q])

    def start():
        for q in range(len(srcs)):
            mine(q).start()
            for j, (px, py) in enumerate(chips):
                copy(q, j, 2 * px + py, my_chip).start()

    def wait():
        for q in range(len(srcs)):
            for j, (px, py) in enumerate(chips):
                copy(q, j, my_chip, 2 * px + py).wait_recv()
        for q in range(len(srcs)):
            for j, (px, py) in enumerate(chips):
                copy(q, j, 2 * px + py, my_chip).wait_send()
            mine(q).wait()

    return start, wait


def _exchange_scratch(n):
    return [pltpu.SemaphoreType.DMA((3 * n,)), pltpu.SemaphoreType.DMA((3 * n,)), pltpu.SemaphoreType.DMA((n,))]


def _exchange4(name, buffers):
    n = len(buffers)
    kinds = [kind for _, kind in buffers]

    def body(*refs):
        start, wait = _exchange4_ops(refs[:n], refs[n:2 * n], *refs[2 * n:], kinds)
        start()
        wait()

    return pl.pallas_call(
        body, name=name, in_specs=[ANY] * n, out_specs=[ANY] * n,
        out_shape=[jax.ShapeDtypeStruct(_landing_shape(s, k), s.dtype) for s, k in buffers],
        scratch_shapes=_exchange_scratch(n),
    )(*[s for s, _ in buffers])


def _swap_sibling(name, block):
    def body(x_ref, out_ref, send_sem, recv_sem):
        x, y, c = _coords()
        cp = pltpu.make_async_remote_copy(src_ref=x_ref, dst_ref=out_ref, send_sem=send_sem, recv_sem=recv_sem,
                                          device_id=(x, y, 1 - c), device_id_type=MESH)
        cp.start()
        cp.wait()

    return pl.pallas_call(
        body, name=name, in_specs=[ANY], out_specs=ANY, out_shape=jax.ShapeDtypeStruct(block.shape, block.dtype),
        scratch_shapes=[pltpu.SemaphoreType.DMA, pltpu.SemaphoreType.DMA],
    )(block)


def _sum_slots(name, stacked):
    n, r, wd = stacked.shape
    tile = min(FLAT_ROWS, r)

    def kern(x_ref, o_ref):
        acc = x_ref[0].astype(F32)
        for s in range(1, n):
            acc = acc + x_ref[s].astype(F32)
        o_ref[...] = acc

    return pl.pallas_call(
        kern, name=name, grid=(r // tile,), in_specs=[pl.BlockSpec((n, tile, wd), lambda i: (0, i, 0))],
        out_specs=pl.BlockSpec((tile, wd), lambda i: (i, 0)), out_shape=jax.ShapeDtypeStruct((r, wd), F32),
        compiler_params=_params("parallel"),
    )(stacked)


def _adamw(name, w, m, v, g, g2=None):
    r, wd = w.shape
    grads = [g] if g2 is None else [g, g2]
    c1 = 1.0 - ADAM_B1 ** ADAM_STEP
    c2 = 1.0 - ADAM_B2 ** ADAM_STEP

    def body(i, *refs):
        w_ref, m_ref, v_ref = refs[:3]
        g_refs = refs[3:3 + len(grads)]
        go_ref, d_ref, mo_ref, vo_ref = refs[3 + len(grads):]
        gv = g_refs[0][...]
        if g2 is not None:
            gv = gv + g_refs[1][...]
        mn = ADAM_B1 * m_ref[...] + (1.0 - ADAM_B1) * gv
        vn = ADAM_B2 * v_ref[...] + (1.0 - ADAM_B2) * (gv * gv)
        go_ref[...] = gv
        mo_ref[...] = mn
        vo_ref[...] = vn
        d_ref[...] = -ADAM_LR * ((mn / c1) / (jnp.sqrt(vn / c2) + ADAM_EPS) + ADAM_WD * w_ref[...])

    return _rowcall(name, body, r, FLAT_ROWS, [w, m, v] + grads, [], [(wd, F32)] * 4, [])


FLAT_BLOCK = FLAT_ROWS * FLAT_W


def _pack(arrays, dtype):
    flat = jnp.concatenate([a.reshape(-1).astype(dtype) for a in arrays])
    pad = (-flat.shape[0]) % FLAT_BLOCK
    return jnp.pad(flat, (0, pad)).reshape(-1, FLAT_W)


def _unpack(buf, shapes):
    flat = buf.reshape(-1)
    out, off = [], 0
    for s in shapes:
        n = math.prod(s)
        out.append(flat[off:off + n].reshape(s))
        off += n
    return out


SHARDED_BIG = {"mlp_w1": 2, "mlp_w2": 1, "s5_w_in": 1, "s5_w_glu": 2, "m2_w_in": 2, "m2_w_out": 1}
SHARDED_SMALL = {"m2_conv_w": 2, "m2_conv_b": 1, "m2_norm_g": 1}
REPLICATED = ("ada_b", "norm_mix_g", "norm_mlp_g", "s5_lambda_re", "s5_lambda_im", "s5_log_dt", "s5_b_re",
              "s5_b_im", "s5_c_re", "s5_c_im", "s5_d", "s5_b_glu", "m2_dt_bias", "m2_a_log", "m2_d", "final_norm_g")
WEIGHT_NAMES = ("ada_w", "ada_b", "norm_mix_g", "norm_mlp_g", "mlp_w1", "mlp_w2", "s5_w_in", "s5_lambda_re",
                "s5_lambda_im", "s5_log_dt", "s5_b_re", "s5_b_im", "s5_c_re", "s5_c_im", "s5_d", "s5_w_glu",
                "s5_b_glu", "m2_w_in", "m2_conv_w", "m2_conv_b", "m2_dt_bias", "m2_a_log", "m2_d", "m2_norm_g",
                "m2_w_out", "final_norm_g")


def _gather_weights(name, local, names_axes, dtype):
    names = list(names_axes)
    got = _exchange4(name, [(_pack([local[k] for k in names], dtype), "same")])[0]
    per_chip = [_unpack(got[j], [local[k].shape for k in names]) for j in range(N_CHIP)]
    return {k: jnp.concatenate([per_chip[j][i] for j in range(N_CHIP)], axis=names_axes[k])
            for i, k in enumerate(names)}


def _chip_slice(a, chip, axis):
    size = a.shape[axis] // N_CHIP
    return lax.slice_in_dim(a, chip * size, (chip + 1) * size, axis=axis)


def kernel(x, c, ada_w, ada_b, norm_mix_g, norm_mlp_g, mlp_w1, mlp_w2, s5_w_in, s5_lambda_re, s5_lambda_im, s5_log_dt, s5_b_re, s5_b_im, s5_c_re, s5_c_im, s5_d, s5_w_glu, s5_b_glu, m2_w_in, m2_conv_w, m2_conv_b, m2_dt_bias, m2_a_log, m2_d, m2_norm_g, m2_w_out, final_norm_g, loss_target, m_ada_w, m_ada_b, m_norm_mix_g, m_norm_mlp_g, m_mlp_w1, m_mlp_w2, m_s5_w_in, m_s5_lambda_re, m_s5_lambda_im, m_s5_log_dt, m_s5_b_re, m_s5_b_im, m_s5_c_re, m_s5_c_im, m_s5_d, m_s5_w_glu, m_s5_b_glu, m_m2_w_in, m_m2_conv_w, m_m2_conv_b, m_m2_dt_bias, m_m2_a_log, m_m2_d, m_m2_norm_g, m_m2_w_out, m_final_norm_g, v_ada_w, v_ada_b, v_norm_mix_g, v_norm_mlp_g, v_mlp_w1, v_mlp_w2, v_s5_w_in, v_s5_lambda_re, v_s5_lambda_im, v_s5_log_dt, v_s5_b_re, v_s5_b_im, v_s5_c_re, v_s5_c_im, v_s5_d, v_s5_w_glu, v_s5_b_glu, v_m2_w_in, v_m2_conv_w, v_m2_conv_b, v_m2_dt_bias, v_m2_a_log, v_m2_d, v_m2_norm_g, v_m2_w_out, v_final_norm_g):
    args = locals()
    local = {k: args[k] for k in WEIGHT_NAMES}
    mom_m = {k: args["m_" + k] for k in WEIGHT_NAMES}
    mom_v = {k: args["v_" + k] for k in WEIGHT_NAMES}
    depth, d = norm_mix_g.shape
    xi, yi, ci = _coords()
    my_chip = 2 * xi + yi
    my_dev = 2 * my_chip + ci

    cond = jax.nn.silu(c).reshape(-1, LANES)
    cond_all = _allgather8("ag_cond", cond).reshape(N_DEV, d)
    cond_pad = jnp.zeros((LANES, d), F32).at[:N_DEV].set(cond_all)
    mod_cols = ada_w.shape[2]
    mod_part = jnp.stack([_matmul("ada_%d" % i, cond_pad, ada_w[i])[:N_DEV] for i in range(depth)])
    mod_all = _allgather8("ag_mod", mod_part.reshape(-1, LANES)).reshape(N_CHIP, 2, depth, N_DEV, mod_cols)[:, 0]
    mod_mine = lax.dynamic_index_in_dim(mod_all, my_dev, axis=2, keepdims=False)
    mods = jnp.transpose(mod_mine, (1, 0, 2)).reshape(depth, N_CHIP * mod_cols) + ada_b

    w = {k: local[k] for k in REPLICATED}
    w.update(_gather_weights("ag_w_small", local, SHARDED_SMALL, F32))

    loss_row, dx, layer_grads, g_final, dmods, landed = _local_step(x[0], loss_target[0], mods, w, local)
    grads = {"ada_b": dmods, "final_norm_g": g_final}
    for k in REPLICATED[1:-1]:
        grads[k] = jnp.stack([g[k] for g in layer_grads if k in g])

    rep_shapes = [grads[k].shape for k in REPLICATED]
    rep_all = _allgather8("ag_grep", _pack([grads[k] for k in REPLICATED], F32))
    rep_sum = _sum_slots("sum_grep", rep_all)
    dmods_all = rep_all.reshape(N_DEV, -1)[:, :dmods.size].reshape(N_DEV, depth, N_CHIP * mod_cols)

    dm_mine = lax.dynamic_slice_in_dim(dmods_all, my_chip * mod_cols, mod_cols, axis=2)
    dm_pad = jnp.zeros((LANES, depth, mod_cols), F32).at[:N_DEV].set(dm_mine)
    g_ada_w = jnp.stack([_matmul("dada_%d" % i, cond_pad, dm_pad[:, i], "tn") for i in range(depth)])

    red = {}
    for li in range(depth):
        for k, part in enumerate(_grad_specs(li)):
            for (kind, entry), land in zip(part, landed[(li, k)]):
                if kind == "one":
                    red[entry[:2]] = _sum_slots("sum_%s_%d" % entry[:2], land)
                else:
                    total = _sum_slots("sum_packed_L%d" % li, land)
                    shapes = [local[n][i].shape for n, i, _ in entry]
                    red.update({(n, i): v for (n, i, _), v in zip(entry, _unpack(total, shapes))})

    out_g, out_d, out_m, out_v = {}, {}, {}, {}
    for name in list(SHARDED_BIG) + list(SHARDED_SMALL):
        shape = local[name].shape
        flat2 = (-1, shape[-1])
        g1 = jnp.stack([red[(name, i)] for i in range(shape[0])]).reshape(flat2)
        g2 = _swap_sibling("swap_" + name, g1)
        res = _adamw("adam_" + name, local[name].reshape(flat2), mom_m[name].reshape(flat2),
                     mom_v[name].reshape(flat2), g1, g2)
        for dst, buf in zip((out_g, out_d, out_m, out_v), res):
            dst[name] = buf.reshape(shape)
    res = _adamw("adam_rep", _pack([local[k] for k in REPLICATED], F32), _pack([mom_m[k] for k in REPLICATED], F32),
                 _pack([mom_v[k] for k in REPLICATED], F32), rep_sum)
    for dst, buf in zip((out_g, out_d, out_m, out_v), res):
        dst.update(zip(REPLICATED, _unpack(buf, rep_shapes)))
    flat2 = (-1, mod_cols)
    res = _adamw("adam_ada", ada_w.reshape(flat2), m_ada_w.reshape(flat2), v_ada_w.reshape(flat2),
                 g_ada_w.reshape(flat2))
    for dst, buf in zip((out_g, out_d, out_m, out_v), res):
        dst["ada_w"] = buf.reshape(ada_w.shape)

    loss = lax.psum(loss_row[0, 0], ("x", "y", "c"))
    outs = [loss, dx[None]]
    for dst in (out_g, out_d, out_m, out_v):
        outs += [dst[k] for k in WEIGHT_NAMES]
    return tuple(outs)
```

```python
import functools
import math

import jax
import jax.numpy as jnp
from jax import lax
from jax.experimental import pallas as pl
from jax.experimental.pallas import tpu as pltpu

F32 = jnp.float32
BF16 = jnp.bfloat16
HIGHEST = lax.Precision.HIGHEST

NORM_EPS = 1e-5
N_MOD = 6
S5_H, S5_P, S5_T = 16, 64, 64
M2_P, M2_N, M2_G, M2_Q, M2_K = 64, 128, 4, 128, 4
LANES = 128
ADAM_LR, ADAM_B1, ADAM_B2, ADAM_EPS, ADAM_WD, ADAM_STEP = 0.001, 0.9, 0.999, 1e-08, 0.01, 10
VMEM_LIMIT_BYTES = 56 * 1024 * 1024
ROW_TILE = 256
FLAT_W = 1024
FLAT_ROWS = 256
MESH = pl.DeviceIdType.MESH


def _params(*sem):
    return pltpu.CompilerParams(dimension_semantics=sem, vmem_limit_bytes=VMEM_LIMIT_BYTES)


def _dot(a, b, dn="nn", precision=None):
    dims = {"nn": ((1,), (0,)), "nt": ((1,), (1,)), "tn": ((0,), (0,))}[dn]
    return lax.dot_general(a, b, (dims, ((), ())), preferred_element_type=F32, precision=precision)


def _bdot(a, b, dn="nn"):
    return _dot(a.astype(BF16), b.astype(BF16), dn)


def _sigmoid(x):
    return jax.nn.sigmoid(x)


def _colsum(x):
    return jnp.sum(x, axis=0, keepdims=True)


def _pick_tile(dim, want):
    if dim <= want:
        return dim
    for t in range(want - want % LANES, 0, -LANES):
        if dim % t == 0:
            return t
    raise ValueError((dim, want))


MATMUL_TILE = 1024
MATMUL_VMEM_BUDGET = 44 * 1024 * 1024


def _matmul_tiles(m, n, k, mode, in_bytes, out_bytes):
    tn = _pick_tile(n, MATMUL_TILE)
    k_tiles = [k] + [t for t in (4096, 2048, 1024) if t < k and k % t == 0]
    m_tiles = [_pick_tile(m, MATMUL_TILE)] + ([512] if mode != "tn" and m % 512 == 0 and m > 512 else [])
    for tk in k_tiles:
        for tm in m_tiles:
            blocks = 2 * (tm * tk * in_bytes[0] + tk * tn * in_bytes[1] + tm * tn * out_bytes)
            if blocks + (4 * tm * tn if tk < k else 0) <= MATMUL_VMEM_BUDGET:
                return tm, tn, tk
    raise ValueError((m, n, k))


def _matmul(name, a, b, mode="nn", out_dtype=F32, relu=False, square_a=False, mul2=None, colscale=None,
            addin=None, colsum_with=None, ride=None):
    if mode == "nn":
        (m, k), (k2, n) = a.shape, b.shape
    elif mode == "nt":
        (m, k), (n, k2) = a.shape, b.shape
    else:
        (k, m), (k2, n) = a.shape, b.shape
    assert k == k2, (name, a.shape, b.shape)
    tiles = [e for e in (mul2, addin, colsum_with) if e is not None]
    out_bytes = jnp.dtype(out_dtype).itemsize + sum(e.dtype.itemsize for e in tiles)
    tm, tn, tk = _matmul_tiles(m, n, k, mode, (a.dtype.itemsize, b.dtype.itemsize), out_bytes)
    nk = k // tk
    n_ext = len(tiles) + (colscale is not None)
    n_out = 1 + (colsum_with is not None)
    n_ride = 0 if ride is None else len(ride)
    grid = (m // tm, n // tn, nk)

    def kern(*refs):
        a_ref, b_ref = refs[:2]
        e_refs = list(refs[2:2 + n_ext])
        o_refs = refs[2 + n_ext + n_ride:2 + n_ext + n_ride + n_out]
        kk = pl.program_id(2)
        if ride is not None:
            here = [pl.program_id(ax) for ax in range(3)]
            land0 = 2 + n_ext + n_ride + n_out
            ride_refs = (refs[2 + n_ext:2 + n_ext + n_ride], refs[land0:land0 + n_ride]) + tuple(refs[-3:])
            ride_kinds = [kind for _, kind in ride]
            first = (here[0] == 0) & (here[1] == 0) & (here[2] == 0)
            last = (here[0] == grid[0] - 1) & (here[1] == grid[1] - 1) & (here[2] == grid[2] - 1)

            @pl.when(first)
            def _():
                _exchange4_ops(*ride_refs, ride_kinds)[0]()

        av = a_ref[...]
        if square_a:
            av = av * av
        part = _bdot(av, b_ref[...], mode)

        def finish(r):
            ext = list(e_refs)
            m2v = ext.pop(0)[...].astype(F32) if mul2 is not None else None
            addv = ext.pop(0)[...].astype(F32) if addin is not None else None
            if colsum_with is not None:
                o_refs[1][0] = _colsum(r * ext.pop(0)[...].astype(F32))
            if relu:
                r = jnp.maximum(r, 0.0)
            if m2v is not None:
                r = r * (2.0 * m2v)
            if colscale is not None:
                r = r * ext.pop(0)[...]
            if addv is not None:
                r = r + addv
            o_refs[0][...] = r.astype(out_dtype)

        if nk == 1:
            finish(part)
        else:
            acc = refs[-4] if ride is not None else refs[-1]

            @pl.when(kk == 0)
            def _():
                acc[...] = part

            @pl.when(kk > 0)
            def _():
                acc[...] += part

            @pl.when(kk == nk - 1)
            def _():
                finish(acc[...])

        if ride is not None:
            @pl.when(last)
            def _():
                _exchange4_ops(*ride_refs, ride_kinds)[1]()

    if mode == "tn":
        a_spec = pl.BlockSpec((tk, tm), lambda i, j, kk: (kk, i))
    else:
        a_spec = pl.BlockSpec((tm, tk), lambda i, j, kk: (i, kk))
    if mode == "nt":
        b_spec = pl.BlockSpec((tn, tk), lambda i, j, kk: (j, kk))
    else:
        b_spec = pl.BlockSpec((tk, tn), lambda i, j, kk: (kk, j))
    o_spec = pl.BlockSpec((tm, tn), lambda i, j, kk: (i, j))
    in_specs = [a_spec, b_spec] + [o_spec] * len(tiles)
    operands = [a, b] + tiles
    if colscale is not None:
        in_specs.append(pl.BlockSpec((1, tn), lambda i, j, kk: (0, j)))
        operands.append(colscale)
    out_specs = [o_spec]
    out_shape = [jax.ShapeDtypeStruct((m, n), out_dtype)]
    if colsum_with is not None:
        out_specs.append(pl.BlockSpec((1, 1, tn), lambda i, j, kk: (i, 0, j)))
        out_shape.append(jax.ShapeDtypeStruct((m // tm, 1, n), F32))
    scratch = [pltpu.VMEM((tm, tn), F32)] if nk > 1 else []
    semantics = ("parallel", "parallel", "arbitrary")
    if ride is not None:
        for src, kind in ride:
            in_specs.append(pl.BlockSpec(memory_space=pl.ANY))
            operands.append(src)
            out_specs.append(pl.BlockSpec(memory_space=pl.ANY))
            out_shape.append(jax.ShapeDtypeStruct(_landing_shape(src, kind), src.dtype))
        scratch += _exchange_scratch(n_ride)
        semantics = ("arbitrary", "arbitrary", "arbitrary")
    res = pl.pallas_call(
        kern, name=name, grid=grid, in_specs=in_specs, out_specs=out_specs, out_shape=out_shape,
        scratch_shapes=scratch, compiler_params=_params(*semantics),
    )(*operands)
    return res if len(res) > 1 else res[0]


def _rowcall(name, body, rows, tile, row_ins, small_ins, row_outs, acc_outs):
    tile = min(tile, rows)
    assert rows % tile == 0, (name, rows, tile)
    n_in = len(row_ins) + len(small_ins)

    def kern(*refs):
        i = pl.program_id(0)
        accs = refs[n_in + len(row_outs):]

        @pl.when(i == 0)
        def _():
            for acc in accs:
                acc[...] = jnp.zeros_like(acc)

        body(i, *refs)

    def whole(shape):
        return pl.BlockSpec(shape, lambda i, nd=len(shape): (0,) * nd)

    in_specs = [pl.BlockSpec((tile, a.shape[1]), lambda i: (i, 0)) for a in row_ins]
    in_specs += [whole(a.shape) for a in small_ins]
    out_specs = [pl.BlockSpec((tile, w), lambda i: (i, 0)) for (w, _) in row_outs]
    out_specs += [whole(s) for s in acc_outs]
    out_shape = [jax.ShapeDtypeStruct((rows, w), dt) for (w, dt) in row_outs]
    out_shape += [jax.ShapeDtypeStruct(s, F32) for s in acc_outs]
    return pl.pallas_call(
        kern, name=name, grid=(rows // tile,), in_specs=in_specs, out_specs=out_specs, out_shape=out_shape,
        compiler_params=_params("arbitrary"),
    )(*row_ins, *small_ins)


def _rms(x):
    r = lax.rsqrt(jnp.mean(x * x, axis=-1, keepdims=True) + NORM_EPS)
    return x * r, r


def _rms_bwd(dxhat, xhat, r):
    return r * (dxhat - xhat * jnp.mean(dxhat * xhat, axis=-1, keepdims=True))


def _normmod_fwd(name, x, g, sh, sc):
    def body(i, x_ref, g_ref, sh_ref, sc_ref, o_ref):
        xhat, _ = _rms(x_ref[...])
        o_ref[...] = ((xhat * g_ref[...]) * (1.0 + sc_ref[...]) + sh_ref[...]).astype(BF16)

    return _rowcall(name, body, x.shape[0], ROW_TILE, [x], [g, sh, sc], [(x.shape[1], BF16)], [])[0]


def _normmod_bwd(name, x, g, sh, sc, dh, dx_pass):
    d = x.shape[1]

    def body(i, x_ref, dh_ref, dxp_ref, g_ref, sh_ref, sc_ref, dx_ref, dg_ref, dsh_ref, dsc_ref):
        xhat, r = _rms(x_ref[...])
        dh = dh_ref[...].astype(F32)
        gv = g_ref[...]
        dn = dh * (1.0 + sc_ref[...])
        dsc_ref[...] += _colsum(dh * (xhat * gv))
        dsh_ref[...] += _colsum(dh)
        dg_ref[...] += _colsum(dn * xhat)
        dx_ref[...] = dxp_ref[...] + _rms_bwd(dn * gv, xhat, r)

    return _rowcall(name, body, x.shape[0], ROW_TILE, [x, dh, dx_pass], [g, sh, sc], [(d, F32)],
                    [(1, d), (1, d), (1, d)])


def _scale_cols(name, w, g):
    def body(i, w_ref, g_ref, o_ref):
        o_ref[...] = (w_ref[...].astype(F32) * g_ref[...]).astype(BF16)

    return _rowcall(name, body, w.shape[0], ROW_TILE, [w], [g], [(w.shape[1], BF16)], [])[0]


GELU_K = math.sqrt(2.0 / math.pi)
GELU_C = 0.044715


def _gelu_fwd(name, y, u, skip):
    def body(i, y_ref, u_ref, s_ref, o_ref):
        v = y_ref[...].astype(F32) + s_ref[...] * u_ref[...]
        t = jnp.tanh(GELU_K * (v + GELU_C * (v * v * v)))
        o_ref[...] = (0.5 * v * (1.0 + t)).astype(BF16)

    return _rowcall(name, body, y.shape[0], ROW_TILE, [y, u], [skip], [(y.shape[1], BF16)], [])[0]


def _gelu_bwd(name, y, u, skip, dgl):
    d = y.shape[1]

    def body(i, y_ref, u_ref, d_ref, s_ref, o_ref, ds_ref):
        uv = u_ref[...]
        v = y_ref[...].astype(F32) + s_ref[...] * uv
        t = jnp.tanh(GELU_K * (v + GELU_C * (v * v * v)))
        dv = d_ref[...] * (0.5 * (1.0 + t) + 0.5 * v * (1.0 - t * t) * (GELU_K * (1.0 + 3.0 * GELU_C * v * v)))
        o_ref[...] = dv
        ds_ref[...] += _colsum(dv * uv)

    return _rowcall(name, body, y.shape[0], ROW_TILE, [y, u, dgl], [skip], [(d, F32)], [(1, d)])


def _axpy(name, a, b, scale):
    def body(i, a_ref, b_ref, s_ref, o_ref):
        o_ref[...] = (a_ref[...].astype(F32) + s_ref[...] * b_ref[...]).astype(BF16)

    return _rowcall(name, body, a.shape[0], ROW_TILE, [a, b], [scale], [(a.shape[1], BF16)], [])[0]


def _glu_fwd(name, ab, bias, x, gate):
    d = ab.shape[1] // 2

    def body(i, ab_ref, x_ref, b_ref, g_ref, o_ref):
        v = ab_ref[:, :d] + b_ref[:, :d]
        gt = ab_ref[:, d:] + b_ref[:, d:]
        o_ref[...] = x_ref[...] + g_ref[...] * (v * _sigmoid(gt))

    return _rowcall(name, body, ab.shape[0], ROW_TILE, [ab, x], [bias, gate], [(d, F32)], [])[0]


def _glu_bwd(name, ab, bias, dxo, gate):
    d = ab.shape[1] // 2

    def body(i, ab_ref, dx_ref, b_ref, g_ref, dab_ref, db_ref, dg_ref):
        v = ab_ref[:, :d] + b_ref[:, :d]
        s = _sigmoid(ab_ref[:, d:] + b_ref[:, d:])
        dxo_v = dx_ref[...]
        dg_ref[...] += _colsum(dxo_v * (v * s))
        do = g_ref[...] * dxo_v
        dv = do * s
        dgt = do * v * (s * (1.0 - s))
        dab_ref[:, :d] = dv.astype(BF16)
        dab_ref[:, d:] = dgt.astype(BF16)
        db_ref[:, :d] += _colsum(dv)
        db_ref[:, d:] += _colsum(dgt)

    return _rowcall(name, body, ab.shape[0], ROW_TILE, [ab, dxo], [bias, gate], [(2 * d, BF16)],
                    [(1, 2 * d), (1, d)])


def _gatenorm_fwd(name, y, z, ng):
    di = y.shape[1]
    gw = di // M2_G

    def body(i, y_ref, z_ref, g_ref, o_ref):
        for gi in range(M2_G):
            sl = slice(gi * gw, (gi + 1) * gw)
            zz = z_ref[:, sl]
            y2 = y_ref[:, sl] * (zz * _sigmoid(zz))
            yh, _ = _rms(y2)
            o_ref[:, sl] = (yh * g_ref[:, sl]).astype(BF16)

    return _rowcall(name, body, y.shape[0], ROW_TILE, [y, z], [ng], [(di, BF16)], [])[0]


def _gatenorm_bwd(name, y, z, ng, dyn):
    di = y.shape[1]
    gw = di // M2_G

    def body(i, y_ref, z_ref, d_ref, g_ref, dy_ref, dz_ref, dg_ref):
        for gi in range(M2_G):
            sl = slice(gi * gw, (gi + 1) * gw)
            zz = z_ref[:, sl]
            yy = y_ref[:, sl]
            s = _sigmoid(zz)
            sz = zz * s
            yh, r = _rms(yy * sz)
            dn = d_ref[:, sl]
            dg_ref[:, sl] += _colsum(dn * yh)
            dy2 = _rms_bwd(dn * g_ref[:, sl], yh, r)
            dy_ref[:, sl] = dy2 * sz
            dz_ref[:, sl] = (dy2 * yy * (s * (1.0 + zz * (1.0 - s)))).astype(BF16)

    return _rowcall(name, body, y.shape[0], ROW_TILE, [y, z, dyn], [ng], [(di, F32), (di, BF16)], [(1, di)])


def _loss_head(name, x, target, g):
    d = x.shape[1]

    def body(i, x_ref, t_ref, g_ref, dx_ref, dg_ref, loss_ref):
        xhat, r = _rms(x_ref[...])
        gv = g_ref[...]
        err = xhat * gv - t_ref[...]
        per_row = jnp.sum(err * err, axis=-1, keepdims=True) * (0.5 / d)
        loss_ref[...] += jnp.broadcast_to(_colsum(per_row), loss_ref.shape)
        dy = err * (1.0 / d)
        dg_ref[...] += _colsum(dy * xhat)
        dx_ref[...] = _rms_bwd(dy * gv, xhat, r)

    return _rowcall(name, body, x.shape[0], ROW_TILE, [x, target], [g], [(d, F32)], [(1, d), (1, LANES)])


HALO = 8


def _halo_call(name, body, rows, tile, width, mains, halo_of, halo_next, smalls, row_outs, acc_outs, scratch):
    tile = min(tile, rows)
    nb = tile // HALO
    last = rows // HALO - 1
    n_in = len(mains) + 1 + len(smalls)

    def kern(*refs):
        i = pl.program_id(0)
        accs = refs[n_in + len(row_outs):n_in + len(row_outs) + len(acc_outs)]

        @pl.when(i == 0)
        def _():
            for acc in accs:
                acc[...] = jnp.zeros_like(acc)

        body(i, *refs)

    def whole(shape):
        return pl.BlockSpec(shape, lambda i, nd=len(shape): (0,) * nd)

    if halo_next:
        halo_spec = pl.BlockSpec((HALO, width), lambda i: (jnp.minimum((i + 1) * nb, last), 0))
    else:
        halo_spec = pl.BlockSpec((HALO, width), lambda i: (jnp.maximum(i * nb - 1, 0), 0))
    in_specs = [pl.BlockSpec((tile, a.shape[1]), lambda i: (i, 0)) for a in mains] + [halo_spec]
    in_specs += [whole(a.shape) for a in smalls]
    out_specs = [pl.BlockSpec((tile, w), lambda i: (i, 0)) for (w, _) in row_outs] + [whole(s) for s in acc_outs]
    out_shape = [jax.ShapeDtypeStruct((rows, w), dt) for (w, dt) in row_outs]
    out_shape += [jax.ShapeDtypeStruct(s, F32) for s in acc_outs]
    return pl.pallas_call(
        kern, name=name, grid=(rows // tile,), in_specs=in_specs, out_specs=out_specs, out_shape=out_shape,
        scratch_shapes=scratch, compiler_params=_params("arbitrary"),
    )(*mains, mains[halo_of], *smalls)


CONV_TILE = 128
CONV_ROWS = 16
CONV_STRIP = 512


def _conv_blocks(tile, c):
    strip = CONV_STRIP if c % CONV_STRIP == 0 else LANES
    rb = min(CONV_ROWS, tile)
    return [(r0, rb, slice(c0, c0 + strip)) for c0 in range(0, c, strip) for r0 in range(0, tile, rb)]


def _shifted_windows(base, rb, offsets):
    n = base.shape[0]
    out = []
    for o in offsets:
        if o % HALO == 0:
            out.append(base[o:o + rb, :])
        else:
            out.append(pltpu.roll(base, n - o, 0)[0:rb, :])
    return out


def _conv_windows(x_ref, ext, r0, rb, sl):
    base = ext[:, sl] if r0 == 0 else x_ref[r0 - HALO:r0 + rb, sl]
    return _shifted_windows(base, rb, [HALO - 3 + k for k in range(M2_K)])


def _conv_fwd(name, xin, w, b):
    rows, c = xin.shape
    tile = min(CONV_TILE, rows)
    rb0 = min(CONV_ROWS, tile)

    def body(i, x_ref, h_ref, w_ref, b_ref, o_ref, ext):
        ext[0:HALO, :] = jnp.where(i == 0, 0.0, h_ref[...])
        ext[HALO:, :] = x_ref[0:rb0, :]
        for r0, rb, sl in _conv_blocks(tile, c):
            taps = _conv_windows(x_ref, ext, r0, rb, sl)
            pre = b_ref[:, sl] + w_ref[0:1, sl] * taps[0]
            for k in range(1, M2_K):
                pre = pre + w_ref[k:k + 1, sl] * taps[k]
            o_ref[r0:r0 + rb, sl] = pre * _sigmoid(pre)

    return _halo_call(name, body, rows, tile, c, [xin], 0, False, [w, b], [(c, F32)], [],
                      [pltpu.VMEM((rb0 + HALO, c), F32)])[0]


def _conv_bwd_pre(name, xin, dout, w, b):
    rows, c = xin.shape
    tile = min(CONV_TILE, rows)
    rb0 = min(CONV_ROWS, tile)

    def body(i, x_ref, d_ref, h_ref, w_ref, b_ref, dp_ref, dw_ref, db_ref, ext):
        ext[0:HALO, :] = jnp.where(i == 0, 0.0, h_ref[...])
        ext[HALO:, :] = x_ref[0:rb0, :]
        sums = {}
        for r0, rb, sl in _conv_blocks(tile, c):
            taps = _conv_windows(x_ref, ext, r0, rb, sl)
            pre = b_ref[:, sl] + w_ref[0:1, sl] * taps[0]
            for k in range(1, M2_K):
                pre = pre + w_ref[k:k + 1, sl] * taps[k]
            s = _sigmoid(pre)
            dp = d_ref[r0:r0 + rb, sl] * (s * (1.0 + pre * (1.0 - s)))
            dp_ref[r0:r0 + rb, sl] = dp
            part = [_colsum(dp)] + [_colsum(dp * taps[k]) for k in range(M2_K)]
            key = sl.start
            sums[key] = part if key not in sums else [p + q for p, q in zip(sums[key], part)]
            if r0 + rb == tile:
                db_ref[:, sl] += sums[key][0]
                for k in range(M2_K):
                    dw_ref[k:k + 1, sl] += sums[key][1 + k]

    return _halo_call(name, body, rows, tile, c, [xin, dout], 0, False, [w, b], [(c, F32)], [(M2_K, c), (1, c)],
                      [pltpu.VMEM((rb0 + HALO, c), F32)])


def _conv_bwd_in(name, dpre, w):
    rows, c = dpre.shape
    tile = min(CONV_TILE, rows)
    n_tiles = rows // tile
    rb0 = min(CONV_ROWS, tile)

    def body(i, d_ref, h_ref, w_ref, o_ref, ext):
        ext[0:rb0, :] = d_ref[tile - rb0:tile, :]
        ext[rb0:, :] = jnp.where(i == n_tiles - 1, 0.0, h_ref[...])
        for r0, rb, sl in _conv_blocks(tile, c):
            base = ext[:, sl] if r0 + rb == tile else d_ref[r0:r0 + rb + HALO, sl]
            wins = _shifted_windows(base, rb, [3 - k for k in range(M2_K)])
            acc = w_ref[0:1, sl] * wins[0]
            for k in range(1, M2_K):
                acc = acc + w_ref[k:k + 1, sl] * wins[k]
            o_ref[r0:r0 + rb, sl] = acc.astype(BF16)

    return _halo_call(name, body, rows, tile, c, [dpre], 0, True, [w], [(c, BF16)], [],
                      [pltpu.VMEM((rb0 + HALO, c), F32)])[0]


def _s5_build(lam_re, lam_im, log_dt, b_re, b_im, c_re, c_im):
    g, p = lam_re.shape
    h = b_re.shape[-1]
    t = S5_T
    dt = jnp.exp(log_dt)[:, None]
    lam = lax.complex(lam_re, lam_im)
    lam_dt = lam * dt
    lam_bar = jnp.exp(lam_dt)
    b_bar = ((lam_bar - 1) / lam)[..., None] * lax.complex(b_re, b_im)
    c_mat = lax.complex(c_re, c_im)
    tau = jnp.arange(t + 1, dtype=F32)
    pw = jnp.exp(lam_dt[:, :, None] * tau[None, None, :])
    c_t = jnp.transpose(c_mat, (0, 2, 1))
    cp = pw[:, :, :, None] * c_t[:, :, None, :]
    cp0 = cp[:, :, :t].reshape(g, p, t * h)
    cp1 = cp[:, :, 1:].reshape(g, p, t * h)
    bb_t = jnp.transpose(b_bar, (0, 2, 1))
    kc = (jnp.einsum("ghp,gpn->ghn", jnp.real(bb_t), jnp.real(cp0), precision=HIGHEST)
          - jnp.einsum("ghp,gpn->ghn", jnp.imag(bb_t), jnp.imag(cp0), precision=HIGHEST))
    bpow = jnp.transpose(pw[:, :, t - 1::-1][:, :, :t], (0, 2, 1))
    be = bb_t[:, :, None, :] * bpow[:, None, :, :]
    bend = jnp.concatenate([jnp.real(be), jnp.imag(be)], axis=-1).reshape(g, h * t, 2 * p)
    cpow = jnp.concatenate([jnp.real(cp1), -jnp.imag(cp1)], axis=1)
    at = pw[:, :, t]
    a1 = jnp.concatenate([jnp.real(at), jnp.real(at)], axis=-1)[:, None, :]
    a2 = jnp.concatenate([-jnp.imag(at), jnp.imag(at)], axis=-1)[:, None, :]
    return kc, bend, cpow, a1, a2


def _swap_halves(x, axis):
    n = x.shape[axis] // 2
    lo = lax.slice_in_dim(x, 0, n, axis=axis)
    hi = lax.slice_in_dim(x, n, 2 * n, axis=axis)
    return jnp.concatenate([hi, lo], axis=axis)


def _group_spec(shape):
    return pl.BlockSpec((1,) + tuple(shape[1:]), lambda g: (g, 0, 0))


S5_ROWS = 8


def _s5_expand_toeplitz(kc_ref, ext, toep):
    t, th = S5_T, S5_T * S5_H
    ext[:, th:] = jnp.zeros((S5_ROWS, LANES), F32)
    for hin in range(S5_H):
        ext[:, :th] = jnp.broadcast_to(kc_ref[0, hin:hin + 1, :], (S5_ROWS, th))
        rolled = pltpu.roll(ext[...], 0, 1, stride=S5_H, stride_axis=0)
        tiles = []
        for q in range(t // S5_ROWS):
            if q == 0:
                tiles.append(rolled[:, :th])
            else:
                tiles.append(jnp.concatenate([jnp.zeros((S5_ROWS, q * LANES), F32), rolled[:, :th - q * LANES]],
                                             axis=1))
        toep[hin * t:(hin + 1) * t, :] = jnp.concatenate(tiles, axis=0).astype(BF16)


def _s5_core_fwd(name, u, ops):
    kc, bend, cpow, a1, a2 = ops
    g, nc, th = u.shape
    p2 = bend.shape[-1]
    bend_b, cpow_b = bend.astype(BF16), cpow.astype(BF16)
    bend_s = _swap_halves(bend_b, 2)
    a2s = _swap_halves(a2, 2)

    def kern(u_ref, k_ref, b_ref, bs_ref, c_ref, a1_ref, a2_ref, a2s_ref, y_ref, sp_ref, x_scr, xs_scr, ext, toep):
        _s5_expand_toeplitz(k_ref, ext, toep)
        ub = u_ref[0].astype(BF16)
        x_scr[...] = _dot(ub, b_ref[0])
        xs_scr[...] = _dot(ub, bs_ref[0])
        a1v, a2v, a2sv = a1_ref[0], a2_ref[0], a2s_ref[0]

        def step(c, carry):
            s, ss = carry
            sp_ref[0, pl.ds(c, 1), :] = s
            s_new = a1v * s + a2v * ss + x_scr[pl.ds(c, 1), :]
            ss_new = a1v * ss + a2sv * s + xs_scr[pl.ds(c, 1), :]
            return s_new, ss_new

        zero = jnp.zeros((1, p2), F32)
        lax.fori_loop(0, nc, step, (zero, zero))
        y_ref[0] = (_dot(ub, toep[...]) + _dot(sp_ref[0].astype(BF16), c_ref[0])).astype(BF16)

    ins = [u, kc, bend_b, bend_s, cpow_b, a1, a2, a2s]
    return pl.pallas_call(
        kern, name=name, grid=(g,), in_specs=[_group_spec(a.shape) for a in ins],
        out_specs=[_group_spec((g, nc, th)), _group_spec((g, nc, p2))],
        out_shape=[jax.ShapeDtypeStruct((g, nc, th), BF16), jax.ShapeDtypeStruct((g, nc, p2), F32)],
        scratch_shapes=[pltpu.VMEM((nc, p2), F32), pltpu.VMEM((nc, p2), F32),
                        pltpu.VMEM((S5_ROWS, th + LANES), F32), pltpu.VMEM((th, th), BF16)],
        compiler_params=_params("arbitrary"),
    )(*ins)


def _s5_core_bwd(name, u, dy, sprev, ops):
    kc, bend, cpow, a1, a2 = ops
    g, nc, th = u.shape
    t = S5_T
    p2 = bend.shape[-1]
    bend_b, cpow_b = bend.astype(BF16), cpow.astype(BF16)
    cpow_s = _swap_halves(cpow_b, 1)
    a2s = _swap_halves(a2, 2)
    idx = jnp.arange(th)
    flip = (idx[:, None] // t == idx[None, :] // t) & (idx[:, None] % t == t - 1 - idx[None, :] % t)
    flip = flip.astype(BF16)

    def kern(u_ref, dy_ref, sp_ref, k_ref, b_ref, c_ref, cs_ref, a1_ref, a2_ref, a2s_ref, f_ref,
             du_ref, dk_ref, db_ref, dc_ref, da1_ref, da2_ref, g_scr, gs_scr, dx_scr, ext, toep, dtoep):
        _s5_expand_toeplitz(k_ref, ext, toep)
        ub, dyb = u_ref[0].astype(BF16), dy_ref[0].astype(BF16)
        dtoep[...] = _dot(_dot(ub, f_ref[...]).astype(BF16), dyb, "tn")
        n_q = t // S5_ROWS
        width = th + LANES
        for hin in range(S5_H):
            folded = dtoep[hin * t + (n_q - 1) * S5_ROWS:(hin + 1) * t, :]
            for qp in range(n_q - 1):
                q = n_q - 1 - qp
                tile = dtoep[hin * t + qp * S5_ROWS:hin * t + (qp + 1) * S5_ROWS, :]
                folded = folded + jnp.concatenate([tile[:, q * LANES:], jnp.zeros((S5_ROWS, q * LANES), F32)],
                                                  axis=1)
            ext[:, :th] = folded
            rolled = pltpu.roll(ext[...], 0, 1, stride=S5_H, stride_axis=0)
            rolled = pltpu.roll(rolled, width - S5_H * (S5_ROWS - 1), 1)
            dk_ref[0, hin:hin + 1, :] = _colsum(rolled)[:, :th]
        spv = sp_ref[0]
        dc_ref[0] = _dot(spv.astype(BF16), dyb, "tn")
        g_scr[...] = _dot(dyb, c_ref[0], "nt")
        gs_scr[...] = _dot(dyb, cs_ref[0], "nt")
        a1v, a2v, a2sv = a1_ref[0], a2_ref[0], a2s_ref[0]

        def step(k, carry):
            gr, grs, da1, da2 = carry
            c = nc - 1 - k
            dx_scr[pl.ds(c, 1), :] = gr
            s_in = sp_ref[0, pl.ds(c, 1), :]
            da1 = da1 + gr * s_in
            da2 = da2 + grs * s_in
            gr_new = g_scr[pl.ds(c, 1), :] + a1v * gr + a2sv * grs
            grs_new = gs_scr[pl.ds(c, 1), :] + a1v * grs + a2v * gr
            return gr_new, grs_new, da1, da2

        zero = jnp.zeros((1, p2), F32)
        _, _, da1, da2 = lax.fori_loop(0, nc, step, (zero, zero, zero, zero))
        da1_ref[0] = da1
        da2_ref[0] = da2
        dxb = dx_scr[...].astype(BF16)
        db_ref[0] = _dot(ub, dxb, "tn")
        du_ref[0] = (_dot(dyb, toep[...], "nt") + _dot(dxb, b_ref[0], "nt")).astype(BF16)

    ins = [u, dy, sprev, kc, bend_b, cpow_b, cpow_s, a1, a2, a2s]
    outs = [(g, nc, th), (g, S5_H, th), (g, th, p2), (g, p2, th), (g, 1, p2), (g, 1, p2)]
    out_types = [BF16] + [F32] * (len(outs) - 1)
    return pl.pallas_call(
        kern, name=name, grid=(g,),
        in_specs=[_group_spec(a.shape) for a in ins] + [pl.BlockSpec((th, th), lambda gi: (0, 0))],
        out_specs=[_group_spec(s) for s in outs],
        out_shape=[jax.ShapeDtypeStruct(s, dt) for s, dt in zip(outs, out_types)],
        scratch_shapes=[pltpu.VMEM((nc, p2), F32), pltpu.VMEM((nc, p2), F32), pltpu.VMEM((nc, p2), F32),
                        pltpu.VMEM((S5_ROWS, th + LANES), F32), pltpu.VMEM((th, th), BF16),
                        pltpu.VMEM((th, th), F32)],
        compiler_params=_params("arbitrary"),
    )(*ins, flip)


def _s5_to_groups(u, channel_major):
    rows, w = u.shape
    g = w // S5_H
    nc = rows // S5_T
    perm = (2, 0, 3, 1) if channel_major else (2, 0, 1, 3)
    return u.reshape(nc, S5_T, g, S5_H).transpose(perm).reshape(g, nc, S5_T * S5_H)


def _s5_from_groups(y, channel_major):
    g, nc, _ = y.shape
    if channel_major:
        return y.reshape(g, nc, S5_H, S5_T).transpose(1, 3, 0, 2).reshape(nc * S5_T, g * S5_H)
    return y.reshape(g, nc, S5_T, S5_H).transpose(1, 2, 0, 3).reshape(nc * S5_T, g * S5_H)


def _softplus(x):
    return jnp.maximum(x, 0.0) + jnp.log(1.0 + jnp.exp(-jnp.abs(x)))


def _ssd_chunk_prep(dtraw_ref, dtb_ref, a_ref, cst, dtt, lastt, n_heads):
    q = M2_Q
    lane = lax.broadcasted_iota(jnp.int32, (q, LANES), 1)
    dt = jnp.where(lane < n_heads, _softplus(dtraw_ref[...] + dtb_ref[...]), 0.0)
    adt = dt * a_ref[...]
    row = lax.broadcasted_iota(jnp.int32, (q, q), 0)
    col = lax.broadcasted_iota(jnp.int32, (q, q), 1)
    cs = _dot(jnp.where(row >= col, 1.0, 0.0), adt, precision=HIGHEST)
    cst[...] = cs.T
    dtt[...] = dt.T
    lastt[...] = jnp.broadcast_to(_colsum(adt), (q, LANES)).T
    return dt


def _pair_tables(cst, dtt, lastt, p):
    q = M2_Q
    out = []
    for hh in (2 * p, 2 * p + 1):
        rc = jnp.broadcast_to(cst[hh:hh + 1, :], (q, q))
        cc = rc.T
        dtc = jnp.broadcast_to(dtt[hh:hh + 1, :], (q, q)).T
        lb = jnp.broadcast_to(lastt[hh:hh + 1, :], (q, q))
        out.append((rc, cc, dtc, lb))
    return out


def _ssd_pair_fwd(x, bm, cm, cb, hs, tabs):
    q = M2_Q
    row = lax.broadcasted_iota(jnp.int32, (q, q), 0)
    col = lax.broadcasted_iota(jnp.int32, (q, q), 1)
    causal = row >= col
    lo = col < M2_P
    slo = row < M2_P
    (rc0, cc0, dtc0, lb0), (rc1, cc1, dtc1, lb1) = tabs
    l0 = jnp.where(causal, jnp.exp(jnp.where(causal, cc0 - rc0, 0.0)), 0.0)
    l1 = jnp.where(causal, jnp.exp(jnp.where(causal, cc1 - rc1, 0.0)), 0.0)
    m0, m1 = cb * l0, cb * l1
    dtp = jnp.where(lo, dtc0, dtc1)
    xdt = x * dtp
    xdt0 = jnp.where(lo, xdt, 0.0)
    xdt1 = jnp.where(lo, 0.0, xdt)
    e = jnp.where(lo, jnp.exp(cc0), jnp.exp(cc1))
    z = _bdot(cm, hs, "nt")
    yoff = z * e
    dec = jnp.where(lo, jnp.exp(lb0 - cc0), jnp.exp(lb1 - cc1))
    xdd = xdt * dec
    cd = jnp.where(slo, jnp.exp(lb0), jnp.exp(lb1))
    return dict(l0=l0, l1=l1, m0=m0, m1=m1, dtp=dtp, xdt=xdt, xdt0=xdt0, xdt1=xdt1, e=e, yoff=yoff,
                dec=dec, xdd=xdd, cd=cd, lo=lo, slo=slo)


def _ssd_fwd(name, xbc, dtraw, dtb, arow, dvec, n_heads):
    rows, c = xbc.shape
    q, n = M2_Q, M2_N
    di = n_heads * M2_P
    n_pairs = n_heads // 2
    ppg = n_pairs // M2_G
    nc = rows // q

    def kern(xbc_ref, dtraw_ref, dtb_ref, a_ref, d_ref, y_ref, prev_ref, state, cst, dtt, lastt):
        @pl.when(pl.program_id(0) == 0)
        def _():
            state[...] = jnp.zeros_like(state)

        _ssd_chunk_prep(dtraw_ref, dtb_ref, a_ref, cst, dtt, lastt, n_heads)
        for p in range(n_pairs):
            gi = p // ppg
            sl = slice(p * LANES, (p + 1) * LANES)
            x = xbc_ref[:, sl]
            bm = xbc_ref[:, di + gi * n:di + (gi + 1) * n]
            cm = xbc_ref[:, di + (M2_G + gi) * n:di + (M2_G + gi + 1) * n]
            if p % ppg == 0:
                cb = _bdot(cm, bm, "nt")
            hs = state[p]
            f = _ssd_pair_fwd(x, bm, cm, cb, hs, _pair_tables(cst, dtt, lastt, p))
            ydiag = _bdot(f["m0"], f["xdt0"]) + _bdot(f["m1"], f["xdt1"])
            y_ref[:, sl] = ydiag + f["yoff"] + d_ref[:, sl] * x
            prev_ref[0, p] = hs
            state[p] = f["cd"] * hs + _bdot(f["xdd"], bm, "tn")

    def whole(a):
        return pl.BlockSpec(a.shape, lambda i: (0, 0))

    return pl.pallas_call(
        kern, name=name, grid=(nc,),
        in_specs=[pl.BlockSpec((q, c), lambda i: (i, 0)), pl.BlockSpec((q, LANES), lambda i: (i, 0)),
                  whole(dtb), whole(arow), whole(dvec)],
        out_specs=[pl.BlockSpec((q, di), lambda i: (i, 0)),
                   pl.BlockSpec((1, n_pairs, 2 * M2_P, n), lambda i: (i, 0, 0, 0))],
        out_shape=[jax.ShapeDtypeStruct((rows, di), F32),
                   jax.ShapeDtypeStruct((nc, n_pairs, 2 * M2_P, n), F32)],
        scratch_shapes=[pltpu.VMEM((n_pairs, 2 * M2_P, n), F32), pltpu.VMEM((LANES, q), F32),
                        pltpu.VMEM((LANES, q), F32), pltpu.VMEM((LANES, q), F32)],
        compiler_params=_params("arbitrary"),
    )(xbc, dtraw, dtb, arow, dvec)


def _ssd_bwd(name, xbc, dtraw, dy, prev, dtb, arow, dvec, seg, n_heads):
    rows, c = xbc.shape
    q, n = M2_Q, M2_N
    di = n_heads * M2_P
    n_pairs = n_heads // 2
    ppg = n_pairs // M2_G
    nc = rows // q

    def kern(xbc_ref, dtraw_ref, dy_ref, prev_ref, dtb_ref, a_ref, d_ref, seg_ref,
             dxbc_ref, ddt_ref, da_ref, ddtb_ref, dd_ref,
             dstate, cst, dtt, lastt, dcst, wx, colterm, ddfull):
        step = pl.program_id(0)

        @pl.when(step == 0)
        def _():
            dstate[...] = jnp.zeros_like(dstate)
            ddfull[...] = jnp.zeros_like(ddfull)
            da_ref[...] = jnp.zeros_like(da_ref)
            ddtb_ref[...] = jnp.zeros_like(ddtb_ref)
            dd_ref[...] = jnp.zeros_like(dd_ref)

        dt = _ssd_chunk_prep(dtraw_ref, dtb_ref, a_ref, cst, dtt, lastt, n_heads)
        dcst[...] = jnp.zeros_like(dcst)
        lane_q = lax.broadcasted_iota(jnp.int32, (1, q), 1)
        last_hot = jnp.where(lane_q == q - 1, 1.0, 0.0)

        def total(v):
            return jnp.sum(jnp.sum(v, axis=1, keepdims=True), axis=0, keepdims=True)

        for gi in range(M2_G):
            bm = xbc_ref[:, di + gi * n:di + (gi + 1) * n]
            cm = xbc_ref[:, di + (M2_G + gi) * n:di + (M2_G + gi + 1) * n]
            cb = _bdot(cm, bm, "nt")
            dcb = jnp.zeros((q, q), F32)
            dbm = jnp.zeros((q, n), F32)
            dcm = jnp.zeros((q, n), F32)
            for p in range(gi * ppg, (gi + 1) * ppg):
                sl = slice(p * LANES, (p + 1) * LANES)
                x = xbc_ref[:, sl]
                dyp = dy_ref[:, sl]
                hs = prev_ref[0, p]
                ds = dstate[p]
                f = _ssd_pair_fwd(x, bm, cm, cb, hs, _pair_tables(cst, dtt, lastt, p))
                lo, slo = f["lo"], f["slo"]
                ddfull[:, sl] += _colsum(dyp * x)
                dy0 = jnp.where(lo, dyp, 0.0)
                dy1 = jnp.where(lo, 0.0, dyp)
                dm0 = _bdot(dyp, f["xdt0"], "nt")
                dm1 = _bdot(dyp, f["xdt1"], "nt")
                dxdt = _bdot(f["m0"], dy0, "tn") + _bdot(f["m1"], dy1, "tn")
                dcb = dcb + dm0 * f["l0"] + dm1 * f["l1"]
                w0, w1 = dm0 * f["m0"], dm1 * f["m1"]
                dz = dyp * f["e"]
                dcm = dcm + _bdot(dz, hs)
                dhs = _bdot(dz, cm, "tn") + f["cd"] * ds
                tot = ds * hs * f["cd"]
                dxdd = _bdot(bm, ds, "nt")
                dbm = dbm + _bdot(f["xdd"], ds)
                ee = dxdd * f["xdd"]
                colterm[:, sl] = dyp * f["yoff"] - ee
                dxdt = dxdt + dxdd * f["dec"]
                t_all = total(tot)
                t_lo = total(jnp.where(slo, tot, 0.0))
                e_all = total(ee)
                e_lo = total(jnp.where(lo, ee, 0.0))
                dlast0 = t_lo + e_lo
                dlast1 = (t_all - t_lo) + (e_all - e_lo)
                dcst[2 * p:2 * p + 1, :] = _colsum(w0.T - w0) + dlast0 * last_hot
                dcst[2 * p + 1:2 * p + 2, :] = _colsum(w1.T - w1) + dlast1 * last_hot
                dxbc_ref[:, sl] = d_ref[:, sl] * dyp + dxdt * f["dtp"]
                wx[:, sl] = dxdt * x
                dstate[p] = dhs
            dcm = dcm + _bdot(dcb, bm)
            dbm = dbm + _bdot(dcb, cm, "tn")
            dxbc_ref[:, di + gi * n:di + (gi + 1) * n] = dbm
            dxbc_ref[:, di + (M2_G + gi) * n:di + (M2_G + gi + 1) * n] = dcm

        segv = seg_ref[...]
        dcs = _dot(colterm[...], segv, precision=HIGHEST) + dcst[...].T
        row = lax.broadcasted_iota(jnp.int32, (q, q), 0)
        col = lax.broadcasted_iota(jnp.int32, (q, q), 1)
        ddelta = _dot(jnp.where(col >= row, 1.0, 0.0), dcs, precision=HIGHEST)
        ddt = _dot(wx[...], segv, precision=HIGHEST) + ddelta * a_ref[...]
        da_ref[...] += _colsum(ddelta * dt)
        lane = lax.broadcasted_iota(jnp.int32, (q, LANES), 1)
        ddtraw = jnp.where(lane < n_heads, ddt * _sigmoid(dtraw_ref[...] + dtb_ref[...]), 0.0)
        ddt_ref[...] = ddtraw
        ddtb_ref[...] += _colsum(ddtraw)

        @pl.when(step == nc - 1)
        def _():
            dd_ref[...] = _dot(jnp.broadcast_to(ddfull[...], (8, di)), segv, precision=HIGHEST)

    def whole(a):
        return pl.BlockSpec(a.shape, lambda i: (0, 0))

    def rev(i):
        return nc - 1 - i

    acc = jax.ShapeDtypeStruct((1, LANES), F32)
    acc_spec = pl.BlockSpec((1, LANES), lambda i: (0, 0))
    acc8 = jax.ShapeDtypeStruct((8, LANES), F32)
    acc8_spec = pl.BlockSpec((8, LANES), lambda i: (0, 0))
    return pl.pallas_call(
        kern, name=name, grid=(nc,),
        in_specs=[pl.BlockSpec((q, c), lambda i: (rev(i), 0)), pl.BlockSpec((q, LANES), lambda i: (rev(i), 0)),
                  pl.BlockSpec((q, di), lambda i: (rev(i), 0)),
                  pl.BlockSpec((1, n_pairs, 2 * M2_P, n), lambda i: (rev(i), 0, 0, 0)),
                  whole(dtb), whole(arow), whole(dvec), whole(seg)],
        out_specs=[pl.BlockSpec((q, c), lambda i: (rev(i), 0)), pl.BlockSpec((q, LANES), lambda i: (rev(i), 0)),
                   acc_spec, acc_spec, acc8_spec],
        out_shape=[jax.ShapeDtypeStruct((rows, c), F32), jax.ShapeDtypeStruct((rows, LANES), F32), acc, acc, acc8],
        scratch_shapes=[pltpu.VMEM((n_pairs, 2 * M2_P, n), F32), pltpu.VMEM((LANES, q), F32),
                        pltpu.VMEM((LANES, q), F32), pltpu.VMEM((LANES, q), F32), pltpu.VMEM((LANES, q), F32),
                        pltpu.VMEM((q, di), F32), pltpu.VMEM((q, di), F32), pltpu.VMEM((1, di), F32)],
        compiler_params=_params("arbitrary"),
    )(xbc, dtraw, dy, prev, dtb, arow, dvec, seg)


S5_PARAM_NAMES = ("s5_lambda_re", "s5_lambda_im", "s5_log_dt", "s5_b_re", "s5_b_im", "s5_c_re", "s5_c_im")


def _row(v):
    return v.reshape(1, -1)


class _Rides:
    def __init__(self):
        self.pending = {}
        self.landed = {}

    def matmul(self, site, name, *args, **kw):
        ride = self.pending.pop(site, None)
        res = _matmul(name, *args, ride=ride, **kw)
        if ride is None:
            return res
        self.landed[site] = list(res[-len(ride):])
        res = list(res[:-len(ride)])
        return res[0] if len(res) == 1 else res


def _s5_layer_fwd(tag, x, gate, h, w, j, rides):
    u = rides.matmul("s5_win", tag + "_win", h, w["s5_w_in"][j])
    params = [w[k][j] for k in S5_PARAM_NAMES]
    ops, build_vjp = jax.vjp(_s5_build, *params)
    ug = _s5_to_groups(u.astype(BF16), True)
    yg, sprev = _s5_core_fwd(tag + "_core", ug, ops)
    yy = _s5_from_groups(yg, False)
    skip = _row(w["s5_d"][j])
    gl = _gelu_fwd(tag + "_gelu", yy, u, skip)
    ab = rides.matmul("s5_wglu", tag + "_wglu", gl, w["s5_w_glu"][j])
    x1 = _glu_fwd(tag + "_glu", ab, _row(w["s5_b_glu"][j]), x, gate)
    return x1, dict(u=u, ug=ug, ops=ops, build_vjp=build_vjp, sprev=sprev, yy=yy, gl=gl, ab=ab, skip=skip)


def _s5_layer_bwd(tag, dx1, gate, h, sv, w, j, rides):
    dab, db_glu, dgate = _glu_bwd(tag + "_glu_b", sv["ab"], _row(w["s5_b_glu"][j]), dx1, gate)
    dw_glu = rides.matmul("s5_dwglu", tag + "_dwglu", sv["gl"], dab, "tn", out_dtype=BF16)
    dgl = rides.matmul("s5_dgl", tag + "_dgl", dab, w["s5_w_glu"][j], "nt")
    dyy, dskip = _gelu_bwd(tag + "_gelu_b", sv["yy"], sv["u"], sv["skip"], dgl)
    dug, dkc, dbend, dcpow, da1, da2s = _s5_core_bwd(
        tag + "_core_b", sv["ug"], _s5_to_groups(dyy.astype(BF16), False), sv["sprev"], sv["ops"])
    dparams = sv["build_vjp"]((dkc, dbend, dcpow, da1, _swap_halves(da2s, 2)))
    du = _axpy(tag + "_du", _s5_from_groups(dug, True), dyy, sv["skip"])
    grads = dict(zip(S5_PARAM_NAMES, dparams))
    grads["s5_d"] = dskip.reshape(-1)
    grads["s5_w_in"] = _matmul(tag + "_dwin", h, du, "tn", out_dtype=BF16)
    grads["s5_w_glu"] = dw_glu
    grads["s5_b_glu"] = db_glu.reshape(-1)
    dh = _matmul(tag + "_dh", du, w["s5_w_in"][j], "nt")
    return dh, grads, dgate


def _ssd_consts(w, j, d_model):
    di = 2 * d_model
    heads = di // M2_P

    def pad_row(v):
        return jnp.zeros((1, LANES), F32).at[0, :heads].set(v)

    a = -jnp.exp(w["m2_a_log"][j])
    seg = (jnp.arange(di)[:, None] // M2_P == jnp.arange(LANES)[None, :]).astype(F32)
    w_in = w["m2_w_in"][j]
    conv_dim = di + 2 * M2_G * M2_N
    w_dt = jnp.zeros((d_model, LANES), w_in.dtype).at[:, :heads].set(w_in[:, di + conv_dim:])
    return dict(di=di, heads=heads, conv_dim=conv_dim, a=a, arow=pad_row(a), dtb=pad_row(w["m2_dt_bias"][j]),
                dvec=_row(jnp.repeat(w["m2_d"][j], M2_P)), seg=seg,
                w_z=w_in[:, :di], w_xbc=w_in[:, di:di + conv_dim], w_dt=w_dt,
                conv_w=w["m2_conv_w"][j], conv_b=_row(w["m2_conv_b"][j]), norm_g=_row(w["m2_norm_g"][j]))


def _gated_out_bwd(tag, act, dxo, w_out, gate, **kw):
    dw, dgate_parts = _matmul(tag + "_dwo", act, dxo, "tn", out_dtype=BF16, colscale=gate, colsum_with=w_out, **kw)
    dgate = jnp.sum(dgate_parts, axis=0)
    return dw, dgate, _scale_cols(tag + "_wog", w_out, gate)


def _ssd_layer_fwd(tag, x, gate, h, w, j):
    k = _ssd_consts(w, j, h.shape[1])
    z = _matmul(tag + "_wz", h, k["w_z"])
    xbc_pre = _matmul(tag + "_wxbc", h, k["w_xbc"])
    dtraw = _matmul(tag + "_wdt", h, k["w_dt"])
    xbc = _conv_fwd(tag + "_conv", xbc_pre, k["conv_w"], k["conv_b"])
    y, prev = _ssd_fwd(tag + "_core", xbc, dtraw, k["dtb"], k["arow"], k["dvec"], k["heads"])
    yn = _gatenorm_fwd(tag + "_gn", y, z, k["norm_g"])
    x1 = _matmul(tag + "_wout", yn, w["m2_w_out"][j], colscale=gate, addin=x)
    return x1, dict(k=k, z=z, xbc_pre=xbc_pre, dtraw=dtraw, xbc=xbc, y=y, prev=prev, yn=yn)


def _ssd_layer_bwd(tag, dx1, gate, h, sv, w, j):
    k = sv["k"]
    heads = k["heads"]
    dw_out, dgate, wog = _gated_out_bwd(tag, sv["yn"], dx1, w["m2_w_out"][j], gate)
    grads = {"m2_w_out": dw_out}
    dyn = _matmul(tag + "_dyn", dx1, wog, "nt")
    dyssd, dz, dng = _gatenorm_bwd(tag + "_gn_b", sv["y"], sv["z"], k["norm_g"], dyn)
    dxbc, ddtraw, da, ddtb, dd = _ssd_bwd(tag + "_core_b", sv["xbc"], sv["dtraw"], dyssd, sv["prev"],
                                          k["dtb"], k["arow"], k["dvec"], k["seg"], heads)
    dpre, dcw, dcb = _conv_bwd_pre(tag + "_conv_b1", sv["xbc_pre"], dxbc, k["conv_w"], k["conv_b"])
    dxbc_pre = _conv_bwd_in(tag + "_conv_b2", dpre, k["conv_w"])
    dw_z = _matmul(tag + "_dwz", h, dz, "tn", out_dtype=BF16)
    dw_xbc = _matmul(tag + "_dwxbc", h, dxbc_pre, "tn", out_dtype=BF16)
    dw_dt = _matmul(tag + "_dwdt", h, ddtraw, "tn", out_dtype=BF16)
    dh = _matmul(tag + "_dh1", dz, k["w_z"], "nt")
    dh = _matmul(tag + "_dh2", dxbc_pre, k["w_xbc"], "nt", addin=dh)
    dh = _matmul(tag + "_dh3", ddtraw, k["w_dt"], "nt", addin=dh)
    grads["m2_w_in"] = jnp.concatenate([dw_z, dw_xbc, dw_dt[:, :heads]], axis=1)
    grads["m2_conv_w"] = dcw
    grads["m2_conv_b"] = dcb.reshape(-1)
    grads["m2_dt_bias"] = ddtb[0, :heads]
    grads["m2_a_log"] = da[0, :heads] * k["a"]
    grads["m2_d"] = dd[0, :heads]
    grads["m2_norm_g"] = dng.reshape(-1)
    return dh, grads, dgate


def _layer_fwd(li, x, mod, w, rides, late_weights=None):
    tag = "L%d" % li
    sh1, sc1, g1, sh2, sc2, g2 = mod
    j = li // 2
    h = _normmod_fwd(tag + "_nm1", x, _row(w["norm_mix_g"][li]), sh1, sc1)
    if li % 2 == 0:
        x1, mix = _s5_layer_fwd(tag + "_s5", x, g1, h, w, j, rides)
    else:
        x1, mix = _ssd_layer_fwd(tag + "_m2", x, g1, h, w, j)
    if late_weights is not None:
        w = {**w, **late_weights(rides.landed)}
    h2 = _normmod_fwd(tag + "_nm2", x1, _row(w["norm_mlp_g"][li]), sh2, sc2)
    r = rides.matmul("w1", tag + "_w1", h2, w["mlp_w1"][li], relu=True, out_dtype=BF16)
    x2 = rides.matmul("w2", tag + "_w2", r, w["mlp_w2"][li], square_a=True, colscale=g2, addin=x1)
    return x2, dict(x=x, h=h, mix=mix, x1=x1, h2=h2, r=r), w


def _layer_bwd(li, dx2, sv, mod, w, rides, ride_own_mlp):
    tag = "L%d" % li
    sh1, sc1, g1, sh2, sc2, g2 = mod
    j = li // 2
    dw2, dg2, w2g = _gated_out_bwd(tag + "_mlp", sv["r"], dx2, w["mlp_w2"][li], g2, square_a=True)
    dr = rides.matmul("dr", tag + "_dr", dx2, w2g, "nt", out_dtype=BF16, mul2=sv["r"])
    dw1 = _matmul(tag + "_dw1", sv["h2"], dr, "tn", out_dtype=BF16)
    grads = {"mlp_w2": dw2, "mlp_w1": dw1}
    dh2 = rides.matmul("dh2", tag + "_dh2", dr, w["mlp_w1"][li], "nt")
    if ride_own_mlp:
        mlp_bufs = _grad_buffers(li, grads, parts=[1])[0]
        rides.pending["s5_dgl"], rides.pending["s5_dwglu"] = mlp_bufs[:1], mlp_bufs[1:]
    dx1, dgm, dsh2, dsc2 = _normmod_bwd(tag + "_nm2_b", sv["x1"], _row(w["norm_mlp_g"][li]), sh2, sc2, dh2, dx2)
    if li % 2 == 0:
        dh, mix_grads, dg1 = _s5_layer_bwd(tag + "_s5", dx1, g1, sv["h"], sv["mix"], w, j, rides)
    else:
        dh, mix_grads, dg1 = _ssd_layer_bwd(tag + "_m2", dx1, g1, sv["h"], sv["mix"], w, j)
    dx, dgx, dsh1, dsc1 = _normmod_bwd(tag + "_nm1_b", sv["x"], _row(w["norm_mix_g"][li]), sh1, sc1, dh, dx1)
    grads["norm_mix_g"] = dgx.reshape(-1)
    grads["norm_mlp_g"] = dgm.reshape(-1)
    dmod = jnp.concatenate([dsh1, dsc1, dg1, dsh2, dsc2, dg2], axis=1)
    return dx, {**grads, **mix_grads}, dmod


def _layer_parts(li):
    j = li // 2
    mix = [("s5_w_in", j, 0), ("s5_w_glu", j, 1)] if li % 2 == 0 else [("m2_w_in", j, 1), ("m2_w_out", j, 0)]
    return [mix, [("mlp_w1", li, 1), ("mlp_w2", li, 0)]]


def _grad_specs(li):
    j = li // 2
    mlp = [("one", ("mlp_w2", li, 0)), ("one", ("mlp_w1", li, 1))]
    if li % 2 == 0:
        return [[("one", ("s5_w_in", j, 0)), ("one", ("s5_w_glu", j, 1))], mlp]
    packed = [("m2_w_in", j, 1), ("m2_conv_w", j, 1), ("m2_conv_b", j, 0), ("m2_norm_g", j, 0)]
    return [[("one", ("m2_w_out", j, 0)), ("packed", packed)], mlp]


def _grad_buffers(li, grads, parts=(0, 1)):
    out = []
    for k in parts:
        bufs = []
        for kind, entry in _grad_specs(li)[k]:
            if kind == "one":
                bufs.append((grads[entry[0]], "rows" if entry[2] == 0 else "cols"))
            else:
                pieces = [_pack([_chip_slice(grads[n], c, ax) for n, _, ax in entry], BF16) for c in range(N_CHIP)]
                bufs.append((jnp.stack(pieces), "packed"))
        out.append(bufs)
    return out


def _gather_buffers(local, part):
    return [(local[n][i].astype(BF16), "same") for n, i, _ in part]


def _assemble(landed, part):
    out = {}
    for buf, (n, i, ax) in zip(landed, part):
        if ax == 0:
            out[n] = {i: buf.reshape(-1, buf.shape[2])}
        else:
            out[n] = {i: jnp.concatenate([buf[k] for k in range(N_CHIP)], axis=1)}
    return out


def _local_step(x, target, mods, w, local):
    depth = w["norm_mix_g"].shape[0]
    d = x.shape[1]
    rides = _Rides()
    saved, mod_rows, layer_w = [], [], []
    mix0, mlp0 = _layer_parts(0)
    wl = {**w, **_assemble(_exchange4("ag_w_L0", _gather_buffers(local, mix0)), mix0)}
    rides.pending["s5_win"] = _gather_buffers(local, mlp0[:1])
    rides.pending["s5_wglu"] = _gather_buffers(local, mlp0[1:])

    def late_mlp0(landed):
        return _assemble(landed.pop("s5_win") + landed.pop("s5_wglu"), mlp0)

    for li in range(depth):
        if li + 1 < depth:
            nxt = _layer_parts(li + 1)
            rides.pending["w1"] = _gather_buffers(local, nxt[0])
            rides.pending["w2"] = _gather_buffers(local, nxt[1])
        mod = [mods[li:li + 1, i * d:(i + 1) * d] for i in range(N_MOD)]
        mod_rows.append(mod)
        x, sv, wl = _layer_fwd(li, x, mod, wl, rides, late_mlp0 if li == 0 else None)
        saved.append(sv)
        layer_w.append(wl)
        if li + 1 < depth:
            wl = {**w, **_assemble(rides.landed.pop("w1"), nxt[0]), **_assemble(rides.landed.pop("w2"), nxt[1])}
    dx, dgf, loss = _loss_head("loss_head", x, target, _row(w["final_norm_g"]))
    layer_grads = [None] * depth
    dmods = [None] * depth
    landed = {}
    for li in reversed(range(depth)):
        dx, layer_grads[li], dmods[li] = _layer_bwd(li, dx, saved[li], mod_rows[li], layer_w[li], rides, li == 0)
        if li + 1 < depth:
            landed[(li + 1, 0)] = rides.landed.pop("dr")
            landed[(li + 1, 1)] = rides.landed.pop("dh2")
        if li > 0:
            rides.pending["dr"], rides.pending["dh2"] = _grad_buffers(li, layer_grads[li])
    landed[(0, 1)] = rides.landed.pop("s5_dgl") + rides.landed.pop("s5_dwglu")
    landed[(0, 0)] = _exchange4("rs_g_L0", _grad_buffers(0, layer_grads[0], parts=[0])[0])
    return loss, dx, layer_grads, dgf.reshape(-1), jnp.concatenate(dmods, axis=0), landed


ANY = pl.BlockSpec(memory_space=pl.ANY)
N_DEV = 8
N_CHIP = 4


def _coords():
    return lax.axis_index("x"), lax.axis_index("y"), lax.axis_index("c")


def _allgather8(name, block):
    r, wd = block.shape

    def body(x_ref, out_ref, send_sems, recv_sems, local_sem):
        x, y, c = _coords()
        me, sibling = (x, y, c), (x, y, 1 - c)
        chips = [(1 - x, y), (x, 1 - y), (1 - x, 1 - y)]

        def slot(px, py, pc):
            return out_ref.at[4 * px + 2 * py + pc]

        def copy(k, blk, to, src=None):
            return pltpu.make_async_remote_copy(
                src_ref=slot(*blk) if src is None else src, dst_ref=slot(*blk),
                send_sem=send_sems.at[k], recv_sem=recv_sems.at[k], device_id=to, device_id_type=MESH)

        mine = pltpu.make_async_copy(x_ref, slot(*me), local_sem)
        mine.start()
        first = [copy(0, me, sibling, src=x_ref)]
        first += [copy(1 + j, me, (*chip, c), src=x_ref) for j, chip in enumerate(chips)]
        for cp in first:
            cp.start()
        passed = [copy(4 + j, (*chip, c), sibling) for j, chip in enumerate(chips)]
        for j, chip in enumerate(chips):
            copy(1 + j, (*chip, c), me).wait_recv()
            passed[j].start()
        copy(0, sibling, me).wait_recv()
        for j, chip in enumerate(chips):
            copy(4 + j, (*chip, 1 - c), me).wait_recv()
        for cp in first + passed:
            cp.wait_send()
        mine.wait()

    return pl.pallas_call(
        body, name=name, in_specs=[ANY], out_specs=ANY,
        out_shape=jax.ShapeDtypeStruct((N_DEV, r, wd), block.dtype),
        scratch_shapes=[pltpu.SemaphoreType.DMA((7,)), pltpu.SemaphoreType.DMA((7,)), pltpu.SemaphoreType.DMA],
    )(block)


def _landing_shape(src, kind):
    if kind == "same":
        return (N_CHIP,) + src.shape
    if kind == "packed":
        return src.shape
    rows, cols = src.shape
    return (N_CHIP, rows // N_CHIP, cols) if kind == "rows" else (N_CHIP, rows, cols // N_CHIP)


def _exchange4_ops(srcs, dsts, send_sems, recv_sems, local_sems, kinds):
    x, y, c = _coords()
    my_chip = 2 * x + y
    chips = [(1 - x, y), (x, 1 - y), (1 - x, 1 - y)]

    def piece(q, k):
        ref, kind = srcs[q], kinds[q]
        if kind == "same":
            return ref
        if kind == "packed":
            return ref.at[k]
        _, rows, cols = dsts[q].shape
        return ref.at[pl.ds(k * rows, rows), :] if kind == "rows" else ref.at[:, pl.ds(k * cols, cols)]

    def copy(q, j, k, slot):
        px, py = chips[j]
        return pltpu.make_async_remote_copy(
            src_ref=piece(q, k), dst_ref=dsts[q].at[slot], send_sem=send_sems.at[3 * q + j],
            recv_sem=recv_sems.at[3 * q + j], device_id=(px, py, c), device_id_type=MESH)

    def mine(q):
        return pltpu.make_async_copy(piece(q, my_chip), dsts[q].at[my_chip], local_sems.at[q])

    def start():
        for q in range(len(srcs)):
            mine(q).start()
            for j, (px, py) in enumerate(chips):
                copy(q, j, 2 * px + py, my_chip).start()

    def wait():
        for q in range(len(srcs)):
            for j, (px, py) in enumerate(chips):
                copy(q, j, my_chip, 2 * px + py).wait_recv()
        for q in range(len(srcs)):
            for j, (px, py) in enumerate(chips):
                copy(q, j, 2 * px + py, my_chip).wait_send()
            mine(q).wait()

    return start, wait


def _exchange_scratch(n):
    return [pltpu.SemaphoreType.DMA((3 * n,)), pltpu.SemaphoreType.DMA((3 * n,)), pltpu.SemaphoreType.DMA((n,))]


def _exchange4(name, buffers):
    n = len(buffers)
    kinds = [kind for _, kind in buffers]

    def body(*refs):
        start, wait = _exchange4_ops(refs[:n], refs[n:2 * n], *refs[2 * n:], kinds)
        start()
        wait()

    return pl.pallas_call(
        body, name=name, in_specs=[ANY] * n, out_specs=[ANY] * n,
        out_shape=[jax.ShapeDtypeStruct(_landing_shape(s, k), s.dtype) for s, k in buffers],
        scratch_shapes=_exchange_scratch(n),
    )(*[s for s, _ in buffers])


def _swap_sibling(name, block):
    def body(x_ref, out_ref, send_sem, recv_sem):
        x, y, c = _coords()
        cp = pltpu.make_async_remote_copy(src_ref=x_ref, dst_ref=out_ref, send_sem=send_sem, recv_sem=recv_sem,
                                          device_id=(x, y, 1 - c), device_id_type=MESH)
        cp.start()
        cp.wait()

    return pl.pallas_call(
        body, name=name, in_specs=[ANY], out_specs=ANY, out_shape=jax.ShapeDtypeStruct(block.shape, block.dtype),
        scratch_shapes=[pltpu.SemaphoreType.DMA, pltpu.SemaphoreType.DMA],
    )(block)


def _sum_slots(name, stacked):
    n, r, wd = stacked.shape
    tile = min(FLAT_ROWS, r)

    def kern(x_ref, o_ref):
        acc = x_ref[0].astype(F32)
        for s in range(1, n):
            acc = acc + x_ref[s].astype(F32)
        o_ref[...] = acc

    return pl.pallas_call(
        kern, name=name, grid=(r // tile,), in_specs=[pl.BlockSpec((n, tile, wd), lambda i: (0, i, 0))],
        out_specs=pl.BlockSpec((tile, wd), lambda i: (i, 0)), out_shape=jax.ShapeDtypeStruct((r, wd), F32),
        compiler_params=_params("parallel"),
    )(stacked)


def _adamw(name, w, m, v, g, g2=None):
    r, wd = w.shape
    grads = [g] if g2 is None else [g, g2]
    c1 = 1.0 - ADAM_B1 ** ADAM_STEP
    c2 = 1.0 - ADAM_B2 ** ADAM_STEP

    def body(i, *refs):
        w_ref, m_ref, v_ref = refs[:3]
        g_refs = refs[3:3 + len(grads)]
        go_ref, d_ref, mo_ref, vo_ref = refs[3 + len(grads):]
        gv = g_refs[0][...]
        if g2 is not None:
            gv = gv + g_refs[1][...]
        mn = ADAM_B1 * m_ref[...] + (1.0 - ADAM_B1) * gv
        vn = ADAM_B2 * v_ref[...] + (1.0 - ADAM_B2) * (gv * gv)
        go_ref[...] = gv
        mo_ref[...] = mn
        vo_ref[...] = vn
        d_ref[...] = -ADAM_LR * ((mn / c1) / (jnp.sqrt(vn / c2) + ADAM_EPS) + ADAM_WD * w_ref[...])

    return _rowcall(name, body, r, FLAT_ROWS, [w, m, v] + grads, [], [(wd, F32)] * 4, [])


FLAT_BLOCK = FLAT_ROWS * FLAT_W


def _pack(arrays, dtype):
    flat = jnp.concatenate([a.reshape(-1).astype(dtype) for a in arrays])
    pad = (-flat.shape[0]) % FLAT_BLOCK
    return jnp.pad(flat, (0, pad)).reshape(-1, FLAT_W)


def _unpack(buf, shapes):
    flat = buf.reshape(-1)
    out, off = [], 0
    for s in shapes:
        n = math.prod(s)
        out.append(flat[off:off + n].reshape(s))
        off += n
    return out


SHARDED_BIG = {"mlp_w1": 2, "mlp_w2": 1, "s5_w_in": 1, "s5_w_glu": 2, "m2_w_in": 2, "m2_w_out": 1}
SHARDED_SMALL = {"m2_conv_w": 2, "m2_conv_b": 1, "m2_norm_g": 1}
REPLICATED = ("ada_b", "norm_mix_g", "norm_mlp_g", "s5_lambda_re", "s5_lambda_im", "s5_log_dt", "s5_b_re",
              "s5_b_im", "s5_c_re", "s5_c_im", "s5_d", "s5_b_glu", "m2_dt_bias", "m2_a_log", "m2_d", "final_norm_g")
WEIGHT_NAMES = ("ada_w", "ada_b", "norm_mix_g", "norm_mlp_g", "mlp_w1", "mlp_w2", "s5_w_in", "s5_lambda_re",
                "s5_lambda_im", "s5_log_dt", "s5_b_re", "s5_b_im", "s5_c_re", "s5_c_im", "s5_d", "s5_w_glu",
                "s5_b_glu", "m2_w_in", "m2_conv_w", "m2_conv_b", "m2_dt_bias", "m2_a_log", "m2_d", "m2_norm_g",
                "m2_w_out", "final_norm_g")


def _gather_weights(name, local, names_axes, dtype):
    names = list(names_axes)
    got = _exchange4(name, [(_pack([local[k] for k in names], dtype), "same")])[0]
    per_chip = [_unpack(got[j], [local[k].shape for k in names]) for j in range(N_CHIP)]
    return {k: jnp.concatenate([per_chip[j][i] for j in range(N_CHIP)], axis=names_axes[k])
            for i, k in enumerate(names)}


def _chip_slice(a, chip, axis):
    size = a.shape[axis] // N_CHIP
    return lax.slice_in_dim(a, chip * size, (chip + 1) * size, axis=axis)


def kernel(x, c, ada_w, ada_b, norm_mix_g, norm_mlp_g, mlp_w1, mlp_w2, s5_w_in, s5_lambda_re, s5_lambda_im, s5_log_dt, s5_b_re, s5_b_im, s5_c_re, s5_c_im, s5_d, s5_w_glu, s5_b_glu, m2_w_in, m2_conv_w, m2_conv_b, m2_dt_bias, m2_a_log, m2_d, m2_norm_g, m2_w_out, final_norm_g, loss_target, m_ada_w, m_ada_b, m_norm_mix_g, m_norm_mlp_g, m_mlp_w1, m_mlp_w2, m_s5_w_in, m_s5_lambda_re, m_s5_lambda_im, m_s5_log_dt, m_s5_b_re, m_s5_b_im, m_s5_c_re, m_s5_c_im, m_s5_d, m_s5_w_glu, m_s5_b_glu, m_m2_w_in, m_m2_conv_w, m_m2_conv_b, m_m2_dt_bias, m_m2_a_log, m_m2_d, m_m2_norm_g, m_m2_w_out, m_final_norm_g, v_ada_w, v_ada_b, v_norm_mix_g, v_norm_mlp_g, v_mlp_w1, v_mlp_w2, v_s5_w_in, v_s5_lambda_re, v_s5_lambda_im, v_s5_log_dt, v_s5_b_re, v_s5_b_im, v_s5_c_re, v_s5_c_im, v_s5_d, v_s5_w_glu, v_s5_b_glu, v_m2_w_in, v_m2_conv_w, v_m2_conv_b, v_m2_dt_bias, v_m2_a_log, v_m2_d, v_m2_norm_g, v_m2_w_out, v_final_norm_g):
    args = locals()
    local = {k: args[k] for k in WEIGHT_NAMES}
    mom_m = {k: args["m_" + k] for k in WEIGHT_NAMES}
    mom_v = {k: args["v_" + k] for k in WEIGHT_NAMES}
    depth, d = norm_mix_g.shape
    xi, yi, ci = _coords()
    my_chip = 2 * xi + yi
    my_dev = 2 * my_chip + ci

    cond = jax.nn.silu(c).reshape(-1, LANES)
    cond_all = _allgather8("ag_cond", cond).reshape(N_DEV, d)
    cond_pad = jnp.zeros((LANES, d), F32).at[:N_DEV].set(cond_all)
    mod_cols = ada_w.shape[2]
    mod_part = jnp.stack([_matmul("ada_%d" % i, cond_pad, ada_w[i])[:N_DEV] for i in range(depth)])
    mod_all = _allgather8("ag_mod", mod_part.reshape(-1, LANES)).reshape(N_CHIP, 2, depth, N_DEV, mod_cols)[:, 0]
    mod_mine = lax.dynamic_index_in_dim(mod_all, my_dev, axis=2, keepdims=False)
    mods = jnp.transpose(mod_mine, (1, 0, 2)).reshape(depth, N_CHIP * mod_cols) + ada_b

    w = {k: local[k] for k in REPLICATED}
    w.update(_gather_weights("ag_w_small", local, SHARDED_SMALL, F32))

    loss_row, dx, layer_grads, g_final, dmods, landed = _local_step(x[0], loss_target[0], mods, w, local)
    grads = {"ada_b": dmods, "final_norm_g": g_final}
    for k in REPLICATED[1:-1]:
        grads[k] = jnp.stack([g[k] for g in layer_grads if k in g])

    rep_shapes = [grads[k].shape for k in REPLICATED]
    rep_all = _allgather8("ag_grep", _pack([grads[k] for k in REPLICATED], F32))
    rep_sum = _sum_slots("sum_grep", rep_all)
    dmods_all = rep_all.reshape(N_DEV, -1)[:, :dmods.size].reshape(N_DEV, depth, N_CHIP * mod_cols)

    dm_mine = lax.dynamic_slice_in_dim(dmods_all, my_chip * mod_cols, mod_cols, axis=2)
    dm_pad = jnp.zeros((LANES, depth, mod_cols), F32).at[:N_DEV].set(dm_mine)
    g_ada_w = jnp.stack([_matmul("dada_%d" % i, cond_pad, dm_pad[:, i], "tn") for i in range(depth)])

    red = {}
    for li in range(depth):
        for k, part in enumerate(_grad_specs(li)):
            for (kind, entry), land in zip(part, landed[(li, k)]):
                if kind == "one":
                    red[entry[:2]] = _sum_slots("sum_%s_%d" % entry[:2], land)
                else:
                    total = _sum_slots("sum_packed_L%d" % li, land)
                    shapes = [local[n][i].shape for n, i, _ in entry]
                    red.update({(n, i): v for (n, i, _), v in zip(entry, _unpack(total, shapes))})

    out_g, out_d, out_m, out_v = {}, {}, {}, {}
    for name in list(SHARDED_BIG) + list(SHARDED_SMALL):
        shape = local[name].shape
        flat2 = (-1, shape[-1])
        g1 = jnp.stack([red[(name, i)] for i in range(shape[0])]).reshape(flat2)
        g2 = _swap_sibling("swap_" + name, g1)
        res = _adamw("adam_" + name, local[name].reshape(flat2), mom_m[name].reshape(flat2),
                     mom_v[name].reshape(flat2), g1, g2)
        for dst, buf in zip((out_g, out_d, out_m, out_v), res):
            dst[name] = buf.reshape(shape)
    res = _adamw("adam_rep", _pack([local[k] for k in REPLICATED], F32), _pack([mom_m[k] for k in REPLICATED], F32),
                 _pack([mom_v[k] for k in REPLICATED], F32), rep_sum)
    for dst, buf in zip((out_g, out_d, out_m, out_v), res):
        dst.update(zip(REPLICATED, _unpack(buf, rep_shapes)))
    flat2 = (-1, mod_cols)
    res = _adamw("adam_ada", ada_w.reshape(flat2), m_ada_w.reshape(flat2), v_ada_w.reshape(flat2),
                 g_ada_w.reshape(flat2))
    for dst, buf in zip((out_g, out_d, out_m, out_v), res):
        dst["ada_w"] = buf.reshape(ada_w.shape)

    loss = lax.psum(loss_row[0, 0], ("x", "y", "c"))
    outs = [loss, dx[None]]
    for dst in (out_g, out_d, out_m, out_v):
        outs += [dst[k] for k in WEIGHT_NAMES]
    return tuple(outs)
```

```python
import functools
import math

import jax
import jax.numpy as jnp
from jax import lax
from jax.experimental import pallas as pl
from jax.experimental.pallas import tpu as pltpu

F32 = jnp.float32
BF16 = jnp.bfloat16
HIGHEST = lax.Precision.HIGHEST

NORM_EPS = 1e-5
N_MOD = 6
S5_H, S5_P, S5_T = 16, 64, 64
M2_P, M2_N, M2_G, M2_Q, M2_K = 64, 128, 4, 128, 4
LANES = 128
ADAM_LR, ADAM_B1, ADAM_B2, ADAM_EPS, ADAM_WD, ADAM_STEP = 0.001, 0.9, 0.999, 1e-08, 0.01, 10
VMEM_LIMIT_BYTES = 56 * 1024 * 1024
ROW_TILE = 256
FLAT_W = 1024
FLAT_ROWS = 256
MESH = pl.DeviceIdType.MESH


def _params(*sem):
    return pltpu.CompilerParams(dimension_semantics=sem, vmem_limit_bytes=VMEM_LIMIT_BYTES)


def _dot(a, b, dn="nn", precision=None):
    dims = {"nn": ((1,), (0,)), "nt": ((1,), (1,)), "tn": ((0,), (0,))}[dn]
    return lax.dot_general(a, b, (dims, ((), ())), preferred_element_type=F32, precision=precision)


def _bdot(a, b, dn="nn"):
    return _dot(a.astype(BF16), b.astype(BF16), dn)


def _sigmoid(x):
    return jax.nn.sigmoid(x)


def _colsum(x):
    return jnp.sum(x, axis=0, keepdims=True)


def _pick_tile(dim, want):
    if dim <= want:
        return dim
    for t in range(want - want % LANES, 0, -LANES):
        if dim % t == 0:
            return t
    raise ValueError((dim, want))


MATMUL_TILE = 1024
MATMUL_VMEM_BUDGET = 44 * 1024 * 1024


def _matmul_tiles(m, n, k, mode, in_bytes, out_bytes):
    tn = _pick_tile(n, MATMUL_TILE)
    k_tiles = [k] + [t for t in (4096, 2048, 1024) if t < k and k % t == 0]
    m_tiles = [_pick_tile(m, MATMUL_TILE)] + ([512] if mode != "tn" and m % 512 == 0 and m > 512 else [])
    for tk in k_tiles:
        for tm in m_tiles:
            blocks = 2 * (tm * tk * in_bytes[0] + tk * tn * in_bytes[1] + tm * tn * out_bytes)
            if blocks + (4 * tm * tn if tk < k else 0) <= MATMUL_VMEM_BUDGET:
                return tm, tn, tk
    raise ValueError((m, n, k))


def _matmul(name, a, b, mode="nn", out_dtype=F32, relu=False, square_a=False, mul2=None, colscale=None,
            addin=None, colsum_with=None, ride=None):
    if mode == "nn":
        (m, k), (k2, n) = a.shape, b.shape
    elif mode == "nt":
        (m, k), (n, k2) = a.shape, b.shape
    else:
        (k, m), (k2, n) = a.shape, b.shape
    assert k == k2, (name, a.shape, b.shape)
    tiles = [e for e in (mul2, addin, colsum_with) if e is not None]
    out_bytes = jnp.dtype(out_dtype).itemsize + sum(e.dtype.itemsize for e in tiles)
    tm, tn, tk = _matmul_tiles(m, n, k, mode, (a.dtype.itemsize, b.dtype.itemsize), out_bytes)
    nk = k // tk
    n_ext = len(tiles) + (colscale is not None)
    n_out = 1 + (colsum_with is not None)
    n_ride = 0 if ride is None else len(ride)
    grid = (m // tm, n // tn, nk)

    def kern(*refs):
        a_ref, b_ref = refs[:2]
        e_refs = list(refs[2:2 + n_ext])
        o_refs = refs[2 + n_ext + n_ride:2 + n_ext + n_ride + n_out]
        kk = pl.program_id(2)
        if ride is not None:
            here = [pl.program_id(ax) for ax in range(3)]
            land0 = 2 + n_ext + n_ride + n_out
            ride_refs = (refs[2 + n_ext:2 + n_ext + n_ride], refs[land0:land0 + n_ride]) + tuple(refs[-3:])
            ride_kinds = [kind for _, kind in ride]
            first = (here[0] == 0) & (here[1] == 0) & (here[2] == 0)
            last = (here[0] == grid[0] - 1) & (here[1] == grid[1] - 1) & (here[2] == grid[2] - 1)

            @pl.when(first)
            def _():
                _exchange4_ops(*ride_refs, ride_kinds)[0]()

        av = a_ref[...]
        if square_a:
            av = av * av
        part = _bdot(av, b_ref[...], mode)

        def finish(r):
            ext = list(e_refs)
            m2v = ext.pop(0)[...].astype(F32) if mul2 is not None else None
            addv = ext.pop(0)[...].astype(F32) if addin is not None else None
            if colsum_with is not None:
                o_refs[1][0] = _colsum(r * ext.pop(0)[...].astype(F32))
            if relu:
                r = jnp.maximum(r, 0.0)
            if m2v is not None:
                r = r * (2.0 * m2v)
            if colscale is not None:
                r = r * ext.pop(0)[...]
            if addv is not None:
                r = r + addv
            o_refs[0][...] = r.astype(out_dtype)

        if nk == 1:
            finish(part)
        else:
            acc = refs[-4] if ride is not None else refs[-1]

            @pl.when(kk == 0)
            def _():
                acc[...] = part

            @pl.when(kk > 0)
            def _():
                acc[...] += part

            @pl.when(kk == nk - 1)
            def _():
                finish(acc[...])

        if ride is not None:
            @pl.when(last)
            def _():
                _exchange4_ops(*ride_refs, ride_kinds)[1]()

    if mode == "tn":
        a_spec = pl.BlockSpec((tk, tm), lambda i, j, kk: (kk, i))
    else:
        a_spec = pl.BlockSpec((tm, tk), lambda i, j, kk: (i, kk))
    if mode == "nt":
        b_spec = pl.BlockSpec((tn, tk), lambda i, j, kk: (j, kk))
    else:
        b_spec = pl.BlockSpec((tk, tn), lambda i, j, kk: (kk, j))
    o_spec = pl.BlockSpec((tm, tn), lambda i, j, kk: (i, j))
    in_specs = [a_spec, b_spec] + [o_spec] * len(tiles)
    operands = [a, b] + tiles
    if colscale is not None:
        in_specs.append(pl.BlockSpec((1, tn), lambda i, j, kk: (0, j)))
        operands.append(colscale)
    out_specs = [o_spec]
    out_shape = [jax.ShapeDtypeStruct((m, n), out_dtype)]
    if colsum_with is not None:
        out_specs.append(pl.BlockSpec((1, 1, tn), lambda i, j, kk: (i, 0, j)))
        out_shape.append(jax.ShapeDtypeStruct((m // tm, 1, n), F32))
    scratch = [pltpu.VMEM((tm, tn), F32)] if nk > 1 else []
    semantics = ("parallel", "parallel", "arbitrary")
    if ride is not None:
        for src, kind in ride:
            in_specs.append(pl.BlockSpec(memory_space=pl.ANY))
            operands.append(src)
            out_specs.append(pl.BlockSpec(memory_space=pl.ANY))
            out_shape.append(jax.ShapeDtypeStruct(_landing_shape(src, kind), src.dtype))
        scratch += _exchange_scratch(n_ride)
        semantics = ("arbitrary", "arbitrary", "arbitrary")
    res = pl.pallas_call(
        kern, name=name, grid=grid, in_specs=in_specs, out_specs=out_specs, out_shape=out_shape,
        scratch_shapes=scratch, compiler_params=_params(*semantics),
    )(*operands)
    return res if len(res) > 1 else res[0]


def _rowcall(name, body, rows, tile, row_ins, small_ins, row_outs, acc_outs):
    tile = min(tile, rows)
    assert rows % tile == 0, (name, rows, tile)
    n_in = len(row_ins) + len(small_ins)

    def kern(*refs):
        i = pl.program_id(0)
        accs = refs[n_in + len(row_outs):]

        @pl.when(i == 0)
        def _():
            for acc in accs:
                acc[...] = jnp.zeros_like(acc)

        body(i, *refs)

    def whole(shape):
        return pl.BlockSpec(shape, lambda i, nd=len(shape): (0,) * nd)

    in_specs = [pl.BlockSpec((tile, a.shape[1]), lambda i: (i, 0)) for a in row_ins]
    in_specs += [whole(a.shape) for a in small_ins]
    out_specs = [pl.BlockSpec((tile, w), lambda i: (i, 0)) for (w, _) in row_outs]
    out_specs += [whole(s) for s in acc_outs]
    out_shape = [jax.ShapeDtypeStruct((rows, w), dt) for (w, dt) in row_outs]
    out_shape += [jax.ShapeDtypeStruct(s, F32) for s in acc_outs]
    return pl.pallas_call(
        kern, name=name, grid=(rows // tile,), in_specs=in_specs, out_specs=out_specs, out_shape=out_shape,
        compiler_params=_params("arbitrary"),
    )(*row_ins, *small_ins)


def _rms(x):
    r = lax.rsqrt(jnp.mean(x * x, axis=-1, keepdims=True) + NORM_EPS)
    return x * r, r


def _rms_bwd(dxhat, xhat, r):
    return r * (dxhat - xhat * jnp.mean(dxhat * xhat, axis=-1, keepdims=True))


def _normmod_fwd(name, x, g, sh, sc):
    def body(i, x_ref, g_ref, sh_ref, sc_ref, o_ref):
        xhat, _ = _rms(x_ref[...])
        o_ref[...] = ((xhat * g_ref[...]) * (1.0 + sc_ref[...]) + sh_ref[...]).astype(BF16)

    return _rowcall(name, body, x.shape[0], ROW_TILE, [x], [g, sh, sc], [(x.shape[1], BF16)], [])[0]


def _normmod_bwd(name, x, g, sh, sc, dh, dx_pass):
    d = x.shape[1]

    def body(i, x_ref, dh_ref, dxp_ref, g_ref, sh_ref, sc_ref, dx_ref, dg_ref, dsh_ref, dsc_ref):
        xhat, r = _rms(x_ref[...])
        dh = dh_ref[...].astype(F32)
        gv = g_ref[...]
        dn = dh * (1.0 + sc_ref[...])
        dsc_ref[...] += _colsum(dh * (xhat * gv))
        dsh_ref[...] += _colsum(dh)
        dg_ref[...] += _colsum(dn * xhat)
        dx_ref[...] = dxp_ref[...] + _rms_bwd(dn * gv, xhat, r)

    return _rowcall(name, body, x.shape[0], ROW_TILE, [x, dh, dx_pass], [g, sh, sc], [(d, F32)],
                    [(1, d), (1, d), (1, d)])


def _scale_cols(name, w, g):
    def body(i, w_ref, g_ref, o_ref):
        o_ref[...] = (w_ref[...].astype(F32) * g_ref[...]).astype(BF16)

    return _rowcall(name, body, w.shape[0], ROW_TILE, [w], [g], [(w.shape[1], BF16)], [])[0]


GELU_K = math.sqrt(2.0 / math.pi)
GELU_C = 0.044715


def _gelu_fwd(name, y, u, skip):
    def body(i, y_ref, u_ref, s_ref, o_ref):
        v = y_ref[...].astype(F32) + s_ref[...] * u_ref[...]
        t = jnp.tanh(GELU_K * (v + GELU_C * (v * v * v)))
        o_ref[...] = (0.5 * v * (1.0 + t)).astype(BF16)

    return _rowcall(name, body, y.shape[0], ROW_TILE, [y, u], [skip], [(y.shape[1], BF16)], [])[0]


def _gelu_bwd(name, y, u, skip, dgl):
    d = y.shape[1]

    def body(i, y_ref, u_ref, d_ref, s_ref, o_ref, ds_ref):
        uv = u_ref[...]
        v = y_ref[...].astype(F32) + s_ref[...] * uv
        t = jnp.tanh(GELU_K * (v + GELU_C * (v * v * v)))
        dv = d_ref[...] * (0.5 * (1.0 + t) + 0.5 * v * (1.0 - t * t) * (GELU_K * (1.0 + 3.0 * GELU_C * v * v)))
        o_ref[...] = dv
        ds_ref[...] += _colsum(dv * uv)

    return _rowcall(name, body, y.shape[0], ROW_TILE, [y, u, dgl], [skip], [(d, F32)], [(1, d)])


def _axpy(name, a, b, scale):
    def body(i, a_ref, b_ref, s_ref, o_ref):
        o_ref[...] = (a_ref[...].astype(F32) + s_ref[...] * b_ref[...]).astype(BF16)

    return _rowcall(name, body, a.shape[0], ROW_TILE, [a, b], [scale], [(a.shape[1], BF16)], [])[0]


def _glu_fwd(name, ab, bias, x, gate):
    d = ab.shape[1] // 2

    def body(i, ab_ref, x_ref, b_ref, g_ref, o_ref):
        v = ab_ref[:, :d] + b_ref[:, :d]
        gt = ab_ref[:, d:] + b_ref[:, d:]
        o_ref[...] = x_ref[...] + g_ref[...] * (v * _sigmoid(gt))

    return _rowcall(name, body, ab.shape[0], ROW_TILE, [ab, x], [bias, gate], [(d, F32)], [])[0]


def _glu_bwd(name, ab, bias, dxo, gate):
    d = ab.shape[1] // 2

    def body(i, ab_ref, dx_ref, b_ref, g_ref, dab_ref, db_ref, dg_ref):
        v = ab_ref[:, :d] + b_ref[:, :d]
        s = _sigmoid(ab_ref[:, d:] + b_ref[:, d:])
        dxo_v = dx_ref[...]
        dg_ref[...] += _colsum(dxo_v * (v * s))
        do = g_ref[...] * dxo_v
        dv = do * s
        dgt = do * v * (s * (1.0 - s))
        dab_ref[:, :d] = dv.astype(BF16)
        dab_ref[:, d:] = dgt.astype(BF16)
        db_ref[:, :d] += _colsum(dv)
        db_ref[:, d:] += _colsum(dgt)

    return _rowcall(name, body, ab.shape[0], ROW_TILE, [ab, dxo], [bias, gate], [(2 * d, BF16)],
                    [(1, 2 * d), (1, d)])


def _gatenorm_fwd(name, y, z, ng):
    di = y.shape[1]
    gw = di // M2_G

    def body(i, y_ref, z_ref, g_ref, o_ref):
        for gi in range(M2_G):
            sl = slice(gi * gw, (gi + 1) * gw)
            zz = z_ref[:, sl]
            y2 = y_ref[:, sl] * (zz * _sigmoid(zz))
            yh, _ = _rms(y2)
            o_ref[:, sl] = (yh * g_ref[:, sl]).astype(BF16)

    return _rowcall(name, body, y.shape[0], ROW_TILE, [y, z], [ng], [(di, BF16)], [])[0]


def _gatenorm_bwd(name, y, z, ng, dyn):
    di = y.shape[1]
    gw = di // M2_G

    def body(i, y_ref, z_ref, d_ref, g_ref, dy_ref, dz_ref, dg_ref):
        for gi in range(M2_G):
            sl = slice(gi * gw, (gi + 1) * gw)
            zz = z_ref[:, sl]
            yy = y_ref[:, sl]
            s = _sigmoid(zz)
            sz = zz * s
            yh, r = _rms(yy * sz)
            dn = d_ref[:, sl]
            dg_ref[:, sl] += _colsum(dn * yh)
            dy2 = _rms_bwd(dn * g_ref[:, sl], yh, r)
            dy_ref[:, sl] = dy2 * sz
            dz_ref[:, sl] = (dy2 * yy * (s * (1.0 + zz * (1.0 - s)))).astype(BF16)

    return _rowcall(name, body, y.shape[0], ROW_TILE, [y, z, dyn], [ng], [(di, F32), (di, BF16)], [(1, di)])


def _loss_head(name, x, target, g):
    d = x.shape[1]

    def body(i, x_ref, t_ref, g_ref, dx_ref, dg_ref, loss_ref):
        xhat, r = _rms(x_ref[...])
        gv = g_ref[...]
        err = xhat * gv - t_ref[...]
        per_row = jnp.sum(err * err, axis=-1, keepdims=True) * (0.5 / d)
        loss_ref[...] += jnp.broadcast_to(_colsum(per_row), loss_ref.shape)
        dy = err * (1.0 / d)
        dg_ref[...] += _colsum(dy * xhat)
        dx_ref[...] = _rms_bwd(dy * gv, xhat, r)

    return _rowcall(name, body, x.shape[0], ROW_TILE, [x, target], [g], [(d, F32)], [(1, d), (1, LANES)])


HALO = 8


def _halo_call(name, body, rows, tile, width, mains, halo_of, halo_next, smalls, row_outs, acc_outs, scratch):
    tile = min(tile, rows)
    nb = tile // HALO
    last = rows // HALO - 1
    n_in = len(mains) + 1 + len(smalls)

    def kern(*refs):
        i = pl.program_id(0)
        accs = refs[n_in + len(row_outs):n_in + len(row_outs) + len(acc_outs)]

        @pl.when(i == 0)
        def _():
            for acc in accs:
                acc[...] = jnp.zeros_like(acc)

        body(i, *refs)

    def whole(shape):
        return pl.BlockSpec(shape, lambda i, nd=len(shape): (0,) * nd)

    if halo_next:
        halo_spec = pl.BlockSpec((HALO, width), lambda i: (jnp.minimum((i + 1) * nb, last), 0))
    else:
        halo_spec = pl.BlockSpec((HALO, width), lambda i: (jnp.maximum(i * nb - 1, 0), 0))
    in_specs = [pl.BlockSpec((tile, a.shape[1]), lambda i: (i, 0)) for a in mains] + [halo_spec]
    in_specs += [whole(a.shape) for a in smalls]
    out_specs = [pl.BlockSpec((tile, w), lambda i: (i, 0)) for (w, _) in row_outs] + [whole(s) for s in acc_outs]
    out_shape = [jax.ShapeDtypeStruct((rows, w), dt) for (w, dt) in row_outs]
    out_shape += [jax.ShapeDtypeStruct(s, F32) for s in acc_outs]
    return pl.pallas_call(
        kern, name=name, grid=(rows // tile,), in_specs=in_specs, out_specs=out_specs, out_shape=out_shape,
        scratch_shapes=scratch, compiler_params=_params("arbitrary"),
    )(*mains, mains[halo_of], *smalls)


CONV_TILE = 128
CONV_ROWS = 16
CONV_STRIP = 512


def _conv_blocks(tile, c, strip=CONV_STRIP):
    strip = strip if c % strip == 0 else LANES
    rb = min(CONV_ROWS, tile)
    return [(r0, rb, slice(c0, c0 + strip)) for c0 in range(0, c, strip) for r0 in range(0, tile, rb)]


def _shifted_windows(base, rb, offsets):
    n = base.shape[0]
    out = []
    for o in offsets:
        if o % HALO == 0:
            out.append(base[o:o + rb, :])
        else:
            out.append(pltpu.roll(base, n - o, 0)[0:rb, :])
    return out


def _conv_windows(x_ref, ext, r0, rb, sl):
    base = ext[:, sl] if r0 == 0 else x_ref[r0 - HALO:r0 + rb, sl]
    return _shifted_windows(base, rb, [HALO - 3 + k for k in range(M2_K)])


def _conv_fwd(name, xin, w, b):
    rows, c = xin.shape
    tile = min(CONV_TILE, rows)
    rb0 = min(CONV_ROWS, tile)

    def body(i, x_ref, h_ref, w_ref, b_ref, o_ref, ext):
        ext[0:HALO, :] = jnp.where(i == 0, 0.0, h_ref[...])
        ext[HALO:, :] = x_ref[0:rb0, :]
        for r0, rb, sl in _conv_blocks(tile, c):
            taps = _conv_windows(x_ref, ext, r0, rb, sl)
            pre = b_ref[:, sl] + w_ref[0:1, sl] * taps[0]
            for k in range(1, M2_K):
                pre = pre + w_ref[k:k + 1, sl] * taps[k]
            o_ref[r0:r0 + rb, sl] = pre * _sigmoid(pre)

    return _halo_call(name, body, rows, tile, c, [xin], 0, False, [w, b], [(c, F32)], [],
                      [pltpu.VMEM((rb0 + HALO, c), F32)])[0]


def _conv_bwd_pre(name, xin, dout, w, b):
    rows, c = xin.shape
    tile = min(CONV_TILE, rows)
    rb0 = min(CONV_ROWS, tile)

    def body(i, x_ref, d_ref, h_ref, w_ref, b_ref, dp_ref, dw_ref, db_ref, ext):
        ext[0:HALO, :] = jnp.where(i == 0, 0.0, h_ref[...])
        ext[HALO:, :] = x_ref[0:rb0, :]
        sums = {}
        for r0, rb, sl in _conv_blocks(tile, c, CONV_STRIP // 2):
            taps = _conv_windows(x_ref, ext, r0, rb, sl)
            pre = b_ref[:, sl] + w_ref[0:1, sl] * taps[0]
            for k in range(1, M2_K):
                pre = pre + w_ref[k:k + 1, sl] * taps[k]
            s = _sigmoid(pre)
            dp = d_ref[r0:r0 + rb, sl] * (s * (1.0 + pre * (1.0 - s)))
            dp_ref[r0:r0 + rb, sl] = dp
            part = [dp] + [dp * taps[k] for k in range(M2_K)]
            key = sl.start
            sums[key] = part if key not in sums else [p + q for p, q in zip(sums[key], part)]
            if r0 + rb == tile:
                db_ref[:, sl] += _colsum(sums[key][0])
                for k in range(M2_K):
                    dw_ref[k:k + 1, sl] += _colsum(sums[key][1 + k])

    return _halo_call(name, body, rows, tile, c, [xin, dout], 0, False, [w, b], [(c, F32)], [(M2_K, c), (1, c)],
                      [pltpu.VMEM((rb0 + HALO, c), F32)])


def _conv_bwd_in(name, dpre, w):
    rows, c = dpre.shape
    tile = min(CONV_TILE, rows)
    n_tiles = rows // tile
    rb0 = min(CONV_ROWS, tile)

    def body(i, d_ref, h_ref, w_ref, o_ref, ext):
        ext[0:rb0, :] = d_ref[tile - rb0:tile, :]
        ext[rb0:, :] = jnp.where(i == n_tiles - 1, 0.0, h_ref[...])
        for r0, rb, sl in _conv_blocks(tile, c):
            base = ext[:, sl] if r0 + rb == tile else d_ref[r0:r0 + rb + HALO, sl]
            wins = _shifted_windows(base, rb, [3 - k for k in range(M2_K)])
            acc = w_ref[0:1, sl] * wins[0]
            for k in range(1, M2_K):
                acc = acc + w_ref[k:k + 1, sl] * wins[k]
            o_ref[r0:r0 + rb, sl] = acc.astype(BF16)

    return _halo_call(name, body, rows, tile, c, [dpre], 0, True, [w], [(c, BF16)], [],
                      [pltpu.VMEM((rb0 + HALO, c), F32)])[0]


def _s5_build(lam_re, lam_im, log_dt, b_re, b_im, c_re, c_im):
    g, p = lam_re.shape
    h = b_re.shape[-1]
    t = S5_T
    dt = jnp.exp(log_dt)[:, None]
    lam = lax.complex(lam_re, lam_im)
    lam_dt = lam * dt
    lam_bar = jnp.exp(lam_dt)
    b_bar = ((lam_bar - 1) / lam)[..., None] * lax.complex(b_re, b_im)
    c_mat = lax.complex(c_re, c_im)
    tau = jnp.arange(t + 1, dtype=F32)
    pw = jnp.exp(lam_dt[:, :, None] * tau[None, None, :])
    c_t = jnp.transpose(c_mat, (0, 2, 1))
    cp = pw[:, :, :, None] * c_t[:, :, None, :]
    cp0 = cp[:, :, :t].reshape(g, p, t * h)
    cp1 = cp[:, :, 1:].reshape(g, p, t * h)
    bb_t = jnp.transpose(b_bar, (0, 2, 1))
    kc = (jnp.einsum("ghp,gpn->ghn", jnp.real(bb_t), jnp.real(cp0), precision=HIGHEST)
          - jnp.einsum("ghp,gpn->ghn", jnp.imag(bb_t), jnp.imag(cp0), precision=HIGHEST))
    bpow = jnp.transpose(pw[:, :, t - 1::-1][:, :, :t], (0, 2, 1))
    be = bb_t[:, :, None, :] * bpow[:, None, :, :]
    bend = jnp.concatenate([jnp.real(be), jnp.imag(be)], axis=-1).reshape(g, h * t, 2 * p)
    cpow = jnp.concatenate([jnp.real(cp1), -jnp.imag(cp1)], axis=1)
    at = pw[:, :, t]
    a1 = jnp.concatenate([jnp.real(at), jnp.real(at)], axis=-1)[:, None, :]
    a2 = jnp.concatenate([-jnp.imag(at), jnp.imag(at)], axis=-1)[:, None, :]
    return kc, bend, cpow, a1, a2


def _swap_halves(x, axis):
    n = x.shape[axis] // 2
    lo = lax.slice_in_dim(x, 0, n, axis=axis)
    hi = lax.slice_in_dim(x, n, 2 * n, axis=axis)
    return jnp.concatenate([hi, lo], axis=axis)


def _group_spec(shape):
    return pl.BlockSpec((1,) + tuple(shape[1:]), lambda g: (g, 0, 0))


S5_ROWS = 8


def _s5_expand_toeplitz(kc_ref, ext, toep):
    t, th = S5_T, S5_T * S5_H
    ext[:, th:] = jnp.zeros((S5_ROWS, LANES), F32)
    for hin in range(S5_H):
        ext[:, :th] = jnp.broadcast_to(kc_ref[0, hin:hin + 1, :], (S5_ROWS, th))
        rolled = pltpu.roll(ext[...], 0, 1, stride=S5_H, stride_axis=0)
        tiles = []
        for q in range(t // S5_ROWS):
            if q == 0:
                tiles.append(rolled[:, :th])
            else:
                tiles.append(jnp.concatenate([jnp.zeros((S5_ROWS, q * LANES), F32), rolled[:, :th - q * LANES]],
                                             axis=1))
        toep[hin * t:(hin + 1) * t, :] = jnp.concatenate(tiles, axis=0).astype(BF16)


def _s5_core_fwd(name, u, ops):
    kc, bend, cpow, a1, a2 = ops
    g, nc, th = u.shape
    p2 = bend.shape[-1]
    bend_b, cpow_b = bend.astype(BF16), cpow.astype(BF16)
    a2s = _swap_halves(a2, 2)

    def kern(u_ref, k_ref, b_ref, c_ref, a1_ref, a2_ref, a2s_ref, y_ref, sp_ref, x_scr, xs_scr, ext, toep):
        _s5_expand_toeplitz(k_ref, ext, toep)
        ub = u_ref[0].astype(BF16)
        xv = _dot(ub, b_ref[0])
        x_scr[...] = xv
        xs_scr[...] = pltpu.roll(xv, p2 // 2, 1)
        a1v, a2v, a2sv = a1_ref[0], a2_ref[0], a2s_ref[0]

        def step(c, carry):
            s, ss = carry
            sp_ref[0, pl.ds(c, 1), :] = s
            s_new = a1v * s + a2v * ss + x_scr[pl.ds(c, 1), :]
            ss_new = a1v * ss + a2sv * s + xs_scr[pl.ds(c, 1), :]
            return s_new, ss_new

        zero = jnp.zeros((1, p2), F32)
        lax.fori_loop(0, nc, step, (zero, zero))
        y_ref[0] = (_dot(ub, toep[...]) + _dot(sp_ref[0].astype(BF16), c_ref[0])).astype(BF16)

    ins = [u, kc, bend_b, cpow_b, a1, a2, a2s]
    return pl.pallas_call(
        kern, name=name, grid=(g,), in_specs=[_group_spec(a.shape) for a in ins],
        out_specs=[_group_spec((g, nc, th)), _group_spec((g, nc, p2))],
        out_shape=[jax.ShapeDtypeStruct((g, nc, th), BF16), jax.ShapeDtypeStruct((g, nc, p2), F32)],
        scratch_shapes=[pltpu.VMEM((nc, p2), F32), pltpu.VMEM((nc, p2), F32),
                        pltpu.VMEM((S5_ROWS, th + LANES), F32), pltpu.VMEM((th, th), BF16)],
        compiler_params=_params("arbitrary"),
    )(*ins)


def _s5_core_bwd(name, u, dy, sprev, ops):
    kc, bend, cpow, a1, a2 = ops
    g, nc, th = u.shape
    t = S5_T
    p2 = bend.shape[-1]
    bend_b, cpow_b = bend.astype(BF16), cpow.astype(BF16)
    a2s = _swap_halves(a2, 2)
    idx = jnp.arange(th)
    flip = (idx[:, None] // t == idx[None, :] // t) & (idx[:, None] % t == t - 1 - idx[None, :] % t)
    flip = flip.astype(BF16)

    def kern(u_ref, dy_ref, sp_ref, k_ref, b_ref, c_ref, a1_ref, a2_ref, a2s_ref, f_ref,
             du_ref, dk_ref, db_ref, dc_ref, da1_ref, da2_ref, g_scr, gs_scr, dx_scr, ext, toep, dtoep):
        _s5_expand_toeplitz(k_ref, ext, toep)
        ub, dyb = u_ref[0].astype(BF16), dy_ref[0].astype(BF16)
        dtoep[...] = _dot(_dot(ub, f_ref[...]).astype(BF16), dyb, "tn")
        n_q = t // S5_ROWS
        width = th + LANES
        for hin in range(S5_H):
            folded = dtoep[hin * t + (n_q - 1) * S5_ROWS:(hin + 1) * t, :]
            for qp in range(n_q - 1):
                q = n_q - 1 - qp
                tile = dtoep[hin * t + qp * S5_ROWS:hin * t + (qp + 1) * S5_ROWS, :]
                folded = folded + jnp.concatenate([tile[:, q * LANES:], jnp.zeros((S5_ROWS, q * LANES), F32)],
                                                  axis=1)
            ext[:, :th] = folded
            rolled = pltpu.roll(ext[...], 0, 1, stride=S5_H, stride_axis=0)
            rolled = pltpu.roll(rolled, width - S5_H * (S5_ROWS - 1), 1)
            dk_ref[0, hin:hin + 1, :] = _colsum(rolled)[:, :th]
        spv = sp_ref[0]
        dc_ref[0] = _dot(spv.astype(BF16), dyb, "tn")
        gv = _dot(dyb, c_ref[0], "nt")
        g_scr[...] = gv
        gs_scr[...] = pltpu.roll(gv, p2 // 2, 1)
        a1v, a2v, a2sv = a1_ref[0], a2_ref[0], a2s_ref[0]

        def step(k, carry):
            gr, grs, da1, da2 = carry
            c = nc - 1 - k
            dx_scr[pl.ds(c, 1), :] = gr
            s_in = sp_ref[0, pl.ds(c, 1), :]
            da1 = da1 + gr * s_in
            da2 = da2 + grs * s_in
            gr_new = g_scr[pl.ds(c, 1), :] + a1v * gr + a2sv * grs
            grs_new = gs_scr[pl.ds(c, 1), :] + a1v * grs + a2v * gr
            return gr_new, grs_new, da1, da2

        zero = jnp.zeros((1, p2), F32)
        _, _, da1, da2 = lax.fori_loop(0, nc, step, (zero, zero, zero, zero))
        da1_ref[0] = da1
        da2_ref[0] = da2
        dxb = dx_scr[...].astype(BF16)
        db_ref[0] = _dot(ub, dxb, "tn")
        du_ref[0] = (_dot(dyb, toep[...], "nt") + _dot(dxb, b_ref[0], "nt")).astype(BF16)

    ins = [u, dy, sprev, kc, bend_b, cpow_b, a1, a2, a2s]
    outs = [(g, nc, th), (g, S5_H, th), (g, th, p2), (g, p2, th), (g, 1, p2), (g, 1, p2)]
    out_types = [BF16] + [F32] * (len(outs) - 1)
    return pl.pallas_call(
        kern, name=name, grid=(g,),
        in_specs=[_group_spec(a.shape) for a in ins] + [pl.BlockSpec((th, th), lambda gi: (0, 0))],
        out_specs=[_group_spec(s) for s in outs],
        out_shape=[jax.ShapeDtypeStruct(s, dt) for s, dt in zip(outs, out_types)],
        scratch_shapes=[pltpu.VMEM((nc, p2), F32), pltpu.VMEM((nc, p2), F32), pltpu.VMEM((nc, p2), F32),
                        pltpu.VMEM((S5_ROWS, th + LANES), F32), pltpu.VMEM((th, th), BF16),
                        pltpu.VMEM((th, th), F32)],
        compiler_params=_params("arbitrary"),
    )(*ins, flip)


def _s5_to_groups(u, channel_major):
    rows, w = u.shape
    g = w // S5_H
    nc = rows // S5_T
    perm = (2, 0, 3, 1) if channel_major else (2, 0, 1, 3)
    return u.reshape(nc, S5_T, g, S5_H).transpose(perm).reshape(g, nc, S5_T * S5_H)


def _s5_from_groups(y, channel_major):
    g, nc, _ = y.shape
    if channel_major:
        return y.reshape(g, nc, S5_H, S5_T).transpose(1, 3, 0, 2).reshape(nc * S5_T, g * S5_H)
    return y.reshape(g, nc, S5_T, S5_H).transpose(1, 2, 0, 3).reshape(nc * S5_T, g * S5_H)


def _softplus(x):
    return jnp.maximum(x, 0.0) + jnp.log(1.0 + jnp.exp(-jnp.abs(x)))


def _ssd_chunk_prep(dtraw_ref, dtb_ref, a_ref, cst, dtt, lastt, n_heads):
    q = M2_Q
    lane = lax.broadcasted_iota(jnp.int32, (q, LANES), 1)
    dt = jnp.where(lane < n_heads, _softplus(dtraw_ref[...] + dtb_ref[...]), 0.0)
    adt = dt * a_ref[...]
    row = lax.broadcasted_iota(jnp.int32, (q, q), 0)
    col = lax.broadcasted_iota(jnp.int32, (q, q), 1)
    cs = _dot(jnp.where(row >= col, 1.0, 0.0), adt, precision=HIGHEST)
    cst[...] = cs.T
    dtt[...] = dt.T
    lastt[...] = jnp.broadcast_to(_colsum(adt), (q, LANES)).T
    return dt


def _pair_tables(cst, dtt, lastt, p):
    q = M2_Q
    out = []
    for hh in (2 * p, 2 * p + 1):
        rc = jnp.broadcast_to(cst[hh:hh + 1, :], (q, q))
        cc = rc.T
        dtc = jnp.broadcast_to(dtt[hh:hh + 1, :], (q, q)).T
        lb = jnp.broadcast_to(lastt[hh:hh + 1, :], (q, q))
        out.append((rc, cc, dtc, lb))
    return out


def _ssd_pair_fwd(x, bm, cm, cb, hs, tabs):
    q = M2_Q
    row = lax.broadcasted_iota(jnp.int32, (q, q), 0)
    col = lax.broadcasted_iota(jnp.int32, (q, q), 1)
    causal = row >= col
    lo = col < M2_P
    slo = row < M2_P
    (rc0, cc0, dtc0, lb0), (rc1, cc1, dtc1, lb1) = tabs
    l0 = jnp.where(causal, jnp.exp(jnp.where(causal, cc0 - rc0, 0.0)), 0.0)
    l1 = jnp.where(causal, jnp.exp(jnp.where(causal, cc1 - rc1, 0.0)), 0.0)
    m0, m1 = cb * l0, cb * l1
    dtp = jnp.where(lo, dtc0, dtc1)
    xdt = x * dtp
    xdt0 = jnp.where(lo, xdt, 0.0)
    xdt1 = jnp.where(lo, 0.0, xdt)
    e = jnp.where(lo, jnp.exp(cc0), jnp.exp(cc1))
    z = _bdot(cm, hs, "nt")
    yoff = z * e
    dec = jnp.where(lo, jnp.exp(lb0 - cc0), jnp.exp(lb1 - cc1))
    xdd = xdt * dec
    cd = jnp.where(slo, jnp.exp(lb0), jnp.exp(lb1))
    return dict(l0=l0, l1=l1, m0=m0, m1=m1, dtp=dtp, xdt=xdt, xdt0=xdt0, xdt1=xdt1, e=e, yoff=yoff,
                dec=dec, xdd=xdd, cd=cd, lo=lo, slo=slo)


def _ssd_fwd(name, xbc, dtraw, dtb, arow, dvec, n_heads):
    rows, c = xbc.shape
    q, n = M2_Q, M2_N
    di = n_heads * M2_P
    n_pairs = n_heads // 2
    ppg = n_pairs // M2_G
    nc = rows // q

    def kern(xbc_ref, dtraw_ref, dtb_ref, a_ref, d_ref, y_ref, prev_ref, state, cst, dtt, lastt):
        @pl.when(pl.program_id(0) == 0)
        def _():
            state[...] = jnp.zeros_like(state)

        _ssd_chunk_prep(dtraw_ref, dtb_ref, a_ref, cst, dtt, lastt, n_heads)
        for p in range(n_pairs):
            gi = p // ppg
            sl = slice(p * LANES, (p + 1) * LANES)
            x = xbc_ref[:, sl]
            bm = xbc_ref[:, di + gi * n:di + (gi + 1) * n]
            cm = xbc_ref[:, di + (M2_G + gi) * n:di + (M2_G + gi + 1) * n]
            if p % ppg == 0:
                cb = _bdot(cm, bm, "nt")
            hs = state[p]
            f = _ssd_pair_fwd(x, bm, cm, cb, hs, _pair_tables(cst, dtt, lastt, p))
            ydiag = _bdot(f["m0"], f["xdt0"]) + _bdot(f["m1"], f["xdt1"])
            y_ref[:, sl] = ydiag + f["yoff"] + d_ref[:, sl] * x
            prev_ref[0, p] = hs
            state[p] = f["cd"] * hs + _bdot(f["xdd"], bm, "tn")

    def whole(a):
        return pl.BlockSpec(a.shape, lambda i: (0, 0))

    return pl.pallas_call(
        kern, name=name, grid=(nc,),
        in_specs=[pl.BlockSpec((q, c), lambda i: (i, 0)), pl.BlockSpec((q, LANES), lambda i: (i, 0)),
                  whole(dtb), whole(arow), whole(dvec)],
        out_specs=[pl.BlockSpec((q, di), lambda i: (i, 0)),
                   pl.BlockSpec((1, n_pairs, 2 * M2_P, n), lambda i: (i, 0, 0, 0))],
        out_shape=[jax.ShapeDtypeStruct((rows, di), F32),
                   jax.ShapeDtypeStruct((nc, n_pairs, 2 * M2_P, n), F32)],
        scratch_shapes=[pltpu.VMEM((n_pairs, 2 * M2_P, n), F32), pltpu.VMEM((LANES, q), F32),
                        pltpu.VMEM((LANES, q), F32), pltpu.VMEM((LANES, q), F32)],
        compiler_params=_params("arbitrary"),
    )(xbc, dtraw, dtb, arow, dvec)


def _ssd_bwd(name, xbc, dtraw, dy, prev, dtb, arow, dvec, seg, n_heads):
    rows, c = xbc.shape
    q, n = M2_Q, M2_N
    di = n_heads * M2_P
    n_pairs = n_heads // 2
    ppg = n_pairs // M2_G
    nc = rows // q

    def kern(xbc_ref, dtraw_ref, dy_ref, prev_ref, dtb_ref, a_ref, d_ref, seg_ref,
             dxbc_ref, ddt_ref, da_ref, ddtb_ref, dd_ref,
             dstate, cst, dtt, lastt, dcst, wx, colterm, ddfull):
        step = pl.program_id(0)

        @pl.when(step == 0)
        def _():
            dstate[...] = jnp.zeros_like(dstate)
            ddfull[...] = jnp.zeros_like(ddfull)
            da_ref[...] = jnp.zeros_like(da_ref)
            ddtb_ref[...] = jnp.zeros_like(ddtb_ref)
            dd_ref[...] = jnp.zeros_like(dd_ref)

        dt = _ssd_chunk_prep(dtraw_ref, dtb_ref, a_ref, cst, dtt, lastt, n_heads)
        dcst[...] = jnp.zeros_like(dcst)
        lane_q = lax.broadcasted_iota(jnp.int32, (1, q), 1)
        last_hot = jnp.where(lane_q == q - 1, 1.0, 0.0)

        def total(v):
            return jnp.sum(jnp.sum(v, axis=1, keepdims=True), axis=0, keepdims=True)

        for gi in range(M2_G):
            bm = xbc_ref[:, di + gi * n:di + (gi + 1) * n]
            cm = xbc_ref[:, di + (M2_G + gi) * n:di + (M2_G + gi + 1) * n]
            cb = _bdot(cm, bm, "nt")
            dcb = jnp.zeros((q, q), F32)
            dbm = jnp.zeros((q, n), F32)
            dcm = jnp.zeros((q, n), F32)
            for p in range(gi * ppg, (gi + 1) * ppg):
                sl = slice(p * LANES, (p + 1) * LANES)
                x = xbc_ref[:, sl]
                dyp = dy_ref[:, sl]
                hs = prev_ref[0, p]
                ds = dstate[p]
                f = _ssd_pair_fwd(x, bm, cm, cb, hs, _pair_tables(cst, dtt, lastt, p))
                lo, slo = f["lo"], f["slo"]
                ddfull[:, sl] += _colsum(dyp * x)
                dy0 = jnp.where(lo, dyp, 0.0)
                dy1 = jnp.where(lo, 0.0, dyp)
                dm0 = _bdot(dyp, f["xdt0"], "nt")
                dm1 = _bdot(dyp, f["xdt1"], "nt")
                dxdt = _bdot(f["m0"], dy0, "tn") + _bdot(f["m1"], dy1, "tn")
                dcb = dcb + dm0 * f["l0"] + dm1 * f["l1"]
                w0, w1 = dm0 * f["m0"], dm1 * f["m1"]
                dz = dyp * f["e"]
                dcm = dcm + _bdot(dz, hs)
                dhs = _bdot(dz, cm, "tn") + f["cd"] * ds
                tot = ds * hs * f["cd"]
                dxdd = _bdot(bm, ds, "nt")
                dbm = dbm + _bdot(f["xdd"], ds)
                ee = dxdd * f["xdd"]
                colterm[:, sl] = dyp * f["yoff"] - ee
                dxdt = dxdt + dxdd * f["dec"]
                t_all = total(tot)
                t_lo = total(jnp.where(slo, tot, 0.0))
                e_all = total(ee)
                e_lo = total(jnp.where(lo, ee, 0.0))
                dlast0 = t_lo + e_lo
                dlast1 = (t_all - t_lo) + (e_all - e_lo)
                dcst[2 * p:2 * p + 1, :] = _colsum(w0.T - w0) + dlast0 * last_hot
                dcst[2 * p + 1:2 * p + 2, :] = _colsum(w1.T - w1) + dlast1 * last_hot
                dxbc_ref[:, sl] = d_ref[:, sl] * dyp + dxdt * f["dtp"]
                wx[:, sl] = dxdt * x
                dstate[p] = dhs
            dcm = dcm + _bdot(dcb, bm)
            dbm = dbm + _bdot(dcb, cm, "tn")
            dxbc_ref[:, di + gi * n:di + (gi + 1) * n] = dbm
            dxbc_ref[:, di + (M2_G + gi) * n:di + (M2_G + gi + 1) * n] = dcm

        segv = seg_ref[...]
        dcs = _dot(colterm[...], segv, precision=HIGHEST) + dcst[...].T
        row = lax.broadcasted_iota(jnp.int32, (q, q), 0)
        col = lax.broadcasted_iota(jnp.int32, (q, q), 1)
        ddelta = _dot(jnp.where(col >= row, 1.0, 0.0), dcs, precision=HIGHEST)
        ddt = _dot(wx[...], segv, precision=HIGHEST) + ddelta * a_ref[...]
        da_ref[...] += _colsum(ddelta * dt)
        lane = lax.broadcasted_iota(jnp.int32, (q, LANES), 1)
        ddtraw = jnp.where(lane < n_heads, ddt * _sigmoid(dtraw_ref[...] + dtb_ref[...]), 0.0)
        ddt_ref[...] = ddtraw
        ddtb_ref[...] += _colsum(ddtraw)

        @pl.when(step == nc - 1)
        def _():
            dd_ref[...] = _dot(jnp.broadcast_to(ddfull[...], (8, di)), segv, precision=HIGHEST)

    def whole(a):
        return pl.BlockSpec(a.shape, lambda i: (0, 0))

    def rev(i):
        return nc - 1 - i

    acc = jax.ShapeDtypeStruct((1, LANES), F32)
    acc_spec = pl.BlockSpec((1, LANES), lambda i: (0, 0))
    acc8 = jax.ShapeDtypeStruct((8, LANES), F32)
    acc8_spec = pl.BlockSpec((8, LANES), lambda i: (0, 0))
    return pl.pallas_call(
        kern, name=name, grid=(nc,),
        in_specs=[pl.BlockSpec((q, c), lambda i: (rev(i), 0)), pl.BlockSpec((q, LANES), lambda i: (rev(i), 0)),
                  pl.BlockSpec((q, di), lambda i: (rev(i), 0)),
                  pl.BlockSpec((1, n_pairs, 2 * M2_P, n), lambda i: (rev(i), 0, 0, 0)),
                  whole(dtb), whole(arow), whole(dvec), whole(seg)],
        out_specs=[pl.BlockSpec((q, c), lambda i: (rev(i), 0)), pl.BlockSpec((q, LANES), lambda i: (rev(i), 0)),
                   acc_spec, acc_spec, acc8_spec],
        out_shape=[jax.ShapeDtypeStruct((rows, c), F32), jax.ShapeDtypeStruct((rows, LANES), F32), acc, acc, acc8],
        scratch_shapes=[pltpu.VMEM((n_pairs, 2 * M2_P, n), F32), pltpu.VMEM((LANES, q), F32),
                        pltpu.VMEM((LANES, q), F32), pltpu.VMEM((LANES, q), F32), pltpu.VMEM((LANES, q), F32),
                        pltpu.VMEM((q, di), F32), pltpu.VMEM((q, di), F32), pltpu.VMEM((1, di), F32)],
        compiler_params=_params("arbitrary"),
    )(xbc, dtraw, dy, prev, dtb, arow, dvec, seg)


S5_PARAM_NAMES = ("s5_lambda_re", "s5_lambda_im", "s5_log_dt", "s5_b_re", "s5_b_im", "s5_c_re", "s5_c_im")


def _row(v):
    return v.reshape(1, -1)


class _Rides:
    def __init__(self):
        self.pending = {}
        self.landed = {}

    def matmul(self, site, name, *args, **kw):
        ride = self.pending.pop(site, None)
        res = _matmul(name, *args, ride=ride, **kw)
        if ride is None:
            return res
        self.landed[site] = list(res[-len(ride):])
        res = list(res[:-len(ride)])
        return res[0] if len(res) == 1 else res


def _s5_layer_fwd(tag, x, gate, h, w, j, rides):
    u = rides.matmul("s5_win", tag + "_win", h, w["s5_w_in"][j])
    params = [w[k][j] for k in S5_PARAM_NAMES]
    ops, build_vjp = jax.vjp(_s5_build, *params)
    ug = _s5_to_groups(u.astype(BF16), True)
    yg, sprev = _s5_core_fwd(tag + "_core", ug, ops)
    yy = _s5_from_groups(yg, False)
    skip = _row(w["s5_d"][j])
    gl = _gelu_fwd(tag + "_gelu", yy, u, skip)
    ab = rides.matmul("s5_wglu", tag + "_wglu", gl, w["s5_w_glu"][j])
    x1 = _glu_fwd(tag + "_glu", ab, _row(w["s5_b_glu"][j]), x, gate)
    return x1, dict(u=u, ug=ug, ops=ops, build_vjp=build_vjp, sprev=sprev, yy=yy, gl=gl, ab=ab, skip=skip)


def _s5_layer_bwd(tag, dx1, gate, h, sv, w, j, rides):
    dab, db_glu, dgate = _glu_bwd(tag + "_glu_b", sv["ab"], _row(w["s5_b_glu"][j]), dx1, gate)
    dw_glu = rides.matmul("s5_dwglu", tag + "_dwglu", sv["gl"], dab, "tn", out_dtype=BF16)
    dgl = rides.matmul("s5_dgl", tag + "_dgl", dab, w["s5_w_glu"][j], "nt")
    dyy, dskip = _gelu_bwd(tag + "_gelu_b", sv["yy"], sv["u"], sv["skip"], dgl)
    dug, dkc, dbend, dcpow, da1, da2s = _s5_core_bwd(
        tag + "_core_b", sv["ug"], _s5_to_groups(dyy.astype(BF16), False), sv["sprev"], sv["ops"])
    dparams = sv["build_vjp"]((dkc, dbend, dcpow, da1, _swap_halves(da2s, 2)))
    du = _axpy(tag + "_du", _s5_from_groups(dug, True), dyy, sv["skip"])
    grads = dict(zip(S5_PARAM_NAMES, dparams))
    grads["s5_d"] = dskip.reshape(-1)
    grads["s5_w_in"] = _matmul(tag + "_dwin", h, du, "tn", out_dtype=BF16)
    grads["s5_w_glu"] = dw_glu
    grads["s5_b_glu"] = db_glu.reshape(-1)
    dh = _matmul(tag + "_dh", du, w["s5_w_in"][j], "nt")
    return dh, grads, dgate


def _ssd_consts(w, j, d_model):
    di = 2 * d_model
    heads = di // M2_P

    def pad_row(v):
        return jnp.zeros((1, LANES), F32).at[0, :heads].set(v)

    a = -jnp.exp(w["m2_a_log"][j])
    seg = (jnp.arange(di)[:, None] // M2_P == jnp.arange(LANES)[None, :]).astype(F32)
    w_in = w["m2_w_in"][j]
    conv_dim = di + 2 * M2_G * M2_N
    w_dt = jnp.zeros((d_model, LANES), w_in.dtype).at[:, :heads].set(w_in[:, di + conv_dim:])
    return dict(di=di, heads=heads, conv_dim=conv_dim, a=a, arow=pad_row(a), dtb=pad_row(w["m2_dt_bias"][j]),
                dvec=_row(jnp.repeat(w["m2_d"][j], M2_P)), seg=seg,
                w_z=w_in[:, :di], w_xbc=w_in[:, di:di + conv_dim], w_dt=w_dt,
                conv_w=w["m2_conv_w"][j], conv_b=_row(w["m2_conv_b"][j]), norm_g=_row(w["m2_norm_g"][j]))


def _gated_out_bwd(tag, act, dxo, w_out, gate, **kw):
    dw, dgate_parts = _matmul(tag + "_dwo", act, dxo, "tn", out_dtype=BF16, colscale=gate, colsum_with=w_out, **kw)
    dgate = jnp.sum(dgate_parts, axis=0)
    return dw, dgate, _scale_cols(tag + "_wog", w_out, gate)


def _ssd_layer_fwd(tag, x, gate, h, w, j):
    k = _ssd_consts(w, j, h.shape[1])
    z = _matmul(tag + "_wz", h, k["w_z"])
    xbc_pre = _matmul(tag + "_wxbc", h, k["w_xbc"])
    dtraw = _matmul(tag + "_wdt", h, k["w_dt"])
    xbc = _conv_fwd(tag + "_conv", xbc_pre, k["conv_w"], k["conv_b"])
    y, prev = _ssd_fwd(tag + "_core", xbc, dtraw, k["dtb"], k["arow"], k["dvec"], k["heads"])
    yn = _gatenorm_fwd(tag + "_gn", y, z, k["norm_g"])
    x1 = _matmul(tag + "_wout", yn, w["m2_w_out"][j], colscale=gate, addin=x)
    return x1, dict(k=k, z=z, xbc_pre=xbc_pre, dtraw=dtraw, xbc=xbc, y=y, prev=prev, yn=yn)


def _ssd_layer_bwd(tag, dx1, gate, h, sv, w, j):
    k = sv["k"]
    heads = k["heads"]
    dw_out, dgate, wog = _gated_out_bwd(tag, sv["yn"], dx1, w["m2_w_out"][j], gate)
    grads = {"m2_w_out": dw_out}
    dyn = _matmul(tag + "_dyn", dx1, wog, "nt")
    dyssd, dz, dng = _gatenorm_bwd(tag + "_gn_b", sv["y"], sv["z"], k["norm_g"], dyn)
    dxbc, ddtraw, da, ddtb, dd = _ssd_bwd(tag + "_core_b", sv["xbc"], sv["dtraw"], dyssd, sv["prev"],
                                          k["dtb"], k["arow"], k["dvec"], k["seg"], heads)
    dpre, dcw, dcb = _conv_bwd_pre(tag + "_conv_b1", sv["xbc_pre"], dxbc, k["conv_w"], k["conv_b"])
    dxbc_pre = _conv_bwd_in(tag + "_conv_b2", dpre, k["conv_w"])
    dw_z = _matmul(tag + "_dwz", h, dz, "tn", out_dtype=BF16)
    dw_xbc = _matmul(tag + "_dwxbc", h, dxbc_pre, "tn", out_dtype=BF16)
    dw_dt = _matmul(tag + "_dwdt", h, ddtraw, "tn", out_dtype=BF16)
    dh = _matmul(tag + "_dh1", dz, k["w_z"], "nt")
    dh = _matmul(tag + "_dh2", dxbc_pre, k["w_xbc"], "nt", addin=dh)
    dh = _matmul(tag + "_dh3", ddtraw, k["w_dt"], "nt", addin=dh)
    grads["m2_w_in"] = jnp.concatenate([dw_z, dw_xbc, dw_dt[:, :heads]], axis=1)
    grads["m2_conv_w"] = dcw
    grads["m2_conv_b"] = dcb.reshape(-1)
    grads["m2_dt_bias"] = ddtb[0, :heads]
    grads["m2_a_log"] = da[0, :heads] * k["a"]
    grads["m2_d"] = dd[0, :heads]
    grads["m2_norm_g"] = dng.reshape(-1)
    return dh, grads, dgate


def _layer_fwd(li, x, mod, w, rides, late_weights=None):
    tag = "L%d" % li
    sh1, sc1, g1, sh2, sc2, g2 = mod
    j = li // 2
    h = _normmod_fwd(tag + "_nm1", x, _row(w["norm_mix_g"][li]), sh1, sc1)
    if li % 2 == 0:
        x1, mix = _s5_layer_fwd(tag + "_s5", x, g1, h, w, j, rides)
    else:
        x1, mix = _ssd_layer_fwd(tag + "_m2", x, g1, h, w, j)
    if late_weights is not None:
        w = {**w, **late_weights(rides.landed)}
    h2 = _normmod_fwd(tag + "_nm2", x1, _row(w["norm_mlp_g"][li]), sh2, sc2)
    r = rides.matmul("w1", tag + "_w1", h2, w["mlp_w1"][li], relu=True, out_dtype=BF16)
    x2 = rides.matmul("w2", tag + "_w2", r, w["mlp_w2"][li], square_a=True, colscale=g2, addin=x1)
    return x2, dict(x=x, h=h, mix=mix, x1=x1, h2=h2, r=r), w


def _layer_bwd(li, dx2, sv, mod, w, rides, ride_own_mlp):
    tag = "L%d" % li
    sh1, sc1, g1, sh2, sc2, g2 = mod
    j = li // 2
    dw2, dg2, w2g = _gated_out_bwd(tag + "_mlp", sv["r"], dx2, w["mlp_w2"][li], g2, square_a=True)
    dr = rides.matmul("dr", tag + "_dr", dx2, w2g, "nt", out_dtype=BF16, mul2=sv["r"])
    dw1 = _matmul(tag + "_dw1", sv["h2"], dr, "tn", out_dtype=BF16)
    grads = {"mlp_w2": dw2, "mlp_w1": dw1}
    dh2 = rides.matmul("dh2", tag + "_dh2", dr, w["mlp_w1"][li], "nt")
    if ride_own_mlp:
        mlp_bufs = _grad_buffers(li, grads, parts=[1])[0]
        rides.pending["s5_dgl"], rides.pending["s5_dwglu"] = mlp_bufs[:1], mlp_bufs[1:]
    dx1, dgm, dsh2, dsc2 = _normmod_bwd(tag + "_nm2_b", sv["x1"], _row(w["norm_mlp_g"][li]), sh2, sc2, dh2, dx2)
    if li % 2 == 0:
        dh, mix_grads, dg1 = _s5_layer_bwd(tag + "_s5", dx1, g1, sv["h"], sv["mix"], w, j, rides)
    else:
        dh, mix_grads, dg1 = _ssd_layer_bwd(tag + "_m2", dx1, g1, sv["h"], sv["mix"], w, j)
    dx, dgx, dsh1, dsc1 = _normmod_bwd(tag + "_nm1_b", sv["x"], _row(w["norm_mix_g"][li]), sh1, sc1, dh, dx1)
    grads["norm_mix_g"] = dgx.reshape(-1)
    grads["norm_mlp_g"] = dgm.reshape(-1)
    dmod = jnp.concatenate([dsh1, dsc1, dg1, dsh2, dsc2, dg2], axis=1)
    return dx, {**grads, **mix_grads}, dmod


def _layer_parts(li):
    j = li // 2
    mix = [("s5_w_in", j, 0), ("s5_w_glu", j, 1)] if li % 2 == 0 else [("m2_w_in", j, 1), ("m2_w_out", j, 0)]
    return [mix, [("mlp_w1", li, 1), ("mlp_w2", li, 0)]]


def _grad_specs(li):
    j = li // 2
    mlp = [("one", ("mlp_w2", li, 0)), ("one", ("mlp_w1", li, 1))]
    if li % 2 == 0:
        return [[("one", ("s5_w_in", j, 0)), ("one", ("s5_w_glu", j, 1))], mlp]
    packed = [("m2_w_in", j, 1), ("m2_conv_w", j, 1), ("m2_conv_b", j, 0), ("m2_norm_g", j, 0)]
    return [[("one", ("m2_w_out", j, 0)), ("packed", packed)], mlp]


def _grad_buffers(li, grads, parts=(0, 1)):
    out = []
    for k in parts:
        bufs = []
        for kind, entry in _grad_specs(li)[k]:
            if kind == "one":
                bufs.append((grads[entry[0]], "rows" if entry[2] == 0 else "cols"))
            else:
                pieces = [_pack([_chip_slice(grads[n], c, ax) for n, _, ax in entry], BF16) for c in range(N_CHIP)]
                bufs.append((jnp.stack(pieces), "packed"))
        out.append(bufs)
    return out


def _gather_buffers(local, part):
    return [(local[n][i].astype(BF16), "same") for n, i, _ in part]


def _assemble(landed, part):
    out = {}
    for buf, (n, i, ax) in zip(landed, part):
        if ax == 0:
            out[n] = {i: buf.reshape(-1, buf.shape[2])}
        else:
            out[n] = {i: jnp.concatenate([buf[k] for k in range(N_CHIP)], axis=1)}
    return out


def _local_step(x, target, mods, w, local):
    depth = w["norm_mix_g"].shape[0]
    d = x.shape[1]
    rides = _Rides()
    saved, mod_rows, layer_w = [], [], []
    mix0, mlp0 = _layer_parts(0)
    wl = {**w, **_assemble(_exchange4("ag_w_L0", _gather_buffers(local, mix0)), mix0)}
    rides.pending["s5_win"] = _gather_buffers(local, mlp0[:1])
    rides.pending["s5_wglu"] = _gather_buffers(local, mlp0[1:])

    def late_mlp0(landed):
        return _assemble(landed.pop("s5_win") + landed.pop("s5_wglu"), mlp0)

    for li in range(depth):
        if li + 1 < depth:
            nxt = _layer_parts(li + 1)
            rides.pending["w1"] = _gather_buffers(local, nxt[0])
            rides.pending["w2"] = _gather_buffers(local, nxt[1])
        mod = [mods[li:li + 1, i * d:(i + 1) * d] for i in range(N_MOD)]
        mod_rows.append(mod)
        x, sv, wl = _layer_fwd(li, x, mod, wl, rides, late_mlp0 if li == 0 else None)
        saved.append(sv)
        layer_w.append(wl)
        if li + 1 < depth:
            wl = {**w, **_assemble(rides.landed.pop("w1"), nxt[0]), **_assemble(rides.landed.pop("w2"), nxt[1])}
    dx, dgf, loss = _loss_head("loss_head", x, target, _row(w["final_norm_g"]))
    layer_grads = [None] * depth
    dmods = [None] * depth
    landed = {}
    for li in reversed(range(depth)):
        dx, layer_grads[li], dmods[li] = _layer_bwd(li, dx, saved[li], mod_rows[li], layer_w[li], rides, li == 0)
        if li + 1 < depth:
            landed[(li + 1, 0)] = rides.landed.pop("dr")
            landed[(li + 1, 1)] = rides.landed.pop("dh2")
        if li > 0:
            rides.pending["dr"], rides.pending["dh2"] = _grad_buffers(li, layer_grads[li])
    landed[(0, 1)] = rides.landed.pop("s5_dgl") + rides.landed.pop("s5_dwglu")
    landed[(0, 0)] = _exchange4("rs_g_L0", _grad_buffers(0, layer_grads[0], parts=[0])[0])
    return loss, dx, layer_grads, dgf.reshape(-1), jnp.concatenate(dmods, axis=0), landed


ANY = pl.BlockSpec(memory_space=pl.ANY)
N_DEV = 8
N_CHIP = 4


def _coords():
    return lax.axis_index("x"), lax.axis_index("y"), lax.axis_index("c")


def _allgather8(name, block):
    r, wd = block.shape

    def body(x_ref, out_ref, send_sems, recv_sems, local_sem):
        x, y, c = _coords()
        me, sibling = (x, y, c), (x, y, 1 - c)
        chips = [(1 - x, y), (x, 1 - y), (1 - x, 1 - y)]

        def slot(px, py, pc):
            return out_ref.at[4 * px + 2 * py + pc]

        def copy(k, blk, to, src=None):
            return pltpu.make_async_remote_copy(
                src_ref=slot(*blk) if src is None else src, dst_ref=slot(*blk),
                send_sem=send_sems.at[k], recv_sem=recv_sems.at[k], device_id=to, device_id_type=MESH)

        mine = pltpu.make_async_copy(x_ref, slot(*me), local_sem)
        mine.start()
        first = [copy(0, me, sibling, src=x_ref)]
        first += [copy(1 + j, me, (*chip, c), src=x_ref) for j, chip in enumerate(chips)]
        for cp in first:
            cp.start()
        passed = [copy(4 + j, (*chip, c), sibling) for j, chip in enumerate(chips)]
        for j, chip in enumerate(chips):
            copy(1 + j, (*chip, c), me).wait_recv()
            passed[j].start()
        copy(0, sibling, me).wait_recv()
        for j, chip in enumerate(chips):
            copy(4 + j, (*chip, 1 - c), me).wait_recv()
        for cp in first + passed:
            cp.wait_send()
        mine.wait()

    return pl.pallas_call(
        body, name=name, in_specs=[ANY], out_specs=ANY,
        out_shape=jax.ShapeDtypeStruct((N_DEV, r, wd), block.dtype),
        scratch_shapes=[pltpu.SemaphoreType.DMA((7,)), pltpu.SemaphoreType.DMA((7,)), pltpu.SemaphoreType.DMA],
    )(block)


def _landing_shape(src, kind):
    if kind == "same":
        return (N_CHIP,) + src.shape
    if kind == "packed":
        return src.shape
    rows, cols = src.shape
    return (N_CHIP, rows // N_CHIP, cols) if kind == "rows" else (N_CHIP, rows, cols // N_CHIP)


def _exchange4_ops(srcs, dsts, send_sems, recv_sems, local_sems, kinds):
    x, y, c = _coords()
    my_chip = 2 * x + y
    chips = [(1 - x, y), (x, 1 - y), (1 - x, 1 - y)]

    def piece(q, k):
        ref, kind = srcs[q], kinds[q]
        if kind == "same":
            return ref
        if kind == "packed":
            return ref.at[k]
        _, rows, cols = dsts[q].shape
        return ref.at[pl.ds(k * rows, rows), :] if kind == "rows" else ref.at[:, pl.ds(k * cols, cols)]

    def copy(q, j, k, slot):
        px, py = chips[j]
        return pltpu.make_async_remote_copy(
            src_ref=piece(q, k), dst_ref=dsts[q].at[slot], send_sem=send_sems.at[3 * q + j],
            recv_sem=recv_sems.at[3 * q + j], device_id=(px, py, c), device_id_type=MESH)

    def mine(q):
        return pltpu.make_async_copy(piece(q, my_chip), dsts[q].at[my_chip], local_sems.at[q])

    def start():
        for q in range(len(srcs)):
            mine(q).start()
            for j, (px, py) in enumerate(chips):
                copy(q, j, 2 * px + py, my_chip).start()

    def wait():
        for q in range(len(srcs)):
            for j, (px, py) in enumerate(chips):
                copy(q, j, my_chip, 2 * px + py).wait_recv()
        for q in range(len(srcs)):
            for j, (px, py) in enumerate(chips):
                copy(q, j, 2 * px + py, my_chip).wait_send()
            mine(q).wait()

    return start, wait


def _exchange_scratch(n):
    return [pltpu.SemaphoreType.DMA((3 * n,)), pltpu.SemaphoreType.DMA((3 * n,)), pltpu.SemaphoreType.DMA((n,))]


def _exchange4(name, buffers):
    n = len(buffers)
    kinds = [kind for _, kind in buffers]

    def body(*refs):
        start, wait = _exchange4_ops(refs[:n], refs[n:2 * n], *refs[2 * n:], kinds)
        start()
        wait()

    return pl.pallas_call(
        body, name=name, in_specs=[ANY] * n, out_specs=[ANY] * n,
        out_shape=[jax.ShapeDtypeStruct(_landing_shape(s, k), s.dtype) for s, k in buffers],
        scratch_shapes=_exchange_scratch(n),
    )(*[s for s, _ in buffers])


def _swap_sibling(name, block):
    def body(x_ref, out_ref, send_sem, recv_sem):
        x, y, c = _coords()
        cp = pltpu.make_async_remote_copy(src_ref=x_ref, dst_ref=out_ref, send_sem=send_sem, recv_sem=recv_sem,
                                          device_id=(x, y, 1 - c), device_id_type=MESH)
        cp.start()
        cp.wait()

    return pl.pallas_call(
        body, name=name, in_specs=[ANY], out_specs=ANY, out_shape=jax.ShapeDtypeStruct(block.shape, block.dtype),
        scratch_shapes=[pltpu.SemaphoreType.DMA, pltpu.SemaphoreType.DMA],
    )(block)


def _sum_slots(name, stacked):
    n, r, wd = stacked.shape
    tile = min(FLAT_ROWS, r)

    def kern(x_ref, o_ref):
        acc = x_ref[0].astype(F32)
        for s in range(1, n):
            acc = acc + x_ref[s].astype(F32)
        o_ref[...] = acc

    return pl.pallas_call(
        kern, name=name, grid=(r // tile,), in_specs=[pl.BlockSpec((n, tile, wd), lambda i: (0, i, 0))],
        out_specs=pl.BlockSpec((tile, wd), lambda i: (i, 0)), out_shape=jax.ShapeDtypeStruct((r, wd), F32),
        compiler_params=_params("parallel"),
    )(stacked)


def _adamw(name, w, m, v, g, g2=None):
    r, wd = w.shape
    grads = [g] if g2 is None else [g, g2]
    c1 = 1.0 - ADAM_B1 ** ADAM_STEP
    c2 = 1.0 - ADAM_B2 ** ADAM_STEP

    def body(i, *refs):
        w_ref, m_ref, v_ref = refs[:3]
        g_refs = refs[3:3 + len(grads)]
        go_ref, d_ref, mo_ref, vo_ref = refs[3 + len(grads):]
        gv = g_refs[0][...]
        if g2 is not None:
            gv = gv + g_refs[1][...]
        mn = ADAM_B1 * m_ref[...] + (1.0 - ADAM_B1) * gv
        vn = ADAM_B2 * v_ref[...] + (1.0 - ADAM_B2) * (gv * gv)
        go_ref[...] = gv
        mo_ref[...] = mn
        vo_ref[...] = vn
        d_ref[...] = -ADAM_LR * ((mn / c1) / (jnp.sqrt(vn / c2) + ADAM_EPS) + ADAM_WD * w_ref[...])

    return _rowcall(name, body, r, FLAT_ROWS, [w, m, v] + grads, [], [(wd, F32)] * 4, [])


FLAT_BLOCK = FLAT_ROWS * FLAT_W


def _pack(arrays, dtype):
    flat = jnp.concatenate([a.reshape(-1).astype(dtype) for a in arrays])
    pad = (-flat.shape[0]) % FLAT_BLOCK
    return jnp.pad(flat, (0, pad)).reshape(-1, FLAT_W)


def _unpack(buf, shapes):
    flat = buf.reshape(-1)
    out, off = [], 0
    for s in shapes:
        n = math.prod(s)
        out.append(flat[off:off + n].reshape(s))
        off += n
    return out


SHARDED_BIG = {"mlp_w1": 2, "mlp_w2": 1, "s5_w_in": 1, "s5_w_glu": 2, "m2_w_in": 2, "m2_w_out": 1}
SHARDED_SMALL = {"m2_conv_w": 2, "m2_conv_b": 1, "m2_norm_g": 1}
REPLICATED = ("ada_b", "norm_mix_g", "norm_mlp_g", "s5_lambda_re", "s5_lambda_im", "s5_log_dt", "s5_b_re",
              "s5_b_im", "s5_c_re", "s5_c_im", "s5_d", "s5_b_glu", "m2_dt_bias", "m2_a_log", "m2_d", "final_norm_g")
WEIGHT_NAMES = ("ada_w", "ada_b", "norm_mix_g", "norm_mlp_g", "mlp_w1", "mlp_w2", "s5_w_in", "s5_lambda_re",
                "s5_lambda_im", "s5_log_dt", "s5_b_re", "s5_b_im", "s5_c_re", "s5_c_im", "s5_d", "s5_w_glu",
                "s5_b_glu", "m2_w_in", "m2_conv_w", "m2_conv_b", "m2_dt_bias", "m2_a_log", "m2_d", "m2_norm_g",
                "m2_w_out", "final_norm_g")


def _gather_weights(name, local, names_axes, dtype):
    names = list(names_axes)
    got = _exchange4(name, [(_pack([local[k] for k in names], dtype), "same")])[0]
    per_chip = [_unpack(got[j], [local[k].shape for k in names]) for j in range(N_CHIP)]
    return {k: jnp.concatenate([per_chip[j][i] for j in range(N_CHIP)], axis=names_axes[k])
            for i, k in enumerate(names)}


def _chip_slice(a, chip, axis):
    size = a.shape[axis] // N_CHIP
    return lax.slice_in_dim(a, chip * size, (chip + 1) * size, axis=axis)


def kernel(x, c, ada_w, ada_b, norm_mix_g, norm_mlp_g, mlp_w1, mlp_w2, s5_w_in, s5_lambda_re, s5_lambda_im, s5_log_dt, s5_b_re, s5_b_im, s5_c_re, s5_c_im, s5_d, s5_w_glu, s5_b_glu, m2_w_in, m2_conv_w, m2_conv_b, m2_dt_bias, m2_a_log, m2_d, m2_norm_g, m2_w_out, final_norm_g, loss_target, m_ada_w, m_ada_b, m_norm_mix_g, m_norm_mlp_g, m_mlp_w1, m_mlp_w2, m_s5_w_in, m_s5_lambda_re, m_s5_lambda_im, m_s5_log_dt, m_s5_b_re, m_s5_b_im, m_s5_c_re, m_s5_c_im, m_s5_d, m_s5_w_glu, m_s5_b_glu, m_m2_w_in, m_m2_conv_w, m_m2_conv_b, m_m2_dt_bias, m_m2_a_log, m_m2_d, m_m2_norm_g, m_m2_w_out, m_final_norm_g, v_ada_w, v_ada_b, v_norm_mix_g, v_norm_mlp_g, v_mlp_w1, v_mlp_w2, v_s5_w_in, v_s5_lambda_re, v_s5_lambda_im, v_s5_log_dt, v_s5_b_re, v_s5_b_im, v_s5_c_re, v_s5_c_im, v_s5_d, v_s5_w_glu, v_s5_b_glu, v_m2_w_in, v_m2_conv_w, v_m2_conv_b, v_m2_dt_bias, v_m2_a_log, v_m2_d, v_m2_norm_g, v_m2_w_out, v_final_norm_g):
    args = locals()
    local = {k: args[k] for k in WEIGHT_NAMES}
    mom_m = {k: args["m_" + k] for k in WEIGHT_NAMES}
    mom_v = {k: args["v_" + k] for k in WEIGHT_NAMES}
    depth, d = norm_mix_g.shape
    xi, yi, ci = _coords()
    my_chip = 2 * xi + yi
    my_dev = 2 * my_chip + ci

    cond = jax.nn.silu(c).reshape(-1, LANES)
    cond_all = _allgather8("ag_cond", cond).reshape(N_DEV, d)
    cond_pad = jnp.zeros((LANES, d), F32).at[:N_DEV].set(cond_all)
    mod_cols = ada_w.shape[2]
    mod_part = jnp.stack([_matmul("ada_%d" % i, cond_pad, ada_w[i])[:N_DEV] for i in range(depth)])
    mod_all = _allgather8("ag_mod", mod_part.reshape(-1, LANES)).reshape(N_CHIP, 2, depth, N_DEV, mod_cols)[:, 0]
    mod_mine = lax.dynamic_index_in_dim(mod_all, my_dev, axis=2, keepdims=False)
    mods = jnp.transpose(mod_mine, (1, 0, 2)).reshape(depth, N_CHIP * mod_cols) + ada_b

    w = {k: local[k] for k in REPLICATED}
    w.update(_gather_weights("ag_w_small", local, SHARDED_SMALL, F32))

    loss_row, dx, layer_grads, g_final, dmods, landed = _local_step(x[0], loss_target[0], mods, w, local)
    grads = {"ada_b": dmods, "final_norm_g": g_final}
    for k in REPLICATED[1:-1]:
        grads[k] = jnp.stack([g[k] for g in layer_grads if k in g])

    rep_shapes = [grads[k].shape for k in REPLICATED]
    rep_all = _allgather8("ag_grep", _pack([grads[k] for k in REPLICATED], F32))
    rep_sum = _sum_slots("sum_grep", rep_all)
    dmods_all = rep_all.reshape(N_DEV, -1)[:, :dmods.size].reshape(N_DEV, depth, N_CHIP * mod_cols)

    dm_mine = lax.dynamic_slice_in_dim(dmods_all, my_chip * mod_cols, mod_cols, axis=2)
    dm_pad = jnp.zeros((LANES, depth, mod_cols), F32).at[:N_DEV].set(dm_mine)
    g_ada_w = jnp.stack([_matmul("dada_%d" % i, cond_pad, dm_pad[:, i], "tn") for i in range(depth)])

    red = {}
    for li in range(depth):
        for k, part in enumerate(_grad_specs(li)):
            for (kind, entry), land in zip(part, landed[(li, k)]):
                if kind == "one":
                    red[entry[:2]] = _sum_slots("sum_%s_%d" % entry[:2], land)
                else:
                    total = _sum_slots("sum_packed_L%d" % li, land)
                    shapes = [local[n][i].shape for n, i, _ in entry]
                    red.update({(n, i): v for (n, i, _), v in zip(entry, _unpack(total, shapes))})

    out_g, out_d, out_m, out_v = {}, {}, {}, {}
    for name in list(SHARDED_BIG) + list(SHARDED_SMALL):
        shape = local[name].shape
        flat2 = (-1, shape[-1])
        g1 = jnp.stack([red[(name, i)] for i in range(shape[0])]).reshape(flat2)
        g2 = _swap_sibling("swap_" + name, g1)
        res = _adamw("adam_" + name, local[name].reshape(flat2), mom_m[name].reshape(flat2),
                     mom_v[name].reshape(flat2), g1, g2)
        for dst, buf in zip((out_g, out_d, out_m, out_v), res):
            dst[name] = buf.reshape(shape)
    res = _adamw("adam_rep", _pack([local[k] for k in REPLICATED], F32), _pack([mom_m[k] for k in REPLICATED], F32),
                 _pack([mom_v[k] for k in REPLICATED], F32), rep_sum)
    for dst, buf in zip((out_g, out_d, out_m, out_v), res):
        dst.update(zip(REPLICATED, _unpack(buf, rep_shapes)))
    flat2 = (-1, mod_cols)
    res = _adamw("adam_ada", ada_w.reshape(flat2), m_ada_w.reshape(flat2), v_ada_w.reshape(flat2),
                 g_ada_w.reshape(flat2))
    for dst, buf in zip((out_g, out_d, out_m, out_v), res):
        dst["ada_w"] = buf.reshape(ada_w.shape)

    loss = lax.psum(loss_row[0, 0], ("x", "y", "c"))
    outs = [loss, dx[None]]
    for dst in (out_g, out_d, out_m, out_v):
        outs += [dst[k] for k in WEIGHT_NAMES]
    return tuple(outs)
```

```python
import functools
import math

import jax
import jax.numpy as jnp
from jax import lax
from jax.experimental import pallas as pl
from jax.experimental.pallas import tpu as pltpu

F32 = jnp.float32
BF16 = jnp.bfloat16
HIGHEST = lax.Precision.HIGHEST

NORM_EPS = 1e-5
N_MOD = 6
S5_H, S5_P, S5_T = 16, 64, 64
M2_P, M2_N, M2_G, M2_Q, M2_K = 64, 128, 4, 128, 4
LANES = 128
ADAM_LR, ADAM_B1, ADAM_B2, ADAM_EPS, ADAM_WD, ADAM_STEP = 0.001, 0.9, 0.999, 1e-08, 0.01, 10
VMEM_LIMIT_BYTES = 56 * 1024 * 1024
ROW_TILE = 256
FLAT_W = 1024
FLAT_ROWS = 256
MESH = pl.DeviceIdType.MESH


def _params(*sem):
    return pltpu.CompilerParams(dimension_semantics=sem, vmem_limit_bytes=VMEM_LIMIT_BYTES)


def _dot(a, b, dn="nn", precision=None):
    dims = {"nn": ((1,), (0,)), "nt": ((1,), (1,)), "tn": ((0,), (0,))}[dn]
    return lax.dot_general(a, b, (dims, ((), ())), preferred_element_type=F32, precision=precision)


def _bdot(a, b, dn="nn"):
    return _dot(a.astype(BF16), b.astype(BF16), dn)


def _sigmoid(x):
    return jax.nn.sigmoid(x)


def _colsum(x):
    return jnp.sum(x, axis=0, keepdims=True)


def _pick_tile(dim, want):
    if dim <= want:
        return dim
    for t in range(want - want % LANES, 0, -LANES):
        if dim % t == 0:
            return t
    raise ValueError((dim, want))


MATMUL_TILE = 1024
MATMUL_VMEM_BUDGET = 40 * 1024 * 1024


def _matmul_tiles(m, n, k, mode, in_bytes, out_bytes):
    tn = _pick_tile(n, MATMUL_TILE)
    k_tiles = [k] + [t for t in (4096, 2048, 1024) if t < k and k % t == 0]
    m_tiles = [_pick_tile(m, MATMUL_TILE)] + ([512] if mode != "tn" and m % 512 == 0 and m > 512 else [])
    for tk in k_tiles:
        for tm in m_tiles:
            blocks = 2 * (tm * tk * in_bytes[0] + tk * tn * in_bytes[1] + tm * tn * out_bytes)
            if blocks + (4 * tm * tn if tk < k else 0) <= MATMUL_VMEM_BUDGET:
                return tm, tn, tk
    raise ValueError((m, n, k))


def _matmul(name, a, b, mode="nn", out_dtype=F32, relu=False, square_a=False, mul2=None, colscale=None,
            addin=None, colsum_with=None, ride=None):
    if mode == "nn":
        (m, k), (k2, n) = a.shape, b.shape
    elif mode == "nt":
        (m, k), (n, k2) = a.shape, b.shape
    else:
        (k, m), (k2, n) = a.shape, b.shape
    assert k == k2, (name, a.shape, b.shape)
    tiles = [e for e in (mul2, addin, colsum_with) if e is not None]
    out_bytes = jnp.dtype(out_dtype).itemsize + sum(e.dtype.itemsize for e in tiles)
    tm, tn, tk = _matmul_tiles(m, n, k, mode, (a.dtype.itemsize, b.dtype.itemsize), out_bytes)
    nk = k // tk
    n_ext = len(tiles) + (colscale is not None)
    n_out = 1 + (colsum_with is not None)
    n_ride = 0 if ride is None else len(ride)
    grid = (m // tm, n // tn, nk)

    def kern(*refs):
        a_ref, b_ref = refs[:2]
        e_refs = list(refs[2:2 + n_ext])
        o_refs = refs[2 + n_ext + n_ride:2 + n_ext + n_ride + n_out]
        kk = pl.program_id(2)
        if ride is not None:
            here = [pl.program_id(ax) for ax in range(3)]
            land0 = 2 + n_ext + n_ride + n_out
            ride_refs = (refs[2 + n_ext:2 + n_ext + n_ride], refs[land0:land0 + n_ride]) + tuple(refs[-3:])
            ride_kinds = [kind for _, kind in ride]
            first = (here[0] == 0) & (here[1] == 0) & (here[2] == 0)
            last = (here[0] == grid[0] - 1) & (here[1] == grid[1] - 1) & (here[2] == grid[2] - 1)

            @pl.when(first)
            def _():
                _exchange4_ops(*ride_refs, ride_kinds)[0]()

        av = a_ref[...]
        if square_a:
            av = av * av
        part = _bdot(av, b_ref[...], mode)

        def finish(r):
            ext = list(e_refs)
            m2v = ext.pop(0)[...].astype(F32) if mul2 is not None else None
            addv = ext.pop(0)[...].astype(F32) if addin is not None else None
            if colsum_with is not None:
                o_refs[1][0] = _colsum(r * ext.pop(0)[...].astype(F32))
            if relu:
                r = jnp.maximum(r, 0.0)
            if m2v is not None:
                r = r * (2.0 * m2v)
            if colscale is not None:
                r = r * ext.pop(0)[...]
            if addv is not None:
                r = r + addv
            o_refs[0][...] = r.astype(out_dtype)

        if nk == 1:
            finish(part)
        else:
            acc = refs[-4] if ride is not None else refs[-1]

            @pl.when(kk == 0)
            def _():
                acc[...] = part

            @pl.when(kk > 0)
            def _():
                acc[...] += part

            @pl.when(kk == nk - 1)
            def _():
                finish(acc[...])

        if ride is not None:
            @pl.when(last)
            def _():
                _exchange4_ops(*ride_refs, ride_kinds)[1]()

    if mode == "tn":
        a_spec = pl.BlockSpec((tk, tm), lambda i, j, kk: (kk, i))
    else:
        a_spec = pl.BlockSpec((tm, tk), lambda i, j, kk: (i, kk))
    if mode == "nt":
        b_spec = pl.BlockSpec((tn, tk), lambda i, j, kk: (j, kk))
    else:
        b_spec = pl.BlockSpec((tk, tn), lambda i, j, kk: (kk, j))
    o_spec = pl.BlockSpec((tm, tn), lambda i, j, kk: (i, j))
    in_specs = [a_spec, b_spec] + [o_spec] * len(tiles)
    operands = [a, b] + tiles
    if colscale is not None:
        in_specs.append(pl.BlockSpec((1, tn), lambda i, j, kk: (0, j)))
        operands.append(colscale)
    out_specs = [o_spec]
    out_shape = [jax.ShapeDtypeStruct((m, n), out_dtype)]
    if colsum_with is not None:
        out_specs.append(pl.BlockSpec((1, 1, tn), lambda i, j, kk: (i, 0, j)))
        out_shape.append(jax.ShapeDtypeStruct((m // tm, 1, n), F32))
    scratch = [pltpu.VMEM((tm, tn), F32)] if nk > 1 else []
    semantics = ("parallel", "parallel", "arbitrary")
    if ride is not None:
        for src, kind in ride:
            in_specs.append(pl.BlockSpec(memory_space=pl.ANY))
            operands.append(src)
            out_specs.append(pl.BlockSpec(memory_space=pl.ANY))
            out_shape.append(jax.ShapeDtypeStruct(_landing_shape(src, kind), src.dtype))
        scratch += _exchange_scratch(n_ride)
        semantics = ("arbitrary", "arbitrary", "arbitrary")
    res = pl.pallas_call(
        kern, name=name, grid=grid, in_specs=in_specs, out_specs=out_specs, out_shape=out_shape,
        scratch_shapes=scratch, compiler_params=_params(*semantics),
    )(*operands)
    return res if len(res) > 1 else res[0]


def _rowcall(name, body, rows, tile, row_ins, small_ins, row_outs, acc_outs):
    tile = min(tile, rows)
    assert rows % tile == 0, (name, rows, tile)
    n_in = len(row_ins) + len(small_ins)

    def kern(*refs):
        i = pl.program_id(0)
        accs = refs[n_in + len(row_outs):]

        @pl.when(i == 0)
        def _():
            for acc in accs:
                acc[...] = jnp.zeros_like(acc)

        body(i, *refs)

    def whole(shape):
        return pl.BlockSpec(shape, lambda i, nd=len(shape): (0,) * nd)

    in_specs = [pl.BlockSpec((tile, a.shape[1]), lambda i: (i, 0)) for a in row_ins]
    in_specs += [whole(a.shape) for a in small_ins]
    out_specs = [pl.BlockSpec((tile, w), lambda i: (i, 0)) for (w, _) in row_outs]
    out_specs += [whole(s) for s in acc_outs]
    out_shape = [jax.ShapeDtypeStruct((rows, w), dt) for (w, dt) in row_outs]
    out_shape += [jax.ShapeDtypeStruct(s, F32) for s in acc_outs]
    return pl.pallas_call(
        kern, name=name, grid=(rows // tile,), in_specs=in_specs, out_specs=out_specs, out_shape=out_shape,
        compiler_params=_params("arbitrary"),
    )(*row_ins, *small_ins)


def _rms(x):
    r = lax.rsqrt(jnp.mean(x * x, axis=-1, keepdims=True) + NORM_EPS)
    return x * r, r


def _rms_bwd(dxhat, xhat, r):
    return r * (dxhat - xhat * jnp.mean(dxhat * xhat, axis=-1, keepdims=True))


def _normmod_fwd(name, x, g, sh, sc):
    def body(i, x_ref, g_ref, sh_ref, sc_ref, o_ref):
        xhat, _ = _rms(x_ref[...])
        o_ref[...] = ((xhat * g_ref[...]) * (1.0 + sc_ref[...]) + sh_ref[...]).astype(BF16)

    return _rowcall(name, body, x.shape[0], ROW_TILE, [x], [g, sh, sc], [(x.shape[1], BF16)], [])[0]


def _normmod_bwd(name, x, g, sh, sc, dh, dx_pass):
    d = x.shape[1]

    def body(i, x_ref, dh_ref, dxp_ref, g_ref, sh_ref, sc_ref, dx_ref, dxb_ref, dg_ref, dsh_ref, dsc_ref):
        xhat, r = _rms(x_ref[...])
        dh = dh_ref[...].astype(F32)
        gv = g_ref[...]
        dn = dh * (1.0 + sc_ref[...])
        dsc_ref[...] += _colsum(dh * (xhat * gv))
        dsh_ref[...] += _colsum(dh)
        dg_ref[...] += _colsum(dn * xhat)
        dx = dxp_ref[...] + _rms_bwd(dn * gv, xhat, r)
        dx_ref[...] = dx
        dxb_ref[...] = dx.astype(BF16)

    return _rowcall(name, body, x.shape[0], ROW_TILE, [x, dh, dx_pass], [g, sh, sc], [(d, F32), (d, BF16)],
                    [(1, d), (1, d), (1, d)])


def _scale_cols(name, w, g):
    def body(i, w_ref, g_ref, o_ref):
        o_ref[...] = (w_ref[...].astype(F32) * g_ref[...]).astype(BF16)

    return _rowcall(name, body, w.shape[0], ROW_TILE, [w], [g], [(w.shape[1], BF16)], [])[0]


GELU_K = math.sqrt(2.0 / math.pi)
GELU_C = 0.044715


def _gelu_fwd(name, y, u, skip):
    def body(i, y_ref, u_ref, s_ref, o_ref):
        v = y_ref[...].astype(F32) + s_ref[...] * u_ref[...]
        t = jnp.tanh(GELU_K * (v + GELU_C * (v * v * v)))
        o_ref[...] = (0.5 * v * (1.0 + t)).astype(BF16)

    return _rowcall(name, body, y.shape[0], ROW_TILE, [y, u], [skip], [(y.shape[1], BF16)], [])[0]


def _gelu_bwd(name, y, u, skip, dgl):
    d = y.shape[1]

    def body(i, y_ref, u_ref, d_ref, s_ref, o_ref, ds_ref):
        uv = u_ref[...]
        v = y_ref[...].astype(F32) + s_ref[...] * uv
        t = jnp.tanh(GELU_K * (v + GELU_C * (v * v * v)))
        dv = d_ref[...] * (0.5 * (1.0 + t) + 0.5 * v * (1.0 - t * t) * (GELU_K * (1.0 + 3.0 * GELU_C * v * v)))
        o_ref[...] = dv
        ds_ref[...] += _colsum(dv * uv)

    return _rowcall(name, body, y.shape[0], ROW_TILE, [y, u, dgl], [skip], [(d, F32)], [(1, d)])


def _axpy(name, a, b, scale):
    def body(i, a_ref, b_ref, s_ref, o_ref):
        o_ref[...] = (a_ref[...].astype(F32) + s_ref[...] * b_ref[...]).astype(BF16)

    return _rowcall(name, body, a.shape[0], ROW_TILE, [a, b], [scale], [(a.shape[1], BF16)], [])[0]


def _glu_fwd(name, ab, bias, x, gate):
    d = ab.shape[1] // 2

    def body(i, ab_ref, x_ref, b_ref, g_ref, o_ref):
        v = ab_ref[:, :d] + b_ref[:, :d]
        gt = ab_ref[:, d:] + b_ref[:, d:]
        o_ref[...] = x_ref[...] + g_ref[...] * (v * _sigmoid(gt))

    return _rowcall(name, body, ab.shape[0], ROW_TILE, [ab, x], [bias, gate], [(d, F32)], [])[0]


def _glu_bwd(name, ab, bias, dxo, gate):
    d = ab.shape[1] // 2

    def body(i, ab_ref, dx_ref, b_ref, g_ref, dab_ref, db_ref, dg_ref):
        v = ab_ref[:, :d] + b_ref[:, :d]
        s = _sigmoid(ab_ref[:, d:] + b_ref[:, d:])
        dxo_v = dx_ref[...]
        dg_ref[...] += _colsum(dxo_v * (v * s))
        do = g_ref[...] * dxo_v
        dv = do * s
        dgt = do * v * (s * (1.0 - s))
        dab_ref[:, :d] = dv.astype(BF16)
        dab_ref[:, d:] = dgt.astype(BF16)
        db_ref[:, :d] += _colsum(dv)
        db_ref[:, d:] += _colsum(dgt)

    return _rowcall(name, body, ab.shape[0], ROW_TILE, [ab, dxo], [bias, gate], [(2 * d, BF16)],
                    [(1, 2 * d), (1, d)])


def _gatenorm_fwd(name, y, z, ng):
    di = y.shape[1]
    gw = di // M2_G

    def body(i, y_ref, z_ref, g_ref, o_ref):
        for gi in range(M2_G):
            sl = slice(gi * gw, (gi + 1) * gw)
            zz = z_ref[:, sl]
            y2 = y_ref[:, sl] * (zz * _sigmoid(zz))
            yh, _ = _rms(y2)
            o_ref[:, sl] = (yh * g_ref[:, sl]).astype(BF16)

    return _rowcall(name, body, y.shape[0], ROW_TILE, [y, z], [ng], [(di, BF16)], [])[0]


def _gatenorm_bwd(name, y, z, ng, dyn):
    di = y.shape[1]
    gw = di // M2_G

    def body(i, y_ref, z_ref, d_ref, g_ref, dy_ref, dz_ref, dg_ref):
        for gi in range(M2_G):
            sl = slice(gi * gw, (gi + 1) * gw)
            zz = z_ref[:, sl]
            yy = y_ref[:, sl]
            s = _sigmoid(zz)
            sz = zz * s
            yh, r = _rms(yy * sz)
            dn = d_ref[:, sl]
            dg_ref[:, sl] += _colsum(dn * yh)
            dy2 = _rms_bwd(dn * g_ref[:, sl], yh, r)
            dy_ref[:, sl] = dy2 * sz
            dz_ref[:, sl] = (dy2 * yy * (s * (1.0 + zz * (1.0 - s)))).astype(BF16)

    return _rowcall(name, body, y.shape[0], ROW_TILE, [y, z, dyn], [ng], [(di, F32), (di, BF16)], [(1, di)])


def _loss_head(name, x, target, g):
    d = x.shape[1]

    def body(i, x_ref, t_ref, g_ref, dx_ref, dxb_ref, dg_ref, loss_ref):
        xhat, r = _rms(x_ref[...])
        gv = g_ref[...]
        err = xhat * gv - t_ref[...]
        per_row = jnp.sum(err * err, axis=-1, keepdims=True) * (0.5 / d)
        loss_ref[...] += jnp.broadcast_to(_colsum(per_row), loss_ref.shape)
        dy = err * (1.0 / d)
        dg_ref[...] += _colsum(dy * xhat)
        dx = _rms_bwd(dy * gv, xhat, r)
        dx_ref[...] = dx
        dxb_ref[...] = dx.astype(BF16)

    return _rowcall(name, body, x.shape[0], ROW_TILE, [x, target], [g], [(d, F32), (d, BF16)],
                    [(1, d), (1, LANES)])


HALO = 8


def _halo_call(name, body, rows, tile, width, mains, halo_of, halo_next, smalls, row_outs, acc_outs, scratch):
    tile = min(tile, rows)
    nb = tile // HALO
    last = rows // HALO - 1
    n_in = len(mains) + 1 + len(smalls)

    def kern(*refs):
        i = pl.program_id(0)
        accs = refs[n_in + len(row_outs):n_in + len(row_outs) + len(acc_outs)]

        @pl.when(i == 0)
        def _():
            for acc in accs:
                acc[...] = jnp.zeros_like(acc)

        body(i, *refs)

    def whole(shape):
        return pl.BlockSpec(shape, lambda i, nd=len(shape): (0,) * nd)

    if halo_next:
        halo_spec = pl.BlockSpec((HALO, width), lambda i: (jnp.minimum((i + 1) * nb, last), 0))
    else:
        halo_spec = pl.BlockSpec((HALO, width), lambda i: (jnp.maximum(i * nb - 1, 0), 0))
    in_specs = [pl.BlockSpec((tile, a.shape[1]), lambda i: (i, 0)) for a in mains] + [halo_spec]
    in_specs += [whole(a.shape) for a in smalls]
    out_specs = [pl.BlockSpec((tile, w), lambda i: (i, 0)) for (w, _) in row_outs] + [whole(s) for s in acc_outs]
    out_shape = [jax.ShapeDtypeStruct((rows, w), dt) for (w, dt) in row_outs]
    out_shape += [jax.ShapeDtypeStruct(s, F32) for s in acc_outs]
    return pl.pallas_call(
        kern, name=name, grid=(rows // tile,), in_specs=in_specs, out_specs=out_specs, out_shape=out_shape,
        scratch_shapes=scratch, compiler_params=_params("arbitrary"),
    )(*mains, mains[halo_of], *smalls)


CONV_TILE = 128
CONV_ROWS = 16
CONV_STRIP = 512


def _conv_blocks(tile, c, strip=CONV_STRIP):
    strip = strip if c % strip == 0 else LANES
    rb = min(CONV_ROWS, tile)
    return [(r0, rb, slice(c0, c0 + strip)) for c0 in range(0, c, strip) for r0 in range(0, tile, rb)]


def _shifted_windows(base, rb, offsets):
    n = base.shape[0]
    out = []
    for o in offsets:
        if o % HALO == 0:
            out.append(base[o:o + rb, :])
        else:
            out.append(pltpu.roll(base, n - o, 0)[0:rb, :])
    return out


def _conv_windows(x_ref, ext, r0, rb, sl):
    base = ext[:, sl] if r0 == 0 else x_ref[r0 - HALO:r0 + rb, sl]
    return _shifted_windows(base, rb, [HALO - 3 + k for k in range(M2_K)])


def _conv_fwd(name, xin, w, b):
    rows, c = xin.shape
    tile = min(CONV_TILE, rows)
    rb0 = min(CONV_ROWS, tile)

    def body(i, x_ref, h_ref, w_ref, b_ref, o_ref, ext):
        ext[0:HALO, :] = jnp.where(i == 0, 0.0, h_ref[...])
        ext[HALO:, :] = x_ref[0:rb0, :]
        for r0, rb, sl in _conv_blocks(tile, c):
            taps = _conv_windows(x_ref, ext, r0, rb, sl)
            pre = b_ref[:, sl] + w_ref[0:1, sl] * taps[0]
            for k in range(1, M2_K):
                pre = pre + w_ref[k:k + 1, sl] * taps[k]
            o_ref[r0:r0 + rb, sl] = pre * _sigmoid(pre)

    return _halo_call(name, body, rows, tile, c, [xin], 0, False, [w, b], [(c, F32)], [],
                      [pltpu.VMEM((rb0 + HALO, c), F32)])[0]


def _conv_bwd_pre(name, xin, dout, w, b):
    rows, c = xin.shape
    tile = min(CONV_TILE, rows)
    rb0 = min(CONV_ROWS, tile)

    def body(i, x_ref, d_ref, h_ref, w_ref, b_ref, dp_ref, dw_ref, db_ref, ext):
        ext[0:HALO, :] = jnp.where(i == 0, 0.0, h_ref[...])
        ext[HALO:, :] = x_ref[0:rb0, :]
        sums = {}
        for r0, rb, sl in _conv_blocks(tile, c, CONV_STRIP // 2):
            taps = _conv_windows(x_ref, ext, r0, rb, sl)
            pre = b_ref[:, sl] + w_ref[0:1, sl] * taps[0]
            for k in range(1, M2_K):
                pre = pre + w_ref[k:k + 1, sl] * taps[k]
            s = _sigmoid(pre)
            dp = d_ref[r0:r0 + rb, sl] * (s * (1.0 + pre * (1.0 - s)))
            dp_ref[r0:r0 + rb, sl] = dp
            part = [dp] + [dp * taps[k] for k in range(M2_K)]
            key = sl.start
            sums[key] = part if key not in sums else [p + q for p, q in zip(sums[key], part)]
            if r0 + rb == tile:
                db_ref[:, sl] += _colsum(sums[key][0])
                for k in range(M2_K):
                    dw_ref[k:k + 1, sl] += _colsum(sums[key][1 + k])

    return _halo_call(name, body, rows, tile, c, [xin, dout], 0, False, [w, b], [(c, F32)], [(M2_K, c), (1, c)],
                      [pltpu.VMEM((rb0 + HALO, c), F32)])


def _conv_bwd_in(name, dpre, w):
    rows, c = dpre.shape
    tile = min(CONV_TILE, rows)
    n_tiles = rows // tile
    rb0 = min(CONV_ROWS, tile)

    def body(i, d_ref, h_ref, w_ref, o_ref, ext):
        ext[0:rb0, :] = d_ref[tile - rb0:tile, :]
        ext[rb0:, :] = jnp.where(i == n_tiles - 1, 0.0, h_ref[...])
        for r0, rb, sl in _conv_blocks(tile, c):
            base = ext[:, sl] if r0 + rb == tile else d_ref[r0:r0 + rb + HALO, sl]
            wins = _shifted_windows(base, rb, [3 - k for k in range(M2_K)])
            acc = w_ref[0:1, sl] * wins[0]
            for k in range(1, M2_K):
                acc = acc + w_ref[k:k + 1, sl] * wins[k]
            o_ref[r0:r0 + rb, sl] = acc.astype(BF16)

    return _halo_call(name, body, rows, tile, c, [dpre], 0, True, [w], [(c, BF16)], [],
                      [pltpu.VMEM((rb0 + HALO, c), F32)])[0]


def _s5_build(lam_re, lam_im, log_dt, b_re, b_im, c_re, c_im):
    g, p = lam_re.shape
    h = b_re.shape[-1]
    t = S5_T
    dt = jnp.exp(log_dt)[:, None]
    lam = lax.complex(lam_re, lam_im)
    lam_dt = lam * dt
    lam_bar = jnp.exp(lam_dt)
    b_bar = ((lam_bar - 1) / lam)[..., None] * lax.complex(b_re, b_im)
    c_mat = lax.complex(c_re, c_im)
    tau = jnp.arange(t + 1, dtype=F32)
    pw = jnp.exp(lam_dt[:, :, None] * tau[None, None, :])
    c_t = jnp.transpose(c_mat, (0, 2, 1))
    cp = pw[:, :, :, None] * c_t[:, :, None, :]
    cp0 = cp[:, :, :t].reshape(g, p, t * h)
    cp1 = cp[:, :, 1:].reshape(g, p, t * h)
    bb_t = jnp.transpose(b_bar, (0, 2, 1))
    kc = (jnp.einsum("ghp,gpn->ghn", jnp.real(bb_t), jnp.real(cp0), precision=HIGHEST)
          - jnp.einsum("ghp,gpn->ghn", jnp.imag(bb_t), jnp.imag(cp0), precision=HIGHEST))
    bpow = jnp.transpose(pw[:, :, t - 1::-1][:, :, :t], (0, 2, 1))
    be = bb_t[:, :, None, :] * bpow[:, None, :, :]
    bend = jnp.concatenate([jnp.real(be), jnp.imag(be)], axis=-1).reshape(g, h * t, 2 * p)
    cpow = jnp.concatenate([jnp.real(cp1), -jnp.imag(cp1)], axis=1)
    at = pw[:, :, t]
    a1 = jnp.concatenate([jnp.real(at), jnp.real(at)], axis=-1)[:, None, :]
    a2 = jnp.concatenate([-jnp.imag(at), jnp.imag(at)], axis=-1)[:, None, :]
    return kc, bend, cpow, a1, a2


def _swap_halves(x, axis):
    n = x.shape[axis] // 2
    lo = lax.slice_in_dim(x, 0, n, axis=axis)
    hi = lax.slice_in_dim(x, n, 2 * n, axis=axis)
    return jnp.concatenate([hi, lo], axis=axis)


def _group_spec(shape):
    return pl.BlockSpec((1,) + tuple(shape[1:]), lambda g: (g, 0, 0))


S5_ROWS = 8


def _s5_expand_toeplitz(kc_ref, ext, toep):
    t, th = S5_T, S5_T * S5_H
    ext[:, th:] = jnp.zeros((S5_ROWS, LANES), F32)
    for hin in range(S5_H):
        ext[:, :th] = jnp.broadcast_to(kc_ref[0, hin:hin + 1, :], (S5_ROWS, th))
        rolled = pltpu.roll(ext[...], 0, 1, stride=S5_H, stride_axis=0)
        tiles = []
        for q in range(t // S5_ROWS):
            if q == 0:
                tiles.append(rolled[:, :th])
            else:
                tiles.append(jnp.concatenate([jnp.zeros((S5_ROWS, q * LANES), F32), rolled[:, :th - q * LANES]],
                                             axis=1))
        toep[hin * t:(hin + 1) * t, :] = jnp.concatenate(tiles, axis=0).astype(BF16)


def _s5_core_fwd(name, u, ops):
    kc, bend, cpow, a1, a2 = ops
    g, nc, th = u.shape
    p2 = bend.shape[-1]
    bend_b, cpow_b = bend.astype(BF16), cpow.astype(BF16)
    a2s = _swap_halves(a2, 2)

    def kern(u_ref, k_ref, b_ref, c_ref, a1_ref, a2_ref, a2s_ref, y_ref, sp_ref, x_scr, xs_scr, ext, toep):
        _s5_expand_toeplitz(k_ref, ext, toep)
        ub = u_ref[0].astype(BF16)
        xv = _dot(ub, b_ref[0])
        x_scr[...] = xv
        xs_scr[...] = pltpu.roll(xv, p2 // 2, 1)
        a1v, a2v, a2sv = a1_ref[0], a2_ref[0], a2s_ref[0]

        def step(c, carry):
            s, ss = carry
            sp_ref[0, pl.ds(c, 1), :] = s
            s_new = a1v * s + a2v * ss + x_scr[pl.ds(c, 1), :]
            ss_new = a1v * ss + a2sv * s + xs_scr[pl.ds(c, 1), :]
            return s_new, ss_new

        zero = jnp.zeros((1, p2), F32)
        lax.fori_loop(0, nc, step, (zero, zero))
        y_ref[0] = (_dot(ub, toep[...]) + _dot(sp_ref[0].astype(BF16), c_ref[0])).astype(BF16)

    ins = [u, kc, bend_b, cpow_b, a1, a2, a2s]
    return pl.pallas_call(
        kern, name=name, grid=(g,), in_specs=[_group_spec(a.shape) for a in ins],
        out_specs=[_group_spec((g, nc, th)), _group_spec((g, nc, p2))],
        out_shape=[jax.ShapeDtypeStruct((g, nc, th), BF16), jax.ShapeDtypeStruct((g, nc, p2), F32)],
        scratch_shapes=[pltpu.VMEM((nc, p2), F32), pltpu.VMEM((nc, p2), F32),
                        pltpu.VMEM((S5_ROWS, th + LANES), F32), pltpu.VMEM((th, th), BF16)],
        compiler_params=_params("arbitrary"),
    )(*ins)


def _s5_core_bwd(name, u, dy, sprev, ops):
    kc, bend, cpow, a1, a2 = ops
    g, nc, th = u.shape
    t = S5_T
    p2 = bend.shape[-1]
    bend_b, cpow_b = bend.astype(BF16), cpow.astype(BF16)
    a2s = _swap_halves(a2, 2)
    idx = jnp.arange(th)
    flip = (idx[:, None] // t == idx[None, :] // t) & (idx[:, None] % t == t - 1 - idx[None, :] % t)
    flip = flip.astype(BF16)

    def kern(u_ref, dy_ref, sp_ref, k_ref, b_ref, c_ref, a1_ref, a2_ref, a2s_ref, f_ref,
             du_ref, dk_ref, db_ref, dc_ref, da1_ref, da2_ref, g_scr, gs_scr, dx_scr, ext, toep, dtoep):
        _s5_expand_toeplitz(k_ref, ext, toep)
        ub, dyb = u_ref[0].astype(BF16), dy_ref[0].astype(BF16)
        dtoep[...] = _dot(_dot(ub, f_ref[...]).astype(BF16), dyb, "tn")
        n_q = t // S5_ROWS
        width = th + LANES
        for hin in range(S5_H):
            folded = dtoep[hin * t + (n_q - 1) * S5_ROWS:(hin + 1) * t, :]
            for qp in range(n_q - 1):
                q = n_q - 1 - qp
                tile = dtoep[hin * t + qp * S5_ROWS:hin * t + (qp + 1) * S5_ROWS, :]
                folded = folded + jnp.concatenate([tile[:, q * LANES:], jnp.zeros((S5_ROWS, q * LANES), F32)],
                                                  axis=1)
            ext[:, :th] = folded
            rolled = pltpu.roll(ext[...], 0, 1, stride=S5_H, stride_axis=0)
            rolled = pltpu.roll(rolled, width - S5_H * (S5_ROWS - 1), 1)
            dk_ref[0, hin:hin + 1, :] = _colsum(rolled)[:, :th]
        spv = sp_ref[0]
        dc_ref[0] = _dot(spv.astype(BF16), dyb, "tn")
        gv = _dot(dyb, c_ref[0], "nt")
        g_scr[...] = gv
        gs_scr[...] = pltpu.roll(gv, p2 // 2, 1)
        a1v, a2v, a2sv = a1_ref[0], a2_ref[0], a2s_ref[0]

        def step(k, carry):
            gr, grs, da1, da2 = carry
            c = nc - 1 - k
            dx_scr[pl.ds(c, 1), :] = gr
            s_in = sp_ref[0, pl.ds(c, 1), :]
            da1 = da1 + gr * s_in
            da2 = da2 + grs * s_in
            gr_new = g_scr[pl.ds(c, 1), :] + a1v * gr + a2sv * grs
            grs_new = gs_scr[pl.ds(c, 1), :] + a1v * grs + a2v * gr
            return gr_new, grs_new, da1, da2

        zero = jnp.zeros((1, p2), F32)
        _, _, da1, da2 = lax.fori_loop(0, nc, step, (zero, zero, zero, zero))
        da1_ref[0] = da1
        da2_ref[0] = da2
        dxb = dx_scr[...].astype(BF16)
        db_ref[0] = _dot(ub, dxb, "tn")
        du_ref[0] = (_dot(dyb, toep[...], "nt") + _dot(dxb, b_ref[0], "nt")).astype(BF16)

    ins = [u, dy, sprev, kc, bend_b, cpow_b, a1, a2, a2s]
    outs = [(g, nc, th), (g, S5_H, th), (g, th, p2), (g, p2, th), (g, 1, p2), (g, 1, p2)]
    out_types = [BF16] + [F32] * (len(outs) - 1)
    return pl.pallas_call(
        kern, name=name, grid=(g,),
        in_specs=[_group_spec(a.shape) for a in ins] + [pl.BlockSpec((th, th), lambda gi: (0, 0))],
        out_specs=[_group_spec(s) for s in outs],
        out_shape=[jax.ShapeDtypeStruct(s, dt) for s, dt in zip(outs, out_types)],
        scratch_shapes=[pltpu.VMEM((nc, p2), F32), pltpu.VMEM((nc, p2), F32), pltpu.VMEM((nc, p2), F32),
                        pltpu.VMEM((S5_ROWS, th + LANES), F32), pltpu.VMEM((th, th), BF16),
                        pltpu.VMEM((th, th), F32)],
        compiler_params=_params("arbitrary"),
    )(*ins, flip)


def _s5_to_groups(u, channel_major):
    rows, w = u.shape
    g = w // S5_H
    nc = rows // S5_T
    perm = (2, 0, 3, 1) if channel_major else (2, 0, 1, 3)
    return u.reshape(nc, S5_T, g, S5_H).transpose(perm).reshape(g, nc, S5_T * S5_H)


def _s5_from_groups(y, channel_major):
    g, nc, _ = y.shape
    if channel_major:
        return y.reshape(g, nc, S5_H, S5_T).transpose(1, 3, 0, 2).reshape(nc * S5_T, g * S5_H)
    return y.reshape(g, nc, S5_T, S5_H).transpose(1, 2, 0, 3).reshape(nc * S5_T, g * S5_H)


def _softplus(x):
    return jnp.maximum(x, 0.0) + jnp.log(1.0 + jnp.exp(-jnp.abs(x)))


def _ssd_chunk_prep(dtraw_ref, dtb_ref, a_ref, cst, dtt, lastt, n_heads):
    q = M2_Q
    lane = lax.broadcasted_iota(jnp.int32, (q, LANES), 1)
    dt = jnp.where(lane < n_heads, _softplus(dtraw_ref[...] + dtb_ref[...]), 0.0)
    adt = dt * a_ref[...]
    row = lax.broadcasted_iota(jnp.int32, (q, q), 0)
    col = lax.broadcasted_iota(jnp.int32, (q, q), 1)
    cs = _dot(jnp.where(row >= col, 1.0, 0.0), adt, precision=HIGHEST)
    cst[...] = cs.T
    dtt[...] = dt.T
    lastt[...] = jnp.broadcast_to(_colsum(adt), (q, LANES)).T
    return dt


def _pair_tables(cst, dtt, lastt, p):
    q = M2_Q
    out = []
    for hh in (2 * p, 2 * p + 1):
        rc = jnp.broadcast_to(cst[hh:hh + 1, :], (q, q))
        cc = rc.T
        dtc = jnp.broadcast_to(dtt[hh:hh + 1, :], (q, q)).T
        lb = jnp.broadcast_to(lastt[hh:hh + 1, :], (q, q))
        out.append((rc, cc, dtc, lb))
    return out


def _ssd_pair_fwd(x, bm, cm, cb, hs, tabs):
    q = M2_Q
    row = lax.broadcasted_iota(jnp.int32, (q, q), 0)
    col = lax.broadcasted_iota(jnp.int32, (q, q), 1)
    causal = row >= col
    lo = col < M2_P
    slo = row < M2_P
    (rc0, cc0, dtc0, lb0), (rc1, cc1, dtc1, lb1) = tabs
    l0 = jnp.where(causal, jnp.exp(jnp.where(causal, cc0 - rc0, 0.0)), 0.0)
    l1 = jnp.where(causal, jnp.exp(jnp.where(causal, cc1 - rc1, 0.0)), 0.0)
    m0, m1 = cb * l0, cb * l1
    dtp = jnp.where(lo, dtc0, dtc1)
    xdt = x * dtp
    xdt0 = jnp.where(lo, xdt, 0.0)
    xdt1 = jnp.where(lo, 0.0, xdt)
    e = jnp.where(lo, jnp.exp(cc0), jnp.exp(cc1))
    z = _bdot(cm, hs, "nt")
    yoff = z * e
    dec = jnp.where(lo, jnp.exp(lb0 - cc0), jnp.exp(lb1 - cc1))
    xdd = xdt * dec
    cd = jnp.where(slo, jnp.exp(lb0), jnp.exp(lb1))
    return dict(l0=l0, l1=l1, m0=m0, m1=m1, dtp=dtp, xdt=xdt, xdt0=xdt0, xdt1=xdt1, e=e, yoff=yoff,
                dec=dec, xdd=xdd, cd=cd, lo=lo, slo=slo)


def _ssd_fwd(name, xbc, dtraw, dtb, arow, dvec, n_heads):
    rows, c = xbc.shape
    q, n = M2_Q, M2_N
    di = n_heads * M2_P
    n_pairs = n_heads // 2
    ppg = n_pairs // M2_G
    nc = rows // q

    def kern(xbc_ref, dtraw_ref, dtb_ref, a_ref, d_ref, y_ref, prev_ref, state, cst, dtt, lastt):
        @pl.when(pl.program_id(0) == 0)
        def _():
            state[...] = jnp.zeros_like(state)

        _ssd_chunk_prep(dtraw_ref, dtb_ref, a_ref, cst, dtt, lastt, n_heads)
        for p in range(n_pairs):
            gi = p // ppg
            sl = slice(p * LANES, (p + 1) * LANES)
            x = xbc_ref[:, sl]
            bm = xbc_ref[:, di + gi * n:di + (gi + 1) * n]
            cm = xbc_ref[:, di + (M2_G + gi) * n:di + (M2_G + gi + 1) * n]
            if p % ppg == 0:
                cb = _bdot(cm, bm, "nt")
            hs = state[p]
            f = _ssd_pair_fwd(x, bm, cm, cb, hs, _pair_tables(cst, dtt, lastt, p))
            ydiag = _bdot(f["m0"], f["xdt0"]) + _bdot(f["m1"], f["xdt1"])
            y_ref[:, sl] = ydiag + f["yoff"] + d_ref[:, sl] * x
            prev_ref[0, p] = hs
            state[p] = f["cd"] * hs + _bdot(f["xdd"], bm, "tn")

    def whole(a):
        return pl.BlockSpec(a.shape, lambda i: (0, 0))

    return pl.pallas_call(
        kern, name=name, grid=(nc,),
        in_specs=[pl.BlockSpec((q, c), lambda i: (i, 0)), pl.BlockSpec((q, LANES), lambda i: (i, 0)),
                  whole(dtb), whole(arow), whole(dvec)],
        out_specs=[pl.BlockSpec((q, di), lambda i: (i, 0)),
                   pl.BlockSpec((1, n_pairs, 2 * M2_P, n), lambda i: (i, 0, 0, 0))],
        out_shape=[jax.ShapeDtypeStruct((rows, di), F32),
                   jax.ShapeDtypeStruct((nc, n_pairs, 2 * M2_P, n), F32)],
        scratch_shapes=[pltpu.VMEM((n_pairs, 2 * M2_P, n), F32), pltpu.VMEM((LANES, q), F32),
                        pltpu.VMEM((LANES, q), F32), pltpu.VMEM((LANES, q), F32)],
        compiler_params=_params("arbitrary"),
    )(xbc, dtraw, dtb, arow, dvec)


def _ssd_bwd(name, xbc, dtraw, dy, prev, dtb, arow, dvec, seg, n_heads):
    rows, c = xbc.shape
    q, n = M2_Q, M2_N
    di = n_heads * M2_P
    n_pairs = n_heads // 2
    ppg = n_pairs // M2_G
    nc = rows // q

    def kern(xbc_ref, dtraw_ref, dy_ref, prev_ref, dtb_ref, a_ref, d_ref, seg_ref,
             dxbc_ref, ddt_ref, da_ref, ddtb_ref, dd_ref,
             dstate, cst, dtt, lastt, dcst, wx, colterm, ddfull):
        step = pl.program_id(0)

        @pl.when(step == 0)
        def _():
            dstate[...] = jnp.zeros_like(dstate)
            ddfull[...] = jnp.zeros_like(ddfull)
            da_ref[...] = jnp.zeros_like(da_ref)
            ddtb_ref[...] = jnp.zeros_like(ddtb_ref)
            dd_ref[...] = jnp.zeros_like(dd_ref)

        dt = _ssd_chunk_prep(dtraw_ref, dtb_ref, a_ref, cst, dtt, lastt, n_heads)
        dcst[...] = jnp.zeros_like(dcst)
        lane_q = lax.broadcasted_iota(jnp.int32, (1, q), 1)
        last_hot = jnp.where(lane_q == q - 1, 1.0, 0.0)

        def total(v):
            return jnp.sum(jnp.sum(v, axis=1, keepdims=True), axis=0, keepdims=True)

        for gi in range(M2_G):
            bm = xbc_ref[:, di + gi * n:di + (gi + 1) * n]
            cm = xbc_ref[:, di + (M2_G + gi) * n:di + (M2_G + gi + 1) * n]
            cb = _bdot(cm, bm, "nt")
            dcb = jnp.zeros((q, q), F32)
            dbm = jnp.zeros((q, n), F32)
            dcm = jnp.zeros((q, n), F32)
            for p in range(gi * ppg, (gi + 1) * ppg):
                sl = slice(p * LANES, (p + 1) * LANES)
                x = xbc_ref[:, sl]
                dyp = dy_ref[:, sl]
                hs = prev_ref[0, p]
                ds = dstate[p]
                f = _ssd_pair_fwd(x, bm, cm, cb, hs, _pair_tables(cst, dtt, lastt, p))
                lo, slo = f["lo"], f["slo"]
                ddfull[:, sl] += _colsum(dyp * x)
                dy0 = jnp.where(lo, dyp, 0.0)
                dy1 = jnp.where(lo, 0.0, dyp)
                dm0 = _bdot(dyp, f["xdt0"], "nt")
                dm1 = _bdot(dyp, f["xdt1"], "nt")
                dxdt = _bdot(f["m0"], dy0, "tn") + _bdot(f["m1"], dy1, "tn")
                dcb = dcb + dm0 * f["l0"] + dm1 * f["l1"]
                w0, w1 = dm0 * f["m0"], dm1 * f["m1"]
                dz = dyp * f["e"]
                dcm = dcm + _bdot(dz, hs)
                dhs = _bdot(dz, cm, "tn") + f["cd"] * ds
                tot = ds * hs * f["cd"]
                dxdd = _bdot(bm, ds, "nt")
                dbm = dbm + _bdot(f["xdd"], ds)
                ee = dxdd * f["xdd"]
                colterm[:, sl] = dyp * f["yoff"] - ee
                dxdt = dxdt + dxdd * f["dec"]
                t_all = total(tot)
                t_lo = total(jnp.where(slo, tot, 0.0))
                e_all = total(ee)
                e_lo = total(jnp.where(lo, ee, 0.0))
                dlast0 = t_lo + e_lo
                dlast1 = (t_all - t_lo) + (e_all - e_lo)
                dcst[2 * p:2 * p + 1, :] = _colsum(w0.T - w0) + dlast0 * last_hot
                dcst[2 * p + 1:2 * p + 2, :] = _colsum(w1.T - w1) + dlast1 * last_hot
                dxbc_ref[:, sl] = d_ref[:, sl] * dyp + dxdt * f["dtp"]
                wx[:, sl] = dxdt * x
                dstate[p] = dhs
            dcm = dcm + _bdot(dcb, bm)
            dbm = dbm + _bdot(dcb, cm, "tn")
            dxbc_ref[:, di + gi * n:di + (gi + 1) * n] = dbm
            dxbc_ref[:, di + (M2_G + gi) * n:di + (M2_G + gi + 1) * n] = dcm

        segv = seg_ref[...]
        dcs = _dot(colterm[...], segv, precision=HIGHEST) + dcst[...].T
        row = lax.broadcasted_iota(jnp.int32, (q, q), 0)
        col = lax.broadcasted_iota(jnp.int32, (q, q), 1)
        ddelta = _dot(jnp.where(col >= row, 1.0, 0.0), dcs, precision=HIGHEST)
        ddt = _dot(wx[...], segv, precision=HIGHEST) + ddelta * a_ref[...]
        da_ref[...] += _colsum(ddelta * dt)
        lane = lax.broadcasted_iota(jnp.int32, (q, LANES), 1)
        ddtraw = jnp.where(lane < n_heads, ddt * _sigmoid(dtraw_ref[...] + dtb_ref[...]), 0.0)
        ddt_ref[...] = ddtraw
        ddtb_ref[...] += _colsum(ddtraw)

        @pl.when(step == nc - 1)
        def _():
            dd_ref[...] = _dot(jnp.broadcast_to(ddfull[...], (8, di)), segv, precision=HIGHEST)

    def whole(a):
        return pl.BlockSpec(a.shape, lambda i: (0, 0))

    def rev(i):
        return nc - 1 - i

    acc = jax.ShapeDtypeStruct((1, LANES), F32)
    acc_spec = pl.BlockSpec((1, LANES), lambda i: (0, 0))
    acc8 = jax.ShapeDtypeStruct((8, LANES), F32)
    acc8_spec = pl.BlockSpec((8, LANES), lambda i: (0, 0))
    return pl.pallas_call(
        kern, name=name, grid=(nc,),
        in_specs=[pl.BlockSpec((q, c), lambda i: (rev(i), 0)), pl.BlockSpec((q, LANES), lambda i: (rev(i), 0)),
                  pl.BlockSpec((q, di), lambda i: (rev(i), 0)),
                  pl.BlockSpec((1, n_pairs, 2 * M2_P, n), lambda i: (rev(i), 0, 0, 0)),
                  whole(dtb), whole(arow), whole(dvec), whole(seg)],
        out_specs=[pl.BlockSpec((q, c), lambda i: (rev(i), 0)), pl.BlockSpec((q, LANES), lambda i: (rev(i), 0)),
                   acc_spec, acc_spec, acc8_spec],
        out_shape=[jax.ShapeDtypeStruct((rows, c), F32), jax.ShapeDtypeStruct((rows, LANES), F32), acc, acc, acc8],
        scratch_shapes=[pltpu.VMEM((n_pairs, 2 * M2_P, n), F32), pltpu.VMEM((LANES, q), F32),
                        pltpu.VMEM((LANES, q), F32), pltpu.VMEM((LANES, q), F32), pltpu.VMEM((LANES, q), F32),
                        pltpu.VMEM((q, di), F32), pltpu.VMEM((q, di), F32), pltpu.VMEM((1, di), F32)],
        compiler_params=_params("arbitrary"),
    )(xbc, dtraw, dy, prev, dtb, arow, dvec, seg)


S5_PARAM_NAMES = ("s5_lambda_re", "s5_lambda_im", "s5_log_dt", "s5_b_re", "s5_b_im", "s5_c_re", "s5_c_im")


def _row(v):
    return v.reshape(1, -1)


class _Rides:
    def __init__(self):
        self.pending = {}
        self.landed = {}

    def matmul(self, site, name, *args, **kw):
        ride = self.pending.pop(site, None)
        res = _matmul(name, *args, ride=ride, **kw)
        if ride is None:
            return res
        self.landed[site] = list(res[-len(ride):])
        res = list(res[:-len(ride)])
        return res[0] if len(res) == 1 else res


def _s5_layer_fwd(tag, x, gate, h, w, j, rides):
    u = rides.matmul("s5_win", tag + "_win", h, w["s5_w_in"][j])
    params = [w[k][j] for k in S5_PARAM_NAMES]
    ops, build_vjp = jax.vjp(_s5_build, *params)
    ug = _s5_to_groups(u.astype(BF16), True)
    yg, sprev = _s5_core_fwd(tag + "_core", ug, ops)
    yy = _s5_from_groups(yg, False)
    skip = _row(w["s5_d"][j])
    gl = _gelu_fwd(tag + "_gelu", yy, u, skip)
    ab = rides.matmul("s5_wglu", tag + "_wglu", gl, w["s5_w_glu"][j])
    x1 = _glu_fwd(tag + "_glu", ab, _row(w["s5_b_glu"][j]), x, gate)
    return x1, dict(u=u, ug=ug, ops=ops, build_vjp=build_vjp, sprev=sprev, yy=yy, gl=gl, ab=ab, skip=skip)


def _s5_layer_bwd(tag, dx1, gate, h, sv, w, j, rides):
    dab, db_glu, dgate = _glu_bwd(tag + "_glu_b", sv["ab"], _row(w["s5_b_glu"][j]), dx1, gate)
    dw_glu = rides.matmul("s5_dwglu", tag + "_dwglu", sv["gl"], dab, "tn", out_dtype=BF16)
    dgl = rides.matmul("s5_dgl", tag + "_dgl", dab, w["s5_w_glu"][j], "nt")
    dyy, dskip = _gelu_bwd(tag + "_gelu_b", sv["yy"], sv["u"], sv["skip"], dgl)
    dug, dkc, dbend, dcpow, da1, da2s = _s5_core_bwd(
        tag + "_core_b", sv["ug"], _s5_to_groups(dyy.astype(BF16), False), sv["sprev"], sv["ops"])
    dparams = sv["build_vjp"]((dkc, dbend, dcpow, da1, _swap_halves(da2s, 2)))
    du = _axpy(tag + "_du", _s5_from_groups(dug, True), dyy, sv["skip"])
    grads = dict(zip(S5_PARAM_NAMES, dparams))
    grads["s5_d"] = dskip.reshape(-1)
    grads["s5_w_in"] = _matmul(tag + "_dwin", h, du, "tn", out_dtype=BF16)
    grads["s5_w_glu"] = dw_glu
    grads["s5_b_glu"] = db_glu.reshape(-1)
    dh = _matmul(tag + "_dh", du, w["s5_w_in"][j], "nt")
    return dh, grads, dgate


def _ssd_consts(w, j, d_model):
    di = 2 * d_model
    heads = di // M2_P

    def pad_row(v):
        return jnp.zeros((1, LANES), F32).at[0, :heads].set(v)

    a = -jnp.exp(w["m2_a_log"][j])
    seg = (jnp.arange(di)[:, None] // M2_P == jnp.arange(LANES)[None, :]).astype(F32)
    w_in = w["m2_w_in"][j]
    conv_dim = di + 2 * M2_G * M2_N
    w_dt = jnp.zeros((d_model, LANES), w_in.dtype).at[:, :heads].set(w_in[:, di + conv_dim:])
    return dict(di=di, heads=heads, conv_dim=conv_dim, a=a, arow=pad_row(a), dtb=pad_row(w["m2_dt_bias"][j]),
                dvec=_row(jnp.repeat(w["m2_d"][j], M2_P)), seg=seg,
                w_z=w_in[:, :di], w_xbc=w_in[:, di:di + conv_dim], w_dt=w_dt,
                conv_w=w["m2_conv_w"][j], conv_b=_row(w["m2_conv_b"][j]), norm_g=_row(w["m2_norm_g"][j]))


def _gated_out_bwd(tag, act, dxo, w_out, gate, **kw):
    dw, dgate_parts = _matmul(tag + "_dwo", act, dxo, "tn", out_dtype=BF16, colscale=gate, colsum_with=w_out, **kw)
    dgate = jnp.sum(dgate_parts, axis=0)
    return dw, dgate, _scale_cols(tag + "_wog", w_out, gate)


def _ssd_layer_fwd(tag, x, gate, h, w, j):
    k = _ssd_consts(w, j, h.shape[1])
    z = _matmul(tag + "_wz", h, k["w_z"])
    xbc_pre = _matmul(tag + "_wxbc", h, k["w_xbc"])
    dtraw = _matmul(tag + "_wdt", h, k["w_dt"])
    xbc = _conv_fwd(tag + "_conv", xbc_pre, k["conv_w"], k["conv_b"])
    y, prev = _ssd_fwd(tag + "_core", xbc, dtraw, k["dtb"], k["arow"], k["dvec"], k["heads"])
    yn = _gatenorm_fwd(tag + "_gn", y, z, k["norm_g"])
    x1 = _matmul(tag + "_wout", yn, w["m2_w_out"][j], colscale=gate, addin=x)
    return x1, dict(k=k, z=z, xbc_pre=xbc_pre, dtraw=dtraw, xbc=xbc, y=y, prev=prev, yn=yn)


def _ssd_layer_bwd(tag, dx1_b, gate, h, sv, w, j):
    k = sv["k"]
    heads = k["heads"]
    dw_out, dgate, wog = _gated_out_bwd(tag, sv["yn"], dx1_b, w["m2_w_out"][j], gate)
    grads = {"m2_w_out": dw_out}
    dyn = _matmul(tag + "_dyn", dx1_b, wog, "nt")
    dyssd, dz, dng = _gatenorm_bwd(tag + "_gn_b", sv["y"], sv["z"], k["norm_g"], dyn)
    dxbc, ddtraw, da, ddtb, dd = _ssd_bwd(tag + "_core_b", sv["xbc"], sv["dtraw"], dyssd, sv["prev"],
                                          k["dtb"], k["arow"], k["dvec"], k["seg"], heads)
    dpre, dcw, dcb = _conv_bwd_pre(tag + "_conv_b1", sv["xbc_pre"], dxbc, k["conv_w"], k["conv_b"])
    dxbc_pre = _conv_bwd_in(tag + "_conv_b2", dpre, k["conv_w"])
    dw_z = _matmul(tag + "_dwz", h, dz, "tn", out_dtype=BF16)
    dw_xbc = _matmul(tag + "_dwxbc", h, dxbc_pre, "tn", out_dtype=BF16)
    dw_dt = _matmul(tag + "_dwdt", h, ddtraw, "tn", out_dtype=BF16)
    dh = _matmul(tag + "_dh1", dz, k["w_z"], "nt")
    dh = _matmul(tag + "_dh2", dxbc_pre, k["w_xbc"], "nt", addin=dh)
    dh = _matmul(tag + "_dh3", ddtraw, k["w_dt"], "nt", addin=dh)
    grads["m2_w_in"] = jnp.concatenate([dw_z, dw_xbc, dw_dt[:, :heads]], axis=1)
    grads["m2_conv_w"] = dcw
    grads["m2_conv_b"] = dcb.reshape(-1)
    grads["m2_dt_bias"] = ddtb[0, :heads]
    grads["m2_a_log"] = da[0, :heads] * k["a"]
    grads["m2_d"] = dd[0, :heads]
    grads["m2_norm_g"] = dng.reshape(-1)
    return dh, grads, dgate


def _layer_fwd(li, x, mod, w, rides, late_weights=None):
    tag = "L%d" % li
    sh1, sc1, g1, sh2, sc2, g2 = mod
    j = li // 2
    h = _normmod_fwd(tag + "_nm1", x, _row(w["norm_mix_g"][li]), sh1, sc1)
    if li % 2 == 0:
        x1, mix = _s5_layer_fwd(tag + "_s5", x, g1, h, w, j, rides)
    else:
        x1, mix = _ssd_layer_fwd(tag + "_m2", x, g1, h, w, j)
    if late_weights is not None:
        w = {**w, **late_weights(rides.landed)}
    h2 = _normmod_fwd(tag + "_nm2", x1, _row(w["norm_mlp_g"][li]), sh2, sc2)
    r = rides.matmul("w1", tag + "_w1", h2, w["mlp_w1"][li], relu=True, out_dtype=BF16)
    x2 = rides.matmul("w2", tag + "_w2", r, w["mlp_w2"][li], square_a=True, colscale=g2, addin=x1)
    return x2, dict(x=x, h=h, mix=mix, x1=x1, h2=h2, r=r), w


def _layer_bwd(li, dx2, dx2_b, sv, mod, w, rides, ride_own_mlp):
    tag = "L%d" % li
    sh1, sc1, g1, sh2, sc2, g2 = mod
    j = li // 2
    dw2, dg2, w2g = _gated_out_bwd(tag + "_mlp", sv["r"], dx2_b, w["mlp_w2"][li], g2, square_a=True)
    dr = rides.matmul("dr", tag + "_dr", dx2_b, w2g, "nt", out_dtype=BF16, mul2=sv["r"])
    dw1 = _matmul(tag + "_dw1", sv["h2"], dr, "tn", out_dtype=BF16)
    grads = {"mlp_w2": dw2, "mlp_w1": dw1}
    dh2 = rides.matmul("dh2", tag + "_dh2", dr, w["mlp_w1"][li], "nt")
    if ride_own_mlp:
        mlp_bufs = _grad_buffers(li, grads, parts=[1])[0]
        rides.pending["s5_dgl"], rides.pending["s5_dwglu"] = mlp_bufs[:1], mlp_bufs[1:]
    dx1, dx1_b, dgm, dsh2, dsc2 = _normmod_bwd(tag + "_nm2_b", sv["x1"], _row(w["norm_mlp_g"][li]), sh2, sc2, dh2,
                                               dx2)
    if li % 2 == 0:
        dh, mix_grads, dg1 = _s5_layer_bwd(tag + "_s5", dx1, g1, sv["h"], sv["mix"], w, j, rides)
    else:
        dh, mix_grads, dg1 = _ssd_layer_bwd(tag + "_m2", dx1_b, g1, sv["h"], sv["mix"], w, j)
    dx, dx_b, dgx, dsh1, dsc1 = _normmod_bwd(tag + "_nm1_b", sv["x"], _row(w["norm_mix_g"][li]), sh1, sc1, dh, dx1)
    grads["norm_mix_g"] = dgx.reshape(-1)
    grads["norm_mlp_g"] = dgm.reshape(-1)
    dmod = jnp.concatenate([dsh1, dsc1, dg1, dsh2, dsc2, dg2], axis=1)
    return dx, dx_b, {**grads, **mix_grads}, dmod


def _layer_parts(li):
    j = li // 2
    mix = [("s5_w_in", j, 0), ("s5_w_glu", j, 1)] if li % 2 == 0 else [("m2_w_in", j, 1), ("m2_w_out", j, 0)]
    return [mix, [("mlp_w1", li, 1), ("mlp_w2", li, 0)]]


def _grad_specs(li):
    j = li // 2
    mlp = [("one", ("mlp_w2", li, 0)), ("one", ("mlp_w1", li, 1))]
    if li % 2 == 0:
        return [[("one", ("s5_w_in", j, 0)), ("one", ("s5_w_glu", j, 1))], mlp]
    packed = [("m2_w_in", j, 1), ("m2_conv_w", j, 1), ("m2_conv_b", j, 0), ("m2_norm_g", j, 0)]
    return [[("one", ("m2_w_out", j, 0)), ("packed", packed)], mlp]


def _grad_buffers(li, grads, parts=(0, 1)):
    out = []
    for k in parts:
        bufs = []
        for kind, entry in _grad_specs(li)[k]:
            if kind == "one":
                bufs.append((grads[entry[0]], "rows" if entry[2] == 0 else "cols"))
            else:
                pieces = [_pack([_chip_slice(grads[n], c, ax) for n, _, ax in entry], BF16) for c in range(N_CHIP)]
                bufs.append((jnp.stack(pieces), "packed"))
        out.append(bufs)
    return out


def _gather_buffers(local, part):
    return [(local[n][i].astype(BF16), "same") for n, i, _ in part]


def _assemble(landed, part):
    out = {}
    for buf, (n, i, ax) in zip(landed, part):
        if ax == 0:
            out[n] = {i: buf.reshape(-1, buf.shape[2])}
        else:
            out[n] = {i: jnp.concatenate([buf[k] for k in range(N_CHIP)], axis=1)}
    return out


def _local_step(x, target, mods, w, local):
    depth = w["norm_mix_g"].shape[0]
    d = x.shape[1]
    rides = _Rides()
    saved, mod_rows, layer_w = [], [], []
    mix0, mlp0 = _layer_parts(0)
    wl = {**w, **_assemble(_exchange4("ag_w_L0", _gather_buffers(local, mix0)), mix0)}
    rides.pending["s5_win"] = _gather_buffers(local, mlp0[:1])
    rides.pending["s5_wglu"] = _gather_buffers(local, mlp0[1:])

    def late_mlp0(landed):
        return _assemble(landed.pop("s5_win") + landed.pop("s5_wglu"), mlp0)

    for li in range(depth):
        if li + 1 < depth:
            nxt = _layer_parts(li + 1)
            rides.pending["w1"] = _gather_buffers(local, nxt[0])
            rides.pending["w2"] = _gather_buffers(local, nxt[1])
        mod = [mods[li:li + 1, i * d:(i + 1) * d] for i in range(N_MOD)]
        mod_rows.append(mod)
        x, sv, wl = _layer_fwd(li, x, mod, wl, rides, late_mlp0 if li == 0 else None)
        saved.append(sv)
        layer_w.append(wl)
        if li + 1 < depth:
            wl = {**w, **_assemble(rides.landed.pop("w1"), nxt[0]), **_assemble(rides.landed.pop("w2"), nxt[1])}
    dx, dx_b, dgf, loss = _loss_head("loss_head", x, target, _row(w["final_norm_g"]))
    layer_grads = [None] * depth
    dmods = [None] * depth
    landed = {}
    for li in reversed(range(depth)):
        dx, dx_b, layer_grads[li], dmods[li] = _layer_bwd(li, dx, dx_b, saved[li], mod_rows[li], layer_w[li], rides,
                                                          li == 0)
        if li + 1 < depth:
            landed[(li + 1, 0)] = rides.landed.pop("dr")
            landed[(li + 1, 1)] = rides.landed.pop("dh2")
        if li > 0:
            rides.pending["dr"], rides.pending["dh2"] = _grad_buffers(li, layer_grads[li])
    landed[(0, 1)] = rides.landed.pop("s5_dgl") + rides.landed.pop("s5_dwglu")
    landed[(0, 0)] = _exchange4("rs_g_L0", _grad_buffers(0, layer_grads[0], parts=[0])[0])
    return loss, dx, layer_grads, dgf.reshape(-1), jnp.concatenate(dmods, axis=0), landed


ANY = pl.BlockSpec(memory_space=pl.ANY)
N_DEV = 8
N_CHIP = 4


def _coords():
    return lax.axis_index("x"), lax.axis_index("y"), lax.axis_index("c")


def _allgather8(name, block):
    r, wd = block.shape

    def body(x_ref, out_ref, send_sems, recv_sems, local_sem):
        x, y, c = _coords()
        me, sibling = (x, y, c), (x, y, 1 - c)
        chips = [(1 - x, y), (x, 1 - y), (1 - x, 1 - y)]

        def slot(px, py, pc):
            return out_ref.at[4 * px + 2 * py + pc]

        def copy(k, blk, to, src=None):
            return pltpu.make_async_remote_copy(
                src_ref=slot(*blk) if src is None else src, dst_ref=slot(*blk),
                send_sem=send_sems.at[k], recv_sem=recv_sems.at[k], device_id=to, device_id_type=MESH)

        mine = pltpu.make_async_copy(x_ref, slot(*me), local_sem)
        mine.start()
        first = [copy(0, me, sibling, src=x_ref)]
        first += [copy(1 + j, me, (*chip, c), src=x_ref) for j, chip in enumerate(chips)]
        for cp in first:
            cp.start()
        passed = [copy(4 + j, (*chip, c), sibling) for j, chip in enumerate(chips)]
        for j, chip in enumerate(chips):
            copy(1 + j, (*chip, c), me).wait_recv()
            passed[j].start()
        copy(0, sibling, me).wait_recv()
        for j, chip in enumerate(chips):
            copy(4 + j, (*chip, 1 - c), me).wait_recv()
        for cp in first + passed:
            cp.wait_send()
        mine.wait()

    return pl.pallas_call(
        body, name=name, in_specs=[ANY], out_specs=ANY,
        out_shape=jax.ShapeDtypeStruct((N_DEV, r, wd), block.dtype),
        scratch_shapes=[pltpu.SemaphoreType.DMA((7,)), pltpu.SemaphoreType.DMA((7,)), pltpu.SemaphoreType.DMA],
    )(block)


def _landing_shape(src, kind):
    if kind == "same":
        return (N_CHIP,) + src.shape
    if kind == "packed":
        return src.shape
    rows, cols = src.shape
    return (N_CHIP, rows // N_CHIP, cols) if kind == "rows" else (N_CHIP, rows, cols // N_CHIP)


def _exchange4_ops(srcs, dsts, send_sems, recv_sems, local_sems, kinds):
    x, y, c = _coords()
    my_chip = 2 * x + y
    chips = [(1 - x, y), (x, 1 - y), (1 - x, 1 - y)]

    def piece(q, k):
        ref, kind = srcs[q], kinds[q]
        if kind == "same":
            return ref
        if kind == "packed":
            return ref.at[k]
        _, rows, cols = dsts[q].shape
        return ref.at[pl.ds(k * rows, rows), :] if kind == "rows" else ref.at[:, pl.ds(k * cols, cols)]

    def copy(q, j, k, slot):
        px, py = chips[j]
        return pltpu.make_async_remote_copy(
            src_ref=piece(q, k), dst_ref=dsts[q].at[slot], send_sem=send_sems.at[3 * q + j],
            recv_sem=recv_sems.at[3 * q + j], device_id=(px, py, c), device_id_type=MESH)

    def mine(q):
        return pltpu.make_async_copy(piece(q, my_chip), dsts[q].at[my_chip], local_sems.at[q])

    def start():
        for q in range(len(srcs)):
            mine(q).start()
            for j, (px, py) in enumerate(chips):
                copy(q, j, 2 * px + py, my_chip).start()

    def wait():
        for q in range(len(srcs)):
            for j, (px, py) in enumerate(chips):
                copy(q, j, my_chip, 2 * px + py).wait_recv()
        for q in range(len(srcs)):
            for j, (px, py) in enumerate(chips):
                copy(q, j, 2 * px + py, my_chip).wait_send()
            mine(q).wait()

    return start, wait


def _exchange_scratch(n):
    return [pltpu.SemaphoreType.DMA((3 * n,)), pltpu.SemaphoreType.DMA((3 * n,)), pltpu.SemaphoreType.DMA((n,))]


def _exchange4(name, buffers):
    n = len(buffers)
    kinds = [kind for _, kind in buffers]

    def body(*refs):
        start, wait = _exchange4_ops(refs[:n], refs[n:2 * n], *refs[2 * n:], kinds)
        start()
        wait()

    return pl.pallas_call(
        body, name=name, in_specs=[ANY] * n, out_specs=[ANY] * n,
        out_shape=[jax.ShapeDtypeStruct(_landing_shape(s, k), s.dtype) for s, k in buffers],
        scratch_shapes=_exchange_scratch(n),
    )(*[s for s, _ in buffers])


def _swap_sibling(name, block):
    def body(x_ref, out_ref, send_sem, recv_sem):
        x, y, c = _coords()
        cp = pltpu.make_async_remote_copy(src_ref=x_ref, dst_ref=out_ref, send_sem=send_sem, recv_sem=recv_sem,
                                          device_id=(x, y, 1 - c), device_id_type=MESH)
        cp.start()
        cp.wait()

    return pl.pallas_call(
        body, name=name, in_specs=[ANY], out_specs=ANY, out_shape=jax.ShapeDtypeStruct(block.shape, block.dtype),
        scratch_shapes=[pltpu.SemaphoreType.DMA, pltpu.SemaphoreType.DMA],
    )(block)


def _sum_slots(name, stacked):
    n, r, wd = stacked.shape
    tile = min(FLAT_ROWS, r)

    def kern(x_ref, o_ref):
        acc = x_ref[0].astype(F32)
        for s in range(1, n):
            acc = acc + x_ref[s].astype(F32)
        o_ref[...] = acc

    return pl.pallas_call(
        kern, name=name, grid=(r // tile,), in_specs=[pl.BlockSpec((n, tile, wd), lambda i: (0, i, 0))],
        out_specs=pl.BlockSpec((tile, wd), lambda i: (i, 0)), out_shape=jax.ShapeDtypeStruct((r, wd), F32),
        compiler_params=_params("parallel"),
    )(stacked)


def _adamw(name, w, m, v, g, g2=None):
    r, wd = w.shape
    grads = [g] if g2 is None else [g, g2]
    c1 = 1.0 - ADAM_B1 ** ADAM_STEP
    c2 = 1.0 - ADAM_B2 ** ADAM_STEP

    def body(i, *refs):
        w_ref, m_ref, v_ref = refs[:3]
        g_refs = refs[3:3 + len(grads)]
        go_ref, d_ref, mo_ref, vo_ref = refs[3 + len(grads):]
        gv = g_refs[0][...]
        if g2 is not None:
            gv = gv + g_refs[1][...]
        mn = ADAM_B1 * m_ref[...] + (1.0 - ADAM_B1) * gv
        vn = ADAM_B2 * v_ref[...] + (1.0 - ADAM_B2) * (gv * gv)
        go_ref[...] = gv
        mo_ref[...] = mn
        vo_ref[...] = vn
        d_ref[...] = -ADAM_LR * ((mn / c1) / (jnp.sqrt(vn / c2) + ADAM_EPS) + ADAM_WD * w_ref[...])

    return _rowcall(name, body, r, FLAT_ROWS, [w, m, v] + grads, [], [(wd, F32)] * 4, [])


FLAT_BLOCK = FLAT_ROWS * FLAT_W


def _pack(arrays, dtype):
    flat = jnp.concatenate([a.reshape(-1).astype(dtype) for a in arrays])
    pad = (-flat.shape[0]) % FLAT_BLOCK
    return jnp.pad(flat, (0, pad)).reshape(-1, FLAT_W)


def _unpack(buf, shapes):
    flat = buf.reshape(-1)
    out, off = [], 0
    for s in shapes:
        n = math.prod(s)
        out.append(flat[off:off + n].reshape(s))
        off += n
    return out


SHARDED_BIG = {"mlp_w1": 2, "mlp_w2": 1, "s5_w_in": 1, "s5_w_glu": 2, "m2_w_in": 2, "m2_w_out": 1}
SHARDED_SMALL = {"m2_conv_w": 2, "m2_conv_b": 1, "m2_norm_g": 1}
REPLICATED = ("ada_b", "norm_mix_g", "norm_mlp_g", "s5_lambda_re", "s5_lambda_im", "s5_log_dt", "s5_b_re",
              "s5_b_im", "s5_c_re", "s5_c_im", "s5_d", "s5_b_glu", "m2_dt_bias", "m2_a_log", "m2_d", "final_norm_g")
WEIGHT_NAMES = ("ada_w", "ada_b", "norm_mix_g", "norm_mlp_g", "mlp_w1", "mlp_w2", "s5_w_in", "s5_lambda_re",
                "s5_lambda_im", "s5_log_dt", "s5_b_re", "s5_b_im", "s5_c_re", "s5_c_im", "s5_d", "s5_w_glu",
                "s5_b_glu", "m2_w_in", "m2_conv_w", "m2_conv_b", "m2_dt_bias", "m2_a_log", "m2_d", "m2_norm_g",
                "m2_w_out", "final_norm_g")


def _gather_weights(name, local, names_axes, dtype):
    names = list(names_axes)
    got = _exchange4(name, [(_pack([local[k] for k in names], dtype), "same")])[0]
    per_chip = [_unpack(got[j], [local[k].shape for k in names]) for j in range(N_CHIP)]
    return {k: jnp.concatenate([per_chip[j][i] for j in range(N_CHIP)], axis=names_axes[k])
            for i, k in enumerate(names)}


def _chip_slice(a, chip, axis):
    size = a.shape[axis] // N_CHIP
    return lax.slice_in_dim(a, chip * size, (chip + 1) * size, axis=axis)


def kernel(x, c, ada_w, ada_b, norm_mix_g, norm_mlp_g, mlp_w1, mlp_w2, s5_w_in, s5_lambda_re, s5_lambda_im, s5_log_dt, s5_b_re, s5_b_im, s5_c_re, s5_c_im, s5_d, s5_w_glu, s5_b_glu, m2_w_in, m2_conv_w, m2_conv_b, m2_dt_bias, m2_a_log, m2_d, m2_norm_g, m2_w_out, final_norm_g, loss_target, m_ada_w, m_ada_b, m_norm_mix_g, m_norm_mlp_g, m_mlp_w1, m_mlp_w2, m_s5_w_in, m_s5_lambda_re, m_s5_lambda_im, m_s5_log_dt, m_s5_b_re, m_s5_b_im, m_s5_c_re, m_s5_c_im, m_s5_d, m_s5_w_glu, m_s5_b_glu, m_m2_w_in, m_m2_conv_w, m_m2_conv_b, m_m2_dt_bias, m_m2_a_log, m_m2_d, m_m2_norm_g, m_m2_w_out, m_final_norm_g, v_ada_w, v_ada_b, v_norm_mix_g, v_norm_mlp_g, v_mlp_w1, v_mlp_w2, v_s5_w_in, v_s5_lambda_re, v_s5_lambda_im, v_s5_log_dt, v_s5_b_re, v_s5_b_im, v_s5_c_re, v_s5_c_im, v_s5_d, v_s5_w_glu, v_s5_b_glu, v_m2_w_in, v_m2_conv_w, v_m2_conv_b, v_m2_dt_bias, v_m2_a_log, v_m2_d, v_m2_norm_g, v_m2_w_out, v_final_norm_g):
    args = locals()
    local = {k: args[k] for k in WEIGHT_NAMES}
    mom_m = {k: args["m_" + k] for k in WEIGHT_NAMES}
    mom_v = {k: args["v_" + k] for k in WEIGHT_NAMES}
    depth, d = norm_mix_g.shape
    xi, yi, ci = _coords()
    my_chip = 2 * xi + yi
    my_dev = 2 * my_chip + ci

    cond = jax.nn.silu(c).reshape(-1, LANES)
    cond_all = _allgather8("ag_cond", cond).reshape(N_DEV, d)
    cond_pad = jnp.zeros((LANES, d), F32).at[:N_DEV].set(cond_all)
    mod_cols = ada_w.shape[2]
    mod_part = jnp.stack([_matmul("ada_%d" % i, cond_pad, ada_w[i])[:N_DEV] for i in range(depth)])
    mod_all = _allgather8("ag_mod", mod_part.reshape(-1, LANES)).reshape(N_CHIP, 2, depth, N_DEV, mod_cols)[:, 0]
    mod_mine = lax.dynamic_index_in_dim(mod_all, my_dev, axis=2, keepdims=False)
    mods = jnp.transpose(mod_mine, (1, 0, 2)).reshape(depth, N_CHIP * mod_cols) + ada_b

    w = {k: local[k] for k in REPLICATED}
    w.update(_gather_weights("ag_w_small", local, SHARDED_SMALL, F32))

    loss_row, dx, layer_grads, g_final, dmods, landed = _local_step(x[0], loss_target[0], mods, w, local)
    grads = {"ada_b": dmods, "final_norm_g": g_final}
    for k in REPLICATED[1:-1]:
        grads[k] = jnp.stack([g[k] for g in layer_grads if k in g])

    rep_shapes = [grads[k].shape for k in REPLICATED]
    rep_all = _allgather8("ag_grep", _pack([grads[k] for k in REPLICATED], F32))
    rep_sum = _sum_slots("sum_grep", rep_all)
    dmods_all = rep_all.reshape(N_DEV, -1)[:, :dmods.size].reshape(N_DEV, depth, N_CHIP * mod_cols)

    dm_mine = lax.dynamic_slice_in_dim(dmods_all, my_chip * mod_cols, mod_cols, axis=2)
    dm_pad = jnp.zeros((LANES, depth, mod_cols), F32).at[:N_DEV].set(dm_mine)
    g_ada_w = jnp.stack([_matmul("dada_%d" % i, cond_pad, dm_pad[:, i], "tn") for i in range(depth)])

    red = {}
    for li in range(depth):
        for k, part in enumerate(_grad_specs(li)):
            for (kind, entry), land in zip(part, landed[(li, k)]):
                if kind == "one":
                    red[entry[:2]] = _sum_slots("sum_%s_%d" % entry[:2], land)
                else:
                    total = _sum_slots("sum_packed_L%d" % li, land)
                    shapes = [local[n][i].shape for n, i, _ in entry]
                    red.update({(n, i): v for (n, i, _), v in zip(entry, _unpack(total, shapes))})

    out_g, out_d, out_m, out_v = {}, {}, {}, {}
    for name in list(SHARDED_BIG) + list(SHARDED_SMALL):
        shape = local[name].shape
        flat2 = (-1, shape[-1])
        g1 = jnp.stack([red[(name, i)] for i in range(shape[0])]).reshape(flat2)
        g2 = _swap_sibling("swap_" + name, g1)
        res = _adamw("adam_" + name, local[name].reshape(flat2), mom_m[name].reshape(flat2),
                     mom_v[name].reshape(flat2), g1, g2)
        for dst, buf in zip((out_g, out_d, out_m, out_v), res):
            dst[name] = buf.reshape(shape)
    res = _adamw("adam_rep", _pack([local[k] for k in REPLICATED], F32), _pack([mom_m[k] for k in REPLICATED], F32),
                 _pack([mom_v[k] for k in REPLICATED], F32), rep_sum)
    for dst, buf in zip((out_g, out_d, out_m, out_v), res):
        dst.update(zip(REPLICATED, _unpack(buf, rep_shapes)))
    flat2 = (-1, mod_cols)
    res = _adamw("adam_ada", ada_w.reshape(flat2), m_ada_w.reshape(flat2), v_ada_w.reshape(flat2),
                 g_ada_w.reshape(flat2))
    for dst, buf in zip((out_g, out_d, out_m, out_v), res):
        dst["ada_w"] = buf.reshape(ada_w.shape)

    loss = lax.psum(loss_row[0, 0], ("x", "y", "c"))
    outs = [loss, dx[None]]
    for dst in (out_g, out_d, out_m, out_v):
        outs += [dst[k] for k in WEIGHT_NAMES]
    return tuple(outs)
```

```python
import math

import jax
import jax.numpy as jnp
from jax import lax
from jax.experimental import pallas as pl
from jax.experimental.pallas import tpu as pltpu

F32 = jnp.float32
BF16 = jnp.bfloat16
HIGHEST = lax.Precision.HIGHEST

NORM_EPS = 1e-5
N_MOD = 6
S5_H, S5_P, S5_T = 16, 64, 64
M2_P, M2_N, M2_G, M2_Q, M2_K = 64, 128, 4, 128, 4
LANES = 128
ADAM_LR, ADAM_B1, ADAM_B2, ADAM_EPS, ADAM_WD, ADAM_STEP = 0.001, 0.9, 0.999, 1e-08, 0.01, 10
VMEM_LIMIT_BYTES = 56 * 1024 * 1024
ROW_TILE = 256
FLAT_W = 1024
FLAT_ROWS = 256
MESH = pl.DeviceIdType.MESH


def _params(*sem):
    return pltpu.CompilerParams(dimension_semantics=sem, vmem_limit_bytes=VMEM_LIMIT_BYTES)


def _dot(a, b, dn="nn", precision=None):
    dims = {"nn": ((1,), (0,)), "nt": ((1,), (1,)), "tn": ((0,), (0,))}[dn]
    return lax.dot_general(a, b, (dims, ((), ())), preferred_element_type=F32, precision=precision)


def _bdot(a, b, dn="nn"):
    return _dot(a.astype(BF16), b.astype(BF16), dn)


def _sigmoid(x):
    return jax.nn.sigmoid(x)


def _colsum(x):
    return jnp.sum(x, axis=0, keepdims=True)


def _pick_tile(dim, want):
    if dim <= want:
        return dim
    for t in range(want - want % LANES, 0, -LANES):
        if dim % t == 0:
            return t
    raise ValueError((dim, want))


MATMUL_TILE = 1024
MATMUL_VMEM_BUDGET = 40 * 1024 * 1024


def _matmul_tiles(m, n, k, mode, in_bytes, out_bytes):
    tn = _pick_tile(n, MATMUL_TILE)
    k_tiles = [k] + [t for t in (4096, 2048, 1024) if t < k and k % t == 0]
    m_tiles = [_pick_tile(m, MATMUL_TILE)] + ([512] if mode != "tn" and m % 512 == 0 and m > 512 else [])
    for tk in k_tiles:
        for tm in m_tiles:
            blocks = 2 * (tm * tk * in_bytes[0] + tk * tn * in_bytes[1] + tm * tn * out_bytes)
            if blocks + (4 * tm * tn if tk < k else 0) <= MATMUL_VMEM_BUDGET:
                return tm, tn, tk
    raise ValueError((m, n, k))


def _matmul(name, a, b, mode="nn", out_dtype=F32, relu=False, square_a=False, mul2=None, colscale=None,
            addin=None, colsum_with=None, ride=None):
    if mode == "nn":
        (m, k), (k2, n) = a.shape, b.shape
    elif mode == "nt":
        (m, k), (n, k2) = a.shape, b.shape
    else:
        (k, m), (k2, n) = a.shape, b.shape
    assert k == k2, (name, a.shape, b.shape)
    tiles = [e for e in (mul2, addin, colsum_with) if e is not None]
    out_bytes = jnp.dtype(out_dtype).itemsize + sum(e.dtype.itemsize for e in tiles)
    tm, tn, tk = _matmul_tiles(m, n, k, mode, (a.dtype.itemsize, b.dtype.itemsize), out_bytes)
    nk = k // tk
    n_ext = len(tiles) + (colscale is not None)
    n_out = 1 + (colsum_with is not None)
    n_ride = 0 if ride is None else len(ride)
    grid = (m // tm, n // tn, nk)

    def kern(*refs):
        a_ref, b_ref = refs[:2]
        e_refs = list(refs[2:2 + n_ext])
        o_refs = refs[2 + n_ext + n_ride:2 + n_ext + n_ride + n_out]
        kk = pl.program_id(2)
        if ride is not None:
            here = [pl.program_id(ax) for ax in range(3)]
            land0 = 2 + n_ext + n_ride + n_out
            ride_refs = (refs[2 + n_ext:2 + n_ext + n_ride], refs[land0:land0 + n_ride]) + tuple(refs[-3:])
            ride_kinds = [kind for _, kind in ride]
            first = (here[0] == 0) & (here[1] == 0) & (here[2] == 0)
            last = (here[0] == grid[0] - 1) & (here[1] == grid[1] - 1) & (here[2] == grid[2] - 1)

            @pl.when(first)
            def _():
                _exchange4_ops(*ride_refs, ride_kinds)[0]()

        av = a_ref[...]
        if square_a:
            av = av * av
        part = _bdot(av, b_ref[...], mode)

        def finish(r):
            ext = list(e_refs)
            m2v = ext.pop(0)[...].astype(F32) if mul2 is not None else None
            addv = ext.pop(0)[...].astype(F32) if addin is not None else None
            if colsum_with is not None:
                o_refs[1][0] = _colsum(r * ext.pop(0)[...].astype(F32))
            if relu:
                r = jnp.maximum(r, 0.0)
            if m2v is not None:
                r = r * (2.0 * m2v)
            if colscale is not None:
                r = r * ext.pop(0)[...]
            if addv is not None:
                r = r + addv
            o_refs[0][...] = r.astype(out_dtype)

        if nk == 1:
            finish(part)
        else:
            acc = refs[-4] if ride is not None else refs[-1]

            @pl.when(kk == 0)
            def _():
                acc[...] = part

            @pl.when(kk > 0)
            def _():
                acc[...] += part

            @pl.when(kk == nk - 1)
            def _():
                finish(acc[...])

        if ride is not None:
            @pl.when(last)
            def _():
                _exchange4_ops(*ride_refs, ride_kinds)[1]()

    if mode == "tn":
        a_spec = pl.BlockSpec((tk, tm), lambda i, j, kk: (kk, i))
    else:
        a_spec = pl.BlockSpec((tm, tk), lambda i, j, kk: (i, kk))
    if mode == "nt":
        b_spec = pl.BlockSpec((tn, tk), lambda i, j, kk: (j, kk))
    else:
        b_spec = pl.BlockSpec((tk, tn), lambda i, j, kk: (kk, j))
    o_spec = pl.BlockSpec((tm, tn), lambda i, j, kk: (i, j))
    in_specs = [a_spec, b_spec] + [o_spec] * len(tiles)
    operands = [a, b] + tiles
    if colscale is not None:
        in_specs.append(pl.BlockSpec((1, tn), lambda i, j, kk: (0, j)))
        operands.append(colscale)
    out_specs = [o_spec]
    out_shape = [jax.ShapeDtypeStruct((m, n), out_dtype)]
    if colsum_with is not None:
        out_specs.append(pl.BlockSpec((1, 1, tn), lambda i, j, kk: (i, 0, j)))
        out_shape.append(jax.ShapeDtypeStruct((m // tm, 1, n), F32))
    scratch = [pltpu.VMEM((tm, tn), F32)] if nk > 1 else []
    semantics = ("parallel", "parallel", "arbitrary")
    if ride is not None:
        for src, kind in ride:
            in_specs.append(pl.BlockSpec(memory_space=pl.ANY))
            operands.append(src)
            out_specs.append(pl.BlockSpec(memory_space=pl.ANY))
            out_shape.append(jax.ShapeDtypeStruct(_landing_shape(src, kind), src.dtype))
        scratch += _exchange_scratch(n_ride)
        semantics = ("arbitrary", "arbitrary", "arbitrary")
    res = pl.pallas_call(
        kern, name=name, grid=grid, in_specs=in_specs, out_specs=out_specs, out_shape=out_shape,
        scratch_shapes=scratch, compiler_params=_params(*semantics),
    )(*operands)
    return res if len(res) > 1 else res[0]


def _rowcall(name, body, rows, tile, row_ins, small_ins, row_outs, acc_outs):
    tile = min(tile, rows)
    assert rows % tile == 0, (name, rows, tile)
    n_in = len(row_ins) + len(small_ins)

    def kern(*refs):
        i = pl.program_id(0)
        accs = refs[n_in + len(row_outs):]

        @pl.when(i == 0)
        def _():
            for acc in accs:
                acc[...] = jnp.zeros_like(acc)

        body(i, *refs)

    def whole(shape):
        return pl.BlockSpec(shape, lambda i, nd=len(shape): (0,) * nd)

    in_specs = [pl.BlockSpec((tile, a.shape[1]), lambda i: (i, 0)) for a in row_ins]
    in_specs += [whole(a.shape) for a in small_ins]
    out_specs = [pl.BlockSpec((tile, w), lambda i: (i, 0)) for (w, _) in row_outs]
    out_specs += [whole(s) for s in acc_outs]
    out_shape = [jax.ShapeDtypeStruct((rows, w), dt) for (w, dt) in row_outs]
    out_shape += [jax.ShapeDtypeStruct(s, F32) for s in acc_outs]
    return pl.pallas_call(
        kern, name=name, grid=(rows // tile,), in_specs=in_specs, out_specs=out_specs, out_shape=out_shape,
        compiler_params=_params("arbitrary"),
    )(*row_ins, *small_ins)


def _rms(x):
    r = lax.rsqrt(jnp.mean(x * x, axis=-1, keepdims=True) + NORM_EPS)
    return x * r, r


def _rms_bwd(dxhat, xhat, r):
    return r * (dxhat - xhat * jnp.mean(dxhat * xhat, axis=-1, keepdims=True))


def _normmod_fwd(name, x, g, sh, sc):
    def body(i, x_ref, g_ref, sh_ref, sc_ref, o_ref):
        xhat, _ = _rms(x_ref[...])
        o_ref[...] = ((xhat * g_ref[...]) * (1.0 + sc_ref[...]) + sh_ref[...]).astype(BF16)

    return _rowcall(name, body, x.shape[0], ROW_TILE, [x], [g, sh, sc], [(x.shape[1], BF16)], [])[0]


def _normmod_bwd(name, x, g, sh, sc, dh, dx_pass):
    d = x.shape[1]

    def body(i, x_ref, dh_ref, dxp_ref, g_ref, sh_ref, sc_ref, dx_ref, dxb_ref, dg_ref, dsh_ref, dsc_ref):
        xhat, r = _rms(x_ref[...])
        dh = dh_ref[...].astype(F32)
        gv = g_ref[...]
        dn = dh * (1.0 + sc_ref[...])
        dsc_ref[...] += _colsum(dh * (xhat * gv))
        dsh_ref[...] += _colsum(dh)
        dg_ref[...] += _colsum(dn * xhat)
        dx = dxp_ref[...] + _rms_bwd(dn * gv, xhat, r)
        dx_ref[...] = dx
        dxb_ref[...] = dx.astype(BF16)

    return _rowcall(name, body, x.shape[0], ROW_TILE, [x, dh, dx_pass], [g, sh, sc], [(d, F32), (d, BF16)],
                    [(1, d), (1, d), (1, d)])


def _scale_cols(name, w, g):
    def body(i, w_ref, g_ref, o_ref):
        o_ref[...] = (w_ref[...].astype(F32) * g_ref[...]).astype(BF16)

    return _rowcall(name, body, w.shape[0], ROW_TILE, [w], [g], [(w.shape[1], BF16)], [])[0]


GELU_K = math.sqrt(2.0 / math.pi)
GELU_C = 0.044715


def _gelu_fwd(name, y, u, skip):
    def body(i, y_ref, u_ref, s_ref, o_ref):
        v = y_ref[...].astype(F32) + s_ref[...] * u_ref[...]
        t = jnp.tanh(GELU_K * (v + GELU_C * (v * v * v)))
        o_ref[...] = (0.5 * v * (1.0 + t)).astype(BF16)

    return _rowcall(name, body, y.shape[0], ROW_TILE, [y, u], [skip], [(y.shape[1], BF16)], [])[0]


def _gelu_bwd(name, y, u, skip, dgl):
    d = y.shape[1]

    def body(i, y_ref, u_ref, d_ref, s_ref, o_ref, ds_ref):
        uv = u_ref[...]
        v = y_ref[...].astype(F32) + s_ref[...] * uv
        t = jnp.tanh(GELU_K * (v + GELU_C * (v * v * v)))
        dv = d_ref[...] * (0.5 * (1.0 + t) + 0.5 * v * (1.0 - t * t) * (GELU_K * (1.0 + 3.0 * GELU_C * v * v)))
        o_ref[...] = dv
        ds_ref[...] += _colsum(dv * uv)

    return _rowcall(name, body, y.shape[0], ROW_TILE, [y, u, dgl], [skip], [(d, F32)], [(1, d)])


def _axpy(name, a, b, scale):
    def body(i, a_ref, b_ref, s_ref, o_ref):
        o_ref[...] = (a_ref[...].astype(F32) + s_ref[...] * b_ref[...]).astype(BF16)

    return _rowcall(name, body, a.shape[0], ROW_TILE, [a, b], [scale], [(a.shape[1], BF16)], [])[0]


def _glu_fwd(name, ab, bias, x, gate):
    d = ab.shape[1] // 2

    def body(i, ab_ref, x_ref, b_ref, g_ref, o_ref):
        v = ab_ref[:, :d] + b_ref[:, :d]
        gt = ab_ref[:, d:] + b_ref[:, d:]
        o_ref[...] = x_ref[...] + g_ref[...] * (v * _sigmoid(gt))

    return _rowcall(name, body, ab.shape[0], ROW_TILE, [ab, x], [bias, gate], [(d, F32)], [])[0]


def _glu_bwd(name, ab, bias, dxo, gate):
    d = ab.shape[1] // 2

    def body(i, ab_ref, dx_ref, b_ref, g_ref, dab_ref, db_ref, dg_ref):
        v = ab_ref[:, :d] + b_ref[:, :d]
        s = _sigmoid(ab_ref[:, d:] + b_ref[:, d:])
        dxo_v = dx_ref[...]
        dg_ref[...] += _colsum(dxo_v * (v * s))
        do = g_ref[...] * dxo_v
        dv = do * s
        dgt = do * v * (s * (1.0 - s))
        dab_ref[:, :d] = dv.astype(BF16)
        dab_ref[:, d:] = dgt.astype(BF16)
        db_ref[:, :d] += _colsum(dv)
        db_ref[:, d:] += _colsum(dgt)

    return _rowcall(name, body, ab.shape[0], ROW_TILE, [ab, dxo], [bias, gate], [(2 * d, BF16)],
                    [(1, 2 * d), (1, d)])


def _gatenorm_fwd(name, y, z, ng):
    di = y.shape[1]
    gw = di // M2_G

    def body(i, y_ref, z_ref, g_ref, o_ref):
        for gi in range(M2_G):
            sl = slice(gi * gw, (gi + 1) * gw)
            zz = z_ref[:, sl]
            y2 = y_ref[:, sl] * (zz * _sigmoid(zz))
            yh, _ = _rms(y2)
            o_ref[:, sl] = (yh * g_ref[:, sl]).astype(BF16)

    return _rowcall(name, body, y.shape[0], ROW_TILE, [y, z], [ng], [(di, BF16)], [])[0]


def _gatenorm_bwd(name, y, z, ng, dyn):
    di = y.shape[1]
    gw = di // M2_G

    def body(i, y_ref, z_ref, d_ref, g_ref, dy_ref, dz_ref, dg_ref):
        for gi in range(M2_G):
            sl = slice(gi * gw, (gi + 1) * gw)
            zz = z_ref[:, sl]
            yy = y_ref[:, sl]
            s = _sigmoid(zz)
            sz = zz * s
            yh, r = _rms(yy * sz)
            dn = d_ref[:, sl]
            dg_ref[:, sl] += _colsum(dn * yh)
            dy2 = _rms_bwd(dn * g_ref[:, sl], yh, r)
            dy_ref[:, sl] = dy2 * sz
            dz_ref[:, sl] = (dy2 * yy * (s * (1.0 + zz * (1.0 - s)))).astype(BF16)

    return _rowcall(name, body, y.shape[0], ROW_TILE, [y, z, dyn], [ng], [(di, F32), (di, BF16)], [(1, di)])


def _loss_head(name, x, target, g):
    d = x.shape[1]

    def body(i, x_ref, t_ref, g_ref, dx_ref, dxb_ref, dg_ref, loss_ref):
        xhat, r = _rms(x_ref[...])
        gv = g_ref[...]
        err = xhat * gv - t_ref[...]
        per_row = jnp.sum(err * err, axis=-1, keepdims=True) * (0.5 / d)
        loss_ref[...] += jnp.broadcast_to(_colsum(per_row), loss_ref.shape)
        dy = err * (1.0 / d)
        dg_ref[...] += _colsum(dy * xhat)
        dx = _rms_bwd(dy * gv, xhat, r)
        dx_ref[...] = dx
        dxb_ref[...] = dx.astype(BF16)

    return _rowcall(name, body, x.shape[0], ROW_TILE, [x, target], [g], [(d, F32), (d, BF16)],
                    [(1, d), (1, LANES)])


HALO = 8


def _halo_call(name, body, rows, tile, width, mains, halo_of, halo_next, smalls, row_outs, acc_outs, scratch):
    tile = min(tile, rows)
    nb = tile // HALO
    last = rows // HALO - 1
    n_in = len(mains) + 1 + len(smalls)

    def kern(*refs):
        i = pl.program_id(0)
        accs = refs[n_in + len(row_outs):n_in + len(row_outs) + len(acc_outs)]

        @pl.when(i == 0)
        def _():
            for acc in accs:
                acc[...] = jnp.zeros_like(acc)

        body(i, *refs)

    def whole(shape):
        return pl.BlockSpec(shape, lambda i, nd=len(shape): (0,) * nd)

    if halo_next:
        halo_spec = pl.BlockSpec((HALO, width), lambda i: (jnp.minimum((i + 1) * nb, last), 0))
    else:
        halo_spec = pl.BlockSpec((HALO, width), lambda i: (jnp.maximum(i * nb - 1, 0), 0))
    in_specs = [pl.BlockSpec((tile, a.shape[1]), lambda i: (i, 0)) for a in mains] + [halo_spec]
    in_specs += [whole(a.shape) for a in smalls]
    out_specs = [pl.BlockSpec((tile, w), lambda i: (i, 0)) for (w, _) in row_outs] + [whole(s) for s in acc_outs]
    out_shape = [jax.ShapeDtypeStruct((rows, w), dt) for (w, dt) in row_outs]
    out_shape += [jax.ShapeDtypeStruct(s, F32) for s in acc_outs]
    return pl.pallas_call(
        kern, name=name, grid=(rows // tile,), in_specs=in_specs, out_specs=out_specs, out_shape=out_shape,
        scratch_shapes=scratch, compiler_params=_params("arbitrary"),
    )(*mains, mains[halo_of], *smalls)


CONV_TILE = 128
CONV_ROWS = 16
CONV_STRIP = 512


def _conv_blocks(tile, c, strip=CONV_STRIP):
    strip = strip if c % strip == 0 else LANES
    rb = min(CONV_ROWS, tile)
    return [(r0, rb, slice(c0, c0 + strip)) for c0 in range(0, c, strip) for r0 in range(0, tile, rb)]


def _shifted_windows(base, rb, offsets):
    n = base.shape[0]
    out = []
    for o in offsets:
        if o % HALO == 0:
            out.append(base[o:o + rb, :])
        else:
            out.append(pltpu.roll(base, n - o, 0)[0:rb, :])
    return out


def _conv_windows(x_ref, ext, r0, rb, sl):
    base = ext[:, sl] if r0 == 0 else x_ref[r0 - HALO:r0 + rb, sl]
    return _shifted_windows(base, rb, [HALO - 3 + k for k in range(M2_K)])


def _conv_fwd(name, xin, w, b):
    rows, c = xin.shape
    tile = min(CONV_TILE, rows)
    rb0 = min(CONV_ROWS, tile)

    def body(i, x_ref, h_ref, w_ref, b_ref, o_ref, ext):
        ext[0:HALO, :] = jnp.where(i == 0, 0.0, h_ref[...])
        ext[HALO:, :] = x_ref[0:rb0, :]
        for r0, rb, sl in _conv_blocks(tile, c):
            taps = _conv_windows(x_ref, ext, r0, rb, sl)
            pre = b_ref[:, sl] + w_ref[0:1, sl] * taps[0]
            for k in range(1, M2_K):
                pre = pre + w_ref[k:k + 1, sl] * taps[k]
            o_ref[r0:r0 + rb, sl] = pre * _sigmoid(pre)

    return _halo_call(name, body, rows, tile, c, [xin], 0, False, [w, b], [(c, F32)], [],
                      [pltpu.VMEM((rb0 + HALO, c), F32)])[0]


def _conv_bwd_pre(name, xin, dout, w, b):
    rows, c = xin.shape
    tile = min(CONV_TILE, rows)
    rb0 = min(CONV_ROWS, tile)

    def body(i, x_ref, d_ref, h_ref, w_ref, b_ref, dp_ref, dw_ref, db_ref, ext):
        ext[0:HALO, :] = jnp.where(i == 0, 0.0, h_ref[...])
        ext[HALO:, :] = x_ref[0:rb0, :]
        sums = {}
        for r0, rb, sl in _conv_blocks(tile, c, CONV_STRIP // 2):
            taps = _conv_windows(x_ref, ext, r0, rb, sl)
            pre = b_ref[:, sl] + w_ref[0:1, sl] * taps[0]
            for k in range(1, M2_K):
                pre = pre + w_ref[k:k + 1, sl] * taps[k]
            s = _sigmoid(pre)
            dp = d_ref[r0:r0 + rb, sl] * (s * (1.0 + pre * (1.0 - s)))
            dp_ref[r0:r0 + rb, sl] = dp
            part = [dp] + [dp * taps[k] for k in range(M2_K)]
            key = sl.start
            sums[key] = part if key not in sums else [p + q for p, q in zip(sums[key], part)]
            if r0 + rb == tile:
                db_ref[:, sl] += _colsum(sums[key][0])
                for k in range(M2_K):
                    dw_ref[k:k + 1, sl] += _colsum(sums[key][1 + k])

    return _halo_call(name, body, rows, tile, c, [xin, dout], 0, False, [w, b], [(c, F32)], [(M2_K, c), (1, c)],
                      [pltpu.VMEM((rb0 + HALO, c), F32)])


def _conv_bwd_in(name, dpre, w):
    rows, c = dpre.shape
    tile = min(CONV_TILE, rows)
    n_tiles = rows // tile
    rb0 = min(CONV_ROWS, tile)

    def body(i, d_ref, h_ref, w_ref, o_ref, ext):
        ext[0:rb0, :] = d_ref[tile - rb0:tile, :]
        ext[rb0:, :] = jnp.where(i == n_tiles - 1, 0.0, h_ref[...])
        for r0, rb, sl in _conv_blocks(tile, c):
            base = ext[:, sl] if r0 + rb == tile else d_ref[r0:r0 + rb + HALO, sl]
            wins = _shifted_windows(base, rb, [3 - k for k in range(M2_K)])
            acc = w_ref[0:1, sl] * wins[0]
            for k in range(1, M2_K):
                acc = acc + w_ref[k:k + 1, sl] * wins[k]
            o_ref[r0:r0 + rb, sl] = acc.astype(BF16)

    return _halo_call(name, body, rows, tile, c, [dpre], 0, True, [w], [(c, BF16)], [],
                      [pltpu.VMEM((rb0 + HALO, c), F32)])[0]


def _s5_build(lam_re, lam_im, log_dt, b_re, b_im, c_re, c_im):
    g, p = lam_re.shape
    h = b_re.shape[-1]
    t = S5_T
    dt = jnp.exp(log_dt)[:, None]
    ld_re, ld_im = lam_re * dt, lam_im * dt
    tau = jnp.arange(t + 1, dtype=F32)
    mag = jnp.exp(ld_re[:, :, None] * tau)
    ang = ld_im[:, :, None] * tau
    pw_re, pw_im = mag * jnp.cos(ang), mag * jnp.sin(ang)
    num_re, num_im = pw_re[:, :, 1] - 1.0, pw_im[:, :, 1]
    den = lam_re * lam_re + lam_im * lam_im
    q_re = (num_re * lam_re + num_im * lam_im) / den
    q_im = (num_im * lam_re - num_re * lam_im) / den
    bb_re = q_re[:, :, None] * b_re - q_im[:, :, None] * b_im
    bb_im = q_re[:, :, None] * b_im + q_im[:, :, None] * b_re
    bbt_re, bbt_im = jnp.transpose(bb_re, (0, 2, 1)), jnp.transpose(bb_im, (0, 2, 1))
    ct_re, ct_im = jnp.transpose(c_re, (0, 2, 1)), jnp.transpose(c_im, (0, 2, 1))
    lane = jnp.arange(t * h)
    rep_tau = (lane[None, :] // h == jnp.arange(t)[:, None]).astype(F32)
    tile_h = (lane[None, :] % h == jnp.arange(h)[:, None]).astype(F32)

    def spread(x, m):
        return jnp.einsum("gpk,kn->gpn", x, m, precision=HIGHEST)

    c_re_n, c_im_n = spread(ct_re, tile_h), spread(ct_im, tile_h)

    def c_times_pw(first):
        pr, pi = spread(pw_re[:, :, first:first + t], rep_tau), spread(pw_im[:, :, first:first + t], rep_tau)
        return pr * c_re_n - pi * c_im_n, pr * c_im_n + pi * c_re_n

    cp0_re, cp0_im = c_times_pw(0)
    cp1_re, cp1_im = c_times_pw(1)
    kc = (jnp.einsum("ghp,gpn->ghn", bbt_re, cp0_re, precision=HIGHEST)
          - jnp.einsum("ghp,gpn->ghn", bbt_im, cp0_im, precision=HIGHEST))
    bp_re = jnp.transpose(pw_re[:, :, t - 1::-1][:, :, :t], (0, 2, 1))
    bp_im = jnp.transpose(pw_im[:, :, t - 1::-1][:, :, :t], (0, 2, 1))
    be_re = bbt_re[:, :, None, :] * bp_re[:, None, :, :] - bbt_im[:, :, None, :] * bp_im[:, None, :, :]
    be_im = bbt_re[:, :, None, :] * bp_im[:, None, :, :] + bbt_im[:, :, None, :] * bp_re[:, None, :, :]
    bend = jnp.concatenate([be_re, be_im], axis=-1).reshape(g, h * t, 2 * p)
    cpow = jnp.concatenate([cp1_re, -cp1_im], axis=1)
    at_re, at_im = pw_re[:, :, t], pw_im[:, :, t]
    a1 = jnp.concatenate([at_re, at_re], axis=-1)[:, None, :]
    a2 = jnp.concatenate([-at_im, at_im], axis=-1)[:, None, :]
    return kc, bend, cpow, a1, a2


def _swap_halves(x, axis):
    n = x.shape[axis] // 2
    lo = lax.slice_in_dim(x, 0, n, axis=axis)
    hi = lax.slice_in_dim(x, n, 2 * n, axis=axis)
    return jnp.concatenate([hi, lo], axis=axis)


def _group_spec(shape):
    return pl.BlockSpec((1,) + tuple(shape[1:]), lambda g: (g, 0, 0))


S5_ROWS = 8


def _s5_expand_toeplitz(kc_ref, ext, toep):
    t, th = S5_T, S5_T * S5_H
    ext[:, th:] = jnp.zeros((S5_ROWS, LANES), F32)
    for hin in range(S5_H):
        ext[:, :th] = jnp.broadcast_to(kc_ref[0, hin:hin + 1, :], (S5_ROWS, th))
        rolled = pltpu.roll(ext[...], 0, 1, stride=S5_H, stride_axis=0)
        tiles = []
        for q in range(t // S5_ROWS):
            if q == 0:
                tiles.append(rolled[:, :th])
            else:
                tiles.append(jnp.concatenate([jnp.zeros((S5_ROWS, q * LANES), F32), rolled[:, :th - q * LANES]],
                                             axis=1))
        toep[hin * t:(hin + 1) * t, :] = jnp.concatenate(tiles, axis=0).astype(BF16)


def _s5_core_fwd(name, u, ops):
    kc, bend, cpow, a1, a2 = ops
    g, nc, th = u.shape
    p2 = bend.shape[-1]
    bend_b, cpow_b = bend.astype(BF16), cpow.astype(BF16)
    a2s = _swap_halves(a2, 2)

    def kern(u_ref, k_ref, b_ref, c_ref, a1_ref, a2_ref, a2s_ref, y_ref, sp_ref, x_scr, xs_scr, ext, toep):
        _s5_expand_toeplitz(k_ref, ext, toep)
        ub = u_ref[0].astype(BF16)
        xv = _dot(ub, b_ref[0])
        x_scr[...] = xv
        xs_scr[...] = pltpu.roll(xv, p2 // 2, 1)
        a1v, a2v, a2sv = a1_ref[0], a2_ref[0], a2s_ref[0]

        def step(c, carry):
            s, ss = carry
            sp_ref[0, pl.ds(c, 1), :] = s
            s_new = a1v * s + a2v * ss + x_scr[pl.ds(c, 1), :]
            ss_new = a1v * ss + a2sv * s + xs_scr[pl.ds(c, 1), :]
            return s_new, ss_new

        zero = jnp.zeros((1, p2), F32)
        lax.fori_loop(0, nc, step, (zero, zero))
        y_ref[0] = (_dot(ub, toep[...]) + _dot(sp_ref[0].astype(BF16), c_ref[0])).astype(BF16)

    ins = [u, kc, bend_b, cpow_b, a1, a2, a2s]
    return pl.pallas_call(
        kern, name=name, grid=(g,), in_specs=[_group_spec(a.shape) for a in ins],
        out_specs=[_group_spec((g, nc, th)), _group_spec((g, nc, p2))],
        out_shape=[jax.ShapeDtypeStruct((g, nc, th), BF16), jax.ShapeDtypeStruct((g, nc, p2), F32)],
        scratch_shapes=[pltpu.VMEM((nc, p2), F32), pltpu.VMEM((nc, p2), F32),
                        pltpu.VMEM((S5_ROWS, th + LANES), F32), pltpu.VMEM((th, th), BF16)],
        compiler_params=_params("arbitrary"),
    )(*ins)


def _s5_core_bwd(name, u, dy, sprev, ops):
    kc, bend, cpow, a1, a2 = ops
    g, nc, th = u.shape
    t = S5_T
    p2 = bend.shape[-1]
    bend_b, cpow_b = bend.astype(BF16), cpow.astype(BF16)
    a2s = _swap_halves(a2, 2)
    idx = jnp.arange(th)
    flip = (idx[:, None] // t == idx[None, :] // t) & (idx[:, None] % t == t - 1 - idx[None, :] % t)
    flip = flip.astype(BF16)

    def kern(u_ref, dy_ref, sp_ref, k_ref, b_ref, c_ref, a1_ref, a2_ref, a2s_ref, f_ref,
             du_ref, dk_ref, db_ref, dc_ref, da1_ref, da2_ref, g_scr, gs_scr, dx_scr, ext, toep, dtoep):
        _s5_expand_toeplitz(k_ref, ext, toep)
        ub, dyb = u_ref[0].astype(BF16), dy_ref[0].astype(BF16)
        dtoep[...] = _dot(_dot(ub, f_ref[...]).astype(BF16), dyb, "tn")
        n_q = t // S5_ROWS
        width = th + LANES
        for hin in range(S5_H):
            folded = dtoep[hin * t + (n_q - 1) * S5_ROWS:(hin + 1) * t, :]
            for qp in range(n_q - 1):
                q = n_q - 1 - qp
                tile = dtoep[hin * t + qp * S5_ROWS:hin * t + (qp + 1) * S5_ROWS, :]
                folded = folded + jnp.concatenate([tile[:, q * LANES:], jnp.zeros((S5_ROWS, q * LANES), F32)],
                                                  axis=1)
            ext[:, :th] = folded
            rolled = pltpu.roll(ext[...], 0, 1, stride=S5_H, stride_axis=0)
            rolled = pltpu.roll(rolled, width - S5_H * (S5_ROWS - 1), 1)
            dk_ref[0, hin:hin + 1, :] = _colsum(rolled)[:, :th]
        spv = sp_ref[0]
        dc_ref[0] = _dot(spv.astype(BF16), dyb, "tn")
        gv = _dot(dyb, c_ref[0], "nt")
        g_scr[...] = gv
        gs_scr[...] = pltpu.roll(gv, p2 // 2, 1)
        a1v, a2v, a2sv = a1_ref[0], a2_ref[0], a2s_ref[0]

        def step(k, carry):
            gr, grs, da1, da2 = carry
            c = nc - 1 - k
            dx_scr[pl.ds(c, 1), :] = gr
            s_in = sp_ref[0, pl.ds(c, 1), :]
            da1 = da1 + gr * s_in
            da2 = da2 + grs * s_in
            gr_new = g_scr[pl.ds(c, 1), :] + a1v * gr + a2sv * grs
            grs_new = gs_scr[pl.ds(c, 1), :] + a1v * grs + a2v * gr
            return gr_new, grs_new, da1, da2

        zero = jnp.zeros((1, p2), F32)
        _, _, da1, da2 = lax.fori_loop(0, nc, step, (zero, zero, zero, zero))
        da1_ref[0] = da1
        da2_ref[0] = da2
        dxb = dx_scr[...].astype(BF16)
        db_ref[0] = _dot(ub, dxb, "tn")
        du_ref[0] = (_dot(dyb, toep[...], "nt") + _dot(dxb, b_ref[0], "nt")).astype(BF16)

    ins = [u, dy, sprev, kc, bend_b, cpow_b, a1, a2, a2s]
    outs = [(g, nc, th), (g, S5_H, th), (g, th, p2), (g, p2, th), (g, 1, p2), (g, 1, p2)]
    out_types = [BF16] + [F32] * (len(outs) - 1)
    return pl.pallas_call(
        kern, name=name, grid=(g,),
        in_specs=[_group_spec(a.shape) for a in ins] + [pl.BlockSpec((th, th), lambda gi: (0, 0))],
        out_specs=[_group_spec(s) for s in outs],
        out_shape=[jax.ShapeDtypeStruct(s, dt) for s, dt in zip(outs, out_types)],
        scratch_shapes=[pltpu.VMEM((nc, p2), F32), pltpu.VMEM((nc, p2), F32), pltpu.VMEM((nc, p2), F32),
                        pltpu.VMEM((S5_ROWS, th + LANES), F32), pltpu.VMEM((th, th), BF16),
                        pltpu.VMEM((th, th), F32)],
        compiler_params=_params("arbitrary"),
    )(*ins, flip)


def _s5_to_groups(u, channel_major):
    rows, w = u.shape
    g = w // S5_H
    nc = rows // S5_T
    perm = (2, 0, 3, 1) if channel_major else (2, 0, 1, 3)
    return u.reshape(nc, S5_T, g, S5_H).transpose(perm).reshape(g, nc, S5_T * S5_H)


def _s5_from_groups(y, channel_major):
    g, nc, _ = y.shape
    if channel_major:
        return y.reshape(g, nc, S5_H, S5_T).transpose(1, 3, 0, 2).reshape(nc * S5_T, g * S5_H)
    return y.reshape(g, nc, S5_T, S5_H).transpose(1, 2, 0, 3).reshape(nc * S5_T, g * S5_H)


def _softplus(x):
    return jnp.maximum(x, 0.0) + jnp.log(1.0 + jnp.exp(-jnp.abs(x)))


def _ssd_chunk_prep(dtraw_ref, dtb_ref, a_ref, cst, dtt, lastt, n_heads):
    q = M2_Q
    lane = lax.broadcasted_iota(jnp.int32, (q, LANES), 1)
    dt = jnp.where(lane < n_heads, _softplus(dtraw_ref[...] + dtb_ref[...]), 0.0)
    adt = dt * a_ref[...]
    row = lax.broadcasted_iota(jnp.int32, (q, q), 0)
    col = lax.broadcasted_iota(jnp.int32, (q, q), 1)
    cs = _dot(jnp.where(row >= col, 1.0, 0.0), adt, precision=HIGHEST)
    cst[...] = cs.T
    dtt[...] = dt.T
    lastt[...] = jnp.broadcast_to(_colsum(adt), (q, LANES)).T
    return dt


def _pair_tables(cst, dtt, lastt, p):
    q = M2_Q
    out = []
    for hh in (2 * p, 2 * p + 1):
        rc = jnp.broadcast_to(cst[hh:hh + 1, :], (q, q))
        cc = rc.T
        dtc = jnp.broadcast_to(dtt[hh:hh + 1, :], (q, q)).T
        lb = jnp.broadcast_to(lastt[hh:hh + 1, :], (q, q))
        out.append((rc, cc, dtc, lb))
    return out


def _ssd_pair_fwd(x, bm, cm, cb, hs, tabs):
    q = M2_Q
    row = lax.broadcasted_iota(jnp.int32, (q, q), 0)
    col = lax.broadcasted_iota(jnp.int32, (q, q), 1)
    causal = row >= col
    lo = col < M2_P
    slo = row < M2_P
    (rc0, cc0, dtc0, lb0), (rc1, cc1, dtc1, lb1) = tabs
    l0 = jnp.where(causal, jnp.exp(jnp.where(causal, cc0 - rc0, 0.0)), 0.0)
    l1 = jnp.where(causal, jnp.exp(jnp.where(causal, cc1 - rc1, 0.0)), 0.0)
    m0, m1 = cb * l0, cb * l1
    dtp = jnp.where(lo, dtc0, dtc1)
    xdt = x * dtp
    xdt0 = jnp.where(lo, xdt, 0.0)
    xdt1 = jnp.where(lo, 0.0, xdt)
    e = jnp.where(lo, jnp.exp(cc0), jnp.exp(cc1))
    z = _bdot(cm, hs, "nt")
    yoff = z * e
    dec = jnp.where(lo, jnp.exp(lb0 - cc0), jnp.exp(lb1 - cc1))
    xdd = xdt * dec
    cd = jnp.where(slo, jnp.exp(lb0), jnp.exp(lb1))
    return dict(l0=l0, l1=l1, m0=m0, m1=m1, dtp=dtp, xdt=xdt, xdt0=xdt0, xdt1=xdt1, e=e, yoff=yoff,
                dec=dec, xdd=xdd, cd=cd, lo=lo, slo=slo)


def _ssd_fwd(name, xbc, dtraw, dtb, arow, dvec, n_heads):
    rows, c = xbc.shape
    q, n = M2_Q, M2_N
    di = n_heads * M2_P
    n_pairs = n_heads // 2
    ppg = n_pairs // M2_G
    nc = rows // q

    def kern(xbc_ref, dtraw_ref, dtb_ref, a_ref, d_ref, y_ref, prev_ref, state, cst, dtt, lastt):
        @pl.when(pl.program_id(0) == 0)
        def _():
            state[...] = jnp.zeros_like(state)

        _ssd_chunk_prep(dtraw_ref, dtb_ref, a_ref, cst, dtt, lastt, n_heads)
        for p in range(n_pairs):
            gi = p // ppg
            sl = slice(p * LANES, (p + 1) * LANES)
            x = xbc_ref[:, sl]
            bm = xbc_ref[:, di + gi * n:di + (gi + 1) * n]
            cm = xbc_ref[:, di + (M2_G + gi) * n:di + (M2_G + gi + 1) * n]
            if p % ppg == 0:
                cb = _bdot(cm, bm, "nt")
            hs = state[p]
            f = _ssd_pair_fwd(x, bm, cm, cb, hs, _pair_tables(cst, dtt, lastt, p))
            ydiag = _bdot(f["m0"], f["xdt0"]) + _bdot(f["m1"], f["xdt1"])
            y_ref[:, sl] = ydiag + f["yoff"] + d_ref[:, sl] * x
            prev_ref[0, p] = hs
            state[p] = f["cd"] * hs + _bdot(f["xdd"], bm, "tn")

    def whole(a):
        return pl.BlockSpec(a.shape, lambda i: (0, 0))

    return pl.pallas_call(
        kern, name=name, grid=(nc,),
        in_specs=[pl.BlockSpec((q, c), lambda i: (i, 0)), pl.BlockSpec((q, LANES), lambda i: (i, 0)),
                  whole(dtb), whole(arow), whole(dvec)],
        out_specs=[pl.BlockSpec((q, di), lambda i: (i, 0)),
                   pl.BlockSpec((1, n_pairs, 2 * M2_P, n), lambda i: (i, 0, 0, 0))],
        out_shape=[jax.ShapeDtypeStruct((rows, di), F32),
                   jax.ShapeDtypeStruct((nc, n_pairs, 2 * M2_P, n), F32)],
        scratch_shapes=[pltpu.VMEM((n_pairs, 2 * M2_P, n), F32), pltpu.VMEM((LANES, q), F32),
                        pltpu.VMEM((LANES, q), F32), pltpu.VMEM((LANES, q), F32)],
        compiler_params=_params("arbitrary"),
    )(xbc, dtraw, dtb, arow, dvec)


def _ssd_bwd(name, xbc, dtraw, dy, prev, dtb, arow, dvec, seg, n_heads):
    rows, c = xbc.shape
    q, n = M2_Q, M2_N
    di = n_heads * M2_P
    n_pairs = n_heads // 2
    ppg = n_pairs // M2_G
    nc = rows // q

    def kern(xbc_ref, dtraw_ref, dy_ref, prev_ref, dtb_ref, a_ref, d_ref, seg_ref,
             dxbc_ref, ddt_ref, da_ref, ddtb_ref, dd_ref,
             dstate, cst, dtt, lastt, dcst, wx, colterm, ddfull):
        step = pl.program_id(0)

        @pl.when(step == 0)
        def _():
            dstate[...] = jnp.zeros_like(dstate)
            ddfull[...] = jnp.zeros_like(ddfull)
            da_ref[...] = jnp.zeros_like(da_ref)
            ddtb_ref[...] = jnp.zeros_like(ddtb_ref)
            dd_ref[...] = jnp.zeros_like(dd_ref)

        dt = _ssd_chunk_prep(dtraw_ref, dtb_ref, a_ref, cst, dtt, lastt, n_heads)
        dcst[...] = jnp.zeros_like(dcst)
        lane_q = lax.broadcasted_iota(jnp.int32, (1, q), 1)
        last_hot = jnp.where(lane_q == q - 1, 1.0, 0.0)

        def total(v):
            return jnp.sum(jnp.sum(v, axis=1, keepdims=True), axis=0, keepdims=True)

        for gi in range(M2_G):
            bm = xbc_ref[:, di + gi * n:di + (gi + 1) * n]
            cm = xbc_ref[:, di + (M2_G + gi) * n:di + (M2_G + gi + 1) * n]
            cb = _bdot(cm, bm, "nt")
            dcb = jnp.zeros((q, q), F32)
            dbm = jnp.zeros((q, n), F32)
            dcm = jnp.zeros((q, n), F32)
            for p in range(gi * ppg, (gi + 1) * ppg):
                sl = slice(p * LANES, (p + 1) * LANES)
                x = xbc_ref[:, sl]
                dyp = dy_ref[:, sl]
                hs = prev_ref[0, p]
                ds = dstate[p]
                f = _ssd_pair_fwd(x, bm, cm, cb, hs, _pair_tables(cst, dtt, lastt, p))
                lo, slo = f["lo"], f["slo"]
                ddfull[:, sl] += _colsum(dyp * x)
                dy0 = jnp.where(lo, dyp, 0.0)
                dy1 = jnp.where(lo, 0.0, dyp)
                dm0 = _bdot(dyp, f["xdt0"], "nt")
                dm1 = _bdot(dyp, f["xdt1"], "nt")
                dxdt = _bdot(f["m0"], dy0, "tn") + _bdot(f["m1"], dy1, "tn")
                dcb = dcb + dm0 * f["l0"] + dm1 * f["l1"]
                w0, w1 = dm0 * f["m0"], dm1 * f["m1"]
                dz = dyp * f["e"]
                dcm = dcm + _bdot(dz, hs)
                dhs = _bdot(dz, cm, "tn") + f["cd"] * ds
                tot = ds * hs * f["cd"]
                dxdd = _bdot(bm, ds, "nt")
                dbm = dbm + _bdot(f["xdd"], ds)
                ee = dxdd * f["xdd"]
                colterm[:, sl] = dyp * f["yoff"] - ee
                dxdt = dxdt + dxdd * f["dec"]
                t_all = total(tot)
                t_lo = total(jnp.where(slo, tot, 0.0))
                e_all = total(ee)
                e_lo = total(jnp.where(lo, ee, 0.0))
                dlast0 = t_lo + e_lo
                dlast1 = (t_all - t_lo) + (e_all - e_lo)
                dcst[2 * p:2 * p + 1, :] = _colsum(w0.T - w0) + dlast0 * last_hot
                dcst[2 * p + 1:2 * p + 2, :] = _colsum(w1.T - w1) + dlast1 * last_hot
                dxbc_ref[:, sl] = d_ref[:, sl] * dyp + dxdt * f["dtp"]
                wx[:, sl] = dxdt * x
                dstate[p] = dhs
            dcm = dcm + _bdot(dcb, bm)
            dbm = dbm + _bdot(dcb, cm, "tn")
            dxbc_ref[:, di + gi * n:di + (gi + 1) * n] = dbm
            dxbc_ref[:, di + (M2_G + gi) * n:di + (M2_G + gi + 1) * n] = dcm

        segv = seg_ref[...]
        dcs = _dot(colterm[...], segv, precision=HIGHEST) + dcst[...].T
        row = lax.broadcasted_iota(jnp.int32, (q, q), 0)
        col = lax.broadcasted_iota(jnp.int32, (q, q), 1)
        ddelta = _dot(jnp.where(col >= row, 1.0, 0.0), dcs, precision=HIGHEST)
        ddt = _dot(wx[...], segv, precision=HIGHEST) + ddelta * a_ref[...]
        da_ref[...] += _colsum(ddelta * dt)
        lane = lax.broadcasted_iota(jnp.int32, (q, LANES), 1)
        ddtraw = jnp.where(lane < n_heads, ddt * _sigmoid(dtraw_ref[...] + dtb_ref[...]), 0.0)
        ddt_ref[...] = ddtraw
        ddtb_ref[...] += _colsum(ddtraw)

        @pl.when(step == nc - 1)
        def _():
            dd_ref[...] = _dot(jnp.broadcast_to(ddfull[...], (8, di)), segv, precision=HIGHEST)

    def whole(a):
        return pl.BlockSpec(a.shape, lambda i: (0, 0))

    def rev(i):
        return nc - 1 - i

    acc = jax.ShapeDtypeStruct((1, LANES), F32)
    acc_spec = pl.BlockSpec((1, LANES), lambda i: (0, 0))
    acc8 = jax.ShapeDtypeStruct((8, LANES), F32)
    acc8_spec = pl.BlockSpec((8, LANES), lambda i: (0, 0))
    return pl.pallas_call(
        kern, name=name, grid=(nc,),
        in_specs=[pl.BlockSpec((q, c), lambda i: (rev(i), 0)), pl.BlockSpec((q, LANES), lambda i: (rev(i), 0)),
                  pl.BlockSpec((q, di), lambda i: (rev(i), 0)),
                  pl.BlockSpec((1, n_pairs, 2 * M2_P, n), lambda i: (rev(i), 0, 0, 0)),
                  whole(dtb), whole(arow), whole(dvec), whole(seg)],
        out_specs=[pl.BlockSpec((q, c), lambda i: (rev(i), 0)), pl.BlockSpec((q, LANES), lambda i: (rev(i), 0)),
                   acc_spec, acc_spec, acc8_spec],
        out_shape=[jax.ShapeDtypeStruct((rows, c), F32), jax.ShapeDtypeStruct((rows, LANES), F32), acc, acc, acc8],
        scratch_shapes=[pltpu.VMEM((n_pairs, 2 * M2_P, n), F32), pltpu.VMEM((LANES, q), F32),
                        pltpu.VMEM((LANES, q), F32), pltpu.VMEM((LANES, q), F32), pltpu.VMEM((LANES, q), F32),
                        pltpu.VMEM((q, di), F32), pltpu.VMEM((q, di), F32), pltpu.VMEM((1, di), F32)],
        compiler_params=_params("arbitrary"),
    )(xbc, dtraw, dy, prev, dtb, arow, dvec, seg)


S5_PARAM_NAMES = ("s5_lambda_re", "s5_lambda_im", "s5_log_dt", "s5_b_re", "s5_b_im", "s5_c_re", "s5_c_im")


def _row(v):
    return v.reshape(1, -1)


class _Rides:
    def __init__(self):
        self.pending = {}
        self.landed = {}

    def matmul(self, site, name, *args, **kw):
        ride = self.pending.pop(site, None)
        res = _matmul(name, *args, ride=ride, **kw)
        if ride is None:
            return res
        self.landed[site] = list(res[-len(ride):])
        res = list(res[:-len(ride)])
        return res[0] if len(res) == 1 else res


def _s5_layer_fwd(tag, x, gate, h, w, j, rides):
    u = rides.matmul("s5_win", tag + "_win", h, w["s5_w_in"][j])
    params = [w[k][j] for k in S5_PARAM_NAMES]
    ops, build_vjp = jax.vjp(_s5_build, *params)
    ug = _s5_to_groups(u.astype(BF16), True)
    yg, sprev = _s5_core_fwd(tag + "_core", ug, ops)
    yy = _s5_from_groups(yg, False)
    skip = _row(w["s5_d"][j])
    gl = _gelu_fwd(tag + "_gelu", yy, u, skip)
    ab = rides.matmul("s5_wglu", tag + "_wglu", gl, w["s5_w_glu"][j])
    x1 = _glu_fwd(tag + "_glu", ab, _row(w["s5_b_glu"][j]), x, gate)
    return x1, dict(u=u, ug=ug, ops=ops, build_vjp=build_vjp, sprev=sprev, yy=yy, gl=gl, ab=ab, skip=skip)


def _s5_layer_bwd(tag, dx1, gate, h, sv, w, j, rides):
    dab, db_glu, dgate = _glu_bwd(tag + "_glu_b", sv["ab"], _row(w["s5_b_glu"][j]), dx1, gate)
    dw_glu = rides.matmul("s5_dwglu", tag + "_dwglu", sv["gl"], dab, "tn", out_dtype=BF16)
    dgl = rides.matmul("s5_dgl", tag + "_dgl", dab, w["s5_w_glu"][j], "nt")
    dyy, dskip = _gelu_bwd(tag + "_gelu_b", sv["yy"], sv["u"], sv["skip"], dgl)
    dug, dkc, dbend, dcpow, da1, da2s = _s5_core_bwd(
        tag + "_core_b", sv["ug"], _s5_to_groups(dyy.astype(BF16), False), sv["sprev"], sv["ops"])
    dparams = sv["build_vjp"]((dkc, dbend, dcpow, da1, _swap_halves(da2s, 2)))
    du = _axpy(tag + "_du", _s5_from_groups(dug, True), dyy, sv["skip"])
    grads = dict(zip(S5_PARAM_NAMES, dparams))
    grads["s5_d"] = dskip.reshape(-1)
    grads["s5_w_in"] = _matmul(tag + "_dwin", h, du, "tn", out_dtype=BF16)
    grads["s5_w_glu"] = dw_glu
    grads["s5_b_glu"] = db_glu.reshape(-1)
    dh = _matmul(tag + "_dh", du, w["s5_w_in"][j], "nt")
    return dh, grads, dgate


def _ssd_consts(w, j, d_model):
    di = 2 * d_model
    heads = di // M2_P

    def pad_row(v):
        return jnp.zeros((1, LANES), F32).at[0, :heads].set(v)

    a = -jnp.exp(w["m2_a_log"][j])
    seg = (jnp.arange(di)[:, None] // M2_P == jnp.arange(LANES)[None, :]).astype(F32)
    w_in = w["m2_w_in"][j]
    conv_dim = di + 2 * M2_G * M2_N
    w_dt = jnp.zeros((d_model, LANES), w_in.dtype).at[:, :heads].set(w_in[:, di + conv_dim:])
    return dict(di=di, heads=heads, conv_dim=conv_dim, a=a, arow=pad_row(a), dtb=pad_row(w["m2_dt_bias"][j]),
                dvec=_row(jnp.repeat(w["m2_d"][j], M2_P)), seg=seg,
                w_z=w_in[:, :di], w_xbc=w_in[:, di:di + conv_dim], w_dt=w_dt,
                conv_w=w["m2_conv_w"][j], conv_b=_row(w["m2_conv_b"][j]), norm_g=_row(w["m2_norm_g"][j]))


def _gated_out_bwd(tag, act, dxo, w_out, gate, **kw):
    dw, dgate_parts = _matmul(tag + "_dwo", act, dxo, "tn", out_dtype=BF16, colscale=gate, colsum_with=w_out, **kw)
    dgate = jnp.sum(dgate_parts, axis=0)
    return dw, dgate, _scale_cols(tag + "_wog", w_out, gate)


def _ssd_layer_fwd(tag, x, gate, h, w, j):
    k = _ssd_consts(w, j, h.shape[1])
    z = _matmul(tag + "_wz", h, k["w_z"])
    xbc_pre = _matmul(tag + "_wxbc", h, k["w_xbc"])
    dtraw = _matmul(tag + "_wdt", h, k["w_dt"])
    xbc = _conv_fwd(tag + "_conv", xbc_pre, k["conv_w"], k["conv_b"])
    y, prev = _ssd_fwd(tag + "_core", xbc, dtraw, k["dtb"], k["arow"], k["dvec"], k["heads"])
    yn = _gatenorm_fwd(tag + "_gn", y, z, k["norm_g"])
    x1 = _matmul(tag + "_wout", yn, w["m2_w_out"][j], colscale=gate, addin=x)
    return x1, dict(k=k, z=z, xbc_pre=xbc_pre, dtraw=dtraw, xbc=xbc, y=y, prev=prev, yn=yn)


def _ssd_layer_bwd(tag, dx1_b, gate, h, sv, w, j):
    k = sv["k"]
    heads = k["heads"]
    dw_out, dgate, wog = _gated_out_bwd(tag, sv["yn"], dx1_b, w["m2_w_out"][j], gate)
    grads = {"m2_w_out": dw_out}
    dyn = _matmul(tag + "_dyn", dx1_b, wog, "nt")
    dyssd, dz, dng = _gatenorm_bwd(tag + "_gn_b", sv["y"], sv["z"], k["norm_g"], dyn)
    dxbc, ddtraw, da, ddtb, dd = _ssd_bwd(tag + "_core_b", sv["xbc"], sv["dtraw"], dyssd, sv["prev"],
                                          k["dtb"], k["arow"], k["dvec"], k["seg"], heads)
    dpre, dcw, dcb = _conv_bwd_pre(tag + "_conv_b1", sv["xbc_pre"], dxbc, k["conv_w"], k["conv_b"])
    dxbc_pre = _conv_bwd_in(tag + "_conv_b2", dpre, k["conv_w"])
    dw_z = _matmul(tag + "_dwz", h, dz, "tn", out_dtype=BF16)
    dw_xbc = _matmul(tag + "_dwxbc", h, dxbc_pre, "tn", out_dtype=BF16)
    dw_dt = _matmul(tag + "_dwdt", h, ddtraw, "tn", out_dtype=BF16)
    dh = _matmul(tag + "_dh1", dz, k["w_z"], "nt")
    dh = _matmul(tag + "_dh2", dxbc_pre, k["w_xbc"], "nt", addin=dh)
    dh = _matmul(tag + "_dh3", ddtraw, k["w_dt"], "nt", addin=dh)
    grads["m2_w_in"] = jnp.concatenate([dw_z, dw_xbc, dw_dt[:, :heads]], axis=1)
    grads["m2_conv_w"] = dcw
    grads["m2_conv_b"] = dcb.reshape(-1)
    grads["m2_dt_bias"] = ddtb[0, :heads]
    grads["m2_a_log"] = da[0, :heads] * k["a"]
    grads["m2_d"] = dd[0, :heads]
    grads["m2_norm_g"] = dng.reshape(-1)
    return dh, grads, dgate


def _layer_fwd(li, x, mod, w, rides, late_weights=None):
    tag = "L%d" % li
    sh1, sc1, g1, sh2, sc2, g2 = mod
    j = li // 2
    h = _normmod_fwd(tag + "_nm1", x, _row(w["norm_mix_g"][li]), sh1, sc1)
    if li % 2 == 0:
        x1, mix = _s5_layer_fwd(tag + "_s5", x, g1, h, w, j, rides)
    else:
        x1, mix = _ssd_layer_fwd(tag + "_m2", x, g1, h, w, j)
    if late_weights is not None:
        w = {**w, **late_weights(rides.landed)}
    h2 = _normmod_fwd(tag + "_nm2", x1, _row(w["norm_mlp_g"][li]), sh2, sc2)
    r = rides.matmul("w1", tag + "_w1", h2, w["mlp_w1"][li], relu=True, out_dtype=BF16)
    x2 = rides.matmul("w2", tag + "_w2", r, w["mlp_w2"][li], square_a=True, colscale=g2, addin=x1)
    return x2, dict(x=x, h=h, mix=mix, x1=x1, h2=h2, r=r), w


def _layer_bwd(li, dx2, dx2_b, sv, mod, w, rides, ride_own_mlp):
    tag = "L%d" % li
    sh1, sc1, g1, sh2, sc2, g2 = mod
    j = li // 2
    dw2, dg2, w2g = _gated_out_bwd(tag + "_mlp", sv["r"], dx2_b, w["mlp_w2"][li], g2, square_a=True)
    dr = rides.matmul("dr", tag + "_dr", dx2_b, w2g, "nt", out_dtype=BF16, mul2=sv["r"])
    dw1 = _matmul(tag + "_dw1", sv["h2"], dr, "tn", out_dtype=BF16)
    grads = {"mlp_w2": dw2, "mlp_w1": dw1}
    dh2 = rides.matmul("dh2", tag + "_dh2", dr, w["mlp_w1"][li], "nt")
    if ride_own_mlp:
        mlp_bufs = _grad_buffers(li, grads, parts=[1])[0]
        rides.pending["s5_dgl"], rides.pending["s5_dwglu"] = mlp_bufs[:1], mlp_bufs[1:]
    dx1, dx1_b, dgm, dsh2, dsc2 = _normmod_bwd(tag + "_nm2_b", sv["x1"], _row(w["norm_mlp_g"][li]), sh2, sc2, dh2,
                                               dx2)
    if li % 2 == 0:
        dh, mix_grads, dg1 = _s5_layer_bwd(tag + "_s5", dx1, g1, sv["h"], sv["mix"], w, j, rides)
    else:
        dh, mix_grads, dg1 = _ssd_layer_bwd(tag + "_m2", dx1_b, g1, sv["h"], sv["mix"], w, j)
    dx, dx_b, dgx, dsh1, dsc1 = _normmod_bwd(tag + "_nm1_b", sv["x"], _row(w["norm_mix_g"][li]), sh1, sc1, dh, dx1)
    grads["norm_mix_g"] = dgx.reshape(-1)
    grads["norm_mlp_g"] = dgm.reshape(-1)
    dmod = jnp.concatenate([dsh1, dsc1, dg1, dsh2, dsc2, dg2], axis=1)
    return dx, dx_b, {**grads, **mix_grads}, dmod


def _layer_parts(li):
    j = li // 2
    mix = [("s5_w_in", j, 0), ("s5_w_glu", j, 1)] if li % 2 == 0 else [("m2_w_in", j, 1), ("m2_w_out", j, 0)]
    return [mix, [("mlp_w1", li, 1), ("mlp_w2", li, 0)]]


def _grad_specs(li):
    j = li // 2
    mlp = [("one", ("mlp_w2", li, 0)), ("one", ("mlp_w1", li, 1))]
    if li % 2 == 0:
        return [[("one", ("s5_w_in", j, 0)), ("one", ("s5_w_glu", j, 1))], mlp]
    packed = [("m2_w_in", j, 1), ("m2_conv_w", j, 1), ("m2_conv_b", j, 0), ("m2_norm_g", j, 0)]
    return [[("one", ("m2_w_out", j, 0)), ("packed", packed)], mlp]


def _grad_buffers(li, grads, parts=(0, 1)):
    out = []
    for k in parts:
        bufs = []
        for kind, entry in _grad_specs(li)[k]:
            if kind == "one":
                bufs.append((grads[entry[0]], "rows" if entry[2] == 0 else "cols"))
            else:
                pieces = [_pack([_chip_slice(grads[n], c, ax) for n, _, ax in entry], BF16) for c in range(N_CHIP)]
                bufs.append((jnp.stack(pieces), "packed"))
        out.append(bufs)
    return out


def _gather_buffers(local, part):
    return [(local[n][i].astype(BF16), "same") for n, i, _ in part]


def _assemble(landed, part):
    out = {}
    for buf, (n, i, ax) in zip(landed, part):
        if ax == 0:
            out[n] = {i: buf.reshape(-1, buf.shape[2])}
        else:
            out[n] = {i: jnp.concatenate([buf[k] for k in range(N_CHIP)], axis=1)}
    return out


def _local_step(x, target, mods, w, local):
    depth = w["norm_mix_g"].shape[0]
    d = x.shape[1]
    rides = _Rides()
    saved, mod_rows, layer_w = [], [], []
    mix0, mlp0 = _layer_parts(0)
    wl = {**w, **_assemble(_exchange4("ag_w_L0", _gather_buffers(local, mix0)), mix0)}
    rides.pending["s5_win"] = _gather_buffers(local, mlp0[:1])
    rides.pending["s5_wglu"] = _gather_buffers(local, mlp0[1:])

    def late_mlp0(landed):
        return _assemble(landed.pop("s5_win") + landed.pop("s5_wglu"), mlp0)

    for li in range(depth):
        if li + 1 < depth:
            nxt = _layer_parts(li + 1)
            rides.pending["w1"] = _gather_buffers(local, nxt[0])
            rides.pending["w2"] = _gather_buffers(local, nxt[1])
        mod = [mods[li:li + 1, i * d:(i + 1) * d] for i in range(N_MOD)]
        mod_rows.append(mod)
        x, sv, wl = _layer_fwd(li, x, mod, wl, rides, late_mlp0 if li == 0 else None)
        saved.append(sv)
        layer_w.append(wl)
        if li + 1 < depth:
            wl = {**w, **_assemble(rides.landed.pop("w1"), nxt[0]), **_assemble(rides.landed.pop("w2"), nxt[1])}
    dx, dx_b, dgf, loss = _loss_head("loss_head", x, target, _row(w["final_norm_g"]))
    layer_grads = [None] * depth
    dmods = [None] * depth
    landed = {}
    for li in reversed(range(depth)):
        dx, dx_b, layer_grads[li], dmods[li] = _layer_bwd(li, dx, dx_b, saved[li], mod_rows[li], layer_w[li], rides,
                                                          li == 0)
        if li + 1 < depth:
            landed[(li + 1, 0)] = rides.landed.pop("dr")
            landed[(li + 1, 1)] = rides.landed.pop("dh2")
        if li > 0:
            rides.pending["dr"], rides.pending["dh2"] = _grad_buffers(li, layer_grads[li])
    landed[(0, 1)] = rides.landed.pop("s5_dgl") + rides.landed.pop("s5_dwglu")
    landed[(0, 0)] = _exchange4("rs_g_L0", _grad_buffers(0, layer_grads[0], parts=[0])[0])
    return loss, dx, layer_grads, dgf.reshape(-1), jnp.concatenate(dmods, axis=0), landed


ANY = pl.BlockSpec(memory_space=pl.ANY)
N_DEV = 8
N_CHIP = 4


def _coords():
    return lax.axis_index("x"), lax.axis_index("y"), lax.axis_index("c")


def _allgather8(name, block):
    r, wd = block.shape

    def body(x_ref, out_ref, send_sems, recv_sems, local_sem):
        x, y, c = _coords()
        me, sibling = (x, y, c), (x, y, 1 - c)
        chips = [(1 - x, y), (x, 1 - y), (1 - x, 1 - y)]

        def slot(px, py, pc):
            return out_ref.at[4 * px + 2 * py + pc]

        def copy(k, blk, to, src=None):
            return pltpu.make_async_remote_copy(
                src_ref=slot(*blk) if src is None else src, dst_ref=slot(*blk),
                send_sem=send_sems.at[k], recv_sem=recv_sems.at[k], device_id=to, device_id_type=MESH)

        mine = pltpu.make_async_copy(x_ref, slot(*me), local_sem)
        mine.start()
        first = [copy(0, me, sibling, src=x_ref)]
        first += [copy(1 + j, me, (*chip, c), src=x_ref) for j, chip in enumerate(chips)]
        for cp in first:
            cp.start()
        passed = [copy(4 + j, (*chip, c), sibling) for j, chip in enumerate(chips)]
        for j, chip in enumerate(chips):
            copy(1 + j, (*chip, c), me).wait_recv()
            passed[j].start()
        copy(0, sibling, me).wait_recv()
        for j, chip in enumerate(chips):
            copy(4 + j, (*chip, 1 - c), me).wait_recv()
        for cp in first + passed:
            cp.wait_send()
        mine.wait()

    return pl.pallas_call(
        body, name=name, in_specs=[ANY], out_specs=ANY,
        out_shape=jax.ShapeDtypeStruct((N_DEV, r, wd), block.dtype),
        scratch_shapes=[pltpu.SemaphoreType.DMA((7,)), pltpu.SemaphoreType.DMA((7,)), pltpu.SemaphoreType.DMA],
    )(block)


def _landing_shape(src, kind):
    if kind == "same":
        return (N_CHIP,) + src.shape
    if kind == "packed":
        return src.shape
    rows, cols = src.shape
    return (N_CHIP, rows // N_CHIP, cols) if kind == "rows" else (N_CHIP, rows, cols // N_CHIP)


def _exchange4_ops(srcs, dsts, send_sems, recv_sems, local_sems, kinds):
    x, y, c = _coords()
    my_chip = 2 * x + y
    chips = [(1 - x, y), (x, 1 - y), (1 - x, 1 - y)]

    def piece(q, k):
        ref, kind = srcs[q], kinds[q]
        if kind == "same":
            return ref
        if kind == "packed":
            return ref.at[k]
        _, rows, cols = dsts[q].shape
        return ref.at[pl.ds(k * rows, rows), :] if kind == "rows" else ref.at[:, pl.ds(k * cols, cols)]

    def copy(q, j, k, slot):
        px, py = chips[j]
        return pltpu.make_async_remote_copy(
            src_ref=piece(q, k), dst_ref=dsts[q].at[slot], send_sem=send_sems.at[3 * q + j],
            recv_sem=recv_sems.at[3 * q + j], device_id=(px, py, c), device_id_type=MESH)

    def mine(q):
        return pltpu.make_async_copy(piece(q, my_chip), dsts[q].at[my_chip], local_sems.at[q])

    def start():
        for q in range(len(srcs)):
            mine(q).start()
            for j, (px, py) in enumerate(chips):
                copy(q, j, 2 * px + py, my_chip).start()

    def wait():
        for q in range(len(srcs)):
            for j, (px, py) in enumerate(chips):
                copy(q, j, my_chip, 2 * px + py).wait_recv()
        for q in range(len(srcs)):
            for j, (px, py) in enumerate(chips):
                copy(q, j, 2 * px + py, my_chip).wait_send()
            mine(q).wait()

    return start, wait


def _exchange_scratch(n):
    return [pltpu.SemaphoreType.DMA((3 * n,)), pltpu.SemaphoreType.DMA((3 * n,)), pltpu.SemaphoreType.DMA((n,))]


def _exchange4(name, buffers):
    n = len(buffers)
    kinds = [kind for _, kind in buffers]

    def body(*refs):
        start, wait = _exchange4_ops(refs[:n], refs[n:2 * n], *refs[2 * n:], kinds)
        start()
        wait()

    return pl.pallas_call(
        body, name=name, in_specs=[ANY] * n, out_specs=[ANY] * n,
        out_shape=[jax.ShapeDtypeStruct(_landing_shape(s, k), s.dtype) for s, k in buffers],
        scratch_shapes=_exchange_scratch(n),
    )(*[s for s, _ in buffers])


def _swap_sibling(name, block):
    def body(x_ref, out_ref, send_sem, recv_sem):
        x, y, c = _coords()
        cp = pltpu.make_async_remote_copy(src_ref=x_ref, dst_ref=out_ref, send_sem=send_sem, recv_sem=recv_sem,
                                          device_id=(x, y, 1 - c), device_id_type=MESH)
        cp.start()
        cp.wait()

    return pl.pallas_call(
        body, name=name, in_specs=[ANY], out_specs=ANY, out_shape=jax.ShapeDtypeStruct(block.shape, block.dtype),
        scratch_shapes=[pltpu.SemaphoreType.DMA, pltpu.SemaphoreType.DMA],
    )(block)


def _sum_slots(name, stacked):
    n, r, wd = stacked.shape
    tile = min(FLAT_ROWS, r)

    def kern(x_ref, o_ref):
        acc = x_ref[0].astype(F32)
        for s in range(1, n):
            acc = acc + x_ref[s].astype(F32)
        o_ref[...] = acc

    return pl.pallas_call(
        kern, name=name, grid=(r // tile,), in_specs=[pl.BlockSpec((n, tile, wd), lambda i: (0, i, 0))],
        out_specs=pl.BlockSpec((tile, wd), lambda i: (i, 0)), out_shape=jax.ShapeDtypeStruct((r, wd), F32),
        compiler_params=_params("parallel"),
    )(stacked)


def _adamw(name, w, m, v, g, g2=None):
    r, wd = w.shape
    grads = [g] if g2 is None else [g, g2]
    c1 = 1.0 - ADAM_B1 ** ADAM_STEP
    c2 = 1.0 - ADAM_B2 ** ADAM_STEP

    def body(i, *refs):
        w_ref, m_ref, v_ref = refs[:3]
        g_refs = refs[3:3 + len(grads)]
        go_ref, d_ref, mo_ref, vo_ref = refs[3 + len(grads):]
        gv = g_refs[0][...]
        if g2 is not None:
            gv = gv + g_refs[1][...]
        mn = ADAM_B1 * m_ref[...] + (1.0 - ADAM_B1) * gv
        vn = ADAM_B2 * v_ref[...] + (1.0 - ADAM_B2) * (gv * gv)
        go_ref[...] = gv
        mo_ref[...] = mn
        vo_ref[...] = vn
        d_ref[...] = -ADAM_LR * ((mn / c1) / (jnp.sqrt(vn / c2) + ADAM_EPS) + ADAM_WD * w_ref[...])

    return _rowcall(name, body, r, FLAT_ROWS, [w, m, v] + grads, [], [(wd, F32)] * 4, [])


FLAT_BLOCK = FLAT_ROWS * FLAT_W


def _pack(arrays, dtype):
    flat = jnp.concatenate([a.reshape(-1).astype(dtype) for a in arrays])
    pad = (-flat.shape[0]) % FLAT_BLOCK
    return jnp.pad(flat, (0, pad)).reshape(-1, FLAT_W)


def _unpack(buf, shapes):
    flat = buf.reshape(-1)
    out, off = [], 0
    for s in shapes:
        n = math.prod(s)
        out.append(flat[off:off + n].reshape(s))
        off += n
    return out


SHARDED_BIG = {"mlp_w1": 2, "mlp_w2": 1, "s5_w_in": 1, "s5_w_glu": 2, "m2_w_in": 2, "m2_w_out": 1}
SHARDED_SMALL = {"m2_conv_w": 2, "m2_conv_b": 1, "m2_norm_g": 1}
REPLICATED = ("ada_b", "norm_mix_g", "norm_mlp_g", "s5_lambda_re", "s5_lambda_im", "s5_log_dt", "s5_b_re",
              "s5_b_im", "s5_c_re", "s5_c_im", "s5_d", "s5_b_glu", "m2_dt_bias", "m2_a_log", "m2_d", "final_norm_g")
WEIGHT_NAMES = ("ada_w", "ada_b", "norm_mix_g", "norm_mlp_g", "mlp_w1", "mlp_w2", "s5_w_in", "s5_lambda_re",
                "s5_lambda_im", "s5_log_dt", "s5_b_re", "s5_b_im", "s5_c_re", "s5_c_im", "s5_d", "s5_w_glu",
                "s5_b_glu", "m2_w_in", "m2_conv_w", "m2_conv_b", "m2_dt_bias", "m2_a_log", "m2_d", "m2_norm_g",
                "m2_w_out", "final_norm_g")


def _gather_weights(name, local, names_axes, dtype):
    names = list(names_axes)
    got = _exchange4(name, [(_pack([local[k] for k in names], dtype), "same")])[0]
    per_chip = [_unpack(got[j], [local[k].shape for k in names]) for j in range(N_CHIP)]
    return {k: jnp.concatenate([per_chip[j][i] for j in range(N_CHIP)], axis=names_axes[k])
            for i, k in enumerate(names)}


def _chip_slice(a, chip, axis):
    size = a.shape[axis] // N_CHIP
    return lax.slice_in_dim(a, chip * size, (chip + 1) * size, axis=axis)


def kernel(x, c, ada_w, ada_b, norm_mix_g, norm_mlp_g, mlp_w1, mlp_w2, s5_w_in, s5_lambda_re, s5_lambda_im, s5_log_dt, s5_b_re, s5_b_im, s5_c_re, s5_c_im, s5_d, s5_w_glu, s5_b_glu, m2_w_in, m2_conv_w, m2_conv_b, m2_dt_bias, m2_a_log, m2_d, m2_norm_g, m2_w_out, final_norm_g, loss_target, m_ada_w, m_ada_b, m_norm_mix_g, m_norm_mlp_g, m_mlp_w1, m_mlp_w2, m_s5_w_in, m_s5_lambda_re, m_s5_lambda_im, m_s5_log_dt, m_s5_b_re, m_s5_b_im, m_s5_c_re, m_s5_c_im, m_s5_d, m_s5_w_glu, m_s5_b_glu, m_m2_w_in, m_m2_conv_w, m_m2_conv_b, m_m2_dt_bias, m_m2_a_log, m_m2_d, m_m2_norm_g, m_m2_w_out, m_final_norm_g, v_ada_w, v_ada_b, v_norm_mix_g, v_norm_mlp_g, v_mlp_w1, v_mlp_w2, v_s5_w_in, v_s5_lambda_re, v_s5_lambda_im, v_s5_log_dt, v_s5_b_re, v_s5_b_im, v_s5_c_re, v_s5_c_im, v_s5_d, v_s5_w_glu, v_s5_b_glu, v_m2_w_in, v_m2_conv_w, v_m2_conv_b, v_m2_dt_bias, v_m2_a_log, v_m2_d, v_m2_norm_g, v_m2_w_out, v_final_norm_g):
    args = locals()
    local = {k: args[k] for k in WEIGHT_NAMES}
    mom_m = {k: args["m_" + k] for k in WEIGHT_NAMES}
    mom_v = {k: args["v_" + k] for k in WEIGHT_NAMES}
    depth, d = norm_mix_g.shape
    xi, yi, ci = _coords()
    my_chip = 2 * xi + yi
    my_dev = 2 * my_chip + ci

    cond = jax.nn.silu(c).reshape(-1, LANES)
    cond_all = _allgather8("ag_cond", cond).reshape(N_DEV, d)
    cond_pad = jnp.zeros((LANES, d), F32).at[:N_DEV].set(cond_all)
    mod_cols = ada_w.shape[2]
    mod_part = jnp.stack([_matmul("ada_%d" % i, cond_pad, ada_w[i])[:N_DEV] for i in range(depth)])
    mod_all = _allgather8("ag_mod", mod_part.reshape(-1, LANES)).reshape(N_CHIP, 2, depth, N_DEV, mod_cols)[:, 0]
    mod_mine = lax.dynamic_index_in_dim(mod_all, my_dev, axis=2, keepdims=False)
    mods = jnp.transpose(mod_mine, (1, 0, 2)).reshape(depth, N_CHIP * mod_cols) + ada_b

    w = {k: local[k] for k in REPLICATED}
    w.update(_gather_weights("ag_w_small", local, SHARDED_SMALL, F32))

    loss_row, dx, layer_grads, g_final, dmods, landed = _local_step(x[0], loss_target[0], mods, w, local)
    grads = {"ada_b": dmods, "final_norm_g": g_final}
    for k in REPLICATED[1:-1]:
        grads[k] = jnp.stack([g[k] for g in layer_grads if k in g])

    rep_shapes = [grads[k].shape for k in REPLICATED]
    rep_all = _allgather8("ag_grep", _pack([grads[k] for k in REPLICATED], F32))
    rep_sum = _sum_slots("sum_grep", rep_all)
    dmods_all = rep_all.reshape(N_DEV, -1)[:, :dmods.size].reshape(N_DEV, depth, N_CHIP * mod_cols)

    dm_mine = lax.dynamic_slice_in_dim(dmods_all, my_chip * mod_cols, mod_cols, axis=2)
    dm_pad = jnp.zeros((LANES, depth, mod_cols), F32).at[:N_DEV].set(dm_mine)
    g_ada_w = jnp.stack([_matmul("dada_%d" % i, cond_pad, dm_pad[:, i], "tn") for i in range(depth)])

    red = {}
    for li in range(depth):
        for k, part in enumerate(_grad_specs(li)):
            for (kind, entry), land in zip(part, landed[(li, k)]):
                if kind == "one":
                    red[entry[:2]] = _sum_slots("sum_%s_%d" % entry[:2], land)
                else:
                    total = _sum_slots("sum_packed_L%d" % li, land)
                    shapes = [local[n][i].shape for n, i, _ in entry]
                    red.update({(n, i): v for (n, i, _), v in zip(entry, _unpack(total, shapes))})

    out_g, out_d, out_m, out_v = {}, {}, {}, {}
    for name in list(SHARDED_BIG) + list(SHARDED_SMALL):
        shape = local[name].shape
        flat2 = (-1, shape[-1])
        g1 = jnp.stack([red[(name, i)] for i in range(shape[0])]).reshape(flat2)
        g2 = _swap_sibling("swap_" + name, g1)
        res = _adamw("adam_" + name, local[name].reshape(flat2), mom_m[name].reshape(flat2),
                     mom_v[name].reshape(flat2), g1, g2)
        for dst, buf in zip((out_g, out_d, out_m, out_v), res):
            dst[name] = buf.reshape(shape)
    res = _adamw("adam_rep", _pack([local[k] for k in REPLICATED], F32), _pack([mom_m[k] for k in REPLICATED], F32),
                 _pack([mom_v[k] for k in REPLICATED], F32), rep_sum)
    for dst, buf in zip((out_g, out_d, out_m, out_v), res):
        dst.update(zip(REPLICATED, _unpack(buf, rep_shapes)))
    flat2 = (-1, mod_cols)
    res = _adamw("adam_ada", ada_w.reshape(flat2), m_ada_w.reshape(flat2), v_ada_w.reshape(flat2),
                 g_ada_w.reshape(flat2))
    for dst, buf in zip((out_g, out_d, out_m, out_v), res):
        dst["ada_w"] = buf.reshape(ada_w.shape)

    loss = lax.psum(loss_row[0, 0], ("x", "y", "c"))
    outs = [loss, dx[None]]
    for dst in (out_g, out_d, out_m, out_v):
        outs += [dst[k] for k in WEIGHT_NAMES]
    return tuple(outs)
```

```python
import math

import jax
import jax.numpy as jnp
from jax import lax
from jax.experimental import pallas as pl
from jax.experimental.pallas import tpu as pltpu

F32 = jnp.float32
BF16 = jnp.bfloat16
HIGHEST = lax.Precision.HIGHEST

NORM_EPS = 1e-5
N_MOD = 6
S5_H, S5_P, S5_T = 16, 64, 64
M2_P, M2_N, M2_G, M2_Q, M2_K = 64, 128, 4, 128, 4
LANES = 128
ADAM_LR, ADAM_B1, ADAM_B2, ADAM_EPS, ADAM_WD, ADAM_STEP = 0.001, 0.9, 0.999, 1e-08, 0.01, 10
VMEM_LIMIT_BYTES = 56 * 1024 * 1024
ROW_TILE = 256
FLAT_W = 1024
FLAT_ROWS = 256
MESH = pl.DeviceIdType.MESH


def _params(*sem):
    return pltpu.CompilerParams(dimension_semantics=sem, vmem_limit_bytes=VMEM_LIMIT_BYTES)


def _dot(a, b, dn="nn", precision=None):
    dims = {"nn": ((1,), (0,)), "nt": ((1,), (1,)), "tn": ((0,), (0,))}[dn]
    return lax.dot_general(a, b, (dims, ((), ())), preferred_element_type=F32, precision=precision)


def _bdot(a, b, dn="nn"):
    return _dot(a.astype(BF16), b.astype(BF16), dn)


def _sigmoid(x):
    return jax.nn.sigmoid(x)


def _colsum(x):
    return jnp.sum(x, axis=0, keepdims=True)


def _pick_tile(dim, want):
    if dim <= want:
        return dim
    for t in range(want - want % LANES, 0, -LANES):
        if dim % t == 0:
            return t
    raise ValueError((dim, want))


MATMUL_TILE = 1024
MATMUL_VMEM_BUDGET = 40 * 1024 * 1024


def _matmul_tiles(m, n, k, mode, in_bytes, out_bytes):
    tn = _pick_tile(n, MATMUL_TILE)
    k_tiles = [k] + [t for t in (4096, 2048, 1024) if t < k and k % t == 0]
    m_tiles = [_pick_tile(m, MATMUL_TILE)] + ([512] if mode != "tn" and m % 512 == 0 and m > 512 else [])
    for tk in k_tiles:
        for tm in m_tiles:
            blocks = 2 * (tm * tk * in_bytes[0] + tk * tn * in_bytes[1] + tm * tn * out_bytes)
            if blocks + (4 * tm * tn if tk < k else 0) <= MATMUL_VMEM_BUDGET:
                return tm, tn, tk
    raise ValueError((m, n, k))


def _matmul(name, a, b, mode="nn", out_dtype=F32, relu=False, square_a=False, mul2=None, colscale=None,
            addin=None, colsum_with=None, ride=None):
    if mode == "nn":
        (m, k), (k2, n) = a.shape, b.shape
    elif mode == "nt":
        (m, k), (n, k2) = a.shape, b.shape
    else:
        (k, m), (k2, n) = a.shape, b.shape
    assert k == k2, (name, a.shape, b.shape)
    tiles = [e for e in (mul2, addin, colsum_with) if e is not None]
    out_bytes = jnp.dtype(out_dtype).itemsize + sum(e.dtype.itemsize for e in tiles)
    tm, tn, tk = _matmul_tiles(m, n, k, mode, (a.dtype.itemsize, b.dtype.itemsize), out_bytes)
    nk = k // tk
    n_ext = len(tiles) + (colscale is not None)
    n_out = 1 + (colsum_with is not None)
    n_ride = 0 if ride is None else len(ride)
    grid = (m // tm, n // tn, nk)

    def kern(*refs):
        a_ref, b_ref = refs[:2]
        e_refs = list(refs[2:2 + n_ext])
        o_refs = refs[2 + n_ext + n_ride:2 + n_ext + n_ride + n_out]
        kk = pl.program_id(2)
        if ride is not None:
            here = [pl.program_id(ax) for ax in range(3)]
            land0 = 2 + n_ext + n_ride + n_out
            ride_refs = (refs[2 + n_ext:2 + n_ext + n_ride], refs[land0:land0 + n_ride]) + tuple(refs[-3:])
            ride_kinds = [kind for _, kind in ride]
            first = (here[0] == 0) & (here[1] == 0) & (here[2] == 0)
            last = (here[0] == grid[0] - 1) & (here[1] == grid[1] - 1) & (here[2] == grid[2] - 1)

            @pl.when(first)
            def _():
                _exchange4_ops(*ride_refs, ride_kinds)[0]()

        av = a_ref[...]
        if square_a:
            av = av * av
        part = _bdot(av, b_ref[...], mode)

        def finish(r):
            ext = list(e_refs)
            m2v = ext.pop(0)[...].astype(F32) if mul2 is not None else None
            addv = ext.pop(0)[...].astype(F32) if addin is not None else None
            if colsum_with is not None:
                o_refs[1][0] = _colsum(r * ext.pop(0)[...].astype(F32))
            if relu:
                r = jnp.maximum(r, 0.0)
            if m2v is not None:
                r = r * (2.0 * m2v)
            if colscale is not None:
                r = r * ext.pop(0)[...]
            if addv is not None:
                r = r + addv
            o_refs[0][...] = r.astype(out_dtype)

        if nk == 1:
            finish(part)
        else:
            acc = refs[-4] if ride is not None else refs[-1]

            @pl.when(kk == 0)
            def _():
                acc[...] = part

            @pl.when(kk > 0)
            def _():
                acc[...] += part

            @pl.when(kk == nk - 1)
            def _():
                finish(acc[...])

        if ride is not None:
            @pl.when(last)
            def _():
                _exchange4_ops(*ride_refs, ride_kinds)[1]()

    if mode == "tn":
        a_spec = pl.BlockSpec((tk, tm), lambda i, j, kk: (kk, i))
    else:
        a_spec = pl.BlockSpec((tm, tk), lambda i, j, kk: (i, kk))
    if mode == "nt":
        b_spec = pl.BlockSpec((tn, tk), lambda i, j, kk: (j, kk))
    else:
        b_spec = pl.BlockSpec((tk, tn), lambda i, j, kk: (kk, j))
    o_spec = pl.BlockSpec((tm, tn), lambda i, j, kk: (i, j))
    in_specs = [a_spec, b_spec] + [o_spec] * len(tiles)
    operands = [a, b] + tiles
    if colscale is not None:
        in_specs.append(pl.BlockSpec((1, tn), lambda i, j, kk: (0, j)))
        operands.append(colscale)
    out_specs = [o_spec]
    out_shape = [jax.ShapeDtypeStruct((m, n), out_dtype)]
    if colsum_with is not None:
        out_specs.append(pl.BlockSpec((1, 1, tn), lambda i, j, kk: (i, 0, j)))
        out_shape.append(jax.ShapeDtypeStruct((m // tm, 1, n), F32))
    scratch = [pltpu.VMEM((tm, tn), F32)] if nk > 1 else []
    semantics = ("parallel", "parallel", "arbitrary")
    if ride is not None:
        for src, kind in ride:
            in_specs.append(pl.BlockSpec(memory_space=pl.ANY))
            operands.append(src)
            out_specs.append(pl.BlockSpec(memory_space=pl.ANY))
            out_shape.append(jax.ShapeDtypeStruct(_landing_shape(src, kind), src.dtype))
        scratch += _exchange_scratch(n_ride)
        semantics = ("arbitrary", "arbitrary", "arbitrary")
    res = pl.pallas_call(
        kern, name=name, grid=grid, in_specs=in_specs, out_specs=out_specs, out_shape=out_shape,
        scratch_shapes=scratch, compiler_params=_params(*semantics),
    )(*operands)
    return res if len(res) > 1 else res[0]


def _rowcall(name, body, rows, tile, row_ins, small_ins, row_outs, acc_outs):
    tile = min(tile, rows)
    assert rows % tile == 0, (name, rows, tile)
    n_in = len(row_ins) + len(small_ins)

    def kern(*refs):
        i = pl.program_id(0)
        accs = refs[n_in + len(row_outs):]

        @pl.when(i == 0)
        def _():
            for acc in accs:
                acc[...] = jnp.zeros_like(acc)

        body(i, *refs)

    def whole(shape):
        return pl.BlockSpec(shape, lambda i, nd=len(shape): (0,) * nd)

    in_specs = [pl.BlockSpec((tile, a.shape[1]), lambda i: (i, 0)) for a in row_ins]
    in_specs += [whole(a.shape) for a in small_ins]
    out_specs = [pl.BlockSpec((tile, w), lambda i: (i, 0)) for (w, _) in row_outs]
    out_specs += [whole(s) for s in acc_outs]
    out_shape = [jax.ShapeDtypeStruct((rows, w), dt) for (w, dt) in row_outs]
    out_shape += [jax.ShapeDtypeStruct(s, F32) for s in acc_outs]
    return pl.pallas_call(
        kern, name=name, grid=(rows // tile,), in_specs=in_specs, out_specs=out_specs, out_shape=out_shape,
        compiler_params=_params("arbitrary"),
    )(*row_ins, *small_ins)


def _rms(x):
    r = lax.rsqrt(jnp.mean(x * x, axis=-1, keepdims=True) + NORM_EPS)
    return x * r, r


def _rms_bwd(dxhat, xhat, r):
    return r * (dxhat - xhat * jnp.mean(dxhat * xhat, axis=-1, keepdims=True))


def _normmod_fwd(name, x, g, sh, sc):
    def body(i, x_ref, g_ref, sh_ref, sc_ref, o_ref):
        xhat, _ = _rms(x_ref[...])
        o_ref[...] = ((xhat * g_ref[...]) * (1.0 + sc_ref[...]) + sh_ref[...]).astype(BF16)

    return _rowcall(name, body, x.shape[0], ROW_TILE, [x], [g, sh, sc], [(x.shape[1], BF16)], [])[0]


def _normmod_bwd(name, x, g, sh, sc, dh, dx_pass):
    d = x.shape[1]

    def body(i, x_ref, dh_ref, dxp_ref, g_ref, sh_ref, sc_ref, dx_ref, dxb_ref, dg_ref, dsh_ref, dsc_ref):
        xhat, r = _rms(x_ref[...])
        dh = dh_ref[...].astype(F32)
        gv = g_ref[...]
        dn = dh * (1.0 + sc_ref[...])
        dsc_ref[...] += _colsum(dh * (xhat * gv))
        dsh_ref[...] += _colsum(dh)
        dg_ref[...] += _colsum(dn * xhat)
        dx = dxp_ref[...] + _rms_bwd(dn * gv, xhat, r)
        dx_ref[...] = dx
        dxb_ref[...] = dx.astype(BF16)

    return _rowcall(name, body, x.shape[0], ROW_TILE, [x, dh, dx_pass], [g, sh, sc], [(d, F32), (d, BF16)],
                    [(1, d), (1, d), (1, d)])


def _scale_cols(name, w, g):
    def body(i, w_ref, g_ref, o_ref):
        o_ref[...] = (w_ref[...].astype(F32) * g_ref[...]).astype(BF16)

    return _rowcall(name, body, w.shape[0], ROW_TILE, [w], [g], [(w.shape[1], BF16)], [])[0]


GELU_K = math.sqrt(2.0 / math.pi)
GELU_C = 0.044715


def _gelu_fwd(name, y, u, skip):
    def body(i, y_ref, u_ref, s_ref, o_ref):
        v = y_ref[...].astype(F32) + s_ref[...] * u_ref[...]
        t = jnp.tanh(GELU_K * (v + GELU_C * (v * v * v)))
        o_ref[...] = (0.5 * v * (1.0 + t)).astype(BF16)

    return _rowcall(name, body, y.shape[0], ROW_TILE, [y, u], [skip], [(y.shape[1], BF16)], [])[0]


def _gelu_bwd(name, y, u, skip, dgl):
    d = y.shape[1]

    def body(i, y_ref, u_ref, d_ref, s_ref, o_ref, ds_ref):
        uv = u_ref[...]
        v = y_ref[...].astype(F32) + s_ref[...] * uv
        t = jnp.tanh(GELU_K * (v + GELU_C * (v * v * v)))
        dv = d_ref[...] * (0.5 * (1.0 + t) + 0.5 * v * (1.0 - t * t) * (GELU_K * (1.0 + 3.0 * GELU_C * v * v)))
        o_ref[...] = dv
        ds_ref[...] += _colsum(dv * uv)

    return _rowcall(name, body, y.shape[0], ROW_TILE, [y, u, dgl], [skip], [(d, F32)], [(1, d)])


def _axpy(name, a, b, scale):
    def body(i, a_ref, b_ref, s_ref, o_ref):
        o_ref[...] = (a_ref[...].astype(F32) + s_ref[...] * b_ref[...]).astype(BF16)

    return _rowcall(name, body, a.shape[0], ROW_TILE, [a, b], [scale], [(a.shape[1], BF16)], [])[0]


def _glu_fwd(name, ab, bias, x, gate):
    d = ab.shape[1] // 2

    def body(i, ab_ref, x_ref, b_ref, g_ref, o_ref):
        v = ab_ref[:, :d] + b_ref[:, :d]
        gt = ab_ref[:, d:] + b_ref[:, d:]
        o_ref[...] = x_ref[...] + g_ref[...] * (v * _sigmoid(gt))

    return _rowcall(name, body, ab.shape[0], ROW_TILE, [ab, x], [bias, gate], [(d, F32)], [])[0]


def _glu_bwd(name, ab, bias, dxo, gate):
    d = ab.shape[1] // 2

    def body(i, ab_ref, dx_ref, b_ref, g_ref, dab_ref, db_ref, dg_ref):
        v = ab_ref[:, :d] + b_ref[:, :d]
        s = _sigmoid(ab_ref[:, d:] + b_ref[:, d:])
        dxo_v = dx_ref[...]
        dg_ref[...] += _colsum(dxo_v * (v * s))
        do = g_ref[...] * dxo_v
        dv = do * s
        dgt = do * v * (s * (1.0 - s))
        dab_ref[:, :d] = dv.astype(BF16)
        dab_ref[:, d:] = dgt.astype(BF16)
        db_ref[:, :d] += _colsum(dv)
        db_ref[:, d:] += _colsum(dgt)

    return _rowcall(name, body, ab.shape[0], ROW_TILE, [ab, dxo], [bias, gate], [(2 * d, BF16)],
                    [(1, 2 * d), (1, d)])


def _gatenorm_fwd(name, y, z, ng):
    di = y.shape[1]
    gw = di // M2_G

    def body(i, y_ref, z_ref, g_ref, o_ref):
        for gi in range(M2_G):
            sl = slice(gi * gw, (gi + 1) * gw)
            zz = z_ref[:, sl]
            y2 = y_ref[:, sl] * (zz * _sigmoid(zz))
            yh, _ = _rms(y2)
            o_ref[:, sl] = (yh * g_ref[:, sl]).astype(BF16)

    return _rowcall(name, body, y.shape[0], ROW_TILE, [y, z], [ng], [(di, BF16)], [])[0]


def _gatenorm_bwd(name, y, z, ng, dyn):
    di = y.shape[1]
    gw = di // M2_G

    def body(i, y_ref, z_ref, d_ref, g_ref, dy_ref, dz_ref, dg_ref):
        for gi in range(M2_G):
            sl = slice(gi * gw, (gi + 1) * gw)
            zz = z_ref[:, sl]
            yy = y_ref[:, sl]
            s = _sigmoid(zz)
            sz = zz * s
            yh, r = _rms(yy * sz)
            dn = d_ref[:, sl]
            dg_ref[:, sl] += _colsum(dn * yh)
            dy2 = _rms_bwd(dn * g_ref[:, sl], yh, r)
            dy_ref[:, sl] = dy2 * sz
            dz_ref[:, sl] = (dy2 * yy * (s * (1.0 + zz * (1.0 - s)))).astype(BF16)

    return _rowcall(name, body, y.shape[0], ROW_TILE, [y, z, dyn], [ng], [(di, F32), (di, BF16)], [(1, di)])


def _loss_head(name, x, target, g):
    d = x.shape[1]

    def body(i, x_ref, t_ref, g_ref, dx_ref, dxb_ref, dg_ref, loss_ref):
        xhat, r = _rms(x_ref[...])
        gv = g_ref[...]
        err = xhat * gv - t_ref[...]
        per_row = jnp.sum(err * err, axis=-1, keepdims=True) * (0.5 / d)
        loss_ref[...] += jnp.broadcast_to(_colsum(per_row), loss_ref.shape)
        dy = err * (1.0 / d)
        dg_ref[...] += _colsum(dy * xhat)
        dx = _rms_bwd(dy * gv, xhat, r)
        dx_ref[...] = dx
        dxb_ref[...] = dx.astype(BF16)

    return _rowcall(name, body, x.shape[0], ROW_TILE, [x, target], [g], [(d, F32), (d, BF16)],
                    [(1, d), (1, LANES)])


HALO = 8


def _halo_call(name, body, rows, tile, width, mains, halo_of, halo_next, smalls, row_outs, acc_outs, scratch):
    tile = min(tile, rows)
    nb = tile // HALO
    last = rows // HALO - 1
    n_in = len(mains) + 1 + len(smalls)

    def kern(*refs):
        i = pl.program_id(0)
        accs = refs[n_in + len(row_outs):n_in + len(row_outs) + len(acc_outs)]

        @pl.when(i == 0)
        def _():
            for acc in accs:
                acc[...] = jnp.zeros_like(acc)

        body(i, *refs)

    def whole(shape):
        return pl.BlockSpec(shape, lambda i, nd=len(shape): (0,) * nd)

    if halo_next:
        halo_spec = pl.BlockSpec((HALO, width), lambda i: (jnp.minimum((i + 1) * nb, last), 0))
    else:
        halo_spec = pl.BlockSpec((HALO, width), lambda i: (jnp.maximum(i * nb - 1, 0), 0))
    in_specs = [pl.BlockSpec((tile, a.shape[1]), lambda i: (i, 0)) for a in mains] + [halo_spec]
    in_specs += [whole(a.shape) for a in smalls]
    out_specs = [pl.BlockSpec((tile, w), lambda i: (i, 0)) for (w, _) in row_outs] + [whole(s) for s in acc_outs]
    out_shape = [jax.ShapeDtypeStruct((rows, w), dt) for (w, dt) in row_outs]
    out_shape += [jax.ShapeDtypeStruct(s, F32) for s in acc_outs]
    return pl.pallas_call(
        kern, name=name, grid=(rows // tile,), in_specs=in_specs, out_specs=out_specs, out_shape=out_shape,
        scratch_shapes=scratch, compiler_params=_params("arbitrary"),
    )(*mains, mains[halo_of], *smalls)


CONV_TILE = 128
CONV_ROWS = 16
CONV_STRIP = 512


def _conv_blocks(tile, c, strip=CONV_STRIP):
    strip = strip if c % strip == 0 else LANES
    rb = min(CONV_ROWS, tile)
    return [(r0, rb, slice(c0, c0 + strip)) for c0 in range(0, c, strip) for r0 in range(0, tile, rb)]


def _shifted_windows(base, rb, offsets):
    n = base.shape[0]
    out = []
    for o in offsets:
        if o % HALO == 0:
            out.append(base[o:o + rb, :])
        else:
            out.append(pltpu.roll(base, n - o, 0)[0:rb, :])
    return out


def _conv_windows(x_ref, ext, r0, rb, sl):
    base = ext[:, sl] if r0 == 0 else x_ref[r0 - HALO:r0 + rb, sl]
    return _shifted_windows(base, rb, [HALO - 3 + k for k in range(M2_K)])


def _conv_fwd(name, xin, w, b):
    rows, c = xin.shape
    tile = min(CONV_TILE, rows)
    rb0 = min(CONV_ROWS, tile)

    def body(i, x_ref, h_ref, w_ref, b_ref, o_ref, ext):
        ext[0:HALO, :] = jnp.where(i == 0, 0.0, h_ref[...])
        ext[HALO:, :] = x_ref[0:rb0, :]
        for r0, rb, sl in _conv_blocks(tile, c):
            taps = _conv_windows(x_ref, ext, r0, rb, sl)
            pre = b_ref[:, sl] + w_ref[0:1, sl] * taps[0]
            for k in range(1, M2_K):
                pre = pre + w_ref[k:k + 1, sl] * taps[k]
            o_ref[r0:r0 + rb, sl] = pre * _sigmoid(pre)

    return _halo_call(name, body, rows, tile, c, [xin], 0, False, [w, b], [(c, F32)], [],
                      [pltpu.VMEM((rb0 + HALO, c), F32)])[0]


def _conv_bwd_pre(name, xin, dout, w, b):
    rows, c = xin.shape
    tile = min(CONV_TILE, rows)
    rb0 = min(CONV_ROWS, tile)

    def body(i, x_ref, d_ref, h_ref, w_ref, b_ref, dp_ref, dw_ref, db_ref, ext):
        ext[0:HALO, :] = jnp.where(i == 0, 0.0, h_ref[...])
        ext[HALO:, :] = x_ref[0:rb0, :]
        sums = {}
        for r0, rb, sl in _conv_blocks(tile, c, CONV_STRIP // 2):
            taps = _conv_windows(x_ref, ext, r0, rb, sl)
            pre = b_ref[:, sl] + w_ref[0:1, sl] * taps[0]
            for k in range(1, M2_K):
                pre = pre + w_ref[k:k + 1, sl] * taps[k]
            s = _sigmoid(pre)
            dp = d_ref[r0:r0 + rb, sl] * (s * (1.0 + pre * (1.0 - s)))
            dp_ref[r0:r0 + rb, sl] = dp
            part = [dp] + [dp * taps[k] for k in range(M2_K)]
            key = sl.start
            sums[key] = part if key not in sums else [p + q for p, q in zip(sums[key], part)]
            if r0 + rb == tile:
                db_ref[:, sl] += _colsum(sums[key][0])
                for k in range(M2_K):
                    dw_ref[k:k + 1, sl] += _colsum(sums[key][1 + k])

    return _halo_call(name, body, rows, tile, c, [xin, dout], 0, False, [w, b], [(c, F32)], [(M2_K, c), (1, c)],
                      [pltpu.VMEM((rb0 + HALO, c), F32)])


def _conv_bwd_in(name, dpre, w):
    rows, c = dpre.shape
    tile = min(CONV_TILE, rows)
    n_tiles = rows // tile
    rb0 = min(CONV_ROWS, tile)

    def body(i, d_ref, h_ref, w_ref, o_ref, ext):
        ext[0:rb0, :] = d_ref[tile - rb0:tile, :]
        ext[rb0:, :] = jnp.where(i == n_tiles - 1, 0.0, h_ref[...])
        for r0, rb, sl in _conv_blocks(tile, c):
            base = ext[:, sl] if r0 + rb == tile else d_ref[r0:r0 + rb + HALO, sl]
            wins = _shifted_windows(base, rb, [3 - k for k in range(M2_K)])
            acc = w_ref[0:1, sl] * wins[0]
            for k in range(1, M2_K):
                acc = acc + w_ref[k:k + 1, sl] * wins[k]
            o_ref[r0:r0 + rb, sl] = acc.astype(BF16)

    return _halo_call(name, body, rows, tile, c, [dpre], 0, True, [w], [(c, BF16)], [],
                      [pltpu.VMEM((rb0 + HALO, c), F32)])[0]


def _s5_build(lam_re, lam_im, log_dt, b_re, b_im, c_re, c_im):
    g, p = lam_re.shape
    h = b_re.shape[-1]
    t = S5_T
    dt = jnp.exp(log_dt)[:, None]
    ld_re, ld_im = lam_re * dt, lam_im * dt
    tau = jnp.arange(t + 1, dtype=F32)
    mag = jnp.exp(ld_re[:, :, None] * tau)
    ang = ld_im[:, :, None] * tau
    pw_re, pw_im = mag * jnp.cos(ang), mag * jnp.sin(ang)
    num_re, num_im = pw_re[:, :, 1] - 1.0, pw_im[:, :, 1]
    den = lam_re * lam_re + lam_im * lam_im
    q_re = (num_re * lam_re + num_im * lam_im) / den
    q_im = (num_im * lam_re - num_re * lam_im) / den
    bb_re = q_re[:, :, None] * b_re - q_im[:, :, None] * b_im
    bb_im = q_re[:, :, None] * b_im + q_im[:, :, None] * b_re
    bbt_re, bbt_im = jnp.transpose(bb_re, (0, 2, 1)), jnp.transpose(bb_im, (0, 2, 1))
    ct_re, ct_im = jnp.transpose(c_re, (0, 2, 1)), jnp.transpose(c_im, (0, 2, 1))
    lane = jnp.arange(t * h)
    rep_tau = (lane[None, :] // h == jnp.arange(t)[:, None]).astype(F32)
    tile_h = (lane[None, :] % h == jnp.arange(h)[:, None]).astype(F32)

    def spread(x, m):
        return jnp.einsum("gpk,kn->gpn", x, m, precision=lax.Precision.HIGH)

    c_re_n, c_im_n = spread(ct_re, tile_h), spread(ct_im, tile_h)

    def c_times_pw(first):
        pr, pi = spread(pw_re[:, :, first:first + t], rep_tau), spread(pw_im[:, :, first:first + t], rep_tau)
        return pr * c_re_n - pi * c_im_n, pr * c_im_n + pi * c_re_n

    cp0_re, cp0_im = c_times_pw(0)
    cp1_re, cp1_im = c_times_pw(1)
    kc = (jnp.einsum("ghp,gpn->ghn", bbt_re, cp0_re, precision=lax.Precision.HIGH)
          - jnp.einsum("ghp,gpn->ghn", bbt_im, cp0_im, precision=lax.Precision.HIGH))
    bp_re = jnp.transpose(pw_re[:, :, t - 1::-1][:, :, :t], (0, 2, 1))
    bp_im = jnp.transpose(pw_im[:, :, t - 1::-1][:, :, :t], (0, 2, 1))
    be_re = bbt_re[:, :, None, :] * bp_re[:, None, :, :] - bbt_im[:, :, None, :] * bp_im[:, None, :, :]
    be_im = bbt_re[:, :, None, :] * bp_im[:, None, :, :] + bbt_im[:, :, None, :] * bp_re[:, None, :, :]
    bend = jnp.concatenate([be_re, be_im], axis=-1).reshape(g, h * t, 2 * p)
    cpow = jnp.concatenate([cp1_re, -cp1_im], axis=1)
    at_re, at_im = pw_re[:, :, t], pw_im[:, :, t]
    a1 = jnp.concatenate([at_re, at_re], axis=-1)[:, None, :]
    a2 = jnp.concatenate([-at_im, at_im], axis=-1)[:, None, :]
    return kc, bend, cpow, a1, a2


def _swap_halves(x, axis):
    n = x.shape[axis] // 2
    lo = lax.slice_in_dim(x, 0, n, axis=axis)
    hi = lax.slice_in_dim(x, n, 2 * n, axis=axis)
    return jnp.concatenate([hi, lo], axis=axis)


def _group_spec(shape):
    return pl.BlockSpec((1,) + tuple(shape[1:]), lambda g: (g, 0, 0))


S5_ROWS = 8


def _s5_expand_toeplitz(kc_ref, ext, toep):
    t, th = S5_T, S5_T * S5_H
    ext[:, th:] = jnp.zeros((S5_ROWS, LANES), F32)
    for hin in range(S5_H):
        ext[:, :th] = jnp.broadcast_to(kc_ref[0, hin:hin + 1, :], (S5_ROWS, th))
        rolled = pltpu.roll(ext[...], 0, 1, stride=S5_H, stride_axis=0)
        tiles = []
        for q in range(t // S5_ROWS):
            if q == 0:
                tiles.append(rolled[:, :th])
            else:
                tiles.append(jnp.concatenate([jnp.zeros((S5_ROWS, q * LANES), F32), rolled[:, :th - q * LANES]],
                                             axis=1))
        toep[hin * t:(hin + 1) * t, :] = jnp.concatenate(tiles, axis=0).astype(BF16)


def _s5_core_fwd(name, u, ops):
    kc, bend, cpow, a1, a2 = ops
    g, nc, th = u.shape
    p2 = bend.shape[-1]
    bend_b, cpow_b = bend.astype(BF16), cpow.astype(BF16)
    a2s = _swap_halves(a2, 2)

    def kern(u_ref, k_ref, b_ref, c_ref, a1_ref, a2_ref, a2s_ref, y_ref, sp_ref, x_scr, xs_scr, ext, toep):
        _s5_expand_toeplitz(k_ref, ext, toep)
        ub = u_ref[0].astype(BF16)
        xv = _dot(ub, b_ref[0])
        x_scr[...] = xv
        xs_scr[...] = pltpu.roll(xv, p2 // 2, 1)
        a1v, a2v, a2sv = a1_ref[0], a2_ref[0], a2s_ref[0]

        def step(c, carry):
            s, ss = carry
            sp_ref[0, pl.ds(c, 1), :] = s
            s_new = a1v * s + a2v * ss + x_scr[pl.ds(c, 1), :]
            ss_new = a1v * ss + a2sv * s + xs_scr[pl.ds(c, 1), :]
            return s_new, ss_new

        zero = jnp.zeros((1, p2), F32)
        lax.fori_loop(0, nc, step, (zero, zero))
        y_ref[0] = (_dot(ub, toep[...]) + _dot(sp_ref[0].astype(BF16), c_ref[0])).astype(BF16)

    ins = [u, kc, bend_b, cpow_b, a1, a2, a2s]
    return pl.pallas_call(
        kern, name=name, grid=(g,), in_specs=[_group_spec(a.shape) for a in ins],
        out_specs=[_group_spec((g, nc, th)), _group_spec((g, nc, p2))],
        out_shape=[jax.ShapeDtypeStruct((g, nc, th), BF16), jax.ShapeDtypeStruct((g, nc, p2), F32)],
        scratch_shapes=[pltpu.VMEM((nc, p2), F32), pltpu.VMEM((nc, p2), F32),
                        pltpu.VMEM((S5_ROWS, th + LANES), F32), pltpu.VMEM((th, th), BF16)],
        compiler_params=_params("arbitrary"),
    )(*ins)


def _s5_core_bwd(name, u, dy, sprev, ops):
    kc, bend, cpow, a1, a2 = ops
    g, nc, th = u.shape
    t = S5_T
    p2 = bend.shape[-1]
    bend_b, cpow_b = bend.astype(BF16), cpow.astype(BF16)
    a2s = _swap_halves(a2, 2)
    idx = jnp.arange(th)
    flip = (idx[:, None] // t == idx[None, :] // t) & (idx[:, None] % t == t - 1 - idx[None, :] % t)
    flip = flip.astype(BF16)

    def kern(u_ref, dy_ref, sp_ref, k_ref, b_ref, c_ref, a1_ref, a2_ref, a2s_ref, f_ref,
             du_ref, dk_ref, db_ref, dc_ref, da1_ref, da2_ref, g_scr, gs_scr, dx_scr, ext, toep, dtoep):
        _s5_expand_toeplitz(k_ref, ext, toep)
        ub, dyb = u_ref[0].astype(BF16), dy_ref[0].astype(BF16)
        dtoep[...] = _dot(_dot(ub, f_ref[...]).astype(BF16), dyb, "tn")
        n_q = t // S5_ROWS
        width = th + LANES
        for hin in range(S5_H):
            folded = dtoep[hin * t + (n_q - 1) * S5_ROWS:(hin + 1) * t, :]
            for qp in range(n_q - 1):
                q = n_q - 1 - qp
                tile = dtoep[hin * t + qp * S5_ROWS:hin * t + (qp + 1) * S5_ROWS, :]
                folded = folded + jnp.concatenate([tile[:, q * LANES:], jnp.zeros((S5_ROWS, q * LANES), F32)],
                                                  axis=1)
            ext[:, :th] = folded
            rolled = pltpu.roll(ext[...], 0, 1, stride=S5_H, stride_axis=0)
            rolled = pltpu.roll(rolled, width - S5_H * (S5_ROWS - 1), 1)
            dk_ref[0, hin:hin + 1, :] = _colsum(rolled)[:, :th]
        spv = sp_ref[0]
        dc_ref[0] = _dot(spv.astype(BF16), dyb, "tn")
        gv = _dot(dyb, c_ref[0], "nt")
        g_scr[...] = gv
        gs_scr[...] = pltpu.roll(gv, p2 // 2, 1)
        a1v, a2v, a2sv = a1_ref[0], a2_ref[0], a2s_ref[0]

        def step(k, carry):
            gr, grs, da1, da2 = carry
            c = nc - 1 - k
            dx_scr[pl.ds(c, 1), :] = gr
            s_in = sp_ref[0, pl.ds(c, 1), :]
            da1 = da1 + gr * s_in
            da2 = da2 + grs * s_in
            gr_new = g_scr[pl.ds(c, 1), :] + a1v * gr + a2sv * grs
            grs_new = gs_scr[pl.ds(c, 1), :] + a1v * grs + a2v * gr
            return gr_new, grs_new, da1, da2

        zero = jnp.zeros((1, p2), F32)
        _, _, da1, da2 = lax.fori_loop(0, nc, step, (zero, zero, zero, zero))
        da1_ref[0] = da1
        da2_ref[0] = da2
        dxb = dx_scr[...].astype(BF16)
        db_ref[0] = _dot(ub, dxb, "tn")
        du_ref[0] = (_dot(dyb, toep[...], "nt") + _dot(dxb, b_ref[0], "nt")).astype(BF16)

    ins = [u, dy, sprev, kc, bend_b, cpow_b, a1, a2, a2s]
    outs = [(g, nc, th), (g, S5_H, th), (g, th, p2), (g, p2, th), (g, 1, p2), (g, 1, p2)]
    out_types = [BF16] + [F32] * (len(outs) - 1)
    return pl.pallas_call(
        kern, name=name, grid=(g,),
        in_specs=[_group_spec(a.shape) for a in ins] + [pl.BlockSpec((th, th), lambda gi: (0, 0))],
        out_specs=[_group_spec(s) for s in outs],
        out_shape=[jax.ShapeDtypeStruct(s, dt) for s, dt in zip(outs, out_types)],
        scratch_shapes=[pltpu.VMEM((nc, p2), F32), pltpu.VMEM((nc, p2), F32), pltpu.VMEM((nc, p2), F32),
                        pltpu.VMEM((S5_ROWS, th + LANES), F32), pltpu.VMEM((th, th), BF16),
                        pltpu.VMEM((th, th), F32)],
        compiler_params=_params("arbitrary"),
    )(*ins, flip)


def _s5_to_groups(u, channel_major):
    rows, w = u.shape
    g = w // S5_H
    nc = rows // S5_T
    perm = (2, 0, 3, 1) if channel_major else (2, 0, 1, 3)
    return u.reshape(nc, S5_T, g, S5_H).transpose(perm).reshape(g, nc, S5_T * S5_H)


def _s5_from_groups(y, channel_major):
    g, nc, _ = y.shape
    if channel_major:
        return y.reshape(g, nc, S5_H, S5_T).transpose(1, 3, 0, 2).reshape(nc * S5_T, g * S5_H)
    return y.reshape(g, nc, S5_T, S5_H).transpose(1, 2, 0, 3).reshape(nc * S5_T, g * S5_H)


def _softplus(x):
    return jnp.maximum(x, 0.0) + jnp.log(1.0 + jnp.exp(-jnp.abs(x)))


def _ssd_chunk_prep(dtraw_ref, dtb_ref, a_ref, cst, dtt, lastt, n_heads):
    q = M2_Q
    lane = lax.broadcasted_iota(jnp.int32, (q, LANES), 1)
    dt = jnp.where(lane < n_heads, _softplus(dtraw_ref[...] + dtb_ref[...]), 0.0)
    adt = dt * a_ref[...]
    row = lax.broadcasted_iota(jnp.int32, (q, q), 0)
    col = lax.broadcasted_iota(jnp.int32, (q, q), 1)
    cs = _dot(jnp.where(row >= col, 1.0, 0.0), adt, precision=HIGHEST)
    cst[...] = cs.T
    dtt[...] = dt.T
    lastt[...] = jnp.broadcast_to(_colsum(adt), (q, LANES)).T
    return dt


def _pair_tables(cst, dtt, lastt, p):
    q = M2_Q
    out = []
    for hh in (2 * p, 2 * p + 1):
        rc = jnp.broadcast_to(cst[hh:hh + 1, :], (q, q))
        cc = rc.T
        dtc = jnp.broadcast_to(dtt[hh:hh + 1, :], (q, q)).T
        lb = jnp.broadcast_to(lastt[hh:hh + 1, :], (q, q))
        out.append((rc, cc, dtc, lb))
    return out


def _ssd_pair_fwd(x, bm, cm, cb, hs, tabs):
    q = M2_Q
    row = lax.broadcasted_iota(jnp.int32, (q, q), 0)
    col = lax.broadcasted_iota(jnp.int32, (q, q), 1)
    causal = row >= col
    lo = col < M2_P
    slo = row < M2_P
    (rc0, cc0, dtc0, lb0), (rc1, cc1, dtc1, lb1) = tabs
    l0 = jnp.where(causal, jnp.exp(jnp.where(causal, cc0 - rc0, 0.0)), 0.0)
    l1 = jnp.where(causal, jnp.exp(jnp.where(causal, cc1 - rc1, 0.0)), 0.0)
    m0, m1 = cb * l0, cb * l1
    dtp = jnp.where(lo, dtc0, dtc1)
    xdt = x * dtp
    xdt0 = jnp.where(lo, xdt, 0.0)
    xdt1 = jnp.where(lo, 0.0, xdt)
    e = jnp.where(lo, jnp.exp(cc0), jnp.exp(cc1))
    z = _bdot(cm, hs, "nt")
    yoff = z * e
    dec = jnp.where(lo, jnp.exp(lb0 - cc0), jnp.exp(lb1 - cc1))
    xdd = xdt * dec
    cd = jnp.where(slo, jnp.exp(lb0), jnp.exp(lb1))
    return dict(l0=l0, l1=l1, m0=m0, m1=m1, dtp=dtp, xdt=xdt, xdt0=xdt0, xdt1=xdt1, e=e, yoff=yoff,
                dec=dec, xdd=xdd, cd=cd, lo=lo, slo=slo)


def _ssd_fwd(name, xbc, dtraw, dtb, arow, dvec, n_heads):
    rows, c = xbc.shape
    q, n = M2_Q, M2_N
    di = n_heads * M2_P
    n_pairs = n_heads // 2
    ppg = n_pairs // M2_G
    nc = rows // q

    def kern(xbc_ref, dtraw_ref, dtb_ref, a_ref, d_ref, y_ref, prev_ref, state, cst, dtt, lastt):
        @pl.when(pl.program_id(0) == 0)
        def _():
            state[...] = jnp.zeros_like(state)

        _ssd_chunk_prep(dtraw_ref, dtb_ref, a_ref, cst, dtt, lastt, n_heads)
        for p in range(n_pairs):
            gi = p // ppg
            sl = slice(p * LANES, (p + 1) * LANES)
            x = xbc_ref[:, sl]
            bm = xbc_ref[:, di + gi * n:di + (gi + 1) * n]
            cm = xbc_ref[:, di + (M2_G + gi) * n:di + (M2_G + gi + 1) * n]
            if p % ppg == 0:
                cb = _bdot(cm, bm, "nt")
            hs = state[p]
            f = _ssd_pair_fwd(x, bm, cm, cb, hs, _pair_tables(cst, dtt, lastt, p))
            ydiag = _bdot(f["m0"], f["xdt0"]) + _bdot(f["m1"], f["xdt1"])
            y_ref[:, sl] = ydiag + f["yoff"] + d_ref[:, sl] * x
            prev_ref[0, p] = hs
            state[p] = f["cd"] * hs + _bdot(f["xdd"], bm, "tn")

    def whole(a):
        return pl.BlockSpec(a.shape, lambda i: (0, 0))

    return pl.pallas_call(
        kern, name=name, grid=(nc,),
        in_specs=[pl.BlockSpec((q, c), lambda i: (i, 0)), pl.BlockSpec((q, LANES), lambda i: (i, 0)),
                  whole(dtb), whole(arow), whole(dvec)],
        out_specs=[pl.BlockSpec((q, di), lambda i: (i, 0)),
                   pl.BlockSpec((1, n_pairs, 2 * M2_P, n), lambda i: (i, 0, 0, 0))],
        out_shape=[jax.ShapeDtypeStruct((rows, di), F32),
                   jax.ShapeDtypeStruct((nc, n_pairs, 2 * M2_P, n), F32)],
        scratch_shapes=[pltpu.VMEM((n_pairs, 2 * M2_P, n), F32), pltpu.VMEM((LANES, q), F32),
                        pltpu.VMEM((LANES, q), F32), pltpu.VMEM((LANES, q), F32)],
        compiler_params=_params("arbitrary"),
    )(xbc, dtraw, dtb, arow, dvec)


def _ssd_bwd(name, xbc, dtraw, dy, prev, dtb, arow, dvec, seg, n_heads):
    rows, c = xbc.shape
    q, n = M2_Q, M2_N
    di = n_heads * M2_P
    n_pairs = n_heads // 2
    ppg = n_pairs // M2_G
    nc = rows // q

    def kern(xbc_ref, dtraw_ref, dy_ref, prev_ref, dtb_ref, a_ref, d_ref, seg_ref,
             dxbc_ref, ddt_ref, da_ref, ddtb_ref, dd_ref,
             dstate, cst, dtt, lastt, dcst, wx, colterm, ddfull):
        step = pl.program_id(0)

        @pl.when(step == 0)
        def _():
            dstate[...] = jnp.zeros_like(dstate)
            ddfull[...] = jnp.zeros_like(ddfull)
            da_ref[...] = jnp.zeros_like(da_ref)
            ddtb_ref[...] = jnp.zeros_like(ddtb_ref)
            dd_ref[...] = jnp.zeros_like(dd_ref)

        dt = _ssd_chunk_prep(dtraw_ref, dtb_ref, a_ref, cst, dtt, lastt, n_heads)
        dcst[...] = jnp.zeros_like(dcst)
        lane_q = lax.broadcasted_iota(jnp.int32, (1, q), 1)
        last_hot = jnp.where(lane_q == q - 1, 1.0, 0.0)

        def total(v):
            return jnp.sum(jnp.sum(v, axis=1, keepdims=True), axis=0, keepdims=True)

        for gi in range(M2_G):
            bm = xbc_ref[:, di + gi * n:di + (gi + 1) * n]
            cm = xbc_ref[:, di + (M2_G + gi) * n:di + (M2_G + gi + 1) * n]
            cb = _bdot(cm, bm, "nt")
            dcb = jnp.zeros((q, q), F32)
            dbm = jnp.zeros((q, n), F32)
            dcm = jnp.zeros((q, n), F32)
            for p in range(gi * ppg, (gi + 1) * ppg):
                sl = slice(p * LANES, (p + 1) * LANES)
                x = xbc_ref[:, sl]
                dyp = dy_ref[:, sl]
                hs = prev_ref[0, p]
                ds = dstate[p]
                f = _ssd_pair_fwd(x, bm, cm, cb, hs, _pair_tables(cst, dtt, lastt, p))
                lo, slo = f["lo"], f["slo"]
                ddfull[:, sl] += _colsum(dyp * x)
                dy0 = jnp.where(lo, dyp, 0.0)
                dy1 = jnp.where(lo, 0.0, dyp)
                dm0 = _bdot(dyp, f["xdt0"], "nt")
                dm1 = _bdot(dyp, f["xdt1"], "nt")
                dxdt = _bdot(f["m0"], dy0, "tn") + _bdot(f["m1"], dy1, "tn")
                dcb = dcb + dm0 * f["l0"] + dm1 * f["l1"]
                w0, w1 = dm0 * f["m0"], dm1 * f["m1"]
                dz = dyp * f["e"]
                dcm = dcm + _bdot(dz, hs)
                dhs = _bdot(dz, cm, "tn") + f["cd"] * ds
                tot = ds * hs * f["cd"]
                dxdd = _bdot(bm, ds, "nt")
                dbm = dbm + _bdot(f["xdd"], ds)
                ee = dxdd * f["xdd"]
                colterm[:, sl] = dyp * f["yoff"] - ee
                dxdt = dxdt + dxdd * f["dec"]
                t_all = total(tot)
                t_lo = total(jnp.where(slo, tot, 0.0))
                e_all = total(ee)
                e_lo = total(jnp.where(lo, ee, 0.0))
                dlast0 = t_lo + e_lo
                dlast1 = (t_all - t_lo) + (e_all - e_lo)
                dcst[2 * p:2 * p + 1, :] = _colsum(w0.T - w0) + dlast0 * last_hot
                dcst[2 * p + 1:2 * p + 2, :] = _colsum(w1.T - w1) + dlast1 * last_hot
                dxbc_ref[:, sl] = d_ref[:, sl] * dyp + dxdt * f["dtp"]
                wx[:, sl] = dxdt * x
                dstate[p] = dhs
            dcm = dcm + _bdot(dcb, bm)
            dbm = dbm + _bdot(dcb, cm, "tn")
            dxbc_ref[:, di + gi * n:di + (gi + 1) * n] = dbm
            dxbc_ref[:, di + (M2_G + gi) * n:di + (M2_G + gi + 1) * n] = dcm

        segv = seg_ref[...]
        dcs = _dot(colterm[...], segv, precision=HIGHEST) + dcst[...].T
        row = lax.broadcasted_iota(jnp.int32, (q, q), 0)
        col = lax.broadcasted_iota(jnp.int32, (q, q), 1)
        ddelta = _dot(jnp.where(col >= row, 1.0, 0.0), dcs, precision=HIGHEST)
        ddt = _dot(wx[...], segv, precision=HIGHEST) + ddelta * a_ref[...]
        da_ref[...] += _colsum(ddelta * dt)
        lane = lax.broadcasted_iota(jnp.int32, (q, LANES), 1)
        ddtraw = jnp.where(lane < n_heads, ddt * _sigmoid(dtraw_ref[...] + dtb_ref[...]), 0.0)
        ddt_ref[...] = ddtraw
        ddtb_ref[...] += _colsum(ddtraw)

        @pl.when(step == nc - 1)
        def _():
            dd_ref[...] = _dot(jnp.broadcast_to(ddfull[...], (8, di)), segv, precision=HIGHEST)

    def whole(a):
        return pl.BlockSpec(a.shape, lambda i: (0, 0))

    def rev(i):
        return nc - 1 - i

    acc = jax.ShapeDtypeStruct((1, LANES), F32)
    acc_spec = pl.BlockSpec((1, LANES), lambda i: (0, 0))
    acc8 = jax.ShapeDtypeStruct((8, LANES), F32)
    acc8_spec = pl.BlockSpec((8, LANES), lambda i: (0, 0))
    return pl.pallas_call(
        kern, name=name, grid=(nc,),
        in_specs=[pl.BlockSpec((q, c), lambda i: (rev(i), 0)), pl.BlockSpec((q, LANES), lambda i: (rev(i), 0)),
                  pl.BlockSpec((q, di), lambda i: (rev(i), 0)),
                  pl.BlockSpec((1, n_pairs, 2 * M2_P, n), lambda i: (rev(i), 0, 0, 0)),
                  whole(dtb), whole(arow), whole(dvec), whole(seg)],
        out_specs=[pl.BlockSpec((q, c), lambda i: (rev(i), 0)), pl.BlockSpec((q, LANES), lambda i: (rev(i), 0)),
                   acc_spec, acc_spec, acc8_spec],
        out_shape=[jax.ShapeDtypeStruct((rows, c), F32), jax.ShapeDtypeStruct((rows, LANES), F32), acc, acc, acc8],
        scratch_shapes=[pltpu.VMEM((n_pairs, 2 * M2_P, n), F32), pltpu.VMEM((LANES, q), F32),
                        pltpu.VMEM((LANES, q), F32), pltpu.VMEM((LANES, q), F32), pltpu.VMEM((LANES, q), F32),
                        pltpu.VMEM((q, di), F32), pltpu.VMEM((q, di), F32), pltpu.VMEM((1, di), F32)],
        compiler_params=_params("arbitrary"),
    )(xbc, dtraw, dy, prev, dtb, arow, dvec, seg)


S5_PARAM_NAMES = ("s5_lambda_re", "s5_lambda_im", "s5_log_dt", "s5_b_re", "s5_b_im", "s5_c_re", "s5_c_im")


def _row(v):
    return v.reshape(1, -1)


class _Rides:
    def __init__(self):
        self.pending = {}
        self.landed = {}

    def matmul(self, site, name, *args, **kw):
        ride = self.pending.pop(site, None)
        res = _matmul(name, *args, ride=ride, **kw)
        if ride is None:
            return res
        self.landed[site] = list(res[-len(ride):])
        res = list(res[:-len(ride)])
        return res[0] if len(res) == 1 else res


def _s5_layer_fwd(tag, x, gate, h, w, j, rides):
    u = rides.matmul("s5_win", tag + "_win", h, w["s5_w_in"][j])
    params = [w[k][j] for k in S5_PARAM_NAMES]
    ops, build_vjp = jax.vjp(_s5_build, *params)
    ug = _s5_to_groups(u.astype(BF16), True)
    yg, sprev = _s5_core_fwd(tag + "_core", ug, ops)
    yy = _s5_from_groups(yg, False)
    skip = _row(w["s5_d"][j])
    gl = _gelu_fwd(tag + "_gelu", yy, u, skip)
    ab = rides.matmul("s5_wglu", tag + "_wglu", gl, w["s5_w_glu"][j])
    x1 = _glu_fwd(tag + "_glu", ab, _row(w["s5_b_glu"][j]), x, gate)
    return x1, dict(u=u, ug=ug, ops=ops, build_vjp=build_vjp, sprev=sprev, yy=yy, gl=gl, ab=ab, skip=skip)


def _s5_layer_bwd(tag, dx1, gate, h, sv, w, j, rides):
    dab, db_glu, dgate = _glu_bwd(tag + "_glu_b", sv["ab"], _row(w["s5_b_glu"][j]), dx1, gate)
    dw_glu = rides.matmul("s5_dwglu", tag + "_dwglu", sv["gl"], dab, "tn", out_dtype=BF16)
    dgl = rides.matmul("s5_dgl", tag + "_dgl", dab, w["s5_w_glu"][j], "nt")
    dyy, dskip = _gelu_bwd(tag + "_gelu_b", sv["yy"], sv["u"], sv["skip"], dgl)
    dug, dkc, dbend, dcpow, da1, da2s = _s5_core_bwd(
        tag + "_core_b", sv["ug"], _s5_to_groups(dyy.astype(BF16), False), sv["sprev"], sv["ops"])
    dparams = sv["build_vjp"]((dkc, dbend, dcpow, da1, _swap_halves(da2s, 2)))
    du = _axpy(tag + "_du", _s5_from_groups(dug, True), dyy, sv["skip"])
    grads = dict(zip(S5_PARAM_NAMES, dparams))
    grads["s5_d"] = dskip.reshape(-1)
    grads["s5_w_in"] = _matmul(tag + "_dwin", h, du, "tn", out_dtype=BF16)
    grads["s5_w_glu"] = dw_glu
    grads["s5_b_glu"] = db_glu.reshape(-1)
    dh = _matmul(tag + "_dh", du, w["s5_w_in"][j], "nt")
    return dh, grads, dgate


def _ssd_consts(w, j, d_model):
    di = 2 * d_model
    heads = di // M2_P

    def pad_row(v):
        return jnp.zeros((1, LANES), F32).at[0, :heads].set(v)

    a = -jnp.exp(w["m2_a_log"][j])
    seg = (jnp.arange(di)[:, None] // M2_P == jnp.arange(LANES)[None, :]).astype(F32)
    w_in = w["m2_w_in"][j]
    conv_dim = di + 2 * M2_G * M2_N
    w_dt = jnp.zeros((d_model, LANES), w_in.dtype).at[:, :heads].set(w_in[:, di + conv_dim:])
    return dict(di=di, heads=heads, conv_dim=conv_dim, a=a, arow=pad_row(a), dtb=pad_row(w["m2_dt_bias"][j]),
                dvec=_row(jnp.repeat(w["m2_d"][j], M2_P)), seg=seg,
                w_z=w_in[:, :di], w_xbc=w_in[:, di:di + conv_dim], w_dt=w_dt,
                conv_w=w["m2_conv_w"][j], conv_b=_row(w["m2_conv_b"][j]), norm_g=_row(w["m2_norm_g"][j]))


def _gated_out_bwd(tag, act, dxo, w_out, gate, **kw):
    dw, dgate_parts = _matmul(tag + "_dwo", act, dxo, "tn", out_dtype=BF16, colscale=gate, colsum_with=w_out, **kw)
    dgate = jnp.sum(dgate_parts, axis=0)
    return dw, dgate, _scale_cols(tag + "_wog", w_out, gate)


def _ssd_layer_fwd(tag, x, gate, h, w, j):
    k = _ssd_consts(w, j, h.shape[1])
    z = _matmul(tag + "_wz", h, k["w_z"])
    xbc_pre = _matmul(tag + "_wxbc", h, k["w_xbc"])
    dtraw = _matmul(tag + "_wdt", h, k["w_dt"])
    xbc = _conv_fwd(tag + "_conv", xbc_pre, k["conv_w"], k["conv_b"])
    y, prev = _ssd_fwd(tag + "_core", xbc, dtraw, k["dtb"], k["arow"], k["dvec"], k["heads"])
    yn = _gatenorm_fwd(tag + "_gn", y, z, k["norm_g"])
    x1 = _matmul(tag + "_wout", yn, w["m2_w_out"][j], colscale=gate, addin=x)
    return x1, dict(k=k, z=z, xbc_pre=xbc_pre, dtraw=dtraw, xbc=xbc, y=y, prev=prev, yn=yn)


def _ssd_layer_bwd(tag, dx1_b, gate, h, sv, w, j):
    k = sv["k"]
    heads = k["heads"]
    dw_out, dgate, wog = _gated_out_bwd(tag, sv["yn"], dx1_b, w["m2_w_out"][j], gate)
    grads = {"m2_w_out": dw_out}
    dyn = _matmul(tag + "_dyn", dx1_b, wog, "nt")
    dyssd, dz, dng = _gatenorm_bwd(tag + "_gn_b", sv["y"], sv["z"], k["norm_g"], dyn)
    dxbc, ddtraw, da, ddtb, dd = _ssd_bwd(tag + "_core_b", sv["xbc"], sv["dtraw"], dyssd, sv["prev"],
                                          k["dtb"], k["arow"], k["dvec"], k["seg"], heads)
    dpre, dcw, dcb = _conv_bwd_pre(tag + "_conv_b1", sv["xbc_pre"], dxbc, k["conv_w"], k["conv_b"])
    dxbc_pre = _conv_bwd_in(tag + "_conv_b2", dpre, k["conv_w"])
    dw_z = _matmul(tag + "_dwz", h, dz, "tn", out_dtype=BF16)
    dw_xbc = _matmul(tag + "_dwxbc", h, dxbc_pre, "tn", out_dtype=BF16)
    dw_dt = _matmul(tag + "_dwdt", h, ddtraw, "tn", out_dtype=BF16)
    dh = _matmul(tag + "_dh1", dz, k["w_z"], "nt")
    dh = _matmul(tag + "_dh2", dxbc_pre, k["w_xbc"], "nt", addin=dh)
    dh = _matmul(tag + "_dh3", ddtraw, k["w_dt"], "nt", addin=dh)
    grads["m2_w_in"] = jnp.concatenate([dw_z, dw_xbc, dw_dt[:, :heads]], axis=1)
    grads["m2_conv_w"] = dcw
    grads["m2_conv_b"] = dcb.reshape(-1)
    grads["m2_dt_bias"] = ddtb[0, :heads]
    grads["m2_a_log"] = da[0, :heads] * k["a"]
    grads["m2_d"] = dd[0, :heads]
    grads["m2_norm_g"] = dng.reshape(-1)
    return dh, grads, dgate


def _layer_fwd(li, x, mod, w, rides, late_weights=None):
    tag = "L%d" % li
    sh1, sc1, g1, sh2, sc2, g2 = mod
    j = li // 2
    h = _normmod_fwd(tag + "_nm1", x, _row(w["norm_mix_g"][li]), sh1, sc1)
    if li % 2 == 0:
        x1, mix = _s5_layer_fwd(tag + "_s5", x, g1, h, w, j, rides)
    else:
        x1, mix = _ssd_layer_fwd(tag + "_m2", x, g1, h, w, j)
    if late_weights is not None:
        w = {**w, **late_weights(rides.landed)}
    h2 = _normmod_fwd(tag + "_nm2", x1, _row(w["norm_mlp_g"][li]), sh2, sc2)
    r = rides.matmul("w1", tag + "_w1", h2, w["mlp_w1"][li], relu=True, out_dtype=BF16)
    x2 = rides.matmul("w2", tag + "_w2", r, w["mlp_w2"][li], square_a=True, colscale=g2, addin=x1)
    return x2, dict(x=x, h=h, mix=mix, x1=x1, h2=h2, r=r), w


def _layer_bwd(li, dx2, dx2_b, sv, mod, w, rides, ride_own_mlp):
    tag = "L%d" % li
    sh1, sc1, g1, sh2, sc2, g2 = mod
    j = li // 2
    dw2, dg2, w2g = _gated_out_bwd(tag + "_mlp", sv["r"], dx2_b, w["mlp_w2"][li], g2, square_a=True)
    dr = rides.matmul("dr", tag + "_dr", dx2_b, w2g, "nt", out_dtype=BF16, mul2=sv["r"])
    dw1 = _matmul(tag + "_dw1", sv["h2"], dr, "tn", out_dtype=BF16)
    grads = {"mlp_w2": dw2, "mlp_w1": dw1}
    dh2 = rides.matmul("dh2", tag + "_dh2", dr, w["mlp_w1"][li], "nt")
    if ride_own_mlp:
        mlp_bufs = _grad_buffers(li, grads, parts=[1])[0]
        rides.pending["s5_dgl"], rides.pending["s5_dwglu"] = mlp_bufs[:1], mlp_bufs[1:]
    dx1, dx1_b, dgm, dsh2, dsc2 = _normmod_bwd(tag + "_nm2_b", sv["x1"], _row(w["norm_mlp_g"][li]), sh2, sc2, dh2,
                                               dx2)
    if li % 2 == 0:
        dh, mix_grads, dg1 = _s5_layer_bwd(tag + "_s5", dx1, g1, sv["h"], sv["mix"], w, j, rides)
    else:
        dh, mix_grads, dg1 = _ssd_layer_bwd(tag + "_m2", dx1_b, g1, sv["h"], sv["mix"], w, j)
    dx, dx_b, dgx, dsh1, dsc1 = _normmod_bwd(tag + "_nm1_b", sv["x"], _row(w["norm_mix_g"][li]), sh1, sc1, dh, dx1)
    grads["norm_mix_g"] = dgx.reshape(-1)
    grads["norm_mlp_g"] = dgm.reshape(-1)
    dmod = jnp.concatenate([dsh1, dsc1, dg1, dsh2, dsc2, dg2], axis=1)
    return dx, dx_b, {**grads, **mix_grads}, dmod


def _layer_parts(li):
    j = li // 2
    mix = [("s5_w_in", j, 0), ("s5_w_glu", j, 1)] if li % 2 == 0 else [("m2_w_in", j, 1), ("m2_w_out", j, 0)]
    return [mix, [("mlp_w1", li, 1), ("mlp_w2", li, 0)]]


def _grad_specs(li):
    j = li // 2
    mlp = [("one", ("mlp_w2", li, 0)), ("one", ("mlp_w1", li, 1))]
    if li % 2 == 0:
        return [[("one", ("s5_w_in", j, 0)), ("one", ("s5_w_glu", j, 1))], mlp]
    packed = [("m2_w_in", j, 1), ("m2_conv_w", j, 1), ("m2_conv_b", j, 0), ("m2_norm_g", j, 0)]
    return [[("one", ("m2_w_out", j, 0)), ("packed", packed)], mlp]


def _grad_buffers(li, grads, parts=(0, 1)):
    out = []
    for k in parts:
        bufs = []
        for kind, entry in _grad_specs(li)[k]:
            if kind == "one":
                bufs.append((grads[entry[0]], "rows" if entry[2] == 0 else "cols"))
            else:
                pieces = [_pack([_chip_slice(grads[n], c, ax) for n, _, ax in entry], BF16) for c in range(N_CHIP)]
                bufs.append((jnp.stack(pieces), "packed"))
        out.append(bufs)
    return out


def _gather_buffers(local, part):
    return [(local[n][i].astype(BF16), "same") for n, i, _ in part]


def _assemble(landed, part):
    out = {}
    for buf, (n, i, ax) in zip(landed, part):
        if ax == 0:
            out[n] = {i: buf.reshape(-1, buf.shape[2])}
        else:
            out[n] = {i: jnp.concatenate([buf[k] for k in range(N_CHIP)], axis=1)}
    return out


def _local_step(x, target, mods, w, local):
    depth = w["norm_mix_g"].shape[0]
    d = x.shape[1]
    rides = _Rides()
    saved, mod_rows, layer_w = [], [], []
    mix0, mlp0 = _layer_parts(0)
    wl = {**w, **_assemble(_exchange4("ag_w_L0", _gather_buffers(local, mix0)), mix0)}
    rides.pending["s5_win"] = _gather_buffers(local, mlp0[:1])
    rides.pending["s5_wglu"] = _gather_buffers(local, mlp0[1:])

    def late_mlp0(landed):
        return _assemble(landed.pop("s5_win") + landed.pop("s5_wglu"), mlp0)

    for li in range(depth):
        if li + 1 < depth:
            nxt = _layer_parts(li + 1)
            rides.pending["w1"] = _gather_buffers(local, nxt[0])
            rides.pending["w2"] = _gather_buffers(local, nxt[1])
        mod = [mods[li:li + 1, i * d:(i + 1) * d] for i in range(N_MOD)]
        mod_rows.append(mod)
        x, sv, wl = _layer_fwd(li, x, mod, wl, rides, late_mlp0 if li == 0 else None)
        saved.append(sv)
        layer_w.append(wl)
        if li + 1 < depth:
            wl = {**w, **_assemble(rides.landed.pop("w1"), nxt[0]), **_assemble(rides.landed.pop("w2"), nxt[1])}
    dx, dx_b, dgf, loss = _loss_head("loss_head", x, target, _row(w["final_norm_g"]))
    layer_grads = [None] * depth
    dmods = [None] * depth
    landed = {}
    for li in reversed(range(depth)):
        dx, dx_b, layer_grads[li], dmods[li] = _layer_bwd(li, dx, dx_b, saved[li], mod_rows[li], layer_w[li], rides,
                                                          li == 0)
        if li + 1 < depth:
            landed[(li + 1, 0)] = rides.landed.pop("dr")
            landed[(li + 1, 1)] = rides.landed.pop("dh2")
        if li > 0:
            rides.pending["dr"], rides.pending["dh2"] = _grad_buffers(li, layer_grads[li])
    landed[(0, 1)] = rides.landed.pop("s5_dgl") + rides.landed.pop("s5_dwglu")
    landed[(0, 0)] = _exchange4("rs_g_L0", _grad_buffers(0, layer_grads[0], parts=[0])[0])
    return loss, dx, layer_grads, dgf.reshape(-1), jnp.concatenate(dmods, axis=0), landed


ANY = pl.BlockSpec(memory_space=pl.ANY)
N_DEV = 8
N_CHIP = 4


def _coords():
    return lax.axis_index("x"), lax.axis_index("y"), lax.axis_index("c")


def _allgather8(name, block):
    r, wd = block.shape

    def body(x_ref, out_ref, send_sems, recv_sems, local_sem):
        x, y, c = _coords()
        me, sibling = (x, y, c), (x, y, 1 - c)
        chips = [(1 - x, y), (x, 1 - y), (1 - x, 1 - y)]

        def slot(px, py, pc):
            return out_ref.at[4 * px + 2 * py + pc]

        def copy(k, blk, to, src=None):
            return pltpu.make_async_remote_copy(
                src_ref=slot(*blk) if src is None else src, dst_ref=slot(*blk),
                send_sem=send_sems.at[k], recv_sem=recv_sems.at[k], device_id=to, device_id_type=MESH)

        mine = pltpu.make_async_copy(x_ref, slot(*me), local_sem)
        mine.start()
        first = [copy(0, me, sibling, src=x_ref)]
        first += [copy(1 + j, me, (*chip, c), src=x_ref) for j, chip in enumerate(chips)]
        for cp in first:
            cp.start()
        passed = [copy(4 + j, (*chip, c), sibling) for j, chip in enumerate(chips)]
        for j, chip in enumerate(chips):
            copy(1 + j, (*chip, c), me).wait_recv()
            passed[j].start()
        copy(0, sibling, me).wait_recv()
        for j, chip in enumerate(chips):
            copy(4 + j, (*chip, 1 - c), me).wait_recv()
        for cp in first + passed:
            cp.wait_send()
        mine.wait()

    return pl.pallas_call(
        body, name=name, in_specs=[ANY], out_specs=ANY,
        out_shape=jax.ShapeDtypeStruct((N_DEV, r, wd), block.dtype),
        scratch_shapes=[pltpu.SemaphoreType.DMA((7,)), pltpu.SemaphoreType.DMA((7,)), pltpu.SemaphoreType.DMA],
    )(block)


def _landing_shape(src, kind):
    if kind == "same":
        return (N_CHIP,) + src.shape
    if kind == "packed":
        return src.shape
    rows, cols = src.shape
    return (N_CHIP, rows // N_CHIP, cols) if kind == "rows" else (N_CHIP, rows, cols // N_CHIP)


def _exchange4_ops(srcs, dsts, send_sems, recv_sems, local_sems, kinds):
    x, y, c = _coords()
    my_chip = 2 * x + y
    chips = [(1 - x, y), (x, 1 - y), (1 - x, 1 - y)]

    def piece(q, k):
        ref, kind = srcs[q], kinds[q]
        if kind == "same":
            return ref
        if kind == "packed":
            return ref.at[k]
        _, rows, cols = dsts[q].shape
        return ref.at[pl.ds(k * rows, rows), :] if kind == "rows" else ref.at[:, pl.ds(k * cols, cols)]

    def copy(q, j, k, slot):
        px, py = chips[j]
        return pltpu.make_async_remote_copy(
            src_ref=piece(q, k), dst_ref=dsts[q].at[slot], send_sem=send_sems.at[3 * q + j],
            recv_sem=recv_sems.at[3 * q + j], device_id=(px, py, c), device_id_type=MESH)

    def mine(q):
        return pltpu.make_async_copy(piece(q, my_chip), dsts[q].at[my_chip], local_sems.at[q])

    def start():
        for q in range(len(srcs)):
            mine(q).start()
            for j, (px, py) in enumerate(chips):
                copy(q, j, 2 * px + py, my_chip).start()

    def wait():
        for q in range(len(srcs)):
            for j, (px, py) in enumerate(chips):
                copy(q, j, my_chip, 2 * px + py).wait_recv()
        for q in range(len(srcs)):
            for j, (px, py) in enumerate(chips):
                copy(q, j, 2 * px + py, my_chip).wait_send()
            mine(q).wait()

    return start, wait


def _exchange_scratch(n):
    return [pltpu.SemaphoreType.DMA((3 * n,)), pltpu.SemaphoreType.DMA((3 * n,)), pltpu.SemaphoreType.DMA((n,))]


def _exchange4(name, buffers):
    n = len(buffers)
    kinds = [kind for _, kind in buffers]

    def body(*refs):
        start, wait = _exchange4_ops(refs[:n], refs[n:2 * n], *refs[2 * n:], kinds)
        start()
        wait()

    return pl.pallas_call(
        body, name=name, in_specs=[ANY] * n, out_specs=[ANY] * n,
        out_shape=[jax.ShapeDtypeStruct(_landing_shape(s, k), s.dtype) for s, k in buffers],
        scratch_shapes=_exchange_scratch(n),
    )(*[s for s, _ in buffers])


def _swap_sibling(name, block):
    def body(x_ref, out_ref, send_sem, recv_sem):
        x, y, c = _coords()
        cp = pltpu.make_async_remote_copy(src_ref=x_ref, dst_ref=out_ref, send_sem=send_sem, recv_sem=recv_sem,
                                          device_id=(x, y, 1 - c), device_id_type=MESH)
        cp.start()
        cp.wait()

    return pl.pallas_call(
        body, name=name, in_specs=[ANY], out_specs=ANY, out_shape=jax.ShapeDtypeStruct(block.shape, block.dtype),
        scratch_shapes=[pltpu.SemaphoreType.DMA, pltpu.SemaphoreType.DMA],
    )(block)


def _sum_slots(name, stacked):
    n, r, wd = stacked.shape
    tile = min(FLAT_ROWS, r)

    def kern(x_ref, o_ref):
        acc = x_ref[0].astype(F32)
        for s in range(1, n):
            acc = acc + x_ref[s].astype(F32)
        o_ref[...] = acc

    return pl.pallas_call(
        kern, name=name, grid=(r // tile,), in_specs=[pl.BlockSpec((n, tile, wd), lambda i: (0, i, 0))],
        out_specs=pl.BlockSpec((tile, wd), lambda i: (i, 0)), out_shape=jax.ShapeDtypeStruct((r, wd), F32),
        compiler_params=_params("parallel"),
    )(stacked)


def _adamw(name, w, m, v, g, g2=None):
    r, wd = w.shape
    grads = [g] if g2 is None else [g, g2]
    c1 = 1.0 - ADAM_B1 ** ADAM_STEP
    c2 = 1.0 - ADAM_B2 ** ADAM_STEP

    def body(i, *refs):
        w_ref, m_ref, v_ref = refs[:3]
        g_refs = refs[3:3 + len(grads)]
        go_ref, d_ref, mo_ref, vo_ref = refs[3 + len(grads):]
        gv = g_refs[0][...]
        if g2 is not None:
            gv = gv + g_refs[1][...]
        mn = ADAM_B1 * m_ref[...] + (1.0 - ADAM_B1) * gv
        vn = ADAM_B2 * v_ref[...] + (1.0 - ADAM_B2) * (gv * gv)
        go_ref[...] = gv
        mo_ref[...] = mn
        vo_ref[...] = vn
        d_ref[...] = -ADAM_LR * ((mn / c1) / (jnp.sqrt(vn / c2) + ADAM_EPS) + ADAM_WD * w_ref[...])

    return _rowcall(name, body, r, FLAT_ROWS, [w, m, v] + grads, [], [(wd, F32)] * 4, [])


FLAT_BLOCK = FLAT_ROWS * FLAT_W


def _pack(arrays, dtype):
    flat = jnp.concatenate([a.reshape(-1).astype(dtype) for a in arrays])
    pad = (-flat.shape[0]) % FLAT_BLOCK
    return jnp.pad(flat, (0, pad)).reshape(-1, FLAT_W)


def _unpack(buf, shapes):
    flat = buf.reshape(-1)
    out, off = [], 0
    for s in shapes:
        n = math.prod(s)
        out.append(flat[off:off + n].reshape(s))
        off += n
    return out


SHARDED_BIG = {"mlp_w1": 2, "mlp_w2": 1, "s5_w_in": 1, "s5_w_glu": 2, "m2_w_in": 2, "m2_w_out": 1}
SHARDED_SMALL = {"m2_conv_w": 2, "m2_conv_b": 1, "m2_norm_g": 1}
REPLICATED = ("ada_b", "norm_mix_g", "norm_mlp_g", "s5_lambda_re", "s5_lambda_im", "s5_log_dt", "s5_b_re",
              "s5_b_im", "s5_c_re", "s5_c_im", "s5_d", "s5_b_glu", "m2_dt_bias", "m2_a_log", "m2_d", "final_norm_g")
WEIGHT_NAMES = ("ada_w", "ada_b", "norm_mix_g", "norm_mlp_g", "mlp_w1", "mlp_w2", "s5_w_in", "s5_lambda_re",
                "s5_lambda_im", "s5_log_dt", "s5_b_re", "s5_b_im", "s5_c_re", "s5_c_im", "s5_d", "s5_w_glu",
                "s5_b_glu", "m2_w_in", "m2_conv_w", "m2_conv_b", "m2_dt_bias", "m2_a_log", "m2_d", "m2_norm_g",
                "m2_w_out", "final_norm_g")


def _gather_weights(name, local, names_axes, dtype):
    names = list(names_axes)
    got = _exchange4(name, [(_pack([local[k] for k in names], dtype), "same")])[0]
    per_chip = [_unpack(got[j], [local[k].shape for k in names]) for j in range(N_CHIP)]
    return {k: jnp.concatenate([per_chip[j][i] for j in range(N_CHIP)], axis=names_axes[k])
            for i, k in enumerate(names)}


def _chip_slice(a, chip, axis):
    size = a.shape[axis] // N_CHIP
    return lax.slice_in_dim(a, chip * size, (chip + 1) * size, axis=axis)


def kernel(x, c, ada_w, ada_b, norm_mix_g, norm_mlp_g, mlp_w1, mlp_w2, s5_w_in, s5_lambda_re, s5_lambda_im, s5_log_dt, s5_b_re, s5_b_im, s5_c_re, s5_c_im, s5_d, s5_w_glu, s5_b_glu, m2_w_in, m2_conv_w, m2_conv_b, m2_dt_bias, m2_a_log, m2_d, m2_norm_g, m2_w_out, final_norm_g, loss_target, m_ada_w, m_ada_b, m_norm_mix_g, m_norm_mlp_g, m_mlp_w1, m_mlp_w2, m_s5_w_in, m_s5_lambda_re, m_s5_lambda_im, m_s5_log_dt, m_s5_b_re, m_s5_b_im, m_s5_c_re, m_s5_c_im, m_s5_d, m_s5_w_glu, m_s5_b_glu, m_m2_w_in, m_m2_conv_w, m_m2_conv_b, m_m2_dt_bias, m_m2_a_log, m_m2_d, m_m2_norm_g, m_m2_w_out, m_final_norm_g, v_ada_w, v_ada_b, v_norm_mix_g, v_norm_mlp_g, v_mlp_w1, v_mlp_w2, v_s5_w_in, v_s5_lambda_re, v_s5_lambda_im, v_s5_log_dt, v_s5_b_re, v_s5_b_im, v_s5_c_re, v_s5_c_im, v_s5_d, v_s5_w_glu, v_s5_b_glu, v_m2_w_in, v_m2_conv_w, v_m2_conv_b, v_m2_dt_bias, v_m2_a_log, v_m2_d, v_m2_norm_g, v_m2_w_out, v_final_norm_g):
    args = locals()
    local = {k: args[k] for k in WEIGHT_NAMES}
    mom_m = {k: args["m_" + k] for k in WEIGHT_NAMES}
    mom_v = {k: args["v_" + k] for k in WEIGHT_NAMES}
    depth, d = norm_mix_g.shape
    xi, yi, ci = _coords()
    my_chip = 2 * xi + yi
    my_dev = 2 * my_chip + ci

    cond = jax.nn.silu(c).reshape(-1, LANES)
    cond_all = _allgather8("ag_cond", cond).reshape(N_DEV, d)
    cond_pad = jnp.zeros((LANES, d), F32).at[:N_DEV].set(cond_all)
    mod_cols = ada_w.shape[2]
    mod_part = jnp.stack([_matmul("ada_%d" % i, cond_pad, ada_w[i])[:N_DEV] for i in range(depth)])
    mod_all = _allgather8("ag_mod", mod_part.reshape(-1, LANES)).reshape(N_CHIP, 2, depth, N_DEV, mod_cols)[:, 0]
    mod_mine = lax.dynamic_index_in_dim(mod_all, my_dev, axis=2, keepdims=False)
    mods = jnp.transpose(mod_mine, (1, 0, 2)).reshape(depth, N_CHIP * mod_cols) + ada_b

    w = {k: local[k] for k in REPLICATED}
    w.update(_gather_weights("ag_w_small", local, SHARDED_SMALL, F32))

    loss_row, dx, layer_grads, g_final, dmods, landed = _local_step(x[0], loss_target[0], mods, w, local)
    grads = {"ada_b": dmods, "final_norm_g": g_final}
    for k in REPLICATED[1:-1]:
        grads[k] = jnp.stack([g[k] for g in layer_grads if k in g])

    rep_shapes = [grads[k].shape for k in REPLICATED]
    rep_all = _allgather8("ag_grep", _pack([grads[k] for k in REPLICATED], F32))
    rep_sum = _sum_slots("sum_grep", rep_all)
    dmods_all = rep_all.reshape(N_DEV, -1)[:, :dmods.size].reshape(N_DEV, depth, N_CHIP * mod_cols)

    dm_mine = lax.dynamic_slice_in_dim(dmods_all, my_chip * mod_cols, mod_cols, axis=2)
    dm_pad = jnp.zeros((LANES, depth, mod_cols), F32).at[:N_DEV].set(dm_mine)
    g_ada_w = jnp.stack([_matmul("dada_%d" % i, cond_pad, dm_pad[:, i], "tn") for i in range(depth)])

    red = {}
    for li in range(depth):
        for k, part in enumerate(_grad_specs(li)):
            for (kind, entry), land in zip(part, landed[(li, k)]):
                if kind == "one":
                    red[entry[:2]] = _sum_slots("sum_%s_%d" % entry[:2], land)
                else:
                    total = _sum_slots("sum_packed_L%d" % li, land)
                    shapes = [local[n][i].shape for n, i, _ in entry]
                    red.update({(n, i): v for (n, i, _), v in zip(entry, _unpack(total, shapes))})

    out_g, out_d, out_m, out_v = {}, {}, {}, {}
    for name in list(SHARDED_BIG) + list(SHARDED_SMALL):
        shape = local[name].shape
        flat2 = (-1, shape[-1])
        g1 = jnp.stack([red[(name, i)] for i in range(shape[0])]).reshape(flat2)
        g2 = _swap_sibling("swap_" + name, g1)
        res = _adamw("adam_" + name, local[name].reshape(flat2), mom_m[name].reshape(flat2),
                     mom_v[name].reshape(flat2), g1, g2)
        for dst, buf in zip((out_g, out_d, out_m, out_v), res):
            dst[name] = buf.reshape(shape)
    res = _adamw("adam_rep", _pack([local[k] for k in REPLICATED], F32), _pack([mom_m[k] for k in REPLICATED], F32),
                 _pack([mom_v[k] for k in REPLICATED], F32), rep_sum)
    for dst, buf in zip((out_g, out_d, out_m, out_v), res):
        dst.update(zip(REPLICATED, _unpack(buf, rep_shapes)))
    flat2 = (-1, mod_cols)
    res = _adamw("adam_ada", ada_w.reshape(flat2), m_ada_w.reshape(flat2), v_ada_w.reshape(flat2),
                 g_ada_w.reshape(flat2))
    for dst, buf in zip((out_g, out_d, out_m, out_v), res):
        dst["ada_w"] = buf.reshape(ada_w.shape)

    loss = lax.psum(loss_row[0, 0], ("x", "y", "c"))
    outs = [loss, dx[None]]
    for dst in (out_g, out_d, out_m, out_v):
        outs += [dst[k] for k in WEIGHT_NAMES]
    return tuple(outs)
```

```python
import math

import jax
import jax.numpy as jnp
from jax import lax
from jax.experimental import pallas as pl
from jax.experimental.pallas import tpu as pltpu

F32 = jnp.float32
BF16 = jnp.bfloat16
HIGHEST = lax.Precision.HIGHEST

NORM_EPS = 1e-5
N_MOD = 6
S5_H, S5_P, S5_T = 16, 64, 64
M2_P, M2_N, M2_G, M2_Q, M2_K = 64, 128, 4, 128, 4
LANES = 128
ADAM_LR, ADAM_B1, ADAM_B2, ADAM_EPS, ADAM_WD, ADAM_STEP = 0.001, 0.9, 0.999, 1e-08, 0.01, 10
VMEM_LIMIT_BYTES = 56 * 1024 * 1024
ROW_TILE = 256
FLAT_W = 1024
FLAT_ROWS = 256
MESH = pl.DeviceIdType.MESH


def _params(*sem):
    return pltpu.CompilerParams(dimension_semantics=sem, vmem_limit_bytes=VMEM_LIMIT_BYTES)


def _dot(a, b, dn="nn", precision=None):
    dims = {"nn": ((1,), (0,)), "nt": ((1,), (1,)), "tn": ((0,), (0,))}[dn]
    return lax.dot_general(a, b, (dims, ((), ())), preferred_element_type=F32, precision=precision)


def _bdot(a, b, dn="nn"):
    return _dot(a.astype(BF16), b.astype(BF16), dn)


def _sigmoid(x):
    return jax.nn.sigmoid(x)


def _colsum(x):
    return jnp.sum(x, axis=0, keepdims=True)


def _pick_tile(dim, want):
    if dim <= want:
        return dim
    for t in range(want - want % LANES, 0, -LANES):
        if dim % t == 0:
            return t
    raise ValueError((dim, want))


MATMUL_TILE = 1024
MATMUL_VMEM_BUDGET = 40 * 1024 * 1024


def _matmul_tiles(m, n, k, mode, in_bytes, out_bytes):
    tn = _pick_tile(n, MATMUL_TILE)
    k_tiles = [k] + [t for t in (4096, 2048, 1024) if t < k and k % t == 0]
    m_tiles = [_pick_tile(m, MATMUL_TILE)] + ([512] if mode != "tn" and m % 512 == 0 and m > 512 else [])
    for tk in k_tiles:
        for tm in m_tiles:
            blocks = 2 * (tm * tk * in_bytes[0] + tk * tn * in_bytes[1] + tm * tn * out_bytes)
            if blocks + (4 * tm * tn if tk < k else 0) <= MATMUL_VMEM_BUDGET:
                return tm, tn, tk
    raise ValueError((m, n, k))


def _matmul(name, a, b, mode="nn", out_dtype=F32, relu=False, square_a=False, mul2=None, colscale=None,
            addin=None, colsum_with=None, ride=None):
    if mode == "nn":
        (m, k), (k2, n) = a.shape, b.shape
    elif mode == "nt":
        (m, k), (n, k2) = a.shape, b.shape
    else:
        (k, m), (k2, n) = a.shape, b.shape
    assert k == k2, (name, a.shape, b.shape)
    tiles = [e for e in (mul2, addin, colsum_with) if e is not None]
    out_bytes = jnp.dtype(out_dtype).itemsize + sum(e.dtype.itemsize for e in tiles)
    tm, tn, tk = _matmul_tiles(m, n, k, mode, (a.dtype.itemsize, b.dtype.itemsize), out_bytes)
    nk = k // tk
    n_ext = len(tiles) + (colscale is not None)
    n_out = 1 + (colsum_with is not None)
    n_ride = 0 if ride is None else len(ride)
    grid = (m // tm, n // tn, nk)

    def kern(*refs):
        a_ref, b_ref = refs[:2]
        e_refs = list(refs[2:2 + n_ext])
        o_refs = refs[2 + n_ext + n_ride:2 + n_ext + n_ride + n_out]
        kk = pl.program_id(2)
        if ride is not None:
            here = [pl.program_id(ax) for ax in range(3)]
            land0 = 2 + n_ext + n_ride + n_out
            ride_refs = (refs[2 + n_ext:2 + n_ext + n_ride], refs[land0:land0 + n_ride]) + tuple(refs[-3:])
            ride_kinds = [kind for _, kind in ride]
            first = (here[0] == 0) & (here[1] == 0) & (here[2] == 0)
            last = (here[0] == grid[0] - 1) & (here[1] == grid[1] - 1) & (here[2] == grid[2] - 1)

            @pl.when(first)
            def _():
                _exchange4_ops(*ride_refs, ride_kinds)[0]()

        av = a_ref[...]
        if square_a:
            av = av * av
        part = _bdot(av, b_ref[...], mode)

        def finish(r):
            ext = list(e_refs)
            m2v = ext.pop(0)[...].astype(F32) if mul2 is not None else None
            addv = ext.pop(0)[...].astype(F32) if addin is not None else None
            if colsum_with is not None:
                o_refs[1][0] = _colsum(r * ext.pop(0)[...].astype(F32))
            if relu:
                r = jnp.maximum(r, 0.0)
            if m2v is not None:
                r = r * (2.0 * m2v)
            if colscale is not None:
                r = r * ext.pop(0)[...]
            if addv is not None:
                r = r + addv
            o_refs[0][...] = r.astype(out_dtype)

        if nk == 1:
            finish(part)
        else:
            acc = refs[-4] if ride is not None else refs[-1]

            @pl.when(kk == 0)
            def _():
                acc[...] = part

            @pl.when(kk > 0)
            def _():
                acc[...] += part

            @pl.when(kk == nk - 1)
            def _():
                finish(acc[...])

        if ride is not None:
            @pl.when(last)
            def _():
                _exchange4_ops(*ride_refs, ride_kinds)[1]()

    if mode == "tn":
        a_spec = pl.BlockSpec((tk, tm), lambda i, j, kk: (kk, i))
    else:
        a_spec = pl.BlockSpec((tm, tk), lambda i, j, kk: (i, kk))
    if mode == "nt":
        b_spec = pl.BlockSpec((tn, tk), lambda i, j, kk: (j, kk))
    else:
        b_spec = pl.BlockSpec((tk, tn), lambda i, j, kk: (kk, j))
    o_spec = pl.BlockSpec((tm, tn), lambda i, j, kk: (i, j))
    in_specs = [a_spec, b_spec] + [o_spec] * len(tiles)
    operands = [a, b] + tiles
    if colscale is not None:
        in_specs.append(pl.BlockSpec((1, tn), lambda i, j, kk: (0, j)))
        operands.append(colscale)
    out_specs = [o_spec]
    out_shape = [jax.ShapeDtypeStruct((m, n), out_dtype)]
    if colsum_with is not None:
        out_specs.append(pl.BlockSpec((1, 1, tn), lambda i, j, kk: (i, 0, j)))
        out_shape.append(jax.ShapeDtypeStruct((m // tm, 1, n), F32))
    scratch = [pltpu.VMEM((tm, tn), F32)] if nk > 1 else []
    semantics = ("parallel", "parallel", "arbitrary")
    if ride is not None:
        for src, kind in ride:
            in_specs.append(pl.BlockSpec(memory_space=pl.ANY))
            operands.append(src)
            out_specs.append(pl.BlockSpec(memory_space=pl.ANY))
            out_shape.append(jax.ShapeDtypeStruct(_landing_shape(src, kind), src.dtype))
        scratch += _exchange_scratch(n_ride)
        semantics = ("arbitrary", "arbitrary", "arbitrary")
    res = pl.pallas_call(
        kern, name=name, grid=grid, in_specs=in_specs, out_specs=out_specs, out_shape=out_shape,
        scratch_shapes=scratch, compiler_params=_params(*semantics),
    )(*operands)
    return res if len(res) > 1 else res[0]


def _rowcall(name, body, rows, tile, row_ins, small_ins, row_outs, acc_outs):
    tile = min(tile, rows)
    assert rows % tile == 0, (name, rows, tile)
    n_in = len(row_ins) + len(small_ins)

    def kern(*refs):
        i = pl.program_id(0)
        accs = refs[n_in + len(row_outs):]

        @pl.when(i == 0)
        def _():
            for acc in accs:
                acc[...] = jnp.zeros_like(acc)

        body(i, *refs)

    def whole(shape):
        return pl.BlockSpec(shape, lambda i, nd=len(shape): (0,) * nd)

    in_specs = [pl.BlockSpec((tile, a.shape[1]), lambda i: (i, 0)) for a in row_ins]
    in_specs += [whole(a.shape) for a in small_ins]
    out_specs = [pl.BlockSpec((tile, w), lambda i: (i, 0)) for (w, _) in row_outs]
    out_specs += [whole(s) for s in acc_outs]
    out_shape = [jax.ShapeDtypeStruct((rows, w), dt) for (w, dt) in row_outs]
    out_shape += [jax.ShapeDtypeStruct(s, F32) for s in acc_outs]
    return pl.pallas_call(
        kern, name=name, grid=(rows // tile,), in_specs=in_specs, out_specs=out_specs, out_shape=out_shape,
        compiler_params=_params("arbitrary"),
    )(*row_ins, *small_ins)


def _rms(x):
    r = lax.rsqrt(jnp.mean(x * x, axis=-1, keepdims=True) + NORM_EPS)
    return x * r, r


def _rms_bwd(dxhat, xhat, r):
    return r * (dxhat - xhat * jnp.mean(dxhat * xhat, axis=-1, keepdims=True))


def _normmod_fwd(name, x, g, sh, sc):
    def body(i, x_ref, g_ref, sh_ref, sc_ref, o_ref):
        xhat, _ = _rms(x_ref[...])
        o_ref[...] = ((xhat * g_ref[...]) * (1.0 + sc_ref[...]) + sh_ref[...]).astype(BF16)

    return _rowcall(name, body, x.shape[0], ROW_TILE, [x], [g, sh, sc], [(x.shape[1], BF16)], [])[0]


def _normmod_bwd(name, x, g, sh, sc, dh, dx_pass):
    d = x.shape[1]

    def body(i, x_ref, dh_ref, dxp_ref, g_ref, sh_ref, sc_ref, dx_ref, dxb_ref, dg_ref, dsh_ref, dsc_ref):
        xhat, r = _rms(x_ref[...])
        dh = dh_ref[...].astype(F32)
        gv = g_ref[...]
        dn = dh * (1.0 + sc_ref[...])
        dsc_ref[...] += _colsum(dh * (xhat * gv))
        dsh_ref[...] += _colsum(dh)
        dg_ref[...] += _colsum(dn * xhat)
        dx = dxp_ref[...] + _rms_bwd(dn * gv, xhat, r)
        dx_ref[...] = dx
        dxb_ref[...] = dx.astype(BF16)

    return _rowcall(name, body, x.shape[0], ROW_TILE, [x, dh, dx_pass], [g, sh, sc], [(d, F32), (d, BF16)],
                    [(1, d), (1, d), (1, d)])


def _scale_cols(name, w, g):
    def body(i, w_ref, g_ref, o_ref):
        o_ref[...] = (w_ref[...].astype(F32) * g_ref[...]).astype(BF16)

    return _rowcall(name, body, w.shape[0], ROW_TILE, [w], [g], [(w.shape[1], BF16)], [])[0]


GELU_K = math.sqrt(2.0 / math.pi)
GELU_C = 0.044715


def _gelu_fwd(name, y, u, skip):
    def body(i, y_ref, u_ref, s_ref, o_ref):
        v = y_ref[...].astype(F32) + s_ref[...] * u_ref[...]
        t = jnp.tanh(GELU_K * (v + GELU_C * (v * v * v)))
        o_ref[...] = (0.5 * v * (1.0 + t)).astype(BF16)

    return _rowcall(name, body, y.shape[0], ROW_TILE, [y, u], [skip], [(y.shape[1], BF16)], [])[0]


def _gelu_bwd(name, y, u, skip, dgl):
    d = y.shape[1]

    def body(i, y_ref, u_ref, d_ref, s_ref, o_ref, ds_ref):
        uv = u_ref[...]
        v = y_ref[...].astype(F32) + s_ref[...] * uv
        t = jnp.tanh(GELU_K * (v + GELU_C * (v * v * v)))
        dv = d_ref[...] * (0.5 * (1.0 + t) + 0.5 * v * (1.0 - t * t) * (GELU_K * (1.0 + 3.0 * GELU_C * v * v)))
        o_ref[...] = dv
        ds_ref[...] += _colsum(dv * uv)

    return _rowcall(name, body, y.shape[0], ROW_TILE, [y, u, dgl], [skip], [(d, F32)], [(1, d)])


def _axpy(name, a, b, scale):
    def body(i, a_ref, b_ref, s_ref, o_ref):
        o_ref[...] = (a_ref[...].astype(F32) + s_ref[...] * b_ref[...]).astype(BF16)

    return _rowcall(name, body, a.shape[0], ROW_TILE, [a, b], [scale], [(a.shape[1], BF16)], [])[0]


def _glu_fwd(name, ab, bias, x, gate):
    d = ab.shape[1] // 2

    def body(i, ab_ref, x_ref, b_ref, g_ref, o_ref):
        v = ab_ref[:, :d] + b_ref[:, :d]
        gt = ab_ref[:, d:] + b_ref[:, d:]
        o_ref[...] = x_ref[...] + g_ref[...] * (v * _sigmoid(gt))

    return _rowcall(name, body, ab.shape[0], ROW_TILE, [ab, x], [bias, gate], [(d, F32)], [])[0]


def _glu_bwd(name, ab, bias, dxo, gate):
    d = ab.shape[1] // 2

    def body(i, ab_ref, dx_ref, b_ref, g_ref, dab_ref, db_ref, dg_ref):
        v = ab_ref[:, :d] + b_ref[:, :d]
        s = _sigmoid(ab_ref[:, d:] + b_ref[:, d:])
        dxo_v = dx_ref[...]
        dg_ref[...] += _colsum(dxo_v * (v * s))
        do = g_ref[...] * dxo_v
        dv = do * s
        dgt = do * v * (s * (1.0 - s))
        dab_ref[:, :d] = dv.astype(BF16)
        dab_ref[:, d:] = dgt.astype(BF16)
        db_ref[:, :d] += _colsum(dv)
        db_ref[:, d:] += _colsum(dgt)

    return _rowcall(name, body, ab.shape[0], ROW_TILE, [ab, dxo], [bias, gate], [(2 * d, BF16)],
                    [(1, 2 * d), (1, d)])


def _gatenorm_fwd(name, y, z, ng):
    di = y.shape[1]
    gw = di // M2_G

    def body(i, y_ref, z_ref, g_ref, o_ref):
        for gi in range(M2_G):
            sl = slice(gi * gw, (gi + 1) * gw)
            zz = z_ref[:, sl]
            y2 = y_ref[:, sl] * (zz * _sigmoid(zz))
            yh, _ = _rms(y2)
            o_ref[:, sl] = (yh * g_ref[:, sl]).astype(BF16)

    return _rowcall(name, body, y.shape[0], ROW_TILE, [y, z], [ng], [(di, BF16)], [])[0]


def _gatenorm_bwd(name, y, z, ng, dyn):
    di = y.shape[1]
    gw = di // M2_G

    def body(i, y_ref, z_ref, d_ref, g_ref, dy_ref, dz_ref, dg_ref):
        for gi in range(M2_G):
            sl = slice(gi * gw, (gi + 1) * gw)
            zz = z_ref[:, sl]
            yy = y_ref[:, sl]
            s = _sigmoid(zz)
            sz = zz * s
            yh, r = _rms(yy * sz)
            dn = d_ref[:, sl]
            dg_ref[:, sl] += _colsum(dn * yh)
            dy2 = _rms_bwd(dn * g_ref[:, sl], yh, r)
            dy_ref[:, sl] = dy2 * sz
            dz_ref[:, sl] = (dy2 * yy * (s * (1.0 + zz * (1.0 - s)))).astype(BF16)

    return _rowcall(name, body, y.shape[0], ROW_TILE, [y, z, dyn], [ng], [(di, F32), (di, BF16)], [(1, di)])


def _loss_head(name, x, target, g):
    d = x.shape[1]

    def body(i, x_ref, t_ref, g_ref, dx_ref, dxb_ref, dg_ref, loss_ref):
        xhat, r = _rms(x_ref[...])
        gv = g_ref[...]
        err = xhat * gv - t_ref[...]
        per_row = jnp.sum(err * err, axis=-1, keepdims=True) * (0.5 / d)
        loss_ref[...] += jnp.broadcast_to(_colsum(per_row), loss_ref.shape)
        dy = err * (1.0 / d)
        dg_ref[...] += _colsum(dy * xhat)
        dx = _rms_bwd(dy * gv, xhat, r)
        dx_ref[...] = dx
        dxb_ref[...] = dx.astype(BF16)

    return _rowcall(name, body, x.shape[0], ROW_TILE, [x, target], [g], [(d, F32), (d, BF16)],
                    [(1, d), (1, LANES)])


HALO = 8


def _halo_call(name, body, rows, tile, width, mains, halo_of, halo_next, smalls, row_outs, acc_outs, scratch):
    tile = min(tile, rows)
    nb = tile // HALO
    last = rows // HALO - 1
    n_in = len(mains) + 1 + len(smalls)

    def kern(*refs):
        i = pl.program_id(0)
        accs = refs[n_in + len(row_outs):n_in + len(row_outs) + len(acc_outs)]

        @pl.when(i == 0)
        def _():
            for acc in accs:
                acc[...] = jnp.zeros_like(acc)

        body(i, *refs)

    def whole(shape):
        return pl.BlockSpec(shape, lambda i, nd=len(shape): (0,) * nd)

    if halo_next:
        halo_spec = pl.BlockSpec((HALO, width), lambda i: (jnp.minimum((i + 1) * nb, last), 0))
    else:
        halo_spec = pl.BlockSpec((HALO, width), lambda i: (jnp.maximum(i * nb - 1, 0), 0))
    in_specs = [pl.BlockSpec((tile, a.shape[1]), lambda i: (i, 0)) for a in mains] + [halo_spec]
    in_specs += [whole(a.shape) for a in smalls]
    out_specs = [pl.BlockSpec((tile, w), lambda i: (i, 0)) for (w, _) in row_outs] + [whole(s) for s in acc_outs]
    out_shape = [jax.ShapeDtypeStruct((rows, w), dt) for (w, dt) in row_outs]
    out_shape += [jax.ShapeDtypeStruct(s, F32) for s in acc_outs]
    return pl.pallas_call(
        kern, name=name, grid=(rows // tile,), in_specs=in_specs, out_specs=out_specs, out_shape=out_shape,
        scratch_shapes=scratch, compiler_params=_params("arbitrary"),
    )(*mains, mains[halo_of], *smalls)


CONV_TILE = 128
CONV_ROWS = 16
CONV_STRIP = 512


def _conv_blocks(tile, c, strip=CONV_STRIP):
    strip = strip if c % strip == 0 else LANES
    rb = min(CONV_ROWS, tile)
    return [(r0, rb, slice(c0, c0 + strip)) for c0 in range(0, c, strip) for r0 in range(0, tile, rb)]


def _shifted_windows(base, rb, offsets):
    n = base.shape[0]
    out = []
    for o in offsets:
        if o % HALO == 0:
            out.append(base[o:o + rb, :])
        else:
            out.append(pltpu.roll(base, n - o, 0)[0:rb, :])
    return out


def _conv_windows(x_ref, ext, r0, rb, sl):
    base = ext[:, sl] if r0 == 0 else x_ref[r0 - HALO:r0 + rb, sl]
    return _shifted_windows(base, rb, [HALO - 3 + k for k in range(M2_K)])


def _conv_fwd(name, xin, w, b):
    rows, c = xin.shape
    tile = min(CONV_TILE, rows)
    rb0 = min(CONV_ROWS, tile)

    def body(i, x_ref, h_ref, w_ref, b_ref, o_ref, ext):
        ext[0:HALO, :] = jnp.where(i == 0, 0.0, h_ref[...])
        ext[HALO:, :] = x_ref[0:rb0, :]
        for r0, rb, sl in _conv_blocks(tile, c):
            taps = _conv_windows(x_ref, ext, r0, rb, sl)
            pre = b_ref[:, sl] + w_ref[0:1, sl] * taps[0]
            for k in range(1, M2_K):
                pre = pre + w_ref[k:k + 1, sl] * taps[k]
            o_ref[r0:r0 + rb, sl] = pre * _sigmoid(pre)

    return _halo_call(name, body, rows, tile, c, [xin], 0, False, [w, b], [(c, F32)], [],
                      [pltpu.VMEM((rb0 + HALO, c), F32)])[0]


def _conv_bwd_pre(name, xin, dout, w, b):
    rows, c = xin.shape
    tile = min(CONV_TILE, rows)
    rb0 = min(CONV_ROWS, tile)

    def body(i, x_ref, d_ref, h_ref, w_ref, b_ref, dp_ref, dw_ref, db_ref, ext):
        ext[0:HALO, :] = jnp.where(i == 0, 0.0, h_ref[...])
        ext[HALO:, :] = x_ref[0:rb0, :]
        sums = {}
        for r0, rb, sl in _conv_blocks(tile, c, CONV_STRIP // 2):
            taps = _conv_windows(x_ref, ext, r0, rb, sl)
            pre = b_ref[:, sl] + w_ref[0:1, sl] * taps[0]
            for k in range(1, M2_K):
                pre = pre + w_ref[k:k + 1, sl] * taps[k]
            s = _sigmoid(pre)
            dp = d_ref[r0:r0 + rb, sl] * (s * (1.0 + pre * (1.0 - s)))
            dp_ref[r0:r0 + rb, sl] = dp
            part = [dp] + [dp * taps[k] for k in range(M2_K)]
            key = sl.start
            sums[key] = part if key not in sums else [p + q for p, q in zip(sums[key], part)]
            if r0 + rb == tile:
                db_ref[:, sl] += _colsum(sums[key][0])
                for k in range(M2_K):
                    dw_ref[k:k + 1, sl] += _colsum(sums[key][1 + k])

    return _halo_call(name, body, rows, tile, c, [xin, dout], 0, False, [w, b], [(c, F32)], [(M2_K, c), (1, c)],
                      [pltpu.VMEM((rb0 + HALO, c), F32)])


def _conv_bwd_in(name, dpre, w):
    rows, c = dpre.shape
    tile = min(CONV_TILE, rows)
    n_tiles = rows // tile
    rb0 = min(CONV_ROWS, tile)

    def body(i, d_ref, h_ref, w_ref, o_ref, ext):
        ext[0:rb0, :] = d_ref[tile - rb0:tile, :]
        ext[rb0:, :] = jnp.where(i == n_tiles - 1, 0.0, h_ref[...])
        for r0, rb, sl in _conv_blocks(tile, c):
            base = ext[:, sl] if r0 + rb == tile else d_ref[r0:r0 + rb + HALO, sl]
            wins = _shifted_windows(base, rb, [3 - k for k in range(M2_K)])
            acc = w_ref[0:1, sl] * wins[0]
            for k in range(1, M2_K):
                acc = acc + w_ref[k:k + 1, sl] * wins[k]
            o_ref[r0:r0 + rb, sl] = acc.astype(BF16)

    return _halo_call(name, body, rows, tile, c, [dpre], 0, True, [w], [(c, BF16)], [],
                      [pltpu.VMEM((rb0 + HALO, c), F32)])[0]


def _s5_build(lam_re, lam_im, log_dt, b_re, b_im, c_re, c_im):
    g, p = lam_re.shape
    h = b_re.shape[-1]
    t = S5_T
    dt = jnp.exp(log_dt)[:, None]
    ld_re, ld_im = lam_re * dt, lam_im * dt
    tau = jnp.arange(t + 1, dtype=F32)
    mag = jnp.exp(ld_re[:, :, None] * tau)
    ang = ld_im[:, :, None] * tau
    pw_re, pw_im = mag * jnp.cos(ang), mag * jnp.sin(ang)
    num_re, num_im = pw_re[:, :, 1] - 1.0, pw_im[:, :, 1]
    den = lam_re * lam_re + lam_im * lam_im
    q_re = (num_re * lam_re + num_im * lam_im) / den
    q_im = (num_im * lam_re - num_re * lam_im) / den
    bb_re = q_re[:, :, None] * b_re - q_im[:, :, None] * b_im
    bb_im = q_re[:, :, None] * b_im + q_im[:, :, None] * b_re
    bbt_re, bbt_im = jnp.transpose(bb_re, (0, 2, 1)), jnp.transpose(bb_im, (0, 2, 1))
    ct_re, ct_im = jnp.transpose(c_re, (0, 2, 1)), jnp.transpose(c_im, (0, 2, 1))
    lane = jnp.arange(t * h)
    rep_tau = (lane[None, :] // h == jnp.arange(t)[:, None]).astype(F32)
    tile_h = (lane[None, :] % h == jnp.arange(h)[:, None]).astype(F32)

    def spread(x, m):
        return jnp.einsum("gpk,kn->gpn", x, m, precision=lax.Precision.HIGH)

    c_re_n, c_im_n = spread(ct_re, tile_h), spread(ct_im, tile_h)

    def c_times_pw(first):
        pr, pi = spread(pw_re[:, :, first:first + t], rep_tau), spread(pw_im[:, :, first:first + t], rep_tau)
        return pr * c_re_n - pi * c_im_n, pr * c_im_n + pi * c_re_n

    cp0_re, cp0_im = c_times_pw(0)
    cp1_re, cp1_im = c_times_pw(1)
    kc = (jnp.einsum("ghp,gpn->ghn", bbt_re, cp0_re, precision=lax.Precision.HIGH)
          - jnp.einsum("ghp,gpn->ghn", bbt_im, cp0_im, precision=lax.Precision.HIGH))
    bp_re = jnp.transpose(pw_re[:, :, t - 1::-1][:, :, :t], (0, 2, 1))
    bp_im = jnp.transpose(pw_im[:, :, t - 1::-1][:, :, :t], (0, 2, 1))
    be_re = bbt_re[:, :, None, :] * bp_re[:, None, :, :] - bbt_im[:, :, None, :] * bp_im[:, None, :, :]
    be_im = bbt_re[:, :, None, :] * bp_im[:, None, :, :] + bbt_im[:, :, None, :] * bp_re[:, None, :, :]
    bend = jnp.concatenate([be_re, be_im], axis=-1).reshape(g, h * t, 2 * p)
    cpow = jnp.concatenate([cp1_re, -cp1_im], axis=1)
    at_re, at_im = pw_re[:, :, t], pw_im[:, :, t]
    a1 = jnp.concatenate([at_re, at_re], axis=-1)[:, None, :]
    a2 = jnp.concatenate([-at_im, at_im], axis=-1)[:, None, :]
    return kc, bend, cpow, a1, a2


def _swap_halves(x, axis):
    n = x.shape[axis] // 2
    lo = lax.slice_in_dim(x, 0, n, axis=axis)
    hi = lax.slice_in_dim(x, n, 2 * n, axis=axis)
    return jnp.concatenate([hi, lo], axis=axis)


def _group_spec(shape):
    return pl.BlockSpec((1,) + tuple(shape[1:]), lambda g: (g, 0, 0))


S5_ROWS = 8


def _s5_expand_toeplitz(kc_ref, ext, toep):
    t, th = S5_T, S5_T * S5_H
    ext[:, th:] = jnp.zeros((S5_ROWS, LANES), F32)
    for hin in range(S5_H):
        ext[:, :th] = jnp.broadcast_to(kc_ref[0, hin:hin + 1, :], (S5_ROWS, th))
        rolled = pltpu.roll(ext[...], 0, 1, stride=S5_H, stride_axis=0)
        tiles = []
        for q in range(t // S5_ROWS):
            if q == 0:
                tiles.append(rolled[:, :th])
            else:
                tiles.append(jnp.concatenate([jnp.zeros((S5_ROWS, q * LANES), F32), rolled[:, :th - q * LANES]],
                                             axis=1))
        toep[hin * t:(hin + 1) * t, :] = jnp.concatenate(tiles, axis=0).astype(BF16)


def _s5_core_fwd(name, u, ops):
    kc, bend, cpow, a1, a2 = ops
    g, nc, th = u.shape
    p2 = bend.shape[-1]
    bend_b, cpow_b = bend.astype(BF16), cpow.astype(BF16)
    a2s = _swap_halves(a2, 2)

    def kern(u_ref, k_ref, b_ref, c_ref, a1_ref, a2_ref, a2s_ref, y_ref, sp_ref, x_scr, xs_scr, ext, toep):
        _s5_expand_toeplitz(k_ref, ext, toep)
        ub = u_ref[0].astype(BF16)
        xv = _dot(ub, b_ref[0])
        x_scr[...] = xv
        xs_scr[...] = pltpu.roll(xv, p2 // 2, 1)
        a1v, a2v, a2sv = a1_ref[0], a2_ref[0], a2s_ref[0]

        def step(c, carry):
            s, ss = carry
            sp_ref[0, pl.ds(c, 1), :] = s
            s_new = a1v * s + a2v * ss + x_scr[pl.ds(c, 1), :]
            ss_new = a1v * ss + a2sv * s + xs_scr[pl.ds(c, 1), :]
            return s_new, ss_new

        zero = jnp.zeros((1, p2), F32)
        lax.fori_loop(0, nc, step, (zero, zero))
        y_ref[0] = (_dot(ub, toep[...]) + _dot(sp_ref[0].astype(BF16), c_ref[0])).astype(BF16)

    ins = [u, kc, bend_b, cpow_b, a1, a2, a2s]
    return pl.pallas_call(
        kern, name=name, grid=(g,), in_specs=[_group_spec(a.shape) for a in ins],
        out_specs=[_group_spec((g, nc, th)), _group_spec((g, nc, p2))],
        out_shape=[jax.ShapeDtypeStruct((g, nc, th), BF16), jax.ShapeDtypeStruct((g, nc, p2), F32)],
        scratch_shapes=[pltpu.VMEM((nc, p2), F32), pltpu.VMEM((nc, p2), F32),
                        pltpu.VMEM((S5_ROWS, th + LANES), F32), pltpu.VMEM((th, th), BF16)],
        compiler_params=_params("arbitrary"),
    )(*ins)


def _s5_core_bwd(name, u, dy, sprev, ops):
    kc, bend, cpow, a1, a2 = ops
    g, nc, th = u.shape
    t = S5_T
    p2 = bend.shape[-1]
    bend_b, cpow_b = bend.astype(BF16), cpow.astype(BF16)
    a2s = _swap_halves(a2, 2)
    idx = jnp.arange(th)
    flip = (idx[:, None] // t == idx[None, :] // t) & (idx[:, None] % t == t - 1 - idx[None, :] % t)
    flip = flip.astype(BF16)

    def kern(u_ref, dy_ref, sp_ref, k_ref, b_ref, c_ref, a1_ref, a2_ref, a2s_ref, f_ref,
             du_ref, dk_ref, db_ref, dc_ref, da1_ref, da2_ref, g_scr, gs_scr, dx_scr, ext, toep, dtoep):
        _s5_expand_toeplitz(k_ref, ext, toep)
        ub, dyb = u_ref[0].astype(BF16), dy_ref[0].astype(BF16)
        dtoep[...] = _dot(_dot(ub, f_ref[...]).astype(BF16), dyb, "tn")
        n_q = t // S5_ROWS
        width = th + LANES
        for hin in range(S5_H):
            folded = dtoep[hin * t + (n_q - 1) * S5_ROWS:(hin + 1) * t, :]
            for qp in range(n_q - 1):
                q = n_q - 1 - qp
                tile = dtoep[hin * t + qp * S5_ROWS:hin * t + (qp + 1) * S5_ROWS, :]
                folded = folded + jnp.concatenate([tile[:, q * LANES:], jnp.zeros((S5_ROWS, q * LANES), F32)],
                                                  axis=1)
            ext[:, :th] = folded
            rolled = pltpu.roll(ext[...], 0, 1, stride=S5_H, stride_axis=0)
            rolled = pltpu.roll(rolled, width - S5_H * (S5_ROWS - 1), 1)
            dk_ref[0, hin:hin + 1, :] = _colsum(rolled)[:, :th]
        spv = sp_ref[0]
        dc_ref[0] = _dot(spv.astype(BF16), dyb, "tn")
        gv = _dot(dyb, c_ref[0], "nt")
        g_scr[...] = gv
        gs_scr[...] = pltpu.roll(gv, p2 // 2, 1)
        a1v, a2v, a2sv = a1_ref[0], a2_ref[0], a2s_ref[0]

        def step(k, carry):
            gr, grs, da1, da2 = carry
            c = nc - 1 - k
            dx_scr[pl.ds(c, 1), :] = gr
            s_in = sp_ref[0, pl.ds(c, 1), :]
            da1 = da1 + gr * s_in
            da2 = da2 + grs * s_in
            gr_new = g_scr[pl.ds(c, 1), :] + a1v * gr + a2sv * grs
            grs_new = gs_scr[pl.ds(c, 1), :] + a1v * grs + a2v * gr
            return gr_new, grs_new, da1, da2

        zero = jnp.zeros((1, p2), F32)
        _, _, da1, da2 = lax.fori_loop(0, nc, step, (zero, zero, zero, zero))
        da1_ref[0] = da1
        da2_ref[0] = da2
        dxb = dx_scr[...].astype(BF16)
        db_ref[0] = _dot(ub, dxb, "tn")
        du_ref[0] = (_dot(dyb, toep[...], "nt") + _dot(dxb, b_ref[0], "nt")).astype(BF16)

    ins = [u, dy, sprev, kc, bend_b, cpow_b, a1, a2, a2s]
    outs = [(g, nc, th), (g, S5_H, th), (g, th, p2), (g, p2, th), (g, 1, p2), (g, 1, p2)]
    out_types = [BF16] + [F32] * (len(outs) - 1)
    return pl.pallas_call(
        kern, name=name, grid=(g,),
        in_specs=[_group_spec(a.shape) for a in ins] + [pl.BlockSpec((th, th), lambda gi: (0, 0))],
        out_specs=[_group_spec(s) for s in outs],
        out_shape=[jax.ShapeDtypeStruct(s, dt) for s, dt in zip(outs, out_types)],
        scratch_shapes=[pltpu.VMEM((nc, p2), F32), pltpu.VMEM((nc, p2), F32), pltpu.VMEM((nc, p2), F32),
                        pltpu.VMEM((S5_ROWS, th + LANES), F32), pltpu.VMEM((th, th), BF16),
                        pltpu.VMEM((th, th), F32)],
        compiler_params=_params("arbitrary"),
    )(*ins, flip)


def _s5_to_groups(u, channel_major):
    rows, w = u.shape
    g = w // S5_H
    nc = rows // S5_T
    perm = (2, 0, 3, 1) if channel_major else (2, 0, 1, 3)
    return u.reshape(nc, S5_T, g, S5_H).transpose(perm).reshape(g, nc, S5_T * S5_H)


def _s5_from_groups(y, channel_major):
    g, nc, _ = y.shape
    if channel_major:
        return y.reshape(g, nc, S5_H, S5_T).transpose(1, 3, 0, 2).reshape(nc * S5_T, g * S5_H)
    return y.reshape(g, nc, S5_T, S5_H).transpose(1, 2, 0, 3).reshape(nc * S5_T, g * S5_H)


def _softplus(x):
    return jnp.maximum(x, 0.0) + jnp.log(1.0 + jnp.exp(-jnp.abs(x)))


def _ssd_chunk_prep(dtraw_ref, dtb_ref, a_ref, expand_ref, cst, dtx, lastt, n_heads):
    q = M2_Q
    lane = lax.broadcasted_iota(jnp.int32, (q, LANES), 1)
    dt = jnp.where(lane < n_heads, _softplus(dtraw_ref[...] + dtb_ref[...]), 0.0)
    adt = dt * a_ref[...]
    row = lax.broadcasted_iota(jnp.int32, (q, q), 0)
    col = lax.broadcasted_iota(jnp.int32, (q, q), 1)
    cs = _dot(jnp.where(row >= col, 1.0, 0.0), adt, precision=HIGHEST)
    cst[...] = cs.T
    dtx[...] = _bdot(dt, expand_ref[...])
    lastt[...] = jnp.broadcast_to(_colsum(adt), (q, LANES)).T
    return dt


def _pair_tables(cst, lastt, p):
    q = M2_Q
    out = []
    for hh in (2 * p, 2 * p + 1):
        rc = jnp.broadcast_to(cst[hh:hh + 1, :], (q, q))
        cc = rc.T
        lb = jnp.broadcast_to(lastt[hh:hh + 1, :], (q, q))
        out.append((rc, cc, lb))
    return out


def _ssd_pair_fwd(x, bm, cm, cb, hs, tabs, dtp):
    q = M2_Q
    row = lax.broadcasted_iota(jnp.int32, (q, q), 0)
    col = lax.broadcasted_iota(jnp.int32, (q, q), 1)
    causal = row >= col
    lo = col < M2_P
    slo = row < M2_P
    (rc0, cc0, lb0), (rc1, cc1, lb1) = tabs
    l0 = jnp.where(causal, jnp.exp(cc0 - rc0), 0.0)
    l1 = jnp.where(causal, jnp.exp(cc1 - rc1), 0.0)
    m0, m1 = cb * l0, cb * l1
    xdt = x * dtp
    xdt0 = jnp.where(lo, xdt, 0.0)
    xdt1 = jnp.where(lo, 0.0, xdt)
    e = jnp.where(lo, jnp.exp(cc0), jnp.exp(cc1))
    z = _bdot(cm, hs, "nt")
    yoff = z * e
    dec = jnp.where(lo, jnp.exp(lb0 - cc0), jnp.exp(lb1 - cc1))
    xdd = xdt * dec
    cd = jnp.where(slo, jnp.exp(lb0), jnp.exp(lb1))
    return dict(l0=l0, l1=l1, m0=m0, m1=m1, dtp=dtp, xdt=xdt, xdt0=xdt0, xdt1=xdt1, e=e, yoff=yoff,
                dec=dec, xdd=xdd, cd=cd, lo=lo, slo=slo)


def _ssd_fwd(name, xbc, dtraw, dtb, arow, dvec, expand, n_heads):
    rows, c = xbc.shape
    q, n = M2_Q, M2_N
    di = n_heads * M2_P
    n_pairs = n_heads // 2
    ppg = n_pairs // M2_G
    nc = rows // q

    def kern(xbc_ref, dtraw_ref, dtb_ref, a_ref, d_ref, e_ref, y_ref, prev_ref, state, cst, dtx, lastt):
        @pl.when(pl.program_id(0) == 0)
        def _():
            state[...] = jnp.zeros_like(state)

        _ssd_chunk_prep(dtraw_ref, dtb_ref, a_ref, e_ref, cst, dtx, lastt, n_heads)
        for p in range(n_pairs):
            gi = p // ppg
            sl = slice(p * LANES, (p + 1) * LANES)
            x = xbc_ref[:, sl]
            bm = xbc_ref[:, di + gi * n:di + (gi + 1) * n]
            cm = xbc_ref[:, di + (M2_G + gi) * n:di + (M2_G + gi + 1) * n]
            if p % ppg == 0:
                cb = _bdot(cm, bm, "nt")
            hs = state[p]
            f = _ssd_pair_fwd(x, bm, cm, cb, hs, _pair_tables(cst, lastt, p), dtx[:, sl])
            ydiag = _bdot(f["m0"], f["xdt0"]) + _bdot(f["m1"], f["xdt1"])
            y_ref[:, sl] = ydiag + f["yoff"] + d_ref[:, sl] * x
            prev_ref[0, p] = hs
            state[p] = f["cd"] * hs + _bdot(f["xdd"], bm, "tn")

    def whole(a):
        return pl.BlockSpec(a.shape, lambda i: (0, 0))

    return pl.pallas_call(
        kern, name=name, grid=(nc,),
        in_specs=[pl.BlockSpec((q, c), lambda i: (i, 0)), pl.BlockSpec((q, LANES), lambda i: (i, 0)),
                  whole(dtb), whole(arow), whole(dvec), whole(expand)],
        out_specs=[pl.BlockSpec((q, di), lambda i: (i, 0)),
                   pl.BlockSpec((1, n_pairs, 2 * M2_P, n), lambda i: (i, 0, 0, 0))],
        out_shape=[jax.ShapeDtypeStruct((rows, di), F32),
                   jax.ShapeDtypeStruct((nc, n_pairs, 2 * M2_P, n), F32)],
        scratch_shapes=[pltpu.VMEM((n_pairs, 2 * M2_P, n), F32), pltpu.VMEM((LANES, q), F32),
                        pltpu.VMEM((q, di), F32), pltpu.VMEM((LANES, q), F32)],
        compiler_params=_params("arbitrary"),
    )(xbc, dtraw, dtb, arow, dvec, expand)


def _ssd_bwd(name, xbc, dtraw, dy, prev, dtb, arow, dvec, seg, expand, n_heads):
    rows, c = xbc.shape
    q, n = M2_Q, M2_N
    di = n_heads * M2_P
    n_pairs = n_heads // 2
    ppg = n_pairs // M2_G
    nc = rows // q

    def kern(xbc_ref, dtraw_ref, dy_ref, prev_ref, dtb_ref, a_ref, d_ref, seg_ref, e_ref,
             dxbc_ref, ddt_ref, da_ref, ddtb_ref, dd_ref,
             dstate, cst, dtx, lastt, dcst, wx, colterm, ddfull):
        step = pl.program_id(0)

        @pl.when(step == 0)
        def _():
            dstate[...] = jnp.zeros_like(dstate)
            ddfull[...] = jnp.zeros_like(ddfull)
            da_ref[...] = jnp.zeros_like(da_ref)
            ddtb_ref[...] = jnp.zeros_like(ddtb_ref)
            dd_ref[...] = jnp.zeros_like(dd_ref)

        dt = _ssd_chunk_prep(dtraw_ref, dtb_ref, a_ref, e_ref, cst, dtx, lastt, n_heads)
        dcst[...] = jnp.zeros_like(dcst)
        lane_q = lax.broadcasted_iota(jnp.int32, (1, q), 1)
        last_hot = jnp.where(lane_q == q - 1, 1.0, 0.0)

        def total(v):
            return jnp.sum(jnp.sum(v, axis=1, keepdims=True), axis=0, keepdims=True)

        for gi in range(M2_G):
            bm = xbc_ref[:, di + gi * n:di + (gi + 1) * n]
            cm = xbc_ref[:, di + (M2_G + gi) * n:di + (M2_G + gi + 1) * n]
            cb = _bdot(cm, bm, "nt")
            dcb = jnp.zeros((q, q), F32)
            dbm = jnp.zeros((q, n), F32)
            dcm = jnp.zeros((q, n), F32)
            for p in range(gi * ppg, (gi + 1) * ppg):
                sl = slice(p * LANES, (p + 1) * LANES)
                x = xbc_ref[:, sl]
                dyp = dy_ref[:, sl]
                hs = prev_ref[0, p]
                ds = dstate[p]
                f = _ssd_pair_fwd(x, bm, cm, cb, hs, _pair_tables(cst, lastt, p), dtx[:, sl])
                lo, slo = f["lo"], f["slo"]
                ddfull[:, sl] += _colsum(dyp * x)
                dy0 = jnp.where(lo, dyp, 0.0)
                dy1 = jnp.where(lo, 0.0, dyp)
                dm0 = _bdot(dyp, f["xdt0"], "nt")
                dm1 = _bdot(dyp, f["xdt1"], "nt")
                dxdt = _bdot(f["m0"], dy0, "tn") + _bdot(f["m1"], dy1, "tn")
                dcb = dcb + dm0 * f["l0"] + dm1 * f["l1"]
                w0, w1 = dm0 * f["m0"], dm1 * f["m1"]
                dz = dyp * f["e"]
                dcm = dcm + _bdot(dz, hs)
                dhs = _bdot(dz, cm, "tn") + f["cd"] * ds
                tot = ds * hs * f["cd"]
                dxdd = _bdot(bm, ds, "nt")
                dbm = dbm + _bdot(f["xdd"], ds)
                ee = dxdd * f["xdd"]
                colterm[:, sl] = dyp * f["yoff"] - ee
                dxdt = dxdt + dxdd * f["dec"]
                t_all = total(tot)
                t_lo = total(jnp.where(slo, tot, 0.0))
                e_all = total(ee)
                e_lo = total(jnp.where(lo, ee, 0.0))
                dlast0 = t_lo + e_lo
                dlast1 = (t_all - t_lo) + (e_all - e_lo)
                dcst[2 * p:2 * p + 1, :] = _colsum(w0.T - w0) + dlast0 * last_hot
                dcst[2 * p + 1:2 * p + 2, :] = _colsum(w1.T - w1) + dlast1 * last_hot
                dxbc_ref[:, sl] = d_ref[:, sl] * dyp + dxdt * f["dtp"]
                wx[:, sl] = dxdt * x
                dstate[p] = dhs
            dcm = dcm + _bdot(dcb, bm)
            dbm = dbm + _bdot(dcb, cm, "tn")
            dxbc_ref[:, di + gi * n:di + (gi + 1) * n] = dbm
            dxbc_ref[:, di + (M2_G + gi) * n:di + (M2_G + gi + 1) * n] = dcm

        segv = seg_ref[...]
        dcs = _dot(colterm[...], segv, precision=HIGHEST) + dcst[...].T
        row = lax.broadcasted_iota(jnp.int32, (q, q), 0)
        col = lax.broadcasted_iota(jnp.int32, (q, q), 1)
        ddelta = _dot(jnp.where(col >= row, 1.0, 0.0), dcs, precision=HIGHEST)
        ddt = _dot(wx[...], segv, precision=HIGHEST) + ddelta * a_ref[...]
        da_ref[...] += _colsum(ddelta * dt)
        lane = lax.broadcasted_iota(jnp.int32, (q, LANES), 1)
        ddtraw = jnp.where(lane < n_heads, ddt * _sigmoid(dtraw_ref[...] + dtb_ref[...]), 0.0)
        ddt_ref[...] = ddtraw
        ddtb_ref[...] += _colsum(ddtraw)

        @pl.when(step == nc - 1)
        def _():
            dd_ref[...] = _dot(jnp.broadcast_to(ddfull[...], (8, di)), segv, precision=HIGHEST)

    def whole(a):
        return pl.BlockSpec(a.shape, lambda i: (0, 0))

    def rev(i):
        return nc - 1 - i

    acc = jax.ShapeDtypeStruct((1, LANES), F32)
    acc_spec = pl.BlockSpec((1, LANES), lambda i: (0, 0))
    acc8 = jax.ShapeDtypeStruct((8, LANES), F32)
    acc8_spec = pl.BlockSpec((8, LANES), lambda i: (0, 0))
    return pl.pallas_call(
        kern, name=name, grid=(nc,),
        in_specs=[pl.BlockSpec((q, c), lambda i: (rev(i), 0)), pl.BlockSpec((q, LANES), lambda i: (rev(i), 0)),
                  pl.BlockSpec((q, di), lambda i: (rev(i), 0)),
                  pl.BlockSpec((1, n_pairs, 2 * M2_P, n), lambda i: (rev(i), 0, 0, 0)),
                  whole(dtb), whole(arow), whole(dvec), whole(seg), whole(expand)],
        out_specs=[pl.BlockSpec((q, c), lambda i: (rev(i), 0)), pl.BlockSpec((q, LANES), lambda i: (rev(i), 0)),
                   acc_spec, acc_spec, acc8_spec],
        out_shape=[jax.ShapeDtypeStruct((rows, c), F32), jax.ShapeDtypeStruct((rows, LANES), F32), acc, acc, acc8],
        scratch_shapes=[pltpu.VMEM((n_pairs, 2 * M2_P, n), F32), pltpu.VMEM((LANES, q), F32),
                        pltpu.VMEM((q, di), F32), pltpu.VMEM((LANES, q), F32), pltpu.VMEM((LANES, q), F32),
                        pltpu.VMEM((q, di), F32), pltpu.VMEM((q, di), F32), pltpu.VMEM((1, di), F32)],
        compiler_params=_params("arbitrary"),
    )(xbc, dtraw, dy, prev, dtb, arow, dvec, seg, expand)


S5_PARAM_NAMES = ("s5_lambda_re", "s5_lambda_im", "s5_log_dt", "s5_b_re", "s5_b_im", "s5_c_re", "s5_c_im")


def _row(v):
    return v.reshape(1, -1)


class _Rides:
    def __init__(self):
        self.pending = {}
        self.landed = {}

    def matmul(self, site, name, *args, **kw):
        ride = self.pending.pop(site, None)
        res = _matmul(name, *args, ride=ride, **kw)
        if ride is None:
            return res
        self.landed[site] = list(res[-len(ride):])
        res = list(res[:-len(ride)])
        return res[0] if len(res) == 1 else res


def _s5_layer_fwd(tag, x, gate, h, w, j, rides):
    u = rides.matmul("s5_win", tag + "_win", h, w["s5_w_in"][j])
    params = [w[k][j] for k in S5_PARAM_NAMES]
    ops, build_vjp = jax.vjp(_s5_build, *params)
    ug = _s5_to_groups(u.astype(BF16), True)
    yg, sprev = _s5_core_fwd(tag + "_core", ug, ops)
    yy = _s5_from_groups(yg, False)
    skip = _row(w["s5_d"][j])
    gl = _gelu_fwd(tag + "_gelu", yy, u, skip)
    ab = rides.matmul("s5_wglu", tag + "_wglu", gl, w["s5_w_glu"][j])
    x1 = _glu_fwd(tag + "_glu", ab, _row(w["s5_b_glu"][j]), x, gate)
    return x1, dict(u=u, ug=ug, ops=ops, build_vjp=build_vjp, sprev=sprev, yy=yy, gl=gl, ab=ab, skip=skip)


def _s5_layer_bwd(tag, dx1, gate, h, sv, w, j, rides):
    dab, db_glu, dgate = _glu_bwd(tag + "_glu_b", sv["ab"], _row(w["s5_b_glu"][j]), dx1, gate)
    dw_glu = rides.matmul("s5_dwglu", tag + "_dwglu", sv["gl"], dab, "tn", out_dtype=BF16)
    dgl = rides.matmul("s5_dgl", tag + "_dgl", dab, w["s5_w_glu"][j], "nt")
    dyy, dskip = _gelu_bwd(tag + "_gelu_b", sv["yy"], sv["u"], sv["skip"], dgl)
    dug, dkc, dbend, dcpow, da1, da2s = _s5_core_bwd(
        tag + "_core_b", sv["ug"], _s5_to_groups(dyy.astype(BF16), False), sv["sprev"], sv["ops"])
    dparams = sv["build_vjp"]((dkc, dbend, dcpow, da1, _swap_halves(da2s, 2)))
    du = _axpy(tag + "_du", _s5_from_groups(dug, True), dyy, sv["skip"])
    grads = dict(zip(S5_PARAM_NAMES, dparams))
    grads["s5_d"] = dskip.reshape(-1)
    grads["s5_w_in"] = _matmul(tag + "_dwin", h, du, "tn", out_dtype=BF16)
    grads["s5_w_glu"] = dw_glu
    grads["s5_b_glu"] = db_glu.reshape(-1)
    dh = _matmul(tag + "_dh", du, w["s5_w_in"][j], "nt")
    return dh, grads, dgate


def _ssd_consts(w, j, d_model):
    di = 2 * d_model
    heads = di // M2_P

    def pad_row(v):
        return jnp.zeros((1, LANES), F32).at[0, :heads].set(v)

    a = -jnp.exp(w["m2_a_log"][j])
    seg = (jnp.arange(di)[:, None] // M2_P == jnp.arange(LANES)[None, :]).astype(F32)
    w_in = w["m2_w_in"][j]
    conv_dim = di + 2 * M2_G * M2_N
    w_dt = jnp.zeros((d_model, LANES), w_in.dtype).at[:, :heads].set(w_in[:, di + conv_dim:])
    return dict(di=di, heads=heads, conv_dim=conv_dim, a=a, arow=pad_row(a), dtb=pad_row(w["m2_dt_bias"][j]),
                dvec=_row(jnp.repeat(w["m2_d"][j], M2_P)), seg=seg, expand=seg.T.astype(BF16),
                w_z=w_in[:, :di], w_xbc=w_in[:, di:di + conv_dim], w_dt=w_dt,
                conv_w=w["m2_conv_w"][j], conv_b=_row(w["m2_conv_b"][j]), norm_g=_row(w["m2_norm_g"][j]))


def _gated_out_bwd(tag, act, dxo, w_out, gate, **kw):
    dw, dgate_parts = _matmul(tag + "_dwo", act, dxo, "tn", out_dtype=BF16, colscale=gate, colsum_with=w_out, **kw)
    dgate = jnp.sum(dgate_parts, axis=0)
    return dw, dgate, _scale_cols(tag + "_wog", w_out, gate)


def _ssd_layer_fwd(tag, x, gate, h, w, j):
    k = _ssd_consts(w, j, h.shape[1])
    z = _matmul(tag + "_wz", h, k["w_z"])
    xbc_pre = _matmul(tag + "_wxbc", h, k["w_xbc"])
    dtraw = _matmul(tag + "_wdt", h, k["w_dt"])
    xbc = _conv_fwd(tag + "_conv", xbc_pre, k["conv_w"], k["conv_b"])
    y, prev = _ssd_fwd(tag + "_core", xbc, dtraw, k["dtb"], k["arow"], k["dvec"], k["expand"], k["heads"])
    yn = _gatenorm_fwd(tag + "_gn", y, z, k["norm_g"])
    x1 = _matmul(tag + "_wout", yn, w["m2_w_out"][j], colscale=gate, addin=x)
    return x1, dict(k=k, z=z, xbc_pre=xbc_pre, dtraw=dtraw, xbc=xbc, y=y, prev=prev, yn=yn)


def _ssd_layer_bwd(tag, dx1_b, gate, h, sv, w, j):
    k = sv["k"]
    heads = k["heads"]
    dw_out, dgate, wog = _gated_out_bwd(tag, sv["yn"], dx1_b, w["m2_w_out"][j], gate)
    grads = {"m2_w_out": dw_out}
    dyn = _matmul(tag + "_dyn", dx1_b, wog, "nt")
    dyssd, dz, dng = _gatenorm_bwd(tag + "_gn_b", sv["y"], sv["z"], k["norm_g"], dyn)
    dxbc, ddtraw, da, ddtb, dd = _ssd_bwd(tag + "_core_b", sv["xbc"], sv["dtraw"], dyssd, sv["prev"],
                                          k["dtb"], k["arow"], k["dvec"], k["seg"], k["expand"], heads)
    dpre, dcw, dcb = _conv_bwd_pre(tag + "_conv_b1", sv["xbc_pre"], dxbc, k["conv_w"], k["conv_b"])
    dxbc_pre = _conv_bwd_in(tag + "_conv_b2", dpre, k["conv_w"])
    dw_z = _matmul(tag + "_dwz", h, dz, "tn", out_dtype=BF16)
    dw_xbc = _matmul(tag + "_dwxbc", h, dxbc_pre, "tn", out_dtype=BF16)
    dw_dt = _matmul(tag + "_dwdt", h, ddtraw, "tn", out_dtype=BF16)
    dh = _matmul(tag + "_dh1", dz, k["w_z"], "nt")
    dh = _matmul(tag + "_dh2", dxbc_pre, k["w_xbc"], "nt", addin=dh)
    dh = _matmul(tag + "_dh3", ddtraw, k["w_dt"], "nt", addin=dh)
    grads["m2_w_in"] = jnp.concatenate([dw_z, dw_xbc, dw_dt[:, :heads]], axis=1)
    grads["m2_conv_w"] = dcw
    grads["m2_conv_b"] = dcb.reshape(-1)
    grads["m2_dt_bias"] = ddtb[0, :heads]
    grads["m2_a_log"] = da[0, :heads] * k["a"]
    grads["m2_d"] = dd[0, :heads]
    grads["m2_norm_g"] = dng.reshape(-1)
    return dh, grads, dgate


def _layer_fwd(li, x, mod, w, rides, late_weights=None):
    tag = "L%d" % li
    sh1, sc1, g1, sh2, sc2, g2 = mod
    j = li // 2
    h = _normmod_fwd(tag + "_nm1", x, _row(w["norm_mix_g"][li]), sh1, sc1)
    if li % 2 == 0:
        x1, mix = _s5_layer_fwd(tag + "_s5", x, g1, h, w, j, rides)
    else:
        x1, mix = _ssd_layer_fwd(tag + "_m2", x, g1, h, w, j)
    if late_weights is not None:
        w = {**w, **late_weights(rides.landed)}
    h2 = _normmod_fwd(tag + "_nm2", x1, _row(w["norm_mlp_g"][li]), sh2, sc2)
    r = rides.matmul("w1", tag + "_w1", h2, w["mlp_w1"][li], relu=True, out_dtype=BF16)
    x2 = rides.matmul("w2", tag + "_w2", r, w["mlp_w2"][li], square_a=True, colscale=g2, addin=x1)
    return x2, dict(x=x, h=h, mix=mix, x1=x1, h2=h2, r=r), w


def _layer_bwd(li, dx2, dx2_b, sv, mod, w, rides, ride_own_mlp):
    tag = "L%d" % li
    sh1, sc1, g1, sh2, sc2, g2 = mod
    j = li // 2
    dw2, dg2, w2g = _gated_out_bwd(tag + "_mlp", sv["r"], dx2_b, w["mlp_w2"][li], g2, square_a=True)
    dr = rides.matmul("dr", tag + "_dr", dx2_b, w2g, "nt", out_dtype=BF16, mul2=sv["r"])
    dw1 = _matmul(tag + "_dw1", sv["h2"], dr, "tn", out_dtype=BF16)
    grads = {"mlp_w2": dw2, "mlp_w1": dw1}
    dh2 = rides.matmul("dh2", tag + "_dh2", dr, w["mlp_w1"][li], "nt")
    if ride_own_mlp:
        mlp_bufs = _grad_buffers(li, grads, parts=[1])[0]
        rides.pending["s5_dgl"], rides.pending["s5_dwglu"] = mlp_bufs[:1], mlp_bufs[1:]
    dx1, dx1_b, dgm, dsh2, dsc2 = _normmod_bwd(tag + "_nm2_b", sv["x1"], _row(w["norm_mlp_g"][li]), sh2, sc2, dh2,
                                               dx2)
    if li % 2 == 0:
        dh, mix_grads, dg1 = _s5_layer_bwd(tag + "_s5", dx1, g1, sv["h"], sv["mix"], w, j, rides)
    else:
        dh, mix_grads, dg1 = _ssd_layer_bwd(tag + "_m2", dx1_b, g1, sv["h"], sv["mix"], w, j)
    dx, dx_b, dgx, dsh1, dsc1 = _normmod_bwd(tag + "_nm1_b", sv["x"], _row(w["norm_mix_g"][li]), sh1, sc1, dh, dx1)
    grads["norm_mix_g"] = dgx.reshape(-1)
    grads["norm_mlp_g"] = dgm.reshape(-1)
    dmod = jnp.concatenate([dsh1, dsc1, dg1, dsh2, dsc2, dg2], axis=1)
    return dx, dx_b, {**grads, **mix_grads}, dmod


def _layer_parts(li):
    j = li // 2
    mix = [("s5_w_in", j, 0), ("s5_w_glu", j, 1)] if li % 2 == 0 else [("m2_w_in", j, 1), ("m2_w_out", j, 0)]
    return [mix, [("mlp_w1", li, 1), ("mlp_w2", li, 0)]]


def _grad_specs(li):
    j = li // 2
    mlp = [("one", ("mlp_w2", li, 0)), ("one", ("mlp_w1", li, 1))]
    if li % 2 == 0:
        return [[("one", ("s5_w_in", j, 0)), ("one", ("s5_w_glu", j, 1))], mlp]
    packed = [("m2_w_in", j, 1), ("m2_conv_w", j, 1), ("m2_conv_b", j, 0), ("m2_norm_g", j, 0)]
    return [[("one", ("m2_w_out", j, 0)), ("packed", packed)], mlp]


def _grad_buffers(li, grads, parts=(0, 1)):
    out = []
    for k in parts:
        bufs = []
        for kind, entry in _grad_specs(li)[k]:
            if kind == "one":
                bufs.append((grads[entry[0]], "rows" if entry[2] == 0 else "cols"))
            else:
                pieces = [_pack([_chip_slice(grads[n], c, ax) for n, _, ax in entry], BF16) for c in range(N_CHIP)]
                bufs.append((jnp.stack(pieces), "packed"))
        out.append(bufs)
    return out


def _gather_buffers(local, part):
    return [(local[n][i].astype(BF16), "same") for n, i, _ in part]


def _assemble(landed, part):
    out = {}
    for buf, (n, i, ax) in zip(landed, part):
        if ax == 0:
            out[n] = {i: buf.reshape(-1, buf.shape[2])}
        else:
            out[n] = {i: jnp.concatenate([buf[k] for k in range(N_CHIP)], axis=1)}
    return out


def _local_step(x, target, mods, w, local):
    depth = w["norm_mix_g"].shape[0]
    d = x.shape[1]
    rides = _Rides()
    saved, mod_rows, layer_w = [], [], []
    mix0, mlp0 = _layer_parts(0)
    wl = {**w, **_assemble(_exchange4("ag_w_L0", _gather_buffers(local, mix0)), mix0)}
    rides.pending["s5_win"] = _gather_buffers(local, mlp0[:1])
    rides.pending["s5_wglu"] = _gather_buffers(local, mlp0[1:])

    def late_mlp0(landed):
        return _assemble(landed.pop("s5_win") + landed.pop("s5_wglu"), mlp0)

    for li in range(depth):
        if li + 1 < depth:
            nxt = _layer_parts(li + 1)
            rides.pending["w1"] = _gather_buffers(local, nxt[0])
            rides.pending["w2"] = _gather_buffers(local, nxt[1])
        mod = [mods[li:li + 1, i * d:(i + 1) * d] for i in range(N_MOD)]
        mod_rows.append(mod)
        x, sv, wl = _layer_fwd(li, x, mod, wl, rides, late_mlp0 if li == 0 else None)
        saved.append(sv)
        layer_w.append(wl)
        if li + 1 < depth:
            wl = {**w, **_assemble(rides.landed.pop("w1"), nxt[0]), **_assemble(rides.landed.pop("w2"), nxt[1])}
    dx, dx_b, dgf, loss = _loss_head("loss_head", x, target, _row(w["final_norm_g"]))
    layer_grads = [None] * depth
    dmods = [None] * depth
    landed = {}
    for li in reversed(range(depth)):
        dx, dx_b, layer_grads[li], dmods[li] = _layer_bwd(li, dx, dx_b, saved[li], mod_rows[li], layer_w[li], rides,
                                                          li == 0)
        if li + 1 < depth:
            landed[(li + 1, 0)] = rides.landed.pop("dr")
            landed[(li + 1, 1)] = rides.landed.pop("dh2")
        if li > 0:
            rides.pending["dr"], rides.pending["dh2"] = _grad_buffers(li, layer_grads[li])
    landed[(0, 1)] = rides.landed.pop("s5_dgl") + rides.landed.pop("s5_dwglu")
    landed[(0, 0)] = _exchange4("rs_g_L0", _grad_buffers(0, layer_grads[0], parts=[0])[0])
    return loss, dx, layer_grads, dgf.reshape(-1), jnp.concatenate(dmods, axis=0), landed


ANY = pl.BlockSpec(memory_space=pl.ANY)
N_DEV = 8
N_CHIP = 4


def _coords():
    return lax.axis_index("x"), lax.axis_index("y"), lax.axis_index("c")


def _allgather8(name, block):
    r, wd = block.shape

    def body(x_ref, out_ref, send_sems, recv_sems, local_sem):
        x, y, c = _coords()
        me, sibling = (x, y, c), (x, y, 1 - c)
        chips = [(1 - x, y), (x, 1 - y), (1 - x, 1 - y)]

        def slot(px, py, pc):
            return out_ref.at[4 * px + 2 * py + pc]

        def copy(k, blk, to, src=None):
            return pltpu.make_async_remote_copy(
                src_ref=slot(*blk) if src is None else src, dst_ref=slot(*blk),
                send_sem=send_sems.at[k], recv_sem=recv_sems.at[k], device_id=to, device_id_type=MESH)

        mine = pltpu.make_async_copy(x_ref, slot(*me), local_sem)
        mine.start()
        first = [copy(0, me, sibling, src=x_ref)]
        first += [copy(1 + j, me, (*chip, c), src=x_ref) for j, chip in enumerate(chips)]
        for cp in first:
            cp.start()
        passed = [copy(4 + j, (*chip, c), sibling) for j, chip in enumerate(chips)]
        for j, chip in enumerate(chips):
            copy(1 + j, (*chip, c), me).wait_recv()
            passed[j].start()
        copy(0, sibling, me).wait_recv()
        for j, chip in enumerate(chips):
            copy(4 + j, (*chip, 1 - c), me).wait_recv()
        for cp in first + passed:
            cp.wait_send()
        mine.wait()

    return pl.pallas_call(
        body, name=name, in_specs=[ANY], out_specs=ANY,
        out_shape=jax.ShapeDtypeStruct((N_DEV, r, wd), block.dtype),
        scratch_shapes=[pltpu.SemaphoreType.DMA((7,)), pltpu.SemaphoreType.DMA((7,)), pltpu.SemaphoreType.DMA],
    )(block)


def _landing_shape(src, kind):
    if kind == "same":
        return (N_CHIP,) + src.shape
    if kind == "packed":
        return src.shape
    rows, cols = src.shape
    return (N_CHIP, rows // N_CHIP, cols) if kind == "rows" else (N_CHIP, rows, cols // N_CHIP)


def _exchange4_ops(srcs, dsts, send_sems, recv_sems, local_sems, kinds):
    x, y, c = _coords()
    my_chip = 2 * x + y
    chips = [(1 - x, y), (x, 1 - y), (1 - x, 1 - y)]

    def piece(q, k):
        ref, kind = srcs[q], kinds[q]
        if kind == "same":
            return ref
        if kind == "packed":
            return ref.at[k]
        _, rows, cols = dsts[q].shape
        return ref.at[pl.ds(k * rows, rows), :] if kind == "rows" else ref.at[:, pl.ds(k * cols, cols)]

    def copy(q, j, k, slot):
        px, py = chips[j]
        return pltpu.make_async_remote_copy(
            src_ref=piece(q, k), dst_ref=dsts[q].at[slot], send_sem=send_sems.at[3 * q + j],
            recv_sem=recv_sems.at[3 * q + j], device_id=(px, py, c), device_id_type=MESH)

    def mine(q):
        return pltpu.make_async_copy(piece(q, my_chip), dsts[q].at[my_chip], local_sems.at[q])

    def start():
        for q in range(len(srcs)):
            mine(q).start()
            for j, (px, py) in enumerate(chips):
                copy(q, j, 2 * px + py, my_chip).start()

    def wait():
        for q in range(len(srcs)):
            for j, (px, py) in enumerate(chips):
                copy(q, j, my_chip, 2 * px + py).wait_recv()
        for q in range(len(srcs)):
            for j, (px, py) in enumerate(chips):
                copy(q, j, 2 * px + py, my_chip).wait_send()
            mine(q).wait()

    return start, wait


def _exchange_scratch(n):
    return [pltpu.SemaphoreType.DMA((3 * n,)), pltpu.SemaphoreType.DMA((3 * n,)), pltpu.SemaphoreType.DMA((n,))]


def _exchange4(name, buffers):
    n = len(buffers)
    kinds = [kind for _, kind in buffers]

    def body(*refs):
        start, wait = _exchange4_ops(refs[:n], refs[n:2 * n], *refs[2 * n:], kinds)
        start()
        wait()

    return pl.pallas_call(
        body, name=name, in_specs=[ANY] * n, out_specs=[ANY] * n,
        out_shape=[jax.ShapeDtypeStruct(_landing_shape(s, k), s.dtype) for s, k in buffers],
        scratch_shapes=_exchange_scratch(n),
    )(*[s for s, _ in buffers])


def _swap_sibling(name, block):
    def body(x_ref, out_ref, send_sem, recv_sem):
        x, y, c = _coords()
        cp = pltpu.make_async_remote_copy(src_ref=x_ref, dst_ref=out_ref, send_sem=send_sem, recv_sem=recv_sem,
                                          device_id=(x, y, 1 - c), device_id_type=MESH)
        cp.start()
        cp.wait()

    return pl.pallas_call(
        body, name=name, in_specs=[ANY], out_specs=ANY, out_shape=jax.ShapeDtypeStruct(block.shape, block.dtype),
        scratch_shapes=[pltpu.SemaphoreType.DMA, pltpu.SemaphoreType.DMA],
    )(block)


def _sum_slots(name, stacked):
    n, r, wd = stacked.shape
    tile = min(FLAT_ROWS, r)

    def kern(x_ref, o_ref):
        acc = x_ref[0].astype(F32)
        for s in range(1, n):
            acc = acc + x_ref[s].astype(F32)
        o_ref[...] = acc

    return pl.pallas_call(
        kern, name=name, grid=(r // tile,), in_specs=[pl.BlockSpec((n, tile, wd), lambda i: (0, i, 0))],
        out_specs=pl.BlockSpec((tile, wd), lambda i: (i, 0)), out_shape=jax.ShapeDtypeStruct((r, wd), F32),
        compiler_params=_params("parallel"),
    )(stacked)


def _adamw(name, w, m, v, g, g2=None):
    r, wd = w.shape
    grads = [g] if g2 is None else [g, g2]
    c1 = 1.0 - ADAM_B1 ** ADAM_STEP
    c2 = 1.0 - ADAM_B2 ** ADAM_STEP

    def body(i, *refs):
        w_ref, m_ref, v_ref = refs[:3]
        g_refs = refs[3:3 + len(grads)]
        go_ref, d_ref, mo_ref, vo_ref = refs[3 + len(grads):]
        gv = g_refs[0][...]
        if g2 is not None:
            gv = gv + g_refs[1][...]
        mn = ADAM_B1 * m_ref[...] + (1.0 - ADAM_B1) * gv
        vn = ADAM_B2 * v_ref[...] + (1.0 - ADAM_B2) * (gv * gv)
        go_ref[...] = gv
        mo_ref[...] = mn
        vo_ref[...] = vn
        d_ref[...] = -ADAM_LR * ((mn / c1) / (jnp.sqrt(vn / c2) + ADAM_EPS) + ADAM_WD * w_ref[...])

    return _rowcall(name, body, r, FLAT_ROWS, [w, m, v] + grads, [], [(wd, F32)] * 4, [])


FLAT_BLOCK = FLAT_ROWS * FLAT_W


def _pack(arrays, dtype):
    flat = jnp.concatenate([a.reshape(-1).astype(dtype) for a in arrays])
    pad = (-flat.shape[0]) % FLAT_BLOCK
    return jnp.pad(flat, (0, pad)).reshape(-1, FLAT_W)


def _unpack(buf, shapes):
    flat = buf.reshape(-1)
    out, off = [], 0
    for s in shapes:
        n = math.prod(s)
        out.append(flat[off:off + n].reshape(s))
        off += n
    return out


SHARDED_BIG = {"mlp_w1": 2, "mlp_w2": 1, "s5_w_in": 1, "s5_w_glu": 2, "m2_w_in": 2, "m2_w_out": 1}
SHARDED_SMALL = {"m2_conv_w": 2, "m2_conv_b": 1, "m2_norm_g": 1}
REPLICATED = ("ada_b", "norm_mix_g", "norm_mlp_g", "s5_lambda_re", "s5_lambda_im", "s5_log_dt", "s5_b_re",
              "s5_b_im", "s5_c_re", "s5_c_im", "s5_d", "s5_b_glu", "m2_dt_bias", "m2_a_log", "m2_d", "final_norm_g")
WEIGHT_NAMES = ("ada_w", "ada_b", "norm_mix_g", "norm_mlp_g", "mlp_w1", "mlp_w2", "s5_w_in", "s5_lambda_re",
                "s5_lambda_im", "s5_log_dt", "s5_b_re", "s5_b_im", "s5_c_re", "s5_c_im", "s5_d", "s5_w_glu",
                "s5_b_glu", "m2_w_in", "m2_conv_w", "m2_conv_b", "m2_dt_bias", "m2_a_log", "m2_d", "m2_norm_g",
                "m2_w_out", "final_norm_g")


def _gather_weights(name, local, names_axes, dtype):
    names = list(names_axes)
    got = _exchange4(name, [(_pack([local[k] for k in names], dtype), "same")])[0]
    per_chip = [_unpack(got[j], [local[k].shape for k in names]) for j in range(N_CHIP)]
    return {k: jnp.concatenate([per_chip[j][i] for j in range(N_CHIP)], axis=names_axes[k])
            for i, k in enumerate(names)}


def _chip_slice(a, chip, axis):
    size = a.shape[axis] // N_CHIP
    return lax.slice_in_dim(a, chip * size, (chip + 1) * size, axis=axis)


def kernel(x, c, ada_w, ada_b, norm_mix_g, norm_mlp_g, mlp_w1, mlp_w2, s5_w_in, s5_lambda_re, s5_lambda_im, s5_log_dt, s5_b_re, s5_b_im, s5_c_re, s5_c_im, s5_d, s5_w_glu, s5_b_glu, m2_w_in, m2_conv_w, m2_conv_b, m2_dt_bias, m2_a_log, m2_d, m2_norm_g, m2_w_out, final_norm_g, loss_target, m_ada_w, m_ada_b, m_norm_mix_g, m_norm_mlp_g, m_mlp_w1, m_mlp_w2, m_s5_w_in, m_s5_lambda_re, m_s5_lambda_im, m_s5_log_dt, m_s5_b_re, m_s5_b_im, m_s5_c_re, m_s5_c_im, m_s5_d, m_s5_w_glu, m_s5_b_glu, m_m2_w_in, m_m2_conv_w, m_m2_conv_b, m_m2_dt_bias, m_m2_a_log, m_m2_d, m_m2_norm_g, m_m2_w_out, m_final_norm_g, v_ada_w, v_ada_b, v_norm_mix_g, v_norm_mlp_g, v_mlp_w1, v_mlp_w2, v_s5_w_in, v_s5_lambda_re, v_s5_lambda_im, v_s5_log_dt, v_s5_b_re, v_s5_b_im, v_s5_c_re, v_s5_c_im, v_s5_d, v_s5_w_glu, v_s5_b_glu, v_m2_w_in, v_m2_conv_w, v_m2_conv_b, v_m2_dt_bias, v_m2_a_log, v_m2_d, v_m2_norm_g, v_m2_w_out, v_final_norm_g):
    args = locals()
    local = {k: args[k] for k in WEIGHT_NAMES}
    mom_m = {k: args["m_" + k] for k in WEIGHT_NAMES}
    mom_v = {k: args["v_" + k] for k in WEIGHT_NAMES}
    depth, d = norm_mix_g.shape
    xi, yi, ci = _coords()
    my_chip = 2 * xi + yi
    my_dev = 2 * my_chip + ci

    cond = jax.nn.silu(c).reshape(-1, LANES)
    cond_all = _allgather8("ag_cond", cond).reshape(N_DEV, d)
    cond_pad = jnp.zeros((LANES, d), F32).at[:N_DEV].set(cond_all)
    mod_cols = ada_w.shape[2]
    mod_part = jnp.stack([_matmul("ada_%d" % i, cond_pad, ada_w[i])[:N_DEV] for i in range(depth)])
    mod_all = _allgather8("ag_mod", mod_part.reshape(-1, LANES)).reshape(N_CHIP, 2, depth, N_DEV, mod_cols)[:, 0]
    mod_mine = lax.dynamic_index_in_dim(mod_all, my_dev, axis=2, keepdims=False)
    mods = jnp.transpose(mod_mine, (1, 0, 2)).reshape(depth, N_CHIP * mod_cols) + ada_b

    w = {k: local[k] for k in REPLICATED}
    w.update(_gather_weights("ag_w_small", local, SHARDED_SMALL, F32))

    loss_row, dx, layer_grads, g_final, dmods, landed = _local_step(x[0], loss_target[0], mods, w, local)
    grads = {"ada_b": dmods, "final_norm_g": g_final}
    for k in REPLICATED[1:-1]:
        grads[k] = jnp.stack([g[k] for g in layer_grads if k in g])

    rep_shapes = [grads[k].shape for k in REPLICATED]
    rep_all = _allgather8("ag_grep", _pack([grads[k] for k in REPLICATED], F32))
    rep_sum = _sum_slots("sum_grep", rep_all)
    dmods_all = rep_all.reshape(N_DEV, -1)[:, :dmods.size].reshape(N_DEV, depth, N_CHIP * mod_cols)

    dm_mine = lax.dynamic_slice_in_dim(dmods_all, my_chip * mod_cols, mod_cols, axis=2)
    dm_pad = jnp.zeros((LANES, depth, mod_cols), F32).at[:N_DEV].set(dm_mine)
    g_ada_w = jnp.stack([_matmul("dada_%d" % i, cond_pad, dm_pad[:, i], "tn") for i in range(depth)])

    red = {}
    for li in range(depth):
        for k, part in enumerate(_grad_specs(li)):
            for (kind, entry), land in zip(part, landed[(li, k)]):
                if kind == "one":
                    red[entry[:2]] = _sum_slots("sum_%s_%d" % entry[:2], land)
                else:
                    total = _sum_slots("sum_packed_L%d" % li, land)
                    shapes = [local[n][i].shape for n, i, _ in entry]
                    red.update({(n, i): v for (n, i, _), v in zip(entry, _unpack(total, shapes))})

    out_g, out_d, out_m, out_v = {}, {}, {}, {}
    for name in list(SHARDED_BIG) + list(SHARDED_SMALL):
        shape = local[name].shape
        flat2 = (-1, shape[-1])
        g1 = jnp.stack([red[(name, i)] for i in range(shape[0])]).reshape(flat2)
        g2 = _swap_sibling("swap_" + name, g1)
        res = _adamw("adam_" + name, local[name].reshape(flat2), mom_m[name].reshape(flat2),
                     mom_v[name].reshape(flat2), g1, g2)
        for dst, buf in zip((out_g, out_d, out_m, out_v), res):
            dst[name] = buf.reshape(shape)
    res = _adamw("adam_rep", _pack([local[k] for k in REPLICATED], F32), _pack([mom_m[k] for k in REPLICATED], F32),
                 _pack([mom_v[k] for k in REPLICATED], F32), rep_sum)
    for dst, buf in zip((out_g, out_d, out_m, out_v), res):
        dst.update(zip(REPLICATED, _unpack(buf, rep_shapes)))
    flat2 = (-1, mod_cols)
    res = _adamw("adam_ada", ada_w.reshape(flat2), m_ada_w.reshape(flat2), v_ada_w.reshape(flat2),
                 g_ada_w.reshape(flat2))
    for dst, buf in zip((out_g, out_d, out_m, out_v), res):
        dst["ada_w"] = buf.reshape(ada_w.shape)

    loss = lax.psum(loss_row[0, 0], ("x", "y", "c"))
    outs = [loss, dx[None]]
    for dst in (out_g, out_d, out_m, out_v):
        outs += [dst[k] for k in WEIGHT_NAMES]
    return tuple(outs)
```

```python
import math

import jax
import jax.numpy as jnp
from jax import lax
from jax.experimental import pallas as pl
from jax.experimental.pallas import tpu as pltpu

F32 = jnp.float32
BF16 = jnp.bfloat16
HIGHEST = lax.Precision.HIGHEST

NORM_EPS = 1e-5
N_MOD = 6
S5_H, S5_P, S5_T = 16, 64, 64
M2_P, M2_N, M2_G, M2_Q, M2_K = 64, 128, 4, 128, 4
LANES = 128
ADAM_LR, ADAM_B1, ADAM_B2, ADAM_EPS, ADAM_WD, ADAM_STEP = 0.001, 0.9, 0.999, 1e-08, 0.01, 10
VMEM_LIMIT_BYTES = 56 * 1024 * 1024
ROW_TILE = 256
FLAT_W = 1024
FLAT_ROWS = 256
MESH = pl.DeviceIdType.MESH


def _params(*sem):
    return pltpu.CompilerParams(dimension_semantics=sem, vmem_limit_bytes=VMEM_LIMIT_BYTES)


def _dot(a, b, dn="nn", precision=None):
    dims = {"nn": ((1,), (0,)), "nt": ((1,), (1,)), "tn": ((0,), (0,))}[dn]
    return lax.dot_general(a, b, (dims, ((), ())), preferred_element_type=F32, precision=precision)


def _bdot(a, b, dn="nn"):
    return _dot(a.astype(BF16), b.astype(BF16), dn)


def _sigmoid(x):
    return jax.nn.sigmoid(x)


def _colsum(x):
    return jnp.sum(x, axis=0, keepdims=True)


def _pick_tile(dim, want):
    if dim <= want:
        return dim
    for t in range(want - want % LANES, 0, -LANES):
        if dim % t == 0:
            return t
    raise ValueError((dim, want))


MATMUL_TILE = 1024
MATMUL_VMEM_BUDGET = 40 * 1024 * 1024


def _matmul_tiles(m, n, k, mode, in_bytes, out_bytes):
    tn = _pick_tile(n, MATMUL_TILE)
    k_tiles = [k] + [t for t in (4096, 2048, 1024) if t < k and k % t == 0]
    m_tiles = [_pick_tile(m, MATMUL_TILE)] + ([512] if mode != "tn" and m % 512 == 0 and m > 512 else [])
    for tk in k_tiles:
        for tm in m_tiles:
            blocks = 2 * (tm * tk * in_bytes[0] + tk * tn * in_bytes[1] + tm * tn * out_bytes)
            if blocks + (4 * tm * tn if tk < k else 0) <= MATMUL_VMEM_BUDGET:
                return tm, tn, tk
    raise ValueError((m, n, k))


def _matmul(name, a, b, mode="nn", out_dtype=F32, relu=False, square_a=False, mul2=None, colscale=None,
            addin=None, colsum_with=None, ride=None):
    if mode == "nn":
        (m, k), (k2, n) = a.shape, b.shape
    elif mode == "nt":
        (m, k), (n, k2) = a.shape, b.shape
    else:
        (k, m), (k2, n) = a.shape, b.shape
    assert k == k2, (name, a.shape, b.shape)
    tiles = [e for e in (mul2, addin, colsum_with) if e is not None]
    out_bytes = jnp.dtype(out_dtype).itemsize + sum(e.dtype.itemsize for e in tiles)
    tm, tn, tk = _matmul_tiles(m, n, k, mode, (a.dtype.itemsize, b.dtype.itemsize), out_bytes)
    nk = k // tk
    n_ext = len(tiles) + (colscale is not None)
    n_out = 1 + (colsum_with is not None)
    n_ride = 0 if ride is None else len(ride)
    grid = (m // tm, n // tn, nk)

    def kern(*refs):
        a_ref, b_ref = refs[:2]
        e_refs = list(refs[2:2 + n_ext])
        o_refs = refs[2 + n_ext + n_ride:2 + n_ext + n_ride + n_out]
        kk = pl.program_id(2)
        if ride is not None:
            here = [pl.program_id(ax) for ax in range(3)]
            land0 = 2 + n_ext + n_ride + n_out
            ride_refs = (refs[2 + n_ext:2 + n_ext + n_ride], refs[land0:land0 + n_ride]) + tuple(refs[-3:])
            ride_kinds = [kind for _, kind in ride]
            first = (here[0] == 0) & (here[1] == 0) & (here[2] == 0)
            last = (here[0] == grid[0] - 1) & (here[1] == grid[1] - 1) & (here[2] == grid[2] - 1)

            @pl.when(first)
            def _():
                _exchange4_ops(*ride_refs, ride_kinds)[0]()

        av = a_ref[...]
        if square_a:
            av = av * av
        part = _bdot(av, b_ref[...], mode)

        def finish(r):
            ext = list(e_refs)
            m2v = ext.pop(0)[...].astype(F32) if mul2 is not None else None
            addv = ext.pop(0)[...].astype(F32) if addin is not None else None
            if colsum_with is not None:
                o_refs[1][0] = _colsum(r * ext.pop(0)[...].astype(F32))
            if relu:
                r = jnp.maximum(r, 0.0)
            if m2v is not None:
                r = r * (2.0 * m2v)
            if colscale is not None:
                r = r * ext.pop(0)[...]
            if addv is not None:
                r = r + addv
            o_refs[0][...] = r.astype(out_dtype)

        if nk == 1:
            finish(part)
        else:
            acc = refs[-4] if ride is not None else refs[-1]

            @pl.when(kk == 0)
            def _():
                acc[...] = part

            @pl.when(kk > 0)
            def _():
                acc[...] += part

            @pl.when(kk == nk - 1)
            def _():
                finish(acc[...])

        if ride is not None:
            @pl.when(last)
            def _():
                _exchange4_ops(*ride_refs, ride_kinds)[1]()

    if mode == "tn":
        a_spec = pl.BlockSpec((tk, tm), lambda i, j, kk: (kk, i))
    else:
        a_spec = pl.BlockSpec((tm, tk), lambda i, j, kk: (i, kk))
    if mode == "nt":
        b_spec = pl.BlockSpec((tn, tk), lambda i, j, kk: (j, kk))
    else:
        b_spec = pl.BlockSpec((tk, tn), lambda i, j, kk: (kk, j))
    o_spec = pl.BlockSpec((tm, tn), lambda i, j, kk: (i, j))
    in_specs = [a_spec, b_spec] + [o_spec] * len(tiles)
    operands = [a, b] + tiles
    if colscale is not None:
        in_specs.append(pl.BlockSpec((1, tn), lambda i, j, kk: (0, j)))
        operands.append(colscale)
    out_specs = [o_spec]
    out_shape = [jax.ShapeDtypeStruct((m, n), out_dtype)]
    if colsum_with is not None:
        out_specs.append(pl.BlockSpec((1, 1, tn), lambda i, j, kk: (i, 0, j)))
        out_shape.append(jax.ShapeDtypeStruct((m // tm, 1, n), F32))
    scratch = [pltpu.VMEM((tm, tn), F32)] if nk > 1 else []
    semantics = ("parallel", "parallel", "arbitrary")
    if ride is not None:
        for src, kind in ride:
            in_specs.append(pl.BlockSpec(memory_space=pl.ANY))
            operands.append(src)
            out_specs.append(pl.BlockSpec(memory_space=pl.ANY))
            out_shape.append(jax.ShapeDtypeStruct(_landing_shape(src, kind), src.dtype))
        scratch += _exchange_scratch(n_ride)
        semantics = ("arbitrary", "arbitrary", "arbitrary")
    res = pl.pallas_call(
        kern, name=name, grid=grid, in_specs=in_specs, out_specs=out_specs, out_shape=out_shape,
        scratch_shapes=scratch, compiler_params=_params(*semantics),
    )(*operands)
    return res if len(res) > 1 else res[0]


def _rowcall(name, body, rows, tile, row_ins, small_ins, row_outs, acc_outs):
    tile = min(tile, rows)
    assert rows % tile == 0, (name, rows, tile)
    n_in = len(row_ins) + len(small_ins)

    def kern(*refs):
        i = pl.program_id(0)
        accs = refs[n_in + len(row_outs):]

        @pl.when(i == 0)
        def _():
            for acc in accs:
                acc[...] = jnp.zeros_like(acc)

        body(i, *refs)

    def whole(shape):
        return pl.BlockSpec(shape, lambda i, nd=len(shape): (0,) * nd)

    in_specs = [pl.BlockSpec((tile, a.shape[1]), lambda i: (i, 0)) for a in row_ins]
    in_specs += [whole(a.shape) for a in small_ins]
    out_specs = [pl.BlockSpec((tile, w), lambda i: (i, 0)) for (w, _) in row_outs]
    out_specs += [whole(s) for s in acc_outs]
    out_shape = [jax.ShapeDtypeStruct((rows, w), dt) for (w, dt) in row_outs]
    out_shape += [jax.ShapeDtypeStruct(s, F32) for s in acc_outs]
    return pl.pallas_call(
        kern, name=name, grid=(rows // tile,), in_specs=in_specs, out_specs=out_specs, out_shape=out_shape,
        compiler_params=_params("arbitrary"),
    )(*row_ins, *small_ins)


def _rms(x):
    r = lax.rsqrt(jnp.mean(x * x, axis=-1, keepdims=True) + NORM_EPS)
    return x * r, r


def _rms_bwd(dxhat, xhat, r):
    return r * (dxhat - xhat * jnp.mean(dxhat * xhat, axis=-1, keepdims=True))


def _normmod_fwd(name, x, g, sh, sc):
    def body(i, x_ref, g_ref, sh_ref, sc_ref, o_ref):
        xhat, _ = _rms(x_ref[...])
        o_ref[...] = ((xhat * g_ref[...]) * (1.0 + sc_ref[...]) + sh_ref[...]).astype(BF16)

    return _rowcall(name, body, x.shape[0], ROW_TILE, [x], [g, sh, sc], [(x.shape[1], BF16)], [])[0]


def _normmod_bwd(name, x, g, sh, sc, dh, dx_pass):
    d = x.shape[1]

    def body(i, x_ref, dh_ref, dxp_ref, g_ref, sh_ref, sc_ref, dx_ref, dxb_ref, dg_ref, dsh_ref, dsc_ref):
        xhat, r = _rms(x_ref[...])
        dh = dh_ref[...].astype(F32)
        gv = g_ref[...]
        dn = dh * (1.0 + sc_ref[...])
        dsc_ref[...] += _colsum(dh * (xhat * gv))
        dsh_ref[...] += _colsum(dh)
        dg_ref[...] += _colsum(dn * xhat)
        dx = dxp_ref[...] + _rms_bwd(dn * gv, xhat, r)
        dx_ref[...] = dx
        dxb_ref[...] = dx.astype(BF16)

    return _rowcall(name, body, x.shape[0], ROW_TILE, [x, dh, dx_pass], [g, sh, sc], [(d, F32), (d, BF16)],
                    [(1, d), (1, d), (1, d)])


def _scale_cols(name, w, g):
    def body(i, w_ref, g_ref, o_ref):
        o_ref[...] = (w_ref[...].astype(F32) * g_ref[...]).astype(BF16)

    return _rowcall(name, body, w.shape[0], ROW_TILE, [w], [g], [(w.shape[1], BF16)], [])[0]


GELU_K = math.sqrt(2.0 / math.pi)
GELU_C = 0.044715


def _gelu_fwd(name, y, u, skip):
    def body(i, y_ref, u_ref, s_ref, o_ref):
        v = y_ref[...].astype(F32) + s_ref[...] * u_ref[...]
        t = jnp.tanh(GELU_K * (v + GELU_C * (v * v * v)))
        o_ref[...] = (0.5 * v * (1.0 + t)).astype(BF16)

    return _rowcall(name, body, y.shape[0], ROW_TILE, [y, u], [skip], [(y.shape[1], BF16)], [])[0]


def _gelu_bwd(name, y, u, skip, dgl):
    d = y.shape[1]

    def body(i, y_ref, u_ref, d_ref, s_ref, o_ref, ds_ref):
        uv = u_ref[...]
        v = y_ref[...].astype(F32) + s_ref[...] * uv
        t = jnp.tanh(GELU_K * (v + GELU_C * (v * v * v)))
        dv = d_ref[...] * (0.5 * (1.0 + t) + 0.5 * v * (1.0 - t * t) * (GELU_K * (1.0 + 3.0 * GELU_C * v * v)))
        o_ref[...] = dv
        ds_ref[...] += _colsum(dv * uv)

    return _rowcall(name, body, y.shape[0], ROW_TILE, [y, u, dgl], [skip], [(d, F32)], [(1, d)])


def _axpy(name, a, b, scale):
    def body(i, a_ref, b_ref, s_ref, o_ref):
        o_ref[...] = (a_ref[...].astype(F32) + s_ref[...] * b_ref[...]).astype(BF16)

    return _rowcall(name, body, a.shape[0], ROW_TILE, [a, b], [scale], [(a.shape[1], BF16)], [])[0]


def _glu_fwd(name, ab, bias, x, gate):
    d = ab.shape[1] // 2

    def body(i, ab_ref, x_ref, b_ref, g_ref, o_ref):
        v = ab_ref[:, :d] + b_ref[:, :d]
        gt = ab_ref[:, d:] + b_ref[:, d:]
        o_ref[...] = x_ref[...] + g_ref[...] * (v * _sigmoid(gt))

    return _rowcall(name, body, ab.shape[0], ROW_TILE, [ab, x], [bias, gate], [(d, F32)], [])[0]


def _glu_bwd(name, ab, bias, dxo, gate):
    d = ab.shape[1] // 2

    def body(i, ab_ref, dx_ref, b_ref, g_ref, dab_ref, db_ref, dg_ref):
        v = ab_ref[:, :d] + b_ref[:, :d]
        s = _sigmoid(ab_ref[:, d:] + b_ref[:, d:])
        dxo_v = dx_ref[...]
        dg_ref[...] += _colsum(dxo_v * (v * s))
        do = g_ref[...] * dxo_v
        dv = do * s
        dgt = do * v * (s * (1.0 - s))
        dab_ref[:, :d] = dv.astype(BF16)
        dab_ref[:, d:] = dgt.astype(BF16)
        db_ref[:, :d] += _colsum(dv)
        db_ref[:, d:] += _colsum(dgt)

    return _rowcall(name, body, ab.shape[0], ROW_TILE, [ab, dxo], [bias, gate], [(2 * d, BF16)],
                    [(1, 2 * d), (1, d)])


def _gatenorm_fwd(name, y, z, ng):
    di = y.shape[1]
    gw = di // M2_G

    def body(i, y_ref, z_ref, g_ref, o_ref):
        for gi in range(M2_G):
            sl = slice(gi * gw, (gi + 1) * gw)
            zz = z_ref[:, sl]
            y2 = y_ref[:, sl] * (zz * _sigmoid(zz))
            yh, _ = _rms(y2)
            o_ref[:, sl] = (yh * g_ref[:, sl]).astype(BF16)

    return _rowcall(name, body, y.shape[0], ROW_TILE, [y, z], [ng], [(di, BF16)], [])[0]


def _gatenorm_bwd(name, y, z, ng, dyn):
    di = y.shape[1]
    gw = di // M2_G

    def body(i, y_ref, z_ref, d_ref, g_ref, dy_ref, dz_ref, dg_ref):
        for gi in range(M2_G):
            sl = slice(gi * gw, (gi + 1) * gw)
            zz = z_ref[:, sl]
            yy = y_ref[:, sl]
            s = _sigmoid(zz)
            sz = zz * s
            yh, r = _rms(yy * sz)
            dn = d_ref[:, sl]
            dg_ref[:, sl] += _colsum(dn * yh)
            dy2 = _rms_bwd(dn * g_ref[:, sl], yh, r)
            dy_ref[:, sl] = dy2 * sz
            dz_ref[:, sl] = (dy2 * yy * (s * (1.0 + zz * (1.0 - s)))).astype(BF16)

    return _rowcall(name, body, y.shape[0], ROW_TILE, [y, z, dyn], [ng], [(di, F32), (di, BF16)], [(1, di)])


def _loss_head(name, x, target, g):
    d = x.shape[1]

    def body(i, x_ref, t_ref, g_ref, dx_ref, dxb_ref, dg_ref, loss_ref):
        xhat, r = _rms(x_ref[...])
        gv = g_ref[...]
        err = xhat * gv - t_ref[...]
        per_row = jnp.sum(err * err, axis=-1, keepdims=True) * (0.5 / d)
        loss_ref[...] += jnp.broadcast_to(_colsum(per_row), loss_ref.shape)
        dy = err * (1.0 / d)
        dg_ref[...] += _colsum(dy * xhat)
        dx = _rms_bwd(dy * gv, xhat, r)
        dx_ref[...] = dx
        dxb_ref[...] = dx.astype(BF16)

    return _rowcall(name, body, x.shape[0], ROW_TILE, [x, target], [g], [(d, F32), (d, BF16)],
                    [(1, d), (1, LANES)])


HALO = 8


def _halo_call(name, body, rows, tile, width, mains, halo_of, halo_next, smalls, row_outs, acc_outs, scratch):
    tile = min(tile, rows)
    nb = tile // HALO
    last = rows // HALO - 1
    n_in = len(mains) + 1 + len(smalls)

    def kern(*refs):
        i = pl.program_id(0)
        accs = refs[n_in + len(row_outs):n_in + len(row_outs) + len(acc_outs)]

        @pl.when(i == 0)
        def _():
            for acc in accs:
                acc[...] = jnp.zeros_like(acc)

        body(i, *refs)

    def whole(shape):
        return pl.BlockSpec(shape, lambda i, nd=len(shape): (0,) * nd)

    if halo_next:
        halo_spec = pl.BlockSpec((HALO, width), lambda i: (jnp.minimum((i + 1) * nb, last), 0))
    else:
        halo_spec = pl.BlockSpec((HALO, width), lambda i: (jnp.maximum(i * nb - 1, 0), 0))
    in_specs = [pl.BlockSpec((tile, a.shape[1]), lambda i: (i, 0)) for a in mains] + [halo_spec]
    in_specs += [whole(a.shape) for a in smalls]
    out_specs = [pl.BlockSpec((tile, w), lambda i: (i, 0)) for (w, _) in row_outs] + [whole(s) for s in acc_outs]
    out_shape = [jax.ShapeDtypeStruct((rows, w), dt) for (w, dt) in row_outs]
    out_shape += [jax.ShapeDtypeStruct(s, F32) for s in acc_outs]
    return pl.pallas_call(
        kern, name=name, grid=(rows // tile,), in_specs=in_specs, out_specs=out_specs, out_shape=out_shape,
        scratch_shapes=scratch, compiler_params=_params("arbitrary"),
    )(*mains, mains[halo_of], *smalls)


CONV_TILE = 256
CONV_ROWS = 16
CONV_STRIP = 512


def _conv_blocks(tile, c, strip=CONV_STRIP):
    strip = strip if c % strip == 0 else LANES
    rb = min(CONV_ROWS, tile)
    return [(r0, rb, slice(c0, c0 + strip)) for c0 in range(0, c, strip) for r0 in range(0, tile, rb)]


def _shifted_windows(base, rb, offsets):
    n = base.shape[0]
    out = []
    for o in offsets:
        if o % HALO == 0:
            out.append(base[o:o + rb, :])
        else:
            out.append(pltpu.roll(base, n - o, 0)[0:rb, :])
    return out


def _conv_windows(x_ref, ext, r0, rb, sl):
    base = ext[:, sl] if r0 == 0 else x_ref[r0 - HALO:r0 + rb, sl]
    return _shifted_windows(base, rb, [HALO - 3 + k for k in range(M2_K)])


def _conv_fwd(name, xin, w, b):
    rows, c = xin.shape
    tile = min(CONV_TILE, rows)
    rb0 = min(CONV_ROWS, tile)

    def body(i, x_ref, h_ref, w_ref, b_ref, o_ref, ext):
        ext[0:HALO, :] = jnp.where(i == 0, 0.0, h_ref[...])
        ext[HALO:, :] = x_ref[0:rb0, :]
        for r0, rb, sl in _conv_blocks(tile, c):
            taps = _conv_windows(x_ref, ext, r0, rb, sl)
            pre = b_ref[:, sl] + w_ref[0:1, sl] * taps[0]
            for k in range(1, M2_K):
                pre = pre + w_ref[k:k + 1, sl] * taps[k]
            o_ref[r0:r0 + rb, sl] = pre * _sigmoid(pre)

    return _halo_call(name, body, rows, tile, c, [xin], 0, False, [w, b], [(c, F32)], [],
                      [pltpu.VMEM((rb0 + HALO, c), F32)])[0]


def _conv_bwd_pre(name, xin, dout, w, b):
    rows, c = xin.shape
    tile = min(CONV_TILE, rows)
    rb0 = min(CONV_ROWS, tile)

    def body(i, x_ref, d_ref, h_ref, w_ref, b_ref, dp_ref, dw_ref, db_ref, ext):
        ext[0:HALO, :] = jnp.where(i == 0, 0.0, h_ref[...])
        ext[HALO:, :] = x_ref[0:rb0, :]
        sums = {}
        for r0, rb, sl in _conv_blocks(tile, c, CONV_STRIP // 2):
            taps = _conv_windows(x_ref, ext, r0, rb, sl)
            pre = b_ref[:, sl] + w_ref[0:1, sl] * taps[0]
            for k in range(1, M2_K):
                pre = pre + w_ref[k:k + 1, sl] * taps[k]
            s = _sigmoid(pre)
            dp = d_ref[r0:r0 + rb, sl] * (s * (1.0 + pre * (1.0 - s)))
            dp_ref[r0:r0 + rb, sl] = dp
            part = [dp] + [dp * taps[k] for k in range(M2_K)]
            key = sl.start
            sums[key] = part if key not in sums else [p + q for p, q in zip(sums[key], part)]
            if r0 + rb == tile:
                db_ref[:, sl] += _colsum(sums[key][0])
                for k in range(M2_K):
                    dw_ref[k:k + 1, sl] += _colsum(sums[key][1 + k])

    return _halo_call(name, body, rows, tile, c, [xin, dout], 0, False, [w, b], [(c, F32)], [(M2_K, c), (1, c)],
                      [pltpu.VMEM((rb0 + HALO, c), F32)])


def _conv_bwd_in(name, dpre, w):
    rows, c = dpre.shape
    tile = min(CONV_TILE, rows)
    n_tiles = rows // tile
    rb0 = min(CONV_ROWS, tile)

    def body(i, d_ref, h_ref, w_ref, o_ref, ext):
        ext[0:rb0, :] = d_ref[tile - rb0:tile, :]
        ext[rb0:, :] = jnp.where(i == n_tiles - 1, 0.0, h_ref[...])
        for r0, rb, sl in _conv_blocks(tile, c):
            base = ext[:, sl] if r0 + rb == tile else d_ref[r0:r0 + rb + HALO, sl]
            wins = _shifted_windows(base, rb, [3 - k for k in range(M2_K)])
            acc = w_ref[0:1, sl] * wins[0]
            for k in range(1, M2_K):
                acc = acc + w_ref[k:k + 1, sl] * wins[k]
            o_ref[r0:r0 + rb, sl] = acc.astype(BF16)

    return _halo_call(name, body, rows, tile, c, [dpre], 0, True, [w], [(c, BF16)], [],
                      [pltpu.VMEM((rb0 + HALO, c), F32)])[0]


def _s5_build(lam_re, lam_im, log_dt, b_re, b_im, c_re, c_im):
    g, p = lam_re.shape
    h = b_re.shape[-1]
    t = S5_T
    dt = jnp.exp(log_dt)[:, None]
    ld_re, ld_im = lam_re * dt, lam_im * dt
    tau = jnp.arange(t + 1, dtype=F32)
    mag = jnp.exp(ld_re[:, :, None] * tau)
    ang = ld_im[:, :, None] * tau
    pw_re, pw_im = mag * jnp.cos(ang), mag * jnp.sin(ang)
    num_re, num_im = pw_re[:, :, 1] - 1.0, pw_im[:, :, 1]
    den = lam_re * lam_re + lam_im * lam_im
    q_re = (num_re * lam_re + num_im * lam_im) / den
    q_im = (num_im * lam_re - num_re * lam_im) / den
    bb_re = q_re[:, :, None] * b_re - q_im[:, :, None] * b_im
    bb_im = q_re[:, :, None] * b_im + q_im[:, :, None] * b_re
    bbt_re, bbt_im = jnp.transpose(bb_re, (0, 2, 1)), jnp.transpose(bb_im, (0, 2, 1))
    ct_re, ct_im = jnp.transpose(c_re, (0, 2, 1)), jnp.transpose(c_im, (0, 2, 1))
    lane = jnp.arange(t * h)
    rep_tau = (lane[None, :] // h == jnp.arange(t)[:, None]).astype(F32)
    tile_h = (lane[None, :] % h == jnp.arange(h)[:, None]).astype(F32)

    def spread(x, m):
        return jnp.einsum("gpk,kn->gpn", x, m, precision=lax.Precision.HIGH)

    c_re_n, c_im_n = spread(ct_re, tile_h), spread(ct_im, tile_h)

    def c_times_pw(first):
        pr, pi = spread(pw_re[:, :, first:first + t], rep_tau), spread(pw_im[:, :, first:first + t], rep_tau)
        return pr * c_re_n - pi * c_im_n, pr * c_im_n + pi * c_re_n

    cp0_re, cp0_im = c_times_pw(0)
    cp1_re, cp1_im = c_times_pw(1)
    kc = (jnp.einsum("ghp,gpn->ghn", bbt_re, cp0_re, precision=lax.Precision.HIGH)
          - jnp.einsum("ghp,gpn->ghn", bbt_im, cp0_im, precision=lax.Precision.HIGH))
    bp_re = jnp.transpose(pw_re[:, :, t - 1::-1][:, :, :t], (0, 2, 1))
    bp_im = jnp.transpose(pw_im[:, :, t - 1::-1][:, :, :t], (0, 2, 1))
    be_re = bbt_re[:, :, None, :] * bp_re[:, None, :, :] - bbt_im[:, :, None, :] * bp_im[:, None, :, :]
    be_im = bbt_re[:, :, None, :] * bp_im[:, None, :, :] + bbt_im[:, :, None, :] * bp_re[:, None, :, :]
    bend = jnp.concatenate([be_re, be_im], axis=-1).reshape(g, h * t, 2 * p)
    cpow = jnp.concatenate([cp1_re, -cp1_im], axis=1)
    at_re, at_im = pw_re[:, :, t], pw_im[:, :, t]
    a1 = jnp.concatenate([at_re, at_re], axis=-1)[:, None, :]
    a2 = jnp.concatenate([-at_im, at_im], axis=-1)[:, None, :]
    return kc, bend, cpow, a1, a2


def _swap_halves(x, axis):
    n = x.shape[axis] // 2
    lo = lax.slice_in_dim(x, 0, n, axis=axis)
    hi = lax.slice_in_dim(x, n, 2 * n, axis=axis)
    return jnp.concatenate([hi, lo], axis=axis)


def _group_spec(shape):
    return pl.BlockSpec((1,) + tuple(shape[1:]), lambda g: (g, 0, 0))


S5_ROWS = 8


def _s5_expand_toeplitz(kc_ref, ext, toep):
    t, th = S5_T, S5_T * S5_H
    ext[:, th:] = jnp.zeros((S5_ROWS, LANES), F32)
    for hin in range(S5_H):
        ext[:, :th] = jnp.broadcast_to(kc_ref[0, hin:hin + 1, :], (S5_ROWS, th))
        rolled = pltpu.roll(ext[...], 0, 1, stride=S5_H, stride_axis=0)
        tiles = []
        for q in range(t // S5_ROWS):
            if q == 0:
                tiles.append(rolled[:, :th])
            else:
                tiles.append(jnp.concatenate([jnp.zeros((S5_ROWS, q * LANES), F32), rolled[:, :th - q * LANES]],
                                             axis=1))
        toep[hin * t:(hin + 1) * t, :] = jnp.concatenate(tiles, axis=0).astype(BF16)


def _s5_core_fwd(name, u, ops):
    kc, bend, cpow, a1, a2 = ops
    g, nc, th = u.shape
    p2 = bend.shape[-1]
    bend_b, cpow_b = bend.astype(BF16), cpow.astype(BF16)
    a2s = _swap_halves(a2, 2)

    def kern(u_ref, k_ref, b_ref, c_ref, a1_ref, a2_ref, a2s_ref, y_ref, sp_ref, x_scr, xs_scr, ext, toep):
        _s5_expand_toeplitz(k_ref, ext, toep)
        ub = u_ref[0].astype(BF16)
        xv = _dot(ub, b_ref[0])
        x_scr[...] = xv
        xs_scr[...] = pltpu.roll(xv, p2 // 2, 1)
        a1v, a2v, a2sv = a1_ref[0], a2_ref[0], a2s_ref[0]

        def step(c, carry):
            s, ss = carry
            sp_ref[0, pl.ds(c, 1), :] = s
            s_new = a1v * s + a2v * ss + x_scr[pl.ds(c, 1), :]
            ss_new = a1v * ss + a2sv * s + xs_scr[pl.ds(c, 1), :]
            return s_new, ss_new

        zero = jnp.zeros((1, p2), F32)
        lax.fori_loop(0, nc, step, (zero, zero))
        y_ref[0] = (_dot(ub, toep[...]) + _dot(sp_ref[0].astype(BF16), c_ref[0])).astype(BF16)

    ins = [u, kc, bend_b, cpow_b, a1, a2, a2s]
    return pl.pallas_call(
        kern, name=name, grid=(g,), in_specs=[_group_spec(a.shape) for a in ins],
        out_specs=[_group_spec((g, nc, th)), _group_spec((g, nc, p2))],
        out_shape=[jax.ShapeDtypeStruct((g, nc, th), BF16), jax.ShapeDtypeStruct((g, nc, p2), F32)],
        scratch_shapes=[pltpu.VMEM((nc, p2), F32), pltpu.VMEM((nc, p2), F32),
                        pltpu.VMEM((S5_ROWS, th + LANES), F32), pltpu.VMEM((th, th), BF16)],
        compiler_params=_params("arbitrary"),
    )(*ins)


def _s5_core_bwd(name, u, dy, sprev, ops):
    kc, bend, cpow, a1, a2 = ops
    g, nc, th = u.shape
    t = S5_T
    p2 = bend.shape[-1]
    bend_b, cpow_b = bend.astype(BF16), cpow.astype(BF16)
    a2s = _swap_halves(a2, 2)
    idx = jnp.arange(th)
    flip = (idx[:, None] // t == idx[None, :] // t) & (idx[:, None] % t == t - 1 - idx[None, :] % t)
    flip = flip.astype(BF16)

    def kern(u_ref, dy_ref, sp_ref, k_ref, b_ref, c_ref, a1_ref, a2_ref, a2s_ref, f_ref,
             du_ref, dk_ref, db_ref, dc_ref, da1_ref, da2_ref, g_scr, gs_scr, dx_scr, ext, toep, dtoep):
        _s5_expand_toeplitz(k_ref, ext, toep)
        ub, dyb = u_ref[0].astype(BF16), dy_ref[0].astype(BF16)
        dtoep[...] = _dot(_dot(ub, f_ref[...]).astype(BF16), dyb, "tn")
        n_q = t // S5_ROWS
        width = th + LANES
        for hin in range(S5_H):
            folded = dtoep[hin * t + (n_q - 1) * S5_ROWS:(hin + 1) * t, :]
            for qp in range(n_q - 1):
                q = n_q - 1 - qp
                tile = dtoep[hin * t + qp * S5_ROWS:hin * t + (qp + 1) * S5_ROWS, :]
                folded = folded + jnp.concatenate([tile[:, q * LANES:], jnp.zeros((S5_ROWS, q * LANES), F32)],
                                                  axis=1)
            ext[:, :th] = folded
            rolled = pltpu.roll(ext[...], 0, 1, stride=S5_H, stride_axis=0)
            rolled = pltpu.roll(rolled, width - S5_H * (S5_ROWS - 1), 1)
            dk_ref[0, hin:hin + 1, :] = _colsum(rolled)[:, :th]
        spv = sp_ref[0]
        dc_ref[0] = _dot(spv.astype(BF16), dyb, "tn")
        gv = _dot(dyb, c_ref[0], "nt")
        g_scr[...] = gv
        gs_scr[...] = pltpu.roll(gv, p2 // 2, 1)
        a1v, a2v, a2sv = a1_ref[0], a2_ref[0], a2s_ref[0]

        def step(k, carry):
            gr, grs, da1, da2 = carry
            c = nc - 1 - k
            dx_scr[pl.ds(c, 1), :] = gr
            s_in = sp_ref[0, pl.ds(c, 1), :]
            da1 = da1 + gr * s_in
            da2 = da2 + grs * s_in
            gr_new = g_scr[pl.ds(c, 1), :] + a1v * gr + a2sv * grs
            grs_new = gs_scr[pl.ds(c, 1), :] + a1v * grs + a2v * gr
            return gr_new, grs_new, da1, da2

        zero = jnp.zeros((1, p2), F32)
        _, _, da1, da2 = lax.fori_loop(0, nc, step, (zero, zero, zero, zero))
        da1_ref[0] = da1
        da2_ref[0] = da2
        dxb = dx_scr[...].astype(BF16)
        db_ref[0] = _dot(ub, dxb, "tn")
        du_ref[0] = (_dot(dyb, toep[...], "nt") + _dot(dxb, b_ref[0], "nt")).astype(BF16)

    ins = [u, dy, sprev, kc, bend_b, cpow_b, a1, a2, a2s]
    outs = [(g, nc, th), (g, S5_H, th), (g, th, p2), (g, p2, th), (g, 1, p2), (g, 1, p2)]
    out_types = [BF16] + [F32] * (len(outs) - 1)
    return pl.pallas_call(
        kern, name=name, grid=(g,),
        in_specs=[_group_spec(a.shape) for a in ins] + [pl.BlockSpec((th, th), lambda gi: (0, 0))],
        out_specs=[_group_spec(s) for s in outs],
        out_shape=[jax.ShapeDtypeStruct(s, dt) for s, dt in zip(outs, out_types)],
        scratch_shapes=[pltpu.VMEM((nc, p2), F32), pltpu.VMEM((nc, p2), F32), pltpu.VMEM((nc, p2), F32),
                        pltpu.VMEM((S5_ROWS, th + LANES), F32), pltpu.VMEM((th, th), BF16),
                        pltpu.VMEM((th, th), F32)],
        compiler_params=_params("arbitrary"),
    )(*ins, flip)


def _s5_to_groups(u, channel_major):
    rows, w = u.shape
    g = w // S5_H
    nc = rows // S5_T
    perm = (2, 0, 3, 1) if channel_major else (2, 0, 1, 3)
    return u.reshape(nc, S5_T, g, S5_H).transpose(perm).reshape(g, nc, S5_T * S5_H)


def _s5_from_groups(y, channel_major):
    g, nc, _ = y.shape
    if channel_major:
        return y.reshape(g, nc, S5_H, S5_T).transpose(1, 3, 0, 2).reshape(nc * S5_T, g * S5_H)
    return y.reshape(g, nc, S5_T, S5_H).transpose(1, 2, 0, 3).reshape(nc * S5_T, g * S5_H)


def _softplus(x):
    return jnp.maximum(x, 0.0) + jnp.log(1.0 + jnp.exp(-jnp.abs(x)))


def _ssd_chunk_prep(dtraw_ref, dtb_ref, a_ref, expand_ref, cst, dtx, lastt, n_heads):
    q = M2_Q
    lane = lax.broadcasted_iota(jnp.int32, (q, LANES), 1)
    dt = jnp.where(lane < n_heads, _softplus(dtraw_ref[...] + dtb_ref[...]), 0.0)
    adt = dt * a_ref[...]
    row = lax.broadcasted_iota(jnp.int32, (q, q), 0)
    col = lax.broadcasted_iota(jnp.int32, (q, q), 1)
    cs = _dot(jnp.where(row >= col, 1.0, 0.0), adt, precision=HIGHEST)
    cst[...] = cs.T
    dtx[...] = _bdot(dt, expand_ref[...])
    lastt[...] = jnp.broadcast_to(_colsum(adt), (q, LANES)).T
    return dt


def _pair_tables(cst, lastt, p):
    q = M2_Q
    out = []
    for hh in (2 * p, 2 * p + 1):
        rc = jnp.broadcast_to(cst[hh:hh + 1, :], (q, q))
        cc = rc.T
        lb = jnp.broadcast_to(lastt[hh:hh + 1, :], (q, q))
        out.append((rc, cc, lb))
    return out


def _ssd_pair_fwd(x, bm, cm, cb, hs, tabs, dtp):
    q = M2_Q
    row = lax.broadcasted_iota(jnp.int32, (q, q), 0)
    col = lax.broadcasted_iota(jnp.int32, (q, q), 1)
    causal = row >= col
    lo = col < M2_P
    slo = row < M2_P
    (rc0, cc0, lb0), (rc1, cc1, lb1) = tabs
    l0 = jnp.where(causal, jnp.exp(cc0 - rc0), 0.0)
    l1 = jnp.where(causal, jnp.exp(cc1 - rc1), 0.0)
    m0, m1 = cb * l0, cb * l1
    xdt = x * dtp
    xdt0 = jnp.where(lo, xdt, 0.0)
    xdt1 = jnp.where(lo, 0.0, xdt)
    e = jnp.where(lo, jnp.exp(cc0), jnp.exp(cc1))
    z = _bdot(cm, hs, "nt")
    yoff = z * e
    dec = jnp.where(lo, jnp.exp(lb0 - cc0), jnp.exp(lb1 - cc1))
    xdd = xdt * dec
    cd = jnp.where(slo, jnp.exp(lb0), jnp.exp(lb1))
    return dict(l0=l0, l1=l1, m0=m0, m1=m1, dtp=dtp, xdt=xdt, xdt0=xdt0, xdt1=xdt1, e=e, yoff=yoff,
                dec=dec, xdd=xdd, cd=cd, lo=lo, slo=slo)


def _ssd_fwd(name, xbc, dtraw, dtb, arow, dvec, expand, n_heads):
    rows, c = xbc.shape
    q, n = M2_Q, M2_N
    di = n_heads * M2_P
    n_pairs = n_heads // 2
    ppg = n_pairs // M2_G
    nc = rows // q

    def kern(xbc_ref, dtraw_ref, dtb_ref, a_ref, d_ref, e_ref, y_ref, prev_ref, state, cst, dtx, lastt):
        @pl.when(pl.program_id(0) == 0)
        def _():
            state[...] = jnp.zeros_like(state)

        _ssd_chunk_prep(dtraw_ref, dtb_ref, a_ref, e_ref, cst, dtx, lastt, n_heads)
        for p in range(n_pairs):
            gi = p // ppg
            sl = slice(p * LANES, (p + 1) * LANES)
            x = xbc_ref[:, sl]
            bm = xbc_ref[:, di + gi * n:di + (gi + 1) * n]
            cm = xbc_ref[:, di + (M2_G + gi) * n:di + (M2_G + gi + 1) * n]
            if p % ppg == 0:
                cb = _bdot(cm, bm, "nt")
            hs = state[p]
            f = _ssd_pair_fwd(x, bm, cm, cb, hs, _pair_tables(cst, lastt, p), dtx[:, sl])
            ydiag = _bdot(f["m0"], f["xdt0"]) + _bdot(f["m1"], f["xdt1"])
            y_ref[:, sl] = ydiag + f["yoff"] + d_ref[:, sl] * x
            prev_ref[0, p] = hs
            state[p] = f["cd"] * hs + _bdot(f["xdd"], bm, "tn")

    def whole(a):
        return pl.BlockSpec(a.shape, lambda i: (0, 0))

    return pl.pallas_call(
        kern, name=name, grid=(nc,),
        in_specs=[pl.BlockSpec((q, c), lambda i: (i, 0)), pl.BlockSpec((q, LANES), lambda i: (i, 0)),
                  whole(dtb), whole(arow), whole(dvec), whole(expand)],
        out_specs=[pl.BlockSpec((q, di), lambda i: (i, 0)),
                   pl.BlockSpec((1, n_pairs, 2 * M2_P, n), lambda i: (i, 0, 0, 0))],
        out_shape=[jax.ShapeDtypeStruct((rows, di), F32),
                   jax.ShapeDtypeStruct((nc, n_pairs, 2 * M2_P, n), F32)],
        scratch_shapes=[pltpu.VMEM((n_pairs, 2 * M2_P, n), F32), pltpu.VMEM((LANES, q), F32),
                        pltpu.VMEM((q, di), F32), pltpu.VMEM((LANES, q), F32)],
        compiler_params=_params("arbitrary"),
    )(xbc, dtraw, dtb, arow, dvec, expand)


def _ssd_bwd(name, xbc, dtraw, dy, prev, dtb, arow, dvec, seg, expand, n_heads):
    rows, c = xbc.shape
    q, n = M2_Q, M2_N
    di = n_heads * M2_P
    n_pairs = n_heads // 2
    ppg = n_pairs // M2_G
    nc = rows // q

    def kern(xbc_ref, dtraw_ref, dy_ref, prev_ref, dtb_ref, a_ref, d_ref, seg_ref, e_ref,
             dxbc_ref, ddt_ref, da_ref, ddtb_ref, dd_ref,
             dstate, cst, dtx, lastt, dcst, wx, colterm, ddfull):
        step = pl.program_id(0)

        @pl.when(step == 0)
        def _():
            dstate[...] = jnp.zeros_like(dstate)
            ddfull[...] = jnp.zeros_like(ddfull)
            da_ref[...] = jnp.zeros_like(da_ref)
            ddtb_ref[...] = jnp.zeros_like(ddtb_ref)
            dd_ref[...] = jnp.zeros_like(dd_ref)

        dt = _ssd_chunk_prep(dtraw_ref, dtb_ref, a_ref, e_ref, cst, dtx, lastt, n_heads)
        dcst[...] = jnp.zeros_like(dcst)
        lane_q = lax.broadcasted_iota(jnp.int32, (1, q), 1)
        last_hot = jnp.where(lane_q == q - 1, 1.0, 0.0)

        def total(v):
            return jnp.sum(jnp.sum(v, axis=1, keepdims=True), axis=0, keepdims=True)

        for gi in range(M2_G):
            bm = xbc_ref[:, di + gi * n:di + (gi + 1) * n]
            cm = xbc_ref[:, di + (M2_G + gi) * n:di + (M2_G + gi + 1) * n]
            cb = _bdot(cm, bm, "nt")
            dcb = jnp.zeros((q, q), F32)
            dbm = jnp.zeros((q, n), F32)
            dcm = jnp.zeros((q, n), F32)
            for p in range(gi * ppg, (gi + 1) * ppg):
                sl = slice(p * LANES, (p + 1) * LANES)
                x = xbc_ref[:, sl]
                dyp = dy_ref[:, sl]
                hs = prev_ref[0, p]
                ds = dstate[p]
                f = _ssd_pair_fwd(x, bm, cm, cb, hs, _pair_tables(cst, lastt, p), dtx[:, sl])
                lo, slo = f["lo"], f["slo"]
                ddfull[:, sl] += _colsum(dyp * x)
                dy0 = jnp.where(lo, dyp, 0.0)
                dy1 = jnp.where(lo, 0.0, dyp)
                dm0 = _bdot(dyp, f["xdt0"], "nt")
                dm1 = _bdot(dyp, f["xdt1"], "nt")
                dxdt = _bdot(f["m0"], dy0, "tn") + _bdot(f["m1"], dy1, "tn")
                dcb = dcb + dm0 * f["l0"] + dm1 * f["l1"]
                w0, w1 = dm0 * f["m0"], dm1 * f["m1"]
                dz = dyp * f["e"]
                dcm = dcm + _bdot(dz, hs)
                dhs = _bdot(dz, cm, "tn") + f["cd"] * ds
                tot = ds * hs * f["cd"]
                dxdd = _bdot(bm, ds, "nt")
                dbm = dbm + _bdot(f["xdd"], ds)
                ee = dxdd * f["xdd"]
                colterm[:, sl] = dyp * f["yoff"] - ee
                dxdt = dxdt + dxdd * f["dec"]
                t_all = total(tot)
                t_lo = total(jnp.where(slo, tot, 0.0))
                e_all = total(ee)
                e_lo = total(jnp.where(lo, ee, 0.0))
                dlast0 = t_lo + e_lo
                dlast1 = (t_all - t_lo) + (e_all - e_lo)
                dcst[2 * p:2 * p + 1, :] = _colsum(w0.T - w0) + dlast0 * last_hot
                dcst[2 * p + 1:2 * p + 2, :] = _colsum(w1.T - w1) + dlast1 * last_hot
                dxbc_ref[:, sl] = d_ref[:, sl] * dyp + dxdt * f["dtp"]
                wx[:, sl] = dxdt * x
                dstate[p] = dhs
            dcm = dcm + _bdot(dcb, bm)
            dbm = dbm + _bdot(dcb, cm, "tn")
            dxbc_ref[:, di + gi * n:di + (gi + 1) * n] = dbm
            dxbc_ref[:, di + (M2_G + gi) * n:di + (M2_G + gi + 1) * n] = dcm

        segv = seg_ref[...]
        dcs = _dot(colterm[...], segv, precision=HIGHEST) + dcst[...].T
        row = lax.broadcasted_iota(jnp.int32, (q, q), 0)
        col = lax.broadcasted_iota(jnp.int32, (q, q), 1)
        ddelta = _dot(jnp.where(col >= row, 1.0, 0.0), dcs, precision=HIGHEST)
        ddt = _dot(wx[...], segv, precision=HIGHEST) + ddelta * a_ref[...]
        da_ref[...] += _colsum(ddelta * dt)
        lane = lax.broadcasted_iota(jnp.int32, (q, LANES), 1)
        ddtraw = jnp.where(lane < n_heads, ddt * _sigmoid(dtraw_ref[...] + dtb_ref[...]), 0.0)
        ddt_ref[...] = ddtraw
        ddtb_ref[...] += _colsum(ddtraw)

        @pl.when(step == nc - 1)
        def _():
            dd_ref[...] = _dot(jnp.broadcast_to(ddfull[...], (8, di)), segv, precision=HIGHEST)

    def whole(a):
        return pl.BlockSpec(a.shape, lambda i: (0, 0))

    def rev(i):
        return nc - 1 - i

    acc = jax.ShapeDtypeStruct((1, LANES), F32)
    acc_spec = pl.BlockSpec((1, LANES), lambda i: (0, 0))
    acc8 = jax.ShapeDtypeStruct((8, LANES), F32)
    acc8_spec = pl.BlockSpec((8, LANES), lambda i: (0, 0))
    return pl.pallas_call(
        kern, name=name, grid=(nc,),
        in_specs=[pl.BlockSpec((q, c), lambda i: (rev(i), 0)), pl.BlockSpec((q, LANES), lambda i: (rev(i), 0)),
                  pl.BlockSpec((q, di), lambda i: (rev(i), 0)),
                  pl.BlockSpec((1, n_pairs, 2 * M2_P, n), lambda i: (rev(i), 0, 0, 0)),
                  whole(dtb), whole(arow), whole(dvec), whole(seg), whole(expand)],
        out_specs=[pl.BlockSpec((q, c), lambda i: (rev(i), 0)), pl.BlockSpec((q, LANES), lambda i: (rev(i), 0)),
                   acc_spec, acc_spec, acc8_spec],
        out_shape=[jax.ShapeDtypeStruct((rows, c), F32), jax.ShapeDtypeStruct((rows, LANES), F32), acc, acc, acc8],
        scratch_shapes=[pltpu.VMEM((n_pairs, 2 * M2_P, n), F32), pltpu.VMEM((LANES, q), F32),
                        pltpu.VMEM((q, di), F32), pltpu.VMEM((LANES, q), F32), pltpu.VMEM((LANES, q), F32),
                        pltpu.VMEM((q, di), F32), pltpu.VMEM((q, di), F32), pltpu.VMEM((1, di), F32)],
        compiler_params=_params("arbitrary"),
    )(xbc, dtraw, dy, prev, dtb, arow, dvec, seg, expand)


S5_PARAM_NAMES = ("s5_lambda_re", "s5_lambda_im", "s5_log_dt", "s5_b_re", "s5_b_im", "s5_c_re", "s5_c_im")


def _row(v):
    return v.reshape(1, -1)


class _Rides:
    def __init__(self):
        self.pending = {}
        self.landed = {}

    def matmul(self, site, name, *args, **kw):
        ride = self.pending.pop(site, None)
        res = _matmul(name, *args, ride=ride, **kw)
        if ride is None:
            return res
        self.landed[site] = list(res[-len(ride):])
        res = list(res[:-len(ride)])
        return res[0] if len(res) == 1 else res


def _s5_layer_fwd(tag, x, gate, h, w, j, rides):
    u = rides.matmul("s5_win", tag + "_win", h, w["s5_w_in"][j])
    params = [w[k][j] for k in S5_PARAM_NAMES]
    ops, build_vjp = jax.vjp(_s5_build, *params)
    ug = _s5_to_groups(u.astype(BF16), True)
    yg, sprev = _s5_core_fwd(tag + "_core", ug, ops)
    yy = _s5_from_groups(yg, False)
    skip = _row(w["s5_d"][j])
    gl = _gelu_fwd(tag + "_gelu", yy, u, skip)
    ab = rides.matmul("s5_wglu", tag + "_wglu", gl, w["s5_w_glu"][j])
    x1 = _glu_fwd(tag + "_glu", ab, _row(w["s5_b_glu"][j]), x, gate)
    return x1, dict(u=u, ug=ug, ops=ops, build_vjp=build_vjp, sprev=sprev, yy=yy, gl=gl, ab=ab, skip=skip)


def _s5_layer_bwd(tag, dx1, gate, h, sv, w, j, rides):
    dab, db_glu, dgate = _glu_bwd(tag + "_glu_b", sv["ab"], _row(w["s5_b_glu"][j]), dx1, gate)
    dw_glu = rides.matmul("s5_dwglu", tag + "_dwglu", sv["gl"], dab, "tn", out_dtype=BF16)
    dgl = rides.matmul("s5_dgl", tag + "_dgl", dab, w["s5_w_glu"][j], "nt")
    dyy, dskip = _gelu_bwd(tag + "_gelu_b", sv["yy"], sv["u"], sv["skip"], dgl)
    dug, dkc, dbend, dcpow, da1, da2s = _s5_core_bwd(
        tag + "_core_b", sv["ug"], _s5_to_groups(dyy.astype(BF16), False), sv["sprev"], sv["ops"])
    dparams = sv["build_vjp"]((dkc, dbend, dcpow, da1, _swap_halves(da2s, 2)))
    du = _axpy(tag + "_du", _s5_from_groups(dug, True), dyy, sv["skip"])
    grads = dict(zip(S5_PARAM_NAMES, dparams))
    grads["s5_d"] = dskip.reshape(-1)
    grads["s5_w_in"] = _matmul(tag + "_dwin", h, du, "tn", out_dtype=BF16)
    grads["s5_w_glu"] = dw_glu
    grads["s5_b_glu"] = db_glu.reshape(-1)
    dh = _matmul(tag + "_dh", du, w["s5_w_in"][j], "nt")
    return dh, grads, dgate


def _ssd_consts(w, j, d_model):
    di = 2 * d_model
    heads = di // M2_P

    def pad_row(v):
        return jnp.zeros((1, LANES), F32).at[0, :heads].set(v)

    a = -jnp.exp(w["m2_a_log"][j])
    seg = (jnp.arange(di)[:, None] // M2_P == jnp.arange(LANES)[None, :]).astype(F32)
    w_in = w["m2_w_in"][j]
    conv_dim = di + 2 * M2_G * M2_N
    w_dt = jnp.zeros((d_model, LANES), w_in.dtype).at[:, :heads].set(w_in[:, di + conv_dim:])
    return dict(di=di, heads=heads, conv_dim=conv_dim, a=a, arow=pad_row(a), dtb=pad_row(w["m2_dt_bias"][j]),
                dvec=_row(jnp.repeat(w["m2_d"][j], M2_P)), seg=seg, expand=seg.T.astype(BF16),
                w_z=w_in[:, :di], w_xbc=w_in[:, di:di + conv_dim], w_dt=w_dt,
                conv_w=w["m2_conv_w"][j], conv_b=_row(w["m2_conv_b"][j]), norm_g=_row(w["m2_norm_g"][j]))


def _gated_out_bwd(tag, act, dxo, w_out, gate, **kw):
    dw, dgate_parts = _matmul(tag + "_dwo", act, dxo, "tn", out_dtype=BF16, colscale=gate, colsum_with=w_out, **kw)
    dgate = jnp.sum(dgate_parts, axis=0)
    return dw, dgate, _scale_cols(tag + "_wog", w_out, gate)


def _ssd_layer_fwd(tag, x, gate, h, w, j):
    k = _ssd_consts(w, j, h.shape[1])
    z = _matmul(tag + "_wz", h, k["w_z"])
    xbc_pre = _matmul(tag + "_wxbc", h, k["w_xbc"])
    dtraw = _matmul(tag + "_wdt", h, k["w_dt"])
    xbc = _conv_fwd(tag + "_conv", xbc_pre, k["conv_w"], k["conv_b"])
    y, prev = _ssd_fwd(tag + "_core", xbc, dtraw, k["dtb"], k["arow"], k["dvec"], k["expand"], k["heads"])
    yn = _gatenorm_fwd(tag + "_gn", y, z, k["norm_g"])
    x1 = _matmul(tag + "_wout", yn, w["m2_w_out"][j], colscale=gate, addin=x)
    return x1, dict(k=k, z=z, xbc_pre=xbc_pre, dtraw=dtraw, xbc=xbc, y=y, prev=prev, yn=yn)


def _ssd_layer_bwd(tag, dx1_b, gate, h, sv, w, j):
    k = sv["k"]
    heads = k["heads"]
    dw_out, dgate, wog = _gated_out_bwd(tag, sv["yn"], dx1_b, w["m2_w_out"][j], gate)
    grads = {"m2_w_out": dw_out}
    dyn = _matmul(tag + "_dyn", dx1_b, wog, "nt")
    dyssd, dz, dng = _gatenorm_bwd(tag + "_gn_b", sv["y"], sv["z"], k["norm_g"], dyn)
    dxbc, ddtraw, da, ddtb, dd = _ssd_bwd(tag + "_core_b", sv["xbc"], sv["dtraw"], dyssd, sv["prev"],
                                          k["dtb"], k["arow"], k["dvec"], k["seg"], k["expand"], heads)
    dpre, dcw, dcb = _conv_bwd_pre(tag + "_conv_b1", sv["xbc_pre"], dxbc, k["conv_w"], k["conv_b"])
    dxbc_pre = _conv_bwd_in(tag + "_conv_b2", dpre, k["conv_w"])
    dw_z = _matmul(tag + "_dwz", h, dz, "tn", out_dtype=BF16)
    dw_xbc = _matmul(tag + "_dwxbc", h, dxbc_pre, "tn", out_dtype=BF16)
    dw_dt = _matmul(tag + "_dwdt", h, ddtraw, "tn", out_dtype=BF16)
    dh = _matmul(tag + "_dh1", dz, k["w_z"], "nt")
    dh = _matmul(tag + "_dh2", dxbc_pre, k["w_xbc"], "nt", addin=dh)
    dh = _matmul(tag + "_dh3", ddtraw, k["w_dt"], "nt", addin=dh)
    grads["m2_w_in"] = jnp.concatenate([dw_z, dw_xbc, dw_dt[:, :heads]], axis=1)
    grads["m2_conv_w"] = dcw
    grads["m2_conv_b"] = dcb.reshape(-1)
    grads["m2_dt_bias"] = ddtb[0, :heads]
    grads["m2_a_log"] = da[0, :heads] * k["a"]
    grads["m2_d"] = dd[0, :heads]
    grads["m2_norm_g"] = dng.reshape(-1)
    return dh, grads, dgate


def _layer_fwd(li, x, mod, w, rides, late_weights=None):
    tag = "L%d" % li
    sh1, sc1, g1, sh2, sc2, g2 = mod
    j = li // 2
    h = _normmod_fwd(tag + "_nm1", x, _row(w["norm_mix_g"][li]), sh1, sc1)
    if li % 2 == 0:
        x1, mix = _s5_layer_fwd(tag + "_s5", x, g1, h, w, j, rides)
    else:
        x1, mix = _ssd_layer_fwd(tag + "_m2", x, g1, h, w, j)
    if late_weights is not None:
        w = {**w, **late_weights(rides.landed)}
    h2 = _normmod_fwd(tag + "_nm2", x1, _row(w["norm_mlp_g"][li]), sh2, sc2)
    r = rides.matmul("w1", tag + "_w1", h2, w["mlp_w1"][li], relu=True, out_dtype=BF16)
    x2 = rides.matmul("w2", tag + "_w2", r, w["mlp_w2"][li], square_a=True, colscale=g2, addin=x1)
    return x2, dict(x=x, h=h, mix=mix, x1=x1, h2=h2, r=r), w


def _layer_bwd(li, dx2, dx2_b, sv, mod, w, rides, ride_own_mlp):
    tag = "L%d" % li
    sh1, sc1, g1, sh2, sc2, g2 = mod
    j = li // 2
    dw2, dg2, w2g = _gated_out_bwd(tag + "_mlp", sv["r"], dx2_b, w["mlp_w2"][li], g2, square_a=True)
    dr = rides.matmul("dr", tag + "_dr", dx2_b, w2g, "nt", out_dtype=BF16, mul2=sv["r"])
    dw1 = _matmul(tag + "_dw1", sv["h2"], dr, "tn", out_dtype=BF16)
    grads = {"mlp_w2": dw2, "mlp_w1": dw1}
    dh2 = rides.matmul("dh2", tag + "_dh2", dr, w["mlp_w1"][li], "nt")
    if ride_own_mlp:
        mlp_bufs = _grad_buffers(li, grads, parts=[1])[0]
        rides.pending["s5_dgl"], rides.pending["s5_dwglu"] = mlp_bufs[:1], mlp_bufs[1:]
    dx1, dx1_b, dgm, dsh2, dsc2 = _normmod_bwd(tag + "_nm2_b", sv["x1"], _row(w["norm_mlp_g"][li]), sh2, sc2, dh2,
                                               dx2)
    if li % 2 == 0:
        dh, mix_grads, dg1 = _s5_layer_bwd(tag + "_s5", dx1, g1, sv["h"], sv["mix"], w, j, rides)
    else:
        dh, mix_grads, dg1 = _ssd_layer_bwd(tag + "_m2", dx1_b, g1, sv["h"], sv["mix"], w, j)
    dx, dx_b, dgx, dsh1, dsc1 = _normmod_bwd(tag + "_nm1_b", sv["x"], _row(w["norm_mix_g"][li]), sh1, sc1, dh, dx1)
    grads["norm_mix_g"] = dgx.reshape(-1)
    grads["norm_mlp_g"] = dgm.reshape(-1)
    dmod = jnp.concatenate([dsh1, dsc1, dg1, dsh2, dsc2, dg2], axis=1)
    return dx, dx_b, {**grads, **mix_grads}, dmod


def _layer_parts(li):
    j = li // 2
    mix = [("s5_w_in", j, 0), ("s5_w_glu", j, 1)] if li % 2 == 0 else [("m2_w_in", j, 1), ("m2_w_out", j, 0)]
    return [mix, [("mlp_w1", li, 1), ("mlp_w2", li, 0)]]


def _grad_specs(li):
    j = li // 2
    mlp = [("one", ("mlp_w2", li, 0)), ("one", ("mlp_w1", li, 1))]
    if li % 2 == 0:
        return [[("one", ("s5_w_in", j, 0)), ("one", ("s5_w_glu", j, 1))], mlp]
    packed = [("m2_w_in", j, 1), ("m2_conv_w", j, 1), ("m2_conv_b", j, 0), ("m2_norm_g", j, 0)]
    return [[("one", ("m2_w_out", j, 0)), ("packed", packed)], mlp]


def _grad_buffers(li, grads, parts=(0, 1)):
    out = []
    for k in parts:
        bufs = []
        for kind, entry in _grad_specs(li)[k]:
            if kind == "one":
                bufs.append((grads[entry[0]], "rows" if entry[2] == 0 else "cols"))
            else:
                pieces = [_pack([_chip_slice(grads[n], c, ax) for n, _, ax in entry], BF16) for c in range(N_CHIP)]
                bufs.append((jnp.stack(pieces), "packed"))
        out.append(bufs)
    return out


def _gather_buffers(local, part):
    return [(local[n][i].astype(BF16), "same") for n, i, _ in part]


def _assemble(landed, part):
    out = {}
    for buf, (n, i, ax) in zip(landed, part):
        if ax == 0:
            out[n] = {i: buf.reshape(-1, buf.shape[2])}
        else:
            out[n] = {i: jnp.concatenate([buf[k] for k in range(N_CHIP)], axis=1)}
    return out


def _local_step(x, target, mods, w, local):
    depth = w["norm_mix_g"].shape[0]
    d = x.shape[1]
    rides = _Rides()
    saved, mod_rows, layer_w = [], [], []
    mix0, mlp0 = _layer_parts(0)
    wl = {**w, **_assemble(_exchange4("ag_w_L0", _gather_buffers(local, mix0)), mix0)}
    rides.pending["s5_win"] = _gather_buffers(local, mlp0[:1])
    rides.pending["s5_wglu"] = _gather_buffers(local, mlp0[1:])

    def late_mlp0(landed):
        return _assemble(landed.pop("s5_win") + landed.pop("s5_wglu"), mlp0)

    for li in range(depth):
        if li + 1 < depth:
            nxt = _layer_parts(li + 1)
            rides.pending["w1"] = _gather_buffers(local, nxt[0])
            rides.pending["w2"] = _gather_buffers(local, nxt[1])
        mod = [mods[li:li + 1, i * d:(i + 1) * d] for i in range(N_MOD)]
        mod_rows.append(mod)
        x, sv, wl = _layer_fwd(li, x, mod, wl, rides, late_mlp0 if li == 0 else None)
        saved.append(sv)
        layer_w.append(wl)
        if li + 1 < depth:
            wl = {**w, **_assemble(rides.landed.pop("w1"), nxt[0]), **_assemble(rides.landed.pop("w2"), nxt[1])}
    dx, dx_b, dgf, loss = _loss_head("loss_head", x, target, _row(w["final_norm_g"]))
    layer_grads = [None] * depth
    dmods = [None] * depth
    landed = {}
    for li in reversed(range(depth)):
        dx, dx_b, layer_grads[li], dmods[li] = _layer_bwd(li, dx, dx_b, saved[li], mod_rows[li], layer_w[li], rides,
                                                          li == 0)
        if li + 1 < depth:
            landed[(li + 1, 0)] = rides.landed.pop("dr")
            landed[(li + 1, 1)] = rides.landed.pop("dh2")
        if li > 0:
            rides.pending["dr"], rides.pending["dh2"] = _grad_buffers(li, layer_grads[li])
    landed[(0, 1)] = rides.landed.pop("s5_dgl") + rides.landed.pop("s5_dwglu")
    landed[(0, 0)] = _exchange4("rs_g_L0", _grad_buffers(0, layer_grads[0], parts=[0])[0])
    return loss, dx, layer_grads, dgf.reshape(-1), jnp.concatenate(dmods, axis=0), landed


ANY = pl.BlockSpec(memory_space=pl.ANY)
N_DEV = 8
N_CHIP = 4


def _coords():
    return lax.axis_index("x"), lax.axis_index("y"), lax.axis_index("c")


def _allgather8(name, block):
    r, wd = block.shape

    def body(x_ref, out_ref, send_sems, recv_sems, local_sem):
        x, y, c = _coords()
        me, sibling = (x, y, c), (x, y, 1 - c)
        chips = [(1 - x, y), (x, 1 - y), (1 - x, 1 - y)]

        def slot(px, py, pc):
            return out_ref.at[4 * px + 2 * py + pc]

        def copy(k, blk, to, src=None):
            return pltpu.make_async_remote_copy(
                src_ref=slot(*blk) if src is None else src, dst_ref=slot(*blk),
                send_sem=send_sems.at[k], recv_sem=recv_sems.at[k], device_id=to, device_id_type=MESH)

        mine = pltpu.make_async_copy(x_ref, slot(*me), local_sem)
        mine.start()
        first = [copy(0, me, sibling, src=x_ref)]
        first += [copy(1 + j, me, (*chip, c), src=x_ref) for j, chip in enumerate(chips)]
        for cp in first:
            cp.start()
        passed = [copy(4 + j, (*chip, c), sibling) for j, chip in enumerate(chips)]
        for j, chip in enumerate(chips):
            copy(1 + j, (*chip, c), me).wait_recv()
            passed[j].start()
        copy(0, sibling, me).wait_recv()
        for j, chip in enumerate(chips):
            copy(4 + j, (*chip, 1 - c), me).wait_recv()
        for cp in first + passed:
            cp.wait_send()
        mine.wait()

    return pl.pallas_call(
        body, name=name, in_specs=[ANY], out_specs=ANY,
        out_shape=jax.ShapeDtypeStruct((N_DEV, r, wd), block.dtype),
        scratch_shapes=[pltpu.SemaphoreType.DMA((7,)), pltpu.SemaphoreType.DMA((7,)), pltpu.SemaphoreType.DMA],
    )(block)


def _landing_shape(src, kind):
    if kind == "same":
        return (N_CHIP,) + src.shape
    if kind == "packed":
        return src.shape
    rows, cols = src.shape
    return (N_CHIP, rows // N_CHIP, cols) if kind == "rows" else (N_CHIP, rows, cols // N_CHIP)


def _exchange4_ops(srcs, dsts, send_sems, recv_sems, local_sems, kinds):
    x, y, c = _coords()
    my_chip = 2 * x + y
    chips = [(1 - x, y), (x, 1 - y), (1 - x, 1 - y)]

    def piece(q, k):
        ref, kind = srcs[q], kinds[q]
        if kind == "same":
            return ref
        if kind == "packed":
            return ref.at[k]
        _, rows, cols = dsts[q].shape
        return ref.at[pl.ds(k * rows, rows), :] if kind == "rows" else ref.at[:, pl.ds(k * cols, cols)]

    def copy(q, j, k, slot):
        px, py = chips[j]
        return pltpu.make_async_remote_copy(
            src_ref=piece(q, k), dst_ref=dsts[q].at[slot], send_sem=send_sems.at[3 * q + j],
            recv_sem=recv_sems.at[3 * q + j], device_id=(px, py, c), device_id_type=MESH)

    def mine(q):
        return pltpu.make_async_copy(piece(q, my_chip), dsts[q].at[my_chip], local_sems.at[q])

    def start():
        for q in range(len(srcs)):
            mine(q).start()
            for j, (px, py) in enumerate(chips):
                copy(q, j, 2 * px + py, my_chip).start()

    def wait():
        for q in range(len(srcs)):
            for j, (px, py) in enumerate(chips):
                copy(q, j, my_chip, 2 * px + py).wait_recv()
        for q in range(len(srcs)):
            for j, (px, py) in enumerate(chips):
                copy(q, j, 2 * px + py, my_chip).wait_send()
            mine(q).wait()

    return start, wait


def _exchange_scratch(n):
    return [pltpu.SemaphoreType.DMA((3 * n,)), pltpu.SemaphoreType.DMA((3 * n,)), pltpu.SemaphoreType.DMA((n,))]


def _exchange4(name, buffers):
    n = len(buffers)
    kinds = [kind for _, kind in buffers]

    def body(*refs):
        start, wait = _exchange4_ops(refs[:n], refs[n:2 * n], *refs[2 * n:], kinds)
        start()
        wait()

    return pl.pallas_call(
        body, name=name, in_specs=[ANY] * n, out_specs=[ANY] * n,
        out_shape=[jax.ShapeDtypeStruct(_landing_shape(s, k), s.dtype) for s, k in buffers],
        scratch_shapes=_exchange_scratch(n),
    )(*[s for s, _ in buffers])


def _swap_sibling(name, block):
    def body(x_ref, out_ref, send_sem, recv_sem):
        x, y, c = _coords()
        cp = pltpu.make_async_remote_copy(src_ref=x_ref, dst_ref=out_ref, send_sem=send_sem, recv_sem=recv_sem,
                                          device_id=(x, y, 1 - c), device_id_type=MESH)
        cp.start()
        cp.wait()

    return pl.pallas_call(
        body, name=name, in_specs=[ANY], out_specs=ANY, out_shape=jax.ShapeDtypeStruct(block.shape, block.dtype),
        scratch_shapes=[pltpu.SemaphoreType.DMA, pltpu.SemaphoreType.DMA],
    )(block)


def _sum_slots(name, stacked):
    n, r, wd = stacked.shape
    tile = min(FLAT_ROWS, r)

    def kern(x_ref, o_ref):
        acc = x_ref[0].astype(F32)
        for s in range(1, n):
            acc = acc + x_ref[s].astype(F32)
        o_ref[...] = acc

    return pl.pallas_call(
        kern, name=name, grid=(r // tile,), in_specs=[pl.BlockSpec((n, tile, wd), lambda i: (0, i, 0))],
        out_specs=pl.BlockSpec((tile, wd), lambda i: (i, 0)), out_shape=jax.ShapeDtypeStruct((r, wd), F32),
        compiler_params=_params("parallel"),
    )(stacked)


def _adamw(name, w, m, v, g, g2=None):
    r, wd = w.shape
    grads = [g] if g2 is None else [g, g2]
    c1 = 1.0 - ADAM_B1 ** ADAM_STEP
    c2 = 1.0 - ADAM_B2 ** ADAM_STEP

    def body(i, *refs):
        w_ref, m_ref, v_ref = refs[:3]
        g_refs = refs[3:3 + len(grads)]
        go_ref, d_ref, mo_ref, vo_ref = refs[3 + len(grads):]
        gv = g_refs[0][...]
        if g2 is not None:
            gv = gv + g_refs[1][...]
        mn = ADAM_B1 * m_ref[...] + (1.0 - ADAM_B1) * gv
        vn = ADAM_B2 * v_ref[...] + (1.0 - ADAM_B2) * (gv * gv)
        go_ref[...] = gv
        mo_ref[...] = mn
        vo_ref[...] = vn
        d_ref[...] = -ADAM_LR * ((mn / c1) / (jnp.sqrt(vn / c2) + ADAM_EPS) + ADAM_WD * w_ref[...])

    return _rowcall(name, body, r, FLAT_ROWS, [w, m, v] + grads, [], [(wd, F32)] * 4, [])


FLAT_BLOCK = FLAT_ROWS * FLAT_W


def _pack(arrays, dtype):
    flat = jnp.concatenate([a.reshape(-1).astype(dtype) for a in arrays])
    pad = (-flat.shape[0]) % FLAT_BLOCK
    return jnp.pad(flat, (0, pad)).reshape(-1, FLAT_W)


def _unpack(buf, shapes):
    flat = buf.reshape(-1)
    out, off = [], 0
    for s in shapes:
        n = math.prod(s)
        out.append(flat[off:off + n].reshape(s))
        off += n
    return out


SHARDED_BIG = {"mlp_w1": 2, "mlp_w2": 1, "s5_w_in": 1, "s5_w_glu": 2, "m2_w_in": 2, "m2_w_out": 1}
SHARDED_SMALL = {"m2_conv_w": 2, "m2_conv_b": 1, "m2_norm_g": 1}
REPLICATED = ("ada_b", "norm_mix_g", "norm_mlp_g", "s5_lambda_re", "s5_lambda_im", "s5_log_dt", "s5_b_re",
              "s5_b_im", "s5_c_re", "s5_c_im", "s5_d", "s5_b_glu", "m2_dt_bias", "m2_a_log", "m2_d", "final_norm_g")
WEIGHT_NAMES = ("ada_w", "ada_b", "norm_mix_g", "norm_mlp_g", "mlp_w1", "mlp_w2", "s5_w_in", "s5_lambda_re",
                "s5_lambda_im", "s5_log_dt", "s5_b_re", "s5_b_im", "s5_c_re", "s5_c_im", "s5_d", "s5_w_glu",
                "s5_b_glu", "m2_w_in", "m2_conv_w", "m2_conv_b", "m2_dt_bias", "m2_a_log", "m2_d", "m2_norm_g",
                "m2_w_out", "final_norm_g")


def _gather_weights(name, local, names_axes, dtype):
    names = list(names_axes)
    got = _exchange4(name, [(_pack([local[k] for k in names], dtype), "same")])[0]
    per_chip = [_unpack(got[j], [local[k].shape for k in names]) for j in range(N_CHIP)]
    return {k: jnp.concatenate([per_chip[j][i] for j in range(N_CHIP)], axis=names_axes[k])
            for i, k in enumerate(names)}


def _chip_slice(a, chip, axis):
    size = a.shape[axis] // N_CHIP
    return lax.slice_in_dim(a, chip * size, (chip + 1) * size, axis=axis)


def kernel(x, c, ada_w, ada_b, norm_mix_g, norm_mlp_g, mlp_w1, mlp_w2, s5_w_in, s5_lambda_re, s5_lambda_im, s5_log_dt, s5_b_re, s5_b_im, s5_c_re, s5_c_im, s5_d, s5_w_glu, s5_b_glu, m2_w_in, m2_conv_w, m2_conv_b, m2_dt_bias, m2_a_log, m2_d, m2_norm_g, m2_w_out, final_norm_g, loss_target, m_ada_w, m_ada_b, m_norm_mix_g, m_norm_mlp_g, m_mlp_w1, m_mlp_w2, m_s5_w_in, m_s5_lambda_re, m_s5_lambda_im, m_s5_log_dt, m_s5_b_re, m_s5_b_im, m_s5_c_re, m_s5_c_im, m_s5_d, m_s5_w_glu, m_s5_b_glu, m_m2_w_in, m_m2_conv_w, m_m2_conv_b, m_m2_dt_bias, m_m2_a_log, m_m2_d, m_m2_norm_g, m_m2_w_out, m_final_norm_g, v_ada_w, v_ada_b, v_norm_mix_g, v_norm_mlp_g, v_mlp_w1, v_mlp_w2, v_s5_w_in, v_s5_lambda_re, v_s5_lambda_im, v_s5_log_dt, v_s5_b_re, v_s5_b_im, v_s5_c_re, v_s5_c_im, v_s5_d, v_s5_w_glu, v_s5_b_glu, v_m2_w_in, v_m2_conv_w, v_m2_conv_b, v_m2_dt_bias, v_m2_a_log, v_m2_d, v_m2_norm_g, v_m2_w_out, v_final_norm_g):
    args = locals()
    local = {k: args[k] for k in WEIGHT_NAMES}
    mom_m = {k: args["m_" + k] for k in WEIGHT_NAMES}
    mom_v = {k: args["v_" + k] for k in WEIGHT_NAMES}
    depth, d = norm_mix_g.shape
    xi, yi, ci = _coords()
    my_chip = 2 * xi + yi
    my_dev = 2 * my_chip + ci

    cond = jax.nn.silu(c).reshape(-1, LANES)
    cond_all = _allgather8("ag_cond", cond).reshape(N_DEV, d)
    cond_pad = jnp.zeros((LANES, d), F32).at[:N_DEV].set(cond_all)
    mod_cols = ada_w.shape[2]
    mod_part = jnp.stack([_matmul("ada_%d" % i, cond_pad, ada_w[i])[:N_DEV] for i in range(depth)])
    mod_all = _allgather8("ag_mod", mod_part.reshape(-1, LANES)).reshape(N_CHIP, 2, depth, N_DEV, mod_cols)[:, 0]
    mod_mine = lax.dynamic_index_in_dim(mod_all, my_dev, axis=2, keepdims=False)
    mods = jnp.transpose(mod_mine, (1, 0, 2)).reshape(depth, N_CHIP * mod_cols) + ada_b

    w = {k: local[k] for k in REPLICATED}
    w.update(_gather_weights("ag_w_small", local, SHARDED_SMALL, F32))

    loss_row, dx, layer_grads, g_final, dmods, landed = _local_step(x[0], loss_target[0], mods, w, local)
    grads = {"ada_b": dmods, "final_norm_g": g_final}
    for k in REPLICATED[1:-1]:
        grads[k] = jnp.stack([g[k] for g in layer_grads if k in g])

    rep_shapes = [grads[k].shape for k in REPLICATED]
    rep_all = _allgather8("ag_grep", _pack([grads[k] for k in REPLICATED], F32))
    rep_sum = _sum_slots("sum_grep", rep_all)
    dmods_all = rep_all.reshape(N_DEV, -1)[:, :dmods.size].reshape(N_DEV, depth, N_CHIP * mod_cols)

    dm_mine = lax.dynamic_slice_in_dim(dmods_all, my_chip * mod_cols, mod_cols, axis=2)
    dm_pad = jnp.zeros((LANES, depth, mod_cols), F32).at[:N_DEV].set(dm_mine)
    g_ada_w = jnp.stack([_matmul("dada_%d" % i, cond_pad, dm_pad[:, i], "tn") for i in range(depth)])

    red = {}
    for li in range(depth):
        for k, part in enumerate(_grad_specs(li)):
            for (kind, entry), land in zip(part, landed[(li, k)]):
                if kind == "one":
                    red[entry[:2]] = _sum_slots("sum_%s_%d" % entry[:2], land)
                else:
                    total = _sum_slots("sum_packed_L%d" % li, land)
                    shapes = [local[n][i].shape for n, i, _ in entry]
                    red.update({(n, i): v for (n, i, _), v in zip(entry, _unpack(total, shapes))})

    out_g, out_d, out_m, out_v = {}, {}, {}, {}
    for name in list(SHARDED_BIG) + list(SHARDED_SMALL):
        shape = local[name].shape
        flat2 = (-1, shape[-1])
        g1 = jnp.stack([red[(name, i)] for i in range(shape[0])]).reshape(flat2)
        g2 = _swap_sibling("swap_" + name, g1)
        res = _adamw("adam_" + name, local[name].reshape(flat2), mom_m[name].reshape(flat2),
                     mom_v[name].reshape(flat2), g1, g2)
        for dst, buf in zip((out_g, out_d, out_m, out_v), res):
            dst[name] = buf.reshape(shape)
    res = _adamw("adam_rep", _pack([local[k] for k in REPLICATED], F32), _pack([mom_m[k] for k in REPLICATED], F32),
                 _pack([mom_v[k] for k in REPLICATED], F32), rep_sum)
    for dst, buf in zip((out_g, out_d, out_m, out_v), res):
        dst.update(zip(REPLICATED, _unpack(buf, rep_shapes)))
    flat2 = (-1, mod_cols)
    res = _adamw("adam_ada", ada_w.reshape(flat2), m_ada_w.reshape(flat2), v_ada_w.reshape(flat2),
                 g_ada_w.reshape(flat2))
    for dst, buf in zip((out_g, out_d, out_m, out_v), res):
        dst["ada_w"] = buf.reshape(ada_w.shape)

    loss = lax.psum(loss_row[0, 0], ("x", "y", "c"))
    outs = [loss, dx[None]]
    for dst in (out_g, out_d, out_m, out_v):
        outs += [dst[k] for k in WEIGHT_NAMES]
    return tuple(outs)
```

```python
import math

import jax
import jax.numpy as jnp
from jax import lax
from jax.experimental import pallas as pl
from jax.experimental.pallas import tpu as pltpu

F32 = jnp.float32
BF16 = jnp.bfloat16
HIGHEST = lax.Precision.HIGHEST

NORM_EPS = 1e-5
N_MOD = 6
S5_H, S5_P, S5_T = 16, 64, 64
M2_P, M2_N, M2_G, M2_Q, M2_K = 64, 128, 4, 128, 4
LANES = 128
ADAM_LR, ADAM_B1, ADAM_B2, ADAM_EPS, ADAM_WD, ADAM_STEP = 0.001, 0.9, 0.999, 1e-08, 0.01, 10
VMEM_LIMIT_BYTES = 56 * 1024 * 1024
ROW_TILE = 256
FLAT_W = 1024
FLAT_ROWS = 256
MESH = pl.DeviceIdType.MESH


def _params(*sem):
    return pltpu.CompilerParams(dimension_semantics=sem, vmem_limit_bytes=VMEM_LIMIT_BYTES)


def _dot(a, b, dn="nn", precision=None):
    dims = {"nn": ((1,), (0,)), "nt": ((1,), (1,)), "tn": ((0,), (0,))}[dn]
    return lax.dot_general(a, b, (dims, ((), ())), preferred_element_type=F32, precision=precision)


def _bdot(a, b, dn="nn"):
    return _dot(a.astype(BF16), b.astype(BF16), dn)


def _sigmoid(x):
    return jax.nn.sigmoid(x)


def _colsum(x):
    return jnp.sum(x, axis=0, keepdims=True)


def _pick_tile(dim, want):
    if dim <= want:
        return dim
    for t in range(want - want % LANES, 0, -LANES):
        if dim % t == 0:
            return t
    raise ValueError((dim, want))


MATMUL_TILE = 1024
MATMUL_VMEM_BUDGET = 40 * 1024 * 1024


def _matmul_tiles(m, n, k, mode, in_bytes, out_bytes):
    tn = _pick_tile(n, MATMUL_TILE)
    k_tiles = [k] + [t for t in (4096, 2048, 1024) if t < k and k % t == 0]
    m_tiles = [_pick_tile(m, MATMUL_TILE)] + ([512] if mode != "tn" and m % 512 == 0 and m > 512 else [])
    for tk in k_tiles:
        for tm in m_tiles:
            blocks = 2 * (tm * tk * in_bytes[0] + tk * tn * in_bytes[1] + tm * tn * out_bytes)
            if blocks + (4 * tm * tn if tk < k else 0) <= MATMUL_VMEM_BUDGET:
                return tm, tn, tk
    raise ValueError((m, n, k))


def _matmul(name, a, b, mode="nn", out_dtype=F32, relu=False, square_a=False, mul2=None, colscale=None,
            addin=None, colsum_with=None, ride=None):
    if mode == "nn":
        (m, k), (k2, n) = a.shape, b.shape
    elif mode == "nt":
        (m, k), (n, k2) = a.shape, b.shape
    else:
        (k, m), (k2, n) = a.shape, b.shape
    assert k == k2, (name, a.shape, b.shape)
    tiles = [e for e in (mul2, addin, colsum_with) if e is not None]
    out_bytes = jnp.dtype(out_dtype).itemsize + sum(e.dtype.itemsize for e in tiles)
    tm, tn, tk = _matmul_tiles(m, n, k, mode, (a.dtype.itemsize, b.dtype.itemsize), out_bytes)
    nk = k // tk
    n_ext = len(tiles) + (colscale is not None)
    n_out = 1 + (colsum_with is not None)
    n_ride = 0 if ride is None else len(ride)
    grid = (m // tm, n // tn, nk)

    def kern(*refs):
        a_ref, b_ref = refs[:2]
        e_refs = list(refs[2:2 + n_ext])
        o_refs = refs[2 + n_ext + n_ride:2 + n_ext + n_ride + n_out]
        kk = pl.program_id(2)
        if ride is not None:
            here = [pl.program_id(ax) for ax in range(3)]
            land0 = 2 + n_ext + n_ride + n_out
            ride_refs = (refs[2 + n_ext:2 + n_ext + n_ride], refs[land0:land0 + n_ride]) + tuple(refs[-3:])
            ride_kinds = [kind for _, kind in ride]
            first = (here[0] == 0) & (here[1] == 0) & (here[2] == 0)
            last = (here[0] == grid[0] - 1) & (here[1] == grid[1] - 1) & (here[2] == grid[2] - 1)

            @pl.when(first)
            def _():
                _exchange4_ops(*ride_refs, ride_kinds)[0]()

        av = a_ref[...]
        if square_a:
            av = av * av
        part = _bdot(av, b_ref[...], mode)

        def finish(r):
            ext = list(e_refs)
            m2v = ext.pop(0)[...].astype(F32) if mul2 is not None else None
            addv = ext.pop(0)[...].astype(F32) if addin is not None else None
            if colsum_with is not None:
                o_refs[1][0] = _colsum(r * ext.pop(0)[...].astype(F32))
            if relu:
                r = jnp.maximum(r, 0.0)
            if m2v is not None:
                r = r * (2.0 * m2v)
            if colscale is not None:
                r = r * ext.pop(0)[...]
            if addv is not None:
                r = r + addv
            o_refs[0][...] = r.astype(out_dtype)

        if nk == 1:
            finish(part)
        else:
            acc = refs[-4] if ride is not None else refs[-1]

            @pl.when(kk == 0)
            def _():
                acc[...] = part

            @pl.when(kk > 0)
            def _():
                acc[...] += part

            @pl.when(kk == nk - 1)
            def _():
                finish(acc[...])

        if ride is not None:
            @pl.when(last)
            def _():
                _exchange4_ops(*ride_refs, ride_kinds)[1]()

    if mode == "tn":
        a_spec = pl.BlockSpec((tk, tm), lambda i, j, kk: (kk, i))
    else:
        a_spec = pl.BlockSpec((tm, tk), lambda i, j, kk: (i, kk))
    if mode == "nt":
        b_spec = pl.BlockSpec((tn, tk), lambda i, j, kk: (j, kk))
    else:
        b_spec = pl.BlockSpec((tk, tn), lambda i, j, kk: (kk, j))
    o_spec = pl.BlockSpec((tm, tn), lambda i, j, kk: (i, j))
    in_specs = [a_spec, b_spec] + [o_spec] * len(tiles)
    operands = [a, b] + tiles
    if colscale is not None:
        in_specs.append(pl.BlockSpec((1, tn), lambda i, j, kk: (0, j)))
        operands.append(colscale)
    out_specs = [o_spec]
    out_shape = [jax.ShapeDtypeStruct((m, n), out_dtype)]
    if colsum_with is not None:
        out_specs.append(pl.BlockSpec((1, 1, tn), lambda i, j, kk: (i, 0, j)))
        out_shape.append(jax.ShapeDtypeStruct((m // tm, 1, n), F32))
    scratch = [pltpu.VMEM((tm, tn), F32)] if nk > 1 else []
    semantics = ("parallel", "parallel", "arbitrary")
    if ride is not None:
        for src, kind in ride:
            in_specs.append(pl.BlockSpec(memory_space=pl.ANY))
            operands.append(src)
            out_specs.append(pl.BlockSpec(memory_space=pl.ANY))
            out_shape.append(jax.ShapeDtypeStruct(_landing_shape(src, kind), src.dtype))
        scratch += _exchange_scratch(n_ride)
        semantics = ("arbitrary", "arbitrary", "arbitrary")
    res = pl.pallas_call(
        kern, name=name, grid=grid, in_specs=in_specs, out_specs=out_specs, out_shape=out_shape,
        scratch_shapes=scratch, compiler_params=_params(*semantics),
    )(*operands)
    return res if len(res) > 1 else res[0]


def _rowcall(name, body, rows, tile, row_ins, small_ins, row_outs, acc_outs):
    tile = min(tile, rows)
    assert rows % tile == 0, (name, rows, tile)
    n_in = len(row_ins) + len(small_ins)

    def kern(*refs):
        i = pl.program_id(0)
        accs = refs[n_in + len(row_outs):]

        @pl.when(i == 0)
        def _():
            for acc in accs:
                acc[...] = jnp.zeros_like(acc)

        body(i, *refs)

    def whole(shape):
        return pl.BlockSpec(shape, lambda i, nd=len(shape): (0,) * nd)

    in_specs = [pl.BlockSpec((tile, a.shape[1]), lambda i: (i, 0)) for a in row_ins]
    in_specs += [whole(a.shape) for a in small_ins]
    out_specs = [pl.BlockSpec((tile, w), lambda i: (i, 0)) for (w, _) in row_outs]
    out_specs += [whole(s) for s in acc_outs]
    out_shape = [jax.ShapeDtypeStruct((rows, w), dt) for (w, dt) in row_outs]
    out_shape += [jax.ShapeDtypeStruct(s, F32) for s in acc_outs]
    return pl.pallas_call(
        kern, name=name, grid=(rows // tile,), in_specs=in_specs, out_specs=out_specs, out_shape=out_shape,
        compiler_params=_params("arbitrary"),
    )(*row_ins, *small_ins)


def _rms(x):
    r = lax.rsqrt(jnp.mean(x * x, axis=-1, keepdims=True) + NORM_EPS)
    return x * r, r


def _rms_bwd(dxhat, xhat, r):
    return r * (dxhat - xhat * jnp.mean(dxhat * xhat, axis=-1, keepdims=True))


def _normmod_fwd(name, x, g, sh, sc):
    def body(i, x_ref, g_ref, sh_ref, sc_ref, o_ref):
        xhat, _ = _rms(x_ref[...])
        o_ref[...] = ((xhat * g_ref[...]) * (1.0 + sc_ref[...]) + sh_ref[...]).astype(BF16)

    return _rowcall(name, body, x.shape[0], ROW_TILE, [x], [g, sh, sc], [(x.shape[1], BF16)], [])[0]


def _normmod_bwd(name, x, g, sh, sc, dh, dx_pass):
    d = x.shape[1]

    def body(i, x_ref, dh_ref, dxp_ref, g_ref, sh_ref, sc_ref, dx_ref, dxb_ref, dg_ref, dsh_ref, dsc_ref):
        xhat, r = _rms(x_ref[...])
        dh = dh_ref[...].astype(F32)
        gv = g_ref[...]
        dn = dh * (1.0 + sc_ref[...])
        dsc_ref[...] += _colsum(dh * (xhat * gv))
        dsh_ref[...] += _colsum(dh)
        dg_ref[...] += _colsum(dn * xhat)
        dx = dxp_ref[...] + _rms_bwd(dn * gv, xhat, r)
        dx_ref[...] = dx
        dxb_ref[...] = dx.astype(BF16)

    return _rowcall(name, body, x.shape[0], ROW_TILE, [x, dh, dx_pass], [g, sh, sc], [(d, F32), (d, BF16)],
                    [(1, d), (1, d), (1, d)])


def _scale_cols(name, w, g):
    def body(i, w_ref, g_ref, o_ref):
        o_ref[...] = (w_ref[...].astype(F32) * g_ref[...]).astype(BF16)

    return _rowcall(name, body, w.shape[0], ROW_TILE, [w], [g], [(w.shape[1], BF16)], [])[0]


GELU_K = math.sqrt(2.0 / math.pi)
GELU_C = 0.044715


def _gelu_fwd(name, y, u, skip):
    def body(i, y_ref, u_ref, s_ref, o_ref):
        v = y_ref[...].astype(F32) + s_ref[...] * u_ref[...]
        t = jnp.tanh(GELU_K * (v + GELU_C * (v * v * v)))
        o_ref[...] = (0.5 * v * (1.0 + t)).astype(BF16)

    return _rowcall(name, body, y.shape[0], ROW_TILE, [y, u], [skip], [(y.shape[1], BF16)], [])[0]


def _gelu_bwd(name, y, u, skip, dgl):
    d = y.shape[1]

    def body(i, y_ref, u_ref, d_ref, s_ref, o_ref, ds_ref):
        uv = u_ref[...]
        v = y_ref[...].astype(F32) + s_ref[...] * uv
        t = jnp.tanh(GELU_K * (v + GELU_C * (v * v * v)))
        dv = d_ref[...] * (0.5 * (1.0 + t) + 0.5 * v * (1.0 - t * t) * (GELU_K * (1.0 + 3.0 * GELU_C * v * v)))
        o_ref[...] = dv
        ds_ref[...] += _colsum(dv * uv)

    return _rowcall(name, body, y.shape[0], ROW_TILE, [y, u, dgl], [skip], [(d, F32)], [(1, d)])


def _axpy(name, a, b, scale):
    def body(i, a_ref, b_ref, s_ref, o_ref):
        o_ref[...] = (a_ref[...].astype(F32) + s_ref[...] * b_ref[...]).astype(BF16)

    return _rowcall(name, body, a.shape[0], ROW_TILE, [a, b], [scale], [(a.shape[1], BF16)], [])[0]


def _glu_fwd(name, ab, bias, x, gate):
    d = ab.shape[1] // 2

    def body(i, ab_ref, x_ref, b_ref, g_ref, o_ref):
        v = ab_ref[:, :d] + b_ref[:, :d]
        gt = ab_ref[:, d:] + b_ref[:, d:]
        o_ref[...] = x_ref[...] + g_ref[...] * (v * _sigmoid(gt))

    return _rowcall(name, body, ab.shape[0], ROW_TILE, [ab, x], [bias, gate], [(d, F32)], [])[0]


def _glu_bwd(name, ab, bias, dxo, gate):
    d = ab.shape[1] // 2

    def body(i, ab_ref, dx_ref, b_ref, g_ref, dab_ref, db_ref, dg_ref):
        v = ab_ref[:, :d] + b_ref[:, :d]
        s = _sigmoid(ab_ref[:, d:] + b_ref[:, d:])
        dxo_v = dx_ref[...]
        dg_ref[...] += _colsum(dxo_v * (v * s))
        do = g_ref[...] * dxo_v
        dv = do * s
        dgt = do * v * (s * (1.0 - s))
        dab_ref[:, :d] = dv.astype(BF16)
        dab_ref[:, d:] = dgt.astype(BF16)
        db_ref[:, :d] += _colsum(dv)
        db_ref[:, d:] += _colsum(dgt)

    return _rowcall(name, body, ab.shape[0], ROW_TILE, [ab, dxo], [bias, gate], [(2 * d, BF16)],
                    [(1, 2 * d), (1, d)])


def _gatenorm_fwd(name, y, z, ng):
    di = y.shape[1]
    gw = di // M2_G

    def body(i, y_ref, z_ref, g_ref, o_ref):
        for gi in range(M2_G):
            sl = slice(gi * gw, (gi + 1) * gw)
            zz = z_ref[:, sl]
            y2 = y_ref[:, sl] * (zz * _sigmoid(zz))
            yh, _ = _rms(y2)
            o_ref[:, sl] = (yh * g_ref[:, sl]).astype(BF16)

    return _rowcall(name, body, y.shape[0], ROW_TILE, [y, z], [ng], [(di, BF16)], [])[0]


def _gatenorm_bwd(name, y, z, ng, dyn):
    di = y.shape[1]
    gw = di // M2_G

    def body(i, y_ref, z_ref, d_ref, g_ref, dy_ref, dz_ref, dg_ref):
        for gi in range(M2_G):
            sl = slice(gi * gw, (gi + 1) * gw)
            zz = z_ref[:, sl]
            yy = y_ref[:, sl]
            s = _sigmoid(zz)
            sz = zz * s
            yh, r = _rms(yy * sz)
            dn = d_ref[:, sl]
            dg_ref[:, sl] += _colsum(dn * yh)
            dy2 = _rms_bwd(dn * g_ref[:, sl], yh, r)
            dy_ref[:, sl] = dy2 * sz
            dz_ref[:, sl] = (dy2 * yy * (s * (1.0 + zz * (1.0 - s)))).astype(BF16)

    return _rowcall(name, body, y.shape[0], ROW_TILE, [y, z, dyn], [ng], [(di, F32), (di, BF16)], [(1, di)])


def _loss_head(name, x, target, g):
    d = x.shape[1]

    def body(i, x_ref, t_ref, g_ref, dx_ref, dxb_ref, dg_ref, loss_ref):
        xhat, r = _rms(x_ref[...])
        gv = g_ref[...]
        err = xhat * gv - t_ref[...]
        per_row = jnp.sum(err * err, axis=-1, keepdims=True) * (0.5 / d)
        loss_ref[...] += jnp.broadcast_to(_colsum(per_row), loss_ref.shape)
        dy = err * (1.0 / d)
        dg_ref[...] += _colsum(dy * xhat)
        dx = _rms_bwd(dy * gv, xhat, r)
        dx_ref[...] = dx
        dxb_ref[...] = dx.astype(BF16)

    return _rowcall(name, body, x.shape[0], ROW_TILE, [x, target], [g], [(d, F32), (d, BF16)],
                    [(1, d), (1, LANES)])


HALO = 8


def _halo_call(name, body, rows, tile, width, mains, halo_of, halo_next, smalls, row_outs, acc_outs, scratch):
    tile = min(tile, rows)
    nb = tile // HALO
    last = rows // HALO - 1
    n_in = len(mains) + 1 + len(smalls)

    def kern(*refs):
        i = pl.program_id(0)
        accs = refs[n_in + len(row_outs):n_in + len(row_outs) + len(acc_outs)]

        @pl.when(i == 0)
        def _():
            for acc in accs:
                acc[...] = jnp.zeros_like(acc)

        body(i, *refs)

    def whole(shape):
        return pl.BlockSpec(shape, lambda i, nd=len(shape): (0,) * nd)

    if halo_next:
        halo_spec = pl.BlockSpec((HALO, width), lambda i: (jnp.minimum((i + 1) * nb, last), 0))
    else:
        halo_spec = pl.BlockSpec((HALO, width), lambda i: (jnp.maximum(i * nb - 1, 0), 0))
    in_specs = [pl.BlockSpec((tile, a.shape[1]), lambda i: (i, 0)) for a in mains] + [halo_spec]
    in_specs += [whole(a.shape) for a in smalls]
    out_specs = [pl.BlockSpec((tile, w), lambda i: (i, 0)) for (w, _) in row_outs] + [whole(s) for s in acc_outs]
    out_shape = [jax.ShapeDtypeStruct((rows, w), dt) for (w, dt) in row_outs]
    out_shape += [jax.ShapeDtypeStruct(s, F32) for s in acc_outs]
    return pl.pallas_call(
        kern, name=name, grid=(rows // tile,), in_specs=in_specs, out_specs=out_specs, out_shape=out_shape,
        scratch_shapes=scratch, compiler_params=_params("arbitrary"),
    )(*mains, mains[halo_of], *smalls)


CONV_TILE = 512
CONV_ROWS = 16
CONV_STRIP = 512


def _conv_blocks(tile, c, strip=CONV_STRIP):
    strip = strip if c % strip == 0 else LANES
    rb = min(CONV_ROWS, tile)
    return [(r0, rb, slice(c0, c0 + strip)) for c0 in range(0, c, strip) for r0 in range(0, tile, rb)]


def _shifted_windows(base, rb, offsets):
    n = base.shape[0]
    out = []
    for o in offsets:
        if o % HALO == 0:
            out.append(base[o:o + rb, :])
        else:
            out.append(pltpu.roll(base, n - o, 0)[0:rb, :])
    return out


def _conv_windows(x_ref, ext, r0, rb, sl):
    base = ext[:, sl] if r0 == 0 else x_ref[r0 - HALO:r0 + rb, sl]
    return _shifted_windows(base, rb, [HALO - 3 + k for k in range(M2_K)])


def _conv_fwd(name, xin, w, b):
    rows, c = xin.shape
    tile = min(CONV_TILE, rows)
    rb0 = min(CONV_ROWS, tile)

    def body(i, x_ref, h_ref, w_ref, b_ref, o_ref, ext):
        ext[0:HALO, :] = jnp.where(i == 0, 0.0, h_ref[...])
        ext[HALO:, :] = x_ref[0:rb0, :]
        for r0, rb, sl in _conv_blocks(tile, c):
            taps = _conv_windows(x_ref, ext, r0, rb, sl)
            pre = b_ref[:, sl] + w_ref[0:1, sl] * taps[0]
            for k in range(1, M2_K):
                pre = pre + w_ref[k:k + 1, sl] * taps[k]
            o_ref[r0:r0 + rb, sl] = pre * _sigmoid(pre)

    return _halo_call(name, body, rows, tile, c, [xin], 0, False, [w, b], [(c, F32)], [],
                      [pltpu.VMEM((rb0 + HALO, c), F32)])[0]


def _conv_bwd_pre(name, xin, dout, w, b):
    rows, c = xin.shape
    tile = min(CONV_TILE, rows)
    rb0 = min(CONV_ROWS, tile)

    def body(i, x_ref, d_ref, h_ref, w_ref, b_ref, dp_ref, dw_ref, db_ref, ext):
        ext[0:HALO, :] = jnp.where(i == 0, 0.0, h_ref[...])
        ext[HALO:, :] = x_ref[0:rb0, :]
        sums = {}
        for r0, rb, sl in _conv_blocks(tile, c, CONV_STRIP // 2):
            taps = _conv_windows(x_ref, ext, r0, rb, sl)
            pre = b_ref[:, sl] + w_ref[0:1, sl] * taps[0]
            for k in range(1, M2_K):
                pre = pre + w_ref[k:k + 1, sl] * taps[k]
            s = _sigmoid(pre)
            dp = d_ref[r0:r0 + rb, sl] * (s * (1.0 + pre * (1.0 - s)))
            dp_ref[r0:r0 + rb, sl] = dp
            part = [dp] + [dp * taps[k] for k in range(M2_K)]
            key = sl.start
            sums[key] = part if key not in sums else [p + q for p, q in zip(sums[key], part)]
            if r0 + rb == tile:
                db_ref[:, sl] += _colsum(sums[key][0])
                for k in range(M2_K):
                    dw_ref[k:k + 1, sl] += _colsum(sums[key][1 + k])

    return _halo_call(name, body, rows, tile, c, [xin, dout], 0, False, [w, b], [(c, F32)], [(M2_K, c), (1, c)],
                      [pltpu.VMEM((rb0 + HALO, c), F32)])


def _conv_bwd_in(name, dpre, w):
    rows, c = dpre.shape
    tile = min(CONV_TILE, rows)
    n_tiles = rows // tile
    rb0 = min(CONV_ROWS, tile)

    def body(i, d_ref, h_ref, w_ref, o_ref, ext):
        ext[0:rb0, :] = d_ref[tile - rb0:tile, :]
        ext[rb0:, :] = jnp.where(i == n_tiles - 1, 0.0, h_ref[...])
        for r0, rb, sl in _conv_blocks(tile, c):
            base = ext[:, sl] if r0 + rb == tile else d_ref[r0:r0 + rb + HALO, sl]
            wins = _shifted_windows(base, rb, [3 - k for k in range(M2_K)])
            acc = w_ref[0:1, sl] * wins[0]
            for k in range(1, M2_K):
                acc = acc + w_ref[k:k + 1, sl] * wins[k]
            o_ref[r0:r0 + rb, sl] = acc.astype(BF16)

    return _halo_call(name, body, rows, tile, c, [dpre], 0, True, [w], [(c, BF16)], [],
                      [pltpu.VMEM((rb0 + HALO, c), F32)])[0]


def _s5_build(lam_re, lam_im, log_dt, b_re, b_im, c_re, c_im):
    g, p = lam_re.shape
    h = b_re.shape[-1]
    t = S5_T
    dt = jnp.exp(log_dt)[:, None]
    ld_re, ld_im = lam_re * dt, lam_im * dt
    tau = jnp.arange(t + 1, dtype=F32)
    mag = jnp.exp(ld_re[:, :, None] * tau)
    ang = ld_im[:, :, None] * tau
    pw_re, pw_im = mag * jnp.cos(ang), mag * jnp.sin(ang)
    num_re, num_im = pw_re[:, :, 1] - 1.0, pw_im[:, :, 1]
    den = lam_re * lam_re + lam_im * lam_im
    q_re = (num_re * lam_re + num_im * lam_im) / den
    q_im = (num_im * lam_re - num_re * lam_im) / den
    bb_re = q_re[:, :, None] * b_re - q_im[:, :, None] * b_im
    bb_im = q_re[:, :, None] * b_im + q_im[:, :, None] * b_re
    bbt_re, bbt_im = jnp.transpose(bb_re, (0, 2, 1)), jnp.transpose(bb_im, (0, 2, 1))
    ct_re, ct_im = jnp.transpose(c_re, (0, 2, 1)), jnp.transpose(c_im, (0, 2, 1))
    lane = jnp.arange(t * h)
    rep_tau = (lane[None, :] // h == jnp.arange(t)[:, None]).astype(F32)
    tile_h = (lane[None, :] % h == jnp.arange(h)[:, None]).astype(F32)

    def spread(x, m):
        return jnp.einsum("gpk,kn->gpn", x, m, precision=lax.Precision.HIGH)

    c_re_n, c_im_n = spread(ct_re, tile_h), spread(ct_im, tile_h)

    def c_times_pw(first):
        pr, pi = spread(pw_re[:, :, first:first + t], rep_tau), spread(pw_im[:, :, first:first + t], rep_tau)
        return pr * c_re_n - pi * c_im_n, pr * c_im_n + pi * c_re_n

    cp0_re, cp0_im = c_times_pw(0)
    cp1_re, cp1_im = c_times_pw(1)
    kc = (jnp.einsum("ghp,gpn->ghn", bbt_re, cp0_re, precision=lax.Precision.HIGH)
          - jnp.einsum("ghp,gpn->ghn", bbt_im, cp0_im, precision=lax.Precision.HIGH))
    bp_re = jnp.transpose(pw_re[:, :, t - 1::-1][:, :, :t], (0, 2, 1))
    bp_im = jnp.transpose(pw_im[:, :, t - 1::-1][:, :, :t], (0, 2, 1))
    be_re = bbt_re[:, :, None, :] * bp_re[:, None, :, :] - bbt_im[:, :, None, :] * bp_im[:, None, :, :]
    be_im = bbt_re[:, :, None, :] * bp_im[:, None, :, :] + bbt_im[:, :, None, :] * bp_re[:, None, :, :]
    bend = jnp.concatenate([be_re, be_im], axis=-1).reshape(g, h * t, 2 * p)
    cpow = jnp.concatenate([cp1_re, -cp1_im], axis=1)
    at_re, at_im = pw_re[:, :, t], pw_im[:, :, t]
    a1 = jnp.concatenate([at_re, at_re], axis=-1)[:, None, :]
    a2 = jnp.concatenate([-at_im, at_im], axis=-1)[:, None, :]
    return kc, bend, cpow, a1, a2


def _swap_halves(x, axis):
    n = x.shape[axis] // 2
    lo = lax.slice_in_dim(x, 0, n, axis=axis)
    hi = lax.slice_in_dim(x, n, 2 * n, axis=axis)
    return jnp.concatenate([hi, lo], axis=axis)


def _group_spec(shape):
    return pl.BlockSpec((1,) + tuple(shape[1:]), lambda g: (g, 0, 0))


S5_ROWS = 8


def _s5_expand_toeplitz(kc_ref, ext, toep):
    t, th = S5_T, S5_T * S5_H
    ext[:, th:] = jnp.zeros((S5_ROWS, LANES), F32)
    for hin in range(S5_H):
        ext[:, :th] = jnp.broadcast_to(kc_ref[0, hin:hin + 1, :], (S5_ROWS, th))
        rolled = pltpu.roll(ext[...], 0, 1, stride=S5_H, stride_axis=0)
        tiles = []
        for q in range(t // S5_ROWS):
            if q == 0:
                tiles.append(rolled[:, :th])
            else:
                tiles.append(jnp.concatenate([jnp.zeros((S5_ROWS, q * LANES), F32), rolled[:, :th - q * LANES]],
                                             axis=1))
        toep[hin * t:(hin + 1) * t, :] = jnp.concatenate(tiles, axis=0).astype(BF16)


def _s5_core_fwd(name, u, ops):
    kc, bend, cpow, a1, a2 = ops
    g, nc, th = u.shape
    p2 = bend.shape[-1]
    bend_b, cpow_b = bend.astype(BF16), cpow.astype(BF16)
    a2s = _swap_halves(a2, 2)

    def kern(u_ref, k_ref, b_ref, c_ref, a1_ref, a2_ref, a2s_ref, y_ref, sp_ref, x_scr, xs_scr, ext, toep):
        _s5_expand_toeplitz(k_ref, ext, toep)
        ub = u_ref[0].astype(BF16)
        xv = _dot(ub, b_ref[0])
        x_scr[...] = xv
        xs_scr[...] = pltpu.roll(xv, p2 // 2, 1)
        a1v, a2v, a2sv = a1_ref[0], a2_ref[0], a2s_ref[0]

        def step(c, carry):
            s, ss = carry
            sp_ref[0, pl.ds(c, 1), :] = s
            s_new = a1v * s + a2v * ss + x_scr[pl.ds(c, 1), :]
            ss_new = a1v * ss + a2sv * s + xs_scr[pl.ds(c, 1), :]
            return s_new, ss_new

        zero = jnp.zeros((1, p2), F32)
        lax.fori_loop(0, nc, step, (zero, zero))
        y_ref[0] = (_dot(ub, toep[...]) + _dot(sp_ref[0].astype(BF16), c_ref[0])).astype(BF16)

    ins = [u, kc, bend_b, cpow_b, a1, a2, a2s]
    return pl.pallas_call(
        kern, name=name, grid=(g,), in_specs=[_group_spec(a.shape) for a in ins],
        out_specs=[_group_spec((g, nc, th)), _group_spec((g, nc, p2))],
        out_shape=[jax.ShapeDtypeStruct((g, nc, th), BF16), jax.ShapeDtypeStruct((g, nc, p2), F32)],
        scratch_shapes=[pltpu.VMEM((nc, p2), F32), pltpu.VMEM((nc, p2), F32),
                        pltpu.VMEM((S5_ROWS, th + LANES), F32), pltpu.VMEM((th, th), BF16)],
        compiler_params=_params("arbitrary"),
    )(*ins)


def _s5_core_bwd(name, u, dy, sprev, ops):
    kc, bend, cpow, a1, a2 = ops
    g, nc, th = u.shape
    t = S5_T
    p2 = bend.shape[-1]
    bend_b, cpow_b = bend.astype(BF16), cpow.astype(BF16)
    a2s = _swap_halves(a2, 2)
    idx = jnp.arange(th)
    flip = (idx[:, None] // t == idx[None, :] // t) & (idx[:, None] % t == t - 1 - idx[None, :] % t)
    flip = flip.astype(BF16)

    def kern(u_ref, dy_ref, sp_ref, k_ref, b_ref, c_ref, a1_ref, a2_ref, a2s_ref, f_ref,
             du_ref, dk_ref, db_ref, dc_ref, da1_ref, da2_ref, g_scr, gs_scr, dx_scr, ext, toep, dtoep):
        _s5_expand_toeplitz(k_ref, ext, toep)
        ub, dyb = u_ref[0].astype(BF16), dy_ref[0].astype(BF16)
        dtoep[...] = _dot(_dot(ub, f_ref[...]).astype(BF16), dyb, "tn")
        n_q = t // S5_ROWS
        width = th + LANES
        for hin in range(S5_H):
            folded = dtoep[hin * t + (n_q - 1) * S5_ROWS:(hin + 1) * t, :]
            for qp in range(n_q - 1):
                q = n_q - 1 - qp
                tile = dtoep[hin * t + qp * S5_ROWS:hin * t + (qp + 1) * S5_ROWS, :]
                folded = folded + jnp.concatenate([tile[:, q * LANES:], jnp.zeros((S5_ROWS, q * LANES), F32)],
                                                  axis=1)
            ext[:, :th] = folded
            rolled = pltpu.roll(ext[...], 0, 1, stride=S5_H, stride_axis=0)
            rolled = pltpu.roll(rolled, width - S5_H * (S5_ROWS - 1), 1)
            dk_ref[0, hin:hin + 1, :] = _colsum(rolled)[:, :th]
        spv = sp_ref[0]
        dc_ref[0] = _dot(spv.astype(BF16), dyb, "tn")
        gv = _dot(dyb, c_ref[0], "nt")
        g_scr[...] = gv
        gs_scr[...] = pltpu.roll(gv, p2 // 2, 1)
        a1v, a2v, a2sv = a1_ref[0], a2_ref[0], a2s_ref[0]

        def step(k, carry):
            gr, grs, da1, da2 = carry
            c = nc - 1 - k
            dx_scr[pl.ds(c, 1), :] = gr
            s_in = sp_ref[0, pl.ds(c, 1), :]
            da1 = da1 + gr * s_in
            da2 = da2 + grs * s_in
            gr_new = g_scr[pl.ds(c, 1), :] + a1v * gr + a2sv * grs
            grs_new = gs_scr[pl.ds(c, 1), :] + a1v * grs + a2v * gr
            return gr_new, grs_new, da1, da2

        zero = jnp.zeros((1, p2), F32)
        _, _, da1, da2 = lax.fori_loop(0, nc, step, (zero, zero, zero, zero))
        da1_ref[0] = da1
        da2_ref[0] = da2
        dxb = dx_scr[...].astype(BF16)
        db_ref[0] = _dot(ub, dxb, "tn")
        du_ref[0] = (_dot(dyb, toep[...], "nt") + _dot(dxb, b_ref[0], "nt")).astype(BF16)

    ins = [u, dy, sprev, kc, bend_b, cpow_b, a1, a2, a2s]
    outs = [(g, nc, th), (g, S5_H, th), (g, th, p2), (g, p2, th), (g, 1, p2), (g, 1, p2)]
    out_types = [BF16] + [F32] * (len(outs) - 1)
    return pl.pallas_call(
        kern, name=name, grid=(g,),
        in_specs=[_group_spec(a.shape) for a in ins] + [pl.BlockSpec((th, th), lambda gi: (0, 0))],
        out_specs=[_group_spec(s) for s in outs],
        out_shape=[jax.ShapeDtypeStruct(s, dt) for s, dt in zip(outs, out_types)],
        scratch_shapes=[pltpu.VMEM((nc, p2), F32), pltpu.VMEM((nc, p2), F32), pltpu.VMEM((nc, p2), F32),
                        pltpu.VMEM((S5_ROWS, th + LANES), F32), pltpu.VMEM((th, th), BF16),
                        pltpu.VMEM((th, th), F32)],
        compiler_params=_params("arbitrary"),
    )(*ins, flip)


def _s5_to_groups(u, channel_major):
    rows, w = u.shape
    g = w // S5_H
    nc = rows // S5_T
    perm = (2, 0, 3, 1) if channel_major else (2, 0, 1, 3)
    return u.reshape(nc, S5_T, g, S5_H).transpose(perm).reshape(g, nc, S5_T * S5_H)


def _s5_from_groups(y, channel_major):
    g, nc, _ = y.shape
    if channel_major:
        return y.reshape(g, nc, S5_H, S5_T).transpose(1, 3, 0, 2).reshape(nc * S5_T, g * S5_H)
    return y.reshape(g, nc, S5_T, S5_H).transpose(1, 2, 0, 3).reshape(nc * S5_T, g * S5_H)


def _softplus(x):
    return jnp.maximum(x, 0.0) + jnp.log(1.0 + jnp.exp(-jnp.abs(x)))


def _ssd_chunk_prep(dtraw_ref, dtb_ref, a_ref, expand_ref, cst, dtx, lastt, n_heads):
    q = M2_Q
    lane = lax.broadcasted_iota(jnp.int32, (q, LANES), 1)
    dt = jnp.where(lane < n_heads, _softplus(dtraw_ref[...] + dtb_ref[...]), 0.0)
    adt = dt * a_ref[...]
    row = lax.broadcasted_iota(jnp.int32, (q, q), 0)
    col = lax.broadcasted_iota(jnp.int32, (q, q), 1)
    cs = _dot(jnp.where(row >= col, 1.0, 0.0), adt, precision=HIGHEST)
    cst[...] = cs.T
    dtx[...] = _bdot(dt, expand_ref[...])
    lastt[...] = jnp.broadcast_to(_colsum(adt), (q, LANES)).T
    return dt


def _pair_tables(cst, lastt, p):
    q = M2_Q
    out = []
    for hh in (2 * p, 2 * p + 1):
        rc = jnp.broadcast_to(cst[hh:hh + 1, :], (q, q))
        cc = rc.T
        lb = jnp.broadcast_to(lastt[hh:hh + 1, :], (q, q))
        out.append((rc, cc, lb))
    return out


def _ssd_pair_fwd(x, bm, cm, cb, hs, tabs, dtp):
    q = M2_Q
    row = lax.broadcasted_iota(jnp.int32, (q, q), 0)
    col = lax.broadcasted_iota(jnp.int32, (q, q), 1)
    causal = row >= col
    lo = col < M2_P
    slo = row < M2_P
    (rc0, cc0, lb0), (rc1, cc1, lb1) = tabs
    l0 = jnp.where(causal, jnp.exp(cc0 - rc0), 0.0)
    l1 = jnp.where(causal, jnp.exp(cc1 - rc1), 0.0)
    m0, m1 = cb * l0, cb * l1
    xdt = x * dtp
    xdt0 = jnp.where(lo, xdt, 0.0)
    xdt1 = jnp.where(lo, 0.0, xdt)
    e = jnp.where(lo, jnp.exp(cc0), jnp.exp(cc1))
    z = _bdot(cm, hs, "nt")
    yoff = z * e
    dec = jnp.where(lo, jnp.exp(lb0 - cc0), jnp.exp(lb1 - cc1))
    xdd = xdt * dec
    cd = jnp.where(slo, jnp.exp(lb0), jnp.exp(lb1))
    return dict(l0=l0, l1=l1, m0=m0, m1=m1, dtp=dtp, xdt=xdt, xdt0=xdt0, xdt1=xdt1, e=e, yoff=yoff,
                dec=dec, xdd=xdd, cd=cd, lo=lo, slo=slo)


def _ssd_fwd(name, xbc, dtraw, dtb, arow, dvec, expand, n_heads):
    rows, c = xbc.shape
    q, n = M2_Q, M2_N
    di = n_heads * M2_P
    n_pairs = n_heads // 2
    ppg = n_pairs // M2_G
    nc = rows // q

    def kern(xbc_ref, dtraw_ref, dtb_ref, a_ref, d_ref, e_ref, y_ref, prev_ref, state, cst, dtx, lastt):
        @pl.when(pl.program_id(0) == 0)
        def _():
            state[...] = jnp.zeros_like(state)

        _ssd_chunk_prep(dtraw_ref, dtb_ref, a_ref, e_ref, cst, dtx, lastt, n_heads)
        for p in range(n_pairs):
            gi = p // ppg
            sl = slice(p * LANES, (p + 1) * LANES)
            x = xbc_ref[:, sl]
            bm = xbc_ref[:, di + gi * n:di + (gi + 1) * n]
            cm = xbc_ref[:, di + (M2_G + gi) * n:di + (M2_G + gi + 1) * n]
            if p % ppg == 0:
                cb = _bdot(cm, bm, "nt")
            hs = state[p]
            f = _ssd_pair_fwd(x, bm, cm, cb, hs, _pair_tables(cst, lastt, p), dtx[:, sl])
            ydiag = _bdot(f["m0"], f["xdt0"]) + _bdot(f["m1"], f["xdt1"])
            y_ref[:, sl] = ydiag + f["yoff"] + d_ref[:, sl] * x
            prev_ref[0, p] = hs
            state[p] = f["cd"] * hs + _bdot(f["xdd"], bm, "tn")

    def whole(a):
        return pl.BlockSpec(a.shape, lambda i: (0, 0))

    return pl.pallas_call(
        kern, name=name, grid=(nc,),
        in_specs=[pl.BlockSpec((q, c), lambda i: (i, 0)), pl.BlockSpec((q, LANES), lambda i: (i, 0)),
                  whole(dtb), whole(arow), whole(dvec), whole(expand)],
        out_specs=[pl.BlockSpec((q, di), lambda i: (i, 0)),
                   pl.BlockSpec((1, n_pairs, 2 * M2_P, n), lambda i: (i, 0, 0, 0))],
        out_shape=[jax.ShapeDtypeStruct((rows, di), F32),
                   jax.ShapeDtypeStruct((nc, n_pairs, 2 * M2_P, n), F32)],
        scratch_shapes=[pltpu.VMEM((n_pairs, 2 * M2_P, n), F32), pltpu.VMEM((LANES, q), F32),
                        pltpu.VMEM((q, di), F32), pltpu.VMEM((LANES, q), F32)],
        compiler_params=_params("arbitrary"),
    )(xbc, dtraw, dtb, arow, dvec, expand)


def _ssd_bwd(name, xbc, dtraw, dy, prev, dtb, arow, dvec, seg, expand, n_heads):
    rows, c = xbc.shape
    q, n = M2_Q, M2_N
    di = n_heads * M2_P
    n_pairs = n_heads // 2
    ppg = n_pairs // M2_G
    nc = rows // q

    def kern(xbc_ref, dtraw_ref, dy_ref, prev_ref, dtb_ref, a_ref, d_ref, seg_ref, e_ref,
             dxbc_ref, ddt_ref, da_ref, ddtb_ref, dd_ref,
             dstate, cst, dtx, lastt, dcst, wx, colterm, ddfull):
        step = pl.program_id(0)

        @pl.when(step == 0)
        def _():
            dstate[...] = jnp.zeros_like(dstate)
            ddfull[...] = jnp.zeros_like(ddfull)
            da_ref[...] = jnp.zeros_like(da_ref)
            ddtb_ref[...] = jnp.zeros_like(ddtb_ref)
            dd_ref[...] = jnp.zeros_like(dd_ref)

        dt = _ssd_chunk_prep(dtraw_ref, dtb_ref, a_ref, e_ref, cst, dtx, lastt, n_heads)
        dcst[...] = jnp.zeros_like(dcst)
        lane_q = lax.broadcasted_iota(jnp.int32, (1, q), 1)
        last_hot = jnp.where(lane_q == q - 1, 1.0, 0.0)

        def total(v):
            return jnp.sum(jnp.sum(v, axis=1, keepdims=True), axis=0, keepdims=True)

        for gi in range(M2_G):
            bm = xbc_ref[:, di + gi * n:di + (gi + 1) * n]
            cm = xbc_ref[:, di + (M2_G + gi) * n:di + (M2_G + gi + 1) * n]
            cb = _bdot(cm, bm, "nt")
            dcb = jnp.zeros((q, q), F32)
            dbm = jnp.zeros((q, n), F32)
            dcm = jnp.zeros((q, n), F32)
            for p in range(gi * ppg, (gi + 1) * ppg):
                sl = slice(p * LANES, (p + 1) * LANES)
                x = xbc_ref[:, sl]
                dyp = dy_ref[:, sl]
                hs = prev_ref[0, p]
                ds = dstate[p]
                f = _ssd_pair_fwd(x, bm, cm, cb, hs, _pair_tables(cst, lastt, p), dtx[:, sl])
                lo, slo = f["lo"], f["slo"]
                ddfull[:, sl] += _colsum(dyp * x)
                dy0 = jnp.where(lo, dyp, 0.0)
                dy1 = jnp.where(lo, 0.0, dyp)
                dm0 = _bdot(dyp, f["xdt0"], "nt")
                dm1 = _bdot(dyp, f["xdt1"], "nt")
                dxdt = _bdot(f["m0"], dy0, "tn") + _bdot(f["m1"], dy1, "tn")
                dcb = dcb + dm0 * f["l0"] + dm1 * f["l1"]
                w0, w1 = dm0 * f["m0"], dm1 * f["m1"]
                dz = dyp * f["e"]
                dcm = dcm + _bdot(dz, hs)
                dhs = _bdot(dz, cm, "tn") + f["cd"] * ds
                tot = ds * hs * f["cd"]
                dxdd = _bdot(bm, ds, "nt")
                dbm = dbm + _bdot(f["xdd"], ds)
                ee = dxdd * f["xdd"]
                colterm[:, sl] = dyp * f["yoff"] - ee
                dxdt = dxdt + dxdd * f["dec"]
                t_all = total(tot)
                t_lo = total(jnp.where(slo, tot, 0.0))
                e_all = total(ee)
                e_lo = total(jnp.where(lo, ee, 0.0))
                dlast0 = t_lo + e_lo
                dlast1 = (t_all - t_lo) + (e_all - e_lo)
                dcst[2 * p:2 * p + 1, :] = _colsum(w0.T - w0) + dlast0 * last_hot
                dcst[2 * p + 1:2 * p + 2, :] = _colsum(w1.T - w1) + dlast1 * last_hot
                dxbc_ref[:, sl] = d_ref[:, sl] * dyp + dxdt * f["dtp"]
                wx[:, sl] = dxdt * x
                dstate[p] = dhs
            dcm = dcm + _bdot(dcb, bm)
            dbm = dbm + _bdot(dcb, cm, "tn")
            dxbc_ref[:, di + gi * n:di + (gi + 1) * n] = dbm
            dxbc_ref[:, di + (M2_G + gi) * n:di + (M2_G + gi + 1) * n] = dcm

        segv = seg_ref[...]
        dcs = _dot(colterm[...], segv, precision=HIGHEST) + dcst[...].T
        row = lax.broadcasted_iota(jnp.int32, (q, q), 0)
        col = lax.broadcasted_iota(jnp.int32, (q, q), 1)
        ddelta = _dot(jnp.where(col >= row, 1.0, 0.0), dcs, precision=HIGHEST)
        ddt = _dot(wx[...], segv, precision=HIGHEST) + ddelta * a_ref[...]
        da_ref[...] += _colsum(ddelta * dt)
        lane = lax.broadcasted_iota(jnp.int32, (q, LANES), 1)
        ddtraw = jnp.where(lane < n_heads, ddt * _sigmoid(dtraw_ref[...] + dtb_ref[...]), 0.0)
        ddt_ref[...] = ddtraw
        ddtb_ref[...] += _colsum(ddtraw)

        @pl.when(step == nc - 1)
        def _():
            dd_ref[...] = _dot(jnp.broadcast_to(ddfull[...], (8, di)), segv, precision=HIGHEST)

    def whole(a):
        return pl.BlockSpec(a.shape, lambda i: (0, 0))

    def rev(i):
        return nc - 1 - i

    acc = jax.ShapeDtypeStruct((1, LANES), F32)
    acc_spec = pl.BlockSpec((1, LANES), lambda i: (0, 0))
    acc8 = jax.ShapeDtypeStruct((8, LANES), F32)
    acc8_spec = pl.BlockSpec((8, LANES), lambda i: (0, 0))
    return pl.pallas_call(
        kern, name=name, grid=(nc,),
        in_specs=[pl.BlockSpec((q, c), lambda i: (rev(i), 0)), pl.BlockSpec((q, LANES), lambda i: (rev(i), 0)),
                  pl.BlockSpec((q, di), lambda i: (rev(i), 0)),
                  pl.BlockSpec((1, n_pairs, 2 * M2_P, n), lambda i: (rev(i), 0, 0, 0)),
                  whole(dtb), whole(arow), whole(dvec), whole(seg), whole(expand)],
        out_specs=[pl.BlockSpec((q, c), lambda i: (rev(i), 0)), pl.BlockSpec((q, LANES), lambda i: (rev(i), 0)),
                   acc_spec, acc_spec, acc8_spec],
        out_shape=[jax.ShapeDtypeStruct((rows, c), F32), jax.ShapeDtypeStruct((rows, LANES), F32), acc, acc, acc8],
        scratch_shapes=[pltpu.VMEM((n_pairs, 2 * M2_P, n), F32), pltpu.VMEM((LANES, q), F32),
                        pltpu.VMEM((q, di), F32), pltpu.VMEM((LANES, q), F32), pltpu.VMEM((LANES, q), F32),
                        pltpu.VMEM((q, di), F32), pltpu.VMEM((q, di), F32), pltpu.VMEM((1, di), F32)],
        compiler_params=_params("arbitrary"),
    )(xbc, dtraw, dy, prev, dtb, arow, dvec, seg, expand)


S5_PARAM_NAMES = ("s5_lambda_re", "s5_lambda_im", "s5_log_dt", "s5_b_re", "s5_b_im", "s5_c_re", "s5_c_im")


def _row(v):
    return v.reshape(1, -1)


class _Rides:
    def __init__(self):
        self.pending = {}
        self.landed = {}

    def matmul(self, site, name, *args, **kw):
        ride = self.pending.pop(site, None)
        res = _matmul(name, *args, ride=ride, **kw)
        if ride is None:
            return res
        self.landed[site] = list(res[-len(ride):])
        res = list(res[:-len(ride)])
        return res[0] if len(res) == 1 else res


def _s5_layer_fwd(tag, x, gate, h, w, j, rides):
    u = rides.matmul("s5_win", tag + "_win", h, w["s5_w_in"][j])
    params = [w[k][j] for k in S5_PARAM_NAMES]
    ops, build_vjp = jax.vjp(_s5_build, *params)
    ug = _s5_to_groups(u.astype(BF16), True)
    yg, sprev = _s5_core_fwd(tag + "_core", ug, ops)
    yy = _s5_from_groups(yg, False)
    skip = _row(w["s5_d"][j])
    gl = _gelu_fwd(tag + "_gelu", yy, u, skip)
    ab = rides.matmul("s5_wglu", tag + "_wglu", gl, w["s5_w_glu"][j])
    x1 = _glu_fwd(tag + "_glu", ab, _row(w["s5_b_glu"][j]), x, gate)
    return x1, dict(u=u, ug=ug, ops=ops, build_vjp=build_vjp, sprev=sprev, yy=yy, gl=gl, ab=ab, skip=skip)


def _s5_layer_bwd(tag, dx1, gate, h, sv, w, j, rides):
    dab, db_glu, dgate = _glu_bwd(tag + "_glu_b", sv["ab"], _row(w["s5_b_glu"][j]), dx1, gate)
    dw_glu = rides.matmul("s5_dwglu", tag + "_dwglu", sv["gl"], dab, "tn", out_dtype=BF16)
    dgl = rides.matmul("s5_dgl", tag + "_dgl", dab, w["s5_w_glu"][j], "nt")
    dyy, dskip = _gelu_bwd(tag + "_gelu_b", sv["yy"], sv["u"], sv["skip"], dgl)
    dug, dkc, dbend, dcpow, da1, da2s = _s5_core_bwd(
        tag + "_core_b", sv["ug"], _s5_to_groups(dyy.astype(BF16), False), sv["sprev"], sv["ops"])
    dparams = sv["build_vjp"]((dkc, dbend, dcpow, da1, _swap_halves(da2s, 2)))
    du = _axpy(tag + "_du", _s5_from_groups(dug, True), dyy, sv["skip"])
    grads = dict(zip(S5_PARAM_NAMES, dparams))
    grads["s5_d"] = dskip.reshape(-1)
    grads["s5_w_in"] = _matmul(tag + "_dwin", h, du, "tn", out_dtype=BF16)
    grads["s5_w_glu"] = dw_glu
    grads["s5_b_glu"] = db_glu.reshape(-1)
    dh = _matmul(tag + "_dh", du, w["s5_w_in"][j], "nt")
    return dh, grads, dgate


def _ssd_consts(w, j, d_model):
    di = 2 * d_model
    heads = di // M2_P

    def pad_row(v):
        return jnp.zeros((1, LANES), F32).at[0, :heads].set(v)

    a = -jnp.exp(w["m2_a_log"][j])
    seg = (jnp.arange(di)[:, None] // M2_P == jnp.arange(LANES)[None, :]).astype(F32)
    w_in = w["m2_w_in"][j]
    conv_dim = di + 2 * M2_G * M2_N
    w_dt = jnp.zeros((d_model, LANES), w_in.dtype).at[:, :heads].set(w_in[:, di + conv_dim:])
    return dict(di=di, heads=heads, conv_dim=conv_dim, a=a, arow=pad_row(a), dtb=pad_row(w["m2_dt_bias"][j]),
                dvec=_row(jnp.repeat(w["m2_d"][j], M2_P)), seg=seg, expand=seg.T.astype(BF16),
                w_z=w_in[:, :di], w_xbc=w_in[:, di:di + conv_dim], w_dt=w_dt,
                conv_w=w["m2_conv_w"][j], conv_b=_row(w["m2_conv_b"][j]), norm_g=_row(w["m2_norm_g"][j]))


def _gated_out_bwd(tag, act, dxo, w_out, gate, **kw):
    dw, dgate_parts = _matmul(tag + "_dwo", act, dxo, "tn", out_dtype=BF16, colscale=gate, colsum_with=w_out, **kw)
    dgate = jnp.sum(dgate_parts, axis=0)
    return dw, dgate, _scale_cols(tag + "_wog", w_out, gate)


def _ssd_layer_fwd(tag, x, gate, h, w, j):
    k = _ssd_consts(w, j, h.shape[1])
    z = _matmul(tag + "_wz", h, k["w_z"])
    xbc_pre = _matmul(tag + "_wxbc", h, k["w_xbc"])
    dtraw = _matmul(tag + "_wdt", h, k["w_dt"])
    xbc = _conv_fwd(tag + "_conv", xbc_pre, k["conv_w"], k["conv_b"])
    y, prev = _ssd_fwd(tag + "_core", xbc, dtraw, k["dtb"], k["arow"], k["dvec"], k["expand"], k["heads"])
    yn = _gatenorm_fwd(tag + "_gn", y, z, k["norm_g"])
    x1 = _matmul(tag + "_wout", yn, w["m2_w_out"][j], colscale=gate, addin=x)
    return x1, dict(k=k, z=z, xbc_pre=xbc_pre, dtraw=dtraw, xbc=xbc, y=y, prev=prev, yn=yn)


def _ssd_layer_bwd(tag, dx1_b, gate, h, sv, w, j):
    k = sv["k"]
    heads = k["heads"]
    dw_out, dgate, wog = _gated_out_bwd(tag, sv["yn"], dx1_b, w["m2_w_out"][j], gate)
    grads = {"m2_w_out": dw_out}
    dyn = _matmul(tag + "_dyn", dx1_b, wog, "nt")
    dyssd, dz, dng = _gatenorm_bwd(tag + "_gn_b", sv["y"], sv["z"], k["norm_g"], dyn)
    dxbc, ddtraw, da, ddtb, dd = _ssd_bwd(tag + "_core_b", sv["xbc"], sv["dtraw"], dyssd, sv["prev"],
                                          k["dtb"], k["arow"], k["dvec"], k["seg"], k["expand"], heads)
    dpre, dcw, dcb = _conv_bwd_pre(tag + "_conv_b1", sv["xbc_pre"], dxbc, k["conv_w"], k["conv_b"])
    dxbc_pre = _conv_bwd_in(tag + "_conv_b2", dpre, k["conv_w"])
    dw_z = _matmul(tag + "_dwz", h, dz, "tn", out_dtype=BF16)
    dw_xbc = _matmul(tag + "_dwxbc", h, dxbc_pre, "tn", out_dtype=BF16)
    dw_dt = _matmul(tag + "_dwdt", h, ddtraw, "tn", out_dtype=BF16)
    dh = _matmul(tag + "_dh1", dz, k["w_z"], "nt")
    dh = _matmul(tag + "_dh2", dxbc_pre, k["w_xbc"], "nt", addin=dh)
    dh = _matmul(tag + "_dh3", ddtraw, k["w_dt"], "nt", addin=dh)
    grads["m2_w_in"] = jnp.concatenate([dw_z, dw_xbc, dw_dt[:, :heads]], axis=1)
    grads["m2_conv_w"] = dcw
    grads["m2_conv_b"] = dcb.reshape(-1)
    grads["m2_dt_bias"] = ddtb[0, :heads]
    grads["m2_a_log"] = da[0, :heads] * k["a"]
    grads["m2_d"] = dd[0, :heads]
    grads["m2_norm_g"] = dng.reshape(-1)
    return dh, grads, dgate


def _layer_fwd(li, x, mod, w, rides, late_weights=None):
    tag = "L%d" % li
    sh1, sc1, g1, sh2, sc2, g2 = mod
    j = li // 2
    h = _normmod_fwd(tag + "_nm1", x, _row(w["norm_mix_g"][li]), sh1, sc1)
    if li % 2 == 0:
        x1, mix = _s5_layer_fwd(tag + "_s5", x, g1, h, w, j, rides)
    else:
        x1, mix = _ssd_layer_fwd(tag + "_m2", x, g1, h, w, j)
    if late_weights is not None:
        w = {**w, **late_weights(rides.landed)}
    h2 = _normmod_fwd(tag + "_nm2", x1, _row(w["norm_mlp_g"][li]), sh2, sc2)
    r = rides.matmul("w1", tag + "_w1", h2, w["mlp_w1"][li], relu=True, out_dtype=BF16)
    x2 = rides.matmul("w2", tag + "_w2", r, w["mlp_w2"][li], square_a=True, colscale=g2, addin=x1)
    return x2, dict(x=x, h=h, mix=mix, x1=x1, h2=h2, r=r), w


def _layer_bwd(li, dx2, dx2_b, sv, mod, w, rides, ride_own_mlp):
    tag = "L%d" % li
    sh1, sc1, g1, sh2, sc2, g2 = mod
    j = li // 2
    dw2, dg2, w2g = _gated_out_bwd(tag + "_mlp", sv["r"], dx2_b, w["mlp_w2"][li], g2, square_a=True)
    dr = rides.matmul("dr", tag + "_dr", dx2_b, w2g, "nt", out_dtype=BF16, mul2=sv["r"])
    dw1 = _matmul(tag + "_dw1", sv["h2"], dr, "tn", out_dtype=BF16)
    grads = {"mlp_w2": dw2, "mlp_w1": dw1}
    dh2 = rides.matmul("dh2", tag + "_dh2", dr, w["mlp_w1"][li], "nt")
    if ride_own_mlp:
        mlp_bufs = _grad_buffers(li, grads, parts=[1])[0]
        rides.pending["s5_dgl"], rides.pending["s5_dwglu"] = mlp_bufs[:1], mlp_bufs[1:]
    dx1, dx1_b, dgm, dsh2, dsc2 = _normmod_bwd(tag + "_nm2_b", sv["x1"], _row(w["norm_mlp_g"][li]), sh2, sc2, dh2,
                                               dx2)
    if li % 2 == 0:
        dh, mix_grads, dg1 = _s5_layer_bwd(tag + "_s5", dx1, g1, sv["h"], sv["mix"], w, j, rides)
    else:
        dh, mix_grads, dg1 = _ssd_layer_bwd(tag + "_m2", dx1_b, g1, sv["h"], sv["mix"], w, j)
    dx, dx_b, dgx, dsh1, dsc1 = _normmod_bwd(tag + "_nm1_b", sv["x"], _row(w["norm_mix_g"][li]), sh1, sc1, dh, dx1)
    grads["norm_mix_g"] = dgx.reshape(-1)
    grads["norm_mlp_g"] = dgm.reshape(-1)
    dmod = jnp.concatenate([dsh1, dsc1, dg1, dsh2, dsc2, dg2], axis=1)
    return dx, dx_b, {**grads, **mix_grads}, dmod


def _layer_parts(li):
    j = li // 2
    mix = [("s5_w_in", j, 0), ("s5_w_glu", j, 1)] if li % 2 == 0 else [("m2_w_in", j, 1), ("m2_w_out", j, 0)]
    return [mix, [("mlp_w1", li, 1), ("mlp_w2", li, 0)]]


def _grad_specs(li):
    j = li // 2
    mlp = [("one", ("mlp_w2", li, 0)), ("one", ("mlp_w1", li, 1))]
    if li % 2 == 0:
        return [[("one", ("s5_w_in", j, 0)), ("one", ("s5_w_glu", j, 1))], mlp]
    packed = [("m2_w_in", j, 1), ("m2_conv_w", j, 1), ("m2_conv_b", j, 0), ("m2_norm_g", j, 0)]
    return [[("one", ("m2_w_out", j, 0)), ("packed", packed)], mlp]


def _grad_buffers(li, grads, parts=(0, 1)):
    out = []
    for k in parts:
        bufs = []
        for kind, entry in _grad_specs(li)[k]:
            if kind == "one":
                bufs.append((grads[entry[0]], "rows" if entry[2] == 0 else "cols"))
            else:
                pieces = [_pack([_chip_slice(grads[n], c, ax) for n, _, ax in entry], BF16) for c in range(N_CHIP)]
                bufs.append((jnp.stack(pieces), "packed"))
        out.append(bufs)
    return out


def _gather_buffers(local, part):
    return [(local[n][i].astype(BF16), "same") for n, i, _ in part]


def _assemble(landed, part):
    out = {}
    for buf, (n, i, ax) in zip(landed, part):
        if ax == 0:
            out[n] = {i: buf.reshape(-1, buf.shape[2])}
        else:
            out[n] = {i: jnp.concatenate([buf[k] for k in range(N_CHIP)], axis=1)}
    return out


def _local_step(x, target, mods, w, local):
    depth = w["norm_mix_g"].shape[0]
    d = x.shape[1]
    rides = _Rides()
    saved, mod_rows, layer_w = [], [], []
    mix0, mlp0 = _layer_parts(0)
    wl = {**w, **_assemble(_exchange4("ag_w_L0", _gather_buffers(local, mix0)), mix0)}
    rides.pending["s5_win"] = _gather_buffers(local, mlp0[:1])
    rides.pending["s5_wglu"] = _gather_buffers(local, mlp0[1:])

    def late_mlp0(landed):
        return _assemble(landed.pop("s5_win") + landed.pop("s5_wglu"), mlp0)

    for li in range(depth):
        if li + 1 < depth:
            nxt = _layer_parts(li + 1)
            rides.pending["w1"] = _gather_buffers(local, nxt[0])
            rides.pending["w2"] = _gather_buffers(local, nxt[1])
        mod = [mods[li:li + 1, i * d:(i + 1) * d] for i in range(N_MOD)]
        mod_rows.append(mod)
        x, sv, wl = _layer_fwd(li, x, mod, wl, rides, late_mlp0 if li == 0 else None)
        saved.append(sv)
        layer_w.append(wl)
        if li + 1 < depth:
            wl = {**w, **_assemble(rides.landed.pop("w1"), nxt[0]), **_assemble(rides.landed.pop("w2"), nxt[1])}
    dx, dx_b, dgf, loss = _loss_head("loss_head", x, target, _row(w["final_norm_g"]))
    layer_grads = [None] * depth
    dmods = [None] * depth
    landed = {}
    for li in reversed(range(depth)):
        dx, dx_b, layer_grads[li], dmods[li] = _layer_bwd(li, dx, dx_b, saved[li], mod_rows[li], layer_w[li], rides,
                                                          li == 0)
        if li + 1 < depth:
            landed[(li + 1, 0)] = rides.landed.pop("dr")
            landed[(li + 1, 1)] = rides.landed.pop("dh2")
        if li > 0:
            rides.pending["dr"], rides.pending["dh2"] = _grad_buffers(li, layer_grads[li])
    landed[(0, 1)] = rides.landed.pop("s5_dgl") + rides.landed.pop("s5_dwglu")
    landed[(0, 0)] = _exchange4("rs_g_L0", _grad_buffers(0, layer_grads[0], parts=[0])[0])
    return loss, dx, layer_grads, dgf.reshape(-1), jnp.concatenate(dmods, axis=0), landed


ANY = pl.BlockSpec(memory_space=pl.ANY)
N_DEV = 8
N_CHIP = 4


def _coords():
    return lax.axis_index("x"), lax.axis_index("y"), lax.axis_index("c")


def _allgather8(name, block):
    r, wd = block.shape

    def body(x_ref, out_ref, send_sems, recv_sems, local_sem):
        x, y, c = _coords()
        me, sibling = (x, y, c), (x, y, 1 - c)
        chips = [(1 - x, y), (x, 1 - y), (1 - x, 1 - y)]

        def slot(px, py, pc):
            return out_ref.at[4 * px + 2 * py + pc]

        def copy(k, blk, to, src=None):
            return pltpu.make_async_remote_copy(
                src_ref=slot(*blk) if src is None else src, dst_ref=slot(*blk),
                send_sem=send_sems.at[k], recv_sem=recv_sems.at[k], device_id=to, device_id_type=MESH)

        mine = pltpu.make_async_copy(x_ref, slot(*me), local_sem)
        mine.start()
        first = [copy(0, me, sibling, src=x_ref)]
        first += [copy(1 + j, me, (*chip, c), src=x_ref) for j, chip in enumerate(chips)]
        for cp in first:
            cp.start()
        passed = [copy(4 + j, (*chip, c), sibling) for j, chip in enumerate(chips)]
        for j, chip in enumerate(chips):
            copy(1 + j, (*chip, c), me).wait_recv()
            passed[j].start()
        copy(0, sibling, me).wait_recv()
        for j, chip in enumerate(chips):
            copy(4 + j, (*chip, 1 - c), me).wait_recv()
        for cp in first + passed:
            cp.wait_send()
        mine.wait()

    return pl.pallas_call(
        body, name=name, in_specs=[ANY], out_specs=ANY,
        out_shape=jax.ShapeDtypeStruct((N_DEV, r, wd), block.dtype),
        scratch_shapes=[pltpu.SemaphoreType.DMA((7,)), pltpu.SemaphoreType.DMA((7,)), pltpu.SemaphoreType.DMA],
    )(block)


def _landing_shape(src, kind):
    if kind == "same":
        return (N_CHIP,) + src.shape
    if kind == "packed":
        return src.shape
    rows, cols = src.shape
    return (N_CHIP, rows // N_CHIP, cols) if kind == "rows" else (N_CHIP, rows, cols // N_CHIP)


def _exchange4_ops(srcs, dsts, send_sems, recv_sems, local_sems, kinds):
    x, y, c = _coords()
    my_chip = 2 * x + y
    chips = [(1 - x, y), (x, 1 - y), (1 - x, 1 - y)]

    def piece(q, k):
        ref, kind = srcs[q], kinds[q]
        if kind == "same":
            return ref
        if kind == "packed":
            return ref.at[k]
        _, rows, cols = dsts[q].shape
        return ref.at[pl.ds(k * rows, rows), :] if kind == "rows" else ref.at[:, pl.ds(k * cols, cols)]

    def copy(q, j, k, slot):
        px, py = chips[j]
        return pltpu.make_async_remote_copy(
            src_ref=piece(q, k), dst_ref=dsts[q].at[slot], send_sem=send_sems.at[3 * q + j],
            recv_sem=recv_sems.at[3 * q + j], device_id=(px, py, c), device_id_type=MESH)

    def mine(q):
        return pltpu.make_async_copy(piece(q, my_chip), dsts[q].at[my_chip], local_sems.at[q])

    def start():
        for q in range(len(srcs)):
            mine(q).start()
            for j, (px, py) in enumerate(chips):
                copy(q, j, 2 * px + py, my_chip).start()

    def wait():
        for q in range(len(srcs)):
            for j, (px, py) in enumerate(chips):
                copy(q, j, my_chip, 2 * px + py).wait_recv()
        for q in range(len(srcs)):
            for j, (px, py) in enumerate(chips):
                copy(q, j, 2 * px + py, my_chip).wait_send()
            mine(q).wait()

    return start, wait


def _exchange_scratch(n):
    return [pltpu.SemaphoreType.DMA((3 * n,)), pltpu.SemaphoreType.DMA((3 * n,)), pltpu.SemaphoreType.DMA((n,))]


def _exchange4(name, buffers):
    n = len(buffers)
    kinds = [kind for _, kind in buffers]

    def body(*refs):
        start, wait = _exchange4_ops(refs[:n], refs[n:2 * n], *refs[2 * n:], kinds)
        start()
        wait()

    return pl.pallas_call(
        body, name=name, in_specs=[ANY] * n, out_specs=[ANY] * n,
        out_shape=[jax.ShapeDtypeStruct(_landing_shape(s, k), s.dtype) for s, k in buffers],
        scratch_shapes=_exchange_scratch(n),
    )(*[s for s, _ in buffers])


def _swap_sibling(name, block):
    def body(x_ref, out_ref, send_sem, recv_sem):
        x, y, c = _coords()
        cp = pltpu.make_async_remote_copy(src_ref=x_ref, dst_ref=out_ref, send_sem=send_sem, recv_sem=recv_sem,
                                          device_id=(x, y, 1 - c), device_id_type=MESH)
        cp.start()
        cp.wait()

    return pl.pallas_call(
        body, name=name, in_specs=[ANY], out_specs=ANY, out_shape=jax.ShapeDtypeStruct(block.shape, block.dtype),
        scratch_shapes=[pltpu.SemaphoreType.DMA, pltpu.SemaphoreType.DMA],
    )(block)


def _sum_slots(name, stacked):
    n, r, wd = stacked.shape
    tile = min(FLAT_ROWS, r)

    def kern(x_ref, o_ref):
        acc = x_ref[0].astype(F32)
        for s in range(1, n):
            acc = acc + x_ref[s].astype(F32)
        o_ref[...] = acc

    return pl.pallas_call(
        kern, name=name, grid=(r // tile,), in_specs=[pl.BlockSpec((n, tile, wd), lambda i: (0, i, 0))],
        out_specs=pl.BlockSpec((tile, wd), lambda i: (i, 0)), out_shape=jax.ShapeDtypeStruct((r, wd), F32),
        compiler_params=_params("parallel"),
    )(stacked)


def _adamw(name, w, m, v, g, g2=None):
    r, wd = w.shape
    grads = [g] if g2 is None else [g, g2]
    c1 = 1.0 - ADAM_B1 ** ADAM_STEP
    c2 = 1.0 - ADAM_B2 ** ADAM_STEP

    def body(i, *refs):
        w_ref, m_ref, v_ref = refs[:3]
        g_refs = refs[3:3 + len(grads)]
        go_ref, d_ref, mo_ref, vo_ref = refs[3 + len(grads):]
        gv = g_refs[0][...]
        if g2 is not None:
            gv = gv + g_refs[1][...]
        mn = ADAM_B1 * m_ref[...] + (1.0 - ADAM_B1) * gv
        vn = ADAM_B2 * v_ref[...] + (1.0 - ADAM_B2) * (gv * gv)
        go_ref[...] = gv
        mo_ref[...] = mn
        vo_ref[...] = vn
        d_ref[...] = -ADAM_LR * ((mn / c1) / (jnp.sqrt(vn / c2) + ADAM_EPS) + ADAM_WD * w_ref[...])

    return _rowcall(name, body, r, FLAT_ROWS, [w, m, v] + grads, [], [(wd, F32)] * 4, [])


FLAT_BLOCK = FLAT_ROWS * FLAT_W


def _pack(arrays, dtype):
    flat = jnp.concatenate([a.reshape(-1).astype(dtype) for a in arrays])
    pad = (-flat.shape[0]) % FLAT_BLOCK
    return jnp.pad(flat, (0, pad)).reshape(-1, FLAT_W)


def _unpack(buf, shapes):
    flat = buf.reshape(-1)
    out, off = [], 0
    for s in shapes:
        n = math.prod(s)
        out.append(flat[off:off + n].reshape(s))
        off += n
    return out


SHARDED_BIG = {"mlp_w1": 2, "mlp_w2": 1, "s5_w_in": 1, "s5_w_glu": 2, "m2_w_in": 2, "m2_w_out": 1}
SHARDED_SMALL = {"m2_conv_w": 2, "m2_conv_b": 1, "m2_norm_g": 1}
REPLICATED = ("ada_b", "norm_mix_g", "norm_mlp_g", "s5_lambda_re", "s5_lambda_im", "s5_log_dt", "s5_b_re",
              "s5_b_im", "s5_c_re", "s5_c_im", "s5_d", "s5_b_glu", "m2_dt_bias", "m2_a_log", "m2_d", "final_norm_g")
WEIGHT_NAMES = ("ada_w", "ada_b", "norm_mix_g", "norm_mlp_g", "mlp_w1", "mlp_w2", "s5_w_in", "s5_lambda_re",
                "s5_lambda_im", "s5_log_dt", "s5_b_re", "s5_b_im", "s5_c_re", "s5_c_im", "s5_d", "s5_w_glu",
                "s5_b_glu", "m2_w_in", "m2_conv_w", "m2_conv_b", "m2_dt_bias", "m2_a_log", "m2_d", "m2_norm_g",
                "m2_w_out", "final_norm_g")


def _gather_weights(name, local, names_axes, dtype):
    names = list(names_axes)
    got = _exchange4(name, [(_pack([local[k] for k in names], dtype), "same")])[0]
    per_chip = [_unpack(got[j], [local[k].shape for k in names]) for j in range(N_CHIP)]
    return {k: jnp.concatenate([per_chip[j][i] for j in range(N_CHIP)], axis=names_axes[k])
            for i, k in enumerate(names)}


def _chip_slice(a, chip, axis):
    size = a.shape[axis] // N_CHIP
    return lax.slice_in_dim(a, chip * size, (chip + 1) * size, axis=axis)


def kernel(x, c, ada_w, ada_b, norm_mix_g, norm_mlp_g, mlp_w1, mlp_w2, s5_w_in, s5_lambda_re, s5_lambda_im, s5_log_dt, s5_b_re, s5_b_im, s5_c_re, s5_c_im, s5_d, s5_w_glu, s5_b_glu, m2_w_in, m2_conv_w, m2_conv_b, m2_dt_bias, m2_a_log, m2_d, m2_norm_g, m2_w_out, final_norm_g, loss_target, m_ada_w, m_ada_b, m_norm_mix_g, m_norm_mlp_g, m_mlp_w1, m_mlp_w2, m_s5_w_in, m_s5_lambda_re, m_s5_lambda_im, m_s5_log_dt, m_s5_b_re, m_s5_b_im, m_s5_c_re, m_s5_c_im, m_s5_d, m_s5_w_glu, m_s5_b_glu, m_m2_w_in, m_m2_conv_w, m_m2_conv_b, m_m2_dt_bias, m_m2_a_log, m_m2_d, m_m2_norm_g, m_m2_w_out, m_final_norm_g, v_ada_w, v_ada_b, v_norm_mix_g, v_norm_mlp_g, v_mlp_w1, v_mlp_w2, v_s5_w_in, v_s5_lambda_re, v_s5_lambda_im, v_s5_log_dt, v_s5_b_re, v_s5_b_im, v_s5_c_re, v_s5_c_im, v_s5_d, v_s5_w_glu, v_s5_b_glu, v_m2_w_in, v_m2_conv_w, v_m2_conv_b, v_m2_dt_bias, v_m2_a_log, v_m2_d, v_m2_norm_g, v_m2_w_out, v_final_norm_g):
    args = locals()
    local = {k: args[k] for k in WEIGHT_NAMES}
    mom_m = {k: args["m_" + k] for k in WEIGHT_NAMES}
    mom_v = {k: args["v_" + k] for k in WEIGHT_NAMES}
    depth, d = norm_mix_g.shape
    xi, yi, ci = _coords()
    my_chip = 2 * xi + yi
    my_dev = 2 * my_chip + ci

    cond = jax.nn.silu(c).reshape(-1, LANES)
    cond_all = _allgather8("ag_cond", cond).reshape(N_DEV, d)
    cond_pad = jnp.zeros((LANES, d), F32).at[:N_DEV].set(cond_all)
    mod_cols = ada_w.shape[2]
    mod_part = jnp.stack([_matmul("ada_%d" % i, cond_pad, ada_w[i])[:N_DEV] for i in range(depth)])
    mod_all = _allgather8("ag_mod", mod_part.reshape(-1, LANES)).reshape(N_CHIP, 2, depth, N_DEV, mod_cols)[:, 0]
    mod_mine = lax.dynamic_index_in_dim(mod_all, my_dev, axis=2, keepdims=False)
    mods = jnp.transpose(mod_mine, (1, 0, 2)).reshape(depth, N_CHIP * mod_cols) + ada_b

    w = {k: local[k] for k in REPLICATED}
    w.update(_gather_weights("ag_w_small", local, SHARDED_SMALL, F32))

    loss_row, dx, layer_grads, g_final, dmods, landed = _local_step(x[0], loss_target[0], mods, w, local)
    grads = {"ada_b": dmods, "final_norm_g": g_final}
    for k in REPLICATED[1:-1]:
        grads[k] = jnp.stack([g[k] for g in layer_grads if k in g])

    rep_shapes = [grads[k].shape for k in REPLICATED]
    rep_all = _allgather8("ag_grep", _pack([grads[k] for k in REPLICATED], F32))
    rep_sum = _sum_slots("sum_grep", rep_all)
    dmods_all = rep_all.reshape(N_DEV, -1)[:, :dmods.size].reshape(N_DEV, depth, N_CHIP * mod_cols)

    dm_mine = lax.dynamic_slice_in_dim(dmods_all, my_chip * mod_cols, mod_cols, axis=2)
    dm_pad = jnp.zeros((LANES, depth, mod_cols), F32).at[:N_DEV].set(dm_mine)
    g_ada_w = jnp.stack([_matmul("dada_%d" % i, cond_pad, dm_pad[:, i], "tn") for i in range(depth)])

    red = {}
    for li in range(depth):
        for k, part in enumerate(_grad_specs(li)):
            for (kind, entry), land in zip(part, landed[(li, k)]):
                if kind == "one":
                    red[entry[:2]] = _sum_slots("sum_%s_%d" % entry[:2], land)
                else:
                    total = _sum_slots("sum_packed_L%d" % li, land)
                    shapes = [local[n][i].shape for n, i, _ in entry]
                    red.update({(n, i): v for (n, i, _), v in zip(entry, _unpack(total, shapes))})

    out_g, out_d, out_m, out_v = {}, {}, {}, {}
    for name in list(SHARDED_BIG) + list(SHARDED_SMALL):
        shape = local[name].shape
        flat2 = (-1, shape[-1])
        g1 = jnp.stack([red[(name, i)] for i in range(shape[0])]).reshape(flat2)
        g2 = _swap_sibling("swap_" + name, g1)
        res = _adamw("adam_" + name, local[name].reshape(flat2), mom_m[name].reshape(flat2),
                     mom_v[name].reshape(flat2), g1, g2)
        for dst, buf in zip((out_g, out_d, out_m, out_v), res):
            dst[name] = buf.reshape(shape)
    res = _adamw("adam_rep", _pack([local[k] for k in REPLICATED], F32), _pack([mom_m[k] for k in REPLICATED], F32),
                 _pack([mom_v[k] for k in REPLICATED], F32), rep_sum)
    for dst, buf in zip((out_g, out_d, out_m, out_v), res):
        dst.update(zip(REPLICATED, _unpack(buf, rep_shapes)))
    flat2 = (-1, mod_cols)
    res = _adamw("adam_ada", ada_w.reshape(flat2), m_ada_w.reshape(flat2), v_ada_w.reshape(flat2),
                 g_ada_w.reshape(flat2))
    for dst, buf in zip((out_g, out_d, out_m, out_v), res):
        dst["ada_w"] = buf.reshape(ada_w.shape)

    loss = lax.psum(loss_row[0, 0], ("x", "y", "c"))
    outs = [loss, dx[None]]
    for dst in (out_g, out_d, out_m, out_v):
        outs += [dst[k] for k in WEIGHT_NAMES]
    return tuple(outs)
```
